```python
import math
import jax, jax.numpy as jnp
from jax import lax
import numpy as np

D_MODEL = 2048
BATCH = 8
SEQ = 8192
DEPTH = 2

CHUNK = 64
Q_BLOCK = 128
HEAD_DIM = 128
N_BRANCH = 4
BRANCH_WIDTH = D_MODEL // N_BRANCH
N_HEADS = BRANCH_WIDTH // HEAD_DIM
LRU_WIDTH = BRANCH_WIDTH
LRU_BLOCKS = N_HEADS
LRU_BLOCK = LRU_WIDTH // LRU_BLOCKS
CONV_WIDTH = 4
LRU_C = 8.0
LOOKBACK_CHUNKS = 8
BAND_CHUNKS = LOOKBACK_CHUNKS + 1
REL_CLIP = 256
REL_TABLE = REL_CLIP + CHUNK
D_FF = 4 * D_MODEL
ALPHA = (2.0 * DEPTH) ** 0.25
BETA = (8.0 * DEPTH) ** -0.25
LN_EPS = 1e-5

IN_SIZES = (
    BRANCH_WIDTH, BRANCH_WIDTH, BRANCH_WIDTH, N_HEADS,
    LRU_WIDTH, LRU_WIDTH,
    BRANCH_WIDTH, BRANCH_WIDTH, BRANCH_WIDTH,
    BRANCH_WIDTH, BRANCH_WIDTH, BRANCH_WIDTH,
)
D_IN = sum(IN_SIZES)

kernel_name = "chunk_causal_hybrid_fox_rglru_stickbreak_chunkattn"

F32 = jnp.float32


def layer_norm(x, g, b):
    xf = x.astype(F32)
    mu = jnp.mean(xf, axis=-1, keepdims=True)
    var = jnp.mean(jnp.square(xf - mu), axis=-1, keepdims=True)
    y = (xf - mu) * lax.rsqrt(var + LN_EPS) * g.astype(F32) + b.astype(F32)
    return y.astype(x.dtype)


def split_cols(u, sizes):
    outs, off = [], 0
    for n in sizes:
        outs.append(u[..., off:off + n])
        off += n
    return outs


def heads(t):
    b, s, _ = t.shape
    return t.reshape(b, s, N_HEADS, HEAD_DIM)


def fox_attention(q, k, v, f_logit):
    B, S, H, Dh = q.shape
    nb = S // Q_BLOCK
    cum_f = jnp.cumsum(jax.nn.log_sigmoid(f_logit.astype(F32)), axis=1)
    cum_f_k = cum_f.transpose(0, 2, 1)
    qb = q.reshape(B, nb, Q_BLOCK, H, Dh).transpose(1, 0, 2, 3, 4)
    fb = cum_f.reshape(B, nb, Q_BLOCK, H).transpose(1, 0, 3, 2)
    kpos = jnp.arange(S)
    scale = Dh ** -0.5

    def block(args):
        i, qi, fi = args
        s = jnp.einsum('bqhd,bkhd->bhqk', qi, k).astype(F32) * scale
        s = s + fi[..., :, None] - cum_f_k[:, :, None, :]
        qpos = i * Q_BLOCK + jnp.arange(Q_BLOCK)
        s = jnp.where(kpos[None, :] <= qpos[:, None], s, -jnp.inf)
        p = jax.nn.softmax(s, axis=-1)
        return jnp.einsum('bhqk,bkhd->bqhd', p.astype(v.dtype), v)

    out = lax.map(block, (jnp.arange(nb), qb, fb))
    return out.transpose(1, 0, 2, 3, 4).reshape(B, S, H * Dh)


def stick_breaking_attention(q, k, v):
    B, S, H, Dh = q.shape
    nb = S // Q_BLOCK
    qb = q.reshape(B, nb, Q_BLOCK, H, Dh).transpose(1, 0, 2, 3, 4)
    kpos = jnp.arange(S)
    scale = Dh ** -0.5

    def block(args):
        i, qi = args
        z = jnp.einsum('bqhd,bkhd->bhqk', qi, k).astype(F32) * scale
        qpos = i * Q_BLOCK + jnp.arange(Q_BLOCK)
        mask = kpos[None, :] < qpos[:, None]
        log_1m_beta = jnp.where(mask, jax.nn.log_sigmoid(-z), 0.0)
        later = lax.cumsum(log_1m_beta, axis=3, reverse=True) - log_1m_beta
        a = jnp.where(mask, jnp.exp(jax.nn.log_sigmoid(z) + later), 0.0)
        return jnp.einsum('bhqk,bkhd->bqhd', a.astype(v.dtype), v)

    out = lax.map(block, (jnp.arange(nb), qb))
    return out.transpose(1, 0, 2, 3, 4).reshape(B, S, H * Dh)


def chunk_band_attention(q, k, v, rel_bias):
    B, S, H, Dh = q.shape
    nc = S // CHUNK
    qc = q.reshape(B, nc, CHUNK, H, Dh)
    pad = ((0, 0), (LOOKBACK_CHUNKS, 0), (0, 0), (0, 0), (0, 0))
    kc = jnp.pad(k.reshape(B, nc, CHUNK, H, Dh), pad)
    vc = jnp.pad(v.reshape(B, nc, CHUNK, H, Dh), pad)
    kband = jnp.concatenate([kc[:, j:j + nc] for j in range(BAND_CHUNKS)], axis=2)
    vband = jnp.concatenate([vc[:, j:j + nc] for j in range(BAND_CHUNKS)], axis=2)
    kidx = jnp.arange(BAND_CHUNKS * CHUNK)
    dist = LOOKBACK_CHUNKS * CHUNK + jnp.arange(CHUNK)[:, None] - kidx[None, :]
    ridx = jnp.clip(dist, -(CHUNK - 1), REL_CLIP) + (CHUNK - 1)
    bias = rel_bias.astype(F32)[:, ridx]
    chunk_of_slot = jnp.arange(nc)[:, None] - LOOKBACK_CHUNKS + jnp.arange(BAND_CHUNKS)[None, :]
    valid = jnp.repeat(chunk_of_slot >= 0, CHUNK, axis=1)
    s = jnp.einsum('bcqhd,bckhd->bchqk', qc, kband).astype(F32) * (Dh ** -0.5)
    s = s + bias[None, None]
    s = jnp.where(valid[None, :, None, None, :], s, -jnp.inf)
    p = jax.nn.softmax(s, axis=-1)
    out = jnp.einsum('bchqk,bckhd->bcqhd', p.astype(v.dtype), vband)
    return out.reshape(B, S, H * Dh)


def recurrent_branch(xr, yr, conv_w, conv_b, w_r, b_r, w_i, b_i, lam):
    B, S, W = xr.shape
    xp = jnp.pad(xr, ((0, 0), (CONV_WIDTH - 1, 0), (0, 0)))
    xc = conv_b
    for j in range(CONV_WIDTH):
        xc = xc + xp[:, j:j + S] * conv_w[j]
    xg = xc.reshape(B, S, LRU_BLOCKS, LRU_BLOCK)
    r = jax.nn.sigmoid(jnp.einsum('bsnc,ncd->bsnd', xg, w_r).reshape(B, S, W) + b_r)
    gi = jax.nn.sigmoid(jnp.einsum('bsnc,ncd->bsnd', xg, w_i).reshape(B, S, W) + b_i)
    log_a = LRU_C * r.astype(F32) * jax.nn.log_sigmoid(lam.astype(F32))
    a = jnp.exp(log_a)
    inp = jnp.sqrt(-jnp.expm1(2.0 * log_a)) * (gi * xc).astype(F32)

    def combine(left, right):
        a1, b1 = left
        a2, b2 = right
        return a1 * a2, a2 * b1 + b2

    _, h = lax.associative_scan(combine, (a, inp), axis=1)
    return h.astype(xr.dtype) * jax.nn.gelu(yr)


def _fwd_setup_inputs(seed: int = 0) -> dict:
    key = jax.random.key(seed)
    ks = jax.random.split(key, 24)
    L, D, Wb = DEPTH, D_MODEL, BRANCH_WIDTH

    def nrm(k, shape, scale):
        return jax.random.normal(k, shape, F32) * scale

    u = jax.random.uniform(ks[9], (L, LRU_WIDTH), F32, 0.9, 0.999)
    a0 = u ** (1.0 / LRU_C)
    lru_lambda = jnp.log(a0) - jnp.log1p(-a0)
    return {
        "x": nrm(ks[0], (BATCH, SEQ, D), 1.0),
        "ln_in_g": 1.0 + nrm(ks[1], (D,), 0.02),
        "ln_in_b": nrm(ks[2], (D,), 0.02),
        "w_in": nrm(ks[3], (L, D, D_IN), D ** -0.5),
        "b_forget": 3.0 + nrm(ks[4], (L, N_HEADS), 0.5),
        "conv_w": nrm(ks[5], (L, CONV_WIDTH, LRU_WIDTH), CONV_WIDTH ** -0.5),
        "conv_b": nrm(ks[6], (L, LRU_WIDTH), 0.02),
        "w_r": nrm(ks[7], (L, LRU_BLOCKS, LRU_BLOCK, LRU_BLOCK), LRU_BLOCK ** -0.5),
        "b_r": nrm(ks[8], (L, LRU_WIDTH), 0.02),
        "w_i": nrm(ks[10], (L, LRU_BLOCKS, LRU_BLOCK, LRU_BLOCK), LRU_BLOCK ** -0.5),
        "b_i": nrm(ks[11], (L, LRU_WIDTH), 0.02),
        "lru_lambda": lru_lambda,
        "rel_bias": nrm(ks[12], (L, N_HEADS, REL_TABLE), 0.1),
        "w_branch": nrm(ks[13], (L, N_BRANCH, Wb, D), Wb ** -0.5),
        "w_gate": nrm(ks[14], (L, N_BRANCH, D, D), D ** -0.5),
        "b_gate": nrm(ks[15], (L, N_BRANCH, D), 0.02),
        "w_out": nrm(ks[16], (L, D, D), BETA * D ** -0.5),
        "ln1_g": 1.0 + nrm(ks[17], (L, D), 0.02),
        "ln1_b": nrm(ks[18], (L, D), 0.02),
        "w_ff1": nrm(ks[19], (L, D, D_FF), D ** -0.5),
        "w_ff2": nrm(ks[20], (L, D_FF, D), BETA * D_FF ** -0.5),
        "ln2_g": 1.0 + nrm(ks[21], (L, D), 0.02),
        "ln2_b": nrm(ks[22], (L, D), 0.02),
    }


def _fwd_reference(x, ln_in_g, ln_in_b, w_in, b_forget, conv_w, conv_b, w_r, b_r, w_i, b_i,
              lru_lambda, rel_bias, w_branch, w_gate, b_gate, w_out, ln1_g, ln1_b,
              w_ff1, w_ff2, ln2_g, ln2_b):
    x = layer_norm(x, ln_in_g, ln_in_b)
    for l in range(DEPTH):
        u = x @ w_in[l]
        (fq, fk, fv, ff, rx, ry, sq, sk, sv, cq, ck, cv) = split_cols(u, IN_SIZES)
        o_fox = fox_attention(heads(fq), heads(fk), heads(fv), ff + b_forget[l])
        o_lru = recurrent_branch(rx, ry, conv_w[l], conv_b[l], w_r[l], b_r[l],
                                 w_i[l], b_i[l], lru_lambda[l])
        o_sb = stick_breaking_attention(heads(sq), heads(sk), heads(sv))
        o_ch = chunk_band_attention(heads(cq), heads(ck), heads(cv), rel_bias[l])
        merged = None
        for g, o in enumerate((o_fox, o_lru, o_sb, o_ch)):
            gate = jax.nn.sigmoid(x @ w_gate[l, g] + b_gate[l, g])
            term = gate * (o @ w_branch[l, g])
            merged = term if merged is None else merged + term
        x = layer_norm(ALPHA * x + merged @ w_out[l], ln1_g[l], ln1_b[l])
        hid = jnp.square(jax.nn.relu(x @ w_ff1[l]))
        x = layer_norm(ALPHA * x + hid @ w_ff2[l], ln2_g[l], ln2_b[l])
    return x


import jax as _jax
import jax.numpy as _jnp

TWIN_FORMAT = 'train_step'
FWD_PARAMS = ['x', 'ln_in_g', 'ln_in_b', 'w_in', 'b_forget', 'conv_w', 'conv_b', 'w_r', 'b_r', 'w_i', 'b_i', 'lru_lambda', 'rel_bias', 'w_branch', 'w_gate', 'b_gate', 'w_out', 'ln1_g', 'ln1_b', 'w_ff1', 'w_ff2', 'ln2_g', 'ln2_b']
TWIN_WEIGHTS = ['ln_in_g', 'ln_in_b', 'w_in', 'b_forget', 'conv_w', 'conv_b', 'w_r', 'b_r', 'w_i', 'b_i', 'lru_lambda', 'rel_bias', 'w_branch', 'w_gate', 'b_gate', 'w_out', 'ln1_g', 'ln1_b', 'w_ff1', 'w_ff2', 'ln2_g', 'ln2_b']
TWIN_DIFF_INPUT = 'x'
TWIN_INPUTS = ['x', 'ln_in_g', 'ln_in_b', 'w_in', 'b_forget', 'conv_w', 'conv_b', 'w_r', 'b_r', 'w_i', 'b_i', 'lru_lambda', 'rel_bias', 'w_branch', 'w_gate', 'b_gate', 'w_out', 'ln1_g', 'ln1_b', 'w_ff1', 'w_ff2', 'ln2_g', 'ln2_b', 'loss_target', 'm_ln_in_g', 'm_ln_in_b', 'm_w_in', 'm_b_forget', 'm_conv_w', 'm_conv_b', 'm_w_r', 'm_b_r', 'm_w_i', 'm_b_i', 'm_lru_lambda', 'm_rel_bias', 'm_w_branch', 'm_w_gate', 'm_b_gate', 'm_w_out', 'm_ln1_g', 'm_ln1_b', 'm_w_ff1', 'm_w_ff2', 'm_ln2_g', 'm_ln2_b', 'v_ln_in_g', 'v_ln_in_b', 'v_w_in', 'v_b_forget', 'v_conv_w', 'v_conv_b', 'v_w_r', 'v_b_r', 'v_w_i', 'v_b_i', 'v_lru_lambda', 'v_rel_bias', 'v_w_branch', 'v_w_gate', 'v_b_gate', 'v_w_out', 'v_ln1_g', 'v_ln1_b', 'v_w_ff1', 'v_w_ff2', 'v_ln2_g', 'v_ln2_b']
TWIN_OUTPUTS = ['loss', 'grad_x', 'grad_ln_in_g', 'grad_ln_in_b', 'grad_w_in', 'grad_b_forget', 'grad_conv_w', 'grad_conv_b', 'grad_w_r', 'grad_b_r', 'grad_w_i', 'grad_b_i', 'grad_lru_lambda', 'grad_rel_bias', 'grad_w_branch', 'grad_w_gate', 'grad_b_gate', 'grad_w_out', 'grad_ln1_g', 'grad_ln1_b', 'grad_w_ff1', 'grad_w_ff2', 'grad_ln2_g', 'grad_ln2_b', 'delta_ln_in_g', 'delta_ln_in_b', 'delta_w_in', 'delta_b_forget', 'delta_conv_w', 'delta_conv_b', 'delta_w_r', 'delta_b_r', 'delta_w_i', 'delta_b_i', 'delta_lru_lambda', 'delta_rel_bias', 'delta_w_branch', 'delta_w_gate', 'delta_b_gate', 'delta_w_out', 'delta_ln1_g', 'delta_ln1_b', 'delta_w_ff1', 'delta_w_ff2', 'delta_ln2_g', 'delta_ln2_b', 'new_m_ln_in_g', 'new_m_ln_in_b', 'new_m_w_in', 'new_m_b_forget', 'new_m_conv_w', 'new_m_conv_b', 'new_m_w_r', 'new_m_b_r', 'new_m_w_i', 'new_m_b_i', 'new_m_lru_lambda', 'new_m_rel_bias', 'new_m_w_branch', 'new_m_w_gate', 'new_m_b_gate', 'new_m_w_out', 'new_m_ln1_g', 'new_m_ln1_b', 'new_m_w_ff1', 'new_m_w_ff2', 'new_m_ln2_g', 'new_m_ln2_b', 'new_v_ln_in_g', 'new_v_ln_in_b', 'new_v_w_in', 'new_v_b_forget', 'new_v_conv_w', 'new_v_conv_b', 'new_v_w_r', 'new_v_b_r', 'new_v_w_i', 'new_v_b_i', 'new_v_lru_lambda', 'new_v_rel_bias', 'new_v_w_branch', 'new_v_w_gate', 'new_v_b_gate', 'new_v_w_out', 'new_v_ln1_g', 'new_v_ln1_b', 'new_v_w_ff1', 'new_v_w_ff2', 'new_v_ln2_g', 'new_v_ln2_b']
TWIN_LEAF_KINDS = {'loss': 'loss', 'grad_x': 'grad_x', 'grad_ln_in_g': 'grad_w', 'grad_ln_in_b': 'grad_w', 'grad_w_in': 'grad_w', 'grad_b_forget': 'grad_w', 'grad_conv_w': 'grad_w', 'grad_conv_b': 'grad_w', 'grad_w_r': 'grad_w', 'grad_b_r': 'grad_w', 'grad_w_i': 'grad_w', 'grad_b_i': 'grad_w', 'grad_lru_lambda': 'grad_w', 'grad_rel_bias': 'grad_w', 'grad_w_branch': 'grad_w', 'grad_w_gate': 'grad_w', 'grad_b_gate': 'grad_w', 'grad_w_out': 'grad_w', 'grad_ln1_g': 'grad_w', 'grad_ln1_b': 'grad_w', 'grad_w_ff1': 'grad_w', 'grad_w_ff2': 'grad_w', 'grad_ln2_g': 'grad_w', 'grad_ln2_b': 'grad_w', 'delta_ln_in_g': 'delta_w', 'delta_ln_in_b': 'delta_w', 'delta_w_in': 'delta_w', 'delta_b_forget': 'delta_w', 'delta_conv_w': 'delta_w', 'delta_conv_b': 'delta_w', 'delta_w_r': 'delta_w', 'delta_b_r': 'delta_w', 'delta_w_i': 'delta_w', 'delta_b_i': 'delta_w', 'delta_lru_lambda': 'delta_w', 'delta_rel_bias': 'delta_w', 'delta_w_branch': 'delta_w', 'delta_w_gate': 'delta_w', 'delta_b_gate': 'delta_w', 'delta_w_out': 'delta_w', 'delta_ln1_g': 'delta_w', 'delta_ln1_b': 'delta_w', 'delta_w_ff1': 'delta_w', 'delta_w_ff2': 'delta_w', 'delta_ln2_g': 'delta_w', 'delta_ln2_b': 'delta_w', 'new_m_ln_in_g': 'new_m', 'new_m_ln_in_b': 'new_m', 'new_m_w_in': 'new_m', 'new_m_b_forget': 'new_m', 'new_m_conv_w': 'new_m', 'new_m_conv_b': 'new_m', 'new_m_w_r': 'new_m', 'new_m_b_r': 'new_m', 'new_m_w_i': 'new_m', 'new_m_b_i': 'new_m', 'new_m_lru_lambda': 'new_m', 'new_m_rel_bias': 'new_m', 'new_m_w_branch': 'new_m', 'new_m_w_gate': 'new_m', 'new_m_b_gate': 'new_m', 'new_m_w_out': 'new_m', 'new_m_ln1_g': 'new_m', 'new_m_ln1_b': 'new_m', 'new_m_w_ff1': 'new_m', 'new_m_w_ff2': 'new_m', 'new_m_ln2_g': 'new_m', 'new_m_ln2_b': 'new_m', 'new_v_ln_in_g': 'new_v', 'new_v_ln_in_b': 'new_v', 'new_v_w_in': 'new_v', 'new_v_b_forget': 'new_v', 'new_v_conv_w': 'new_v', 'new_v_conv_b': 'new_v', 'new_v_w_r': 'new_v', 'new_v_b_r': 'new_v', 'new_v_w_i': 'new_v', 'new_v_b_i': 'new_v', 'new_v_lru_lambda': 'new_v', 'new_v_rel_bias': 'new_v', 'new_v_w_branch': 'new_v', 'new_v_w_gate': 'new_v', 'new_v_b_gate': 'new_v', 'new_v_w_out': 'new_v', 'new_v_ln1_g': 'new_v', 'new_v_ln1_b': 'new_v', 'new_v_w_ff1': 'new_v', 'new_v_w_ff2': 'new_v', 'new_v_ln2_g': 'new_v', 'new_v_ln2_b': 'new_v'}


def _forward(args):
    return _fwd_reference(*[args[k] for k in FWD_PARAMS])


def _output_shape():
    def fwd():
        inp = _fwd_setup_inputs(0)
        return _fwd_reference(*[inp[k] for k in FWD_PARAMS])
    out = _jax.eval_shape(fwd)
    return out.shape, out.dtype

N_MICROBATCH = 1
ADAM_LR = 0.001
ADAM_B1 = 0.9
ADAM_B2 = 0.999
ADAM_EPS = 1e-08
ADAM_WD = 0.01
ADAM_STEP = 10
PER_EXAMPLE_BATCH_AXIS = {'x': 0, 'loss_target': 0}
SHARED_INPUTS = []
_WEIGHT_DTYPES = {'ln_in_g': _jnp.float32, 'ln_in_b': _jnp.float32, 'w_in': _jnp.float32, 'b_forget': _jnp.float32, 'conv_w': _jnp.float32, 'conv_b': _jnp.float32, 'w_r': _jnp.float32, 'b_r': _jnp.float32, 'w_i': _jnp.float32, 'b_i': _jnp.float32, 'lru_lambda': _jnp.float32, 'rel_bias': _jnp.float32, 'w_branch': _jnp.float32, 'w_gate': _jnp.float32, 'b_gate': _jnp.float32, 'w_out': _jnp.float32, 'ln1_g': _jnp.float32, 'ln1_b': _jnp.float32, 'w_ff1': _jnp.float32, 'w_ff2': _jnp.float32, 'ln2_g': _jnp.float32, 'ln2_b': _jnp.float32}
MOMENT_SCALE = {'ln_in_g': 7.129136e-01, 'ln_in_b': 5.945262e-01, 'w_in': 2.219368e-02, 'b_forget': 2.548970e-01, 'conv_w': 4.874996e-02, 'conv_b': 3.912969e-01, 'w_r': 9.731001e-03, 'b_r': 1.045813e-02, 'w_i': 1.753935e-02, 'b_i': 1.199068e-02, 'lru_lambda': 2.368313e-02, 'rel_bias': 4.298463e-03, 'w_branch': 1.626634e-02, 'w_gate': 4.989981e-03, 'b_gate': 6.671535e-03, 'w_out': 6.524829e-02, 'ln1_g': 8.669056e-01, 'ln1_b': 5.276892e-01, 'w_ff1': 2.987860e-02, 'w_ff2': 1.717441e-01, 'ln2_g': 2.271552e+01, 'ln2_b': 5.229852e+00}


def _to_microbatches(a, axis):
    t = _jnp.moveaxis(a, axis, 0)
    t = t.reshape((N_MICROBATCH, t.shape[0] // N_MICROBATCH) + t.shape[1:])
    return _jnp.moveaxis(t, 1, axis + 1)


def setup_inputs(seed: int = 0) -> dict:
    inp = _fwd_setup_inputs(seed)
    key = _jax.random.fold_in(_jax.random.key(seed), 7919)
    shape, _ = _output_shape()
    out = dict(inp)
    out["loss_target"] = _jax.random.normal(_jax.random.fold_in(key, 0), shape, _jnp.float32)
    for i, name in enumerate(TWIN_WEIGHTS):
        w = inp[name].astype(_jnp.float32)
        if MOMENT_SCALE is None:
            s = _jnp.sqrt(_jnp.mean(_jnp.square(w)) + 1e-30)
        else:
            s = MOMENT_SCALE[name]
        km, kv = _jax.random.split(_jax.random.fold_in(key, i + 1))
        out[name] = w
        out["m_" + name] = s * _jax.random.normal(km, w.shape, _jnp.float32)
        out["v_" + name] = (s * s) * _jax.random.uniform(kv, w.shape, _jnp.float32, 0.5, 1.5)
    if N_MICROBATCH > 1:
        for name, axis in PER_EXAMPLE_BATCH_AXIS.items():
            out[name] = _to_microbatches(out[name], axis)
    return {'x': out['x'], 'ln_in_g': out['ln_in_g'], 'ln_in_b': out['ln_in_b'], 'w_in': out['w_in'], 'b_forget': out['b_forget'], 'conv_w': out['conv_w'], 'conv_b': out['conv_b'], 'w_r': out['w_r'], 'b_r': out['b_r'], 'w_i': out['w_i'], 'b_i': out['b_i'], 'lru_lambda': out['lru_lambda'], 'rel_bias': out['rel_bias'], 'w_branch': out['w_branch'], 'w_gate': out['w_gate'], 'b_gate': out['b_gate'], 'w_out': out['w_out'], 'ln1_g': out['ln1_g'], 'ln1_b': out['ln1_b'], 'w_ff1': out['w_ff1'], 'w_ff2': out['w_ff2'], 'ln2_g': out['ln2_g'], 'ln2_b': out['ln2_b'], 'loss_target': out['loss_target'], 'm_ln_in_g': out['m_ln_in_g'], 'm_ln_in_b': out['m_ln_in_b'], 'm_w_in': out['m_w_in'], 'm_b_forget': out['m_b_forget'], 'm_conv_w': out['m_conv_w'], 'm_conv_b': out['m_conv_b'], 'm_w_r': out['m_w_r'], 'm_b_r': out['m_b_r'], 'm_w_i': out['m_w_i'], 'm_b_i': out['m_b_i'], 'm_lru_lambda': out['m_lru_lambda'], 'm_rel_bias': out['m_rel_bias'], 'm_w_branch': out['m_w_branch'], 'm_w_gate': out['m_w_gate'], 'm_b_gate': out['m_b_gate'], 'm_w_out': out['m_w_out'], 'm_ln1_g': out['m_ln1_g'], 'm_ln1_b': out['m_ln1_b'], 'm_w_ff1': out['m_w_ff1'], 'm_w_ff2': out['m_w_ff2'], 'm_ln2_g': out['m_ln2_g'], 'm_ln2_b': out['m_ln2_b'], 'v_ln_in_g': out['v_ln_in_g'], 'v_ln_in_b': out['v_ln_in_b'], 'v_w_in': out['v_w_in'], 'v_b_forget': out['v_b_forget'], 'v_conv_w': out['v_conv_w'], 'v_conv_b': out['v_conv_b'], 'v_w_r': out['v_w_r'], 'v_b_r': out['v_b_r'], 'v_w_i': out['v_w_i'], 'v_b_i': out['v_b_i'], 'v_lru_lambda': out['v_lru_lambda'], 'v_rel_bias': out['v_rel_bias'], 'v_w_branch': out['v_w_branch'], 'v_w_gate': out['v_w_gate'], 'v_b_gate': out['v_b_gate'], 'v_w_out': out['v_w_out'], 'v_ln1_g': out['v_ln1_g'], 'v_ln1_b': out['v_ln1_b'], 'v_w_ff1': out['v_w_ff1'], 'v_w_ff2': out['v_w_ff2'], 'v_ln2_g': out['v_ln2_g'], 'v_ln2_b': out['v_ln2_b']}


def _loss(weights, diff, rest, loss_target):
    with _jax.named_scope("forward"):
        args = {**rest, TWIN_DIFF_INPUT: diff, **{k: w.astype(_WEIGHT_DTYPES[k]) for k, w in weights.items()}}
        y = _forward(args)
    with _jax.named_scope("loss_head"):
        err = _jnp.square(y.astype(_jnp.float32) - loss_target)
        return 0.5 * _jnp.sum(_jnp.mean(err, axis=-1)) if err.ndim else 0.5 * err


def _adamw(w, g, m, v):
    m = ADAM_B1 * m + (1.0 - ADAM_B1) * g
    v = ADAM_B2 * v + (1.0 - ADAM_B2) * _jnp.square(g)
    m_hat = m / (1.0 - ADAM_B1 ** ADAM_STEP)
    v_hat = v / (1.0 - ADAM_B2 ** ADAM_STEP)
    delta = -ADAM_LR * (m_hat / (_jnp.sqrt(v_hat) + ADAM_EPS) + ADAM_WD * w)
    return delta, m, v


def reference(x, ln_in_g, ln_in_b, w_in, b_forget, conv_w, conv_b, w_r, b_r, w_i, b_i, lru_lambda, rel_bias, w_branch, w_gate, b_gate, w_out, ln1_g, ln1_b, w_ff1, w_ff2, ln2_g, ln2_b, loss_target, m_ln_in_g, m_ln_in_b, m_w_in, m_b_forget, m_conv_w, m_conv_b, m_w_r, m_b_r, m_w_i, m_b_i, m_lru_lambda, m_rel_bias, m_w_branch, m_w_gate, m_b_gate, m_w_out, m_ln1_g, m_ln1_b, m_w_ff1, m_w_ff2, m_ln2_g, m_ln2_b, v_ln_in_g, v_ln_in_b, v_w_in, v_b_forget, v_conv_w, v_conv_b, v_w_r, v_b_r, v_w_i, v_b_i, v_lru_lambda, v_rel_bias, v_w_branch, v_w_gate, v_b_gate, v_w_out, v_ln1_g, v_ln1_b, v_w_ff1, v_w_ff2, v_ln2_g, v_ln2_b):
    given = dict(x=x, ln_in_g=ln_in_g, ln_in_b=ln_in_b, w_in=w_in, b_forget=b_forget, conv_w=conv_w, conv_b=conv_b, w_r=w_r, b_r=b_r, w_i=w_i, b_i=b_i, lru_lambda=lru_lambda, rel_bias=rel_bias, w_branch=w_branch, w_gate=w_gate, b_gate=b_gate, w_out=w_out, ln1_g=ln1_g, ln1_b=ln1_b, w_ff1=w_ff1, w_ff2=w_ff2, ln2_g=ln2_g, ln2_b=ln2_b, loss_target=loss_target, m_ln_in_g=m_ln_in_g, m_ln_in_b=m_ln_in_b, m_w_in=m_w_in, m_b_forget=m_b_forget, m_conv_w=m_conv_w, m_conv_b=m_conv_b, m_w_r=m_w_r, m_b_r=m_b_r, m_w_i=m_w_i, m_b_i=m_b_i, m_lru_lambda=m_lru_lambda, m_rel_bias=m_rel_bias, m_w_branch=m_w_branch, m_w_gate=m_w_gate, m_b_gate=m_b_gate, m_w_out=m_w_out, m_ln1_g=m_ln1_g, m_ln1_b=m_ln1_b, m_w_ff1=m_w_ff1, m_w_ff2=m_w_ff2, m_ln2_g=m_ln2_g, m_ln2_b=m_ln2_b, v_ln_in_g=v_ln_in_g, v_ln_in_b=v_ln_in_b, v_w_in=v_w_in, v_b_forget=v_b_forget, v_conv_w=v_conv_w, v_conv_b=v_conv_b, v_w_r=v_w_r, v_b_r=v_b_r, v_w_i=v_w_i, v_b_i=v_b_i, v_lru_lambda=v_lru_lambda, v_rel_bias=v_rel_bias, v_w_branch=v_w_branch, v_w_gate=v_w_gate, v_b_gate=v_b_gate, v_w_out=v_w_out, v_ln1_g=v_ln1_g, v_ln1_b=v_ln1_b, v_w_ff1=v_w_ff1, v_w_ff2=v_w_ff2, v_ln2_g=v_ln2_g, v_ln2_b=v_ln2_b)
    weights = {n: given[n] for n in TWIN_WEIGHTS}
    shared = {n: given[n] for n in SHARED_INPUTS}
    per_example = {n: given[n] for n in ['x']}
    grad_fn = _jax.value_and_grad(_loss, argnums=(0, 1))

    def one_microbatch(ex, loss_target):
        ex = dict(ex)
        diff = ex.pop(TWIN_DIFF_INPUT)
        return grad_fn(weights, diff, {**shared, **ex}, loss_target)

    if N_MICROBATCH == 1:
        loss, (grad_w, grad_x) = one_microbatch(per_example, given["loss_target"])
    else:
        def body(carry, xs):
            loss_sum, grad_sum = carry
            l_k, (gw_k, gx_k) = one_microbatch(xs[0], xs[1])
            with _jax.named_scope("update"):
                return (loss_sum + l_k, _jax.tree.map(_jnp.add, grad_sum, gw_k)), gx_k

        init = (_jnp.zeros((), _jnp.float32), _jax.tree.map(_jnp.zeros_like, weights))
        (loss, grad_w), grad_x = _jax.lax.scan(body, init, (per_example, given["loss_target"]))
    with _jax.named_scope("update"):
        delta_w, new_m, new_v = {}, {}, {}
        for n in TWIN_WEIGHTS:
            delta_w[n], new_m[n], new_v[n] = _adamw(weights[n], grad_w[n], given["m_" + n], given["v_" + n])
    return (loss, grad_x, *[grad_w[n] for n in TWIN_WEIGHTS], *[delta_w[n] for n in TWIN_WEIGHTS],
            *[new_m[n] for n in TWIN_WEIGHTS], *[new_v[n] for n in TWIN_WEIGHTS])
```

```python
import functools
import math

import jax
import jax.numpy as jnp
from jax import lax
from jax.experimental import pallas as pl
from jax.experimental.pallas import tpu as pltpu

F32 = jnp.float32
BF16 = jnp.bfloat16

D_MODEL = 2048
DEPTH = 2
CHUNK = 64
HEAD_DIM = 128
N_BRANCH = 4
BRANCH_WIDTH = 512
N_HEADS = 4
CONV_WIDTH = 4
LRU_C = 8.0
LOOKBACK_CHUNKS = 8
BAND = (LOOKBACK_CHUNKS + 1) * CHUNK
PAD_ROWS = LOOKBACK_CHUNKS * CHUNK
REL_CLIP = 256
REL_TABLE = REL_CLIP + CHUNK
REL_PAD = 384
D_FF = 4 * D_MODEL
D_IN = 5636
ALPHA = (2.0 * DEPTH) ** 0.25
LN_EPS = 1e-5
SCALE = HEAD_DIM ** -0.5

ADAM_LR = 0.001
ADAM_B1 = 0.9
ADAM_B2 = 0.999
ADAM_EPS = 1e-08
ADAM_WD = 0.01
ADAM_STEP = 10

N_ATT = 9 * BRANCH_WIDTH
N_REC = 2 * BRANCH_WIDTH + 128

V7X_VMEM_LIMIT = 56 * 1024 * 1024
LANES = 128
ATT_BLOCK = 256

NT = (((1,), (1,)), ((), ()))
TN = (((0,), (0,)), ((), ()))
NN = (((1,), (0,)), ((), ()))

MESH = pl.DeviceIdType.MESH


def _cp(sem=None):
    return pltpu.CompilerParams(dimension_semantics=sem, vmem_limit_bytes=V7X_VMEM_LIMIT)


def _dot(a, b, dims=NN):
    return lax.dot_general(a, b, dims, preferred_element_type=F32)


def _pick(n, prefs):
    for p in prefs:
        if n % p == 0:
            return p
    return n


def _split3(x):
    hi = x.astype(BF16)
    r1 = x - hi.astype(F32)
    mid = r1.astype(BF16)
    lo = (r1 - mid.astype(F32)).astype(BF16)
    return hi, mid, lo


def _split2(x):
    hi = x.astype(BF16)
    lo = (x - hi.astype(F32)).astype(BF16)
    return hi, lo


def _sigmoid(z):
    return 1.0 / (1.0 + jnp.exp(-z))


def _log_sigmoid(z):
    return jnp.minimum(z, 0.0) - jnp.log(1.0 + jnp.exp(-jnp.abs(z)))


def _mm(a, b, *, name, ta=False, tb=False, out_dtypes=(F32,), epilogue=None, extras=(),
        bm=None, bn=None, bk=None):
    M, K = (a.shape[1], a.shape[0]) if ta else a.shape
    N = b.shape[0] if tb else b.shape[1]
    bm = bm or _pick(M, (1024, 512, 256, 128))
    bn = bn or _pick(N, (1024, 1152, 512, 256, 128))
    bk = bk or _pick(K, (1024, 1152, 512, 256, 128))
    nk = K // bk
    a_spec = pl.BlockSpec((bk, bm), lambda i, j, k: (k, i)) if ta else pl.BlockSpec((bm, bk), lambda i, j, k: (i, k))
    b_spec = pl.BlockSpec((bn, bk), lambda i, j, k: (j, k)) if tb else pl.BlockSpec((bk, bn), lambda i, j, k: (k, j))
    ex_specs = [pl.BlockSpec((bm, bn), lambda i, j, k: (i, j)) if kind == 'mn'
                else pl.BlockSpec((1, bn), lambda i, j, k: (0, j)) for _, kind in extras]
    n_ex, n_out = len(extras), len(out_dtypes)
    dims = TN if ta else (NT if tb else NN)

    def kern(*refs):
        a_ref, b_ref = refs[0], refs[1]
        ex_refs = refs[2:2 + n_ex]
        out_refs = refs[2 + n_ex:2 + n_ex + n_out]
        acc_ref = refs[-1]
        k = pl.program_id(2)
        part = _dot(a_ref[...].astype(BF16), b_ref[...].astype(BF16), dims)

        @pl.when(k == 0)
        def _():
            acc_ref[...] = part

        @pl.when(k > 0)
        def _():
            acc_ref[...] += part

        @pl.when(k == nk - 1)
        def _():
            acc = acc_ref[...]
            outs = (acc,) if epilogue is None else epilogue(acc, *[r[...] for r in ex_refs])
            for o_ref, o in zip(out_refs, outs):
                o_ref[...] = o.astype(o_ref.dtype)

    res = pl.pallas_call(
        kern, name=name, grid=(M // bm, N // bn, nk),
        in_specs=[a_spec, b_spec] + ex_specs,
        out_specs=[pl.BlockSpec((bm, bn), lambda i, j, k: (i, j)) for _ in out_dtypes],
        out_shape=[jax.ShapeDtypeStruct((M, N), dt) for dt in out_dtypes],
        scratch_shapes=[pltpu.VMEM((bm, bn), F32)],
        compiler_params=_cp(("parallel", "parallel", "arbitrary")),
    )(a, b, *[e for e, _ in extras])
    return res[0] if n_out == 1 else res


def _ln_fwd(h, g, b, *, name):
    T, D = h.shape
    bt = _pick(T, (512, 256, 128))

    def kern(h_ref, g_ref, b_ref, y_ref, yb_ref):
        x = h_ref[...]
        mu = jnp.mean(x, axis=-1, keepdims=True)
        xc = x - mu
        var = jnp.mean(xc * xc, axis=-1, keepdims=True)
        y = xc * lax.rsqrt(var + LN_EPS) * g_ref[...] + b_ref[...]
        y_ref[...] = y
        yb_ref[...] = y.astype(BF16)

    row = pl.BlockSpec((bt, D), lambda i: (i, 0))
    vec = pl.BlockSpec((1, D), lambda i: (0, 0))
    return pl.pallas_call(
        kern, name=name, grid=(T // bt,), in_specs=[row, vec, vec], out_specs=[row, row],
        out_shape=[jax.ShapeDtypeStruct((T, D), F32), jax.ShapeDtypeStruct((T, D), BF16)],
        compiler_params=_cp(("arbitrary",)),
    )(h, g.reshape(1, D), b.reshape(1, D))


def _ln_bwd(h, dy, g, *, name):
    T, D = h.shape
    bt = _pick(T, (512, 256, 128))

    def kern(h_ref, dy_ref, g_ref, dh_ref, dhb_ref, dg_ref, db_ref):
        i = pl.program_id(0)
        x = h_ref[...]
        dyv = dy_ref[...]
        mu = jnp.mean(x, axis=-1, keepdims=True)
        xc = x - mu
        var = jnp.mean(xc * xc, axis=-1, keepdims=True)
        rstd = lax.rsqrt(var + LN_EPS)
        xhat = xc * rstd
        dxh = dyv * g_ref[...]
        m1 = jnp.mean(dxh, axis=-1, keepdims=True)
        m2 = jnp.mean(dxh * xhat, axis=-1, keepdims=True)
        dh = rstd * (dxh - m1 - xhat * m2)
        dh_ref[...] = dh
        dhb_ref[...] = dh.astype(BF16)
        pg = jnp.sum(dyv * xhat, axis=0, keepdims=True)
        pb = jnp.sum(dyv, axis=0, keepdims=True)

        @pl.when(i == 0)
        def _():
            dg_ref[...] = pg
            db_ref[...] = pb

        @pl.when(i > 0)
        def _():
            dg_ref[...] += pg
            db_ref[...] += pb

    row = pl.BlockSpec((bt, D), lambda i: (i, 0))
    vec = pl.BlockSpec((1, D), lambda i: (0, 0))
    return pl.pallas_call(
        kern, name=name, grid=(T // bt,), in_specs=[row, row, vec], out_specs=[row, row, vec, vec],
        out_shape=[jax.ShapeDtypeStruct((T, D), F32), jax.ShapeDtypeStruct((T, D), BF16),
                   jax.ShapeDtypeStruct((1, D), F32), jax.ShapeDtypeStruct((1, D), F32)],
        compiler_params=_cp(("arbitrary",)),
    )(h, dy, g.reshape(1, D))


def _loss_head(y, tgt, *, name):
    T, D = y.shape
    bt = _pick(T, (512, 256, 128))

    def kern(y_ref, t_ref, dy_ref, loss_ref):
        i = pl.program_id(0)
        e = y_ref[...] - t_ref[...]
        dy_ref[...] = e * (1.0 / D)
        part = 0.5 * jnp.sum(jnp.sum(e * e, axis=-1, keepdims=True) * (1.0 / D), axis=0, keepdims=True)
        part = jnp.broadcast_to(part, (8, LANES))

        @pl.when(i == 0)
        def _():
            loss_ref[...] = part

        @pl.when(i > 0)
        def _():
            loss_ref[...] += part

    row = pl.BlockSpec((bt, D), lambda i: (i, 0))
    return pl.pallas_call(
        kern, name=name, grid=(T // bt,), in_specs=[row, row],
        out_specs=[row, pl.BlockSpec((8, LANES), lambda i: (0, 0))],
        out_shape=[jax.ShapeDtypeStruct((T, D), F32), jax.ShapeDtypeStruct((8, LANES), F32)],
        compiler_params=_cp(("arbitrary",)),
    )(y, tgt)


def _tri(n, upper):
    r = lax.broadcasted_iota(jnp.int32, (n, n), 0)
    c = lax.broadcasted_iota(jnp.int32, (n, n), 1)
    return jnp.where((c >= r) if upper else (c <= r), 1.0, 0.0).astype(BF16)


def _forget_fwd(ff, bf, *, name):
    T = ff.shape[0]
    bt = 256

    def kern(ff_ref, bf_ref, out_ref, carry):
        i = pl.program_id(0)

        @pl.when(i == 0)
        def _():
            carry[...] = jnp.zeros_like(carry)

        ls = _log_sigmoid(ff_ref[...] + bf_ref[...])
        tri = _tri(bt, upper=False)
        hi, mid, lo = _split3(ls)
        cs = _dot(tri, hi) + _dot(tri, mid) + _dot(tri, lo) + carry[0:1, :]
        out_ref[...] = cs
        carry[...] = jnp.broadcast_to(cs[bt - 1:bt, :], carry.shape)

    return pl.pallas_call(
        kern, name=name, grid=(T // bt,),
        in_specs=[pl.BlockSpec((bt, LANES), lambda i: (i, 0)), pl.BlockSpec((1, LANES), lambda i: (0, 0))],
        out_specs=pl.BlockSpec((bt, LANES), lambda i: (i, 0)),
        out_shape=jax.ShapeDtypeStruct((T, LANES), F32),
        scratch_shapes=[pltpu.VMEM((8, LANES), F32)],
        compiler_params=_cp(("arbitrary",)),
    )(ff, bf)


def _forget_bwd(dFk, dFq, ff, bf, *, name):
    T = ff.shape[0]
    bt = 256
    nb = T // bt

    def kern(dFk_ref, dFq_ref, ff_ref, bf_ref, dff_ref, dbf_ref, carry):
        i = pl.program_id(0)

        @pl.when(i == 0)
        def _():
            carry[...] = jnp.zeros_like(carry)
            dbf_ref[...] = jnp.zeros_like(dbf_ref)

        tri = _tri(bt, upper=True)
        hi, mid, lo = _split3(dFk_ref[...] + dFq_ref[...])
        rs = _dot(tri, hi) + _dot(tri, mid) + _dot(tri, lo) + carry[0:1, :]
        carry[...] = jnp.broadcast_to(rs[0:1, :], carry.shape)
        z = ff_ref[...] + bf_ref[...]
        dff = rs * _sigmoid(-z)
        dff_ref[...] = dff.astype(dff_ref.dtype)
        dbf_ref[...] += jnp.sum(dff, axis=0, keepdims=True)

    rev = pl.BlockSpec((bt, LANES), lambda i: (nb - 1 - i, 0))
    vec = pl.BlockSpec((1, LANES), lambda i: (0, 0))
    return pl.pallas_call(
        kern, name=name, grid=(nb,), in_specs=[rev, rev, rev, vec], out_specs=[rev, vec],
        out_shape=[jax.ShapeDtypeStruct((T, LANES), BF16), jax.ShapeDtypeStruct((1, LANES), F32)],
        scratch_shapes=[pltpu.VMEM((8, LANES), F32)],
        compiler_params=_cp(("arbitrary",)),
    )(dFk, dFq, ff, bf)


def _masks(B):
    r = lax.broadcasted_iota(jnp.int32, (B, B), 0)
    c = lax.broadcasted_iota(jnp.int32, (B, B), 1)
    return r, c


def _fox_fwd(u_att, fcol, frow, *, name):
    T = u_att.shape[0]
    B = min(ATT_BLOCK, T)
    nb = T // B
    H = N_HEADS

    def kern(q_ref, k_ref, v_ref, fc_ref, fr_ref, o_ref, lse_ref):
        i = pl.program_id(1)
        q = q_ref[...]
        fq = fc_ref[...]
        r, c = _masks(B)

        def step(j, carry, masked):
            m, l, acc = carry
            off = pl.multiple_of(j * B, B)
            k = k_ref[pl.ds(off, B), :]
            v = v_ref[pl.ds(off, B), :]
            s = _dot(q, k, NT) * SCALE + (fq - fr_ref[j])
            if masked:
                s = jnp.where(c <= r, s, -jnp.inf)
            m_new = jnp.maximum(m, jnp.max(s, axis=1, keepdims=True))
            a = jnp.exp(m - m_new)
            p = jnp.exp(s - m_new)
            l = a * l + jnp.sum(p, axis=1, keepdims=True)
            acc = a * acc + _dot(p.astype(BF16), v)
            return m_new, l, acc

        init = (jnp.full((B, 1), -1e30, F32), jnp.zeros((B, 1), F32), jnp.zeros((B, HEAD_DIM), F32))
        carry = lax.fori_loop(0, i, lambda j, cr: step(j, cr, False), init)
        m, l, acc = step(i, carry, True)
        o_ref[...] = (acc / l).astype(o_ref.dtype)
        lse_ref[...] = m + jnp.log(l)

    return pl.pallas_call(
        kern, name=name, grid=(H, nb),
        in_specs=[pl.BlockSpec((B, HEAD_DIM), lambda h, i: (i, h)),
                  pl.BlockSpec((T, HEAD_DIM), lambda h, i: (0, 4 + h)),
                  pl.BlockSpec((T, HEAD_DIM), lambda h, i: (0, 8 + h)),
                  pl.BlockSpec((None, B, 1), lambda h, i: (h, i, 0)),
                  pl.BlockSpec((None, nb, 1, B), lambda h, i: (h, 0, 0, 0))],
        out_specs=[pl.BlockSpec((B, HEAD_DIM), lambda h, i: (i, h)),
                   pl.BlockSpec((None, B, 1), lambda h, i: (h, i, 0))],
        out_shape=[jax.ShapeDtypeStruct((T, BRANCH_WIDTH), BF16), jax.ShapeDtypeStruct((H, T, 1), F32)],
        compiler_params=_cp(("parallel", "arbitrary")),
    )(u_att, u_att, u_att, fcol, frow)


def _row_dot(a, b, *, name):
    T = a.shape[0]
    bt = _pick(T, (512, 256, 128))

    def kern(a_ref, b_ref, o_ref):
        p = a_ref[...].astype(F32) * b_ref[...].astype(F32)
        for h in range(N_HEADS):
            o_ref[h] = jnp.sum(p[:, h * HEAD_DIM:(h + 1) * HEAD_DIM], axis=1, keepdims=True)

    row = pl.BlockSpec((bt, BRANCH_WIDTH), lambda i: (i, 0))
    return pl.pallas_call(
        kern, name=name, grid=(T // bt,), in_specs=[row, row],
        out_specs=pl.BlockSpec((N_HEADS, bt, 1), lambda i: (0, i, 0)),
        out_shape=jax.ShapeDtypeStruct((N_HEADS, T, 1), F32),
        compiler_params=_cp(("arbitrary",)),
    )(a, b)


def _fox_bwd(u_att, do, lse, delta, fcol, frow, *, name):
    T = u_att.shape[0]
    B = min(ATT_BLOCK, T)
    nb = T // B
    H = N_HEADS

    def kern(q_ref, k_ref, v_ref, do_ref, lse_ref, dl_ref, fc_ref, fr_ref,
             dq_ref, dk_ref, dv_ref, df_ref, dfq_ref, dk_acc, dv_acc, df_acc):
        i = pl.program_id(1)

        @pl.when(i == 0)
        def _():
            dk_acc[...] = jnp.zeros_like(dk_acc)
            dv_acc[...] = jnp.zeros_like(dv_acc)
            df_acc[...] = jnp.zeros_like(df_acc)

        q = q_ref[...]
        dov = do_ref[...]
        fq = fc_ref[...]
        lsev = lse_ref[...]
        dlt = dl_ref[...]
        r, c = _masks(B)

        def step(j, carry, masked):
            dq, dfq = carry
            off = pl.multiple_of(j * B, B)
            k = k_ref[pl.ds(off, B), :]
            v = v_ref[pl.ds(off, B), :]
            s = _dot(q, k, NT) * SCALE + (fq - fr_ref[j])
            p = jnp.exp(s - lsev)
            if masked:
                p = jnp.where(c <= r, p, 0.0)
            dp = _dot(dov, v, NT)
            ds = p * (dp - dlt)
            dsb = ds.astype(BF16)
            dq = dq + _dot(dsb, k)
            dk_acc[pl.ds(off, B), :] += _dot(dsb, q, TN)
            dv_acc[pl.ds(off, B), :] += _dot(p.astype(BF16), dov, TN)
            df_acc[j] += -jnp.sum(ds, axis=0, keepdims=True)
            return dq, dfq + jnp.sum(ds, axis=1, keepdims=True)

        carry = lax.fori_loop(0, i, lambda j, cr: step(j, cr, False),
                              (jnp.zeros((B, HEAD_DIM), F32), jnp.zeros((B, 1), F32)))
        dq, dfq = step(i, carry, True)
        dq_ref[...] = (dq * SCALE).astype(dq_ref.dtype)
        dfq_ref[...] = dfq

        @pl.when(i == nb - 1)
        def _():
            dk_ref[...] = (dk_acc[...] * SCALE).astype(dk_ref.dtype)
            dv_ref[...] = dv_acc[...].astype(dv_ref.dtype)
            df_ref[...] = df_acc[...]

    col = lambda n: pl.BlockSpec((None, B, 1), lambda h, i: (h, i, 0))
    return pl.pallas_call(
        kern, name=name, grid=(H, nb),
        in_specs=[pl.BlockSpec((B, HEAD_DIM), lambda h, i: (i, h)),
                  pl.BlockSpec((T, HEAD_DIM), lambda h, i: (0, 4 + h)),
                  pl.BlockSpec((T, HEAD_DIM), lambda h, i: (0, 8 + h)),
                  pl.BlockSpec((B, HEAD_DIM), lambda h, i: (i, h)),
                  col(0), col(1), col(2),
                  pl.BlockSpec((None, nb, 1, B), lambda h, i: (h, 0, 0, 0))],
        out_specs=[pl.BlockSpec((B, HEAD_DIM), lambda h, i: (i, h)),
                   pl.BlockSpec((T, HEAD_DIM), lambda h, i: (0, h)),
                   pl.BlockSpec((T, HEAD_DIM), lambda h, i: (0, h)),
                   pl.BlockSpec((None, nb, 1, B), lambda h, i: (h, 0, 0, 0)),
                   pl.BlockSpec((None, B, 1), lambda h, i: (h, i, 0))],
        out_shape=[jax.ShapeDtypeStruct((T, BRANCH_WIDTH), BF16)] * 3
                  + [jax.ShapeDtypeStruct((H, nb, 1, B), F32), jax.ShapeDtypeStruct((H, T, 1), F32)],
        scratch_shapes=[pltpu.VMEM((T, HEAD_DIM), F32), pltpu.VMEM((T, HEAD_DIM), F32),
                        pltpu.VMEM((nb, 1, B), F32)],
        compiler_params=_cp(("parallel", "arbitrary")),
    )(u_att, u_att, u_att, do, lse, delta, fcol, frow)


def _softplus_parts(z):
    t = jnp.exp(-jnp.abs(z))
    sp = jnp.maximum(z, 0.0) + jnp.log(1.0 + t)
    return t, sp


def _sb_fwd(u_att, *, name):
    T = u_att.shape[0]
    B = min(ATT_BLOCK, T)
    nb = T // B
    H = N_HEADS

    def kern(q_ref, k_ref, v_ref, o_ref):
        i = pl.program_id(1)
        q = q_ref[...]
        r, c = _masks(B)
        suffix = jnp.where(r >= c, 1.0, 0.0).astype(BF16)

        def step(j, carry, masked):
            run, acc = carry
            off = pl.multiple_of(j * B, B)
            k = k_ref[pl.ds(off, B), :]
            v = v_ref[pl.ds(off, B), :]
            z = _dot(q, k, NT) * SCALE
            _, sp = _softplus_parts(z)
            lg = -sp
            if masked:
                lg = jnp.where(c < r, lg, 0.0)
            hi, lo = _split2(lg)
            cinc = _dot(hi, suffix) + _dot(lo, suffix) + run
            a = jnp.exp(z + cinc)
            if masked:
                a = jnp.where(c < r, a, 0.0)
            acc = acc + _dot(a.astype(BF16), v)
            run = run + jnp.sum(lg, axis=1, keepdims=True)
            return run, acc

        carry = step(i, (jnp.zeros((B, 1), F32), jnp.zeros((B, HEAD_DIM), F32)), True)
        _, acc = lax.fori_loop(0, i, lambda jj, cr: step(i - 1 - jj, cr, False), carry)
        o_ref[...] = acc.astype(o_ref.dtype)

    return pl.pallas_call(
        kern, name=name, grid=(H, nb),
        in_specs=[pl.BlockSpec((B, HEAD_DIM), lambda h, i: (i, 12 + h)),
                  pl.BlockSpec((T, HEAD_DIM), lambda h, i: (0, 16 + h)),
                  pl.BlockSpec((T, HEAD_DIM), lambda h, i: (0, 20 + h))],
        out_specs=pl.BlockSpec((B, HEAD_DIM), lambda h, i: (i, h)),
        out_shape=jax.ShapeDtypeStruct((T, BRANCH_WIDTH), BF16),
        compiler_params=_cp(("parallel", "arbitrary")),
    )(u_att, u_att, u_att)


def _sb_bwd(u_att, do, *, name):
    T = u_att.shape[0]
    B = min(ATT_BLOCK, T)
    nb = T // B
    H = N_HEADS

    def kern(q_ref, k_ref, v_ref, do_ref, dq_ref, dk_ref, dv_ref, dk_acc, dv_acc, de_s, sg_s):
        i = pl.program_id(1)

        @pl.when(i == 0)
        def _():
            dk_acc[...] = jnp.zeros_like(dk_acc)
            dv_acc[...] = jnp.zeros_like(dv_acc)

        q = q_ref[...]
        dov = do_ref[...]
        r, c = _masks(B)
        suffix = jnp.where(r >= c, 1.0, 0.0).astype(BF16)
        prefix = jnp.where(r <= c, 1.0, 0.0).astype(BF16)

        def sweep1(j, run, masked):
            off = pl.multiple_of(j * B, B)
            k = k_ref[pl.ds(off, B), :]
            v = v_ref[pl.ds(off, B), :]
            z = _dot(q, k, NT) * SCALE
            t, sp = _softplus_parts(z)
            lg = -sp
            sg = jnp.where(z >= 0.0, 1.0, t) / (1.0 + t)
            if masked:
                lg = jnp.where(c < r, lg, 0.0)
                sg = jnp.where(c < r, sg, 0.0)
            hi, lo = _split2(lg)
            cinc = _dot(hi, suffix) + _dot(lo, suffix) + run
            a = jnp.exp(z + cinc)
            if masked:
                a = jnp.where(c < r, a, 0.0)
            de = a * _dot(dov, v, NT)
            de_s[j] = de
            sg_s[j] = sg
            dv_acc[pl.ds(off, B), :] += _dot(a.astype(BF16), dov, TN)
            return run + jnp.sum(lg, axis=1, keepdims=True)

        run = sweep1(i, jnp.zeros((B, 1), F32), True)
        lax.fori_loop(0, i, lambda jj, cr: sweep1(i - 1 - jj, cr, False), run)

        def sweep2(j, carry):
            pre, dq = carry
            off = pl.multiple_of(j * B, B)
            k = k_ref[pl.ds(off, B), :]
            de = de_s[j]
            hi, lo = _split2(de)
            g = _dot(hi, prefix) + _dot(lo, prefix) + pre
            dz = (de - sg_s[j] * g).astype(BF16)
            dq = dq + _dot(dz, k)
            dk_acc[pl.ds(off, B), :] += _dot(dz, q, TN)
            return pre + jnp.sum(de, axis=1, keepdims=True), dq

        _, dq = lax.fori_loop(0, i + 1, sweep2, (jnp.zeros((B, 1), F32), jnp.zeros((B, HEAD_DIM), F32)))
        dq_ref[...] = (dq * SCALE).astype(dq_ref.dtype)

        @pl.when(i == nb - 1)
        def _():
            dk_ref[...] = (dk_acc[...] * SCALE).astype(dk_ref.dtype)
            dv_ref[...] = dv_acc[...].astype(dv_ref.dtype)

    return pl.pallas_call(
        kern, name=name, grid=(H, nb),
        in_specs=[pl.BlockSpec((B, HEAD_DIM), lambda h, i: (i, 12 + h)),
                  pl.BlockSpec((T, HEAD_DIM), lambda h, i: (0, 16 + h)),
                  pl.BlockSpec((T, HEAD_DIM), lambda h, i: (0, 20 + h)),
                  pl.BlockSpec((B, HEAD_DIM), lambda h, i: (i, h))],
        out_specs=[pl.BlockSpec((B, HEAD_DIM), lambda h, i: (i, h)),
                   pl.BlockSpec((T, HEAD_DIM), lambda h, i: (0, h)),
                   pl.BlockSpec((T, HEAD_DIM), lambda h, i: (0, h))],
        out_shape=[jax.ShapeDtypeStruct((T, BRANCH_WIDTH), BF16)] * 3,
        scratch_shapes=[pltpu.VMEM((T, HEAD_DIM), F32), pltpu.VMEM((T, HEAD_DIM), F32),
                        pltpu.VMEM((nb, B, B), F32), pltpu.VMEM((nb, B, B), F32)],
        compiler_params=_cp(("parallel", "arbitrary")),
    )(u_att, u_att, u_att, do)


def _rel_onehot(qrow):
    k = lax.broadcasted_iota(jnp.int32, (BAND, REL_PAD), 0)
    rr = lax.broadcasted_iota(jnp.int32, (BAND, REL_PAD), 1)
    idx = jnp.clip(PAD_ROWS + qrow - k, -(CHUNK - 1), REL_CLIP) + (CHUNK - 1)
    return jnp.where(idx == rr, 1.0, 0.0).astype(BF16)


def _band_bias(table, *, name):
    def kern(t_ref, o_ref):
        hi, mid, lo = _split3(t_ref[...])

        def body(qrow, _):
            oh = _rel_onehot(qrow)
            o_ref[qrow] = _dot(hi, oh, NT) + _dot(mid, oh, NT) + _dot(lo, oh, NT)
            return 0

        lax.fori_loop(0, CHUNK, body, 0)

    return pl.pallas_call(
        kern, name=name, out_shape=jax.ShapeDtypeStruct((CHUNK, 8, BAND), F32),
        compiler_params=_cp(),
    )(table)


def _band_bias_bwd(dbias, *, name):
    def kern(d_ref, o_ref):
        def body(qrow, acc):
            oh = _rel_onehot(qrow)
            hi, mid, lo = _split3(d_ref[qrow])
            return acc + _dot(hi, oh) + _dot(mid, oh) + _dot(lo, oh)

        o_ref[...] = lax.fori_loop(0, CHUNK, body, jnp.zeros((8, REL_PAD), F32))

    return pl.pallas_call(
        kern, name=name, out_shape=jax.ShapeDtypeStruct((8, REL_PAD), F32),
        compiler_params=_cp(),
    )(dbias)


def _chunk_rows(T):
    return _pick(T, (512, 256, 128, 64))


def _chunk_scores(q, kw, bias, c_global):
    s = _dot(q, kw, NT) * SCALE + bias
    col = lax.broadcasted_iota(jnp.int32, (CHUNK, BAND), 1)
    valid = (c_global * CHUNK + col) >= PAD_ROWS
    s = jnp.where(valid, s, -jnp.inf)
    m = jnp.max(s, axis=1, keepdims=True)
    e = jnp.exp(s - m)
    return e / jnp.sum(e, axis=1, keepdims=True)


def _chunk_fwd(u_att, bias, *, name):
    T = u_att.shape[0]
    R = _chunk_rows(T)
    nr = T // R
    H = N_HEADS

    def kern(q_ref, k_ref, v_ref, b_ref, o_ref, kpad, vpad):
        i = pl.program_id(1)

        @pl.when(i == 0)
        def _():
            kpad[0:PAD_ROWS, :] = jnp.zeros((PAD_ROWS, HEAD_DIM), BF16)
            vpad[0:PAD_ROWS, :] = jnp.zeros((PAD_ROWS, HEAD_DIM), BF16)
            kpad[PAD_ROWS:, :] = k_ref[...]
            vpad[PAD_ROWS:, :] = v_ref[...]

        bias_v = b_ref[...]
        for cc in range(R // CHUNK):
            cg = i * (R // CHUNK) + cc
            off = pl.multiple_of(cg * CHUNK, CHUNK)
            q = q_ref[cc * CHUNK:(cc + 1) * CHUNK, :]
            kw = kpad[pl.ds(off, BAND), :]
            vw = vpad[pl.ds(off, BAND), :]
            p = _chunk_scores(q, kw, bias_v, cg)
            o_ref[cc * CHUNK:(cc + 1) * CHUNK, :] = _dot(p.astype(BF16), vw).astype(o_ref.dtype)

    return pl.pallas_call(
        kern, name=name, grid=(H, nr),
        in_specs=[pl.BlockSpec((R, HEAD_DIM), lambda h, i: (i, 24 + h)),
                  pl.BlockSpec((T, HEAD_DIM), lambda h, i: (0, 28 + h)),
                  pl.BlockSpec((T, HEAD_DIM), lambda h, i: (0, 32 + h)),
                  pl.BlockSpec((None, CHUNK, BAND), lambda h, i: (h, 0, 0))],
        out_specs=pl.BlockSpec((R, HEAD_DIM), lambda h, i: (i, h)),
        out_shape=jax.ShapeDtypeStruct((T, BRANCH_WIDTH), BF16),
        scratch_shapes=[pltpu.VMEM((T + PAD_ROWS, HEAD_DIM), BF16), pltpu.VMEM((T + PAD_ROWS, HEAD_DIM), BF16)],
        compiler_params=_cp(("parallel", "arbitrary")),
    )(u_att, u_att, u_att, bias)


def _chunk_bwd(u_att, bias, do, *, name):
    T = u_att.shape[0]
    R = _chunk_rows(T)
    nr = T // R
    H = N_HEADS

    def kern(q_ref, k_ref, v_ref, b_ref, do_ref, dq_ref, dk_ref, dv_ref, db_ref, kpad, vpad, dkp, dvp):
        i = pl.program_id(1)

        @pl.when(i == 0)
        def _():
            kpad[0:PAD_ROWS, :] = jnp.zeros((PAD_ROWS, HEAD_DIM), BF16)
            vpad[0:PAD_ROWS, :] = jnp.zeros((PAD_ROWS, HEAD_DIM), BF16)
            kpad[PAD_ROWS:, :] = k_ref[...]
            vpad[PAD_ROWS:, :] = v_ref[...]
            dkp[...] = jnp.zeros_like(dkp)
            dvp[...] = jnp.zeros_like(dvp)
            db_ref[...] = jnp.zeros_like(db_ref)

        bias_v = b_ref[...]
        for cc in range(R // CHUNK):
            cg = i * (R // CHUNK) + cc
            off = pl.multiple_of(cg * CHUNK, CHUNK)
            q = q_ref[cc * CHUNK:(cc + 1) * CHUNK, :]
            dov = do_ref[cc * CHUNK:(cc + 1) * CHUNK, :]
            kw = kpad[pl.ds(off, BAND), :]
            vw = vpad[pl.ds(off, BAND), :]
            p = _chunk_scores(q, kw, bias_v, cg)
            dp = _dot(dov, vw, NT)
            ds = p * (dp - jnp.sum(p * dp, axis=1, keepdims=True))
            dsb = ds.astype(BF16)
            dq_ref[cc * CHUNK:(cc + 1) * CHUNK, :] = (_dot(dsb, kw) * SCALE).astype(dq_ref.dtype)
            dkp[pl.ds(off, BAND), :] += _dot(dsb, q, TN)
            dvp[pl.ds(off, BAND), :] += _dot(p.astype(BF16), dov, TN)
            db_ref[...] += ds

        @pl.when(i == nr - 1)
        def _():
            dk_ref[...] = (dkp[PAD_ROWS:, :] * SCALE).astype(dk_ref.dtype)
            dv_ref[...] = dvp[PAD_ROWS:, :].astype(dv_ref.dtype)

    return pl.pallas_call(
        kern, name=name, grid=(H, nr),
        in_specs=[pl.BlockSpec((R, HEAD_DIM), lambda h, i: (i, 24 + h)),
                  pl.BlockSpec((T, HEAD_DIM), lambda h, i: (0, 28 + h)),
                  pl.BlockSpec((T, HEAD_DIM), lambda h, i: (0, 32 + h)),
                  pl.BlockSpec((None, CHUNK, BAND), lambda h, i: (h, 0, 0)),
                  pl.BlockSpec((R, HEAD_DIM), lambda h, i: (i, h))],
        out_specs=[pl.BlockSpec((R, HEAD_DIM), lambda h, i: (i, h)),
                   pl.BlockSpec((T, HEAD_DIM), lambda h, i: (0, h)),
                   pl.BlockSpec((T, HEAD_DIM), lambda h, i: (0, h)),
                   pl.BlockSpec((None, CHUNK, BAND), lambda h, i: (h, 0, 0))],
        out_shape=[jax.ShapeDtypeStruct((T, BRANCH_WIDTH), BF16)] * 3
                  + [jax.ShapeDtypeStruct((H, CHUNK, BAND), F32)],
        scratch_shapes=[pltpu.VMEM((T + PAD_ROWS, HEAD_DIM), BF16), pltpu.VMEM((T + PAD_ROWS, HEAD_DIM), BF16),
                        pltpu.VMEM((T + PAD_ROWS, HEAD_DIM), F32), pltpu.VMEM((T + PAD_ROWS, HEAD_DIM), F32)],
        compiler_params=_cp(("parallel", "arbitrary")),
    )(u_att, u_att, u_att, bias, do)


LRU_ROWS = 256
HALO = 8


def _gelu(y):
    k0 = math.sqrt(2.0 / math.pi)
    t = jnp.tanh(k0 * (y + 0.044715 * y * y * y))
    return 0.5 * y * (1.0 + t), t


def _gelu_grad(y, t):
    k0 = math.sqrt(2.0 / math.pi)
    return 0.5 * (1.0 + t) + 0.5 * y * (1.0 - t * t) * k0 * (1.0 + 3.0 * 0.044715 * y * y)


def _neg_expm1(y):
    poly = -y * (1.0 + y * (1.0 / 2 + y * (1.0 / 6 + y * (1.0 / 24 + y * (1.0 / 120 + y * (1.0 / 720 + y * (1.0 / 5040)))))))
    return jnp.where(y > -0.5, poly, 1.0 - jnp.exp(y))


def _lru_gates(ext, cw_ref, cb_ref, wr_ref, br_ref, wi_ref, bi_ref, lam_ref, rows):
    xc = cb_ref[...] + jnp.zeros((rows, BRANCH_WIDTH), F32)
    for j in range(CONV_WIDTH):
        xc = xc + ext[pl.ds(HALO - (CONV_WIDTH - 1) + j, rows), :] * cw_ref[j:j + 1, :]
    xcb = xc.astype(BF16)
    zr = jnp.concatenate([_dot(xcb[:, n * 128:(n + 1) * 128], wr_ref[n]) for n in range(4)], axis=1) + br_ref[...]
    zi = jnp.concatenate([_dot(xcb[:, n * 128:(n + 1) * 128], wi_ref[n]) for n in range(4)], axis=1) + bi_ref[...]
    r = _sigmoid(zr)
    gi = _sigmoid(zi)
    ls = _log_sigmoid(lam_ref[...])
    la = LRU_C * r * ls
    a = jnp.exp(la)
    mult = jnp.sqrt(_neg_expm1(2.0 * la))
    return xc, xcb, r, gi, ls, a, mult


def _lru_param_specs():
    full2 = lambda s: pl.BlockSpec(s, lambda i: (0, 0))
    full3 = lambda s: pl.BlockSpec(s, lambda i: (0, 0, 0))
    return [full2((8, BRANCH_WIDTH)), full2((1, BRANCH_WIDTH)), full3((4, 128, 128)), full2((1, BRANCH_WIDTH)),
            full3((4, 128, 128)), full2((1, BRANCH_WIDTH)), full2((1, BRANCH_WIDTH))]


def _lru_fwd(u_rec, p, *, name):
    T = u_rec.shape[0]
    R = min(LRU_ROWS, T)
    nb = T // R
    W = BRANCH_WIDTH
    hb = R // HALO

    def kern(rx_ref, halo_ref, ry_ref, cw_ref, cb_ref, wr_ref, br_ref, wi_ref, bi_ref, lam_ref,
             o_ref, h_ref, ext, a_s, b_s, hc):
        i = pl.program_id(0)

        @pl.when(i == 0)
        def _():
            hc[...] = jnp.zeros_like(hc)

        ext[0:HALO, :] = jnp.where(i == 0, 0.0, halo_ref[...])
        ext[HALO:, :] = rx_ref[...]
        xc, _, r, gi, ls, a, mult = _lru_gates(ext, cw_ref, cb_ref, wr_ref, br_ref, wi_ref, bi_ref, lam_ref, R)
        a_s[...] = a
        b_s[...] = mult * (gi * xc)

        def body(t, h):
            h = a_s[pl.ds(t, 1), :] * h + b_s[pl.ds(t, 1), :]
            h_ref[pl.ds(t, 1), :] = h
            return h

        h = lax.fori_loop(0, R, body, hc[0:1, :], unroll=8)
        hc[...] = jnp.broadcast_to(h, hc.shape)
        g, _ = _gelu(ry_ref[...])
        o_ref[...] = (h_ref[...] * g).astype(o_ref.dtype)

    return pl.pallas_call(
        kern, name=name, grid=(nb,),
        in_specs=[pl.BlockSpec((R, W), lambda i: (i, 0)),
                  pl.BlockSpec((HALO, W), lambda i: (jnp.maximum(i * hb - 1, 0), 0)),
                  pl.BlockSpec((R, W), lambda i: (i, 1))] + _lru_param_specs(),
        out_specs=[pl.BlockSpec((R, W), lambda i: (i, 0)), pl.BlockSpec((R, W), lambda i: (i, 0))],
        out_shape=[jax.ShapeDtypeStruct((T, W), BF16), jax.ShapeDtypeStruct((T, W), F32)],
        scratch_shapes=[pltpu.VMEM((R + HALO, W), F32), pltpu.VMEM((R, W), F32), pltpu.VMEM((R, W), F32),
                        pltpu.VMEM((8, W), F32)],
        compiler_params=_cp(("arbitrary",)),
    )(u_rec, u_rec, u_rec, *p)


def _lru_bwd(u_rec, hs, do, p, *, name):
    T = u_rec.shape[0]
    R = min(LRU_ROWS, T)
    nb = T // R
    W = BRANCH_WIDTH
    hb = R // HALO

    def kern(rx_ref, halo_ref, ry_ref, h_ref, hh_ref, do_ref, cw_ref, cb_ref, wr_ref, br_ref, wi_ref, bi_ref, lam_ref,
             drx_ref, dry_ref, dcw_ref, dcb_ref, dwr_ref, dbr_ref, dwi_ref, dbi_ref, dlam_ref,
             ext, hext, a_s, g_s, dext, gc):
        s = pl.program_id(0)
        first_block = s == nb - 1

        @pl.when(s == 0)
        def _():
            gc[...] = jnp.zeros_like(gc)
            dext[R:, :] = jnp.zeros((HALO, W), F32)
            for ref in (dcw_ref, dcb_ref, dwr_ref, dbr_ref, dwi_ref, dbi_ref, dlam_ref):
                ref[...] = jnp.zeros_like(ref)

        ext[0:HALO, :] = jnp.where(first_block, 0.0, halo_ref[...])
        ext[HALO:, :] = rx_ref[...]
        hext[0:HALO, :] = jnp.where(first_block, 0.0, hh_ref[...])
        hext[HALO:, :] = h_ref[...]
        xc, xcb, r, gi, ls, a, mult = _lru_gates(ext, cw_ref, cb_ref, wr_ref, br_ref, wi_ref, bi_ref, lam_ref, R)
        ry = ry_ref[...]
        gel, th = _gelu(ry)
        dov = do_ref[...].astype(F32)
        dry_ref[...] = (dov * h_ref[...] * _gelu_grad(ry, th)).astype(dry_ref.dtype)
        a_s[...] = a
        g_s[...] = dov * gel

        def body(tt, g):
            t = R - 1 - tt
            dh = g_s[pl.ds(t, 1), :] + g
            g_s[pl.ds(t, 1), :] = dh
            return a_s[pl.ds(t, 1), :] * dh

        g = lax.fori_loop(0, R, body, gc[0:1, :], unroll=8)
        gc[...] = jnp.broadcast_to(g, gc.shape)
        dh = g_s[...]
        hprev = hext[pl.ds(HALO - 1, R), :]
        da = dh * hprev
        gx = gi * xc
        dmult = dh * gx
        dgx = dh * mult
        dgi = dgx * xc
        dxc = dgx * gi
        dla = da * a - dmult * (a * a) / mult
        dr = dla * (LRU_C * ls)
        dlam_ref[...] += jnp.sum(dla * (LRU_C * r), axis=0, keepdims=True)
        dzr = dr * r * (1.0 - r)
        dzi = dgi * gi * (1.0 - gi)
        dbr_ref[...] += jnp.sum(dzr, axis=0, keepdims=True)
        dbi_ref[...] += jnp.sum(dzi, axis=0, keepdims=True)
        dzrb = dzr.astype(BF16)
        dzib = dzi.astype(BF16)
        back = []
        for n in range(4):
            sl = slice(n * 128, (n + 1) * 128)
            dwr_ref[n] += _dot(xcb[:, sl], dzrb[:, sl], TN)
            dwi_ref[n] += _dot(xcb[:, sl], dzib[:, sl], TN)
            back.append(_dot(dzrb[:, sl], wr_ref[n], NT) + _dot(dzib[:, sl], wi_ref[n], NT))
        dxc = dxc + jnp.concatenate(back, axis=1)
        dcb_ref[...] += jnp.sum(dxc, axis=0, keepdims=True)
        for j in range(CONV_WIDTH):
            dcw_ref[j:j + 1, :] += jnp.sum(dxc * ext[pl.ds(HALO - (CONV_WIDTH - 1) + j, R), :], axis=0, keepdims=True)
        dext[0:R, :] = dxc
        drx = jnp.zeros((R, W), F32)
        for j in range(CONV_WIDTH):
            drx = drx + dext[pl.ds(CONV_WIDTH - 1 - j, R), :] * cw_ref[j:j + 1, :]
        drx_ref[...] = drx.astype(drx_ref.dtype)
        dext[R:, :] = dxc[0:HALO, :]

        @pl.when(s == nb - 1)
        def _():
            dlam_ref[...] = dlam_ref[...] * _sigmoid(-lam_ref[...])

    rev = lambda c: pl.BlockSpec((R, W), lambda s: (nb - 1 - s, c))
    halo = lambda: pl.BlockSpec((HALO, W), lambda s: (jnp.maximum((nb - 1 - s) * hb - 1, 0), 0))
    v2 = lambda shp: pl.BlockSpec(shp, lambda s: (0, 0))
    v3 = lambda shp: pl.BlockSpec(shp, lambda s: (0, 0, 0))
    return pl.pallas_call(
        kern, name=name, grid=(nb,),
        in_specs=[rev(0), halo(), rev(1), rev(0), halo(), rev(0)] + _lru_param_specs(),
        out_specs=[rev(0), rev(0), v2((8, W)), v2((1, W)), v3((4, 128, 128)), v2((1, W)), v3((4, 128, 128)),
                   v2((1, W)), v2((1, W))],
        out_shape=[jax.ShapeDtypeStruct((T, W), BF16), jax.ShapeDtypeStruct((T, W), BF16),
                   jax.ShapeDtypeStruct((8, W), F32), jax.ShapeDtypeStruct((1, W), F32),
                   jax.ShapeDtypeStruct((4, 128, 128), F32), jax.ShapeDtypeStruct((1, W), F32),
                   jax.ShapeDtypeStruct((4, 128, 128), F32), jax.ShapeDtypeStruct((1, W), F32),
                   jax.ShapeDtypeStruct((1, W), F32)],
        scratch_shapes=[pltpu.VMEM((R + HALO, W), F32), pltpu.VMEM((R + HALO, W), F32), pltpu.VMEM((R, W), F32),
                        pltpu.VMEM((R, W), F32), pltpu.VMEM((R + HALO, W), F32), pltpu.VMEM((8, W), F32)],
        compiler_params=_cp(("arbitrary",)),
    )(u_rec, u_rec, u_rec, hs, hs, do, *p)


def _merge_fwd(o_all, wb, gate, *, name):
    T = o_all.shape[1]
    D = D_MODEL
    bm = _pick(T, (1024, 512, 256, 128))
    bn = 1024
    nj = D // bn

    def kern(o_ref, w_ref, g_ref, m_ref, pb_ref, acc):
        g = pl.program_id(2)
        pbv = _dot(o_ref[...], w_ref[...])
        pb_ref[...] = pbv.astype(pb_ref.dtype)
        term = g_ref[...].astype(F32) * pbv

        @pl.when(g == 0)
        def _():
            acc[...] = term

        @pl.when(g > 0)
        def _():
            acc[...] += term

        @pl.when(g == N_BRANCH - 1)
        def _():
            m_ref[...] = acc[...].astype(m_ref.dtype)

    return pl.pallas_call(
        kern, name=name, grid=(T // bm, nj, N_BRANCH),
        in_specs=[pl.BlockSpec((None, bm, BRANCH_WIDTH), lambda i, j, g: (g, i, 0)),
                  pl.BlockSpec((None, BRANCH_WIDTH, bn), lambda i, j, g: (g, 0, j)),
                  pl.BlockSpec((bm, bn), lambda i, j, g: (i, g * nj + j))],
        out_specs=[pl.BlockSpec((bm, bn), lambda i, j, g: (i, j)),
                   pl.BlockSpec((bm, bn), lambda i, j, g: (i, g * nj + j))],
        out_shape=[jax.ShapeDtypeStruct((T, D), BF16), jax.ShapeDtypeStruct((T, N_BRANCH * D), BF16)],
        scratch_shapes=[pltpu.VMEM((bm, bn), F32)],
        compiler_params=_cp(("parallel", "parallel", "arbitrary")),
    )(o_all, wb, gate)


def _merge_bwd(dm, gate, pb, *, name):
    T = dm.shape[0]
    D = D_MODEL
    bt = _pick(T, (256, 128))

    def kern(dm_ref, g_ref, pb_ref, dpb_ref, dzg_ref, dbg_ref):
        i = pl.program_id(1)
        dmv = dm_ref[...]
        gv = g_ref[...].astype(F32)
        dpb_ref[...] = (dmv * gv).astype(dpb_ref.dtype)
        dzg = dmv * pb_ref[...].astype(F32) * gv * (1.0 - gv)
        dzg_ref[...] = dzg.astype(dzg_ref.dtype)
        part = jnp.sum(dzg, axis=0, keepdims=True)

        @pl.when(i == 0)
        def _():
            dbg_ref[...] = part

        @pl.when(i > 0)
        def _():
            dbg_ref[...] += part

    return pl.pallas_call(
        kern, name=name, grid=(N_BRANCH, T // bt),
        in_specs=[pl.BlockSpec((bt, D), lambda g, i: (i, 0)),
                  pl.BlockSpec((bt, D), lambda g, i: (i, g)),
                  pl.BlockSpec((bt, D), lambda g, i: (i, g))],
        out_specs=[pl.BlockSpec((None, bt, D), lambda g, i: (g, i, 0)),
                   pl.BlockSpec((bt, D), lambda g, i: (i, g)),
                   pl.BlockSpec((1, D), lambda g, i: (0, g))],
        out_shape=[jax.ShapeDtypeStruct((N_BRANCH, T, D), BF16), jax.ShapeDtypeStruct((T, N_BRANCH * D), BF16),
                   jax.ShapeDtypeStruct((1, N_BRANCH * D), F32)],
        compiler_params=_cp(("parallel", "arbitrary")),
    )(dm, gate, pb)


def _pad_lanes(v, n):
    return jnp.pad(v, [(0, 0)] * (v.ndim - 1) + [(0, n - v.shape[-1])])


def _rows8(v):
    return jnp.pad(v, ((0, 8 - v.shape[0]), (0, 0)))


def _device_step(x, tgt, W):
    T = x.shape[0]
    B = min(ATT_BLOCK, T)
    nb = T // B
    H = N_HEADS
    G = {}
    saved = []

    xf, xb = _ln_fwd(x, W['ln_in_g'], W['ln_in_b'], name='ln_in_fwd')
    for l in range(DEPTH):
        w_att, w_rec = W['w_att'][l], W['w_rec'][l]
        u_att = _mm(xb, w_att, name='in_proj_att', out_dtypes=(BF16,))
        u_rec = _mm(xb, w_rec, name='in_proj_rec', out_dtypes=(F32,))
        ffl = u_rec[:, 2 * BRANCH_WIDTH:]
        bf = _pad_lanes(W['b_forget'][l].reshape(1, H), LANES)
        Fc = _forget_fwd(ffl, bf, name='forget_fwd')
        Fh = Fc[:, :H].T
        fcol = Fh.reshape(H, T, 1)
        frow = Fh.reshape(H, nb, 1, B)
        o_fox, lse = _fox_fwd(u_att, fcol, frow, name='fox_fwd')
        lp = (_rows8(W['conv_w'][l]), W['conv_b'][l].reshape(1, -1), W['w_r'][l].astype(BF16),
              W['b_r'][l].reshape(1, -1), W['w_i'][l].astype(BF16), W['b_i'][l].reshape(1, -1),
              W['lru_lambda'][l].reshape(1, -1))
        o_lru, hs = _lru_fwd(u_rec, lp, name='lru_fwd')
        o_sb = _sb_fwd(u_att, name='sb_fwd')
        table = _rows8(_pad_lanes(W['rel_bias'][l], REL_PAD))
        bias = _band_bias(table, name='band_bias').transpose(1, 0, 2)[:H]
        o_ch = _chunk_fwd(u_att, bias, name='chunk_fwd')
        o_all = jnp.stack([o_fox, o_lru, o_sb, o_ch])
        gate = _mm(xb, W['w_gate_cat'][l], name='gate_proj', out_dtypes=(BF16,),
                   extras=[(W['b_gate'][l].reshape(1, -1), 'n')],
                   epilogue=lambda acc, b: (_sigmoid(acc + b),))
        merged, pb = _merge_fwd(o_all, W['w_branch'][l], gate, name='merge_fwd')
        h1 = _mm(merged, W['w_out'][l], name='out_proj', extras=[(xf, 'mn')],
                 epilogue=lambda acc, xr: (ALPHA * xr + acc,))
        xmf, xmb = _ln_fwd(h1, W['ln1_g'][l], W['ln1_b'][l], name='ln_fwd')
        hid, ra = _mm(xmb, W['w_ff1'][l], name='ff1', out_dtypes=(BF16, BF16),
                      epilogue=lambda acc: (jnp.square(jnp.maximum(acc, 0.0)), jnp.maximum(acc, 0.0)))
        h2 = _mm(hid, W['w_ff2'][l], name='ff2', extras=[(xmf, 'mn')],
                 epilogue=lambda acc, xr: (ALPHA * xr + acc,))
        saved.append(dict(xb=xb, u_att=u_att, u_rec=u_rec, ffl=ffl, bf=bf, fcol=fcol, frow=frow, lse=lse, lp=lp,
                          hs=hs, bias=bias, o_all=o_all, gate=gate, merged=merged, pb=pb, h1=h1, xmb=xmb,
                          hid=hid, ra=ra, h2=h2))
        xf, xb = _ln_fwd(h2, W['ln2_g'][l], W['ln2_b'][l], name='ln_fwd')

    dx, loss_tile = _loss_head(xf, tgt, name='loss_head')
    loss = loss_tile[0, 0]

    for l in reversed(range(DEPTH)):
        S = saved[l]
        dh2, dh2b, G[('ln2_g', l)], G[('ln2_b', l)] = _ln_bwd(S['h2'], dx, W['ln2_g'][l], name='ln_bwd')
        da = _mm(dh2b, W['w_ff2'][l], tb=True, name='ff2_dx', out_dtypes=(BF16,), extras=[(S['ra'], 'mn')],
                 epilogue=lambda acc, rav: (acc * (2.0 * rav.astype(F32)),))
        G[('w_ff2', l)] = _mm(S['hid'], dh2b, ta=True, name='ff2_dw')
        G[('w_ff1', l)] = _mm(S['xmb'], da, ta=True, name='ff1_dw')
        dxm = _mm(da, W['w_ff1'][l], tb=True, name='ff1_dx', extras=[(dh2, 'mn')],
                  epilogue=lambda acc, d: (ALPHA * d + acc,))
        dh1, dh1b, G[('ln1_g', l)], G[('ln1_b', l)] = _ln_bwd(S['h1'], dxm, W['ln1_g'][l], name='ln_bwd')
        dm = _mm(dh1b, W['w_out'][l], tb=True, name='out_dx')
        G[('w_out', l)] = _mm(S['merged'], dh1b, ta=True, name='out_dw')
        dpb, dzg, G[('b_gate', l)] = _merge_bwd(dm, S['gate'], S['pb'], name='merge_bwd')
        do = [_mm(dpb[g], W['w_branch'][l][g], tb=True, name='branch_dx', out_dtypes=(BF16,)) for g in range(N_BRANCH)]
        G[('w_branch', l)] = jnp.stack(
            [_mm(S['o_all'][g], dpb[g], ta=True, name='branch_dw') for g in range(N_BRANCH)])
        G[('w_gate_cat', l)] = _mm(S['xb'], dzg, ta=True, name='gate_dw')
        u_att, u_rec = S['u_att'], S['u_rec']
        delta = _row_dot(do[0], S['o_all'][0], name='row_dot')
        fdq, fdk, fdv, dfk, dfq = _fox_bwd(u_att, do[0], S['lse'], delta, S['fcol'], S['frow'], name='fox_bwd')
        dff, dbf = _forget_bwd(_pad_lanes(dfk.reshape(H, T).T, LANES), _pad_lanes(dfq.reshape(H, T).T, LANES),
                               S['ffl'], S['bf'], name='forget_bwd')
        G[('b_forget', l)] = dbf[0, :H]
        (drx, dry, dcw, dcb, G[('w_r', l)], dbr, G[('w_i', l)], dbi, dlam) = _lru_bwd(
            u_rec, S['hs'], do[1], S['lp'], name='lru_bwd')
        G[('conv_w', l)], G[('conv_b', l)] = dcw[:CONV_WIDTH], dcb[0]
        G[('b_r', l)], G[('b_i', l)], G[('lru_lambda', l)] = dbr[0], dbi[0], dlam[0]
        sdq, sdk, sdv = _sb_bwd(u_att, do[2], name='sb_bwd')
        cdq, cdk, cdv, dbias = _chunk_bwd(u_att, S['bias'], do[3], name='chunk_bwd')
        dtab = _band_bias_bwd(jnp.pad(dbias, ((0, 8 - H), (0, 0), (0, 0))).transpose(1, 0, 2), name='band_bias_bwd')
        G[('rel_bias', l)] = dtab[:H, :REL_TABLE]
        du_att = jnp.concatenate([fdq, fdk, fdv, sdq, sdk, sdv, cdq, cdk, cdv], axis=1)
        du_rec = jnp.concatenate([drx, dry, dff], axis=1)
        G[('w_att', l)] = _mm(S['xb'], du_att, ta=True, name='in_att_dw')
        G[('w_rec', l)] = _mm(S['xb'], du_rec, ta=True, name='in_rec_dw')
        t1 = _mm(dzg, W['w_gate_cat'][l], tb=True, name='gate_dx', extras=[(dh1, 'mn')],
                 epilogue=lambda acc, d: (ALPHA * d + acc,))
        t2 = _mm(du_att, W['w_att'][l], tb=True, name='in_att_dx', extras=[(t1, 'mn')],
                 epilogue=lambda acc, d: (d + acc,))
        dx = _mm(du_rec, W['w_rec'][l], tb=True, name='in_rec_dx', extras=[(t2, 'mn')],
                 epilogue=lambda acc, d: (d + acc,))

    gx, _, G[('ln_in_g', -1)], G[('ln_in_b', -1)] = _ln_bwd(x, dx, W['ln_in_g'], name='ln_in_bwd')
    return loss, gx, G


_IN_FQKV = (0, 1536)
_IN_FF = (1536, 1540)
_IN_REC = (1540, 2564)
_IN_REST = (2564, D_IN)


def _prep_weights(full):
    w_in = full['w_in']
    L = w_in.shape[0]
    W = dict(full)
    W['w_att'] = jnp.concatenate([w_in[..., _IN_FQKV[0]:_IN_FQKV[1]], w_in[..., _IN_REST[0]:_IN_REST[1]]], -1).astype(BF16)
    W['w_rec'] = jnp.concatenate([w_in[..., _IN_REC[0]:_IN_REC[1]], w_in[..., _IN_FF[0]:_IN_FF[1]],
                                  jnp.zeros((L, D_MODEL, N_REC - 1024 - N_HEADS), w_in.dtype)], -1).astype(BF16)
    W['w_gate_cat'] = full['w_gate'].transpose(0, 2, 1, 3).reshape(L, D_MODEL, N_BRANCH * D_MODEL).astype(BF16)
    W['b_gate'] = full['b_gate'].reshape(L, N_BRANCH * D_MODEL)
    for n in ('w_branch', 'w_out', 'w_ff1', 'w_ff2'):
        W[n] = full[n].astype(BF16)
    return W


def _grads_to_reference_layout(G):
    out = {'ln_in_g': G[('ln_in_g', -1)][0], 'ln_in_b': G[('ln_in_b', -1)][0]}
    st = lambda n: jnp.stack([G[(n, l)] for l in range(DEPTH)])
    g_att, g_rec = st('w_att'), st('w_rec')
    out['w_in'] = jnp.concatenate([g_att[..., :1536], g_rec[..., 1024:1024 + N_HEADS], g_rec[..., :1024],
                                   g_att[..., 1536:]], -1)
    out['w_gate'] = st('w_gate_cat').reshape(DEPTH, D_MODEL, N_BRANCH, D_MODEL).transpose(0, 2, 1, 3)
    out['b_gate'] = st('b_gate').reshape(DEPTH, N_BRANCH, D_MODEL)
    for n in ('ln1_g', 'ln1_b', 'ln2_g', 'ln2_b'):
        out[n] = st(n)[:, 0]
    for n in ('b_forget', 'conv_w', 'conv_b', 'w_r', 'b_r', 'w_i', 'b_i', 'lru_lambda', 'rel_bias', 'w_branch',
              'w_out', 'w_ff1', 'w_ff2'):
        out[n] = st(n)
    return out


HBM_SPEC = pl.BlockSpec(memory_space=pl.ANY)
N_CHIPS = 4
PACK_COLS = 1024


def _place():
    x, y, c = lax.axis_index("x"), lax.axis_index("y"), lax.axis_index("c")
    chips = [(1 - x, y), (x, 1 - y), (1 - x, 1 - y)]
    return x, y, c, chips


def _remote(src, dst, send_sems, recv_sems, k, to):
    return pltpu.make_async_remote_copy(src_ref=src, dst_ref=dst, send_sem=send_sems.at[k], recv_sem=recv_sems.at[k],
                                        device_id=to, device_id_type=MESH)


def _gather_layers(params, *, name):
    n = len(params)

    def body(*refs):
        ins, outs = refs[:n], refs[n:2 * n]
        send_sems, recv_sems, local_sems = refs[2 * n:]
        x, y, c, chips = _place()
        me, sibling, k = (x, y, c), (x, y, 1 - c), 2 * x + y
        local = [pltpu.make_async_copy(ins[p], outs[p].at[k], local_sems.at[p]) for p in range(n)]
        for cp in local:
            cp.start()
        first, passed = [], []
        for p in range(n):
            for j, (cx, cy) in enumerate(chips):
                cp = _remote(ins[p].at[c], outs[p].at[k, c], send_sems, recv_sems, 6 * p + j, (cx, cy, c))
                cp.start()
                first.append(cp)
        for p in range(n):
            for j, (cx, cy) in enumerate(chips):
                blk = outs[p].at[2 * cx + cy, c]
                _remote(blk, blk, send_sems, recv_sems, 6 * p + j, me).wait_recv()
                cp = _remote(blk, blk, send_sems, recv_sems, 6 * p + 3 + j, sibling)
                cp.start()
                passed.append(cp)
        for p in range(n):
            for j, (cx, cy) in enumerate(chips):
                blk = outs[p].at[2 * cx + cy, 1 - c]
                _remote(blk, blk, send_sems, recv_sems, 6 * p + 3 + j, me).wait_recv()
        for cp in first + passed:
            cp.wait_send()
        for cp in local:
            cp.wait()

    return pl.pallas_call(
        body, name=name, in_specs=[HBM_SPEC] * n, out_specs=[HBM_SPEC] * n,
        out_shape=[jax.ShapeDtypeStruct((N_CHIPS,) + a.shape, a.dtype) for a in params],
        scratch_shapes=[pltpu.SemaphoreType.DMA((6 * n,)), pltpu.SemaphoreType.DMA((6 * n,)),
                        pltpu.SemaphoreType.DMA((n,))],
    )(*params)


def _pair_exchange(g, *, name):
    _, _, R, C = g.shape

    def body(g_ref, out_ref, send_sems, recv_sems):
        x, y, c, _ = _place()
        cps = [_remote(g_ref.at[k, 1 - c], out_ref.at[k], send_sems, recv_sems, k, (x, y, 1 - c))
               for k in range(N_CHIPS)]
        for cp in cps:
            cp.start()
        for cp in cps:
            cp.wait()

    return pl.pallas_call(
        body, name=name, in_specs=[HBM_SPEC], out_specs=HBM_SPEC,
        out_shape=jax.ShapeDtypeStruct((N_CHIPS, R, C), g.dtype),
        scratch_shapes=[pltpu.SemaphoreType.DMA((N_CHIPS,)), pltpu.SemaphoreType.DMA((N_CHIPS,))],
    )(g)


def _chip_exchange(s, *, name):
    _, R, C = s.shape

    def body(s_ref, out_ref, send_sems, recv_sems, local_sem):
        x, y, c, chips = _place()
        k = 2 * x + y
        local = pltpu.make_async_copy(s_ref.at[k], out_ref.at[k], local_sem)
        local.start()
        cps = [_remote(s_ref.at[2 * cx + cy], out_ref.at[k], send_sems, recv_sems, j, (cx, cy, c))
               for j, (cx, cy) in enumerate(chips)]
        for cp in cps:
            cp.start()
        for j, (cx, cy) in enumerate(chips):
            slot = out_ref.at[2 * cx + cy]
            _remote(slot, slot, send_sems, recv_sems, j, (x, y, c)).wait_recv()
        for cp in cps:
            cp.wait_send()
        local.wait()

    return pl.pallas_call(
        body, name=name, in_specs=[HBM_SPEC], out_specs=HBM_SPEC,
        out_shape=jax.ShapeDtypeStruct((N_CHIPS, R, C), s.dtype),
        scratch_shapes=[pltpu.SemaphoreType.DMA((3,)), pltpu.SemaphoreType.DMA((3,)), pltpu.SemaphoreType.DMA],
    )(s)


def _pair_allgather(r, *, name):
    R, C = r.shape

    def body(r_ref, out_ref, send_sems, recv_sems, local_sem):
        x, y, c, _ = _place()
        local = pltpu.make_async_copy(r_ref, out_ref.at[c], local_sem)
        local.start()
        cp = _remote(r_ref, out_ref.at[c], send_sems, recv_sems, 0, (x, y, 1 - c))
        cp.start()
        other = out_ref.at[1 - c]
        _remote(other, other, send_sems, recv_sems, 0, (x, y, c)).wait_recv()
        cp.wait_send()
        local.wait()

    return pl.pallas_call(
        body, name=name, in_specs=[HBM_SPEC], out_specs=HBM_SPEC,
        out_shape=jax.ShapeDtypeStruct((2, R, C), r.dtype),
        scratch_shapes=[pltpu.SemaphoreType.DMA((1,)), pltpu.SemaphoreType.DMA((1,)), pltpu.SemaphoreType.DMA],
    )(r)


def _gather8(v, *, name):
    R, C = v.shape
    flips = [(bx, by, bc) for bx in (0, 1) for by in (0, 1) for bc in (0, 1)][1:]

    def body(v_ref, out_ref, send_sems, recv_sems, local_sem):
        x, y, c, _ = _place()
        flip = lambda a, b: 1 - a if b else a
        mine = out_ref.at[4 * x + 2 * y + c]
        local = pltpu.make_async_copy(v_ref, mine, local_sem)
        local.start()
        peers = [(flip(x, bx), flip(y, by), flip(c, bc)) for bx, by, bc in flips]
        cps = [_remote(v_ref, mine, send_sems, recv_sems, j, peer) for j, peer in enumerate(peers)]
        for cp in cps:
            cp.start()
        for j, (px, py, pc) in enumerate(peers):
            slot = out_ref.at[4 * px + 2 * py + pc]
            _remote(slot, slot, send_sems, recv_sems, j, (x, y, c)).wait_recv()
        for cp in cps:
            cp.wait_send()
        local.wait()

    return pl.pallas_call(
        body, name=name, in_specs=[HBM_SPEC], out_specs=HBM_SPEC,
        out_shape=jax.ShapeDtypeStruct((8, R, C), v.dtype),
        scratch_shapes=[pltpu.SemaphoreType.DMA((7,)), pltpu.SemaphoreType.DMA((7,)), pltpu.SemaphoreType.DMA],
    )(v)


def _row_block(rows, cols, limit=256 * 1024):
    if rows * cols <= limit:
        return rows
    for br in range(limit // cols // 8 * 8, 0, -8):
        if rows % br == 0:
            return br
    return rows


def _sum_slots(buf, *, name):
    n, R, C = buf.shape
    br = _row_block(R, C)

    def kern(b_ref, o_ref):
        acc = b_ref[0]
        for s in range(1, n):
            acc = acc + b_ref[s]
        o_ref[...] = acc

    return pl.pallas_call(
        kern, name=name, grid=(pl.cdiv(R, br),),
        in_specs=[pl.BlockSpec((n, br, C), lambda i: (0, i, 0))],
        out_specs=pl.BlockSpec((br, C), lambda i: (i, 0)),
        out_shape=jax.ShapeDtypeStruct((R, C), buf.dtype),
        compiler_params=_cp(("arbitrary",)),
    )(buf)


def _sum_pair(g, other, c, *, name):
    _, _, R, C = g.shape
    br = _row_block(R, C)

    def kern(c_ref, g_ref, o_ref, out_ref):
        out_ref[...] = g_ref[...] + o_ref[...]

    return pl.pallas_call(
        kern, name=name,
        grid_spec=pltpu.PrefetchScalarGridSpec(
            num_scalar_prefetch=1, grid=(N_CHIPS, pl.cdiv(R, br)),
            in_specs=[pl.BlockSpec((None, None, br, C), lambda k, i, cr: (k, cr[0], i, 0)),
                      pl.BlockSpec((None, br, C), lambda k, i, cr: (k, i, 0))],
            out_specs=pl.BlockSpec((None, br, C), lambda k, i, cr: (k, i, 0))),
        out_shape=jax.ShapeDtypeStruct((N_CHIPS, R, C), g.dtype),
        compiler_params=_cp(("arbitrary", "arbitrary")),
    )(c.reshape(1).astype(jnp.int32), g, other)


def _adamw(w, g, m, v, *, name):
    shape = w.shape
    cols = shape[-1]
    w2, g2, m2, v2 = (a.reshape(-1, cols) for a in (w, g, m, v))
    rows = w2.shape[0]
    br = _row_block(rows, cols)

    def kern(w_ref, g_ref, m_ref, v_ref, d_ref, nm_ref, nv_ref):
        gv = g_ref[...]
        nm = ADAM_B1 * m_ref[...] + (1.0 - ADAM_B1) * gv
        nv = ADAM_B2 * v_ref[...] + (1.0 - ADAM_B2) * jnp.square(gv)
        m_hat = nm / (1.0 - ADAM_B1 ** ADAM_STEP)
        v_hat = nv / (1.0 - ADAM_B2 ** ADAM_STEP)
        d_ref[...] = -ADAM_LR * (m_hat / (jnp.sqrt(v_hat) + ADAM_EPS) + ADAM_WD * w_ref[...])
        nm_ref[...] = nm
        nv_ref[...] = nv

    spec = pl.BlockSpec((br, cols), lambda i: (i, 0))
    outs = pl.pallas_call(
        kern, name=name, grid=(rows // br,), in_specs=[spec] * 4, out_specs=[spec] * 3,
        out_shape=[jax.ShapeDtypeStruct((rows, cols), F32)] * 3,
        compiler_params=_cp(("arbitrary",)),
    )(w2, g2, m2, v2)
    return [o.reshape(shape) for o in outs]


_NAMES = ['ln_in_g', 'ln_in_b', 'w_in', 'b_forget', 'conv_w', 'conv_b', 'w_r', 'b_r', 'w_i', 'b_i', 'lru_lambda',
          'rel_bias', 'w_branch', 'w_gate', 'b_gate', 'w_out', 'ln1_g', 'ln1_b', 'w_ff1', 'w_ff2', 'ln2_g', 'ln2_b']
_BIG = {'w_in': 2, 'w_branch': 3, 'w_gate': 2, 'w_out': 1, 'w_ff1': 2, 'w_ff2': 1}
_SMALL_SHARDED = {'b_gate': 2, 'conv_w': 2, 'rel_bias': 2}
_SHARDED = {**_BIG, **_SMALL_SHARDED}
_REPLICATED = [n for n in _NAMES if n not in _SHARDED]
_TILE = 8 * LANES


def _tiles(a, cols):
    flat = a.reshape(-1)
    per = 8 * cols
    flat = jnp.pad(flat, (0, (-flat.shape[0]) % per))
    return flat.reshape(-1, cols)


def _pack(arrs, cols):
    return jnp.concatenate([_tiles(a, cols) for a in arrs], axis=0)


def _unpack(packed, like, cols):
    out, r0 = [], 0
    for a in like:
        n = math.prod(a.shape)
        rows = -(-n // (8 * cols)) * 8
        out.append(packed[r0:r0 + rows].reshape(-1)[:n].reshape(a.shape))
        r0 += rows
    return out


def _unshard(blocks, axis):
    return jnp.concatenate([blocks[k] for k in range(N_CHIPS)], axis=axis)


def kernel(x, ln_in_g, ln_in_b, w_in, b_forget, conv_w, conv_b, w_r, b_r, w_i, b_i, lru_lambda, rel_bias, w_branch, w_gate, b_gate, w_out, ln1_g, ln1_b, w_ff1, w_ff2, ln2_g, ln2_b, loss_target, m_ln_in_g, m_ln_in_b, m_w_in, m_b_forget, m_conv_w, m_conv_b, m_w_r, m_b_r, m_w_i, m_b_i, m_lru_lambda, m_rel_bias, m_w_branch, m_w_gate, m_b_gate, m_w_out, m_ln1_g, m_ln1_b, m_w_ff1, m_w_ff2, m_ln2_g, m_ln2_b, v_ln_in_g, v_ln_in_b, v_w_in, v_b_forget, v_conv_w, v_conv_b, v_w_r, v_b_r, v_w_i, v_b_i, v_lru_lambda, v_rel_bias, v_w_branch, v_w_gate, v_b_gate, v_w_out, v_ln1_g, v_ln1_b, v_w_ff1, v_w_ff2, v_ln2_g, v_ln2_b):
    w = dict(zip(_NAMES, (ln_in_g, ln_in_b, w_in, b_forget, conv_w, conv_b, w_r, b_r, w_i, b_i, lru_lambda, rel_bias,
                          w_branch, w_gate, b_gate, w_out, ln1_g, ln1_b, w_ff1, w_ff2, ln2_g, ln2_b)))
    m = dict(zip(_NAMES, (m_ln_in_g, m_ln_in_b, m_w_in, m_b_forget, m_conv_w, m_conv_b, m_w_r, m_b_r, m_w_i, m_b_i,
                          m_lru_lambda, m_rel_bias, m_w_branch, m_w_gate, m_b_gate, m_w_out, m_ln1_g, m_ln1_b,
                          m_w_ff1, m_w_ff2, m_ln2_g, m_ln2_b)))
    v = dict(zip(_NAMES, (v_ln_in_g, v_ln_in_b, v_w_in, v_b_forget, v_conv_w, v_conv_b, v_w_r, v_b_r, v_w_i, v_b_i,
                          v_lru_lambda, v_rel_bias, v_w_branch, v_w_gate, v_b_gate, v_w_out, v_ln1_g, v_ln1_b,
                          v_w_ff1, v_w_ff2, v_ln2_g, v_ln2_b)))
    c = lax.axis_index("c")

    small_like = [w[n] for n in _SMALL_SHARDED]
    small_pack = jnp.stack([_pack([a[l] for a in small_like], LANES) for l in range(DEPTH)])
    gathered = _gather_layers([w[n].astype(BF16) for n in _BIG] + [small_pack], name='gather_weights')
    full = {n: w[n] for n in _REPLICATED}
    for n, blocks in zip(_BIG, gathered):
        full[n] = _unshard(blocks, _BIG[n])
    small_blocks = [[_unpack(gathered[-1][k, l], [a[l] for a in small_like], LANES) for l in range(DEPTH)]
                    for k in range(N_CHIPS)]
    for i, n in enumerate(_SMALL_SHARDED):
        full[n] = jnp.concatenate([jnp.stack([small_blocks[k][l][i] for l in range(DEPTH)])
                                   for k in range(N_CHIPS)], axis=_SMALL_SHARDED[n])

    loss, gx, G = _device_step(x[0], loss_target[0], _prep_weights(full))
    g_full = _grads_to_reference_layout(G)

    def blocks_of(n):
        return jnp.stack(jnp.split(g_full[n], N_CHIPS, axis=_SHARDED[n]))

    shard_like = [w[n] for n in _SHARDED]
    packed = jnp.concatenate(
        [jnp.stack([_tiles(blocks_of(n)[k], PACK_COLS) for k in range(N_CHIPS)]) for n in _SHARDED], axis=1)
    rows = packed.shape[1]
    packed = jnp.pad(packed, ((0, 0), (0, (-rows) % 16), (0, 0)))
    half = packed.shape[1] // 2
    packed = packed.reshape(N_CHIPS, 2, half, PACK_COLS)
    from_sibling = _pair_exchange(packed, name='grad_pair_exchange')
    pair_sum = _sum_pair(packed, from_sibling, c, name='grad_pair_sum')
    from_chips = _chip_exchange(pair_sum, name='grad_chip_exchange')
    mine = _sum_slots(from_chips, name='grad_chip_sum')
    reduced = _pair_allgather(mine, name='grad_pair_allgather').reshape(2 * half, PACK_COLS)
    g_shard = dict(zip(_SHARDED, _unpack(reduced, shard_like, PACK_COLS)))

    rep_like = [w[n] for n in _REPLICATED]
    rep_all = _gather8(_pack([g_full[n] for n in _REPLICATED], LANES), name='grad_gather8')
    g_rep = dict(zip(_REPLICATED, _unpack(_sum_slots(rep_all, name='grad_sum8'), rep_like, LANES)))

    grads, delta, new_m, new_v = {}, {}, {}, {}
    for n in _BIG:
        grads[n] = g_shard[n]
        delta[n], new_m[n], new_v[n] = _adamw(w[n], grads[n], m[n], v[n], name='adamw')
    small = _REPLICATED + list(_SMALL_SHARDED)
    for n in small:
        grads[n] = g_rep[n] if n in g_rep else g_shard[n]
    packs = [_pack([d[n] for n in small], LANES) for d in (w, grads, m, v)]
    outs = _adamw(*packs, name='adamw_small')
    small_like_all = [w[n] for n in small]
    for d, o in zip((delta, new_m, new_v), outs):
        d.update(zip(small, _unpack(o, small_like_all, LANES)))

    loss = lax.psum(loss, ("x", "y", "c"))
    return (loss, gx[None], *[grads[n] for n in _NAMES], *[delta[n] for n in _NAMES],
            *[new_m[n] for n in _NAMES], *[new_v[n] for n in _NAMES])
```

```python
import functools
import math

import jax
import jax.numpy as jnp
from jax import lax
from jax.experimental import pallas as pl
from jax.experimental.pallas import tpu as pltpu

F32 = jnp.float32
BF16 = jnp.bfloat16

D_MODEL = 2048
DEPTH = 2
CHUNK = 64
HEAD_DIM = 128
N_BRANCH = 4
BRANCH_WIDTH = 512
N_HEADS = 4
CONV_WIDTH = 4
LRU_C = 8.0
LOOKBACK_CHUNKS = 8
BAND = (LOOKBACK_CHUNKS + 1) * CHUNK
PAD_ROWS = LOOKBACK_CHUNKS * CHUNK
REL_CLIP = 256
REL_TABLE = REL_CLIP + CHUNK
REL_PAD = 384
D_FF = 4 * D_MODEL
D_IN = 5636
ALPHA = (2.0 * DEPTH) ** 0.25
LN_EPS = 1e-5
SCALE = HEAD_DIM ** -0.5

ADAM_LR = 0.001
ADAM_B1 = 0.9
ADAM_B2 = 0.999
ADAM_EPS = 1e-08
ADAM_WD = 0.01
ADAM_STEP = 10

N_ATT = 9 * BRANCH_WIDTH
N_REC = 2 * BRANCH_WIDTH + 128

V7X_VMEM_LIMIT = 56 * 1024 * 1024
LANES = 128
ATT_BLOCK = 256
ATT_KEYS = 1024

NT = (((1,), (1,)), ((), ()))
TN = (((0,), (0,)), ((), ()))
NN = (((1,), (0,)), ((), ()))

MESH = pl.DeviceIdType.MESH


def _cp(sem=None):
    return pltpu.CompilerParams(dimension_semantics=sem, vmem_limit_bytes=V7X_VMEM_LIMIT)


def _dot(a, b, dims=NN):
    return lax.dot_general(a, b, dims, preferred_element_type=F32)


def _pick(n, prefs):
    for p in prefs:
        if n % p == 0:
            return p
    return n


def _split3(x):
    hi = x.astype(BF16)
    r1 = x - hi.astype(F32)
    mid = r1.astype(BF16)
    lo = (r1 - mid.astype(F32)).astype(BF16)
    return hi, mid, lo


def _split2(x):
    hi = x.astype(BF16)
    lo = (x - hi.astype(F32)).astype(BF16)
    return hi, lo


def _sigmoid(z):
    return 1.0 / (1.0 + jnp.exp(-z))


def _log_sigmoid(z):
    return jnp.minimum(z, 0.0) - jnp.log(1.0 + jnp.exp(-jnp.abs(z)))


def _mm(a, b, *, name, ta=False, tb=False, out_dtypes=(F32,), epilogue=None, extras=(),
        bm=None, bn=None, bk=None):
    M, K = (a.shape[1], a.shape[0]) if ta else a.shape
    N = b.shape[0] if tb else b.shape[1]
    bm = bm or _pick(M, (1024, 512, 256, 128))
    bn = bn or _pick(N, (1024, 1152, 512, 256, 128))
    bk = bk or _pick(K, (2048, 1536, 1024, 1152, 512, 256, 128))
    nk = K // bk
    a_spec = pl.BlockSpec((bk, bm), lambda i, j, k: (k, i)) if ta else pl.BlockSpec((bm, bk), lambda i, j, k: (i, k))
    b_spec = pl.BlockSpec((bn, bk), lambda i, j, k: (j, k)) if tb else pl.BlockSpec((bk, bn), lambda i, j, k: (k, j))
    ex_specs = [pl.BlockSpec((bm, bn), lambda i, j, k: (i, j)) if kind == 'mn'
                else pl.BlockSpec((1, bn), lambda i, j, k: (0, j)) for _, kind in extras]
    n_ex, n_out = len(extras), len(out_dtypes)
    dims = TN if ta else (NT if tb else NN)

    def kern(*refs):
        a_ref, b_ref = refs[0], refs[1]
        ex_refs = refs[2:2 + n_ex]
        out_refs = refs[2 + n_ex:2 + n_ex + n_out]
        acc_ref = refs[-1]
        k = pl.program_id(2)
        part = _dot(a_ref[...].astype(BF16), b_ref[...].astype(BF16), dims)

        @pl.when(k == 0)
        def _():
            acc_ref[...] = part

        @pl.when(k > 0)
        def _():
            acc_ref[...] += part

        @pl.when(k == nk - 1)
        def _():
            acc = acc_ref[...]
            outs = (acc,) if epilogue is None else epilogue(acc, *[r[...] for r in ex_refs])
            for o_ref, o in zip(out_refs, outs):
                o_ref[...] = o.astype(o_ref.dtype)

    res = pl.pallas_call(
        kern, name=name, grid=(M // bm, N // bn, nk),
        in_specs=[a_spec, b_spec] + ex_specs,
        out_specs=[pl.BlockSpec((bm, bn), lambda i, j, k: (i, j)) for _ in out_dtypes],
        out_shape=[jax.ShapeDtypeStruct((M, N), dt) for dt in out_dtypes],
        scratch_shapes=[pltpu.VMEM((bm, bn), F32)],
        compiler_params=_cp(("parallel", "parallel", "arbitrary")),
    )(a, b, *[e for e, _ in extras])
    return res[0] if n_out == 1 else res


def _ln_fwd(h, g, b, *, name):
    T, D = h.shape
    bt = _pick(T, (512, 256, 128))

    def kern(h_ref, g_ref, b_ref, y_ref, yb_ref):
        x = h_ref[...]
        mu = jnp.mean(x, axis=-1, keepdims=True)
        xc = x - mu
        var = jnp.mean(xc * xc, axis=-1, keepdims=True)
        y = xc * lax.rsqrt(var + LN_EPS) * g_ref[...] + b_ref[...]
        y_ref[...] = y
        yb_ref[...] = y.astype(BF16)

    row = pl.BlockSpec((bt, D), lambda i: (i, 0))
    vec = pl.BlockSpec((1, D), lambda i: (0, 0))
    return pl.pallas_call(
        kern, name=name, grid=(T // bt,), in_specs=[row, vec, vec], out_specs=[row, row],
        out_shape=[jax.ShapeDtypeStruct((T, D), F32), jax.ShapeDtypeStruct((T, D), BF16)],
        compiler_params=_cp(("arbitrary",)),
    )(h, g.reshape(1, D), b.reshape(1, D))


def _ln_bwd(h, dy, g, *, name):
    T, D = h.shape
    bt = _pick(T, (512, 256, 128))

    def kern(h_ref, dy_ref, g_ref, dh_ref, dhb_ref, dg_ref, db_ref):
        i = pl.program_id(0)
        x = h_ref[...]
        dyv = dy_ref[...]
        mu = jnp.mean(x, axis=-1, keepdims=True)
        xc = x - mu
        var = jnp.mean(xc * xc, axis=-1, keepdims=True)
        rstd = lax.rsqrt(var + LN_EPS)
        xhat = xc * rstd
        dxh = dyv * g_ref[...]
        m1 = jnp.mean(dxh, axis=-1, keepdims=True)
        m2 = jnp.mean(dxh * xhat, axis=-1, keepdims=True)
        dh = rstd * (dxh - m1 - xhat * m2)
        dh_ref[...] = dh
        dhb_ref[...] = dh.astype(BF16)
        pg = jnp.sum(dyv * xhat, axis=0, keepdims=True)
        pb = jnp.sum(dyv, axis=0, keepdims=True)

        @pl.when(i == 0)
        def _():
            dg_ref[...] = pg
            db_ref[...] = pb

        @pl.when(i > 0)
        def _():
            dg_ref[...] += pg
            db_ref[...] += pb

    row = pl.BlockSpec((bt, D), lambda i: (i, 0))
    vec = pl.BlockSpec((1, D), lambda i: (0, 0))
    return pl.pallas_call(
        kern, name=name, grid=(T // bt,), in_specs=[row, row, vec], out_specs=[row, row, vec, vec],
        out_shape=[jax.ShapeDtypeStruct((T, D), F32), jax.ShapeDtypeStruct((T, D), BF16),
                   jax.ShapeDtypeStruct((1, D), F32), jax.ShapeDtypeStruct((1, D), F32)],
        compiler_params=_cp(("arbitrary",)),
    )(h, dy, g.reshape(1, D))


def _loss_head(y, tgt, *, name):
    T, D = y.shape
    bt = _pick(T, (512, 256, 128))

    def kern(y_ref, t_ref, dy_ref, loss_ref):
        i = pl.program_id(0)
        e = y_ref[...] - t_ref[...]
        dy_ref[...] = e * (1.0 / D)
        part = 0.5 * jnp.sum(jnp.sum(e * e, axis=-1, keepdims=True) * (1.0 / D), axis=0, keepdims=True)
        part = jnp.broadcast_to(part, (8, LANES))

        @pl.when(i == 0)
        def _():
            loss_ref[...] = part

        @pl.when(i > 0)
        def _():
            loss_ref[...] += part

    row = pl.BlockSpec((bt, D), lambda i: (i, 0))
    return pl.pallas_call(
        kern, name=name, grid=(T // bt,), in_specs=[row, row],
        out_specs=[row, pl.BlockSpec((8, LANES), lambda i: (0, 0))],
        out_shape=[jax.ShapeDtypeStruct((T, D), F32), jax.ShapeDtypeStruct((8, LANES), F32)],
        compiler_params=_cp(("arbitrary",)),
    )(y, tgt)


def _tri(n, upper):
    r = lax.broadcasted_iota(jnp.int32, (n, n), 0)
    c = lax.broadcasted_iota(jnp.int32, (n, n), 1)
    return jnp.where((c >= r) if upper else (c <= r), 1.0, 0.0).astype(BF16)


def _forget_fwd(ff, bf, *, name):
    T = ff.shape[0]
    bt = 256

    def kern(ff_ref, bf_ref, out_ref, carry):
        i = pl.program_id(0)

        @pl.when(i == 0)
        def _():
            carry[...] = jnp.zeros_like(carry)

        ls = _log_sigmoid(ff_ref[...] + bf_ref[...])
        tri = _tri(bt, upper=False)
        hi, mid, lo = _split3(ls)
        cs = _dot(tri, hi) + _dot(tri, mid) + _dot(tri, lo) + carry[0:1, :]
        out_ref[...] = cs
        carry[...] = jnp.broadcast_to(cs[bt - 1:bt, :], carry.shape)

    return pl.pallas_call(
        kern, name=name, grid=(T // bt,),
        in_specs=[pl.BlockSpec((bt, LANES), lambda i: (i, 0)), pl.BlockSpec((1, LANES), lambda i: (0, 0))],
        out_specs=pl.BlockSpec((bt, LANES), lambda i: (i, 0)),
        out_shape=jax.ShapeDtypeStruct((T, LANES), F32),
        scratch_shapes=[pltpu.VMEM((8, LANES), F32)],
        compiler_params=_cp(("arbitrary",)),
    )(ff, bf)


def _forget_bwd(dFk, dFq, ff, bf, *, name):
    T = ff.shape[0]
    bt = 256
    nb = T // bt

    def kern(dFk_ref, dFq_ref, ff_ref, bf_ref, dff_ref, dbf_ref, carry):
        i = pl.program_id(0)

        @pl.when(i == 0)
        def _():
            carry[...] = jnp.zeros_like(carry)
            dbf_ref[...] = jnp.zeros_like(dbf_ref)

        tri = _tri(bt, upper=True)
        hi, mid, lo = _split3(dFk_ref[...] + dFq_ref[...])
        rs = _dot(tri, hi) + _dot(tri, mid) + _dot(tri, lo) + carry[0:1, :]
        carry[...] = jnp.broadcast_to(rs[0:1, :], carry.shape)
        z = ff_ref[...] + bf_ref[...]
        dff = rs * _sigmoid(-z)
        dff_ref[...] = dff.astype(dff_ref.dtype)
        dbf_ref[...] += jnp.sum(dff, axis=0, keepdims=True)

    rev = pl.BlockSpec((bt, LANES), lambda i: (nb - 1 - i, 0))
    vec = pl.BlockSpec((1, LANES), lambda i: (0, 0))
    return pl.pallas_call(
        kern, name=name, grid=(nb,), in_specs=[rev, rev, rev, vec], out_specs=[rev, vec],
        out_shape=[jax.ShapeDtypeStruct((T, LANES), BF16), jax.ShapeDtypeStruct((1, LANES), F32)],
        scratch_shapes=[pltpu.VMEM((8, LANES), F32)],
        compiler_params=_cp(("arbitrary",)),
    )(dFk, dFq, ff, bf)


def _att_blocks(T):
    return min(ATT_BLOCK, T), min(ATT_KEYS, T)


def _positions(i, j, bq, bk):
    r = i * bq + lax.broadcasted_iota(jnp.int32, (bq, bk), 0)
    c = j * bk + lax.broadcasted_iota(jnp.int32, (bq, bk), 1)
    return r, c


def _fox_fwd(u_att, fcol, frow, *, name):
    T = u_att.shape[0]
    bq, bk = _att_blocks(T)
    nq, nk = T // bq, T // bk
    H = N_HEADS

    def kern(q_ref, k_ref, v_ref, fc_ref, fr_ref, o_ref, lse_ref):
        i = pl.program_id(1)
        q = q_ref[...]
        fq = fc_ref[...]

        def step(j, carry, masked):
            m, l, acc = carry
            off = pl.multiple_of(j * bk, bk)
            k = k_ref[pl.ds(off, bk), :]
            v = v_ref[pl.ds(off, bk), :]
            s = _dot(q, k, NT) * SCALE + (fq - fr_ref[j])
            if masked:
                r, c = _positions(i, j, bq, bk)
                s = jnp.where(c <= r, s, -jnp.inf)
            m_new = jnp.maximum(m, jnp.max(s, axis=1, keepdims=True))
            a = jnp.exp(m - m_new)
            p = jnp.exp(s - m_new)
            l = a * l + jnp.sum(p, axis=1, keepdims=True)
            acc = a * acc + _dot(p.astype(BF16), v)
            return m_new, l, acc

        init = (jnp.full((bq, 1), -1e30, F32), jnp.zeros((bq, 1), F32), jnp.zeros((bq, HEAD_DIM), F32))
        nfull = (i * bq) // bk
        carry = lax.fori_loop(0, nfull, lambda j, cr: step(j, cr, False), init)
        m, l, acc = step(nfull, carry, True)
        o_ref[...] = (acc / l).astype(o_ref.dtype)
        lse_ref[...] = m + jnp.log(l)

    return pl.pallas_call(
        kern, name=name, grid=(H, nq),
        in_specs=[pl.BlockSpec((bq, HEAD_DIM), lambda h, i: (i, h)),
                  pl.BlockSpec((T, HEAD_DIM), lambda h, i: (0, 4 + h)),
                  pl.BlockSpec((T, HEAD_DIM), lambda h, i: (0, 8 + h)),
                  pl.BlockSpec((None, bq, 1), lambda h, i: (h, i, 0)),
                  pl.BlockSpec((None, nk, 1, bk), lambda h, i: (h, 0, 0, 0))],
        out_specs=[pl.BlockSpec((bq, HEAD_DIM), lambda h, i: (i, h)),
                   pl.BlockSpec((None, bq, 1), lambda h, i: (h, i, 0))],
        out_shape=[jax.ShapeDtypeStruct((T, BRANCH_WIDTH), BF16), jax.ShapeDtypeStruct((H, T, 1), F32)],
        compiler_params=_cp(("parallel", "arbitrary")),
    )(u_att, u_att, u_att, fcol, frow)


def _row_dot(a, b, *, name):
    T = a.shape[0]
    bt = _pick(T, (512, 256, 128))

    def kern(a_ref, b_ref, o_ref):
        p = a_ref[...].astype(F32) * b_ref[...].astype(F32)
        for h in range(N_HEADS):
            o_ref[h] = jnp.sum(p[:, h * HEAD_DIM:(h + 1) * HEAD_DIM], axis=1, keepdims=True)

    row = pl.BlockSpec((bt, BRANCH_WIDTH), lambda i: (i, 0))
    return pl.pallas_call(
        kern, name=name, grid=(T // bt,), in_specs=[row, row],
        out_specs=pl.BlockSpec((N_HEADS, bt, 1), lambda i: (0, i, 0)),
        out_shape=jax.ShapeDtypeStruct((N_HEADS, T, 1), F32),
        compiler_params=_cp(("arbitrary",)),
    )(a, b)


def _fox_bwd(u_att, do, lse, delta, fcol, frow, *, name):
    T = u_att.shape[0]
    bq, bk = _att_blocks(T)
    nq, nk = T // bq, T // bk
    H = N_HEADS

    def kern(q_ref, k_ref, v_ref, do_ref, lse_ref, dl_ref, fc_ref, fr_ref,
             dq_ref, dk_ref, dv_ref, df_ref, dfq_ref, dk_acc, dv_acc, df_acc):
        i = pl.program_id(1)

        @pl.when(i == 0)
        def _():
            dk_acc[...] = jnp.zeros_like(dk_acc)
            dv_acc[...] = jnp.zeros_like(dv_acc)
            df_acc[...] = jnp.zeros_like(df_acc)

        q = q_ref[...]
        dov = do_ref[...]
        fq = fc_ref[...]
        lsev = lse_ref[...]
        dlt = dl_ref[...]

        def step(j, carry, masked):
            dq, dfq = carry
            off = pl.multiple_of(j * bk, bk)
            k = k_ref[pl.ds(off, bk), :]
            v = v_ref[pl.ds(off, bk), :]
            s = _dot(q, k, NT) * SCALE + (fq - fr_ref[j])
            p = jnp.exp(s - lsev)
            if masked:
                r, c = _positions(i, j, bq, bk)
                p = jnp.where(c <= r, p, 0.0)
            dp = _dot(dov, v, NT)
            ds = p * (dp - dlt)
            dsb = ds.astype(BF16)
            dq = dq + _dot(dsb, k)
            dk_acc[pl.ds(off, bk), :] += _dot(dsb, q, TN)
            dv_acc[pl.ds(off, bk), :] += _dot(p.astype(BF16), dov, TN)
            df_acc[j] += -jnp.sum(ds, axis=0, keepdims=True)
            return dq, dfq + jnp.sum(ds, axis=1, keepdims=True)

        nfull = (i * bq) // bk
        carry = lax.fori_loop(0, nfull, lambda j, cr: step(j, cr, False),
                              (jnp.zeros((bq, HEAD_DIM), F32), jnp.zeros((bq, 1), F32)))
        dq, dfq = step(nfull, carry, True)
        dq_ref[...] = (dq * SCALE).astype(dq_ref.dtype)
        dfq_ref[...] = dfq

        @pl.when(i == nq - 1)
        def _():
            dk_ref[...] = (dk_acc[...] * SCALE).astype(dk_ref.dtype)
            dv_ref[...] = dv_acc[...].astype(dv_ref.dtype)
            df_ref[...] = df_acc[...]

    col = lambda: pl.BlockSpec((None, bq, 1), lambda h, i: (h, i, 0))
    return pl.pallas_call(
        kern, name=name, grid=(H, nq),
        in_specs=[pl.BlockSpec((bq, HEAD_DIM), lambda h, i: (i, h)),
                  pl.BlockSpec((T, HEAD_DIM), lambda h, i: (0, 4 + h)),
                  pl.BlockSpec((T, HEAD_DIM), lambda h, i: (0, 8 + h)),
                  pl.BlockSpec((bq, HEAD_DIM), lambda h, i: (i, h)),
                  col(), col(), col(),
                  pl.BlockSpec((None, nk, 1, bk), lambda h, i: (h, 0, 0, 0))],
        out_specs=[pl.BlockSpec((bq, HEAD_DIM), lambda h, i: (i, h)),
                   pl.BlockSpec((T, HEAD_DIM), lambda h, i: (0, h)),
                   pl.BlockSpec((T, HEAD_DIM), lambda h, i: (0, h)),
                   pl.BlockSpec((None, nk, 1, bk), lambda h, i: (h, 0, 0, 0)),
                   pl.BlockSpec((None, bq, 1), lambda h, i: (h, i, 0))],
        out_shape=[jax.ShapeDtypeStruct((T, BRANCH_WIDTH), BF16)] * 3
                  + [jax.ShapeDtypeStruct((H, nk, 1, bk), F32), jax.ShapeDtypeStruct((H, T, 1), F32)],
        scratch_shapes=[pltpu.VMEM((T, HEAD_DIM), F32), pltpu.VMEM((T, HEAD_DIM), F32),
                        pltpu.VMEM((nk, 1, bk), F32)],
        compiler_params=_cp(("parallel", "arbitrary")),
    )(u_att, u_att, u_att, do, lse, delta, fcol, frow)


def _softplus_parts(z):
    t = jnp.exp(-jnp.abs(z))
    sp = jnp.maximum(z, 0.0) + jnp.log(1.0 + t)
    return t, sp


def _sb_tri(B):
    r = lax.broadcasted_iota(jnp.int32, (B, B), 0)
    c = lax.broadcasted_iota(jnp.int32, (B, B), 1)
    suffix = jnp.where(r >= c, 1.0, 0.0).astype(BF16)
    prefix = jnp.where(r <= c, 1.0, 0.0).astype(BF16)
    return suffix, prefix


def _sb_fwd(u_att, *, name):
    T = u_att.shape[0]
    B, bk = _att_blocks(T)
    nq, nsub = T // B, bk // B
    H = N_HEADS

    def kern(q_ref, k_ref, v_ref, o_ref):
        i = pl.program_id(1)
        q = q_ref[...]
        suffix, _ = _sb_tri(B)

        def step(j, carry, masked):
            run, acc = carry
            parts = []
            for s in reversed(range(nsub)):
                jb = j * nsub + s
                off = pl.multiple_of(jb * B, B)
                k = k_ref[pl.ds(off, B), :]
                z = _dot(q, k, NT) * SCALE
                _, sp = _softplus_parts(z)
                lg = -sp
                valid = None
                if masked:
                    r, c = _positions(i, jb, B, B)
                    valid = c < r
                    lg = jnp.where(valid, lg, 0.0)
                hi, lo = _split2(lg)
                cum = _dot(hi, suffix) + _dot(lo, suffix)
                parts.append((off, z, cum, jnp.sum(lg, axis=1, keepdims=True), valid))
            for off, z, cum, rs, valid in parts:
                a = jnp.exp(z + cum + run)
                if masked:
                    a = jnp.where(valid, a, 0.0)
                acc = acc + _dot(a.astype(BF16), v_ref[pl.ds(off, B), :])
                run = run + rs
            return run, acc

        nfull = (i * B) // bk
        carry = step(nfull, (jnp.zeros((B, 1), F32), jnp.zeros((B, HEAD_DIM), F32)), True)
        _, acc = lax.fori_loop(0, nfull, lambda jj, cr: step(nfull - 1 - jj, cr, False), carry)
        o_ref[...] = acc.astype(o_ref.dtype)

    return pl.pallas_call(
        kern, name=name, grid=(H, nq),
        in_specs=[pl.BlockSpec((B, HEAD_DIM), lambda h, i: (i, 12 + h)),
                  pl.BlockSpec((T, HEAD_DIM), lambda h, i: (0, 16 + h)),
                  pl.BlockSpec((T, HEAD_DIM), lambda h, i: (0, 20 + h))],
        out_specs=pl.BlockSpec((B, HEAD_DIM), lambda h, i: (i, h)),
        out_shape=jax.ShapeDtypeStruct((T, BRANCH_WIDTH), BF16),
        compiler_params=_cp(("parallel", "arbitrary")),
    )(u_att, u_att, u_att)


def _sb_bwd(u_att, do, *, name):
    T = u_att.shape[0]
    B, bk = _att_blocks(T)
    nq, nsub = T // B, bk // B
    H = N_HEADS

    def kern(q_ref, k_ref, v_ref, do_ref, dq_ref, dk_ref, dv_ref, dk_acc, dv_acc, de_s, sg_s):
        i = pl.program_id(1)

        @pl.when(i == 0)
        def _():
            dk_acc[...] = jnp.zeros_like(dk_acc)
            dv_acc[...] = jnp.zeros_like(dv_acc)

        q = q_ref[...]
        dov = do_ref[...]
        suffix, prefix = _sb_tri(B)

        def sweep1(j, run, masked):
            parts = []
            for s in reversed(range(nsub)):
                jb = j * nsub + s
                off = pl.multiple_of(jb * B, B)
                k = k_ref[pl.ds(off, B), :]
                z = _dot(q, k, NT) * SCALE
                t, sp = _softplus_parts(z)
                lg = -sp
                sg = jnp.where(z >= 0.0, 1.0, t) / (1.0 + t)
                valid = None
                if masked:
                    r, c = _positions(i, jb, B, B)
                    valid = c < r
                    lg = jnp.where(valid, lg, 0.0)
                    sg = jnp.where(valid, sg, 0.0)
                sg_s[jb] = sg.astype(sg_s.dtype)
                hi, lo = _split2(lg)
                cum = _dot(hi, suffix) + _dot(lo, suffix)
                da = _dot(dov, v_ref[pl.ds(off, B), :], NT)
                parts.append((jb, off, z, cum, da, jnp.sum(lg, axis=1, keepdims=True), valid))
            for jb, off, z, cum, da, rs, valid in parts:
                a = jnp.exp(z + cum + run)
                if masked:
                    a = jnp.where(valid, a, 0.0)
                de_s[jb] = a * da
                dv_acc[pl.ds(off, B), :] += _dot(a.astype(BF16), dov, TN)
                run = run + rs
            return run

        nfull = (i * B) // bk
        run = sweep1(nfull, jnp.zeros((B, 1), F32), True)
        lax.fori_loop(0, nfull, lambda jj, cr: sweep1(nfull - 1 - jj, cr, False), run)

        def sweep2(j, carry):
            pre, dq = carry
            parts = []
            for s in range(nsub):
                jb = j * nsub + s
                de = de_s[jb]
                hi, lo = _split2(de)
                parts.append((jb, de, _dot(hi, prefix) + _dot(lo, prefix), jnp.sum(de, axis=1, keepdims=True)))
            for jb, de, g, rs in parts:
                off = pl.multiple_of(jb * B, B)
                dz = (de - sg_s[jb].astype(F32) * (g + pre)).astype(BF16)
                dq = dq + _dot(dz, k_ref[pl.ds(off, B), :])
                dk_acc[pl.ds(off, B), :] += _dot(dz, q, TN)
                pre = pre + rs
            return pre, dq

        _, dq = lax.fori_loop(0, nfull + 1, sweep2, (jnp.zeros((B, 1), F32), jnp.zeros((B, HEAD_DIM), F32)))
        dq_ref[...] = (dq * SCALE).astype(dq_ref.dtype)

        @pl.when(i == nq - 1)
        def _():
            dk_ref[...] = (dk_acc[...] * SCALE).astype(dk_ref.dtype)
            dv_ref[...] = dv_acc[...].astype(dv_ref.dtype)

    return pl.pallas_call(
        kern, name=name, grid=(H, nq),
        in_specs=[pl.BlockSpec((B, HEAD_DIM), lambda h, i: (i, 12 + h)),
                  pl.BlockSpec((T, HEAD_DIM), lambda h, i: (0, 16 + h)),
                  pl.BlockSpec((T, HEAD_DIM), lambda h, i: (0, 20 + h)),
                  pl.BlockSpec((B, HEAD_DIM), lambda h, i: (i, h))],
        out_specs=[pl.BlockSpec((B, HEAD_DIM), lambda h, i: (i, h)),
                   pl.BlockSpec((T, HEAD_DIM), lambda h, i: (0, h)),
                   pl.BlockSpec((T, HEAD_DIM), lambda h, i: (0, h))],
        out_shape=[jax.ShapeDtypeStruct((T, BRANCH_WIDTH), BF16)] * 3,
        scratch_shapes=[pltpu.VMEM((T, HEAD_DIM), F32), pltpu.VMEM((T, HEAD_DIM), F32),
                        pltpu.VMEM((T // B, B, B), F32), pltpu.VMEM((T // B, B, B), BF16)],
        compiler_params=_cp(("parallel", "arbitrary")),
    )(u_att, u_att, u_att, do)


def _rel_onehot(qrow):
    k = lax.broadcasted_iota(jnp.int32, (BAND, REL_PAD), 0)
    rr = lax.broadcasted_iota(jnp.int32, (BAND, REL_PAD), 1)
    idx = jnp.clip(PAD_ROWS + qrow - k, -(CHUNK - 1), REL_CLIP) + (CHUNK - 1)
    return jnp.where(idx == rr, 1.0, 0.0).astype(BF16)


def _band_bias(table, *, name):
    def kern(t_ref, o_ref):
        hi, mid, lo = _split3(t_ref[...])

        def body(qrow, _):
            oh = _rel_onehot(qrow)
            o_ref[qrow] = _dot(hi, oh, NT) + _dot(mid, oh, NT) + _dot(lo, oh, NT)
            return 0

        lax.fori_loop(0, CHUNK, body, 0)

    return pl.pallas_call(
        kern, name=name, out_shape=jax.ShapeDtypeStruct((CHUNK, 8, BAND), F32),
        compiler_params=_cp(),
    )(table)


def _band_bias_bwd(dbias, *, name):
    def kern(d_ref, o_ref):
        def body(qrow, acc):
            oh = _rel_onehot(qrow)
            hi, mid, lo = _split3(d_ref[qrow])
            return acc + _dot(hi, oh) + _dot(mid, oh) + _dot(lo, oh)

        o_ref[...] = lax.fori_loop(0, CHUNK, body, jnp.zeros((8, REL_PAD), F32))

    return pl.pallas_call(
        kern, name=name, out_shape=jax.ShapeDtypeStruct((8, REL_PAD), F32),
        compiler_params=_cp(),
    )(dbias)


def _chunk_rows(T):
    return _pick(T, (512, 256, 128, 64))


def _chunk_scores(q, kw, bias, c_global):
    s = _dot(q, kw, NT) * SCALE + bias
    col = lax.broadcasted_iota(jnp.int32, (CHUNK, BAND), 1)
    valid = (c_global * CHUNK + col) >= PAD_ROWS
    s = jnp.where(valid, s, -jnp.inf)
    m = jnp.max(s, axis=1, keepdims=True)
    e = jnp.exp(s - m)
    return e / jnp.sum(e, axis=1, keepdims=True)


def _chunk_fwd(u_att, bias, *, name):
    T = u_att.shape[0]
    R = _chunk_rows(T)
    nr = T // R
    H = N_HEADS

    def kern(q_ref, k_ref, v_ref, b_ref, o_ref, kpad, vpad):
        i = pl.program_id(1)

        @pl.when(i == 0)
        def _():
            kpad[0:PAD_ROWS, :] = jnp.zeros((PAD_ROWS, HEAD_DIM), BF16)
            vpad[0:PAD_ROWS, :] = jnp.zeros((PAD_ROWS, HEAD_DIM), BF16)
            kpad[PAD_ROWS:, :] = k_ref[...]
            vpad[PAD_ROWS:, :] = v_ref[...]

        bias_v = b_ref[...]
        for cc in range(R // CHUNK):
            cg = i * (R // CHUNK) + cc
            off = pl.multiple_of(cg * CHUNK, CHUNK)
            q = q_ref[cc * CHUNK:(cc + 1) * CHUNK, :]
            kw = kpad[pl.ds(off, BAND), :]
            vw = vpad[pl.ds(off, BAND), :]
            p = _chunk_scores(q, kw, bias_v, cg)
            o_ref[cc * CHUNK:(cc + 1) * CHUNK, :] = _dot(p.astype(BF16), vw).astype(o_ref.dtype)

    return pl.pallas_call(
        kern, name=name, grid=(H, nr),
        in_specs=[pl.BlockSpec((R, HEAD_DIM), lambda h, i: (i, 24 + h)),
                  pl.BlockSpec((T, HEAD_DIM), lambda h, i: (0, 28 + h)),
                  pl.BlockSpec((T, HEAD_DIM), lambda h, i: (0, 32 + h)),
                  pl.BlockSpec((None, CHUNK, BAND), lambda h, i: (h, 0, 0))],
        out_specs=pl.BlockSpec((R, HEAD_DIM), lambda h, i: (i, h)),
        out_shape=jax.ShapeDtypeStruct((T, BRANCH_WIDTH), BF16),
        scratch_shapes=[pltpu.VMEM((T + PAD_ROWS, HEAD_DIM), BF16), pltpu.VMEM((T + PAD_ROWS, HEAD_DIM), BF16)],
        compiler_params=_cp(("parallel", "arbitrary")),
    )(u_att, u_att, u_att, bias)


def _chunk_bwd(u_att, bias, do, *, name):
    T = u_att.shape[0]
    R = _chunk_rows(T)
    nr = T // R
    H = N_HEADS

    def kern(q_ref, k_ref, v_ref, b_ref, do_ref, dq_ref, dk_ref, dv_ref, db_ref, kpad, vpad, dkp, dvp):
        i = pl.program_id(1)

        @pl.when(i == 0)
        def _():
            kpad[0:PAD_ROWS, :] = jnp.zeros((PAD_ROWS, HEAD_DIM), BF16)
            vpad[0:PAD_ROWS, :] = jnp.zeros((PAD_ROWS, HEAD_DIM), BF16)
            kpad[PAD_ROWS:, :] = k_ref[...]
            vpad[PAD_ROWS:, :] = v_ref[...]
            dkp[...] = jnp.zeros_like(dkp)
            dvp[...] = jnp.zeros_like(dvp)
            db_ref[...] = jnp.zeros_like(db_ref)

        bias_v = b_ref[...]
        for cc in range(R // CHUNK):
            cg = i * (R // CHUNK) + cc
            off = pl.multiple_of(cg * CHUNK, CHUNK)
            q = q_ref[cc * CHUNK:(cc + 1) * CHUNK, :]
            dov = do_ref[cc * CHUNK:(cc + 1) * CHUNK, :]
            kw = kpad[pl.ds(off, BAND), :]
            vw = vpad[pl.ds(off, BAND), :]
            p = _chunk_scores(q, kw, bias_v, cg)
            dp = _dot(dov, vw, NT)
            ds = p * (dp - jnp.sum(p * dp, axis=1, keepdims=True))
            dsb = ds.astype(BF16)
            dq_ref[cc * CHUNK:(cc + 1) * CHUNK, :] = (_dot(dsb, kw) * SCALE).astype(dq_ref.dtype)
            dkp[pl.ds(off, BAND), :] += _dot(dsb, q, TN)
            dvp[pl.ds(off, BAND), :] += _dot(p.astype(BF16), dov, TN)
            db_ref[...] += ds

        @pl.when(i == nr - 1)
        def _():
            dk_ref[...] = (dkp[PAD_ROWS:, :] * SCALE).astype(dk_ref.dtype)
            dv_ref[...] = dvp[PAD_ROWS:, :].astype(dv_ref.dtype)

    return pl.pallas_call(
        kern, name=name, grid=(H, nr),
        in_specs=[pl.BlockSpec((R, HEAD_DIM), lambda h, i: (i, 24 + h)),
                  pl.BlockSpec((T, HEAD_DIM), lambda h, i: (0, 28 + h)),
                  pl.BlockSpec((T, HEAD_DIM), lambda h, i: (0, 32 + h)),
                  pl.BlockSpec((None, CHUNK, BAND), lambda h, i: (h, 0, 0)),
                  pl.BlockSpec((R, HEAD_DIM), lambda h, i: (i, h))],
        out_specs=[pl.BlockSpec((R, HEAD_DIM), lambda h, i: (i, h)),
                   pl.BlockSpec((T, HEAD_DIM), lambda h, i: (0, h)),
                   pl.BlockSpec((T, HEAD_DIM), lambda h, i: (0, h)),
                   pl.BlockSpec((None, CHUNK, BAND), lambda h, i: (h, 0, 0))],
        out_shape=[jax.ShapeDtypeStruct((T, BRANCH_WIDTH), BF16)] * 3
                  + [jax.ShapeDtypeStruct((H, CHUNK, BAND), F32)],
        scratch_shapes=[pltpu.VMEM((T + PAD_ROWS, HEAD_DIM), BF16), pltpu.VMEM((T + PAD_ROWS, HEAD_DIM), BF16),
                        pltpu.VMEM((T + PAD_ROWS, HEAD_DIM), F32), pltpu.VMEM((T + PAD_ROWS, HEAD_DIM), F32)],
        compiler_params=_cp(("parallel", "arbitrary")),
    )(u_att, u_att, u_att, bias, do)


LRU_ROWS = 256
HALO = 8


def _gelu(y):
    k0 = math.sqrt(2.0 / math.pi)
    t = jnp.tanh(k0 * (y + 0.044715 * y * y * y))
    return 0.5 * y * (1.0 + t), t


def _gelu_grad(y, t):
    k0 = math.sqrt(2.0 / math.pi)
    return 0.5 * (1.0 + t) + 0.5 * y * (1.0 - t * t) * k0 * (1.0 + 3.0 * 0.044715 * y * y)


def _neg_expm1(y):
    poly = -y * (1.0 + y * (1.0 / 2 + y * (1.0 / 6 + y * (1.0 / 24 + y * (1.0 / 120 + y * (1.0 / 720 + y * (1.0 / 5040)))))))
    return jnp.where(y > -0.5, poly, 1.0 - jnp.exp(y))


def _lru_gates(ext, cw_ref, cb_ref, wr_ref, br_ref, wi_ref, bi_ref, lam_ref, rows):
    xc = cb_ref[...] + jnp.zeros((rows, BRANCH_WIDTH), F32)
    for j in range(CONV_WIDTH):
        xc = xc + ext[pl.ds(HALO - (CONV_WIDTH - 1) + j, rows), :] * cw_ref[j:j + 1, :]
    xcb = xc.astype(BF16)
    zr = jnp.concatenate([_dot(xcb[:, n * 128:(n + 1) * 128], wr_ref[n]) for n in range(4)], axis=1) + br_ref[...]
    zi = jnp.concatenate([_dot(xcb[:, n * 128:(n + 1) * 128], wi_ref[n]) for n in range(4)], axis=1) + bi_ref[...]
    r = _sigmoid(zr)
    gi = _sigmoid(zi)
    ls = _log_sigmoid(lam_ref[...])
    la = LRU_C * r * ls
    a = jnp.exp(la)
    mult = jnp.sqrt(_neg_expm1(2.0 * la))
    return xc, xcb, r, gi, ls, a, mult


def _lru_param_specs():
    full2 = lambda s: pl.BlockSpec(s, lambda i: (0, 0))
    full3 = lambda s: pl.BlockSpec(s, lambda i: (0, 0, 0))
    return [full2((8, BRANCH_WIDTH)), full2((1, BRANCH_WIDTH)), full3((4, 128, 128)), full2((1, BRANCH_WIDTH)),
            full3((4, 128, 128)), full2((1, BRANCH_WIDTH)), full2((1, BRANCH_WIDTH))]


def _lru_fwd(u_rec, p, *, name):
    T = u_rec.shape[0]
    R = min(LRU_ROWS, T)
    nb = T // R
    W = BRANCH_WIDTH
    hb = R // HALO

    def kern(rx_ref, halo_ref, ry_ref, cw_ref, cb_ref, wr_ref, br_ref, wi_ref, bi_ref, lam_ref,
             o_ref, h_ref, ext, a_s, b_s, hc):
        i = pl.program_id(0)

        @pl.when(i == 0)
        def _():
            hc[...] = jnp.zeros_like(hc)

        ext[0:HALO, :] = jnp.where(i == 0, 0.0, halo_ref[...])
        ext[HALO:, :] = rx_ref[...]
        xc, _, r, gi, ls, a, mult = _lru_gates(ext, cw_ref, cb_ref, wr_ref, br_ref, wi_ref, bi_ref, lam_ref, R)
        a_s[...] = a
        b_s[...] = mult * (gi * xc)

        def body(t, h):
            h = a_s[pl.ds(t, 1), :] * h + b_s[pl.ds(t, 1), :]
            h_ref[pl.ds(t, 1), :] = h
            return h

        h = lax.fori_loop(0, R, body, hc[0:1, :], unroll=8)
        hc[...] = jnp.broadcast_to(h, hc.shape)
        g, _ = _gelu(ry_ref[...])
        o_ref[...] = (h_ref[...] * g).astype(o_ref.dtype)

    return pl.pallas_call(
        kern, name=name, grid=(nb,),
        in_specs=[pl.BlockSpec((R, W), lambda i: (i, 0)),
                  pl.BlockSpec((HALO, W), lambda i: (jnp.maximum(i * hb - 1, 0), 0)),
                  pl.BlockSpec((R, W), lambda i: (i, 1))] + _lru_param_specs(),
        out_specs=[pl.BlockSpec((R, W), lambda i: (i, 0)), pl.BlockSpec((R, W), lambda i: (i, 0))],
        out_shape=[jax.ShapeDtypeStruct((T, W), BF16), jax.ShapeDtypeStruct((T, W), F32)],
        scratch_shapes=[pltpu.VMEM((R + HALO, W), F32), pltpu.VMEM((R, W), F32), pltpu.VMEM((R, W), F32),
                        pltpu.VMEM((8, W), F32)],
        compiler_params=_cp(("arbitrary",)),
    )(u_rec, u_rec, u_rec, *p)


def _lru_bwd(u_rec, hs, do, p, *, name):
    T = u_rec.shape[0]
    R = min(LRU_ROWS, T)
    nb = T // R
    W = BRANCH_WIDTH
    hb = R // HALO

    def kern(rx_ref, halo_ref, ry_ref, h_ref, hh_ref, do_ref, cw_ref, cb_ref, wr_ref, br_ref, wi_ref, bi_ref, lam_ref,
             drx_ref, dry_ref, dcw_ref, dcb_ref, dwr_ref, dbr_ref, dwi_ref, dbi_ref, dlam_ref,
             ext, hext, a_s, g_s, dext, gc):
        s = pl.program_id(0)
        first_block = s == nb - 1

        @pl.when(s == 0)
        def _():
            gc[...] = jnp.zeros_like(gc)
            dext[R:, :] = jnp.zeros((HALO, W), F32)
            for ref in (dcw_ref, dcb_ref, dwr_ref, dbr_ref, dwi_ref, dbi_ref, dlam_ref):
                ref[...] = jnp.zeros_like(ref)

        ext[0:HALO, :] = jnp.where(first_block, 0.0, halo_ref[...])
        ext[HALO:, :] = rx_ref[...]
        hext[0:HALO, :] = jnp.where(first_block, 0.0, hh_ref[...])
        hext[HALO:, :] = h_ref[...]
        xc, xcb, r, gi, ls, a, mult = _lru_gates(ext, cw_ref, cb_ref, wr_ref, br_ref, wi_ref, bi_ref, lam_ref, R)
        ry = ry_ref[...]
        gel, th = _gelu(ry)
        dov = do_ref[...].astype(F32)
        dry_ref[...] = (dov * h_ref[...] * _gelu_grad(ry, th)).astype(dry_ref.dtype)
        a_s[...] = a
        g_s[...] = dov * gel

        def body(tt, g):
            t = R - 1 - tt
            dh = g_s[pl.ds(t, 1), :] + g
            g_s[pl.ds(t, 1), :] = dh
            return a_s[pl.ds(t, 1), :] * dh

        g = lax.fori_loop(0, R, body, gc[0:1, :], unroll=8)
        gc[...] = jnp.broadcast_to(g, gc.shape)
        dh = g_s[...]
        hprev = hext[pl.ds(HALO - 1, R), :]
        da = dh * hprev
        gx = gi * xc
        dmult = dh * gx
        dgx = dh * mult
        dgi = dgx * xc
        dxc = dgx * gi
        dla = da * a - dmult * (a * a) / mult
        dr = dla * (LRU_C * ls)
        dlam_ref[...] += jnp.sum(dla * (LRU_C * r), axis=0, keepdims=True)
        dzr = dr * r * (1.0 - r)
        dzi = dgi * gi * (1.0 - gi)
        dbr_ref[...] += jnp.sum(dzr, axis=0, keepdims=True)
        dbi_ref[...] += jnp.sum(dzi, axis=0, keepdims=True)
        dzrb = dzr.astype(BF16)
        dzib = dzi.astype(BF16)
        back = []
        for n in range(4):
            sl = slice(n * 128, (n + 1) * 128)
            dwr_ref[n] += _dot(xcb[:, sl], dzrb[:, sl], TN)
            dwi_ref[n] += _dot(xcb[:, sl], dzib[:, sl], TN)
            back.append(_dot(dzrb[:, sl], wr_ref[n], NT) + _dot(dzib[:, sl], wi_ref[n], NT))
        dxc = dxc + jnp.concatenate(back, axis=1)
        dcb_ref[...] += jnp.sum(dxc, axis=0, keepdims=True)
        for j in range(CONV_WIDTH):
            dcw_ref[j:j + 1, :] += jnp.sum(dxc * ext[pl.ds(HALO - (CONV_WIDTH - 1) + j, R), :], axis=0, keepdims=True)
        dext[0:R, :] = dxc
        drx = jnp.zeros((R, W), F32)
        for j in range(CONV_WIDTH):
            drx = drx + dext[pl.ds(CONV_WIDTH - 1 - j, R), :] * cw_ref[j:j + 1, :]
        drx_ref[...] = drx.astype(drx_ref.dtype)
        dext[R:, :] = dxc[0:HALO, :]

        @pl.when(s == nb - 1)
        def _():
            dlam_ref[...] = dlam_ref[...] * _sigmoid(-lam_ref[...])

    rev = lambda c: pl.BlockSpec((R, W), lambda s: (nb - 1 - s, c))
    halo = lambda: pl.BlockSpec((HALO, W), lambda s: (jnp.maximum((nb - 1 - s) * hb - 1, 0), 0))
    v2 = lambda shp: pl.BlockSpec(shp, lambda s: (0, 0))
    v3 = lambda shp: pl.BlockSpec(shp, lambda s: (0, 0, 0))
    return pl.pallas_call(
        kern, name=name, grid=(nb,),
        in_specs=[rev(0), halo(), rev(1), rev(0), halo(), rev(0)] + _lru_param_specs(),
        out_specs=[rev(0), rev(0), v2((8, W)), v2((1, W)), v3((4, 128, 128)), v2((1, W)), v3((4, 128, 128)),
                   v2((1, W)), v2((1, W))],
        out_shape=[jax.ShapeDtypeStruct((T, W), BF16), jax.ShapeDtypeStruct((T, W), BF16),
                   jax.ShapeDtypeStruct((8, W), F32), jax.ShapeDtypeStruct((1, W), F32),
                   jax.ShapeDtypeStruct((4, 128, 128), F32), jax.ShapeDtypeStruct((1, W), F32),
                   jax.ShapeDtypeStruct((4, 128, 128), F32), jax.ShapeDtypeStruct((1, W), F32),
                   jax.ShapeDtypeStruct((1, W), F32)],
        scratch_shapes=[pltpu.VMEM((R + HALO, W), F32), pltpu.VMEM((R + HALO, W), F32), pltpu.VMEM((R, W), F32),
                        pltpu.VMEM((R, W), F32), pltpu.VMEM((R + HALO, W), F32), pltpu.VMEM((8, W), F32)],
        compiler_params=_cp(("arbitrary",)),
    )(u_rec, u_rec, u_rec, hs, hs, do, *p)


def _merge_fwd(o_all, wb, gate, *, name):
    T = o_all.shape[1]
    D = D_MODEL
    bm = _pick(T, (1024, 512, 256, 128))
    bn = 1024
    nj = D // bn

    def kern(o_ref, w_ref, g_ref, m_ref, pb_ref, acc):
        g = pl.program_id(2)
        pbv = _dot(o_ref[...], w_ref[...])
        pb_ref[...] = pbv.astype(pb_ref.dtype)
        term = g_ref[...].astype(F32) * pbv

        @pl.when(g == 0)
        def _():
            acc[...] = term

        @pl.when(g > 0)
        def _():
            acc[...] += term

        @pl.when(g == N_BRANCH - 1)
        def _():
            m_ref[...] = acc[...].astype(m_ref.dtype)

    return pl.pallas_call(
        kern, name=name, grid=(T // bm, nj, N_BRANCH),
        in_specs=[pl.BlockSpec((None, bm, BRANCH_WIDTH), lambda i, j, g: (g, i, 0)),
                  pl.BlockSpec((None, BRANCH_WIDTH, bn), lambda i, j, g: (g, 0, j)),
                  pl.BlockSpec((bm, bn), lambda i, j, g: (i, g * nj + j))],
        out_specs=[pl.BlockSpec((bm, bn), lambda i, j, g: (i, j)),
                   pl.BlockSpec((bm, bn), lambda i, j, g: (i, g * nj + j))],
        out_shape=[jax.ShapeDtypeStruct((T, D), BF16), jax.ShapeDtypeStruct((T, N_BRANCH * D), BF16)],
        scratch_shapes=[pltpu.VMEM((bm, bn), F32)],
        compiler_params=_cp(("parallel", "parallel", "arbitrary")),
    )(o_all, wb, gate)


def _merge_bwd(dm, gate, pb, *, name):
    T = dm.shape[0]
    D = D_MODEL
    bt = _pick(T, (256, 128))

    def kern(dm_ref, g_ref, pb_ref, dpb_ref, dzg_ref, dbg_ref):
        i = pl.program_id(1)
        dmv = dm_ref[...]
        gv = g_ref[...].astype(F32)
        dpb_ref[...] = (dmv * gv).astype(dpb_ref.dtype)
        dzg = dmv * pb_ref[...].astype(F32) * gv * (1.0 - gv)
        dzg_ref[...] = dzg.astype(dzg_ref.dtype)
        part = jnp.sum(dzg, axis=0, keepdims=True)

        @pl.when(i == 0)
        def _():
            dbg_ref[...] = part

        @pl.when(i > 0)
        def _():
            dbg_ref[...] += part

    return pl.pallas_call(
        kern, name=name, grid=(N_BRANCH, T // bt),
        in_specs=[pl.BlockSpec((bt, D), lambda g, i: (i, 0)),
                  pl.BlockSpec((bt, D), lambda g, i: (i, g)),
                  pl.BlockSpec((bt, D), lambda g, i: (i, g))],
        out_specs=[pl.BlockSpec((None, bt, D), lambda g, i: (g, i, 0)),
                   pl.BlockSpec((bt, D), lambda g, i: (i, g)),
                   pl.BlockSpec((1, D), lambda g, i: (0, g))],
        out_shape=[jax.ShapeDtypeStruct((N_BRANCH, T, D), BF16), jax.ShapeDtypeStruct((T, N_BRANCH * D), BF16),
                   jax.ShapeDtypeStruct((1, N_BRANCH * D), F32)],
        compiler_params=_cp(("parallel", "arbitrary")),
    )(dm, gate, pb)


def _pad_lanes(v, n):
    return jnp.pad(v, [(0, 0)] * (v.ndim - 1) + [(0, n - v.shape[-1])])


def _rows8(v):
    return jnp.pad(v, ((0, 8 - v.shape[0]), (0, 0)))


def _device_step(x, tgt, W):
    T = x.shape[0]
    _, bk = _att_blocks(T)
    H = N_HEADS
    G = {}
    saved = []

    xf, xb = _ln_fwd(x, W['ln_in_g'], W['ln_in_b'], name='ln_in_fwd')
    for l in range(DEPTH):
        w_att, w_rec = W['w_att'][l], W['w_rec'][l]
        u_att = _mm(xb, w_att, name='in_proj_att', out_dtypes=(BF16,))
        u_rec = _mm(xb, w_rec, name='in_proj_rec', out_dtypes=(F32,))
        ffl = u_rec[:, 2 * BRANCH_WIDTH:]
        bf = _pad_lanes(W['b_forget'][l].reshape(1, H), LANES)
        Fc = _forget_fwd(ffl, bf, name='forget_fwd')
        Fh = Fc[:, :H].T
        fcol = Fh.reshape(H, T, 1)
        frow = Fh.reshape(H, T // bk, 1, bk)
        o_fox, lse = _fox_fwd(u_att, fcol, frow, name='fox_fwd')
        lp = (_rows8(W['conv_w'][l]), W['conv_b'][l].reshape(1, -1), W['w_r'][l].astype(BF16),
              W['b_r'][l].reshape(1, -1), W['w_i'][l].astype(BF16), W['b_i'][l].reshape(1, -1),
              W['lru_lambda'][l].reshape(1, -1))
        o_lru, hs = _lru_fwd(u_rec, lp, name='lru_fwd')
        o_sb = _sb_fwd(u_att, name='sb_fwd')
        table = _rows8(_pad_lanes(W['rel_bias'][l], REL_PAD))
        bias = _band_bias(table, name='band_bias').transpose(1, 0, 2)[:H]
        o_ch = _chunk_fwd(u_att, bias, name='chunk_fwd')
        o_all = jnp.stack([o_fox, o_lru, o_sb, o_ch])
        gate = _mm(xb, W['w_gate_cat'][l], name='gate_proj', out_dtypes=(BF16,),
                   extras=[(W['b_gate'][l].reshape(1, -1), 'n')],
                   epilogue=lambda acc, b: (_sigmoid(acc + b),))
        merged, pb = _merge_fwd(o_all, W['w_branch'][l], gate, name='merge_fwd')
        h1 = _mm(merged, W['w_out'][l], name='out_proj', extras=[(xf, 'mn')],
                 epilogue=lambda acc, xr: (ALPHA * xr + acc,))
        xmf, xmb = _ln_fwd(h1, W['ln1_g'][l], W['ln1_b'][l], name='ln_fwd')
        hid, ra = _mm(xmb, W['w_ff1'][l], name='ff1', out_dtypes=(BF16, BF16),
                      epilogue=lambda acc: (jnp.square(jnp.maximum(acc, 0.0)), jnp.maximum(acc, 0.0)))
        h2 = _mm(hid, W['w_ff2'][l], name='ff2', extras=[(xmf, 'mn')],
                 epilogue=lambda acc, xr: (ALPHA * xr + acc,))
        saved.append(dict(xb=xb, u_att=u_att, u_rec=u_rec, ffl=ffl, bf=bf, fcol=fcol, frow=frow, lse=lse, lp=lp,
                          hs=hs, bias=bias, o_all=o_all, gate=gate, merged=merged, pb=pb, h1=h1, xmb=xmb,
                          hid=hid, ra=ra, h2=h2))
        xf, xb = _ln_fwd(h2, W['ln2_g'][l], W['ln2_b'][l], name='ln_fwd')

    dx, loss_tile = _loss_head(xf, tgt, name='loss_head')
    loss = loss_tile[0, 0]

    for l in reversed(range(DEPTH)):
        S = saved[l]
        dh2, dh2b, G[('ln2_g', l)], G[('ln2_b', l)] = _ln_bwd(S['h2'], dx, W['ln2_g'][l], name='ln_bwd')
        da = _mm(dh2b, W['w_ff2'][l], tb=True, name='ff2_dx', out_dtypes=(BF16,), extras=[(S['ra'], 'mn')],
                 epilogue=lambda acc, rav: (acc * (2.0 * rav.astype(F32)),))
        G[('w_ff2', l)] = _mm(S['hid'], dh2b, ta=True, name='ff2_dw')
        G[('w_ff1', l)] = _mm(S['xmb'], da, ta=True, name='ff1_dw')
        dxm = _mm(da, W['w_ff1'][l], tb=True, name='ff1_dx', extras=[(dh2, 'mn')],
                  epilogue=lambda acc, d: (ALPHA * d + acc,))
        dh1, dh1b, G[('ln1_g', l)], G[('ln1_b', l)] = _ln_bwd(S['h1'], dxm, W['ln1_g'][l], name='ln_bwd')
        dm = _mm(dh1b, W['w_out'][l], tb=True, name='out_dx')
        G[('w_out', l)] = _mm(S['merged'], dh1b, ta=True, name='out_dw')
        dpb, dzg, G[('b_gate', l)] = _merge_bwd(dm, S['gate'], S['pb'], name='merge_bwd')
        do = [_mm(dpb[g], W['w_branch'][l][g], tb=True, name='branch_dx', out_dtypes=(BF16,)) for g in range(N_BRANCH)]
        G[('w_branch', l)] = jnp.stack(
            [_mm(S['o_all'][g], dpb[g], ta=True, name='branch_dw') for g in range(N_BRANCH)])
        G[('w_gate_cat', l)] = _mm(S['xb'], dzg, ta=True, name='gate_dw')
        u_att, u_rec = S['u_att'], S['u_rec']
        delta = _row_dot(do[0], S['o_all'][0], name='row_dot')
        fdq, fdk, fdv, dfk, dfq = _fox_bwd(u_att, do[0], S['lse'], delta, S['fcol'], S['frow'], name='fox_bwd')
        dff, dbf = _forget_bwd(_pad_lanes(dfk.reshape(H, T).T, LANES), _pad_lanes(dfq.reshape(H, T).T, LANES),
                               S['ffl'], S['bf'], name='forget_bwd')
        G[('b_forget', l)] = dbf[0, :H]
        (drx, dry, dcw, dcb, G[('w_r', l)], dbr, G[('w_i', l)], dbi, dlam) = _lru_bwd(
            u_rec, S['hs'], do[1], S['lp'], name='lru_bwd')
        G[('conv_w', l)], G[('conv_b', l)] = dcw[:CONV_WIDTH], dcb[0]
        G[('b_r', l)], G[('b_i', l)], G[('lru_lambda', l)] = dbr[0], dbi[0], dlam[0]
        sdq, sdk, sdv = _sb_bwd(u_att, do[2], name='sb_bwd')
        cdq, cdk, cdv, dbias = _chunk_bwd(u_att, S['bias'], do[3], name='chunk_bwd')
        dtab = _band_bias_bwd(jnp.pad(dbias, ((0, 8 - H), (0, 0), (0, 0))).transpose(1, 0, 2), name='band_bias_bwd')
        G[('rel_bias', l)] = dtab[:H, :REL_TABLE]
        du_att = jnp.concatenate([fdq, fdk, fdv, sdq, sdk, sdv, cdq, cdk, cdv], axis=1)
        du_rec = jnp.concatenate([drx, dry, dff], axis=1)
        G[('w_att', l)] = _mm(S['xb'], du_att, ta=True, name='in_att_dw')
        G[('w_rec', l)] = _mm(S['xb'], du_rec, ta=True, name='in_rec_dw')
        t1 = _mm(dzg, W['w_gate_cat'][l], tb=True, name='gate_dx', extras=[(dh1, 'mn')],
                 epilogue=lambda acc, d: (ALPHA * d + acc,))
        t2 = _mm(du_att, W['w_att'][l], tb=True, name='in_att_dx', extras=[(t1, 'mn')],
                 epilogue=lambda acc, d: (d + acc,))
        dx = _mm(du_rec, W['w_rec'][l], tb=True, name='in_rec_dx', extras=[(t2, 'mn')],
                 epilogue=lambda acc, d: (d + acc,))

    gx, _, G[('ln_in_g', -1)], G[('ln_in_b', -1)] = _ln_bwd(x, dx, W['ln_in_g'], name='ln_in_bwd')
    return loss, gx, G


_IN_FQKV = (0, 1536)
_IN_FF = (1536, 1540)
_IN_REC = (1540, 2564)
_IN_REST = (2564, D_IN)


def _prep_weights(full):
    w_in = full['w_in']
    L = w_in.shape[0]
    W = dict(full)
    W['w_att'] = jnp.concatenate([w_in[..., _IN_FQKV[0]:_IN_FQKV[1]], w_in[..., _IN_REST[0]:_IN_REST[1]]], -1).astype(BF16)
    W['w_rec'] = jnp.concatenate([w_in[..., _IN_REC[0]:_IN_REC[1]], w_in[..., _IN_FF[0]:_IN_FF[1]],
                                  jnp.zeros((L, D_MODEL, N_REC - 1024 - N_HEADS), w_in.dtype)], -1).astype(BF16)
    W['w_gate_cat'] = full['w_gate'].transpose(0, 2, 1, 3).reshape(L, D_MODEL, N_BRANCH * D_MODEL).astype(BF16)
    W['b_gate'] = full['b_gate'].reshape(L, N_BRANCH * D_MODEL)
    for n in ('w_branch', 'w_out', 'w_ff1', 'w_ff2'):
        W[n] = full[n].astype(BF16)
    return W


def _grads_to_reference_layout(G):
    out = {'ln_in_g': G[('ln_in_g', -1)][0], 'ln_in_b': G[('ln_in_b', -1)][0]}
    st = lambda n: jnp.stack([G[(n, l)] for l in range(DEPTH)])
    g_att, g_rec = st('w_att'), st('w_rec')
    out['w_in'] = jnp.concatenate([g_att[..., :1536], g_rec[..., 1024:1024 + N_HEADS], g_rec[..., :1024],
                                   g_att[..., 1536:]], -1)
    out['w_gate'] = st('w_gate_cat').reshape(DEPTH, D_MODEL, N_BRANCH, D_MODEL).transpose(0, 2, 1, 3)
    out['b_gate'] = st('b_gate').reshape(DEPTH, N_BRANCH, D_MODEL)
    for n in ('ln1_g', 'ln1_b', 'ln2_g', 'ln2_b'):
        out[n] = st(n)[:, 0]
    for n in ('b_forget', 'conv_w', 'conv_b', 'w_r', 'b_r', 'w_i', 'b_i', 'lru_lambda', 'rel_bias', 'w_branch',
              'w_out', 'w_ff1', 'w_ff2'):
        out[n] = st(n)
    return out


HBM_SPEC = pl.BlockSpec(memory_space=pl.ANY)
N_CHIPS = 4
PACK_COLS = 1024


def _place():
    x, y, c = lax.axis_index("x"), lax.axis_index("y"), lax.axis_index("c")
    chips = [(1 - x, y), (x, 1 - y), (1 - x, 1 - y)]
    return x, y, c, chips


def _remote(src, dst, send_sems, recv_sems, k, to):
    return pltpu.make_async_remote_copy(src_ref=src, dst_ref=dst, send_sem=send_sems.at[k], recv_sem=recv_sems.at[k],
                                        device_id=to, device_id_type=MESH)


def _gather_layers(params, *, name):
    n = len(params)

    def body(*refs):
        ins, outs = refs[:n], refs[n:2 * n]
        send_sems, recv_sems, local_sems = refs[2 * n:]
        x, y, c, chips = _place()
        me, sibling, k = (x, y, c), (x, y, 1 - c), 2 * x + y
        local = [pltpu.make_async_copy(ins[p], outs[p].at[k], local_sems.at[p]) for p in range(n)]
        for cp in local:
            cp.start()
        first, passed = [], []
        for p in range(n):
            for j, (cx, cy) in enumerate(chips):
                cp = _remote(ins[p].at[c], outs[p].at[k, c], send_sems, recv_sems, 6 * p + j, (cx, cy, c))
                cp.start()
                first.append(cp)
        for p in range(n):
            for j, (cx, cy) in enumerate(chips):
                blk = outs[p].at[2 * cx + cy, c]
                _remote(blk, blk, send_sems, recv_sems, 6 * p + j, me).wait_recv()
                cp = _remote(blk, blk, send_sems, recv_sems, 6 * p + 3 + j, sibling)
                cp.start()
                passed.append(cp)
        for p in range(n):
            for j, (cx, cy) in enumerate(chips):
                blk = outs[p].at[2 * cx + cy, 1 - c]
                _remote(blk, blk, send_sems, recv_sems, 6 * p + 3 + j, me).wait_recv()
        for cp in first + passed:
            cp.wait_send()
        for cp in local:
            cp.wait()

    return pl.pallas_call(
        body, name=name, in_specs=[HBM_SPEC] * n, out_specs=[HBM_SPEC] * n,
        out_shape=[jax.ShapeDtypeStruct((N_CHIPS,) + a.shape, a.dtype) for a in params],
        scratch_shapes=[pltpu.SemaphoreType.DMA((6 * n,)), pltpu.SemaphoreType.DMA((6 * n,)),
                        pltpu.SemaphoreType.DMA((n,))],
    )(*params)


def _pair_exchange(g, *, name):
    _, _, R, C = g.shape

    def body(g_ref, out_ref, send_sems, recv_sems):
        x, y, c, _ = _place()
        cps = [_remote(g_ref.at[k, 1 - c], out_ref.at[k], send_sems, recv_sems, k, (x, y, 1 - c))
               for k in range(N_CHIPS)]
        for cp in cps:
            cp.start()
        for cp in cps:
            cp.wait()

    return pl.pallas_call(
        body, name=name, in_specs=[HBM_SPEC], out_specs=HBM_SPEC,
        out_shape=jax.ShapeDtypeStruct((N_CHIPS, R, C), g.dtype),
        scratch_shapes=[pltpu.SemaphoreType.DMA((N_CHIPS,)), pltpu.SemaphoreType.DMA((N_CHIPS,))],
    )(g)


def _chip_exchange(s, *, name):
    _, R, C = s.shape

    def body(s_ref, out_ref, send_sems, recv_sems, local_sem):
        x, y, c, chips = _place()
        k = 2 * x + y
        local = pltpu.make_async_copy(s_ref.at[k], out_ref.at[k], local_sem)
        local.start()
        cps = [_remote(s_ref.at[2 * cx + cy], out_ref.at[k], send_sems, recv_sems, j, (cx, cy, c))
               for j, (cx, cy) in enumerate(chips)]
        for cp in cps:
            cp.start()
        for j, (cx, cy) in enumerate(chips):
            slot = out_ref.at[2 * cx + cy]
            _remote(slot, slot, send_sems, recv_sems, j, (x, y, c)).wait_recv()
        for cp in cps:
            cp.wait_send()
        local.wait()

    return pl.pallas_call(
        body, name=name, in_specs=[HBM_SPEC], out_specs=HBM_SPEC,
        out_shape=jax.ShapeDtypeStruct((N_CHIPS, R, C), s.dtype),
        scratch_shapes=[pltpu.SemaphoreType.DMA((3,)), pltpu.SemaphoreType.DMA((3,)), pltpu.SemaphoreType.DMA],
    )(s)


def _pair_allgather(r, *, name):
    R, C = r.shape

    def body(r_ref, out_ref, send_sems, recv_sems, local_sem):
        x, y, c, _ = _place()
        local = pltpu.make_async_copy(r_ref, out_ref.at[c], local_sem)
        local.start()
        cp = _remote(r_ref, out_ref.at[c], send_sems, recv_sems, 0, (x, y, 1 - c))
        cp.start()
        other = out_ref.at[1 - c]
        _remote(other, other, send_sems, recv_sems, 0, (x, y, c)).wait_recv()
        cp.wait_send()
        local.wait()

    return pl.pallas_call(
        body, name=name, in_specs=[HBM_SPEC], out_specs=HBM_SPEC,
        out_shape=jax.ShapeDtypeStruct((2, R, C), r.dtype),
        scratch_shapes=[pltpu.SemaphoreType.DMA((1,)), pltpu.SemaphoreType.DMA((1,)), pltpu.SemaphoreType.DMA],
    )(r)


def _gather8(v, *, name):
    R, C = v.shape
    flips = [(bx, by, bc) for bx in (0, 1) for by in (0, 1) for bc in (0, 1)][1:]

    def body(v_ref, out_ref, send_sems, recv_sems, local_sem):
        x, y, c, _ = _place()
        flip = lambda a, b: 1 - a if b else a
        mine = out_ref.at[4 * x + 2 * y + c]
        local = pltpu.make_async_copy(v_ref, mine, local_sem)
        local.start()
        peers = [(flip(x, bx), flip(y, by), flip(c, bc)) for bx, by, bc in flips]
        cps = [_remote(v_ref, mine, send_sems, recv_sems, j, peer) for j, peer in enumerate(peers)]
        for cp in cps:
            cp.start()
        for j, (px, py, pc) in enumerate(peers):
            slot = out_ref.at[4 * px + 2 * py + pc]
            _remote(slot, slot, send_sems, recv_sems, j, (x, y, c)).wait_recv()
        for cp in cps:
            cp.wait_send()
        local.wait()

    return pl.pallas_call(
        body, name=name, in_specs=[HBM_SPEC], out_specs=HBM_SPEC,
        out_shape=jax.ShapeDtypeStruct((8, R, C), v.dtype),
        scratch_shapes=[pltpu.SemaphoreType.DMA((7,)), pltpu.SemaphoreType.DMA((7,)), pltpu.SemaphoreType.DMA],
    )(v)


def _row_block(rows, cols, limit=256 * 1024):
    if rows * cols <= limit:
        return rows
    for br in range(limit // cols // 8 * 8, 0, -8):
        if rows % br == 0:
            return br
    return rows


def _sum_slots(buf, *, name):
    n, R, C = buf.shape
    br = _row_block(R, C)

    def kern(b_ref, o_ref):
        acc = b_ref[0].astype(F32)
        for s in range(1, n):
            acc = acc + b_ref[s].astype(F32)
        o_ref[...] = acc

    return pl.pallas_call(
        kern, name=name, grid=(pl.cdiv(R, br),),
        in_specs=[pl.BlockSpec((n, br, C), lambda i: (0, i, 0))],
        out_specs=pl.BlockSpec((br, C), lambda i: (i, 0)),
        out_shape=jax.ShapeDtypeStruct((R, C), F32),
        compiler_params=_cp(("arbitrary",)),
    )(buf)


def _sum_pair(g, other, c, *, name):
    _, _, R, C = g.shape
    br = _row_block(R, C)

    def kern(c_ref, g_ref, o_ref, out_ref):
        out_ref[...] = (g_ref[...] + o_ref[...]).astype(out_ref.dtype)

    return pl.pallas_call(
        kern, name=name,
        grid_spec=pltpu.PrefetchScalarGridSpec(
            num_scalar_prefetch=1, grid=(N_CHIPS, pl.cdiv(R, br)),
            in_specs=[pl.BlockSpec((None, None, br, C), lambda k, i, cr: (k, cr[0], i, 0)),
                      pl.BlockSpec((None, br, C), lambda k, i, cr: (k, i, 0))],
            out_specs=pl.BlockSpec((None, br, C), lambda k, i, cr: (k, i, 0))),
        out_shape=jax.ShapeDtypeStruct((N_CHIPS, R, C), BF16),
        compiler_params=_cp(("arbitrary", "arbitrary")),
    )(c.reshape(1).astype(jnp.int32), g, other)


def _adamw(w, g, m, v, *, name):
    shape = w.shape
    cols = shape[-1]
    w2, g2, m2, v2 = (a.reshape(-1, cols) for a in (w, g, m, v))
    rows = w2.shape[0]
    br = _row_block(rows, cols)

    def kern(w_ref, g_ref, m_ref, v_ref, d_ref, nm_ref, nv_ref):
        gv = g_ref[...]
        nm = ADAM_B1 * m_ref[...] + (1.0 - ADAM_B1) * gv
        nv = ADAM_B2 * v_ref[...] + (1.0 - ADAM_B2) * jnp.square(gv)
        m_hat = nm / (1.0 - ADAM_B1 ** ADAM_STEP)
        v_hat = nv / (1.0 - ADAM_B2 ** ADAM_STEP)
        d_ref[...] = -ADAM_LR * (m_hat / (jnp.sqrt(v_hat) + ADAM_EPS) + ADAM_WD * w_ref[...])
        nm_ref[...] = nm
        nv_ref[...] = nv

    spec = pl.BlockSpec((br, cols), lambda i: (i, 0))
    outs = pl.pallas_call(
        kern, name=name, grid=(rows // br,), in_specs=[spec] * 4, out_specs=[spec] * 3,
        out_shape=[jax.ShapeDtypeStruct((rows, cols), F32)] * 3,
        compiler_params=_cp(("arbitrary",)),
    )(w2, g2, m2, v2)
    return [o.reshape(shape) for o in outs]


_NAMES = ['ln_in_g', 'ln_in_b', 'w_in', 'b_forget', 'conv_w', 'conv_b', 'w_r', 'b_r', 'w_i', 'b_i', 'lru_lambda',
          'rel_bias', 'w_branch', 'w_gate', 'b_gate', 'w_out', 'ln1_g', 'ln1_b', 'w_ff1', 'w_ff2', 'ln2_g', 'ln2_b']
_BIG = {'w_in': 2, 'w_branch': 3, 'w_gate': 2, 'w_out': 1, 'w_ff1': 2, 'w_ff2': 1}
_SMALL_SHARDED = {'b_gate': 2, 'conv_w': 2, 'rel_bias': 2}
_SHARDED = {**_BIG, **_SMALL_SHARDED}
_REPLICATED = [n for n in _NAMES if n not in _SHARDED]
_TILE = 8 * LANES


def _tiles(a, cols):
    flat = a.reshape(-1)
    per = 8 * cols
    flat = jnp.pad(flat, (0, (-flat.shape[0]) % per))
    return flat.reshape(-1, cols)


def _pack(arrs, cols):
    return jnp.concatenate([_tiles(a, cols) for a in arrs], axis=0)


def _unpack(packed, like, cols):
    out, r0 = [], 0
    for a in like:
        n = math.prod(a.shape)
        rows = -(-n // (8 * cols)) * 8
        out.append(packed[r0:r0 + rows].reshape(-1)[:n].reshape(a.shape))
        r0 += rows
    return out


def _unshard(blocks, axis):
    return jnp.concatenate([blocks[k] for k in range(N_CHIPS)], axis=axis)


def kernel(x, ln_in_g, ln_in_b, w_in, b_forget, conv_w, conv_b, w_r, b_r, w_i, b_i, lru_lambda, rel_bias, w_branch, w_gate, b_gate, w_out, ln1_g, ln1_b, w_ff1, w_ff2, ln2_g, ln2_b, loss_target, m_ln_in_g, m_ln_in_b, m_w_in, m_b_forget, m_conv_w, m_conv_b, m_w_r, m_b_r, m_w_i, m_b_i, m_lru_lambda, m_rel_bias, m_w_branch, m_w_gate, m_b_gate, m_w_out, m_ln1_g, m_ln1_b, m_w_ff1, m_w_ff2, m_ln2_g, m_ln2_b, v_ln_in_g, v_ln_in_b, v_w_in, v_b_forget, v_conv_w, v_conv_b, v_w_r, v_b_r, v_w_i, v_b_i, v_lru_lambda, v_rel_bias, v_w_branch, v_w_gate, v_b_gate, v_w_out, v_ln1_g, v_ln1_b, v_w_ff1, v_w_ff2, v_ln2_g, v_ln2_b):
    w = dict(zip(_NAMES, (ln_in_g, ln_in_b, w_in, b_forget, conv_w, conv_b, w_r, b_r, w_i, b_i, lru_lambda, rel_bias,
                          w_branch, w_gate, b_gate, w_out, ln1_g, ln1_b, w_ff1, w_ff2, ln2_g, ln2_b)))
    m = dict(zip(_NAMES, (m_ln_in_g, m_ln_in_b, m_w_in, m_b_forget, m_conv_w, m_conv_b, m_w_r, m_b_r, m_w_i, m_b_i,
                          m_lru_lambda, m_rel_bias, m_w_branch, m_w_gate, m_b_gate, m_w_out, m_ln1_g, m_ln1_b,
                          m_w_ff1, m_w_ff2, m_ln2_g, m_ln2_b)))
    v = dict(zip(_NAMES, (v_ln_in_g, v_ln_in_b, v_w_in, v_b_forget, v_conv_w, v_conv_b, v_w_r, v_b_r, v_w_i, v_b_i,
                          v_lru_lambda, v_rel_bias, v_w_branch, v_w_gate, v_b_gate, v_w_out, v_ln1_g, v_ln1_b,
                          v_w_ff1, v_w_ff2, v_ln2_g, v_ln2_b)))
    c = lax.axis_index("c")

    small_like = [w[n] for n in _SMALL_SHARDED]
    small_pack = jnp.stack([_pack([a[l] for a in small_like], LANES) for l in range(DEPTH)])
    gathered = _gather_layers([w[n].astype(BF16) for n in _BIG] + [small_pack], name='gather_weights')
    full = {n: w[n] for n in _REPLICATED}
    for n, blocks in zip(_BIG, gathered):
        full[n] = _unshard(blocks, _BIG[n])
    small_blocks = [[_unpack(gathered[-1][k, l], [a[l] for a in small_like], LANES) for l in range(DEPTH)]
                    for k in range(N_CHIPS)]
    for i, n in enumerate(_SMALL_SHARDED):
        full[n] = jnp.concatenate([jnp.stack([small_blocks[k][l][i] for l in range(DEPTH)])
                                   for k in range(N_CHIPS)], axis=_SMALL_SHARDED[n])

    loss, gx, G = _device_step(x[0], loss_target[0], _prep_weights(full))
    g_full = _grads_to_reference_layout(G)

    def blocks_of(n):
        return jnp.stack(jnp.split(g_full[n], N_CHIPS, axis=_SHARDED[n]))

    shard_like = [w[n] for n in _SHARDED]
    packed = jnp.concatenate(
        [jnp.stack([_tiles(blocks_of(n)[k], PACK_COLS) for k in range(N_CHIPS)]) for n in _SHARDED], axis=1)
    rows = packed.shape[1]
    packed = jnp.pad(packed, ((0, 0), (0, (-rows) % 16), (0, 0)))
    half = packed.shape[1] // 2
    packed = packed.reshape(N_CHIPS, 2, half, PACK_COLS)
    from_sibling = _pair_exchange(packed, name='grad_pair_exchange')
    pair_sum = _sum_pair(packed, from_sibling, c, name='grad_pair_sum')
    from_chips = _chip_exchange(pair_sum, name='grad_chip_exchange')
    mine = _sum_slots(from_chips, name='grad_chip_sum')
    reduced = _pair_allgather(mine, name='grad_pair_allgather').reshape(2 * half, PACK_COLS)
    g_shard = dict(zip(_SHARDED, _unpack(reduced, shard_like, PACK_COLS)))

    rep_like = [w[n] for n in _REPLICATED]
    rep_all = _gather8(_pack([g_full[n] for n in _REPLICATED], LANES), name='grad_gather8')
    g_rep = dict(zip(_REPLICATED, _unpack(_sum_slots(rep_all, name='grad_sum8'), rep_like, LANES)))

    grads, delta, new_m, new_v = {}, {}, {}, {}
    for n in _BIG:
        grads[n] = g_shard[n]
        delta[n], new_m[n], new_v[n] = _adamw(w[n], grads[n], m[n], v[n], name='adamw')
    small = _REPLICATED + list(_SMALL_SHARDED)
    for n in small:
        grads[n] = g_rep[n] if n in g_rep else g_shard[n]
    packs = [_pack([d[n] for n in small], LANES) for d in (w, grads, m, v)]
    outs = _adamw(*packs, name='adamw_small')
    small_like_all = [w[n] for n in small]
    for d, o in zip((delta, new_m, new_v), outs):
        d.update(zip(small, _unpack(o, small_like_all, LANES)))

    loss = lax.psum(loss, ("x", "y", "c"))
    return (loss, gx[None], *[grads[n] for n in _NAMES], *[delta[n] for n in _NAMES],
            *[new_m[n] for n in _NAMES], *[new_v[n] for n in _NAMES])
```

```python
import functools
import math

import jax
import jax.numpy as jnp
from jax import lax
from jax.experimental import pallas as pl
from jax.experimental.pallas import tpu as pltpu

F32 = jnp.float32
BF16 = jnp.bfloat16

D_MODEL = 2048
DEPTH = 2
CHUNK = 64
HEAD_DIM = 128
N_BRANCH = 4
BRANCH_WIDTH = 512
N_HEADS = 4
CONV_WIDTH = 4
LRU_C = 8.0
LOOKBACK_CHUNKS = 8
BAND = (LOOKBACK_CHUNKS + 1) * CHUNK
PAD_ROWS = LOOKBACK_CHUNKS * CHUNK
REL_CLIP = 256
REL_TABLE = REL_CLIP + CHUNK
REL_PAD = 384
D_FF = 4 * D_MODEL
D_IN = 5636
ALPHA = (2.0 * DEPTH) ** 0.25
LN_EPS = 1e-5
SCALE = HEAD_DIM ** -0.5

ADAM_LR = 0.001
ADAM_B1 = 0.9
ADAM_B2 = 0.999
ADAM_EPS = 1e-08
ADAM_WD = 0.01
ADAM_STEP = 10

N_ATT = 9 * BRANCH_WIDTH
N_REC = 2 * BRANCH_WIDTH + 128

V7X_VMEM_LIMIT = 56 * 1024 * 1024
LANES = 128
ATT_BLOCK = 256
ATT_KEYS = 1024

NT = (((1,), (1,)), ((), ()))
TN = (((0,), (0,)), ((), ()))
NN = (((1,), (0,)), ((), ()))

MESH = pl.DeviceIdType.MESH


def _cp(sem=None):
    return pltpu.CompilerParams(dimension_semantics=sem, vmem_limit_bytes=V7X_VMEM_LIMIT)


def _dot(a, b, dims=NN):
    return lax.dot_general(a, b, dims, preferred_element_type=F32)


def _pick(n, prefs):
    for p in prefs:
        if n % p == 0:
            return p
    return n


def _split3(x):
    hi = x.astype(BF16)
    r1 = x - hi.astype(F32)
    mid = r1.astype(BF16)
    lo = (r1 - mid.astype(F32)).astype(BF16)
    return hi, mid, lo


def _split2(x):
    hi = x.astype(BF16)
    lo = (x - hi.astype(F32)).astype(BF16)
    return hi, lo


def _sigmoid(z):
    return 1.0 / (1.0 + jnp.exp(-z))


def _log_sigmoid(z):
    return jnp.minimum(z, 0.0) - jnp.log(1.0 + jnp.exp(-jnp.abs(z)))


def _mm(a, b, *, name, ta=False, tb=False, out_dtypes=(F32,), epilogue=None, extras=(),
        bm=None, bn=None, bk=None, out_map=None):
    M, K = (a.shape[1], a.shape[0]) if ta else a.shape
    N = b.shape[0] if tb else b.shape[1]
    bm = bm or _pick(M, (1024, 512, 256, 128))
    bn = bn or _pick(N, (1024, 1152, 512, 256, 128))
    bk = bk or _pick(K, (2048, 1536, 1024, 1152, 512, 256, 128))
    nk = K // bk
    a_spec = pl.BlockSpec((bk, bm), lambda i, j, k: (k, i)) if ta else pl.BlockSpec((bm, bk), lambda i, j, k: (i, k))
    b_spec = pl.BlockSpec((bn, bk), lambda i, j, k: (j, k)) if tb else pl.BlockSpec((bk, bn), lambda i, j, k: (k, j))
    ex_specs = [pl.BlockSpec((bm, bn), lambda i, j, k: (i, j)) if kind == 'mn'
                else pl.BlockSpec((1, bn), lambda i, j, k: (0, j)) for _, kind in extras]
    n_ex, n_out = len(extras), len(out_dtypes)
    dims = TN if ta else (NT if tb else NN)

    def kern(*refs):
        a_ref, b_ref = refs[0], refs[1]
        ex_refs = refs[2:2 + n_ex]
        out_refs = refs[2 + n_ex:2 + n_ex + n_out]
        acc_ref = refs[-1]
        k = pl.program_id(2)
        part = _dot(a_ref[...].astype(BF16), b_ref[...].astype(BF16), dims)

        @pl.when(k == 0)
        def _():
            acc_ref[...] = part

        @pl.when(k > 0)
        def _():
            acc_ref[...] += part

        @pl.when(k == nk - 1)
        def _():
            acc = acc_ref[...]
            outs = (acc,) if epilogue is None else epilogue(acc, *[r[...] for r in ex_refs])
            for o_ref, o in zip(out_refs, outs):
                o_ref[...] = o.astype(o_ref.dtype)

    if out_map is None:
        out_specs = [pl.BlockSpec((bm, bn), lambda i, j, k: (i, j)) for _ in out_dtypes]
        out_shape = [jax.ShapeDtypeStruct((M, N), dt) for dt in out_dtypes]
    else:
        shape, block, index = out_map
        out_specs = [pl.BlockSpec(block, lambda i, j, k: index(i, j))]
        out_shape = [jax.ShapeDtypeStruct(shape, out_dtypes[0])]
    res = pl.pallas_call(
        kern, name=name, grid=(M // bm, N // bn, nk),
        in_specs=[a_spec, b_spec] + ex_specs,
        out_specs=out_specs,
        out_shape=out_shape,
        scratch_shapes=[pltpu.VMEM((bm, bn), F32)],
        compiler_params=_cp(("parallel", "parallel", "arbitrary")),
    )(a, b, *[e for e, _ in extras])
    return res[0] if n_out == 1 else res


def _ln_fwd(h, g, b, *, name):
    T, D = h.shape
    bt = _pick(T, (512, 256, 128))

    def kern(h_ref, g_ref, b_ref, y_ref, yb_ref):
        x = h_ref[...]
        mu = jnp.mean(x, axis=-1, keepdims=True)
        xc = x - mu
        var = jnp.mean(xc * xc, axis=-1, keepdims=True)
        y = xc * lax.rsqrt(var + LN_EPS) * g_ref[...] + b_ref[...]
        y_ref[...] = y
        yb_ref[...] = y.astype(BF16)

    row = pl.BlockSpec((bt, D), lambda i: (i, 0))
    vec = pl.BlockSpec((1, D), lambda i: (0, 0))
    return pl.pallas_call(
        kern, name=name, grid=(T // bt,), in_specs=[row, vec, vec], out_specs=[row, row],
        out_shape=[jax.ShapeDtypeStruct((T, D), F32), jax.ShapeDtypeStruct((T, D), BF16)],
        compiler_params=_cp(("arbitrary",)),
    )(h, g.reshape(1, D), b.reshape(1, D))


def _ln_bwd(h, dy, g, *, name):
    T, D = h.shape
    bt = _pick(T, (512, 256, 128))

    def kern(h_ref, dy_ref, g_ref, dh_ref, dhb_ref, dg_ref, db_ref):
        i = pl.program_id(0)
        x = h_ref[...]
        dyv = dy_ref[...]
        mu = jnp.mean(x, axis=-1, keepdims=True)
        xc = x - mu
        var = jnp.mean(xc * xc, axis=-1, keepdims=True)
        rstd = lax.rsqrt(var + LN_EPS)
        xhat = xc * rstd
        dxh = dyv * g_ref[...]
        m1 = jnp.mean(dxh, axis=-1, keepdims=True)
        m2 = jnp.mean(dxh * xhat, axis=-1, keepdims=True)
        dh = rstd * (dxh - m1 - xhat * m2)
        dh_ref[...] = dh
        dhb_ref[...] = dh.astype(BF16)
        pg = jnp.sum(dyv * xhat, axis=0, keepdims=True)
        pb = jnp.sum(dyv, axis=0, keepdims=True)

        @pl.when(i == 0)
        def _():
            dg_ref[...] = pg
            db_ref[...] = pb

        @pl.when(i > 0)
        def _():
            dg_ref[...] += pg
            db_ref[...] += pb

    row = pl.BlockSpec((bt, D), lambda i: (i, 0))
    vec = pl.BlockSpec((1, D), lambda i: (0, 0))
    return pl.pallas_call(
        kern, name=name, grid=(T // bt,), in_specs=[row, row, vec], out_specs=[row, row, vec, vec],
        out_shape=[jax.ShapeDtypeStruct((T, D), F32), jax.ShapeDtypeStruct((T, D), BF16),
                   jax.ShapeDtypeStruct((1, D), F32), jax.ShapeDtypeStruct((1, D), F32)],
        compiler_params=_cp(("arbitrary",)),
    )(h, dy, g.reshape(1, D))


def _loss_head(y, tgt, *, name):
    T, D = y.shape
    bt = _pick(T, (512, 256, 128))

    def kern(y_ref, t_ref, dy_ref, loss_ref):
        i = pl.program_id(0)
        e = y_ref[...] - t_ref[...]
        dy_ref[...] = e * (1.0 / D)
        part = 0.5 * jnp.sum(jnp.sum(e * e, axis=-1, keepdims=True) * (1.0 / D), axis=0, keepdims=True)
        part = jnp.broadcast_to(part, (8, LANES))

        @pl.when(i == 0)
        def _():
            loss_ref[...] = part

        @pl.when(i > 0)
        def _():
            loss_ref[...] += part

    row = pl.BlockSpec((bt, D), lambda i: (i, 0))
    return pl.pallas_call(
        kern, name=name, grid=(T // bt,), in_specs=[row, row],
        out_specs=[row, pl.BlockSpec((8, LANES), lambda i: (0, 0))],
        out_shape=[jax.ShapeDtypeStruct((T, D), F32), jax.ShapeDtypeStruct((8, LANES), F32)],
        compiler_params=_cp(("arbitrary",)),
    )(y, tgt)


def _tri(n, upper):
    r = lax.broadcasted_iota(jnp.int32, (n, n), 0)
    c = lax.broadcasted_iota(jnp.int32, (n, n), 1)
    return jnp.where((c >= r) if upper else (c <= r), 1.0, 0.0).astype(BF16)


def _forget_fwd(ff, bf, *, name):
    T = ff.shape[0]
    bt = 256

    def kern(ff_ref, bf_ref, out_ref, carry):
        i = pl.program_id(0)

        @pl.when(i == 0)
        def _():
            carry[...] = jnp.zeros_like(carry)

        ls = _log_sigmoid(ff_ref[...] + bf_ref[...])
        tri = _tri(bt, upper=False)
        hi, mid, lo = _split3(ls)
        cs = _dot(tri, hi) + _dot(tri, mid) + _dot(tri, lo) + carry[0:1, :]
        out_ref[...] = cs
        carry[...] = jnp.broadcast_to(cs[bt - 1:bt, :], carry.shape)

    return pl.pallas_call(
        kern, name=name, grid=(T // bt,),
        in_specs=[pl.BlockSpec((bt, LANES), lambda i: (i, 0)), pl.BlockSpec((1, LANES), lambda i: (0, 0))],
        out_specs=pl.BlockSpec((bt, LANES), lambda i: (i, 0)),
        out_shape=jax.ShapeDtypeStruct((T, LANES), F32),
        scratch_shapes=[pltpu.VMEM((8, LANES), F32)],
        compiler_params=_cp(("arbitrary",)),
    )(ff, bf)


def _forget_bwd(dFk, dFq, ff, bf, *, name):
    T = ff.shape[0]
    bt = 256
    nb = T // bt

    def kern(dFk_ref, dFq_ref, ff_ref, bf_ref, dff_ref, dbf_ref, carry):
        i = pl.program_id(0)

        @pl.when(i == 0)
        def _():
            carry[...] = jnp.zeros_like(carry)
            dbf_ref[...] = jnp.zeros_like(dbf_ref)

        tri = _tri(bt, upper=True)
        hi, mid, lo = _split3(dFk_ref[...] + dFq_ref[...])
        rs = _dot(tri, hi) + _dot(tri, mid) + _dot(tri, lo) + carry[0:1, :]
        carry[...] = jnp.broadcast_to(rs[0:1, :], carry.shape)
        z = ff_ref[...] + bf_ref[...]
        dff = rs * _sigmoid(-z)
        dff_ref[...] = dff.astype(dff_ref.dtype)
        dbf_ref[...] += jnp.sum(dff, axis=0, keepdims=True)

    rev = pl.BlockSpec((bt, LANES), lambda i: (nb - 1 - i, 0))
    vec = pl.BlockSpec((1, LANES), lambda i: (0, 0))
    return pl.pallas_call(
        kern, name=name, grid=(nb,), in_specs=[rev, rev, rev, vec], out_specs=[rev, vec],
        out_shape=[jax.ShapeDtypeStruct((T, LANES), BF16), jax.ShapeDtypeStruct((1, LANES), F32)],
        scratch_shapes=[pltpu.VMEM((8, LANES), F32)],
        compiler_params=_cp(("arbitrary",)),
    )(dFk, dFq, ff, bf)


def _att_blocks(T):
    return min(ATT_BLOCK, T), min(ATT_KEYS, T)


def _positions(i, j, bq, bk):
    r = i * bq + lax.broadcasted_iota(jnp.int32, (bq, bk), 0)
    c = j * bk + lax.broadcasted_iota(jnp.int32, (bq, bk), 1)
    return r, c


def _fox_fwd(u_att, fcol, frow, *, name):
    T = u_att.shape[0]
    bq, bk = _att_blocks(T)
    nq, nk = T // bq, T // bk
    H = N_HEADS

    def kern(q_ref, k_ref, v_ref, fc_ref, fr_ref, o_ref, lse_ref):
        i = pl.program_id(1)
        q = q_ref[...]
        fq = fc_ref[...]

        def step(j, carry, masked):
            m, l, acc = carry
            off = pl.multiple_of(j * bk, bk)
            k = k_ref[pl.ds(off, bk), :]
            v = v_ref[pl.ds(off, bk), :]
            s = _dot(q, k, NT) * SCALE + (fq - fr_ref[j])
            if masked:
                r, c = _positions(i, j, bq, bk)
                s = jnp.where(c <= r, s, -jnp.inf)
            m_new = jnp.maximum(m, jnp.max(s, axis=1, keepdims=True))
            a = jnp.exp(m - m_new)
            p = jnp.exp(s - m_new)
            l = a * l + jnp.sum(p, axis=1, keepdims=True)
            acc = a * acc + _dot(p.astype(BF16), v)
            return m_new, l, acc

        init = (jnp.full((bq, 1), -1e30, F32), jnp.zeros((bq, 1), F32), jnp.zeros((bq, HEAD_DIM), F32))
        nfull = (i * bq) // bk
        carry = lax.fori_loop(0, nfull, lambda j, cr: step(j, cr, False), init)
        m, l, acc = step(nfull, carry, True)
        o_ref[...] = (acc / l).astype(o_ref.dtype)
        lse_ref[...] = m + jnp.log(l)

    return pl.pallas_call(
        kern, name=name, grid=(H, nq),
        in_specs=[pl.BlockSpec((bq, HEAD_DIM), lambda h, i: (i, h)),
                  pl.BlockSpec((T, HEAD_DIM), lambda h, i: (0, 4 + h)),
                  pl.BlockSpec((T, HEAD_DIM), lambda h, i: (0, 8 + h)),
                  pl.BlockSpec((None, bq, 1), lambda h, i: (h, i, 0)),
                  pl.BlockSpec((None, nk, 1, bk), lambda h, i: (h, 0, 0, 0))],
        out_specs=[pl.BlockSpec((bq, HEAD_DIM), lambda h, i: (i, h)),
                   pl.BlockSpec((None, bq, 1), lambda h, i: (h, i, 0))],
        out_shape=[jax.ShapeDtypeStruct((T, BRANCH_WIDTH), BF16), jax.ShapeDtypeStruct((H, T, 1), F32)],
        compiler_params=_cp(("parallel", "arbitrary")),
    )(u_att, u_att, u_att, fcol, frow)


def _row_dot(a, b, *, name):
    T = a.shape[0]
    bt = _pick(T, (512, 256, 128))

    def kern(a_ref, b_ref, o_ref):
        p = a_ref[...].astype(F32) * b_ref[...].astype(F32)
        for h in range(N_HEADS):
            o_ref[h] = jnp.sum(p[:, h * HEAD_DIM:(h + 1) * HEAD_DIM], axis=1, keepdims=True)

    row = pl.BlockSpec((bt, BRANCH_WIDTH), lambda i: (i, 0))
    return pl.pallas_call(
        kern, name=name, grid=(T // bt,), in_specs=[row, row],
        out_specs=pl.BlockSpec((N_HEADS, bt, 1), lambda i: (0, i, 0)),
        out_shape=jax.ShapeDtypeStruct((N_HEADS, T, 1), F32),
        compiler_params=_cp(("arbitrary",)),
    )(a, b)


def _fox_bwd(u_att, do, lse, delta, fcol, frow, *, name):
    T = u_att.shape[0]
    bq, bk = _att_blocks(T)
    nq, nk = T // bq, T // bk
    H = N_HEADS

    def kern(q_ref, k_ref, v_ref, do_ref, lse_ref, dl_ref, fc_ref, fr_ref,
             dq_ref, dk_ref, dv_ref, df_ref, dfq_ref, dk_acc, dv_acc, df_acc):
        i = pl.program_id(1)

        @pl.when(i == 0)
        def _():
            dk_acc[...] = jnp.zeros_like(dk_acc)
            dv_acc[...] = jnp.zeros_like(dv_acc)
            df_acc[...] = jnp.zeros_like(df_acc)

        q = q_ref[...]
        dov = do_ref[...]
        fq = fc_ref[...]
        lsev = lse_ref[...]
        dlt = dl_ref[...]

        def step(j, carry, masked):
            dq, dfq = carry
            off = pl.multiple_of(j * bk, bk)
            k = k_ref[pl.ds(off, bk), :]
            v = v_ref[pl.ds(off, bk), :]
            s = _dot(q, k, NT) * SCALE + (fq - fr_ref[j])
            p = jnp.exp(s - lsev)
            if masked:
                r, c = _positions(i, j, bq, bk)
                p = jnp.where(c <= r, p, 0.0)
            dp = _dot(dov, v, NT)
            ds = p * (dp - dlt)
            dsb = ds.astype(BF16)
            dq = dq + _dot(dsb, k)
            dk_acc[pl.ds(off, bk), :] += _dot(dsb, q, TN)
            dv_acc[pl.ds(off, bk), :] += _dot(p.astype(BF16), dov, TN)
            df_acc[j] += -jnp.sum(ds, axis=0, keepdims=True)
            return dq, dfq + jnp.sum(ds, axis=1, keepdims=True)

        nfull = (i * bq) // bk
        carry = lax.fori_loop(0, nfull, lambda j, cr: step(j, cr, False),
                              (jnp.zeros((bq, HEAD_DIM), F32), jnp.zeros((bq, 1), F32)))
        dq, dfq = step(nfull, carry, True)
        dq_ref[...] = (dq * SCALE).astype(dq_ref.dtype)
        dfq_ref[...] = dfq

        @pl.when(i == nq - 1)
        def _():
            dk_ref[...] = (dk_acc[...] * SCALE).astype(dk_ref.dtype)
            dv_ref[...] = dv_acc[...].astype(dv_ref.dtype)
            df_ref[...] = df_acc[...]

    col = lambda: pl.BlockSpec((None, bq, 1), lambda h, i: (h, i, 0))
    return pl.pallas_call(
        kern, name=name, grid=(H, nq),
        in_specs=[pl.BlockSpec((bq, HEAD_DIM), lambda h, i: (i, h)),
                  pl.BlockSpec((T, HEAD_DIM), lambda h, i: (0, 4 + h)),
                  pl.BlockSpec((T, HEAD_DIM), lambda h, i: (0, 8 + h)),
                  pl.BlockSpec((bq, HEAD_DIM), lambda h, i: (i, h)),
                  col(), col(), col(),
                  pl.BlockSpec((None, nk, 1, bk), lambda h, i: (h, 0, 0, 0))],
        out_specs=[pl.BlockSpec((bq, HEAD_DIM), lambda h, i: (i, h)),
                   pl.BlockSpec((T, HEAD_DIM), lambda h, i: (0, h)),
                   pl.BlockSpec((T, HEAD_DIM), lambda h, i: (0, h)),
                   pl.BlockSpec((None, nk, 1, bk), lambda h, i: (h, 0, 0, 0)),
                   pl.BlockSpec((None, bq, 1), lambda h, i: (h, i, 0))],
        out_shape=[jax.ShapeDtypeStruct((T, BRANCH_WIDTH), BF16)] * 3
                  + [jax.ShapeDtypeStruct((H, nk, 1, bk), F32), jax.ShapeDtypeStruct((H, T, 1), F32)],
        scratch_shapes=[pltpu.VMEM((T, HEAD_DIM), F32), pltpu.VMEM((T, HEAD_DIM), F32),
                        pltpu.VMEM((nk, 1, bk), F32)],
        compiler_params=_cp(("parallel", "arbitrary")),
    )(u_att, u_att, u_att, do, lse, delta, fcol, frow)


def _softplus_parts(z):
    t = jnp.exp(-jnp.abs(z))
    sp = jnp.maximum(z, 0.0) + jnp.log(1.0 + t)
    return t, sp


def _sb_tri(B):
    r = lax.broadcasted_iota(jnp.int32, (B, B), 0)
    c = lax.broadcasted_iota(jnp.int32, (B, B), 1)
    suffix = jnp.where(r >= c, 1.0, 0.0).astype(BF16)
    prefix = jnp.where(r <= c, 1.0, 0.0).astype(BF16)
    return suffix, prefix


def _sb_fwd(u_att, *, name):
    T = u_att.shape[0]
    B, bk = _att_blocks(T)
    nq, nsub = T // B, bk // B
    H = N_HEADS

    def kern(q_ref, k_ref, v_ref, o_ref):
        i = pl.program_id(1)
        q = q_ref[...]
        suffix, _ = _sb_tri(B)

        def step(j, carry, masked):
            run, acc = carry
            parts = []
            for s in reversed(range(nsub)):
                jb = j * nsub + s
                off = pl.multiple_of(jb * B, B)
                k = k_ref[pl.ds(off, B), :]
                z = _dot(q, k, NT) * SCALE
                _, sp = _softplus_parts(z)
                lg = -sp
                valid = None
                if masked:
                    r, c = _positions(i, jb, B, B)
                    valid = c < r
                    lg = jnp.where(valid, lg, 0.0)
                hi, lo = _split2(lg)
                cum = _dot(hi, suffix) + _dot(lo, suffix)
                parts.append((off, z, cum, jnp.sum(lg, axis=1, keepdims=True), valid))
            for off, z, cum, rs, valid in parts:
                a = jnp.exp(z + cum + run)
                if masked:
                    a = jnp.where(valid, a, 0.0)
                acc = acc + _dot(a.astype(BF16), v_ref[pl.ds(off, B), :])
                run = run + rs
            return run, acc

        nfull = (i * B) // bk
        carry = step(nfull, (jnp.zeros((B, 1), F32), jnp.zeros((B, HEAD_DIM), F32)), True)
        _, acc = lax.fori_loop(0, nfull, lambda jj, cr: step(nfull - 1 - jj, cr, False), carry)
        o_ref[...] = acc.astype(o_ref.dtype)

    return pl.pallas_call(
        kern, name=name, grid=(H, nq),
        in_specs=[pl.BlockSpec((B, HEAD_DIM), lambda h, i: (i, 12 + h)),
                  pl.BlockSpec((T, HEAD_DIM), lambda h, i: (0, 16 + h)),
                  pl.BlockSpec((T, HEAD_DIM), lambda h, i: (0, 20 + h))],
        out_specs=pl.BlockSpec((B, HEAD_DIM), lambda h, i: (i, h)),
        out_shape=jax.ShapeDtypeStruct((T, BRANCH_WIDTH), BF16),
        compiler_params=_cp(("parallel", "arbitrary")),
    )(u_att, u_att, u_att)


def _sb_bwd(u_att, do, *, name):
    T = u_att.shape[0]
    B, bk = _att_blocks(T)
    nq, nsub = T // B, bk // B
    H = N_HEADS

    def kern(q_ref, k_ref, v_ref, do_ref, dq_ref, dk_ref, dv_ref, dk_acc, dv_acc, de_s, sg_s):
        i = pl.program_id(1)

        @pl.when(i == 0)
        def _():
            dk_acc[...] = jnp.zeros_like(dk_acc)
            dv_acc[...] = jnp.zeros_like(dv_acc)

        q = q_ref[...]
        dov = do_ref[...]
        suffix, prefix = _sb_tri(B)

        def sweep1(j, run, masked):
            parts = []
            for s in reversed(range(nsub)):
                jb = j * nsub + s
                off = pl.multiple_of(jb * B, B)
                k = k_ref[pl.ds(off, B), :]
                z = _dot(q, k, NT) * SCALE
                t, sp = _softplus_parts(z)
                lg = -sp
                sg = jnp.where(z >= 0.0, 1.0, t) / (1.0 + t)
                valid = None
                if masked:
                    r, c = _positions(i, jb, B, B)
                    valid = c < r
                    lg = jnp.where(valid, lg, 0.0)
                    sg = jnp.where(valid, sg, 0.0)
                sg_s[jb] = sg.astype(sg_s.dtype)
                hi, lo = _split2(lg)
                cum = _dot(hi, suffix) + _dot(lo, suffix)
                da = _dot(dov, v_ref[pl.ds(off, B), :], NT)
                parts.append((jb, off, z, cum, da, jnp.sum(lg, axis=1, keepdims=True), valid))
            for jb, off, z, cum, da, rs, valid in parts:
                a = jnp.exp(z + cum + run)
                if masked:
                    a = jnp.where(valid, a, 0.0)
                de_s[jb] = a * da
                dv_acc[pl.ds(off, B), :] += _dot(a.astype(BF16), dov, TN)
                run = run + rs
            return run

        nfull = (i * B) // bk
        run = sweep1(nfull, jnp.zeros((B, 1), F32), True)
        lax.fori_loop(0, nfull, lambda jj, cr: sweep1(nfull - 1 - jj, cr, False), run)

        def sweep2(j, carry):
            pre, dq = carry
            parts = []
            for s in range(nsub):
                jb = j * nsub + s
                de = de_s[jb]
                hi, lo = _split2(de)
                parts.append((jb, de, _dot(hi, prefix) + _dot(lo, prefix), jnp.sum(de, axis=1, keepdims=True)))
            for jb, de, g, rs in parts:
                off = pl.multiple_of(jb * B, B)
                dz = (de - sg_s[jb].astype(F32) * (g + pre)).astype(BF16)
                dq = dq + _dot(dz, k_ref[pl.ds(off, B), :])
                dk_acc[pl.ds(off, B), :] += _dot(dz, q, TN)
                pre = pre + rs
            return pre, dq

        _, dq = lax.fori_loop(0, nfull + 1, sweep2, (jnp.zeros((B, 1), F32), jnp.zeros((B, HEAD_DIM), F32)))
        dq_ref[...] = (dq * SCALE).astype(dq_ref.dtype)

        @pl.when(i == nq - 1)
        def _():
            dk_ref[...] = (dk_acc[...] * SCALE).astype(dk_ref.dtype)
            dv_ref[...] = dv_acc[...].astype(dv_ref.dtype)

    return pl.pallas_call(
        kern, name=name, grid=(H, nq),
        in_specs=[pl.BlockSpec((B, HEAD_DIM), lambda h, i: (i, 12 + h)),
                  pl.BlockSpec((T, HEAD_DIM), lambda h, i: (0, 16 + h)),
                  pl.BlockSpec((T, HEAD_DIM), lambda h, i: (0, 20 + h)),
                  pl.BlockSpec((B, HEAD_DIM), lambda h, i: (i, h))],
        out_specs=[pl.BlockSpec((B, HEAD_DIM), lambda h, i: (i, h)),
                   pl.BlockSpec((T, HEAD_DIM), lambda h, i: (0, h)),
                   pl.BlockSpec((T, HEAD_DIM), lambda h, i: (0, h))],
        out_shape=[jax.ShapeDtypeStruct((T, BRANCH_WIDTH), BF16)] * 3,
        scratch_shapes=[pltpu.VMEM((T, HEAD_DIM), F32), pltpu.VMEM((T, HEAD_DIM), F32),
                        pltpu.VMEM((T // B, B, B), F32), pltpu.VMEM((T // B, B, B), BF16)],
        compiler_params=_cp(("parallel", "arbitrary")),
    )(u_att, u_att, u_att, do)


def _rel_onehot(qrow):
    k = lax.broadcasted_iota(jnp.int32, (BAND, REL_PAD), 0)
    rr = lax.broadcasted_iota(jnp.int32, (BAND, REL_PAD), 1)
    idx = jnp.clip(PAD_ROWS + qrow - k, -(CHUNK - 1), REL_CLIP) + (CHUNK - 1)
    return jnp.where(idx == rr, 1.0, 0.0).astype(BF16)


def _band_bias(table, *, name):
    def kern(t_ref, o_ref):
        hi, mid, lo = _split3(t_ref[...])

        def body(qrow, _):
            oh = _rel_onehot(qrow)
            o_ref[qrow] = _dot(hi, oh, NT) + _dot(mid, oh, NT) + _dot(lo, oh, NT)
            return 0

        lax.fori_loop(0, CHUNK, body, 0)

    return pl.pallas_call(
        kern, name=name, out_shape=jax.ShapeDtypeStruct((CHUNK, 8, BAND), F32),
        compiler_params=_cp(),
    )(table)


def _band_bias_bwd(dbias, *, name):
    def kern(d_ref, o_ref):
        def body(qrow, acc):
            oh = _rel_onehot(qrow)
            hi, mid, lo = _split3(d_ref[qrow])
            return acc + _dot(hi, oh) + _dot(mid, oh) + _dot(lo, oh)

        o_ref[...] = lax.fori_loop(0, CHUNK, body, jnp.zeros((8, REL_PAD), F32))

    return pl.pallas_call(
        kern, name=name, out_shape=jax.ShapeDtypeStruct((8, REL_PAD), F32),
        compiler_params=_cp(),
    )(dbias)


def _chunk_rows(T):
    return _pick(T, (512, 256, 128, 64))


def _chunk_scores(q, kw, bias, c_global):
    s = _dot(q, kw, NT) * SCALE + bias
    col = lax.broadcasted_iota(jnp.int32, (CHUNK, BAND), 1)
    valid = (c_global * CHUNK + col) >= PAD_ROWS
    s = jnp.where(valid, s, -jnp.inf)
    m = jnp.max(s, axis=1, keepdims=True)
    e = jnp.exp(s - m)
    return e / jnp.sum(e, axis=1, keepdims=True)


def _chunk_fwd(u_att, bias, *, name):
    T = u_att.shape[0]
    R = _chunk_rows(T)
    nr = T // R
    H = N_HEADS

    def kern(q_ref, k_ref, v_ref, b_ref, o_ref, kpad, vpad):
        i = pl.program_id(1)

        @pl.when(i == 0)
        def _():
            kpad[0:PAD_ROWS, :] = jnp.zeros((PAD_ROWS, HEAD_DIM), BF16)
            vpad[0:PAD_ROWS, :] = jnp.zeros((PAD_ROWS, HEAD_DIM), BF16)
            kpad[PAD_ROWS:, :] = k_ref[...]
            vpad[PAD_ROWS:, :] = v_ref[...]

        bias_v = b_ref[...]
        for cc in range(R // CHUNK):
            cg = i * (R // CHUNK) + cc
            off = pl.multiple_of(cg * CHUNK, CHUNK)
            q = q_ref[cc * CHUNK:(cc + 1) * CHUNK, :]
            kw = kpad[pl.ds(off, BAND), :]
            vw = vpad[pl.ds(off, BAND), :]
            p = _chunk_scores(q, kw, bias_v, cg)
            o_ref[cc * CHUNK:(cc + 1) * CHUNK, :] = _dot(p.astype(BF16), vw).astype(o_ref.dtype)

    return pl.pallas_call(
        kern, name=name, grid=(H, nr),
        in_specs=[pl.BlockSpec((R, HEAD_DIM), lambda h, i: (i, 24 + h)),
                  pl.BlockSpec((T, HEAD_DIM), lambda h, i: (0, 28 + h)),
                  pl.BlockSpec((T, HEAD_DIM), lambda h, i: (0, 32 + h)),
                  pl.BlockSpec((None, CHUNK, BAND), lambda h, i: (h, 0, 0))],
        out_specs=pl.BlockSpec((R, HEAD_DIM), lambda h, i: (i, h)),
        out_shape=jax.ShapeDtypeStruct((T, BRANCH_WIDTH), BF16),
        scratch_shapes=[pltpu.VMEM((T + PAD_ROWS, HEAD_DIM), BF16), pltpu.VMEM((T + PAD_ROWS, HEAD_DIM), BF16)],
        compiler_params=_cp(("parallel", "arbitrary")),
    )(u_att, u_att, u_att, bias)


def _chunk_bwd(u_att, bias, do, *, name):
    T = u_att.shape[0]
    R = _chunk_rows(T)
    nr = T // R
    H = N_HEADS

    def kern(q_ref, k_ref, v_ref, b_ref, do_ref, dq_ref, dk_ref, dv_ref, db_ref, kpad, vpad, dkp, dvp):
        i = pl.program_id(1)

        @pl.when(i == 0)
        def _():
            kpad[0:PAD_ROWS, :] = jnp.zeros((PAD_ROWS, HEAD_DIM), BF16)
            vpad[0:PAD_ROWS, :] = jnp.zeros((PAD_ROWS, HEAD_DIM), BF16)
            kpad[PAD_ROWS:, :] = k_ref[...]
            vpad[PAD_ROWS:, :] = v_ref[...]
            dkp[...] = jnp.zeros_like(dkp)
            dvp[...] = jnp.zeros_like(dvp)
            db_ref[...] = jnp.zeros_like(db_ref)

        bias_v = b_ref[...]
        for cc in range(R // CHUNK):
            cg = i * (R // CHUNK) + cc
            off = pl.multiple_of(cg * CHUNK, CHUNK)
            q = q_ref[cc * CHUNK:(cc + 1) * CHUNK, :]
            dov = do_ref[cc * CHUNK:(cc + 1) * CHUNK, :]
            kw = kpad[pl.ds(off, BAND), :]
            vw = vpad[pl.ds(off, BAND), :]
            p = _chunk_scores(q, kw, bias_v, cg)
            dp = _dot(dov, vw, NT)
            ds = p * (dp - jnp.sum(p * dp, axis=1, keepdims=True))
            dsb = ds.astype(BF16)
            dq_ref[cc * CHUNK:(cc + 1) * CHUNK, :] = (_dot(dsb, kw) * SCALE).astype(dq_ref.dtype)
            dkp[pl.ds(off, BAND), :] += _dot(dsb, q, TN)
            dvp[pl.ds(off, BAND), :] += _dot(p.astype(BF16), dov, TN)
            db_ref[...] += ds

        @pl.when(i == nr - 1)
        def _():
            dk_ref[...] = (dkp[PAD_ROWS:, :] * SCALE).astype(dk_ref.dtype)
            dv_ref[...] = dvp[PAD_ROWS:, :].astype(dv_ref.dtype)

    return pl.pallas_call(
        kern, name=name, grid=(H, nr),
        in_specs=[pl.BlockSpec((R, HEAD_DIM), lambda h, i: (i, 24 + h)),
                  pl.BlockSpec((T, HEAD_DIM), lambda h, i: (0, 28 + h)),
                  pl.BlockSpec((T, HEAD_DIM), lambda h, i: (0, 32 + h)),
                  pl.BlockSpec((None, CHUNK, BAND), lambda h, i: (h, 0, 0)),
                  pl.BlockSpec((R, HEAD_DIM), lambda h, i: (i, h))],
        out_specs=[pl.BlockSpec((R, HEAD_DIM), lambda h, i: (i, h)),
                   pl.BlockSpec((T, HEAD_DIM), lambda h, i: (0, h)),
                   pl.BlockSpec((T, HEAD_DIM), lambda h, i: (0, h)),
                   pl.BlockSpec((None, CHUNK, BAND), lambda h, i: (h, 0, 0))],
        out_shape=[jax.ShapeDtypeStruct((T, BRANCH_WIDTH), BF16)] * 3
                  + [jax.ShapeDtypeStruct((H, CHUNK, BAND), F32)],
        scratch_shapes=[pltpu.VMEM((T + PAD_ROWS, HEAD_DIM), BF16), pltpu.VMEM((T + PAD_ROWS, HEAD_DIM), BF16),
                        pltpu.VMEM((T + PAD_ROWS, HEAD_DIM), F32), pltpu.VMEM((T + PAD_ROWS, HEAD_DIM), F32)],
        compiler_params=_cp(("parallel", "arbitrary")),
    )(u_att, u_att, u_att, bias, do)


LRU_ROWS = 256
HALO = 8


def _gelu(y):
    k0 = math.sqrt(2.0 / math.pi)
    t = jnp.tanh(k0 * (y + 0.044715 * y * y * y))
    return 0.5 * y * (1.0 + t), t


def _gelu_grad(y, t):
    k0 = math.sqrt(2.0 / math.pi)
    return 0.5 * (1.0 + t) + 0.5 * y * (1.0 - t * t) * k0 * (1.0 + 3.0 * 0.044715 * y * y)


def _neg_expm1(y):
    poly = -y * (1.0 + y * (1.0 / 2 + y * (1.0 / 6 + y * (1.0 / 24 + y * (1.0 / 120 + y * (1.0 / 720 + y * (1.0 / 5040)))))))
    return jnp.where(y > -0.5, poly, 1.0 - jnp.exp(y))


def _lru_gates(ext, cw_ref, cb_ref, wr_ref, br_ref, wi_ref, bi_ref, lam_ref, rows):
    xc = cb_ref[...] + jnp.zeros((rows, BRANCH_WIDTH), F32)
    for j in range(CONV_WIDTH):
        xc = xc + ext[pl.ds(HALO - (CONV_WIDTH - 1) + j, rows), :] * cw_ref[j:j + 1, :]
    xcb = xc.astype(BF16)
    zr = jnp.concatenate([_dot(xcb[:, n * 128:(n + 1) * 128], wr_ref[n]) for n in range(4)], axis=1) + br_ref[...]
    zi = jnp.concatenate([_dot(xcb[:, n * 128:(n + 1) * 128], wi_ref[n]) for n in range(4)], axis=1) + bi_ref[...]
    r = _sigmoid(zr)
    gi = _sigmoid(zi)
    ls = _log_sigmoid(lam_ref[...])
    la = LRU_C * r * ls
    a = jnp.exp(la)
    mult = jnp.sqrt(_neg_expm1(2.0 * la))
    return xc, xcb, r, gi, ls, a, mult


def _lru_param_specs():
    full2 = lambda s: pl.BlockSpec(s, lambda i: (0, 0))
    full3 = lambda s: pl.BlockSpec(s, lambda i: (0, 0, 0))
    return [full2((8, BRANCH_WIDTH)), full2((1, BRANCH_WIDTH)), full3((4, 128, 128)), full2((1, BRANCH_WIDTH)),
            full3((4, 128, 128)), full2((1, BRANCH_WIDTH)), full2((1, BRANCH_WIDTH))]


def _lru_fwd(u_rec, p, *, name):
    T = u_rec.shape[0]
    R = min(LRU_ROWS, T)
    nb = T // R
    W = BRANCH_WIDTH
    hb = R // HALO

    def kern(rx_ref, halo_ref, ry_ref, cw_ref, cb_ref, wr_ref, br_ref, wi_ref, bi_ref, lam_ref,
             o_ref, h_ref, ext, a_s, b_s, hc):
        i = pl.program_id(0)

        @pl.when(i == 0)
        def _():
            hc[...] = jnp.zeros_like(hc)

        ext[0:HALO, :] = jnp.where(i == 0, 0.0, halo_ref[...])
        ext[HALO:, :] = rx_ref[...]
        xc, _, r, gi, ls, a, mult = _lru_gates(ext, cw_ref, cb_ref, wr_ref, br_ref, wi_ref, bi_ref, lam_ref, R)
        a_s[...] = a
        b_s[...] = mult * (gi * xc)

        def body(t, h):
            h = a_s[pl.ds(t, 1), :] * h + b_s[pl.ds(t, 1), :]
            h_ref[pl.ds(t, 1), :] = h
            return h

        h = lax.fori_loop(0, R, body, hc[0:1, :], unroll=8)
        hc[...] = jnp.broadcast_to(h, hc.shape)
        g, _ = _gelu(ry_ref[...])
        o_ref[...] = (h_ref[...] * g).astype(o_ref.dtype)

    return pl.pallas_call(
        kern, name=name, grid=(nb,),
        in_specs=[pl.BlockSpec((R, W), lambda i: (i, 0)),
                  pl.BlockSpec((HALO, W), lambda i: (jnp.maximum(i * hb - 1, 0), 0)),
                  pl.BlockSpec((R, W), lambda i: (i, 1))] + _lru_param_specs(),
        out_specs=[pl.BlockSpec((R, W), lambda i: (i, 0)), pl.BlockSpec((R, W), lambda i: (i, 0))],
        out_shape=[jax.ShapeDtypeStruct((T, W), BF16), jax.ShapeDtypeStruct((T, W), F32)],
        scratch_shapes=[pltpu.VMEM((R + HALO, W), F32), pltpu.VMEM((R, W), F32), pltpu.VMEM((R, W), F32),
                        pltpu.VMEM((8, W), F32)],
        compiler_params=_cp(("arbitrary",)),
    )(u_rec, u_rec, u_rec, *p)


def _lru_bwd(u_rec, hs, do, p, *, name):
    T = u_rec.shape[0]
    R = min(LRU_ROWS, T)
    nb = T // R
    W = BRANCH_WIDTH
    hb = R // HALO

    def kern(rx_ref, halo_ref, ry_ref, h_ref, hh_ref, do_ref, cw_ref, cb_ref, wr_ref, br_ref, wi_ref, bi_ref, lam_ref,
             drx_ref, dry_ref, dcw_ref, dcb_ref, dwr_ref, dbr_ref, dwi_ref, dbi_ref, dlam_ref,
             ext, hext, a_s, g_s, dext, gc):
        s = pl.program_id(0)
        first_block = s == nb - 1

        @pl.when(s == 0)
        def _():
            gc[...] = jnp.zeros_like(gc)
            dext[R:, :] = jnp.zeros((HALO, W), F32)
            for ref in (dcw_ref, dcb_ref, dwr_ref, dbr_ref, dwi_ref, dbi_ref, dlam_ref):
                ref[...] = jnp.zeros_like(ref)

        ext[0:HALO, :] = jnp.where(first_block, 0.0, halo_ref[...])
        ext[HALO:, :] = rx_ref[...]
        hext[0:HALO, :] = jnp.where(first_block, 0.0, hh_ref[...])
        hext[HALO:, :] = h_ref[...]
        xc, xcb, r, gi, ls, a, mult = _lru_gates(ext, cw_ref, cb_ref, wr_ref, br_ref, wi_ref, bi_ref, lam_ref, R)
        ry = ry_ref[...]
        gel, th = _gelu(ry)
        dov = do_ref[...].astype(F32)
        dry_ref[...] = (dov * h_ref[...] * _gelu_grad(ry, th)).astype(dry_ref.dtype)
        a_s[...] = a
        g_s[...] = dov * gel

        def body(tt, g):
            t = R - 1 - tt
            dh = g_s[pl.ds(t, 1), :] + g
            g_s[pl.ds(t, 1), :] = dh
            return a_s[pl.ds(t, 1), :] * dh

        g = lax.fori_loop(0, R, body, gc[0:1, :], unroll=8)
        gc[...] = jnp.broadcast_to(g, gc.shape)
        dh = g_s[...]
        hprev = hext[pl.ds(HALO - 1, R), :]
        da = dh * hprev
        gx = gi * xc
        dmult = dh * gx
        dgx = dh * mult
        dgi = dgx * xc
        dxc = dgx * gi
        dla = da * a - dmult * (a * a) / mult
        dr = dla * (LRU_C * ls)
        dlam_ref[...] += jnp.sum(dla * (LRU_C * r), axis=0, keepdims=True)
        dzr = dr * r * (1.0 - r)
        dzi = dgi * gi * (1.0 - gi)
        dbr_ref[...] += jnp.sum(dzr, axis=0, keepdims=True)
        dbi_ref[...] += jnp.sum(dzi, axis=0, keepdims=True)
        dzrb = dzr.astype(BF16)
        dzib = dzi.astype(BF16)
        back = []
        for n in range(4):
            sl = slice(n * 128, (n + 1) * 128)
            dwr_ref[n] += _dot(xcb[:, sl], dzrb[:, sl], TN)
            dwi_ref[n] += _dot(xcb[:, sl], dzib[:, sl], TN)
            back.append(_dot(dzrb[:, sl], wr_ref[n], NT) + _dot(dzib[:, sl], wi_ref[n], NT))
        dxc = dxc + jnp.concatenate(back, axis=1)
        dcb_ref[...] += jnp.sum(dxc, axis=0, keepdims=True)
        for j in range(CONV_WIDTH):
            dcw_ref[j:j + 1, :] += jnp.sum(dxc * ext[pl.ds(HALO - (CONV_WIDTH - 1) + j, R), :], axis=0, keepdims=True)
        dext[0:R, :] = dxc
        drx = jnp.zeros((R, W), F32)
        for j in range(CONV_WIDTH):
            drx = drx + dext[pl.ds(CONV_WIDTH - 1 - j, R), :] * cw_ref[j:j + 1, :]
        drx_ref[...] = drx.astype(drx_ref.dtype)
        dext[R:, :] = dxc[0:HALO, :]

        @pl.when(s == nb - 1)
        def _():
            dlam_ref[...] = dlam_ref[...] * _sigmoid(-lam_ref[...])

    rev = lambda c: pl.BlockSpec((R, W), lambda s: (nb - 1 - s, c))
    halo = lambda: pl.BlockSpec((HALO, W), lambda s: (jnp.maximum((nb - 1 - s) * hb - 1, 0), 0))
    v2 = lambda shp: pl.BlockSpec(shp, lambda s: (0, 0))
    v3 = lambda shp: pl.BlockSpec(shp, lambda s: (0, 0, 0))
    return pl.pallas_call(
        kern, name=name, grid=(nb,),
        in_specs=[rev(0), halo(), rev(1), rev(0), halo(), rev(0)] + _lru_param_specs(),
        out_specs=[rev(0), rev(0), v2((8, W)), v2((1, W)), v3((4, 128, 128)), v2((1, W)), v3((4, 128, 128)),
                   v2((1, W)), v2((1, W))],
        out_shape=[jax.ShapeDtypeStruct((T, W), BF16), jax.ShapeDtypeStruct((T, W), BF16),
                   jax.ShapeDtypeStruct((8, W), F32), jax.ShapeDtypeStruct((1, W), F32),
                   jax.ShapeDtypeStruct((4, 128, 128), F32), jax.ShapeDtypeStruct((1, W), F32),
                   jax.ShapeDtypeStruct((4, 128, 128), F32), jax.ShapeDtypeStruct((1, W), F32),
                   jax.ShapeDtypeStruct((1, W), F32)],
        scratch_shapes=[pltpu.VMEM((R + HALO, W), F32), pltpu.VMEM((R + HALO, W), F32), pltpu.VMEM((R, W), F32),
                        pltpu.VMEM((R, W), F32), pltpu.VMEM((R + HALO, W), F32), pltpu.VMEM((8, W), F32)],
        compiler_params=_cp(("arbitrary",)),
    )(u_rec, u_rec, u_rec, hs, hs, do, *p)


def _merge_fwd(o_all, wb, gate, *, name):
    T = o_all.shape[1]
    D = D_MODEL
    bm = _pick(T, (1024, 512, 256, 128))
    bn = 1024
    nj = D // bn

    def kern(o_ref, w_ref, g_ref, m_ref, pb_ref, acc):
        g = pl.program_id(2)
        pbv = _dot(o_ref[...], w_ref[...])
        pb_ref[...] = pbv.astype(pb_ref.dtype)
        term = g_ref[...].astype(F32) * pbv

        @pl.when(g == 0)
        def _():
            acc[...] = term

        @pl.when(g > 0)
        def _():
            acc[...] += term

        @pl.when(g == N_BRANCH - 1)
        def _():
            m_ref[...] = acc[...].astype(m_ref.dtype)

    return pl.pallas_call(
        kern, name=name, grid=(T // bm, nj, N_BRANCH),
        in_specs=[pl.BlockSpec((None, bm, BRANCH_WIDTH), lambda i, j, g: (g, i, 0)),
                  pl.BlockSpec((None, BRANCH_WIDTH, bn), lambda i, j, g: (g, 0, j)),
                  pl.BlockSpec((bm, bn), lambda i, j, g: (i, g * nj + j))],
        out_specs=[pl.BlockSpec((bm, bn), lambda i, j, g: (i, j)),
                   pl.BlockSpec((bm, bn), lambda i, j, g: (i, g * nj + j))],
        out_shape=[jax.ShapeDtypeStruct((T, D), BF16), jax.ShapeDtypeStruct((T, N_BRANCH * D), BF16)],
        scratch_shapes=[pltpu.VMEM((bm, bn), F32)],
        compiler_params=_cp(("parallel", "parallel", "arbitrary")),
    )(o_all, wb, gate)


def _merge_bwd(dm, gate, pb, *, name):
    T = dm.shape[0]
    D = D_MODEL
    bt = _pick(T, (256, 128))

    def kern(dm_ref, g_ref, pb_ref, dpb_ref, dzg_ref, dbg_ref):
        i = pl.program_id(1)
        dmv = dm_ref[...]
        gv = g_ref[...].astype(F32)
        dpb_ref[...] = (dmv * gv).astype(dpb_ref.dtype)
        dzg = dmv * pb_ref[...].astype(F32) * gv * (1.0 - gv)
        dzg_ref[...] = dzg.astype(dzg_ref.dtype)
        part = jnp.sum(dzg, axis=0, keepdims=True)

        @pl.when(i == 0)
        def _():
            dbg_ref[...] = part

        @pl.when(i > 0)
        def _():
            dbg_ref[...] += part

    return pl.pallas_call(
        kern, name=name, grid=(N_BRANCH, T // bt),
        in_specs=[pl.BlockSpec((bt, D), lambda g, i: (i, 0)),
                  pl.BlockSpec((bt, D), lambda g, i: (i, g)),
                  pl.BlockSpec((bt, D), lambda g, i: (i, g))],
        out_specs=[pl.BlockSpec((None, bt, D), lambda g, i: (g, i, 0)),
                   pl.BlockSpec((bt, D), lambda g, i: (i, g)),
                   pl.BlockSpec((1, D), lambda g, i: (0, g))],
        out_shape=[jax.ShapeDtypeStruct((N_BRANCH, T, D), BF16), jax.ShapeDtypeStruct((T, N_BRANCH * D), BF16),
                   jax.ShapeDtypeStruct((1, N_BRANCH * D), F32)],
        compiler_params=_cp(("parallel", "arbitrary")),
    )(dm, gate, pb)


def _col_split(M, N, bm, bn):
    per = N // N_CHIPS // bn
    return (N_CHIPS, M, N // N_CHIPS), (None, bm, bn), lambda i, j: (j // per, i, j % per)


def _pad_lanes(v, n):
    return jnp.pad(v, [(0, 0)] * (v.ndim - 1) + [(0, n - v.shape[-1])])


def _rows8(v):
    return jnp.pad(v, ((0, 8 - v.shape[0]), (0, 0)))


def _device_step(x, tgt, W):
    T = x.shape[0]
    _, bk = _att_blocks(T)
    H = N_HEADS
    G = {}
    saved = []

    xf, xb = _ln_fwd(x, W['ln_in_g'], W['ln_in_b'], name='ln_in_fwd')
    for l in range(DEPTH):
        w_att, w_rec = W['w_att'][l], W['w_rec'][l]
        u_att = _mm(xb, w_att, name='in_proj_att', out_dtypes=(BF16,))
        u_rec = _mm(xb, w_rec, name='in_proj_rec', out_dtypes=(F32,))
        ffl = u_rec[:, 2 * BRANCH_WIDTH:]
        bf = _pad_lanes(W['b_forget'][l].reshape(1, H), LANES)
        Fc = _forget_fwd(ffl, bf, name='forget_fwd')
        Fh = Fc[:, :H].T
        fcol = Fh.reshape(H, T, 1)
        frow = Fh.reshape(H, T // bk, 1, bk)
        o_fox, lse = _fox_fwd(u_att, fcol, frow, name='fox_fwd')
        lp = (_rows8(W['conv_w'][l]), W['conv_b'][l].reshape(1, -1), W['w_r'][l].astype(BF16),
              W['b_r'][l].reshape(1, -1), W['w_i'][l].astype(BF16), W['b_i'][l].reshape(1, -1),
              W['lru_lambda'][l].reshape(1, -1))
        o_lru, hs = _lru_fwd(u_rec, lp, name='lru_fwd')
        o_sb = _sb_fwd(u_att, name='sb_fwd')
        table = _rows8(_pad_lanes(W['rel_bias'][l], REL_PAD))
        bias = _band_bias(table, name='band_bias').transpose(1, 0, 2)[:H]
        o_ch = _chunk_fwd(u_att, bias, name='chunk_fwd')
        o_all = jnp.stack([o_fox, o_lru, o_sb, o_ch])
        gate = _mm(xb, W['w_gate_cat'][l], name='gate_proj', out_dtypes=(BF16,),
                   extras=[(W['b_gate'][l].reshape(1, -1), 'n')],
                   epilogue=lambda acc, b: (_sigmoid(acc + b),))
        merged, pb = _merge_fwd(o_all, W['w_branch'][l], gate, name='merge_fwd')
        h1 = _mm(merged, W['w_out'][l], name='out_proj', extras=[(xf, 'mn')],
                 epilogue=lambda acc, xr: (ALPHA * xr + acc,))
        xmf, xmb = _ln_fwd(h1, W['ln1_g'][l], W['ln1_b'][l], name='ln_fwd')
        hid, ra = _mm(xmb, W['w_ff1'][l], name='ff1', out_dtypes=(BF16, BF16),
                      epilogue=lambda acc: (jnp.square(jnp.maximum(acc, 0.0)), jnp.maximum(acc, 0.0)))
        h2 = _mm(hid, W['w_ff2'][l], name='ff2', extras=[(xmf, 'mn')],
                 epilogue=lambda acc, xr: (ALPHA * xr + acc,))
        saved.append(dict(xb=xb, u_att=u_att, u_rec=u_rec, ffl=ffl, bf=bf, fcol=fcol, frow=frow, lse=lse, lp=lp,
                          hs=hs, bias=bias, o_all=o_all, gate=gate, merged=merged, pb=pb, h1=h1, xmb=xmb,
                          hid=hid, ra=ra, h2=h2))
        xf, xb = _ln_fwd(h2, W['ln2_g'][l], W['ln2_b'][l], name='ln_fwd')

    dx, loss_tile = _loss_head(xf, tgt, name='loss_head')
    loss = loss_tile[0, 0]

    for l in reversed(range(DEPTH)):
        S = saved[l]
        dh2, dh2b, G[('ln2_g', l)], G[('ln2_b', l)] = _ln_bwd(S['h2'], dx, W['ln2_g'][l], name='ln_bwd')
        da = _mm(dh2b, W['w_ff2'][l], tb=True, name='ff2_dx', out_dtypes=(BF16,), extras=[(S['ra'], 'mn')],
                 epilogue=lambda acc, rav: (acc * (2.0 * rav.astype(F32)),))
        G[('w_ff2', l)] = _mm(S['hid'], dh2b, ta=True, name='ff2_dw').reshape(N_CHIPS, D_FF // N_CHIPS, D_MODEL)
        G[('w_ff1', l)] = _mm(S['xmb'], da, ta=True, name='ff1_dw', bm=1024, bn=1024,
                              out_map=_col_split(D_MODEL, D_FF, 1024, 1024))
        dxm = _mm(da, W['w_ff1'][l], tb=True, name='ff1_dx', extras=[(dh2, 'mn')],
                  epilogue=lambda acc, d: (ALPHA * d + acc,))
        dh1, dh1b, G[('ln1_g', l)], G[('ln1_b', l)] = _ln_bwd(S['h1'], dxm, W['ln1_g'][l], name='ln_bwd')
        dm = _mm(dh1b, W['w_out'][l], tb=True, name='out_dx')
        G[('w_out', l)] = _mm(S['merged'], dh1b, ta=True, name='out_dw').reshape(
            N_CHIPS, D_MODEL // N_CHIPS, D_MODEL)
        dpb, dzg, G[('b_gate', l)] = _merge_bwd(dm, S['gate'], S['pb'], name='merge_bwd')
        do = [_mm(dpb[g], W['w_branch'][l][g], tb=True, name='branch_dx', out_dtypes=(BF16,)) for g in range(N_BRANCH)]
        G[('w_branch', l)] = jnp.stack(
            [_mm(S['o_all'][g], dpb[g], ta=True, name='branch_dw', bm=BRANCH_WIDTH, bn=BRANCH_WIDTH,
                 out_map=_col_split(BRANCH_WIDTH, D_MODEL, BRANCH_WIDTH, BRANCH_WIDTH))
             for g in range(N_BRANCH)], axis=1)
        G[('w_gate', l)] = _mm(S['xb'], dzg, ta=True, name='gate_dw', bm=D_MODEL // N_CHIPS, bn=1024,
                               out_map=((N_CHIPS, N_BRANCH, D_MODEL // N_CHIPS, D_MODEL),
                                        (None, None, D_MODEL // N_CHIPS, 1024),
                                        lambda i, j: (i, j // 2, 0, j % 2)))
        u_att, u_rec = S['u_att'], S['u_rec']
        delta = _row_dot(do[0], S['o_all'][0], name='row_dot')
        fdq, fdk, fdv, dfk, dfq = _fox_bwd(u_att, do[0], S['lse'], delta, S['fcol'], S['frow'], name='fox_bwd')
        dff, dbf = _forget_bwd(_pad_lanes(dfk.reshape(H, T).T, LANES), _pad_lanes(dfq.reshape(H, T).T, LANES),
                               S['ffl'], S['bf'], name='forget_bwd')
        G[('b_forget', l)] = dbf[0, :H]
        (drx, dry, dcw, dcb, G[('w_r', l)], dbr, G[('w_i', l)], dbi, dlam) = _lru_bwd(
            u_rec, S['hs'], do[1], S['lp'], name='lru_bwd')
        G[('conv_w', l)], G[('conv_b', l)] = dcw[:CONV_WIDTH], dcb[0]
        G[('b_r', l)], G[('b_i', l)], G[('lru_lambda', l)] = dbr[0], dbi[0], dlam[0]
        sdq, sdk, sdv = _sb_bwd(u_att, do[2], name='sb_bwd')
        cdq, cdk, cdv, dbias = _chunk_bwd(u_att, S['bias'], do[3], name='chunk_bwd')
        dtab = _band_bias_bwd(jnp.pad(dbias, ((0, 8 - H), (0, 0), (0, 0))).transpose(1, 0, 2), name='band_bias_bwd')
        G[('rel_bias', l)] = dtab[:H, :REL_TABLE]
        du_att = jnp.concatenate([fdq, fdk, fdv, sdq, sdk, sdv, cdq, cdk, cdv], axis=1)
        du_rec = jnp.concatenate([drx, dry, dff], axis=1)
        G[('w_att', l)] = _mm(S['xb'], du_att, ta=True, name='in_att_dw')
        G[('w_rec', l)] = _mm(S['xb'], du_rec, ta=True, name='in_rec_dw')
        t1 = _mm(dzg, W['w_gate_cat'][l], tb=True, name='gate_dx', extras=[(dh1, 'mn')],
                 epilogue=lambda acc, d: (ALPHA * d + acc,))
        t2 = _mm(du_att, W['w_att'][l], tb=True, name='in_att_dx', extras=[(t1, 'mn')],
                 epilogue=lambda acc, d: (d + acc,))
        dx = _mm(du_rec, W['w_rec'][l], tb=True, name='in_rec_dx', extras=[(t2, 'mn')],
                 epilogue=lambda acc, d: (d + acc,))

    gx, _, G[('ln_in_g', -1)], G[('ln_in_b', -1)] = _ln_bwd(x, dx, W['ln_in_g'], name='ln_in_bwd')
    return loss, gx, G


_IN_FQKV = (0, 1536)
_IN_FF = (1536, 1540)
_IN_REC = (1540, 2564)
_IN_REST = (2564, D_IN)


def _prep_weights(full):
    w_in = full['w_in']
    L = w_in.shape[0]
    W = dict(full)
    W['w_att'] = jnp.concatenate([w_in[..., _IN_FQKV[0]:_IN_FQKV[1]], w_in[..., _IN_REST[0]:_IN_REST[1]]], -1).astype(BF16)
    W['w_rec'] = jnp.concatenate([w_in[..., _IN_REC[0]:_IN_REC[1]], w_in[..., _IN_FF[0]:_IN_FF[1]],
                                  jnp.zeros((L, D_MODEL, N_REC - 1024 - N_HEADS), w_in.dtype)], -1).astype(BF16)
    W['w_gate_cat'] = full['w_gate'].transpose(0, 2, 1, 3).reshape(L, D_MODEL, N_BRANCH * D_MODEL).astype(BF16)
    W['b_gate'] = full['b_gate'].reshape(L, N_BRANCH * D_MODEL)
    for n in ('w_branch', 'w_out', 'w_ff1', 'w_ff2'):
        W[n] = full[n].astype(BF16)
    return W


def _grads_to_reference_layout(G):
    out = {'ln_in_g': G[('ln_in_g', -1)][0], 'ln_in_b': G[('ln_in_b', -1)][0]}
    st = lambda n: jnp.stack([G[(n, l)] for l in range(DEPTH)])
    g_att, g_rec = st('w_att'), st('w_rec')
    out['w_in'] = jnp.concatenate([g_att[..., :1536], g_rec[..., 1024:1024 + N_HEADS], g_rec[..., :1024],
                                   g_att[..., 1536:]], -1)
    out['w_gate'] = st('w_gate').transpose(0, 2, 1, 3, 4).reshape(DEPTH, N_BRANCH, D_MODEL, D_MODEL)
    out['w_branch'] = st('w_branch').transpose(0, 2, 3, 1, 4).reshape(DEPTH, N_BRANCH, BRANCH_WIDTH, D_MODEL)
    out['w_ff1'] = st('w_ff1').transpose(0, 2, 1, 3).reshape(DEPTH, D_MODEL, D_FF)
    out['w_ff2'] = st('w_ff2').reshape(DEPTH, D_FF, D_MODEL)
    out['w_out'] = st('w_out').reshape(DEPTH, D_MODEL, D_MODEL)
    out['b_gate'] = st('b_gate').reshape(DEPTH, N_BRANCH, D_MODEL)
    for n in ('ln1_g', 'ln1_b', 'ln2_g', 'ln2_b'):
        out[n] = st(n)[:, 0]
    for n in ('b_forget', 'conv_w', 'conv_b', 'w_r', 'b_r', 'w_i', 'b_i', 'lru_lambda', 'rel_bias'):
        out[n] = st(n)
    return out


HBM_SPEC = pl.BlockSpec(memory_space=pl.ANY)
N_CHIPS = 4
PACK_COLS = 1024


def _place():
    x, y, c = lax.axis_index("x"), lax.axis_index("y"), lax.axis_index("c")
    chips = [(1 - x, y), (x, 1 - y), (1 - x, 1 - y)]
    return x, y, c, chips


def _remote(src, dst, send_sems, recv_sems, k, to):
    return pltpu.make_async_remote_copy(src_ref=src, dst_ref=dst, send_sem=send_sems.at[k], recv_sem=recv_sems.at[k],
                                        device_id=to, device_id_type=MESH)


def _gather_layers(params, *, name):
    n = len(params)

    def body(*refs):
        ins, outs = refs[:n], refs[n:2 * n]
        send_sems, recv_sems = refs[2 * n:]
        x, y, c, chips = _place()
        me, sibling, k = (x, y, c), (x, y, 1 - c), 2 * x + y
        own = [_remote(ins[p], outs[p].at[k], send_sems, recv_sems, 6 * n + p, sibling) for p in range(n)]
        for cp in own:
            cp.start()
        first, passed = [], []
        for p in range(n):
            for j, (cx, cy) in enumerate(chips):
                cp = _remote(ins[p].at[c], outs[p].at[k, c], send_sems, recv_sems, 6 * p + j, (cx, cy, c))
                cp.start()
                first.append(cp)
        for p in range(n):
            for j, (cx, cy) in enumerate(chips):
                blk = outs[p].at[2 * cx + cy, c]
                _remote(blk, blk, send_sems, recv_sems, 6 * p + j, me).wait_recv()
                cp = _remote(blk, blk, send_sems, recv_sems, 6 * p + 3 + j, sibling)
                cp.start()
                passed.append(cp)
        for p in range(n):
            for j, (cx, cy) in enumerate(chips):
                blk = outs[p].at[2 * cx + cy, 1 - c]
                _remote(blk, blk, send_sems, recv_sems, 6 * p + 3 + j, me).wait_recv()
        for cp in first + passed:
            cp.wait_send()
        for cp in own:
            cp.wait()

    return pl.pallas_call(
        body, name=name, in_specs=[HBM_SPEC] * n, out_specs=[HBM_SPEC] * n,
        out_shape=[jax.ShapeDtypeStruct((N_CHIPS,) + a.shape, a.dtype) for a in params],
        scratch_shapes=[pltpu.SemaphoreType.DMA((7 * n,)), pltpu.SemaphoreType.DMA((7 * n,))],
    )(*params)


def _pair_exchange(g0, g1, *, name):
    n = len(g0)

    def body(*refs):
        a0, a1, outs = refs[:n], refs[n:2 * n], refs[2 * n:3 * n]
        send_sems, recv_sems = refs[3 * n:]
        x, y, c, _ = _place()
        sibling = (x, y, 1 - c)

        @pl.when(c == 0)
        def _():
            for p in range(n):
                _remote(a1[p], outs[p], send_sems, recv_sems, p, sibling).start()

        @pl.when(c == 1)
        def _():
            for p in range(n):
                _remote(a0[p], outs[p], send_sems, recv_sems, p, sibling).start()

        for p in range(n):
            _remote(a0[p], outs[p], send_sems, recv_sems, p, sibling).wait()

    return pl.pallas_call(
        body, name=name, in_specs=[HBM_SPEC] * (2 * n), out_specs=[HBM_SPEC] * n,
        out_shape=[jax.ShapeDtypeStruct(a.shape, a.dtype) for a in g0],
        scratch_shapes=[pltpu.SemaphoreType.DMA((n,)), pltpu.SemaphoreType.DMA((n,))],
    )(*g0, *g1)


def _chip_exchange(s, *, name):
    n = len(s)

    def body(*refs):
        ins, outs = refs[:n], refs[n:2 * n]
        send_sems, recv_sems = refs[2 * n:]
        x, y, c, chips = _place()
        k = 2 * x + y
        cps = [_remote(ins[p].at[2 * cx + cy], outs[p].at[k], send_sems, recv_sems, 3 * p + j, (cx, cy, c))
               for p in range(n) for j, (cx, cy) in enumerate(chips)]
        for cp in cps:
            cp.start()
        for p in range(n):
            for j, (cx, cy) in enumerate(chips):
                slot = outs[p].at[2 * cx + cy]
                _remote(slot, slot, send_sems, recv_sems, 3 * p + j, (x, y, c)).wait_recv()
        for cp in cps:
            cp.wait_send()

    return pl.pallas_call(
        body, name=name, in_specs=[HBM_SPEC] * n, out_specs=[HBM_SPEC] * n,
        out_shape=[jax.ShapeDtypeStruct(a.shape, a.dtype) for a in s],
        scratch_shapes=[pltpu.SemaphoreType.DMA((3 * n,)), pltpu.SemaphoreType.DMA((3 * n,))],
    )(*s)


def _pair_swap(r, *, name):
    n = len(r)

    def body(*refs):
        ins, outs = refs[:n], refs[n:2 * n]
        send_sems, recv_sems = refs[2 * n:]
        x, y, c, _ = _place()
        cps = [_remote(ins[p], outs[p], send_sems, recv_sems, p, (x, y, 1 - c)) for p in range(n)]
        for cp in cps:
            cp.start()
        for cp in cps:
            cp.wait()

    return pl.pallas_call(
        body, name=name, in_specs=[HBM_SPEC] * n, out_specs=[HBM_SPEC] * n,
        out_shape=[jax.ShapeDtypeStruct(a.shape, a.dtype) for a in r],
        scratch_shapes=[pltpu.SemaphoreType.DMA((n,)), pltpu.SemaphoreType.DMA((n,))],
    )(*r)


def _gather8(v, *, name):
    R, C = v.shape
    flips = [(bx, by, bc) for bx in (0, 1) for by in (0, 1) for bc in (0, 1)][1:]

    def body(v_ref, out_ref, send_sems, recv_sems, local_sem):
        x, y, c, _ = _place()
        flip = lambda a, b: 1 - a if b else a
        mine = out_ref.at[4 * x + 2 * y + c]
        local = pltpu.make_async_copy(v_ref, mine, local_sem)
        local.start()
        peers = [(flip(x, bx), flip(y, by), flip(c, bc)) for bx, by, bc in flips]
        cps = [_remote(v_ref, mine, send_sems, recv_sems, j, peer) for j, peer in enumerate(peers)]
        for cp in cps:
            cp.start()
        for j, (px, py, pc) in enumerate(peers):
            slot = out_ref.at[4 * px + 2 * py + pc]
            _remote(slot, slot, send_sems, recv_sems, j, (x, y, c)).wait_recv()
        for cp in cps:
            cp.wait_send()
        local.wait()

    return pl.pallas_call(
        body, name=name, in_specs=[HBM_SPEC], out_specs=HBM_SPEC,
        out_shape=jax.ShapeDtypeStruct((8, R, C), v.dtype),
        scratch_shapes=[pltpu.SemaphoreType.DMA((7,)), pltpu.SemaphoreType.DMA((7,)), pltpu.SemaphoreType.DMA],
    )(v)


def _row_block(rows, cols, limit=256 * 1024):
    if rows * cols <= limit:
        return rows
    for br in range(limit // cols // 8 * 8, 0, -8):
        if rows % br == 0:
            return br
    return rows


def _sum_slots(buf, *, name):
    n, R, C = buf.shape
    br = _row_block(R, C)

    def kern(b_ref, o_ref):
        acc = b_ref[0].astype(F32)
        for s in range(1, n):
            acc = acc + b_ref[s].astype(F32)
        o_ref[...] = acc

    return pl.pallas_call(
        kern, name=name, grid=(pl.cdiv(R, br),),
        in_specs=[pl.BlockSpec((n, br, C), lambda i: (0, i, 0))],
        out_specs=pl.BlockSpec((br, C), lambda i: (i, 0)),
        out_shape=jax.ShapeDtypeStruct((R, C), F32),
        compiler_params=_cp(("arbitrary",)),
    )(buf)


def _scalar(s):
    return jnp.reshape(s, (1,)).astype(jnp.int32)


def _sum_pair(g0, g1, other, c, *, name):
    _, R, C = g0.shape
    br = _row_block(R, C)

    def kern(c_ref, g0_ref, g1_ref, o_ref, out_ref):
        own = jnp.where(c_ref[0] == 0, g0_ref[...], g1_ref[...])
        out_ref[...] = (own + o_ref[...]).astype(out_ref.dtype)

    blk = (None, br, C)
    return pl.pallas_call(
        kern, name=name,
        grid_spec=pltpu.PrefetchScalarGridSpec(
            num_scalar_prefetch=1, grid=(N_CHIPS, R // br),
            in_specs=[pl.BlockSpec(blk, lambda k, i, cr: (k, i * (1 - cr[0]), 0)),
                      pl.BlockSpec(blk, lambda k, i, cr: (k, i * cr[0], 0)),
                      pl.BlockSpec(blk, lambda k, i, cr: (k, i, 0))],
            out_specs=pl.BlockSpec(blk, lambda k, i, cr: (k, i, 0))),
        out_shape=jax.ShapeDtypeStruct((N_CHIPS, R, C), BF16),
        compiler_params=_cp(("arbitrary", "arbitrary")),
    )(_scalar(c), g0, g1, other)


def _sum_chips(s, got, k, *, name):
    _, R, C = s.shape
    br = _row_block(R, C)

    def kern(k_ref, s_ref, a_ref, b_ref, c_ref, out_ref):
        out_ref[...] = ((s_ref[...].astype(F32) + a_ref[...].astype(F32)) + b_ref[...].astype(F32)) \
            + c_ref[...].astype(F32)

    blk = (None, br, C)
    peer = lambda d: pl.BlockSpec(blk, lambda i, kr: ((kr[0] + d) % N_CHIPS, i, 0))
    return pl.pallas_call(
        kern, name=name,
        grid_spec=pltpu.PrefetchScalarGridSpec(
            num_scalar_prefetch=1, grid=(R // br,),
            in_specs=[peer(0), peer(1), peer(2), peer(3)],
            out_specs=pl.BlockSpec((br, C), lambda i, kr: (i, 0))),
        out_shape=jax.ShapeDtypeStruct((R, C), F32),
        compiler_params=_cp(("arbitrary",)),
    )(_scalar(k), s, got, got, got)


def _adam_math(w, g, m, v):
    nm = ADAM_B1 * m + (1.0 - ADAM_B1) * g
    nv = ADAM_B2 * v + (1.0 - ADAM_B2) * jnp.square(g)
    m_hat = nm / (1.0 - ADAM_B1 ** ADAM_STEP)
    v_hat = nv / (1.0 - ADAM_B2 ** ADAM_STEP)
    return -ADAM_LR * (m_hat / (jnp.sqrt(v_hat) + ADAM_EPS) + ADAM_WD * w), nm, nv


def _adamw_layers(w, mine, theirs, m, v, c, *, name):
    shape = w.shape
    R, C = mine.shape
    w3, m3, v3 = (a.reshape(DEPTH, R, C) for a in (w, m, v))
    br = _row_block(R, C)

    def kern(c_ref, w_ref, a_ref, b_ref, m_ref, v_ref, g_ref, d_ref, nm_ref, nv_ref):
        g = jnp.where(pl.program_id(0) == c_ref[0], a_ref[...], b_ref[...])
        g_ref[...] = g
        d_ref[...], nm_ref[...], nv_ref[...] = _adam_math(w_ref[...], g, m_ref[...], v_ref[...])

    lay = pl.BlockSpec((None, br, C), lambda l, i, cr: (l, i, 0))
    outs = pl.pallas_call(
        kern, name=name,
        grid_spec=pltpu.PrefetchScalarGridSpec(
            num_scalar_prefetch=1, grid=(DEPTH, R // br),
            in_specs=[lay,
                      pl.BlockSpec((br, C), lambda l, i, cr: (jnp.where(l == cr[0], i, 0), 0)),
                      pl.BlockSpec((br, C), lambda l, i, cr: (jnp.where(l == cr[0], 0, i), 0)),
                      lay, lay],
            out_specs=[lay] * 4),
        out_shape=[jax.ShapeDtypeStruct((DEPTH, R, C), F32)] * 4,
        compiler_params=_cp(("arbitrary", "arbitrary")),
    )(_scalar(c), w3, mine, theirs, m3, v3)
    return [o.reshape(shape) for o in outs]


def _adamw(w, g, m, v, *, name):
    shape = w.shape
    cols = shape[-1]
    w2, g2, m2, v2 = (a.reshape(-1, cols) for a in (w, g, m, v))
    rows = w2.shape[0]
    br = _row_block(rows, cols)

    def kern(w_ref, g_ref, m_ref, v_ref, d_ref, nm_ref, nv_ref):
        gv = g_ref[...]
        nm = ADAM_B1 * m_ref[...] + (1.0 - ADAM_B1) * gv
        nv = ADAM_B2 * v_ref[...] + (1.0 - ADAM_B2) * jnp.square(gv)
        m_hat = nm / (1.0 - ADAM_B1 ** ADAM_STEP)
        v_hat = nv / (1.0 - ADAM_B2 ** ADAM_STEP)
        d_ref[...] = -ADAM_LR * (m_hat / (jnp.sqrt(v_hat) + ADAM_EPS) + ADAM_WD * w_ref[...])
        nm_ref[...] = nm
        nv_ref[...] = nv

    spec = pl.BlockSpec((br, cols), lambda i: (i, 0))
    outs = pl.pallas_call(
        kern, name=name, grid=(rows // br,), in_specs=[spec] * 4, out_specs=[spec] * 3,
        out_shape=[jax.ShapeDtypeStruct((rows, cols), F32)] * 3,
        compiler_params=_cp(("arbitrary",)),
    )(w2, g2, m2, v2)
    return [o.reshape(shape) for o in outs]


_NAMES = ['ln_in_g', 'ln_in_b', 'w_in', 'b_forget', 'conv_w', 'conv_b', 'w_r', 'b_r', 'w_i', 'b_i', 'lru_lambda',
          'rel_bias', 'w_branch', 'w_gate', 'b_gate', 'w_out', 'ln1_g', 'ln1_b', 'w_ff1', 'w_ff2', 'ln2_g', 'ln2_b']
_BIG = {'w_in': 2, 'w_branch': 3, 'w_gate': 2, 'w_out': 1, 'w_ff1': 2, 'w_ff2': 1}
_SMALL_SHARDED = {'b_gate': 2, 'conv_w': 2, 'rel_bias': 2}
_SHARDED = {**_BIG, **_SMALL_SHARDED}
_REPLICATED = [n for n in _NAMES if n not in _SHARDED]
_TILE = 8 * LANES


def _tiles(a, cols):
    flat = a.reshape(-1)
    per = 8 * cols
    flat = jnp.pad(flat, (0, (-flat.shape[0]) % per))
    return flat.reshape(-1, cols)


def _pack(arrs, cols):
    return jnp.concatenate([_tiles(a, cols) for a in arrs], axis=0)


def _unpack(packed, like, cols):
    out, r0 = [], 0
    for a in like:
        n = math.prod(a.shape)
        rows = -(-n // (8 * cols)) * 8
        out.append(packed[r0:r0 + rows].reshape(-1)[:n].reshape(a.shape))
        r0 += rows
    return out


def _chip_major(G, l):
    g_att, g_rec = G[('w_att', l)], G[('w_rec', l)]
    w_in = jnp.concatenate([g_att[:, :1536], g_rec[:, 1024:1024 + N_HEADS], g_rec[:, :1024], g_att[:, 1536:]], -1)
    big = [w_in.reshape(D_MODEL, N_CHIPS, D_IN // N_CHIPS).transpose(1, 0, 2),
           G[('w_branch', l)].reshape(N_CHIPS, N_BRANCH * BRANCH_WIDTH, BRANCH_WIDTH),
           G[('w_gate', l)].reshape(N_CHIPS, N_BRANCH * (D_MODEL // N_CHIPS), D_MODEL),
           G[('w_out', l)], G[('w_ff1', l)], G[('w_ff2', l)]]
    per_chip = lambda g, rows: g.reshape(rows, N_CHIPS, -1).transpose(1, 0, 2)
    bg = per_chip(G[('b_gate', l)], N_BRANCH)
    cw = per_chip(G[('conv_w', l)], CONV_WIDTH)
    rb = per_chip(G[('rel_bias', l)], N_HEADS)
    small = jnp.stack([_pack([bg[j], cw[j], rb[j]], LANES) for j in range(N_CHIPS)])
    return big + [small]


def _unshard(blocks, axis):
    return jnp.concatenate([blocks[k] for k in range(N_CHIPS)], axis=axis)


def kernel(x, ln_in_g, ln_in_b, w_in, b_forget, conv_w, conv_b, w_r, b_r, w_i, b_i, lru_lambda, rel_bias, w_branch, w_gate, b_gate, w_out, ln1_g, ln1_b, w_ff1, w_ff2, ln2_g, ln2_b, loss_target, m_ln_in_g, m_ln_in_b, m_w_in, m_b_forget, m_conv_w, m_conv_b, m_w_r, m_b_r, m_w_i, m_b_i, m_lru_lambda, m_rel_bias, m_w_branch, m_w_gate, m_b_gate, m_w_out, m_ln1_g, m_ln1_b, m_w_ff1, m_w_ff2, m_ln2_g, m_ln2_b, v_ln_in_g, v_ln_in_b, v_w_in, v_b_forget, v_conv_w, v_conv_b, v_w_r, v_b_r, v_w_i, v_b_i, v_lru_lambda, v_rel_bias, v_w_branch, v_w_gate, v_b_gate, v_w_out, v_ln1_g, v_ln1_b, v_w_ff1, v_w_ff2, v_ln2_g, v_ln2_b):
    w = dict(zip(_NAMES, (ln_in_g, ln_in_b, w_in, b_forget, conv_w, conv_b, w_r, b_r, w_i, b_i, lru_lambda, rel_bias,
                          w_branch, w_gate, b_gate, w_out, ln1_g, ln1_b, w_ff1, w_ff2, ln2_g, ln2_b)))
    m = dict(zip(_NAMES, (m_ln_in_g, m_ln_in_b, m_w_in, m_b_forget, m_conv_w, m_conv_b, m_w_r, m_b_r, m_w_i, m_b_i,
                          m_lru_lambda, m_rel_bias, m_w_branch, m_w_gate, m_b_gate, m_w_out, m_ln1_g, m_ln1_b,
                          m_w_ff1, m_w_ff2, m_ln2_g, m_ln2_b)))
    v = dict(zip(_NAMES, (v_ln_in_g, v_ln_in_b, v_w_in, v_b_forget, v_conv_w, v_conv_b, v_w_r, v_b_r, v_w_i, v_b_i,
                          v_lru_lambda, v_rel_bias, v_w_branch, v_w_gate, v_b_gate, v_w_out, v_ln1_g, v_ln1_b,
                          v_w_ff1, v_w_ff2, v_ln2_g, v_ln2_b)))
    c = lax.axis_index("c")

    small_like = [w[n] for n in _SMALL_SHARDED]
    small_pack = jnp.stack([_pack([a[l] for a in small_like], LANES) for l in range(DEPTH)])
    gathered = _gather_layers([w[n].astype(BF16) for n in _BIG] + [small_pack], name='gather_weights')
    full = {n: w[n] for n in _REPLICATED}
    for n, blocks in zip(_BIG, gathered):
        full[n] = _unshard(blocks, _BIG[n])
    small_blocks = [[_unpack(gathered[-1][k, l], [a[l] for a in small_like], LANES) for l in range(DEPTH)]
                    for k in range(N_CHIPS)]
    for i, n in enumerate(_SMALL_SHARDED):
        full[n] = jnp.concatenate([jnp.stack([small_blocks[k][l][i] for l in range(DEPTH)])
                                   for k in range(N_CHIPS)], axis=_SMALL_SHARDED[n])

    loss, gx, G = _device_step(x[0], loss_target[0], _prep_weights(full))
    k = 2 * lax.axis_index("x") + lax.axis_index("y")

    by_layer = [_chip_major(G, l) for l in range(DEPTH)]
    from_sibling = _pair_exchange(*by_layer, name='grad_pair_exchange')
    pair_sum = [_sum_pair(a0, a1, o, c, name='grad_pair_sum') for a0, a1, o in zip(*by_layer, from_sibling)]
    from_chips = _chip_exchange(pair_sum, name='grad_chip_exchange')
    mine = [_sum_chips(s, got, k, name='grad_chip_sum') for s, got in zip(pair_sum, from_chips)]
    theirs = _pair_swap(mine, name='grad_pair_swap')

    rep_like = [w[n] for n in _REPLICATED]
    g_rep_dev = {'ln_in_g': G[('ln_in_g', -1)][0], 'ln_in_b': G[('ln_in_b', -1)][0]}
    for n in _REPLICATED[2:]:
        g_rep_dev[n] = jnp.stack([G[(n, l)].reshape(w[n].shape[1:]) for l in range(DEPTH)])
    rep_all = _gather8(_pack([g_rep_dev[n] for n in _REPLICATED], LANES), name='grad_gather8')
    g_rep = dict(zip(_REPLICATED, _unpack(_sum_slots(rep_all, name='grad_sum8'), rep_like, LANES)))

    grads, delta, new_m, new_v = {}, {}, {}, {}
    for n, a, b in zip(_BIG, mine, theirs):
        grads[n], delta[n], new_m[n], new_v[n] = _adamw_layers(w[n], a, b, m[n], v[n], c, name='adamw')
    small_layers = [jnp.where(c == l, mine[-1], theirs[-1]) for l in range(DEPTH)]
    small_shards = [_unpack(s, [w[n][0] for n in _SMALL_SHARDED], LANES) for s in small_layers]
    g_shard = {n: jnp.stack([small_shards[l][i] for l in range(DEPTH)]) for i, n in enumerate(_SMALL_SHARDED)}
    small = _REPLICATED + list(_SMALL_SHARDED)
    for n in small:
        grads[n] = g_rep[n] if n in g_rep else g_shard[n]
    packs = [_pack([d[n] for n in small], LANES) for d in (w, grads, m, v)]
    outs = _adamw(*packs, name='adamw_small')
    small_like_all = [w[n] for n in small]
    for d, o in zip((delta, new_m, new_v), outs):
        d.update(zip(small, _unpack(o, small_like_all, LANES)))

    loss = lax.psum(loss, ("x", "y", "c"))
    return (loss, gx[None], *[grads[n] for n in _NAMES], *[delta[n] for n in _NAMES],
            *[new_m[n] for n in _NAMES], *[new_v[n] for n in _NAMES])
```

```python
import functools
import math

import jax
import jax.numpy as jnp
from jax import lax
from jax.experimental import pallas as pl
from jax.experimental.pallas import tpu as pltpu

F32 = jnp.float32
BF16 = jnp.bfloat16

D_MODEL = 2048
DEPTH = 2
CHUNK = 64
HEAD_DIM = 128
N_BRANCH = 4
BRANCH_WIDTH = 512
N_HEADS = 4
CONV_WIDTH = 4
LRU_C = 8.0
LOOKBACK_CHUNKS = 8
BAND = (LOOKBACK_CHUNKS + 1) * CHUNK
PAD_ROWS = LOOKBACK_CHUNKS * CHUNK
REL_CLIP = 256
REL_TABLE = REL_CLIP + CHUNK
REL_PAD = 384
D_FF = 4 * D_MODEL
D_IN = 5636
ALPHA = (2.0 * DEPTH) ** 0.25
LN_EPS = 1e-5
SCALE = HEAD_DIM ** -0.5

ADAM_LR = 0.001
ADAM_B1 = 0.9
ADAM_B2 = 0.999
ADAM_EPS = 1e-08
ADAM_WD = 0.01
ADAM_STEP = 10

N_ATT = 9 * BRANCH_WIDTH
N_REC = 2 * BRANCH_WIDTH + 128

V7X_VMEM_LIMIT = 56 * 1024 * 1024
LANES = 128
ATT_BLOCK = 256
ATT_KEYS = 1024

NT = (((1,), (1,)), ((), ()))
TN = (((0,), (0,)), ((), ()))
NN = (((1,), (0,)), ((), ()))

MESH = pl.DeviceIdType.MESH


def _cp(sem=None):
    return pltpu.CompilerParams(dimension_semantics=sem, vmem_limit_bytes=V7X_VMEM_LIMIT)


def _dot(a, b, dims=NN):
    return lax.dot_general(a, b, dims, preferred_element_type=F32)


def _pick(n, prefs):
    for p in prefs:
        if n % p == 0:
            return p
    return n


def _split3(x):
    hi = x.astype(BF16)
    r1 = x - hi.astype(F32)
    mid = r1.astype(BF16)
    lo = (r1 - mid.astype(F32)).astype(BF16)
    return hi, mid, lo


def _split2(x):
    hi = x.astype(BF16)
    lo = (x - hi.astype(F32)).astype(BF16)
    return hi, lo


def _sigmoid(z):
    return 1.0 / (1.0 + jnp.exp(-z))


def _log_sigmoid(z):
    return jnp.minimum(z, 0.0) - jnp.log(1.0 + jnp.exp(-jnp.abs(z)))


def _mm(a, b, *, name, ta=False, tb=False, out_dtypes=(F32,), epilogue=None, extras=(),
        bm=None, bn=None, bk=None, out_map=None):
    M, K = (a.shape[1], a.shape[0]) if ta else a.shape
    N = b.shape[0] if tb else b.shape[1]
    bm = bm or _pick(M, (1024, 512, 256, 128))
    bn = bn or _pick(N, (1024, 1536, 1152, 512, 256, 128))
    bk = bk or _pick(K, (2048, 1536, 1024, 1152, 512, 256, 128))
    nk = K // bk
    a_spec = pl.BlockSpec((bk, bm), lambda i, j, k: (k, i)) if ta else pl.BlockSpec((bm, bk), lambda i, j, k: (i, k))
    b_spec = pl.BlockSpec((bn, bk), lambda i, j, k: (j, k)) if tb else pl.BlockSpec((bk, bn), lambda i, j, k: (k, j))
    ex_specs = [pl.BlockSpec((bm, bn), lambda i, j, k: (i, j)) if kind == 'mn'
                else pl.BlockSpec((1, bn), lambda i, j, k: (0, j)) for _, kind in extras]
    n_ex, n_out = len(extras), len(out_dtypes)
    dims = TN if ta else (NT if tb else NN)

    def kern(*refs):
        a_ref, b_ref = refs[0], refs[1]
        ex_refs = refs[2:2 + n_ex]
        out_refs = refs[2 + n_ex:2 + n_ex + n_out]
        acc_ref = refs[-1]
        k = pl.program_id(2)
        part = _dot(a_ref[...].astype(BF16), b_ref[...].astype(BF16), dims)

        @pl.when(k == 0)
        def _():
            acc_ref[...] = part

        @pl.when(k > 0)
        def _():
            acc_ref[...] += part

        @pl.when(k == nk - 1)
        def _():
            acc = acc_ref[...]
            outs = (acc,) if epilogue is None else epilogue(acc, *[r[...] for r in ex_refs])
            for o_ref, o in zip(out_refs, outs):
                o_ref[...] = o.astype(o_ref.dtype).reshape(o_ref.shape)

    if out_map is None:
        out_specs = [pl.BlockSpec((bm, bn), lambda i, j, k: (i, j)) for _ in out_dtypes]
        out_shape = [jax.ShapeDtypeStruct((M, N), dt) for dt in out_dtypes]
    else:
        shape, block, index = out_map
        out_specs = [pl.BlockSpec(block, lambda i, j, k: index(i, j))]
        out_shape = [jax.ShapeDtypeStruct(shape, out_dtypes[0])]
    res = pl.pallas_call(
        kern, name=name, grid=(M // bm, N // bn, nk),
        in_specs=[a_spec, b_spec] + ex_specs,
        out_specs=out_specs,
        out_shape=out_shape,
        scratch_shapes=[pltpu.VMEM((bm, bn), F32)],
        compiler_params=_cp(("parallel", "parallel", "arbitrary")),
    )(a, b, *[e for e, _ in extras])
    return res[0] if n_out == 1 else res


def _ln_fwd(h, g, b, *, name):
    T, D = h.shape
    bt = _pick(T, (512, 256, 128))

    def kern(h_ref, g_ref, b_ref, y_ref, yb_ref):
        x = h_ref[...]
        mu = jnp.mean(x, axis=-1, keepdims=True)
        xc = x - mu
        var = jnp.mean(xc * xc, axis=-1, keepdims=True)
        y = xc * lax.rsqrt(var + LN_EPS) * g_ref[...] + b_ref[...]
        y_ref[...] = y
        yb_ref[...] = y.astype(BF16)

    row = pl.BlockSpec((bt, D), lambda i: (i, 0))
    vec = pl.BlockSpec((1, D), lambda i: (0, 0))
    return pl.pallas_call(
        kern, name=name, grid=(T // bt,), in_specs=[row, vec, vec], out_specs=[row, row],
        out_shape=[jax.ShapeDtypeStruct((T, D), F32), jax.ShapeDtypeStruct((T, D), BF16)],
        compiler_params=_cp(("arbitrary",)),
    )(h, g.reshape(1, D), b.reshape(1, D))


def _ln_bwd(h, dy, g, *, name):
    T, D = h.shape
    bt = _pick(T, (512, 256, 128))

    def kern(h_ref, dy_ref, g_ref, dh_ref, dhb_ref, dg_ref, db_ref):
        i = pl.program_id(0)
        x = h_ref[...]
        dyv = dy_ref[...]
        mu = jnp.mean(x, axis=-1, keepdims=True)
        xc = x - mu
        var = jnp.mean(xc * xc, axis=-1, keepdims=True)
        rstd = lax.rsqrt(var + LN_EPS)
        xhat = xc * rstd
        dxh = dyv * g_ref[...]
        m1 = jnp.mean(dxh, axis=-1, keepdims=True)
        m2 = jnp.mean(dxh * xhat, axis=-1, keepdims=True)
        dh = rstd * (dxh - m1 - xhat * m2)
        dh_ref[...] = dh
        dhb_ref[...] = dh.astype(BF16)
        pg = jnp.sum(dyv * xhat, axis=0, keepdims=True)
        pb = jnp.sum(dyv, axis=0, keepdims=True)

        @pl.when(i == 0)
        def _():
            dg_ref[...] = pg
            db_ref[...] = pb

        @pl.when(i > 0)
        def _():
            dg_ref[...] += pg
            db_ref[...] += pb

    row = pl.BlockSpec((bt, D), lambda i: (i, 0))
    vec = pl.BlockSpec((1, D), lambda i: (0, 0))
    return pl.pallas_call(
        kern, name=name, grid=(T // bt,), in_specs=[row, row, vec], out_specs=[row, row, vec, vec],
        out_shape=[jax.ShapeDtypeStruct((T, D), F32), jax.ShapeDtypeStruct((T, D), BF16),
                   jax.ShapeDtypeStruct((1, D), F32), jax.ShapeDtypeStruct((1, D), F32)],
        compiler_params=_cp(("arbitrary",)),
    )(h, dy, g.reshape(1, D))


def _loss_head(y, tgt, *, name):
    T, D = y.shape
    bt = _pick(T, (512, 256, 128))

    def kern(y_ref, t_ref, dy_ref, loss_ref):
        i = pl.program_id(0)
        e = y_ref[...] - t_ref[...]
        dy_ref[...] = e * (1.0 / D)
        part = 0.5 * jnp.sum(jnp.sum(e * e, axis=-1, keepdims=True) * (1.0 / D), axis=0, keepdims=True)
        part = jnp.broadcast_to(part, (8, LANES))

        @pl.when(i == 0)
        def _():
            loss_ref[...] = part

        @pl.when(i > 0)
        def _():
            loss_ref[...] += part

    row = pl.BlockSpec((bt, D), lambda i: (i, 0))
    return pl.pallas_call(
        kern, name=name, grid=(T // bt,), in_specs=[row, row],
        out_specs=[row, pl.BlockSpec((8, LANES), lambda i: (0, 0))],
        out_shape=[jax.ShapeDtypeStruct((T, D), F32), jax.ShapeDtypeStruct((8, LANES), F32)],
        compiler_params=_cp(("arbitrary",)),
    )(y, tgt)


def _tri(n, upper):
    r = lax.broadcasted_iota(jnp.int32, (n, n), 0)
    c = lax.broadcasted_iota(jnp.int32, (n, n), 1)
    return jnp.where((c >= r) if upper else (c <= r), 1.0, 0.0).astype(BF16)


def _forget_fwd(ff, bf, *, name):
    T = ff.shape[0]
    bt = 256

    def kern(ff_ref, bf_ref, out_ref, carry):
        i = pl.program_id(0)

        @pl.when(i == 0)
        def _():
            carry[...] = jnp.zeros_like(carry)

        ls = _log_sigmoid(ff_ref[...] + bf_ref[...])
        tri = _tri(bt, upper=False)
        hi, mid, lo = _split3(ls)
        cs = _dot(tri, hi) + _dot(tri, mid) + _dot(tri, lo) + carry[0:1, :]
        out_ref[...] = cs
        carry[...] = jnp.broadcast_to(cs[bt - 1:bt, :], carry.shape)

    return pl.pallas_call(
        kern, name=name, grid=(T // bt,),
        in_specs=[pl.BlockSpec((bt, LANES), lambda i: (i, 0)), pl.BlockSpec((1, LANES), lambda i: (0, 0))],
        out_specs=pl.BlockSpec((bt, LANES), lambda i: (i, 0)),
        out_shape=jax.ShapeDtypeStruct((T, LANES), F32),
        scratch_shapes=[pltpu.VMEM((8, LANES), F32)],
        compiler_params=_cp(("arbitrary",)),
    )(ff, bf)


def _forget_bwd(dFk, dFq, ff, bf, *, name):
    T = ff.shape[0]
    bt = 256
    nb = T // bt

    def kern(dFk_ref, dFq_ref, ff_ref, bf_ref, dff_ref, dbf_ref, carry):
        i = pl.program_id(0)

        @pl.when(i == 0)
        def _():
            carry[...] = jnp.zeros_like(carry)
            dbf_ref[...] = jnp.zeros_like(dbf_ref)

        tri = _tri(bt, upper=True)
        hi, mid, lo = _split3(dFk_ref[...] + dFq_ref[...])
        rs = _dot(tri, hi) + _dot(tri, mid) + _dot(tri, lo) + carry[0:1, :]
        carry[...] = jnp.broadcast_to(rs[0:1, :], carry.shape)
        z = ff_ref[...] + bf_ref[...]
        dff = rs * _sigmoid(-z)
        dff_ref[...] = dff.astype(dff_ref.dtype)
        dbf_ref[...] += jnp.sum(dff, axis=0, keepdims=True)

    rev = pl.BlockSpec((bt, LANES), lambda i: (nb - 1 - i, 0))
    vec = pl.BlockSpec((1, LANES), lambda i: (0, 0))
    return pl.pallas_call(
        kern, name=name, grid=(nb,), in_specs=[rev, rev, rev, vec], out_specs=[rev, vec],
        out_shape=[jax.ShapeDtypeStruct((T, LANES), BF16), jax.ShapeDtypeStruct((1, LANES), F32)],
        scratch_shapes=[pltpu.VMEM((8, LANES), F32)],
        compiler_params=_cp(("arbitrary",)),
    )(dFk, dFq, ff, bf)


def _head_lane(x, h):
    lane = lax.broadcasted_iota(jnp.int32, x.shape, 1)
    return jnp.sum(jnp.where(lane == h, x, 0.0), axis=1, keepdims=True)


def _att_blocks(T):
    return min(ATT_BLOCK, T), min(ATT_KEYS, T)


def _positions(i, j, bq, bk):
    r = i * bq + lax.broadcasted_iota(jnp.int32, (bq, bk), 0)
    c = j * bk + lax.broadcasted_iota(jnp.int32, (bq, bk), 1)
    return r, c


def _fox_fwd(u_att, fcum, frow, *, name):
    T = u_att.shape[0]
    bq, bk = _att_blocks(T)
    nq, nk = T // bq, T // bk
    H = N_HEADS

    def kern(q_ref, k_ref, v_ref, fc_ref, fr_ref, o_ref, lse_ref):
        i = pl.program_id(1)
        q = q_ref[...]
        fq = _head_lane(fc_ref[...], pl.program_id(0))

        def step(j, carry, masked):
            m, l, acc = carry
            off = pl.multiple_of(j * bk, bk)
            k = k_ref[pl.ds(off, bk), :]
            v = v_ref[pl.ds(off, bk), :]
            s = _dot(q, k, NT) * SCALE + (fq - fr_ref[j])
            if masked:
                r, c = _positions(i, j, bq, bk)
                s = jnp.where(c <= r, s, -jnp.inf)
            m_new = jnp.maximum(m, jnp.max(s, axis=1, keepdims=True))
            a = jnp.exp(m - m_new)
            p = jnp.exp(s - m_new)
            l = a * l + jnp.sum(p, axis=1, keepdims=True)
            acc = a * acc + _dot(p.astype(BF16), v)
            return m_new, l, acc

        init = (jnp.full((bq, 1), -1e30, F32), jnp.zeros((bq, 1), F32), jnp.zeros((bq, HEAD_DIM), F32))
        nfull = (i * bq) // bk
        carry = lax.fori_loop(0, nfull, lambda j, cr: step(j, cr, False), init)
        m, l, acc = step(nfull, carry, True)
        o_ref[...] = (acc / l).astype(o_ref.dtype)
        lse_ref[...] = m + jnp.log(l)

    return pl.pallas_call(
        kern, name=name, grid=(H, nq),
        in_specs=[pl.BlockSpec((bq, HEAD_DIM), lambda h, i: (i, h)),
                  pl.BlockSpec((T, HEAD_DIM), lambda h, i: (0, 4 + h)),
                  pl.BlockSpec((T, HEAD_DIM), lambda h, i: (0, 8 + h)),
                  pl.BlockSpec((bq, LANES), lambda h, i: (i, 0)),
                  pl.BlockSpec((None, nk, 1, bk), lambda h, i: (h, 0, 0, 0))],
        out_specs=[pl.BlockSpec((bq, HEAD_DIM), lambda h, i: (i, h)),
                   pl.BlockSpec((None, bq, 1), lambda h, i: (h, i, 0))],
        out_shape=[jax.ShapeDtypeStruct((T, BRANCH_WIDTH), BF16), jax.ShapeDtypeStruct((H, T, 1), F32)],
        compiler_params=_cp(("parallel", "arbitrary")),
    )(u_att, u_att, u_att, fcum, frow)


def _row_dot(a, b, *, name):
    T = a.shape[0]
    bt = _pick(T, (512, 256, 128))

    def kern(a_ref, b_ref, o_ref):
        p = a_ref[...].astype(F32) * b_ref[...].astype(F32)
        for h in range(N_HEADS):
            o_ref[h] = jnp.sum(p[:, h * HEAD_DIM:(h + 1) * HEAD_DIM], axis=1, keepdims=True)

    row = pl.BlockSpec((bt, BRANCH_WIDTH), lambda i: (i, 0))
    return pl.pallas_call(
        kern, name=name, grid=(T // bt,), in_specs=[row, row],
        out_specs=pl.BlockSpec((N_HEADS, bt, 1), lambda i: (0, i, 0)),
        out_shape=jax.ShapeDtypeStruct((N_HEADS, T, 1), F32),
        compiler_params=_cp(("arbitrary",)),
    )(a, b)


def _fox_bwd(u_att, do, lse, delta, fcum, frow, *, name):
    T = u_att.shape[0]
    bq, bk = _att_blocks(T)
    nq, nk = T // bq, T // bk
    H = N_HEADS

    def kern(q_ref, k_ref, v_ref, do_ref, lse_ref, dl_ref, fc_ref, fr_ref,
             dq_ref, dk_ref, dv_ref, df_ref, dfq_ref, dk_acc, dv_acc, df_acc):
        i = pl.program_id(1)

        @pl.when(i == 0)
        def _():
            dk_acc[...] = jnp.zeros_like(dk_acc)
            dv_acc[...] = jnp.zeros_like(dv_acc)
            df_acc[...] = jnp.zeros_like(df_acc)

        q = q_ref[...]
        dov = do_ref[...]
        fq = _head_lane(fc_ref[...], pl.program_id(0))
        lsev = lse_ref[...]
        dlt = dl_ref[...]

        def step(j, carry, masked):
            dq, dfq = carry
            off = pl.multiple_of(j * bk, bk)
            k = k_ref[pl.ds(off, bk), :]
            v = v_ref[pl.ds(off, bk), :]
            s = _dot(q, k, NT) * SCALE + (fq - fr_ref[j])
            p = jnp.exp(s - lsev)
            if masked:
                r, c = _positions(i, j, bq, bk)
                p = jnp.where(c <= r, p, 0.0)
            dp = _dot(dov, v, NT)
            ds = p * (dp - dlt)
            dsb = ds.astype(BF16)
            dq = dq + _dot(dsb, k)
            dk_acc[pl.ds(off, bk), :] += _dot(dsb, q, TN)
            dv_acc[pl.ds(off, bk), :] += _dot(p.astype(BF16), dov, TN)
            df_acc[j] += -jnp.sum(ds, axis=0, keepdims=True)
            return dq, dfq + jnp.sum(ds, axis=1, keepdims=True)

        nfull = (i * bq) // bk
        carry = lax.fori_loop(0, nfull, lambda j, cr: step(j, cr, False),
                              (jnp.zeros((bq, HEAD_DIM), F32), jnp.zeros((bq, 1), F32)))
        dq, dfq = step(nfull, carry, True)
        dq_ref[...] = (dq * SCALE).astype(dq_ref.dtype)
        dfq_ref[...] = dfq

        @pl.when(i == nq - 1)
        def _():
            dk_ref[...] = (dk_acc[...] * SCALE).astype(dk_ref.dtype)
            dv_ref[...] = dv_acc[...].astype(dv_ref.dtype)
            df_ref[...] = df_acc[...]

    col = lambda: pl.BlockSpec((None, bq, 1), lambda h, i: (h, i, 0))
    return pl.pallas_call(
        kern, name=name, grid=(H, nq),
        in_specs=[pl.BlockSpec((bq, HEAD_DIM), lambda h, i: (i, h)),
                  pl.BlockSpec((T, HEAD_DIM), lambda h, i: (0, 4 + h)),
                  pl.BlockSpec((T, HEAD_DIM), lambda h, i: (0, 8 + h)),
                  pl.BlockSpec((bq, HEAD_DIM), lambda h, i: (i, h)),
                  col(), col(), pl.BlockSpec((bq, LANES), lambda h, i: (i, 0)),
                  pl.BlockSpec((None, nk, 1, bk), lambda h, i: (h, 0, 0, 0))],
        out_specs=[pl.BlockSpec((bq, HEAD_DIM), lambda h, i: (i, h)),
                   pl.BlockSpec((T, HEAD_DIM), lambda h, i: (0, h)),
                   pl.BlockSpec((T, HEAD_DIM), lambda h, i: (0, h)),
                   pl.BlockSpec((None, nk, 1, bk), lambda h, i: (h, 0, 0, 0)),
                   pl.BlockSpec((None, bq, 1), lambda h, i: (h, i, 0))],
        out_shape=[jax.ShapeDtypeStruct((T, BRANCH_WIDTH), BF16)] * 3
                  + [jax.ShapeDtypeStruct((H, nk, 1, bk), F32), jax.ShapeDtypeStruct((H, T, 1), F32)],
        scratch_shapes=[pltpu.VMEM((T, HEAD_DIM), F32), pltpu.VMEM((T, HEAD_DIM), F32),
                        pltpu.VMEM((nk, 1, bk), F32)],
        compiler_params=_cp(("parallel", "arbitrary")),
    )(u_att, u_att, u_att, do, lse, delta, fcum, frow)


def _softplus_parts(z):
    t = jnp.exp(-jnp.abs(z))
    sp = jnp.maximum(z, 0.0) + jnp.log(1.0 + t)
    return t, sp


def _sb_tri(B):
    r = lax.broadcasted_iota(jnp.int32, (B, B), 0)
    c = lax.broadcasted_iota(jnp.int32, (B, B), 1)
    suffix = jnp.where(r >= c, 1.0, 0.0).astype(BF16)
    prefix = jnp.where(r <= c, 1.0, 0.0).astype(BF16)
    return suffix, prefix


def _sb_fwd(u_att, *, name):
    T = u_att.shape[0]
    B, bk = _att_blocks(T)
    nq, nsub = T // B, bk // B
    H = N_HEADS

    def kern(q_ref, k_ref, v_ref, o_ref):
        i = pl.program_id(1)
        q = q_ref[...]
        suffix, _ = _sb_tri(B)

        def step(j, carry, masked):
            run, acc = carry
            parts = []
            for s in reversed(range(nsub)):
                jb = j * nsub + s
                off = pl.multiple_of(jb * B, B)
                k = k_ref[pl.ds(off, B), :]
                z = _dot(q, k, NT) * SCALE
                _, sp = _softplus_parts(z)
                lg = -sp
                valid = None
                if masked:
                    r, c = _positions(i, jb, B, B)
                    valid = c < r
                    lg = jnp.where(valid, lg, 0.0)
                hi, lo = _split2(lg)
                cum = _dot(hi, suffix) + _dot(lo, suffix)
                parts.append((off, z, cum, jnp.sum(lg, axis=1, keepdims=True), valid))
            for off, z, cum, rs, valid in parts:
                a = jnp.exp(z + cum + run)
                if masked:
                    a = jnp.where(valid, a, 0.0)
                acc = acc + _dot(a.astype(BF16), v_ref[pl.ds(off, B), :])
                run = run + rs
            return run, acc

        nfull = (i * B) // bk
        carry = step(nfull, (jnp.zeros((B, 1), F32), jnp.zeros((B, HEAD_DIM), F32)), True)
        _, acc = lax.fori_loop(0, nfull, lambda jj, cr: step(nfull - 1 - jj, cr, False), carry)
        o_ref[...] = acc.astype(o_ref.dtype)

    return pl.pallas_call(
        kern, name=name, grid=(H, nq),
        in_specs=[pl.BlockSpec((B, HEAD_DIM), lambda h, i: (i, 12 + h)),
                  pl.BlockSpec((T, HEAD_DIM), lambda h, i: (0, 16 + h)),
                  pl.BlockSpec((T, HEAD_DIM), lambda h, i: (0, 20 + h))],
        out_specs=pl.BlockSpec((B, HEAD_DIM), lambda h, i: (i, h)),
        out_shape=jax.ShapeDtypeStruct((T, BRANCH_WIDTH), BF16),
        compiler_params=_cp(("parallel", "arbitrary")),
    )(u_att, u_att, u_att)


def _sb_bwd(u_att, do, *, name):
    T = u_att.shape[0]
    B, bk = _att_blocks(T)
    nq, nsub = T // B, bk // B
    H = N_HEADS

    def kern(q_ref, k_ref, v_ref, do_ref, dq_ref, dk_ref, dv_ref, dk_acc, dv_acc, de_s, sg_s):
        i = pl.program_id(1)

        @pl.when(i == 0)
        def _():
            dk_acc[...] = jnp.zeros_like(dk_acc)
            dv_acc[...] = jnp.zeros_like(dv_acc)

        q = q_ref[...]
        dov = do_ref[...]
        suffix, prefix = _sb_tri(B)

        def sweep1(j, run, masked):
            parts = []
            for s in reversed(range(nsub)):
                jb = j * nsub + s
                off = pl.multiple_of(jb * B, B)
                k = k_ref[pl.ds(off, B), :]
                z = _dot(q, k, NT) * SCALE
                t, sp = _softplus_parts(z)
                lg = -sp
                sg = jnp.exp(z + lg)
                valid = None
                if masked:
                    r, c = _positions(i, jb, B, B)
                    valid = c < r
                    lg = jnp.where(valid, lg, 0.0)
                    sg = jnp.where(valid, sg, 0.0)
                sg_s[jb] = sg.astype(sg_s.dtype)
                hi, lo = _split2(lg)
                cum = _dot(hi, suffix) + _dot(lo, suffix)
                da = _dot(dov, v_ref[pl.ds(off, B), :], NT)
                parts.append((jb, off, z, cum, da, jnp.sum(lg, axis=1, keepdims=True), valid))
            for jb, off, z, cum, da, rs, valid in parts:
                a = jnp.exp(z + cum + run)
                if masked:
                    a = jnp.where(valid, a, 0.0)
                de_s[jb] = a * da
                dv_acc[pl.ds(off, B), :] += _dot(a.astype(BF16), dov, TN)
                run = run + rs
            return run

        nfull = (i * B) // bk
        run = sweep1(nfull, jnp.zeros((B, 1), F32), True)
        lax.fori_loop(0, nfull, lambda jj, cr: sweep1(nfull - 1 - jj, cr, False), run)

        def sweep2(j, carry):
            pre, dq = carry
            parts = []
            for s in range(nsub):
                jb = j * nsub + s
                de = de_s[jb]
                hi, lo = _split2(de)
                parts.append((jb, de, _dot(hi, prefix) + _dot(lo, prefix), jnp.sum(de, axis=1, keepdims=True)))
            for jb, de, g, rs in parts:
                off = pl.multiple_of(jb * B, B)
                dz = (de - sg_s[jb].astype(F32) * (g + pre)).astype(BF16)
                dq = dq + _dot(dz, k_ref[pl.ds(off, B), :])
                dk_acc[pl.ds(off, B), :] += _dot(dz, q, TN)
                pre = pre + rs
            return pre, dq

        _, dq = lax.fori_loop(0, nfull + 1, sweep2, (jnp.zeros((B, 1), F32), jnp.zeros((B, HEAD_DIM), F32)))
        dq_ref[...] = (dq * SCALE).astype(dq_ref.dtype)

        @pl.when(i == nq - 1)
        def _():
            dk_ref[...] = (dk_acc[...] * SCALE).astype(dk_ref.dtype)
            dv_ref[...] = dv_acc[...].astype(dv_ref.dtype)

    return pl.pallas_call(
        kern, name=name, grid=(H, nq),
        in_specs=[pl.BlockSpec((B, HEAD_DIM), lambda h, i: (i, 12 + h)),
                  pl.BlockSpec((T, HEAD_DIM), lambda h, i: (0, 16 + h)),
                  pl.BlockSpec((T, HEAD_DIM), lambda h, i: (0, 20 + h)),
                  pl.BlockSpec((B, HEAD_DIM), lambda h, i: (i, h))],
        out_specs=[pl.BlockSpec((B, HEAD_DIM), lambda h, i: (i, h)),
                   pl.BlockSpec((T, HEAD_DIM), lambda h, i: (0, h)),
                   pl.BlockSpec((T, HEAD_DIM), lambda h, i: (0, h))],
        out_shape=[jax.ShapeDtypeStruct((T, BRANCH_WIDTH), BF16)] * 3,
        scratch_shapes=[pltpu.VMEM((T, HEAD_DIM), F32), pltpu.VMEM((T, HEAD_DIM), F32),
                        pltpu.VMEM((T // B, B, B), F32), pltpu.VMEM((T // B, B, B), BF16)],
        compiler_params=_cp(("parallel", "arbitrary")),
    )(u_att, u_att, u_att, do)


def _rel_onehot(qrow):
    k = lax.broadcasted_iota(jnp.int32, (BAND, REL_PAD), 0)
    rr = lax.broadcasted_iota(jnp.int32, (BAND, REL_PAD), 1)
    idx = jnp.clip(PAD_ROWS + qrow - k, -(CHUNK - 1), REL_CLIP) + (CHUNK - 1)
    return jnp.where(idx == rr, 1.0, 0.0).astype(BF16)


def _band_bias(table, *, name):
    def kern(t_ref, o_ref):
        hi, mid, lo = _split3(t_ref[...])

        def body(qrow, _):
            oh = _rel_onehot(qrow)
            o_ref[qrow] = _dot(hi, oh, NT) + _dot(mid, oh, NT) + _dot(lo, oh, NT)
            return 0

        lax.fori_loop(0, CHUNK, body, 0)

    return pl.pallas_call(
        kern, name=name, out_shape=jax.ShapeDtypeStruct((CHUNK, 8, BAND), F32),
        compiler_params=_cp(),
    )(table)


def _band_bias_bwd(dbias, *, name):
    def kern(d_ref, o_ref):
        def body(qrow, acc):
            oh = _rel_onehot(qrow)
            hi, mid, lo = _split3(d_ref[qrow])
            return acc + _dot(hi, oh) + _dot(mid, oh) + _dot(lo, oh)

        o_ref[...] = lax.fori_loop(0, CHUNK, body, jnp.zeros((8, REL_PAD), F32))

    return pl.pallas_call(
        kern, name=name, out_shape=jax.ShapeDtypeStruct((8, REL_PAD), F32),
        compiler_params=_cp(),
    )(dbias)


def _chunk_rows(T):
    return _pick(T, (512, 256, 128, 64))


def _chunk_scores(q, kw, bias, c_global):
    s = _dot(q, kw, NT) * SCALE + bias
    col = lax.broadcasted_iota(jnp.int32, (CHUNK, BAND), 1)
    valid = (c_global * CHUNK + col) >= PAD_ROWS
    s = jnp.where(valid, s, -jnp.inf)
    m = jnp.max(s, axis=1, keepdims=True)
    e = jnp.exp(s - m)
    return e / jnp.sum(e, axis=1, keepdims=True)


def _chunk_fwd(u_att, bias, *, name):
    T = u_att.shape[0]
    R = _chunk_rows(T)
    nr = T // R
    H = N_HEADS

    def kern(q_ref, k_ref, v_ref, b_ref, o_ref, kpad, vpad):
        i = pl.program_id(1)

        @pl.when(i == 0)
        def _():
            kpad[0:PAD_ROWS, :] = jnp.zeros((PAD_ROWS, HEAD_DIM), BF16)
            vpad[0:PAD_ROWS, :] = jnp.zeros((PAD_ROWS, HEAD_DIM), BF16)
            kpad[PAD_ROWS:, :] = k_ref[...]
            vpad[PAD_ROWS:, :] = v_ref[...]

        bias_v = b_ref[...]
        for cc in range(R // CHUNK):
            cg = i * (R // CHUNK) + cc
            off = pl.multiple_of(cg * CHUNK, CHUNK)
            q = q_ref[cc * CHUNK:(cc + 1) * CHUNK, :]
            kw = kpad[pl.ds(off, BAND), :]
            vw = vpad[pl.ds(off, BAND), :]
            p = _chunk_scores(q, kw, bias_v, cg)
            o_ref[cc * CHUNK:(cc + 1) * CHUNK, :] = _dot(p.astype(BF16), vw).astype(o_ref.dtype)

    return pl.pallas_call(
        kern, name=name, grid=(H, nr),
        in_specs=[pl.BlockSpec((R, HEAD_DIM), lambda h, i: (i, 24 + h)),
                  pl.BlockSpec((T, HEAD_DIM), lambda h, i: (0, 28 + h)),
                  pl.BlockSpec((T, HEAD_DIM), lambda h, i: (0, 32 + h)),
                  pl.BlockSpec((None, CHUNK, BAND), lambda h, i: (h, 0, 0))],
        out_specs=pl.BlockSpec((R, HEAD_DIM), lambda h, i: (i, h)),
        out_shape=jax.ShapeDtypeStruct((T, BRANCH_WIDTH), BF16),
        scratch_shapes=[pltpu.VMEM((T + PAD_ROWS, HEAD_DIM), BF16), pltpu.VMEM((T + PAD_ROWS, HEAD_DIM), BF16)],
        compiler_params=_cp(("parallel", "arbitrary")),
    )(u_att, u_att, u_att, bias)


def _chunk_bwd(u_att, bias, do, *, name):
    T = u_att.shape[0]
    R = _chunk_rows(T)
    nr = T // R
    H = N_HEADS

    def kern(q_ref, k_ref, v_ref, b_ref, do_ref, dq_ref, dk_ref, dv_ref, db_ref, kpad, vpad, dkp, dvp):
        i = pl.program_id(1)

        @pl.when(i == 0)
        def _():
            kpad[0:PAD_ROWS, :] = jnp.zeros((PAD_ROWS, HEAD_DIM), BF16)
            vpad[0:PAD_ROWS, :] = jnp.zeros((PAD_ROWS, HEAD_DIM), BF16)
            kpad[PAD_ROWS:, :] = k_ref[...]
            vpad[PAD_ROWS:, :] = v_ref[...]
            dkp[...] = jnp.zeros_like(dkp)
            dvp[...] = jnp.zeros_like(dvp)
            db_ref[...] = jnp.zeros_like(db_ref)

        bias_v = b_ref[...]
        for cc in range(R // CHUNK):
            cg = i * (R // CHUNK) + cc
            off = pl.multiple_of(cg * CHUNK, CHUNK)
            q = q_ref[cc * CHUNK:(cc + 1) * CHUNK, :]
            dov = do_ref[cc * CHUNK:(cc + 1) * CHUNK, :]
            kw = kpad[pl.ds(off, BAND), :]
            vw = vpad[pl.ds(off, BAND), :]
            p = _chunk_scores(q, kw, bias_v, cg)
            dp = _dot(dov, vw, NT)
            ds = p * (dp - jnp.sum(p * dp, axis=1, keepdims=True))
            dsb = ds.astype(BF16)
            dq_ref[cc * CHUNK:(cc + 1) * CHUNK, :] = (_dot(dsb, kw) * SCALE).astype(dq_ref.dtype)
            dkp[pl.ds(off, BAND), :] += _dot(dsb, q, TN)
            dvp[pl.ds(off, BAND), :] += _dot(p.astype(BF16), dov, TN)
            db_ref[...] += ds

        @pl.when(i == nr - 1)
        def _():
            dk_ref[...] = (dkp[PAD_ROWS:, :] * SCALE).astype(dk_ref.dtype)
            dv_ref[...] = dvp[PAD_ROWS:, :].astype(dv_ref.dtype)

    return pl.pallas_call(
        kern, name=name, grid=(H, nr),
        in_specs=[pl.BlockSpec((R, HEAD_DIM), lambda h, i: (i, 24 + h)),
                  pl.BlockSpec((T, HEAD_DIM), lambda h, i: (0, 28 + h)),
                  pl.BlockSpec((T, HEAD_DIM), lambda h, i: (0, 32 + h)),
                  pl.BlockSpec((None, CHUNK, BAND), lambda h, i: (h, 0, 0)),
                  pl.BlockSpec((R, HEAD_DIM), lambda h, i: (i, h))],
        out_specs=[pl.BlockSpec((R, HEAD_DIM), lambda h, i: (i, h)),
                   pl.BlockSpec((T, HEAD_DIM), lambda h, i: (0, h)),
                   pl.BlockSpec((T, HEAD_DIM), lambda h, i: (0, h)),
                   pl.BlockSpec((None, CHUNK, BAND), lambda h, i: (h, 0, 0))],
        out_shape=[jax.ShapeDtypeStruct((T, BRANCH_WIDTH), BF16)] * 3
                  + [jax.ShapeDtypeStruct((H, CHUNK, BAND), F32)],
        scratch_shapes=[pltpu.VMEM((T + PAD_ROWS, HEAD_DIM), BF16), pltpu.VMEM((T + PAD_ROWS, HEAD_DIM), BF16),
                        pltpu.VMEM((T + PAD_ROWS, HEAD_DIM), F32), pltpu.VMEM((T + PAD_ROWS, HEAD_DIM), F32)],
        compiler_params=_cp(("parallel", "arbitrary")),
    )(u_att, u_att, u_att, bias, do)


LRU_ROWS = 256
HALO = 8


def _gelu(y):
    k0 = math.sqrt(2.0 / math.pi)
    t = jnp.tanh(k0 * (y + 0.044715 * y * y * y))
    return 0.5 * y * (1.0 + t), t


def _gelu_grad(y, t):
    k0 = math.sqrt(2.0 / math.pi)
    return 0.5 * (1.0 + t) + 0.5 * y * (1.0 - t * t) * k0 * (1.0 + 3.0 * 0.044715 * y * y)


def _neg_expm1(y):
    poly = -y * (1.0 + y * (1.0 / 2 + y * (1.0 / 6 + y * (1.0 / 24 + y * (1.0 / 120 + y * (1.0 / 720 + y * (1.0 / 5040)))))))
    return jnp.where(y > -0.5, poly, 1.0 - jnp.exp(y))


def _lru_gates(ext, cw_ref, cb_ref, wr_ref, br_ref, wi_ref, bi_ref, lam_ref, rows):
    xc = cb_ref[...] + jnp.zeros((rows, BRANCH_WIDTH), F32)
    for j in range(CONV_WIDTH):
        xc = xc + ext[pl.ds(HALO - (CONV_WIDTH - 1) + j, rows), :] * cw_ref[j:j + 1, :]
    xcb = xc.astype(BF16)
    zr = jnp.concatenate([_dot(xcb[:, n * 128:(n + 1) * 128], wr_ref[n]) for n in range(4)], axis=1) + br_ref[...]
    zi = jnp.concatenate([_dot(xcb[:, n * 128:(n + 1) * 128], wi_ref[n]) for n in range(4)], axis=1) + bi_ref[...]
    r = _sigmoid(zr)
    gi = _sigmoid(zi)
    ls = _log_sigmoid(lam_ref[...])
    la = LRU_C * r * ls
    a = jnp.exp(la)
    mult = jnp.sqrt(_neg_expm1(2.0 * la))
    return xc, xcb, r, gi, ls, a, mult


def _lru_param_specs():
    full2 = lambda s: pl.BlockSpec(s, lambda i: (0, 0))
    full3 = lambda s: pl.BlockSpec(s, lambda i: (0, 0, 0))
    return [full2((8, BRANCH_WIDTH)), full2((1, BRANCH_WIDTH)), full3((4, 128, 128)), full2((1, BRANCH_WIDTH)),
            full3((4, 128, 128)), full2((1, BRANCH_WIDTH)), full2((1, BRANCH_WIDTH))]


def _lru_fwd(u_rec, p, *, name):
    T = u_rec.shape[0]
    R = min(LRU_ROWS, T)
    nb = T // R
    W = BRANCH_WIDTH
    hb = R // HALO

    def kern(rx_ref, halo_ref, ry_ref, cw_ref, cb_ref, wr_ref, br_ref, wi_ref, bi_ref, lam_ref,
             o_ref, h_ref, ext, a_s, b_s, hc):
        i = pl.program_id(0)

        @pl.when(i == 0)
        def _():
            hc[...] = jnp.zeros_like(hc)

        ext[0:HALO, :] = jnp.where(i == 0, 0.0, halo_ref[...])
        ext[HALO:, :] = rx_ref[...]
        xc, _, r, gi, ls, a, mult = _lru_gates(ext, cw_ref, cb_ref, wr_ref, br_ref, wi_ref, bi_ref, lam_ref, R)
        a_s[...] = a
        b_s[...] = mult * (gi * xc)

        def body(t, h):
            h = a_s[pl.ds(t, 1), :] * h + b_s[pl.ds(t, 1), :]
            h_ref[pl.ds(t, 1), :] = h
            return h

        h = lax.fori_loop(0, R, body, hc[0:1, :], unroll=8)
        hc[...] = jnp.broadcast_to(h, hc.shape)
        g, _ = _gelu(ry_ref[...])
        o_ref[...] = (h_ref[...] * g).astype(o_ref.dtype)

    return pl.pallas_call(
        kern, name=name, grid=(nb,),
        in_specs=[pl.BlockSpec((R, W), lambda i: (i, 0)),
                  pl.BlockSpec((HALO, W), lambda i: (jnp.maximum(i * hb - 1, 0), 0)),
                  pl.BlockSpec((R, W), lambda i: (i, 1))] + _lru_param_specs(),
        out_specs=[pl.BlockSpec((R, W), lambda i: (i, 0)), pl.BlockSpec((R, W), lambda i: (i, 0))],
        out_shape=[jax.ShapeDtypeStruct((T, W), BF16), jax.ShapeDtypeStruct((T, W), F32)],
        scratch_shapes=[pltpu.VMEM((R + HALO, W), F32), pltpu.VMEM((R, W), F32), pltpu.VMEM((R, W), F32),
                        pltpu.VMEM((8, W), F32)],
        compiler_params=_cp(("arbitrary",)),
    )(u_rec, u_rec, u_rec, *p)


def _lru_bwd(u_rec, hs, do, p, *, name):
    T = u_rec.shape[0]
    R = min(LRU_ROWS, T)
    nb = T // R
    W = BRANCH_WIDTH
    hb = R // HALO

    def kern(rx_ref, halo_ref, ry_ref, h_ref, hh_ref, do_ref, cw_ref, cb_ref, wr_ref, br_ref, wi_ref, bi_ref, lam_ref,
             drx_ref, dry_ref, dcw_ref, dcb_ref, dwr_ref, dbr_ref, dwi_ref, dbi_ref, dlam_ref,
             ext, hext, a_s, g_s, dext, gc):
        s = pl.program_id(0)
        first_block = s == nb - 1

        @pl.when(s == 0)
        def _():
            gc[...] = jnp.zeros_like(gc)
            dext[R:, :] = jnp.zeros((HALO, W), F32)
            for ref in (dcw_ref, dcb_ref, dwr_ref, dbr_ref, dwi_ref, dbi_ref, dlam_ref):
                ref[...] = jnp.zeros_like(ref)

        ext[0:HALO, :] = jnp.where(first_block, 0.0, halo_ref[...])
        ext[HALO:, :] = rx_ref[...]
        hext[0:HALO, :] = jnp.where(first_block, 0.0, hh_ref[...])
        hext[HALO:, :] = h_ref[...]
        xc, xcb, r, gi, ls, a, mult = _lru_gates(ext, cw_ref, cb_ref, wr_ref, br_ref, wi_ref, bi_ref, lam_ref, R)
        ry = ry_ref[...]
        gel, th = _gelu(ry)
        dov = do_ref[...].astype(F32)
        dry_ref[...] = (dov * h_ref[...] * _gelu_grad(ry, th)).astype(dry_ref.dtype)
        a_s[...] = a
        g_s[...] = dov * gel

        def body(tt, g):
            t = R - 1 - tt
            dh = g_s[pl.ds(t, 1), :] + g
            g_s[pl.ds(t, 1), :] = dh
            return a_s[pl.ds(t, 1), :] * dh

        g = lax.fori_loop(0, R, body, gc[0:1, :], unroll=8)
        gc[...] = jnp.broadcast_to(g, gc.shape)
        dh = g_s[...]
        hprev = hext[pl.ds(HALO - 1, R), :]
        da = dh * hprev
        gx = gi * xc
        dmult = dh * gx
        dgx = dh * mult
        dgi = dgx * xc
        dxc = dgx * gi
        dla = da * a - dmult * (a * a) / mult
        dr = dla * (LRU_C * ls)
        dlam_ref[...] += jnp.sum(dla * (LRU_C * r), axis=0, keepdims=True)
        dzr = dr * r * (1.0 - r)
        dzi = dgi * gi * (1.0 - gi)
        dbr_ref[...] += jnp.sum(dzr, axis=0, keepdims=True)
        dbi_ref[...] += jnp.sum(dzi, axis=0, keepdims=True)
        dzrb = dzr.astype(BF16)
        dzib = dzi.astype(BF16)
        back = []
        for n in range(4):
            sl = slice(n * 128, (n + 1) * 128)
            dwr_ref[n] += _dot(xcb[:, sl], dzrb[:, sl], TN)
            dwi_ref[n] += _dot(xcb[:, sl], dzib[:, sl], TN)
            back.append(_dot(dzrb[:, sl], wr_ref[n], NT) + _dot(dzib[:, sl], wi_ref[n], NT))
        dxc = dxc + jnp.concatenate(back, axis=1)
        dcb_ref[...] += jnp.sum(dxc, axis=0, keepdims=True)
        for j in range(CONV_WIDTH):
            dcw_ref[j:j + 1, :] += jnp.sum(dxc * ext[pl.ds(HALO - (CONV_WIDTH - 1) + j, R), :], axis=0, keepdims=True)
        dext[0:R, :] = dxc
        drx = jnp.zeros((R, W), F32)
        for j in range(CONV_WIDTH):
            drx = drx + dext[pl.ds(CONV_WIDTH - 1 - j, R), :] * cw_ref[j:j + 1, :]
        drx_ref[...] = drx.astype(drx_ref.dtype)
        dext[R:, :] = dxc[0:HALO, :]

        @pl.when(s == nb - 1)
        def _():
            dlam_ref[...] = dlam_ref[...] * _sigmoid(-lam_ref[...])

    rev = lambda c: pl.BlockSpec((R, W), lambda s: (nb - 1 - s, c))
    halo = lambda: pl.BlockSpec((HALO, W), lambda s: (jnp.maximum((nb - 1 - s) * hb - 1, 0), 0))
    v2 = lambda shp: pl.BlockSpec(shp, lambda s: (0, 0))
    v3 = lambda shp: pl.BlockSpec(shp, lambda s: (0, 0, 0))
    return pl.pallas_call(
        kern, name=name, grid=(nb,),
        in_specs=[rev(0), halo(), rev(1), rev(0), halo(), rev(0)] + _lru_param_specs(),
        out_specs=[rev(0), rev(0), v2((8, W)), v2((1, W)), v3((4, 128, 128)), v2((1, W)), v3((4, 128, 128)),
                   v2((1, W)), v2((1, W))],
        out_shape=[jax.ShapeDtypeStruct((T, W), BF16), jax.ShapeDtypeStruct((T, W), BF16),
                   jax.ShapeDtypeStruct((8, W), F32), jax.ShapeDtypeStruct((1, W), F32),
                   jax.ShapeDtypeStruct((4, 128, 128), F32), jax.ShapeDtypeStruct((1, W), F32),
                   jax.ShapeDtypeStruct((4, 128, 128), F32), jax.ShapeDtypeStruct((1, W), F32),
                   jax.ShapeDtypeStruct((1, W), F32)],
        scratch_shapes=[pltpu.VMEM((R + HALO, W), F32), pltpu.VMEM((R + HALO, W), F32), pltpu.VMEM((R, W), F32),
                        pltpu.VMEM((R, W), F32), pltpu.VMEM((R + HALO, W), F32), pltpu.VMEM((8, W), F32)],
        compiler_params=_cp(("arbitrary",)),
    )(u_rec, u_rec, u_rec, hs, hs, do, *p)


def _merge_fwd(o_all, wb, gate, *, name):
    T = o_all.shape[1]
    D = D_MODEL
    bm = _pick(T, (1024, 512, 256, 128))
    bn = 1024
    nj = D // bn

    def kern(o_ref, w_ref, g_ref, m_ref, pb_ref, acc):
        g = pl.program_id(2)
        pbv = _dot(o_ref[...], w_ref[...])
        pb_ref[...] = pbv.astype(pb_ref.dtype)
        term = g_ref[...].astype(F32) * pbv

        @pl.when(g == 0)
        def _():
            acc[...] = term

        @pl.when(g > 0)
        def _():
            acc[...] += term

        @pl.when(g == N_BRANCH - 1)
        def _():
            m_ref[...] = acc[...].astype(m_ref.dtype)

    return pl.pallas_call(
        kern, name=name, grid=(T // bm, nj, N_BRANCH),
        in_specs=[pl.BlockSpec((None, bm, BRANCH_WIDTH), lambda i, j, g: (g, i, 0)),
                  pl.BlockSpec((None, BRANCH_WIDTH, bn), lambda i, j, g: (g, 0, j)),
                  pl.BlockSpec((bm, bn), lambda i, j, g: (i, g * nj + j))],
        out_specs=[pl.BlockSpec((bm, bn), lambda i, j, g: (i, j)),
                   pl.BlockSpec((bm, bn), lambda i, j, g: (i, g * nj + j))],
        out_shape=[jax.ShapeDtypeStruct((T, D), BF16), jax.ShapeDtypeStruct((T, N_BRANCH * D), BF16)],
        scratch_shapes=[pltpu.VMEM((bm, bn), F32)],
        compiler_params=_cp(("parallel", "parallel", "arbitrary")),
    )(o_all, wb, gate)


def _merge_bwd(dm, gate, pb, *, name):
    T = dm.shape[0]
    D = D_MODEL
    bt = _pick(T, (256, 128))

    def kern(dm_ref, g_ref, pb_ref, dpb_ref, dzg_ref, dbg_ref):
        i = pl.program_id(1)
        dmv = dm_ref[...]
        gv = g_ref[...].astype(F32)
        dpb_ref[...] = (dmv * gv).astype(dpb_ref.dtype)
        dzg = dmv * pb_ref[...].astype(F32) * gv * (1.0 - gv)
        dzg_ref[...] = dzg.astype(dzg_ref.dtype)
        part = jnp.sum(dzg, axis=0, keepdims=True)

        @pl.when(i == 0)
        def _():
            dbg_ref[...] = part

        @pl.when(i > 0)
        def _():
            dbg_ref[...] += part

    return pl.pallas_call(
        kern, name=name, grid=(N_BRANCH, T // bt),
        in_specs=[pl.BlockSpec((bt, D), lambda g, i: (i, 0)),
                  pl.BlockSpec((bt, D), lambda g, i: (i, g)),
                  pl.BlockSpec((bt, D), lambda g, i: (i, g))],
        out_specs=[pl.BlockSpec((None, bt, D), lambda g, i: (g, i, 0)),
                   pl.BlockSpec((bt, D), lambda g, i: (i, g)),
                   pl.BlockSpec((1, D), lambda g, i: (0, g))],
        out_shape=[jax.ShapeDtypeStruct((N_BRANCH, T, D), BF16), jax.ShapeDtypeStruct((T, N_BRANCH * D), BF16),
                   jax.ShapeDtypeStruct((1, N_BRANCH * D), F32)],
        compiler_params=_cp(("parallel", "arbitrary")),
    )(dm, gate, pb)


def _col_split(M, N, bm, bn):
    per = N // N_CHIPS // bn
    return (N_CHIPS, M, N // N_CHIPS), (None, bm, bn), lambda i, j: (j // per, i, j % per)


def _pad_lanes(v, n):
    return jnp.pad(v, [(0, 0)] * (v.ndim - 1) + [(0, n - v.shape[-1])])


def _rows8(v):
    return jnp.pad(v, ((0, 8 - v.shape[0]), (0, 0)))


def _device_step(x, tgt, W):
    T = x.shape[0]
    _, bk = _att_blocks(T)
    H = N_HEADS
    G = {}
    saved = []

    xf, xb = _ln_fwd(x, W['ln_in_g'], W['ln_in_b'], name='ln_in_fwd')
    for l in range(DEPTH):
        w_att, w_rec = W['w_att'][l], W['w_rec'][l]
        u_att = _mm(xb, w_att, name='in_proj_att', out_dtypes=(BF16,))
        u_rec = _mm(xb, w_rec, name='in_proj_rec', out_dtypes=(F32,))
        ffl = u_rec[:, 2 * BRANCH_WIDTH:]
        bf = _pad_lanes(W['b_forget'][l].reshape(1, H), LANES)
        Fc = _forget_fwd(ffl, bf, name='forget_fwd')
        Fh = Fc[:, :H].T
        frow = Fh.reshape(H, T // bk, 1, bk)
        o_fox, lse = _fox_fwd(u_att, Fc, frow, name='fox_fwd')
        lp = (_rows8(W['conv_w'][l]), W['conv_b'][l].reshape(1, -1), W['w_r'][l].astype(BF16),
              W['b_r'][l].reshape(1, -1), W['w_i'][l].astype(BF16), W['b_i'][l].reshape(1, -1),
              W['lru_lambda'][l].reshape(1, -1))
        o_lru, hs = _lru_fwd(u_rec, lp, name='lru_fwd')
        o_sb = _sb_fwd(u_att, name='sb_fwd')
        table = _rows8(_pad_lanes(W['rel_bias'][l], REL_PAD))
        bias = _band_bias(table, name='band_bias').transpose(1, 0, 2)[:H]
        o_ch = _chunk_fwd(u_att, bias, name='chunk_fwd')
        o_all = jnp.stack([o_fox, o_lru, o_sb, o_ch])
        gate = _mm(xb, W['w_gate_cat'][l], name='gate_proj', out_dtypes=(BF16,),
                   extras=[(W['b_gate'][l].reshape(1, -1), 'n')],
                   epilogue=lambda acc, b: (_sigmoid(acc + b),))
        merged, pb = _merge_fwd(o_all, W['w_branch'][l], gate, name='merge_fwd')
        h1 = _mm(merged, W['w_out'][l], name='out_proj', extras=[(xf, 'mn')],
                 epilogue=lambda acc, xr: (ALPHA * xr + acc,))
        xmf, xmb = _ln_fwd(h1, W['ln1_g'][l], W['ln1_b'][l], name='ln_fwd')
        hid, ra = _mm(xmb, W['w_ff1'][l], name='ff1', out_dtypes=(BF16, BF16),
                      epilogue=lambda acc: (jnp.square(jnp.maximum(acc, 0.0)), jnp.maximum(acc, 0.0)))
        h2 = _mm(hid, W['w_ff2'][l], name='ff2', extras=[(xmf, 'mn')],
                 epilogue=lambda acc, xr: (ALPHA * xr + acc,))
        saved.append(dict(xb=xb, u_att=u_att, u_rec=u_rec, ffl=ffl, bf=bf, fcum=Fc, frow=frow, lse=lse, lp=lp,
                          hs=hs, bias=bias, o_all=o_all, gate=gate, merged=merged, pb=pb, h1=h1, xmb=xmb,
                          hid=hid, ra=ra, h2=h2))
        xf, xb = _ln_fwd(h2, W['ln2_g'][l], W['ln2_b'][l], name='ln_fwd')

    dx, loss_tile = _loss_head(xf, tgt, name='loss_head')
    loss = loss_tile[0, 0]

    for l in reversed(range(DEPTH)):
        S = saved[l]
        dh2, dh2b, G[('ln2_g', l)], G[('ln2_b', l)] = _ln_bwd(S['h2'], dx, W['ln2_g'][l], name='ln_bwd')
        da = _mm(dh2b, W['w_ff2'][l], tb=True, name='ff2_dx', out_dtypes=(BF16,), extras=[(S['ra'], 'mn')],
                 epilogue=lambda acc, rav: (acc * (2.0 * rav.astype(F32)),))
        G[('w_ff2', l)] = _mm(S['hid'], dh2b, ta=True, name='ff2_dw').reshape(N_CHIPS, D_FF // N_CHIPS, D_MODEL)
        G[('w_ff1', l)] = _mm(S['xmb'], da, ta=True, name='ff1_dw', bm=1024, bn=1024,
                              out_map=_col_split(D_MODEL, D_FF, 1024, 1024))
        dxm = _mm(da, W['w_ff1'][l], tb=True, name='ff1_dx', extras=[(dh2, 'mn')],
                  epilogue=lambda acc, d: (ALPHA * d + acc,))
        dh1, dh1b, G[('ln1_g', l)], G[('ln1_b', l)] = _ln_bwd(S['h1'], dxm, W['ln1_g'][l], name='ln_bwd')
        dm = _mm(dh1b, W['w_out'][l], tb=True, name='out_dx')
        G[('w_out', l)] = _mm(S['merged'], dh1b, ta=True, name='out_dw').reshape(
            N_CHIPS, D_MODEL // N_CHIPS, D_MODEL)
        dpb, dzg, G[('b_gate', l)] = _merge_bwd(dm, S['gate'], S['pb'], name='merge_bwd')
        do = [_mm(dpb[g], W['w_branch'][l][g], tb=True, name='branch_dx', out_dtypes=(BF16,)) for g in range(N_BRANCH)]
        G[('w_branch', l)] = jnp.stack(
            [_mm(S['o_all'][g], dpb[g], ta=True, name='branch_dw', bm=BRANCH_WIDTH, bn=BRANCH_WIDTH,
                 out_map=_col_split(BRANCH_WIDTH, D_MODEL, BRANCH_WIDTH, BRANCH_WIDTH))
             for g in range(N_BRANCH)], axis=1)
        G[('w_gate', l)] = _mm(S['xb'], dzg, ta=True, name='gate_dw', bm=1024, bn=1024,
                               out_map=((N_CHIPS, N_BRANCH, D_MODEL // N_CHIPS, D_MODEL),
                                        (2, None, D_MODEL // N_CHIPS, 1024),
                                        lambda i, j: (i, j // 2, 0, j % 2)))
        u_att, u_rec = S['u_att'], S['u_rec']
        delta = _row_dot(do[0], S['o_all'][0], name='row_dot')
        fdq, fdk, fdv, dfk, dfq = _fox_bwd(u_att, do[0], S['lse'], delta, S['fcum'], S['frow'], name='fox_bwd')
        dff, dbf = _forget_bwd(_pad_lanes(dfk.reshape(H, T).T, LANES), _pad_lanes(dfq.reshape(H, T).T, LANES),
                               S['ffl'], S['bf'], name='forget_bwd')
        G[('b_forget', l)] = dbf[0, :H]
        (drx, dry, dcw, dcb, G[('w_r', l)], dbr, G[('w_i', l)], dbi, dlam) = _lru_bwd(
            u_rec, S['hs'], do[1], S['lp'], name='lru_bwd')
        G[('conv_w', l)], G[('conv_b', l)] = dcw[:CONV_WIDTH], dcb[0]
        G[('b_r', l)], G[('b_i', l)], G[('lru_lambda', l)] = dbr[0], dbi[0], dlam[0]
        sdq, sdk, sdv = _sb_bwd(u_att, do[2], name='sb_bwd')
        cdq, cdk, cdv, dbias = _chunk_bwd(u_att, S['bias'], do[3], name='chunk_bwd')
        dtab = _band_bias_bwd(jnp.pad(dbias, ((0, 8 - H), (0, 0), (0, 0))).transpose(1, 0, 2), name='band_bias_bwd')
        G[('rel_bias', l)] = dtab[:H, :REL_TABLE]
        du_att = jnp.concatenate([fdq, fdk, fdv, sdq, sdk, sdv, cdq, cdk, cdv], axis=1)
        du_rec = jnp.concatenate([drx, dry, dff], axis=1)
        G[('w_att', l)] = _mm(S['xb'], du_att, ta=True, name='in_att_dw')
        G[('w_rec', l)] = _mm(S['xb'], du_rec, ta=True, name='in_rec_dw')
        t1 = _mm(dzg, W['w_gate_cat'][l], tb=True, name='gate_dx', extras=[(dh1, 'mn')],
                 epilogue=lambda acc, d: (ALPHA * d + acc,))
        t2 = _mm(du_att, W['w_att'][l], tb=True, name='in_att_dx', extras=[(t1, 'mn')],
                 epilogue=lambda acc, d: (d + acc,))
        dx = _mm(du_rec, W['w_rec'][l], tb=True, name='in_rec_dx', extras=[(t2, 'mn')],
                 epilogue=lambda acc, d: (d + acc,))

    gx, _, G[('ln_in_g', -1)], G[('ln_in_b', -1)] = _ln_bwd(x, dx, W['ln_in_g'], name='ln_in_bwd')
    return loss, gx, G


_IN_FQKV = (0, 1536)
_IN_FF = (1536, 1540)
_IN_REC = (1540, 2564)
_IN_REST = (2564, D_IN)


def _prep_weights(full):
    w_in = full['w_in']
    L = w_in.shape[0]
    W = dict(full)
    W['w_att'] = jnp.concatenate([w_in[..., _IN_FQKV[0]:_IN_FQKV[1]], w_in[..., _IN_REST[0]:_IN_REST[1]]], -1).astype(BF16)
    W['w_rec'] = jnp.concatenate([w_in[..., _IN_REC[0]:_IN_REC[1]], w_in[..., _IN_FF[0]:_IN_FF[1]],
                                  jnp.zeros((L, D_MODEL, N_REC - 1024 - N_HEADS), w_in.dtype)], -1).astype(BF16)
    W['w_gate_cat'] = full['w_gate'].transpose(0, 2, 1, 3).reshape(L, D_MODEL, N_BRANCH * D_MODEL).astype(BF16)
    W['b_gate'] = full['b_gate'].reshape(L, N_BRANCH * D_MODEL)
    for n in ('w_branch', 'w_out', 'w_ff1', 'w_ff2'):
        W[n] = full[n].astype(BF16)
    return W


def _grads_to_reference_layout(G):
    out = {'ln_in_g': G[('ln_in_g', -1)][0], 'ln_in_b': G[('ln_in_b', -1)][0]}
    st = lambda n: jnp.stack([G[(n, l)] for l in range(DEPTH)])
    g_att, g_rec = st('w_att'), st('w_rec')
    out['w_in'] = jnp.concatenate([g_att[..., :1536], g_rec[..., 1024:1024 + N_HEADS], g_rec[..., :1024],
                                   g_att[..., 1536:]], -1)
    out['w_gate'] = st('w_gate').transpose(0, 2, 1, 3, 4).reshape(DEPTH, N_BRANCH, D_MODEL, D_MODEL)
    out['w_branch'] = st('w_branch').transpose(0, 2, 3, 1, 4).reshape(DEPTH, N_BRANCH, BRANCH_WIDTH, D_MODEL)
    out['w_ff1'] = st('w_ff1').transpose(0, 2, 1, 3).reshape(DEPTH, D_MODEL, D_FF)
    out['w_ff2'] = st('w_ff2').reshape(DEPTH, D_FF, D_MODEL)
    out['w_out'] = st('w_out').reshape(DEPTH, D_MODEL, D_MODEL)
    out['b_gate'] = st('b_gate').reshape(DEPTH, N_BRANCH, D_MODEL)
    for n in ('ln1_g', 'ln1_b', 'ln2_g', 'ln2_b'):
        out[n] = st(n)[:, 0]
    for n in ('b_forget', 'conv_w', 'conv_b', 'w_r', 'b_r', 'w_i', 'b_i', 'lru_lambda', 'rel_bias'):
        out[n] = st(n)
    return out


HBM_SPEC = pl.BlockSpec(memory_space=pl.ANY)
N_CHIPS = 4
PACK_COLS = 1024


def _place():
    x, y, c = lax.axis_index("x"), lax.axis_index("y"), lax.axis_index("c")
    chips = [(1 - x, y), (x, 1 - y), (1 - x, 1 - y)]
    return x, y, c, chips


def _remote(src, dst, send_sems, recv_sems, k, to):
    return pltpu.make_async_remote_copy(src_ref=src, dst_ref=dst, send_sem=send_sems.at[k], recv_sem=recv_sems.at[k],
                                        device_id=to, device_id_type=MESH)


def _gather_layers(params, *, name):
    n = len(params)

    def body(*refs):
        ins, outs = refs[:n], refs[n:2 * n]
        send_sems, recv_sems = refs[2 * n:]
        x, y, c, chips = _place()
        me, sibling, k = (x, y, c), (x, y, 1 - c), 2 * x + y
        own = [_remote(ins[p], outs[p].at[k], send_sems, recv_sems, 6 * n + p, sibling) for p in range(n)]
        for cp in own:
            cp.start()
        first, passed = [], []
        for p in range(n):
            for j, (cx, cy) in enumerate(chips):
                cp = _remote(ins[p].at[c], outs[p].at[k, c], send_sems, recv_sems, 6 * p + j, (cx, cy, c))
                cp.start()
                first.append(cp)
        for p in range(n):
            for j, (cx, cy) in enumerate(chips):
                blk = outs[p].at[2 * cx + cy, c]
                _remote(blk, blk, send_sems, recv_sems, 6 * p + j, me).wait_recv()
                cp = _remote(blk, blk, send_sems, recv_sems, 6 * p + 3 + j, sibling)
                cp.start()
                passed.append(cp)
        for p in range(n):
            for j, (cx, cy) in enumerate(chips):
                blk = outs[p].at[2 * cx + cy, 1 - c]
                _remote(blk, blk, send_sems, recv_sems, 6 * p + 3 + j, me).wait_recv()
        for cp in first + passed:
            cp.wait_send()
        for cp in own:
            cp.wait()

    return pl.pallas_call(
        body, name=name, in_specs=[HBM_SPEC] * n, out_specs=[HBM_SPEC] * n,
        out_shape=[jax.ShapeDtypeStruct((N_CHIPS,) + a.shape, a.dtype) for a in params],
        scratch_shapes=[pltpu.SemaphoreType.DMA((7 * n,)), pltpu.SemaphoreType.DMA((7 * n,))],
    )(*params)


def _pair_exchange(g0, g1, *, name):
    n = len(g0)

    def body(*refs):
        a0, a1, outs = refs[:n], refs[n:2 * n], refs[2 * n:3 * n]
        send_sems, recv_sems = refs[3 * n:]
        x, y, c, _ = _place()
        sibling = (x, y, 1 - c)

        @pl.when(c == 0)
        def _():
            for p in range(n):
                _remote(a1[p], outs[p], send_sems, recv_sems, p, sibling).start()

        @pl.when(c == 1)
        def _():
            for p in range(n):
                _remote(a0[p], outs[p], send_sems, recv_sems, p, sibling).start()

        for p in range(n):
            _remote(a0[p], outs[p], send_sems, recv_sems, p, sibling).wait()

    return pl.pallas_call(
        body, name=name, in_specs=[HBM_SPEC] * (2 * n), out_specs=[HBM_SPEC] * n,
        out_shape=[jax.ShapeDtypeStruct(a.shape, a.dtype) for a in g0],
        scratch_shapes=[pltpu.SemaphoreType.DMA((n,)), pltpu.SemaphoreType.DMA((n,))],
    )(*g0, *g1)


def _chip_exchange(s, *, name):
    n = len(s)

    def body(*refs):
        ins, outs = refs[:n], refs[n:2 * n]
        send_sems, recv_sems = refs[2 * n:]
        x, y, c, chips = _place()
        k = 2 * x + y
        cps = [_remote(ins[p].at[2 * cx + cy], outs[p].at[k], send_sems, recv_sems, 3 * p + j, (cx, cy, c))
               for p in range(n) for j, (cx, cy) in enumerate(chips)]
        for cp in cps:
            cp.start()
        for p in range(n):
            for j, (cx, cy) in enumerate(chips):
                slot = outs[p].at[2 * cx + cy]
                _remote(slot, slot, send_sems, recv_sems, 3 * p + j, (x, y, c)).wait_recv()
        for cp in cps:
            cp.wait_send()

    return pl.pallas_call(
        body, name=name, in_specs=[HBM_SPEC] * n, out_specs=[HBM_SPEC] * n,
        out_shape=[jax.ShapeDtypeStruct(a.shape, a.dtype) for a in s],
        scratch_shapes=[pltpu.SemaphoreType.DMA((3 * n,)), pltpu.SemaphoreType.DMA((3 * n,))],
    )(*s)


def _pair_swap(r, *, name):
    n = len(r)

    def body(*refs):
        ins, outs = refs[:n], refs[n:2 * n]
        send_sems, recv_sems = refs[2 * n:]
        x, y, c, _ = _place()
        cps = [_remote(ins[p], outs[p], send_sems, recv_sems, p, (x, y, 1 - c)) for p in range(n)]
        for cp in cps:
            cp.start()
        for cp in cps:
            cp.wait()

    return pl.pallas_call(
        body, name=name, in_specs=[HBM_SPEC] * n, out_specs=[HBM_SPEC] * n,
        out_shape=[jax.ShapeDtypeStruct(a.shape, a.dtype) for a in r],
        scratch_shapes=[pltpu.SemaphoreType.DMA((n,)), pltpu.SemaphoreType.DMA((n,))],
    )(*r)


def _gather8(v, *, name):
    R, C = v.shape
    flips = [(bx, by, bc) for bx in (0, 1) for by in (0, 1) for bc in (0, 1)][1:]

    def body(v_ref, out_ref, send_sems, recv_sems, local_sem):
        x, y, c, _ = _place()
        flip = lambda a, b: 1 - a if b else a
        mine = out_ref.at[4 * x + 2 * y + c]
        local = pltpu.make_async_copy(v_ref, mine, local_sem)
        local.start()
        peers = [(flip(x, bx), flip(y, by), flip(c, bc)) for bx, by, bc in flips]
        cps = [_remote(v_ref, mine, send_sems, recv_sems, j, peer) for j, peer in enumerate(peers)]
        for cp in cps:
            cp.start()
        for j, (px, py, pc) in enumerate(peers):
            slot = out_ref.at[4 * px + 2 * py + pc]
            _remote(slot, slot, send_sems, recv_sems, j, (x, y, c)).wait_recv()
        for cp in cps:
            cp.wait_send()
        local.wait()

    return pl.pallas_call(
        body, name=name, in_specs=[HBM_SPEC], out_specs=HBM_SPEC,
        out_shape=jax.ShapeDtypeStruct((8, R, C), v.dtype),
        scratch_shapes=[pltpu.SemaphoreType.DMA((7,)), pltpu.SemaphoreType.DMA((7,)), pltpu.SemaphoreType.DMA],
    )(v)


def _row_block(rows, cols, limit=256 * 1024):
    if rows * cols <= limit:
        return rows
    for br in range(limit // cols // 8 * 8, 0, -8):
        if rows % br == 0:
            return br
    return rows


def _sum_slots(buf, *, name):
    n, R, C = buf.shape
    br = _row_block(R, C)

    def kern(b_ref, o_ref):
        acc = b_ref[0].astype(F32)
        for s in range(1, n):
            acc = acc + b_ref[s].astype(F32)
        o_ref[...] = acc

    return pl.pallas_call(
        kern, name=name, grid=(pl.cdiv(R, br),),
        in_specs=[pl.BlockSpec((n, br, C), lambda i: (0, i, 0))],
        out_specs=pl.BlockSpec((br, C), lambda i: (i, 0)),
        out_shape=jax.ShapeDtypeStruct((R, C), F32),
        compiler_params=_cp(("arbitrary",)),
    )(buf)


def _scalar(s):
    return jnp.reshape(s, (1,)).astype(jnp.int32)


def _sum_pair(g0, g1, other, c, *, name):
    _, R, C = g0.shape
    br = _row_block(R, C)

    def kern(c_ref, g0_ref, g1_ref, o_ref, out_ref):
        own = jnp.where(c_ref[0] == 0, g0_ref[...], g1_ref[...])
        out_ref[...] = (own + o_ref[...]).astype(out_ref.dtype)

    blk = (None, br, C)
    return pl.pallas_call(
        kern, name=name,
        grid_spec=pltpu.PrefetchScalarGridSpec(
            num_scalar_prefetch=1, grid=(N_CHIPS, R // br),
            in_specs=[pl.BlockSpec(blk, lambda k, i, cr: (k, i * (1 - cr[0]), 0)),
                      pl.BlockSpec(blk, lambda k, i, cr: (k, i * cr[0], 0)),
                      pl.BlockSpec(blk, lambda k, i, cr: (k, i, 0))],
            out_specs=pl.BlockSpec(blk, lambda k, i, cr: (k, i, 0))),
        out_shape=jax.ShapeDtypeStruct((N_CHIPS, R, C), BF16),
        compiler_params=_cp(("arbitrary", "arbitrary")),
    )(_scalar(c), g0, g1, other)


def _sum_chips(s, got, k, *, name):
    _, R, C = s.shape
    br = _row_block(R, C)

    def kern(k_ref, s_ref, a_ref, b_ref, c_ref, out_ref):
        out_ref[...] = ((s_ref[...].astype(F32) + a_ref[...].astype(F32)) + b_ref[...].astype(F32)) \
            + c_ref[...].astype(F32)

    blk = (None, br, C)
    peer = lambda d: pl.BlockSpec(blk, lambda i, kr: ((kr[0] + d) % N_CHIPS, i, 0))
    return pl.pallas_call(
        kern, name=name,
        grid_spec=pltpu.PrefetchScalarGridSpec(
            num_scalar_prefetch=1, grid=(R // br,),
            in_specs=[peer(0), peer(1), peer(2), peer(3)],
            out_specs=pl.BlockSpec((br, C), lambda i, kr: (i, 0))),
        out_shape=jax.ShapeDtypeStruct((R, C), F32),
        compiler_params=_cp(("arbitrary",)),
    )(_scalar(k), s, got, got, got)


def _adam_math(w, g, m, v):
    nm = ADAM_B1 * m + (1.0 - ADAM_B1) * g
    nv = ADAM_B2 * v + (1.0 - ADAM_B2) * jnp.square(g)
    m_hat = nm / (1.0 - ADAM_B1 ** ADAM_STEP)
    v_hat = nv / (1.0 - ADAM_B2 ** ADAM_STEP)
    return -ADAM_LR * (m_hat / (jnp.sqrt(v_hat) + ADAM_EPS) + ADAM_WD * w), nm, nv


def _adamw_layers(w, mine, theirs, m, v, c, *, name):
    shape = w.shape
    R, C = mine.shape
    w3, m3, v3 = (a.reshape(DEPTH, R, C) for a in (w, m, v))
    br = _row_block(R, C)

    def kern(c_ref, w_ref, a_ref, b_ref, m_ref, v_ref, g_ref, d_ref, nm_ref, nv_ref):
        g = jnp.where(pl.program_id(0) == c_ref[0], a_ref[...], b_ref[...])
        g_ref[...] = g
        d_ref[...], nm_ref[...], nv_ref[...] = _adam_math(w_ref[...], g, m_ref[...], v_ref[...])

    lay = pl.BlockSpec((None, br, C), lambda l, i, cr: (l, i, 0))
    outs = pl.pallas_call(
        kern, name=name,
        grid_spec=pltpu.PrefetchScalarGridSpec(
            num_scalar_prefetch=1, grid=(DEPTH, R // br),
            in_specs=[lay,
                      pl.BlockSpec((br, C), lambda l, i, cr: (jnp.where(l == cr[0], i, 0), 0)),
                      pl.BlockSpec((br, C), lambda l, i, cr: (jnp.where(l == cr[0], 0, i), 0)),
                      lay, lay],
            out_specs=[lay] * 4),
        out_shape=[jax.ShapeDtypeStruct((DEPTH, R, C), F32)] * 4,
        compiler_params=_cp(("arbitrary", "arbitrary")),
    )(_scalar(c), w3, mine, theirs, m3, v3)
    return [o.reshape(shape) for o in outs]


def _adamw(w, g, m, v, *, name):
    shape = w.shape
    cols = shape[-1]
    w2, g2, m2, v2 = (a.reshape(-1, cols) for a in (w, g, m, v))
    rows = w2.shape[0]
    br = _row_block(rows, cols)

    def kern(w_ref, g_ref, m_ref, v_ref, d_ref, nm_ref, nv_ref):
        gv = g_ref[...]
        nm = ADAM_B1 * m_ref[...] + (1.0 - ADAM_B1) * gv
        nv = ADAM_B2 * v_ref[...] + (1.0 - ADAM_B2) * jnp.square(gv)
        m_hat = nm / (1.0 - ADAM_B1 ** ADAM_STEP)
        v_hat = nv / (1.0 - ADAM_B2 ** ADAM_STEP)
        d_ref[...] = -ADAM_LR * (m_hat / (jnp.sqrt(v_hat) + ADAM_EPS) + ADAM_WD * w_ref[...])
        nm_ref[...] = nm
        nv_ref[...] = nv

    spec = pl.BlockSpec((br, cols), lambda i: (i, 0))
    outs = pl.pallas_call(
        kern, name=name, grid=(rows // br,), in_specs=[spec] * 4, out_specs=[spec] * 3,
        out_shape=[jax.ShapeDtypeStruct((rows, cols), F32)] * 3,
        compiler_params=_cp(("arbitrary",)),
    )(w2, g2, m2, v2)
    return [o.reshape(shape) for o in outs]


_NAMES = ['ln_in_g', 'ln_in_b', 'w_in', 'b_forget', 'conv_w', 'conv_b', 'w_r', 'b_r', 'w_i', 'b_i', 'lru_lambda',
          'rel_bias', 'w_branch', 'w_gate', 'b_gate', 'w_out', 'ln1_g', 'ln1_b', 'w_ff1', 'w_ff2', 'ln2_g', 'ln2_b']
_BIG = {'w_in': 2, 'w_branch': 3, 'w_gate': 2, 'w_out': 1, 'w_ff1': 2, 'w_ff2': 1}
_SMALL_SHARDED = {'b_gate': 2, 'conv_w': 2, 'rel_bias': 2}
_SHARDED = {**_BIG, **_SMALL_SHARDED}
_REPLICATED = [n for n in _NAMES if n not in _SHARDED]
_TILE = 8 * LANES


def _tiles(a, cols):
    flat = a.reshape(-1)
    per = 8 * cols
    flat = jnp.pad(flat, (0, (-flat.shape[0]) % per))
    return flat.reshape(-1, cols)


def _pack(arrs, cols):
    return jnp.concatenate([_tiles(a, cols) for a in arrs], axis=0)


def _unpack(packed, like, cols):
    out, r0 = [], 0
    for a in like:
        n = math.prod(a.shape)
        rows = -(-n // (8 * cols)) * 8
        out.append(packed[r0:r0 + rows].reshape(-1)[:n].reshape(a.shape))
        r0 += rows
    return out


def _chip_major(G, l):
    g_att, g_rec = G[('w_att', l)], G[('w_rec', l)]
    w_in = jnp.concatenate([g_att[:, :1536], g_rec[:, 1024:1024 + N_HEADS], g_rec[:, :1024], g_att[:, 1536:]], -1)
    big = [w_in.reshape(D_MODEL, N_CHIPS, D_IN // N_CHIPS).transpose(1, 0, 2),
           G[('w_branch', l)].reshape(N_CHIPS, N_BRANCH * BRANCH_WIDTH, BRANCH_WIDTH),
           G[('w_gate', l)].reshape(N_CHIPS, N_BRANCH * (D_MODEL // N_CHIPS), D_MODEL),
           G[('w_out', l)], G[('w_ff1', l)], G[('w_ff2', l)]]
    per_chip = lambda g, rows: g.reshape(rows, N_CHIPS, -1).transpose(1, 0, 2)
    bg = per_chip(G[('b_gate', l)], N_BRANCH)
    cw = per_chip(G[('conv_w', l)], CONV_WIDTH)
    rb = per_chip(G[('rel_bias', l)], N_HEADS)
    small = jnp.stack([_pack([bg[j], cw[j], rb[j]], LANES) for j in range(N_CHIPS)])
    return big + [small]


def _unshard(blocks, axis):
    return jnp.concatenate([blocks[k] for k in range(N_CHIPS)], axis=axis)


def kernel(x, ln_in_g, ln_in_b, w_in, b_forget, conv_w, conv_b, w_r, b_r, w_i, b_i, lru_lambda, rel_bias, w_branch, w_gate, b_gate, w_out, ln1_g, ln1_b, w_ff1, w_ff2, ln2_g, ln2_b, loss_target, m_ln_in_g, m_ln_in_b, m_w_in, m_b_forget, m_conv_w, m_conv_b, m_w_r, m_b_r, m_w_i, m_b_i, m_lru_lambda, m_rel_bias, m_w_branch, m_w_gate, m_b_gate, m_w_out, m_ln1_g, m_ln1_b, m_w_ff1, m_w_ff2, m_ln2_g, m_ln2_b, v_ln_in_g, v_ln_in_b, v_w_in, v_b_forget, v_conv_w, v_conv_b, v_w_r, v_b_r, v_w_i, v_b_i, v_lru_lambda, v_rel_bias, v_w_branch, v_w_gate, v_b_gate, v_w_out, v_ln1_g, v_ln1_b, v_w_ff1, v_w_ff2, v_ln2_g, v_ln2_b):
    w = dict(zip(_NAMES, (ln_in_g, ln_in_b, w_in, b_forget, conv_w, conv_b, w_r, b_r, w_i, b_i, lru_lambda, rel_bias,
                          w_branch, w_gate, b_gate, w_out, ln1_g, ln1_b, w_ff1, w_ff2, ln2_g, ln2_b)))
    m = dict(zip(_NAMES, (m_ln_in_g, m_ln_in_b, m_w_in, m_b_forget, m_conv_w, m_conv_b, m_w_r, m_b_r, m_w_i, m_b_i,
                          m_lru_lambda, m_rel_bias, m_w_branch, m_w_gate, m_b_gate, m_w_out, m_ln1_g, m_ln1_b,
                          m_w_ff1, m_w_ff2, m_ln2_g, m_ln2_b)))
    v = dict(zip(_NAMES, (v_ln_in_g, v_ln_in_b, v_w_in, v_b_forget, v_conv_w, v_conv_b, v_w_r, v_b_r, v_w_i, v_b_i,
                          v_lru_lambda, v_rel_bias, v_w_branch, v_w_gate, v_b_gate, v_w_out, v_ln1_g, v_ln1_b,
                          v_w_ff1, v_w_ff2, v_ln2_g, v_ln2_b)))
    c = lax.axis_index("c")

    small_like = [w[n] for n in _SMALL_SHARDED]
    small_pack = jnp.stack([_pack([a[l] for a in small_like], LANES) for l in range(DEPTH)])
    gathered = _gather_layers([w[n].astype(BF16) for n in _BIG] + [small_pack], name='gather_weights')
    full = {n: w[n] for n in _REPLICATED}
    for n, blocks in zip(_BIG, gathered):
        full[n] = _unshard(blocks, _BIG[n])
    small_blocks = [[_unpack(gathered[-1][k, l], [a[l] for a in small_like], LANES) for l in range(DEPTH)]
                    for k in range(N_CHIPS)]
    for i, n in enumerate(_SMALL_SHARDED):
        full[n] = jnp.concatenate([jnp.stack([small_blocks[k][l][i] for l in range(DEPTH)])
                                   for k in range(N_CHIPS)], axis=_SMALL_SHARDED[n])

    loss, gx, G = _device_step(x[0], loss_target[0], _prep_weights(full))
    k = 2 * lax.axis_index("x") + lax.axis_index("y")

    by_layer = [_chip_major(G, l) for l in range(DEPTH)]
    from_sibling = _pair_exchange(*by_layer, name='grad_pair_exchange')
    pair_sum = [_sum_pair(a0, a1, o, c, name='grad_pair_sum') for a0, a1, o in zip(*by_layer, from_sibling)]
    from_chips = _chip_exchange(pair_sum, name='grad_chip_exchange')
    mine = [_sum_chips(s, got, k, name='grad_chip_sum') for s, got in zip(pair_sum, from_chips)]
    theirs = _pair_swap(mine, name='grad_pair_swap')

    rep_like = [w[n] for n in _REPLICATED]
    g_rep_dev = {'ln_in_g': G[('ln_in_g', -1)][0], 'ln_in_b': G[('ln_in_b', -1)][0]}
    for n in _REPLICATED[2:]:
        g_rep_dev[n] = jnp.stack([G[(n, l)].reshape(w[n].shape[1:]) for l in range(DEPTH)])
    rep_all = _gather8(_pack([g_rep_dev[n] for n in _REPLICATED], LANES), name='grad_gather8')
    g_rep = dict(zip(_REPLICATED, _unpack(_sum_slots(rep_all, name='grad_sum8'), rep_like, LANES)))

    grads, delta, new_m, new_v = {}, {}, {}, {}
    for n, a, b in zip(_BIG, mine, theirs):
        grads[n], delta[n], new_m[n], new_v[n] = _adamw_layers(w[n], a, b, m[n], v[n], c, name='adamw')
    small_layers = [jnp.where(c == l, mine[-1], theirs[-1]) for l in range(DEPTH)]
    small_shards = [_unpack(s, [w[n][0] for n in _SMALL_SHARDED], LANES) for s in small_layers]
    g_shard = {n: jnp.stack([small_shards[l][i] for l in range(DEPTH)]) for i, n in enumerate(_SMALL_SHARDED)}
    small = _REPLICATED + list(_SMALL_SHARDED)
    for n in small:
        grads[n] = g_rep[n] if n in g_rep else g_shard[n]
    packs = [_pack([d[n] for n in small], LANES) for d in (w, grads, m, v)]
    outs = _adamw(*packs, name='adamw_small')
    small_like_all = [w[n] for n in small]
    for d, o in zip((delta, new_m, new_v), outs):
        d.update(zip(small, _unpack(o, small_like_all, LANES)))

    loss = lax.psum(loss, ("x", "y", "c"))
    return (loss, gx[None], *[grads[n] for n in _NAMES], *[delta[n] for n in _NAMES],
            *[new_m[n] for n in _NAMES], *[new_v[n] for n in _NAMES])
```

```python
import functools
import math

import jax
import jax.numpy as jnp
from jax import lax
from jax.experimental import pallas as pl
from jax.experimental.pallas import tpu as pltpu

F32 = jnp.float32
BF16 = jnp.bfloat16

D_MODEL = 2048
DEPTH = 2
CHUNK = 64
HEAD_DIM = 128
N_BRANCH = 4
BRANCH_WIDTH = 512
N_HEADS = 4
CONV_WIDTH = 4
LRU_C = 8.0
LOOKBACK_CHUNKS = 8
BAND = (LOOKBACK_CHUNKS + 1) * CHUNK
PAD_ROWS = LOOKBACK_CHUNKS * CHUNK
REL_CLIP = 256
REL_TABLE = REL_CLIP + CHUNK
REL_PAD = 384
D_FF = 4 * D_MODEL
D_IN = 5636
ALPHA = (2.0 * DEPTH) ** 0.25
LN_EPS = 1e-5
SCALE = HEAD_DIM ** -0.5

ADAM_LR = 0.001
ADAM_B1 = 0.9
ADAM_B2 = 0.999
ADAM_EPS = 1e-08
ADAM_WD = 0.01
ADAM_STEP = 10

N_ATT = 9 * BRANCH_WIDTH
N_REC = 2 * BRANCH_WIDTH + 128

V7X_VMEM_LIMIT = 56 * 1024 * 1024
LANES = 128
ATT_BLOCK = 256
ATT_KEYS = 1024

NT = (((1,), (1,)), ((), ()))
TN = (((0,), (0,)), ((), ()))
NN = (((1,), (0,)), ((), ()))

MESH = pl.DeviceIdType.MESH


def _cp(sem=None):
    return pltpu.CompilerParams(dimension_semantics=sem, vmem_limit_bytes=V7X_VMEM_LIMIT)


def _dot(a, b, dims=NN):
    return lax.dot_general(a, b, dims, preferred_element_type=F32)


def _pick(n, prefs):
    for p in prefs:
        if n % p == 0:
            return p
    return n


def _split3(x):
    hi = x.astype(BF16)
    r1 = x - hi.astype(F32)
    mid = r1.astype(BF16)
    lo = (r1 - mid.astype(F32)).astype(BF16)
    return hi, mid, lo


def _split2(x):
    hi = x.astype(BF16)
    lo = (x - hi.astype(F32)).astype(BF16)
    return hi, lo


def _sigmoid(z):
    return 1.0 / (1.0 + jnp.exp(-z))


def _log_sigmoid(z):
    return jnp.minimum(z, 0.0) - jnp.log(1.0 + jnp.exp(-jnp.abs(z)))


def _mm(a, b, *, name, ta=False, tb=False, out_dtypes=(F32,), epilogue=None, extras=(),
        bm=None, bn=None, bk=None, out_map=None):
    M, K = (a.shape[1], a.shape[0]) if ta else a.shape
    N = b.shape[0] if tb else b.shape[1]
    bm = bm or _pick(M, (1024, 512, 256, 128))
    bn = bn or _pick(N, (1024, 1536, 1152, 512, 256, 128))
    bk = bk or _pick(K, (2048, 1536, 1024, 1152, 512, 256, 128))
    nk = K // bk
    a_spec = pl.BlockSpec((bk, bm), lambda i, j, k: (k, i)) if ta else pl.BlockSpec((bm, bk), lambda i, j, k: (i, k))
    b_spec = pl.BlockSpec((bn, bk), lambda i, j, k: (j, k)) if tb else pl.BlockSpec((bk, bn), lambda i, j, k: (k, j))
    ex_specs = [pl.BlockSpec((bm, bn), lambda i, j, k: (i, j)) if kind == 'mn'
                else pl.BlockSpec((1, bn), lambda i, j, k: (0, j)) for _, kind in extras]
    n_ex, n_out = len(extras), len(out_dtypes)
    dims = TN if ta else (NT if tb else NN)

    def kern(*refs):
        a_ref, b_ref = refs[0], refs[1]
        ex_refs = refs[2:2 + n_ex]
        out_refs = refs[2 + n_ex:2 + n_ex + n_out]
        acc_ref = refs[-1]
        k = pl.program_id(2)
        part = _dot(a_ref[...].astype(BF16), b_ref[...].astype(BF16), dims)

        @pl.when(k == 0)
        def _():
            acc_ref[...] = part

        @pl.when(k > 0)
        def _():
            acc_ref[...] += part

        @pl.when(k == nk - 1)
        def _():
            acc = acc_ref[...]
            outs = (acc,) if epilogue is None else epilogue(acc, *[r[...] for r in ex_refs])
            for o_ref, o in zip(out_refs, outs):
                o_ref[...] = o.astype(o_ref.dtype).reshape(o_ref.shape)

    if out_map is None:
        out_specs = [pl.BlockSpec((bm, bn), lambda i, j, k: (i, j)) for _ in out_dtypes]
        out_shape = [jax.ShapeDtypeStruct((M, N), dt) for dt in out_dtypes]
    else:
        shape, block, index = out_map
        out_specs = [pl.BlockSpec(block, lambda i, j, k: index(i, j))]
        out_shape = [jax.ShapeDtypeStruct(shape, out_dtypes[0])]
    res = pl.pallas_call(
        kern, name=name, grid=(M // bm, N // bn, nk),
        in_specs=[a_spec, b_spec] + ex_specs,
        out_specs=out_specs,
        out_shape=out_shape,
        scratch_shapes=[pltpu.VMEM((bm, bn), F32)],
        compiler_params=_cp(("parallel", "parallel", "arbitrary")),
    )(a, b, *[e for e, _ in extras])
    return res[0] if n_out == 1 else res


def _ln_fwd(h, g, b, *, name):
    T, D = h.shape
    bt = _pick(T, (512, 256, 128))

    def kern(h_ref, g_ref, b_ref, y_ref, yb_ref):
        x = h_ref[...]
        mu = jnp.mean(x, axis=-1, keepdims=True)
        xc = x - mu
        var = jnp.mean(xc * xc, axis=-1, keepdims=True)
        y = xc * lax.rsqrt(var + LN_EPS) * g_ref[...] + b_ref[...]
        y_ref[...] = y
        yb_ref[...] = y.astype(BF16)

    row = pl.BlockSpec((bt, D), lambda i: (i, 0))
    vec = pl.BlockSpec((1, D), lambda i: (0, 0))
    return pl.pallas_call(
        kern, name=name, grid=(T // bt,), in_specs=[row, vec, vec], out_specs=[row, row],
        out_shape=[jax.ShapeDtypeStruct((T, D), F32), jax.ShapeDtypeStruct((T, D), BF16)],
        compiler_params=_cp(("arbitrary",)),
    )(h, g.reshape(1, D), b.reshape(1, D))


def _ln_bwd(h, dy, g, *, name):
    T, D = h.shape
    bt = _pick(T, (512, 256, 128))

    def kern(h_ref, dy_ref, g_ref, dh_ref, dhb_ref, dg_ref, db_ref):
        i = pl.program_id(0)
        x = h_ref[...]
        dyv = dy_ref[...]
        mu = jnp.mean(x, axis=-1, keepdims=True)
        xc = x - mu
        var = jnp.mean(xc * xc, axis=-1, keepdims=True)
        rstd = lax.rsqrt(var + LN_EPS)
        xhat = xc * rstd
        dxh = dyv * g_ref[...]
        m1 = jnp.mean(dxh, axis=-1, keepdims=True)
        m2 = jnp.mean(dxh * xhat, axis=-1, keepdims=True)
        dh = rstd * (dxh - m1 - xhat * m2)
        dh_ref[...] = dh
        dhb_ref[...] = dh.astype(BF16)
        pg = jnp.sum(dyv * xhat, axis=0, keepdims=True)
        pb = jnp.sum(dyv, axis=0, keepdims=True)

        @pl.when(i == 0)
        def _():
            dg_ref[...] = pg
            db_ref[...] = pb

        @pl.when(i > 0)
        def _():
            dg_ref[...] += pg
            db_ref[...] += pb

    row = pl.BlockSpec((bt, D), lambda i: (i, 0))
    vec = pl.BlockSpec((1, D), lambda i: (0, 0))
    return pl.pallas_call(
        kern, name=name, grid=(T // bt,), in_specs=[row, row, vec], out_specs=[row, row, vec, vec],
        out_shape=[jax.ShapeDtypeStruct((T, D), F32), jax.ShapeDtypeStruct((T, D), BF16),
                   jax.ShapeDtypeStruct((1, D), F32), jax.ShapeDtypeStruct((1, D), F32)],
        compiler_params=_cp(("arbitrary",)),
    )(h, dy, g.reshape(1, D))


def _loss_head(y, tgt, *, name):
    T, D = y.shape
    bt = _pick(T, (512, 256, 128))

    def kern(y_ref, t_ref, dy_ref, loss_ref):
        i = pl.program_id(0)
        e = y_ref[...] - t_ref[...]
        dy_ref[...] = e * (1.0 / D)
        part = 0.5 * jnp.sum(jnp.sum(e * e, axis=-1, keepdims=True) * (1.0 / D), axis=0, keepdims=True)
        part = jnp.broadcast_to(part, (8, LANES))

        @pl.when(i == 0)
        def _():
            loss_ref[...] = part

        @pl.when(i > 0)
        def _():
            loss_ref[...] += part

    row = pl.BlockSpec((bt, D), lambda i: (i, 0))
    return pl.pallas_call(
        kern, name=name, grid=(T // bt,), in_specs=[row, row],
        out_specs=[row, pl.BlockSpec((8, LANES), lambda i: (0, 0))],
        out_shape=[jax.ShapeDtypeStruct((T, D), F32), jax.ShapeDtypeStruct((8, LANES), F32)],
        compiler_params=_cp(("arbitrary",)),
    )(y, tgt)


def _tri(n, upper):
    r = lax.broadcasted_iota(jnp.int32, (n, n), 0)
    c = lax.broadcasted_iota(jnp.int32, (n, n), 1)
    return jnp.where((c >= r) if upper else (c <= r), 1.0, 0.0).astype(BF16)


def _forget_fwd(ff, bf, *, name):
    T = ff.shape[0]
    bt = 256

    def kern(ff_ref, bf_ref, out_ref, carry):
        i = pl.program_id(0)

        @pl.when(i == 0)
        def _():
            carry[...] = jnp.zeros_like(carry)

        ls = _log_sigmoid(ff_ref[...] + bf_ref[...])
        tri = _tri(bt, upper=False)
        hi, mid, lo = _split3(ls)
        cs = _dot(tri, hi) + _dot(tri, mid) + _dot(tri, lo) + carry[0:1, :]
        out_ref[...] = cs
        carry[...] = jnp.broadcast_to(cs[bt - 1:bt, :], carry.shape)

    return pl.pallas_call(
        kern, name=name, grid=(T // bt,),
        in_specs=[pl.BlockSpec((bt, LANES), lambda i: (i, 0)), pl.BlockSpec((1, LANES), lambda i: (0, 0))],
        out_specs=pl.BlockSpec((bt, LANES), lambda i: (i, 0)),
        out_shape=jax.ShapeDtypeStruct((T, LANES), F32),
        scratch_shapes=[pltpu.VMEM((8, LANES), F32)],
        compiler_params=_cp(("arbitrary",)),
    )(ff, bf)


def _forget_bwd(dFk, dFq, ff, bf, *, name):
    T = ff.shape[0]
    bt = 256
    nb = T // bt

    def kern(dFk_ref, dFq_ref, ff_ref, bf_ref, dff_ref, dbf_ref, carry):
        i = pl.program_id(0)

        @pl.when(i == 0)
        def _():
            carry[...] = jnp.zeros_like(carry)
            dbf_ref[...] = jnp.zeros_like(dbf_ref)

        tri = _tri(bt, upper=True)
        hi, mid, lo = _split3(dFk_ref[...] + dFq_ref[...])
        rs = _dot(tri, hi) + _dot(tri, mid) + _dot(tri, lo) + carry[0:1, :]
        carry[...] = jnp.broadcast_to(rs[0:1, :], carry.shape)
        z = ff_ref[...] + bf_ref[...]
        dff = rs * _sigmoid(-z)
        dff_ref[...] = dff.astype(dff_ref.dtype)
        dbf_ref[...] += jnp.sum(dff, axis=0, keepdims=True)

    rev = pl.BlockSpec((bt, LANES), lambda i: (nb - 1 - i, 0))
    vec = pl.BlockSpec((1, LANES), lambda i: (0, 0))
    return pl.pallas_call(
        kern, name=name, grid=(nb,), in_specs=[rev, rev, rev, vec], out_specs=[rev, vec],
        out_shape=[jax.ShapeDtypeStruct((T, LANES), BF16), jax.ShapeDtypeStruct((1, LANES), F32)],
        scratch_shapes=[pltpu.VMEM((8, LANES), F32)],
        compiler_params=_cp(("arbitrary",)),
    )(dFk, dFq, ff, bf)


def _head_lane(x, h):
    lane = lax.broadcasted_iota(jnp.int32, x.shape, 1)
    return jnp.sum(jnp.where(lane == h, x, 0.0), axis=1, keepdims=True)


def _att_blocks(T):
    return min(ATT_BLOCK, T), min(ATT_KEYS, T)


def _positions(i, j, bq, bk):
    r = i * bq + lax.broadcasted_iota(jnp.int32, (bq, bk), 0)
    c = j * bk + lax.broadcasted_iota(jnp.int32, (bq, bk), 1)
    return r, c


def _fox_fwd(u_att, fcum, frow, *, name, carry=None):
    T = u_att.shape[0]
    bq, bk = _att_blocks(T)
    nq, nk = T // bq, T // bk
    H = N_HEADS

    def kern(q_ref, k_ref, v_ref, fc_ref, fr_ref, o_ref, lse_ref):
        i = pl.program_id(1)
        q = q_ref[...]
        fq = _head_lane(fc_ref[...], pl.program_id(0))

        def step(j, carry, masked):
            m, l, acc = carry
            off = pl.multiple_of(j * bk, bk)
            k = k_ref[pl.ds(off, bk), :]
            v = v_ref[pl.ds(off, bk), :]
            s = _dot(q, k, NT) * SCALE + (fq - fr_ref[j])
            if masked:
                r, c = _positions(i, j, bq, bk)
                s = jnp.where(c <= r, s, -jnp.inf)
            m_new = jnp.maximum(m, jnp.max(s, axis=1, keepdims=True))
            a = jnp.exp(m - m_new)
            p = jnp.exp(s - m_new)
            l = a * l + jnp.sum(p, axis=1, keepdims=True)
            acc = a * acc + _dot(p.astype(BF16), v)
            return m_new, l, acc

        init = (jnp.full((bq, 1), -1e30, F32), jnp.zeros((bq, 1), F32), jnp.zeros((bq, HEAD_DIM), F32))
        nfull = (i * bq) // bk
        carry = lax.fori_loop(0, nfull, lambda j, cr: step(j, cr, False), init)
        m, l, acc = step(nfull, carry, True)
        o_ref[...] = (acc / l).astype(o_ref.dtype)
        lse_ref[...] = m + jnp.log(l)

    return _call_with_carry(
        kern, name=name, grid=(H, nq), carry=carry,
        in_specs=[pl.BlockSpec((bq, HEAD_DIM), lambda h, i: (i, h)),
                  pl.BlockSpec((T, HEAD_DIM), lambda h, i: (0, 4 + h)),
                  pl.BlockSpec((T, HEAD_DIM), lambda h, i: (0, 8 + h)),
                  pl.BlockSpec((bq, LANES), lambda h, i: (i, 0)),
                  pl.BlockSpec((None, nk, 1, bk), lambda h, i: (h, 0, 0, 0))],
        out_specs=[pl.BlockSpec((bq, HEAD_DIM), lambda h, i: (i, h)),
                   pl.BlockSpec((None, bq, 1), lambda h, i: (h, i, 0))],
        out_shape=[jax.ShapeDtypeStruct((T, BRANCH_WIDTH), BF16), jax.ShapeDtypeStruct((H, T, 1), F32)],
        scratch_shapes=[], args=(u_att, u_att, u_att, fcum, frow))


def _row_dot(a, b, *, name):
    T = a.shape[0]
    bt = _pick(T, (512, 256, 128))

    def kern(a_ref, b_ref, o_ref):
        p = a_ref[...].astype(F32) * b_ref[...].astype(F32)
        for h in range(N_HEADS):
            o_ref[h] = jnp.sum(p[:, h * HEAD_DIM:(h + 1) * HEAD_DIM], axis=1, keepdims=True)

    row = pl.BlockSpec((bt, BRANCH_WIDTH), lambda i: (i, 0))
    return pl.pallas_call(
        kern, name=name, grid=(T // bt,), in_specs=[row, row],
        out_specs=pl.BlockSpec((N_HEADS, bt, 1), lambda i: (0, i, 0)),
        out_shape=jax.ShapeDtypeStruct((N_HEADS, T, 1), F32),
        compiler_params=_cp(("arbitrary",)),
    )(a, b)


def _fox_bwd(u_att, do, lse, delta, fcum, frow, *, name, carry=None):
    T = u_att.shape[0]
    bq, bk = _att_blocks(T)
    nq, nk = T // bq, T // bk
    H = N_HEADS

    def kern(q_ref, k_ref, v_ref, do_ref, lse_ref, dl_ref, fc_ref, fr_ref,
             dq_ref, dk_ref, dv_ref, df_ref, dfq_ref, dk_acc, dv_acc, df_acc):
        i = pl.program_id(1)

        @pl.when(i == 0)
        def _():
            dk_acc[...] = jnp.zeros_like(dk_acc)
            dv_acc[...] = jnp.zeros_like(dv_acc)
            df_acc[...] = jnp.zeros_like(df_acc)

        q = q_ref[...]
        dov = do_ref[...]
        fq = _head_lane(fc_ref[...], pl.program_id(0))
        lsev = lse_ref[...]
        dlt = dl_ref[...]

        def step(j, carry, masked):
            dq, dfq = carry
            off = pl.multiple_of(j * bk, bk)
            k = k_ref[pl.ds(off, bk), :]
            v = v_ref[pl.ds(off, bk), :]
            s = _dot(q, k, NT) * SCALE + (fq - fr_ref[j])
            p = jnp.exp(s - lsev)
            if masked:
                r, c = _positions(i, j, bq, bk)
                p = jnp.where(c <= r, p, 0.0)
            dp = _dot(dov, v, NT)
            ds = p * (dp - dlt)
            dsb = ds.astype(BF16)
            dq = dq + _dot(dsb, k)
            dk_acc[pl.ds(off, bk), :] += _dot(dsb, q, TN)
            dv_acc[pl.ds(off, bk), :] += _dot(p.astype(BF16), dov, TN)
            df_acc[j] += -jnp.sum(ds, axis=0, keepdims=True)
            return dq, dfq + jnp.sum(ds, axis=1, keepdims=True)

        nfull = (i * bq) // bk
        carry = lax.fori_loop(0, nfull, lambda j, cr: step(j, cr, False),
                              (jnp.zeros((bq, HEAD_DIM), F32), jnp.zeros((bq, 1), F32)))
        dq, dfq = step(nfull, carry, True)
        dq_ref[...] = (dq * SCALE).astype(dq_ref.dtype)
        dfq_ref[...] = dfq

        @pl.when(i == nq - 1)
        def _():
            dk_ref[...] = (dk_acc[...] * SCALE).astype(dk_ref.dtype)
            dv_ref[...] = dv_acc[...].astype(dv_ref.dtype)
            df_ref[...] = df_acc[...]

    col = lambda: pl.BlockSpec((None, bq, 1), lambda h, i: (h, i, 0))
    return _call_with_carry(
        kern, name=name, grid=(H, nq), carry=carry,
        in_specs=[pl.BlockSpec((bq, HEAD_DIM), lambda h, i: (i, h)),
                  pl.BlockSpec((T, HEAD_DIM), lambda h, i: (0, 4 + h)),
                  pl.BlockSpec((T, HEAD_DIM), lambda h, i: (0, 8 + h)),
                  pl.BlockSpec((bq, HEAD_DIM), lambda h, i: (i, h)),
                  col(), col(), pl.BlockSpec((bq, LANES), lambda h, i: (i, 0)),
                  pl.BlockSpec((None, nk, 1, bk), lambda h, i: (h, 0, 0, 0))],
        out_specs=[pl.BlockSpec((bq, HEAD_DIM), lambda h, i: (i, h)),
                   pl.BlockSpec((T, HEAD_DIM), lambda h, i: (0, h)),
                   pl.BlockSpec((T, HEAD_DIM), lambda h, i: (0, h)),
                   pl.BlockSpec((None, nk, 1, bk), lambda h, i: (h, 0, 0, 0)),
                   pl.BlockSpec((None, bq, 1), lambda h, i: (h, i, 0))],
        out_shape=[jax.ShapeDtypeStruct((T, BRANCH_WIDTH), BF16)] * 3
                  + [jax.ShapeDtypeStruct((H, nk, 1, bk), F32), jax.ShapeDtypeStruct((H, T, 1), F32)],
        scratch_shapes=[pltpu.VMEM((T, HEAD_DIM), F32), pltpu.VMEM((T, HEAD_DIM), F32),
                        pltpu.VMEM((nk, 1, bk), F32)],
        args=(u_att, u_att, u_att, do, lse, delta, fcum, frow))


def _softplus_parts(z):
    t = jnp.exp(-jnp.abs(z))
    sp = jnp.maximum(z, 0.0) + jnp.log(1.0 + t)
    return t, sp


def _sb_tri(B):
    r = lax.broadcasted_iota(jnp.int32, (B, B), 0)
    c = lax.broadcasted_iota(jnp.int32, (B, B), 1)
    suffix = jnp.where(r >= c, 1.0, 0.0).astype(BF16)
    prefix = jnp.where(r <= c, 1.0, 0.0).astype(BF16)
    return suffix, prefix


def _sb_fwd(u_att, *, name, carry=None):
    T = u_att.shape[0]
    B, bk = _att_blocks(T)
    nq, nsub = T // B, bk // B
    H = N_HEADS

    def kern(q_ref, k_ref, v_ref, o_ref):
        i = pl.program_id(1)
        q = q_ref[...]
        suffix, _ = _sb_tri(B)

        def step(j, carry, masked):
            run, acc = carry
            parts = []
            for s in reversed(range(nsub)):
                jb = j * nsub + s
                off = pl.multiple_of(jb * B, B)
                k = k_ref[pl.ds(off, B), :]
                z = _dot(q, k, NT) * SCALE
                _, sp = _softplus_parts(z)
                lg = -sp
                valid = None
                if masked:
                    r, c = _positions(i, jb, B, B)
                    valid = c < r
                    lg = jnp.where(valid, lg, 0.0)
                hi, lo = _split2(lg)
                cum = _dot(hi, suffix) + _dot(lo, suffix)
                parts.append((off, z, cum, jnp.sum(lg, axis=1, keepdims=True), valid))
            for off, z, cum, rs, valid in parts:
                a = jnp.exp(z + cum + run)
                if masked:
                    a = jnp.where(valid, a, 0.0)
                acc = acc + _dot(a.astype(BF16), v_ref[pl.ds(off, B), :])
                run = run + rs
            return run, acc

        nfull = (i * B) // bk
        carry = step(nfull, (jnp.zeros((B, 1), F32), jnp.zeros((B, HEAD_DIM), F32)), True)
        _, acc = lax.fori_loop(0, nfull, lambda jj, cr: step(nfull - 1 - jj, cr, False), carry)
        o_ref[...] = acc.astype(o_ref.dtype)

    return _call_with_carry(
        kern, name=name, grid=(H, nq), carry=carry,
        in_specs=[pl.BlockSpec((B, HEAD_DIM), lambda h, i: (i, 12 + h)),
                  pl.BlockSpec((T, HEAD_DIM), lambda h, i: (0, 16 + h)),
                  pl.BlockSpec((T, HEAD_DIM), lambda h, i: (0, 20 + h))],
        out_specs=[pl.BlockSpec((B, HEAD_DIM), lambda h, i: (i, h))],
        out_shape=[jax.ShapeDtypeStruct((T, BRANCH_WIDTH), BF16)],
        scratch_shapes=[], args=(u_att, u_att, u_att))


def _sb_bwd(u_att, do, *, name, carry=None):
    T = u_att.shape[0]
    B, bk = _att_blocks(T)
    nq, nsub = T // B, bk // B
    H = N_HEADS

    def kern(q_ref, k_ref, v_ref, do_ref, dq_ref, dk_ref, dv_ref, dk_acc, dv_acc, de_s, sg_s):
        i = pl.program_id(1)

        @pl.when(i == 0)
        def _():
            dk_acc[...] = jnp.zeros_like(dk_acc)
            dv_acc[...] = jnp.zeros_like(dv_acc)

        q = q_ref[...]
        dov = do_ref[...]
        suffix, prefix = _sb_tri(B)

        def sweep1(j, run, masked):
            parts = []
            for s in reversed(range(nsub)):
                jb = j * nsub + s
                off = pl.multiple_of(jb * B, B)
                k = k_ref[pl.ds(off, B), :]
                z = _dot(q, k, NT) * SCALE
                t, sp = _softplus_parts(z)
                lg = -sp
                sg = jnp.exp(z + lg)
                valid = None
                if masked:
                    r, c = _positions(i, jb, B, B)
                    valid = c < r
                    lg = jnp.where(valid, lg, 0.0)
                    sg = jnp.where(valid, sg, 0.0)
                sg_s[jb] = sg.astype(sg_s.dtype)
                hi, lo = _split2(lg)
                cum = _dot(hi, suffix) + _dot(lo, suffix)
                da = _dot(dov, v_ref[pl.ds(off, B), :], NT)
                parts.append((jb, off, z, cum, da, jnp.sum(lg, axis=1, keepdims=True), valid))
            for jb, off, z, cum, da, rs, valid in parts:
                a = jnp.exp(z + cum + run)
                if masked:
                    a = jnp.where(valid, a, 0.0)
                de_s[jb] = a * da
                dv_acc[pl.ds(off, B), :] += _dot(a.astype(BF16), dov, TN)
                run = run + rs
            return run

        nfull = (i * B) // bk
        run = sweep1(nfull, jnp.zeros((B, 1), F32), True)
        lax.fori_loop(0, nfull, lambda jj, cr: sweep1(nfull - 1 - jj, cr, False), run)

        def sweep2(j, carry):
            pre, dq = carry
            parts = []
            for s in range(nsub):
                jb = j * nsub + s
                de = de_s[jb]
                hi, lo = _split2(de)
                parts.append((jb, de, _dot(hi, prefix) + _dot(lo, prefix), jnp.sum(de, axis=1, keepdims=True)))
            for jb, de, g, rs in parts:
                off = pl.multiple_of(jb * B, B)
                dz = (de - sg_s[jb].astype(F32) * (g + pre)).astype(BF16)
                dq = dq + _dot(dz, k_ref[pl.ds(off, B), :])
                dk_acc[pl.ds(off, B), :] += _dot(dz, q, TN)
                pre = pre + rs
            return pre, dq

        _, dq = lax.fori_loop(0, nfull + 1, sweep2, (jnp.zeros((B, 1), F32), jnp.zeros((B, HEAD_DIM), F32)))
        dq_ref[...] = (dq * SCALE).astype(dq_ref.dtype)

        @pl.when(i == nq - 1)
        def _():
            dk_ref[...] = (dk_acc[...] * SCALE).astype(dk_ref.dtype)
            dv_ref[...] = dv_acc[...].astype(dv_ref.dtype)

    return _call_with_carry(
        kern, name=name, grid=(H, nq), carry=carry,
        in_specs=[pl.BlockSpec((B, HEAD_DIM), lambda h, i: (i, 12 + h)),
                  pl.BlockSpec((T, HEAD_DIM), lambda h, i: (0, 16 + h)),
                  pl.BlockSpec((T, HEAD_DIM), lambda h, i: (0, 20 + h)),
                  pl.BlockSpec((B, HEAD_DIM), lambda h, i: (i, h))],
        out_specs=[pl.BlockSpec((B, HEAD_DIM), lambda h, i: (i, h)),
                   pl.BlockSpec((T, HEAD_DIM), lambda h, i: (0, h)),
                   pl.BlockSpec((T, HEAD_DIM), lambda h, i: (0, h))],
        out_shape=[jax.ShapeDtypeStruct((T, BRANCH_WIDTH), BF16)] * 3,
        scratch_shapes=[pltpu.VMEM((T, HEAD_DIM), F32), pltpu.VMEM((T, HEAD_DIM), F32),
                        pltpu.VMEM((T // B, B, B), F32), pltpu.VMEM((T // B, B, B), BF16)],
        args=(u_att, u_att, u_att, do))


def _rel_onehot(qrow):
    k = lax.broadcasted_iota(jnp.int32, (BAND, REL_PAD), 0)
    rr = lax.broadcasted_iota(jnp.int32, (BAND, REL_PAD), 1)
    idx = jnp.clip(PAD_ROWS + qrow - k, -(CHUNK - 1), REL_CLIP) + (CHUNK - 1)
    return jnp.where(idx == rr, 1.0, 0.0).astype(BF16)


def _band_bias(table, *, name):
    def kern(t_ref, o_ref):
        hi, mid, lo = _split3(t_ref[...])

        def body(qrow, _):
            oh = _rel_onehot(qrow)
            o_ref[qrow] = _dot(hi, oh, NT) + _dot(mid, oh, NT) + _dot(lo, oh, NT)
            return 0

        lax.fori_loop(0, CHUNK, body, 0)

    return pl.pallas_call(
        kern, name=name, out_shape=jax.ShapeDtypeStruct((CHUNK, 8, BAND), F32),
        compiler_params=_cp(),
    )(table)


def _band_bias_bwd(dbias, *, name):
    def kern(d_ref, o_ref):
        def body(qrow, acc):
            oh = _rel_onehot(qrow)
            hi, mid, lo = _split3(d_ref[qrow])
            return acc + _dot(hi, oh) + _dot(mid, oh) + _dot(lo, oh)

        o_ref[...] = lax.fori_loop(0, CHUNK, body, jnp.zeros((8, REL_PAD), F32))

    return pl.pallas_call(
        kern, name=name, out_shape=jax.ShapeDtypeStruct((8, REL_PAD), F32),
        compiler_params=_cp(),
    )(dbias)


def _chunk_rows(T):
    return _pick(T, (512, 256, 128, 64))


def _chunk_scores(q, kw, bias, c_global):
    s = _dot(q, kw, NT) * SCALE + bias
    col = lax.broadcasted_iota(jnp.int32, (CHUNK, BAND), 1)
    valid = (c_global * CHUNK + col) >= PAD_ROWS
    s = jnp.where(valid, s, -jnp.inf)
    m = jnp.max(s, axis=1, keepdims=True)
    e = jnp.exp(s - m)
    return e / jnp.sum(e, axis=1, keepdims=True)


def _chunk_fwd(u_att, bias, *, name):
    T = u_att.shape[0]
    R = _chunk_rows(T)
    nr = T // R
    H = N_HEADS

    def kern(q_ref, k_ref, v_ref, b_ref, o_ref, kpad, vpad):
        i = pl.program_id(1)

        @pl.when(i == 0)
        def _():
            kpad[0:PAD_ROWS, :] = jnp.zeros((PAD_ROWS, HEAD_DIM), BF16)
            vpad[0:PAD_ROWS, :] = jnp.zeros((PAD_ROWS, HEAD_DIM), BF16)
            kpad[PAD_ROWS:, :] = k_ref[...]
            vpad[PAD_ROWS:, :] = v_ref[...]

        bias_v = b_ref[...]
        for cc in range(R // CHUNK):
            cg = i * (R // CHUNK) + cc
            off = pl.multiple_of(cg * CHUNK, CHUNK)
            q = q_ref[cc * CHUNK:(cc + 1) * CHUNK, :]
            kw = kpad[pl.ds(off, BAND), :]
            vw = vpad[pl.ds(off, BAND), :]
            p = _chunk_scores(q, kw, bias_v, cg)
            o_ref[cc * CHUNK:(cc + 1) * CHUNK, :] = _dot(p.astype(BF16), vw).astype(o_ref.dtype)

    return pl.pallas_call(
        kern, name=name, grid=(H, nr),
        in_specs=[pl.BlockSpec((R, HEAD_DIM), lambda h, i: (i, 24 + h)),
                  pl.BlockSpec((T, HEAD_DIM), lambda h, i: (0, 28 + h)),
                  pl.BlockSpec((T, HEAD_DIM), lambda h, i: (0, 32 + h)),
                  pl.BlockSpec((None, CHUNK, BAND), lambda h, i: (h, 0, 0))],
        out_specs=pl.BlockSpec((R, HEAD_DIM), lambda h, i: (i, h)),
        out_shape=jax.ShapeDtypeStruct((T, BRANCH_WIDTH), BF16),
        scratch_shapes=[pltpu.VMEM((T + PAD_ROWS, HEAD_DIM), BF16), pltpu.VMEM((T + PAD_ROWS, HEAD_DIM), BF16)],
        compiler_params=_cp(("parallel", "arbitrary")),
    )(u_att, u_att, u_att, bias)


def _chunk_bwd(u_att, bias, do, *, name):
    T = u_att.shape[0]
    R = _chunk_rows(T)
    nr = T // R
    H = N_HEADS

    def kern(q_ref, k_ref, v_ref, b_ref, do_ref, dq_ref, dk_ref, dv_ref, db_ref, kpad, vpad, dkp, dvp):
        i = pl.program_id(1)

        @pl.when(i == 0)
        def _():
            kpad[0:PAD_ROWS, :] = jnp.zeros((PAD_ROWS, HEAD_DIM), BF16)
            vpad[0:PAD_ROWS, :] = jnp.zeros((PAD_ROWS, HEAD_DIM), BF16)
            kpad[PAD_ROWS:, :] = k_ref[...]
            vpad[PAD_ROWS:, :] = v_ref[...]
            dkp[...] = jnp.zeros_like(dkp)
            dvp[...] = jnp.zeros_like(dvp)
            db_ref[...] = jnp.zeros_like(db_ref)

        bias_v = b_ref[...]
        for cc in range(R // CHUNK):
            cg = i * (R // CHUNK) + cc
            off = pl.multiple_of(cg * CHUNK, CHUNK)
            q = q_ref[cc * CHUNK:(cc + 1) * CHUNK, :]
            dov = do_ref[cc * CHUNK:(cc + 1) * CHUNK, :]
            kw = kpad[pl.ds(off, BAND), :]
            vw = vpad[pl.ds(off, BAND), :]
            p = _chunk_scores(q, kw, bias_v, cg)
            dp = _dot(dov, vw, NT)
            ds = p * (dp - jnp.sum(p * dp, axis=1, keepdims=True))
            dsb = ds.astype(BF16)
            dq_ref[cc * CHUNK:(cc + 1) * CHUNK, :] = (_dot(dsb, kw) * SCALE).astype(dq_ref.dtype)
            dkp[pl.ds(off, BAND), :] += _dot(dsb, q, TN)
            dvp[pl.ds(off, BAND), :] += _dot(p.astype(BF16), dov, TN)
            db_ref[...] += ds

        @pl.when(i == nr - 1)
        def _():
            dk_ref[...] = (dkp[PAD_ROWS:, :] * SCALE).astype(dk_ref.dtype)
            dv_ref[...] = dvp[PAD_ROWS:, :].astype(dv_ref.dtype)

    return pl.pallas_call(
        kern, name=name, grid=(H, nr),
        in_specs=[pl.BlockSpec((R, HEAD_DIM), lambda h, i: (i, 24 + h)),
                  pl.BlockSpec((T, HEAD_DIM), lambda h, i: (0, 28 + h)),
                  pl.BlockSpec((T, HEAD_DIM), lambda h, i: (0, 32 + h)),
                  pl.BlockSpec((None, CHUNK, BAND), lambda h, i: (h, 0, 0)),
                  pl.BlockSpec((R, HEAD_DIM), lambda h, i: (i, h))],
        out_specs=[pl.BlockSpec((R, HEAD_DIM), lambda h, i: (i, h)),
                   pl.BlockSpec((T, HEAD_DIM), lambda h, i: (0, h)),
                   pl.BlockSpec((T, HEAD_DIM), lambda h, i: (0, h)),
                   pl.BlockSpec((None, CHUNK, BAND), lambda h, i: (h, 0, 0))],
        out_shape=[jax.ShapeDtypeStruct((T, BRANCH_WIDTH), BF16)] * 3
                  + [jax.ShapeDtypeStruct((H, CHUNK, BAND), F32)],
        scratch_shapes=[pltpu.VMEM((T + PAD_ROWS, HEAD_DIM), BF16), pltpu.VMEM((T + PAD_ROWS, HEAD_DIM), BF16),
                        pltpu.VMEM((T + PAD_ROWS, HEAD_DIM), F32), pltpu.VMEM((T + PAD_ROWS, HEAD_DIM), F32)],
        compiler_params=_cp(("parallel", "arbitrary")),
    )(u_att, u_att, u_att, bias, do)


LRU_ROWS = 256
HALO = 8


def _gelu(y):
    k0 = math.sqrt(2.0 / math.pi)
    t = jnp.tanh(k0 * (y + 0.044715 * y * y * y))
    return 0.5 * y * (1.0 + t), t


def _gelu_grad(y, t):
    k0 = math.sqrt(2.0 / math.pi)
    return 0.5 * (1.0 + t) + 0.5 * y * (1.0 - t * t) * k0 * (1.0 + 3.0 * 0.044715 * y * y)


def _neg_expm1(y):
    poly = -y * (1.0 + y * (1.0 / 2 + y * (1.0 / 6 + y * (1.0 / 24 + y * (1.0 / 120 + y * (1.0 / 720 + y * (1.0 / 5040)))))))
    return jnp.where(y > -0.5, poly, 1.0 - jnp.exp(y))


def _lru_gates(ext, cw_ref, cb_ref, wr_ref, br_ref, wi_ref, bi_ref, lam_ref, rows):
    xc = cb_ref[...] + jnp.zeros((rows, BRANCH_WIDTH), F32)
    for j in range(CONV_WIDTH):
        xc = xc + ext[pl.ds(HALO - (CONV_WIDTH - 1) + j, rows), :] * cw_ref[j:j + 1, :]
    xcb = xc.astype(BF16)
    zr = jnp.concatenate([_dot(xcb[:, n * 128:(n + 1) * 128], wr_ref[n]) for n in range(4)], axis=1) + br_ref[...]
    zi = jnp.concatenate([_dot(xcb[:, n * 128:(n + 1) * 128], wi_ref[n]) for n in range(4)], axis=1) + bi_ref[...]
    r = _sigmoid(zr)
    gi = _sigmoid(zi)
    ls = _log_sigmoid(lam_ref[...])
    la = LRU_C * r * ls
    a = jnp.exp(la)
    mult = jnp.sqrt(_neg_expm1(2.0 * la))
    return xc, xcb, r, gi, ls, a, mult


def _lru_param_specs():
    full2 = lambda s: pl.BlockSpec(s, lambda i: (0, 0))
    full3 = lambda s: pl.BlockSpec(s, lambda i: (0, 0, 0))
    return [full2((8, BRANCH_WIDTH)), full2((1, BRANCH_WIDTH)), full3((4, 128, 128)), full2((1, BRANCH_WIDTH)),
            full3((4, 128, 128)), full2((1, BRANCH_WIDTH)), full2((1, BRANCH_WIDTH))]


def _lru_fwd(u_rec, p, *, name):
    T = u_rec.shape[0]
    R = min(LRU_ROWS, T)
    nb = T // R
    W = BRANCH_WIDTH
    hb = R // HALO

    def kern(rx_ref, halo_ref, ry_ref, cw_ref, cb_ref, wr_ref, br_ref, wi_ref, bi_ref, lam_ref,
             o_ref, h_ref, ext, a_s, b_s, hc):
        i = pl.program_id(0)

        @pl.when(i == 0)
        def _():
            hc[...] = jnp.zeros_like(hc)

        ext[0:HALO, :] = jnp.where(i == 0, 0.0, halo_ref[...])
        ext[HALO:, :] = rx_ref[...]
        xc, _, r, gi, ls, a, mult = _lru_gates(ext, cw_ref, cb_ref, wr_ref, br_ref, wi_ref, bi_ref, lam_ref, R)
        a_s[...] = a
        b_s[...] = mult * (gi * xc)

        def body(t, h):
            h = a_s[pl.ds(t, 1), :] * h + b_s[pl.ds(t, 1), :]
            h_ref[pl.ds(t, 1), :] = h
            return h

        h = lax.fori_loop(0, R, body, hc[0:1, :], unroll=8)
        hc[...] = jnp.broadcast_to(h, hc.shape)
        g, _ = _gelu(ry_ref[...])
        o_ref[...] = (h_ref[...] * g).astype(o_ref.dtype)

    return pl.pallas_call(
        kern, name=name, grid=(nb,),
        in_specs=[pl.BlockSpec((R, W), lambda i: (i, 0)),
                  pl.BlockSpec((HALO, W), lambda i: (jnp.maximum(i * hb - 1, 0), 0)),
                  pl.BlockSpec((R, W), lambda i: (i, 1))] + _lru_param_specs(),
        out_specs=[pl.BlockSpec((R, W), lambda i: (i, 0)), pl.BlockSpec((R, W), lambda i: (i, 0))],
        out_shape=[jax.ShapeDtypeStruct((T, W), BF16), jax.ShapeDtypeStruct((T, W), F32)],
        scratch_shapes=[pltpu.VMEM((R + HALO, W), F32), pltpu.VMEM((R, W), F32), pltpu.VMEM((R, W), F32),
                        pltpu.VMEM((8, W), F32)],
        compiler_params=_cp(("arbitrary",)),
    )(u_rec, u_rec, u_rec, *p)


def _lru_bwd(u_rec, hs, do, p, *, name):
    T = u_rec.shape[0]
    R = min(LRU_ROWS, T)
    nb = T // R
    W = BRANCH_WIDTH
    hb = R // HALO

    def kern(rx_ref, halo_ref, ry_ref, h_ref, hh_ref, do_ref, cw_ref, cb_ref, wr_ref, br_ref, wi_ref, bi_ref, lam_ref,
             drx_ref, dry_ref, dcw_ref, dcb_ref, dwr_ref, dbr_ref, dwi_ref, dbi_ref, dlam_ref,
             ext, hext, a_s, g_s, dext, gc):
        s = pl.program_id(0)
        first_block = s == nb - 1

        @pl.when(s == 0)
        def _():
            gc[...] = jnp.zeros_like(gc)
            dext[R:, :] = jnp.zeros((HALO, W), F32)
            for ref in (dcw_ref, dcb_ref, dwr_ref, dbr_ref, dwi_ref, dbi_ref, dlam_ref):
                ref[...] = jnp.zeros_like(ref)

        ext[0:HALO, :] = jnp.where(first_block, 0.0, halo_ref[...])
        ext[HALO:, :] = rx_ref[...]
        hext[0:HALO, :] = jnp.where(first_block, 0.0, hh_ref[...])
        hext[HALO:, :] = h_ref[...]
        xc, xcb, r, gi, ls, a, mult = _lru_gates(ext, cw_ref, cb_ref, wr_ref, br_ref, wi_ref, bi_ref, lam_ref, R)
        ry = ry_ref[...]
        gel, th = _gelu(ry)
        dov = do_ref[...].astype(F32)
        dry_ref[...] = (dov * h_ref[...] * _gelu_grad(ry, th)).astype(dry_ref.dtype)
        a_s[...] = a
        g_s[...] = dov * gel

        def body(tt, g):
            t = R - 1 - tt
            dh = g_s[pl.ds(t, 1), :] + g
            g_s[pl.ds(t, 1), :] = dh
            return a_s[pl.ds(t, 1), :] * dh

        g = lax.fori_loop(0, R, body, gc[0:1, :], unroll=8)
        gc[...] = jnp.broadcast_to(g, gc.shape)
        dh = g_s[...]
        hprev = hext[pl.ds(HALO - 1, R), :]
        da = dh * hprev
        gx = gi * xc
        dmult = dh * gx
        dgx = dh * mult
        dgi = dgx * xc
        dxc = dgx * gi
        dla = da * a - dmult * (a * a) / mult
        dr = dla * (LRU_C * ls)
        dlam_ref[...] += jnp.sum(dla * (LRU_C * r), axis=0, keepdims=True)
        dzr = dr * r * (1.0 - r)
        dzi = dgi * gi * (1.0 - gi)
        dbr_ref[...] += jnp.sum(dzr, axis=0, keepdims=True)
        dbi_ref[...] += jnp.sum(dzi, axis=0, keepdims=True)
        dzrb = dzr.astype(BF16)
        dzib = dzi.astype(BF16)
        back = []
        for n in range(4):
            sl = slice(n * 128, (n + 1) * 128)
            dwr_ref[n] += _dot(xcb[:, sl], dzrb[:, sl], TN)
            dwi_ref[n] += _dot(xcb[:, sl], dzib[:, sl], TN)
            back.append(_dot(dzrb[:, sl], wr_ref[n], NT) + _dot(dzib[:, sl], wi_ref[n], NT))
        dxc = dxc + jnp.concatenate(back, axis=1)
        dcb_ref[...] += jnp.sum(dxc, axis=0, keepdims=True)
        for j in range(CONV_WIDTH):
            dcw_ref[j:j + 1, :] += jnp.sum(dxc * ext[pl.ds(HALO - (CONV_WIDTH - 1) + j, R), :], axis=0, keepdims=True)
        dext[0:R, :] = dxc
        drx = jnp.zeros((R, W), F32)
        for j in range(CONV_WIDTH):
            drx = drx + dext[pl.ds(CONV_WIDTH - 1 - j, R), :] * cw_ref[j:j + 1, :]
        drx_ref[...] = drx.astype(drx_ref.dtype)
        dext[R:, :] = dxc[0:HALO, :]

        @pl.when(s == nb - 1)
        def _():
            dlam_ref[...] = dlam_ref[...] * _sigmoid(-lam_ref[...])

    rev = lambda c: pl.BlockSpec((R, W), lambda s: (nb - 1 - s, c))
    halo = lambda: pl.BlockSpec((HALO, W), lambda s: (jnp.maximum((nb - 1 - s) * hb - 1, 0), 0))
    v2 = lambda shp: pl.BlockSpec(shp, lambda s: (0, 0))
    v3 = lambda shp: pl.BlockSpec(shp, lambda s: (0, 0, 0))
    return pl.pallas_call(
        kern, name=name, grid=(nb,),
        in_specs=[rev(0), halo(), rev(1), rev(0), halo(), rev(0)] + _lru_param_specs(),
        out_specs=[rev(0), rev(0), v2((8, W)), v2((1, W)), v3((4, 128, 128)), v2((1, W)), v3((4, 128, 128)),
                   v2((1, W)), v2((1, W))],
        out_shape=[jax.ShapeDtypeStruct((T, W), BF16), jax.ShapeDtypeStruct((T, W), BF16),
                   jax.ShapeDtypeStruct((8, W), F32), jax.ShapeDtypeStruct((1, W), F32),
                   jax.ShapeDtypeStruct((4, 128, 128), F32), jax.ShapeDtypeStruct((1, W), F32),
                   jax.ShapeDtypeStruct((4, 128, 128), F32), jax.ShapeDtypeStruct((1, W), F32),
                   jax.ShapeDtypeStruct((1, W), F32)],
        scratch_shapes=[pltpu.VMEM((R + HALO, W), F32), pltpu.VMEM((R + HALO, W), F32), pltpu.VMEM((R, W), F32),
                        pltpu.VMEM((R, W), F32), pltpu.VMEM((R + HALO, W), F32), pltpu.VMEM((8, W), F32)],
        compiler_params=_cp(("arbitrary",)),
    )(u_rec, u_rec, u_rec, hs, hs, do, *p)


def _merge_fwd(o_all, wb, gate, *, name):
    T = o_all.shape[1]
    D = D_MODEL
    bm = _pick(T, (1024, 512, 256, 128))
    bn = 1024
    nj = D // bn

    def kern(o_ref, w_ref, g_ref, m_ref, pb_ref, acc):
        g = pl.program_id(2)
        pbv = _dot(o_ref[...], w_ref[...])
        pb_ref[...] = pbv.astype(pb_ref.dtype)
        term = g_ref[...].astype(F32) * pbv

        @pl.when(g == 0)
        def _():
            acc[...] = term

        @pl.when(g > 0)
        def _():
            acc[...] += term

        @pl.when(g == N_BRANCH - 1)
        def _():
            m_ref[...] = acc[...].astype(m_ref.dtype)

    return pl.pallas_call(
        kern, name=name, grid=(T // bm, nj, N_BRANCH),
        in_specs=[pl.BlockSpec((None, bm, BRANCH_WIDTH), lambda i, j, g: (g, i, 0)),
                  pl.BlockSpec((None, BRANCH_WIDTH, bn), lambda i, j, g: (g, 0, j)),
                  pl.BlockSpec((bm, bn), lambda i, j, g: (i, g * nj + j))],
        out_specs=[pl.BlockSpec((bm, bn), lambda i, j, g: (i, j)),
                   pl.BlockSpec((bm, bn), lambda i, j, g: (i, g * nj + j))],
        out_shape=[jax.ShapeDtypeStruct((T, D), BF16), jax.ShapeDtypeStruct((T, N_BRANCH * D), BF16)],
        scratch_shapes=[pltpu.VMEM((bm, bn), F32)],
        compiler_params=_cp(("parallel", "parallel", "arbitrary")),
    )(o_all, wb, gate)


def _merge_bwd(dm, gate, pb, *, name):
    T = dm.shape[0]
    D = D_MODEL
    bt = _pick(T, (256, 128))

    def kern(dm_ref, g_ref, pb_ref, dpb_ref, dzg_ref, dbg_ref):
        i = pl.program_id(1)
        dmv = dm_ref[...]
        gv = g_ref[...].astype(F32)
        dpb_ref[...] = (dmv * gv).astype(dpb_ref.dtype)
        dzg = dmv * pb_ref[...].astype(F32) * gv * (1.0 - gv)
        dzg_ref[...] = dzg.astype(dzg_ref.dtype)
        part = jnp.sum(dzg, axis=0, keepdims=True)

        @pl.when(i == 0)
        def _():
            dbg_ref[...] = part

        @pl.when(i > 0)
        def _():
            dbg_ref[...] += part

    return pl.pallas_call(
        kern, name=name, grid=(N_BRANCH, T // bt),
        in_specs=[pl.BlockSpec((bt, D), lambda g, i: (i, 0)),
                  pl.BlockSpec((bt, D), lambda g, i: (i, g)),
                  pl.BlockSpec((bt, D), lambda g, i: (i, g))],
        out_specs=[pl.BlockSpec((None, bt, D), lambda g, i: (g, i, 0)),
                   pl.BlockSpec((bt, D), lambda g, i: (i, g)),
                   pl.BlockSpec((1, D), lambda g, i: (0, g))],
        out_shape=[jax.ShapeDtypeStruct((N_BRANCH, T, D), BF16), jax.ShapeDtypeStruct((T, N_BRANCH * D), BF16),
                   jax.ShapeDtypeStruct((1, N_BRANCH * D), F32)],
        compiler_params=_cp(("parallel", "arbitrary")),
    )(dm, gate, pb)


def _col_split(M, N, bm, bn):
    per = N // N_CHIPS // bn
    return (N_CHIPS, M, N // N_CHIPS), (None, bm, bn), lambda i, j: (j // per, i, j % per)


def _pad_lanes(v, n):
    return jnp.pad(v, [(0, 0)] * (v.ndim - 1) + [(0, n - v.shape[-1])])


def _rows8(v):
    return jnp.pad(v, ((0, 8 - v.shape[0]), (0, 0)))


def _device_step(x, tgt, W, hooks=None):
    hooks = hooks or {}

    def carried(fn, key, *args, **kw):
        hook = hooks.get(key)
        outs, extra = fn(*args, name=key[0], carry=hook.spec(W, G) if hook else None, **kw)
        if hook:
            hook.done(extra, W, G)
        return outs

    T = x.shape[0]
    _, bk = _att_blocks(T)
    H = N_HEADS
    G = {}
    saved = []

    xf, xb = _ln_fwd(x, W['ln_in_g'], W['ln_in_b'], name='ln_in_fwd')
    for l in range(DEPTH):
        w_att, w_rec = W['w_att'][l], W['w_rec'][l]
        u_att = _mm(xb, w_att, name='in_proj_att', out_dtypes=(BF16,))
        u_rec = _mm(xb, w_rec, name='in_proj_rec', out_dtypes=(F32,))
        ffl = u_rec[:, 2 * BRANCH_WIDTH:]
        bf = _pad_lanes(W['b_forget'][l].reshape(1, H), LANES)
        Fc = _forget_fwd(ffl, bf, name='forget_fwd')
        Fh = Fc[:, :H].T
        frow = Fh.reshape(H, T // bk, 1, bk)
        o_fox, lse = carried(_fox_fwd, ('fox_fwd', l), u_att, Fc, frow)
        lp = (_rows8(W['conv_w'][l]), W['conv_b'][l].reshape(1, -1), W['w_r'][l].astype(BF16),
              W['b_r'][l].reshape(1, -1), W['w_i'][l].astype(BF16), W['b_i'][l].reshape(1, -1),
              W['lru_lambda'][l].reshape(1, -1))
        o_lru, hs = _lru_fwd(u_rec, lp, name='lru_fwd')
        o_sb, = carried(_sb_fwd, ('sb_fwd', l), u_att)
        table = _rows8(_pad_lanes(W['rel_bias'][l], REL_PAD))
        bias = _band_bias(table, name='band_bias').transpose(1, 0, 2)[:H]
        o_ch = _chunk_fwd(u_att, bias, name='chunk_fwd')
        o_all = jnp.stack([o_fox, o_lru, o_sb, o_ch])
        gate = _mm(xb, W['w_gate_cat'][l], name='gate_proj', out_dtypes=(BF16,),
                   extras=[(W['b_gate'][l].reshape(1, -1), 'n')],
                   epilogue=lambda acc, b: (_sigmoid(acc + b),))
        merged, pb = _merge_fwd(o_all, W['w_branch'][l], gate, name='merge_fwd')
        h1 = _mm(merged, W['w_out'][l], name='out_proj', extras=[(xf, 'mn')],
                 epilogue=lambda acc, xr: (ALPHA * xr + acc,))
        xmf, xmb = _ln_fwd(h1, W['ln1_g'][l], W['ln1_b'][l], name='ln_fwd')
        hid, ra = _mm(xmb, W['w_ff1'][l], name='ff1', out_dtypes=(BF16, BF16),
                      epilogue=lambda acc: (jnp.square(jnp.maximum(acc, 0.0)), jnp.maximum(acc, 0.0)))
        h2 = _mm(hid, W['w_ff2'][l], name='ff2', extras=[(xmf, 'mn')],
                 epilogue=lambda acc, xr: (ALPHA * xr + acc,))
        saved.append(dict(xb=xb, u_att=u_att, u_rec=u_rec, ffl=ffl, bf=bf, fcum=Fc, frow=frow, lse=lse, lp=lp,
                          hs=hs, bias=bias, o_all=o_all, gate=gate, merged=merged, pb=pb, h1=h1, xmb=xmb,
                          hid=hid, ra=ra, h2=h2))
        xf, xb = _ln_fwd(h2, W['ln2_g'][l], W['ln2_b'][l], name='ln_fwd')

    dx, loss_tile = _loss_head(xf, tgt, name='loss_head')
    loss = loss_tile[0, 0]

    for l in reversed(range(DEPTH)):
        S = saved[l]
        dh2, dh2b, G[('ln2_g', l)], G[('ln2_b', l)] = _ln_bwd(S['h2'], dx, W['ln2_g'][l], name='ln_bwd')
        da = _mm(dh2b, W['w_ff2'][l], tb=True, name='ff2_dx', out_dtypes=(BF16,), extras=[(S['ra'], 'mn')],
                 epilogue=lambda acc, rav: (acc * (2.0 * rav.astype(F32)),))
        G[('w_ff2', l)] = _mm(S['hid'], dh2b, ta=True, name='ff2_dw').reshape(N_CHIPS, D_FF // N_CHIPS, D_MODEL)
        G[('w_ff1', l)] = _mm(S['xmb'], da, ta=True, name='ff1_dw', bm=1024, bn=1024,
                              out_map=_col_split(D_MODEL, D_FF, 1024, 1024))
        dxm = _mm(da, W['w_ff1'][l], tb=True, name='ff1_dx', extras=[(dh2, 'mn')],
                  epilogue=lambda acc, d: (ALPHA * d + acc,))
        dh1, dh1b, G[('ln1_g', l)], G[('ln1_b', l)] = _ln_bwd(S['h1'], dxm, W['ln1_g'][l], name='ln_bwd')
        dm = _mm(dh1b, W['w_out'][l], tb=True, name='out_dx')
        G[('w_out', l)] = _mm(S['merged'], dh1b, ta=True, name='out_dw').reshape(
            N_CHIPS, D_MODEL // N_CHIPS, D_MODEL)
        dpb, dzg, G[('b_gate', l)] = _merge_bwd(dm, S['gate'], S['pb'], name='merge_bwd')
        do = [_mm(dpb[g], W['w_branch'][l][g], tb=True, name='branch_dx', out_dtypes=(BF16,)) for g in range(N_BRANCH)]
        G[('w_branch', l)] = jnp.stack(
            [_mm(S['o_all'][g], dpb[g], ta=True, name='branch_dw', bm=BRANCH_WIDTH, bn=BRANCH_WIDTH,
                 out_map=_col_split(BRANCH_WIDTH, D_MODEL, BRANCH_WIDTH, BRANCH_WIDTH))
             for g in range(N_BRANCH)], axis=1)
        G[('w_gate', l)] = _mm(S['xb'], dzg, ta=True, name='gate_dw', bm=1024, bn=1024,
                               out_map=((N_CHIPS, N_BRANCH, D_MODEL // N_CHIPS, D_MODEL),
                                        (2, None, D_MODEL // N_CHIPS, 1024),
                                        lambda i, j: (i, j // 2, 0, j % 2)))
        u_att, u_rec = S['u_att'], S['u_rec']
        delta = _row_dot(do[0], S['o_all'][0], name='row_dot')
        fdq, fdk, fdv, dfk, dfq = carried(_fox_bwd, ('fox_bwd', l), u_att, do[0], S['lse'], delta, S['fcum'],
                                          S['frow'])
        dff, dbf = _forget_bwd(_pad_lanes(dfk.reshape(H, T).T, LANES), _pad_lanes(dfq.reshape(H, T).T, LANES),
                               S['ffl'], S['bf'], name='forget_bwd')
        G[('b_forget', l)] = dbf[0, :H]
        (drx, dry, dcw, dcb, G[('w_r', l)], dbr, G[('w_i', l)], dbi, dlam) = _lru_bwd(
            u_rec, S['hs'], do[1], S['lp'], name='lru_bwd')
        G[('conv_w', l)], G[('conv_b', l)] = dcw[:CONV_WIDTH], dcb[0]
        G[('b_r', l)], G[('b_i', l)], G[('lru_lambda', l)] = dbr[0], dbi[0], dlam[0]
        sdq, sdk, sdv = carried(_sb_bwd, ('sb_bwd', l), u_att, do[2])
        cdq, cdk, cdv, dbias = _chunk_bwd(u_att, S['bias'], do[3], name='chunk_bwd')
        dtab = _band_bias_bwd(jnp.pad(dbias, ((0, 8 - H), (0, 0), (0, 0))).transpose(1, 0, 2), name='band_bias_bwd')
        G[('rel_bias', l)] = dtab[:H, :REL_TABLE]
        du_att = jnp.concatenate([fdq, fdk, fdv, sdq, sdk, sdv, cdq, cdk, cdv], axis=1)
        du_rec = jnp.concatenate([drx, dry, dff], axis=1)
        G[('w_att', l)] = _mm(S['xb'], du_att, ta=True, name='in_att_dw')
        G[('w_rec', l)] = _mm(S['xb'], du_rec, ta=True, name='in_rec_dw')
        t1 = _mm(dzg, W['w_gate_cat'][l], tb=True, name='gate_dx', extras=[(dh1, 'mn')],
                 epilogue=lambda acc, d: (ALPHA * d + acc,))
        t2 = _mm(du_att, W['w_att'][l], tb=True, name='in_att_dx', extras=[(t1, 'mn')],
                 epilogue=lambda acc, d: (d + acc,))
        dx = _mm(du_rec, W['w_rec'][l], tb=True, name='in_rec_dx', extras=[(t2, 'mn')],
                 epilogue=lambda acc, d: (d + acc,))

    gx, _, G[('ln_in_g', -1)], G[('ln_in_b', -1)] = _ln_bwd(x, dx, W['ln_in_g'], name='ln_in_bwd')
    return loss, gx, G


_IN_FQKV = (0, 1536)
_IN_FF = (1536, 1540)
_IN_REC = (1540, 2564)
_IN_REST = (2564, D_IN)


def _prep_weights(full, W=None):
    W = {} if W is None else W
    for n, a in full.items():
        if n == 'w_in':
            L = a.shape[0]
            W['w_att'] = jnp.concatenate([a[..., _IN_FQKV[0]:_IN_FQKV[1]], a[..., _IN_REST[0]:_IN_REST[1]]],
                                         -1).astype(BF16)
            W['w_rec'] = jnp.concatenate([a[..., _IN_REC[0]:_IN_REC[1]], a[..., _IN_FF[0]:_IN_FF[1]],
                                          jnp.zeros((L, D_MODEL, N_REC - 1024 - N_HEADS), a.dtype)], -1).astype(BF16)
        elif n == 'w_gate':
            W['w_gate_cat'] = a.transpose(0, 2, 1, 3).reshape(a.shape[0], D_MODEL, N_BRANCH * D_MODEL).astype(BF16)
        elif n == 'b_gate':
            W['b_gate'] = a.reshape(a.shape[0], N_BRANCH * D_MODEL)
        elif n in ('w_branch', 'w_out', 'w_ff1', 'w_ff2'):
            W[n] = a.astype(BF16)
        else:
            W[n] = a
    return W


def _grads_to_reference_layout(G):
    out = {'ln_in_g': G[('ln_in_g', -1)][0], 'ln_in_b': G[('ln_in_b', -1)][0]}
    st = lambda n: jnp.stack([G[(n, l)] for l in range(DEPTH)])
    g_att, g_rec = st('w_att'), st('w_rec')
    out['w_in'] = jnp.concatenate([g_att[..., :1536], g_rec[..., 1024:1024 + N_HEADS], g_rec[..., :1024],
                                   g_att[..., 1536:]], -1)
    out['w_gate'] = st('w_gate').transpose(0, 2, 1, 3, 4).reshape(DEPTH, N_BRANCH, D_MODEL, D_MODEL)
    out['w_branch'] = st('w_branch').transpose(0, 2, 3, 1, 4).reshape(DEPTH, N_BRANCH, BRANCH_WIDTH, D_MODEL)
    out['w_ff1'] = st('w_ff1').transpose(0, 2, 1, 3).reshape(DEPTH, D_MODEL, D_FF)
    out['w_ff2'] = st('w_ff2').reshape(DEPTH, D_FF, D_MODEL)
    out['w_out'] = st('w_out').reshape(DEPTH, D_MODEL, D_MODEL)
    out['b_gate'] = st('b_gate').reshape(DEPTH, N_BRANCH, D_MODEL)
    for n in ('ln1_g', 'ln1_b', 'ln2_g', 'ln2_b'):
        out[n] = st(n)[:, 0]
    for n in ('b_forget', 'conv_w', 'conv_b', 'w_r', 'b_r', 'w_i', 'b_i', 'lru_lambda', 'rel_bias'):
        out[n] = st(n)
    return out


HBM_SPEC = pl.BlockSpec(memory_space=pl.ANY)
N_CHIPS = 4
PACK_COLS = 1024


def _place():
    x, y, c = lax.axis_index("x"), lax.axis_index("y"), lax.axis_index("c")
    chips = [(1 - x, y), (x, 1 - y), (1 - x, 1 - y)]
    return x, y, c, chips


def _remote(src, dst, send_sems, recv_sems, k, to):
    return pltpu.make_async_remote_copy(src_ref=src, dst_ref=dst, send_sem=send_sems.at[k], recv_sem=recv_sems.at[k],
                                        device_id=to, device_id_type=MESH)


class _Exchange:
    def __init__(self, ins, out_shapes, n_sems, start, finish, mid=None):
        self.ins, self.out_shapes, self.n_sems = list(ins), list(out_shapes), n_sems
        self.start, self.mid, self.finish = start, mid, finish


def _gather_spec(params):
    n = len(params)

    def start(ins, outs, ss, rs):
        x, y, c, chips = _place()
        for p in range(n):
            _remote(ins[p], outs[p].at[2 * x + y], ss, rs, 6 * n + p, (x, y, 1 - c)).start()
            for j, (cx, cy) in enumerate(chips):
                _remote(ins[p].at[c], outs[p].at[2 * x + y, c], ss, rs, 6 * p + j, (cx, cy, c)).start()

    def mid(ins, outs, ss, rs):
        x, y, c, chips = _place()
        for p in range(n):
            for j, (cx, cy) in enumerate(chips):
                blk = outs[p].at[2 * cx + cy, c]
                _remote(blk, blk, ss, rs, 6 * p + j, (x, y, c)).wait_recv()
                _remote(blk, blk, ss, rs, 6 * p + 3 + j, (x, y, 1 - c)).start()

    def finish(ins, outs, ss, rs):
        x, y, c, chips = _place()
        me = (x, y, c)
        for p in range(n):
            for j, (cx, cy) in enumerate(chips):
                theirs = outs[p].at[2 * cx + cy, 1 - c]
                _remote(theirs, theirs, ss, rs, 6 * p + 3 + j, me).wait_recv()
        for p in range(n):
            for j, (cx, cy) in enumerate(chips):
                _remote(ins[p].at[c], outs[p].at[2 * x + y, c], ss, rs, 6 * p + j, me).wait_send()
                blk = outs[p].at[2 * cx + cy, c]
                _remote(blk, blk, ss, rs, 6 * p + 3 + j, me).wait_send()
            _remote(ins[p], outs[p].at[2 * x + y], ss, rs, 6 * n + p, me).wait()

    shapes = [jax.ShapeDtypeStruct((N_CHIPS,) + a.shape, a.dtype) for a in params]
    return _Exchange(params, shapes, 7 * n, start, finish, mid)


def _pair_spec(g0, g1):
    n = len(g0)

    def start(ins, outs, ss, rs):
        x, y, c, _ = _place()

        @pl.when(c == 0)
        def _():
            for p in range(n):
                _remote(ins[n + p], outs[p], ss, rs, p, (x, y, 1 - c)).start()

        @pl.when(c == 1)
        def _():
            for p in range(n):
                _remote(ins[p], outs[p], ss, rs, p, (x, y, 1 - c)).start()

    def finish(ins, outs, ss, rs):
        x, y, c, _ = _place()
        for p in range(n):
            _remote(ins[p], outs[p], ss, rs, p, (x, y, 1 - c)).wait()

    return _Exchange(list(g0) + list(g1), [jax.ShapeDtypeStruct(a.shape, a.dtype) for a in g0], n, start, finish)


def _chip_spec(s):
    n = len(s)

    def start(ins, outs, ss, rs):
        x, y, c, chips = _place()
        for p in range(n):
            for j, (cx, cy) in enumerate(chips):
                _remote(ins[p].at[2 * cx + cy], outs[p].at[2 * x + y], ss, rs, 3 * p + j, (cx, cy, c)).start()

    def finish(ins, outs, ss, rs):
        x, y, c, chips = _place()
        for p in range(n):
            for j, (cx, cy) in enumerate(chips):
                slot = outs[p].at[2 * cx + cy]
                _remote(slot, slot, ss, rs, 3 * p + j, (x, y, c)).wait_recv()
        for p in range(n):
            for j, (cx, cy) in enumerate(chips):
                _remote(ins[p].at[2 * cx + cy], outs[p].at[2 * x + y], ss, rs, 3 * p + j, (x, y, c)).wait_send()

    return _Exchange(s, [jax.ShapeDtypeStruct(a.shape, a.dtype) for a in s], 3 * n, start, finish)


def _exchange(ex, *, name):
    ni, no = len(ex.ins), len(ex.out_shapes)

    def body(*refs):
        ins, outs = refs[:ni], refs[ni:ni + no]
        ss, rs = refs[ni + no:]
        ex.start(ins, outs, ss, rs)
        if ex.mid is not None:
            ex.mid(ins, outs, ss, rs)
        ex.finish(ins, outs, ss, rs)

    return list(pl.pallas_call(
        body, name=name, in_specs=[HBM_SPEC] * ni, out_specs=[HBM_SPEC] * no, out_shape=ex.out_shapes,
        scratch_shapes=[pltpu.SemaphoreType.DMA((ex.n_sems,)), pltpu.SemaphoreType.DMA((ex.n_sems,))],
    )(*ex.ins))


def _call_with_carry(kern, *, name, grid, in_specs, out_specs, out_shape, scratch_shapes, args, carry=None):
    out_specs, out_shape = list(out_specs), list(out_shape)
    if carry is None:
        res = pl.pallas_call(kern, name=name, grid=grid, in_specs=in_specs, out_specs=out_specs, out_shape=out_shape,
                             scratch_shapes=scratch_shapes, compiler_params=_cp(("parallel", "arbitrary")))(*args)
        return list(res), []
    ni, no, ns = len(in_specs), len(out_specs), len(scratch_shapes)
    ci, co = len(carry.ins), len(carry.out_shapes)

    def wrapped(*refs):
        ins, cins = refs[:ni], refs[ni:ni + ci]
        outs, couts = refs[ni + ci:ni + ci + no], refs[ni + ci + no:ni + ci + no + co]
        scratch = refs[ni + ci + no + co:ni + ci + no + co + ns]
        ss, rs = refs[-2:]
        row, col = pl.program_id(0), pl.program_id(1)

        @pl.when((row == 0) & (col == 0))
        def _():
            carry.start(cins, couts, ss, rs)

        kern(*ins, *outs, *scratch)

        if carry.mid is not None:
            @pl.when((row == grid[0] - 1) & (col == grid[1] // 2))
            def _():
                carry.mid(cins, couts, ss, rs)

        @pl.when((row == grid[0] - 1) & (col == grid[1] - 1))
        def _():
            carry.finish(cins, couts, ss, rs)

    res = pl.pallas_call(
        wrapped, name=name, grid=grid, in_specs=list(in_specs) + [HBM_SPEC] * ci,
        out_specs=out_specs + [HBM_SPEC] * co, out_shape=out_shape + carry.out_shapes,
        scratch_shapes=list(scratch_shapes) + [pltpu.SemaphoreType.DMA((carry.n_sems,)),
                                               pltpu.SemaphoreType.DMA((carry.n_sems,))],
        compiler_params=_cp(("arbitrary", "arbitrary")))(*args, *carry.ins)
    return list(res[:no]), list(res[no:])


def _pair_swap(r, *, name):
    n = len(r)

    def body(*refs):
        ins, outs = refs[:n], refs[n:2 * n]
        send_sems, recv_sems = refs[2 * n:]
        x, y, c, _ = _place()
        cps = [_remote(ins[p], outs[p], send_sems, recv_sems, p, (x, y, 1 - c)) for p in range(n)]
        for cp in cps:
            cp.start()
        for cp in cps:
            cp.wait()

    return pl.pallas_call(
        body, name=name, in_specs=[HBM_SPEC] * n, out_specs=[HBM_SPEC] * n,
        out_shape=[jax.ShapeDtypeStruct(a.shape, a.dtype) for a in r],
        scratch_shapes=[pltpu.SemaphoreType.DMA((n,)), pltpu.SemaphoreType.DMA((n,))],
    )(*r)


def _gather8(v, *, name):
    R, C = v.shape
    flips = [(bx, by, bc) for bx in (0, 1) for by in (0, 1) for bc in (0, 1)][1:]

    def body(v_ref, out_ref, send_sems, recv_sems, local_sem):
        x, y, c, _ = _place()
        flip = lambda a, b: 1 - a if b else a
        mine = out_ref.at[4 * x + 2 * y + c]
        local = pltpu.make_async_copy(v_ref, mine, local_sem)
        local.start()
        peers = [(flip(x, bx), flip(y, by), flip(c, bc)) for bx, by, bc in flips]
        cps = [_remote(v_ref, mine, send_sems, recv_sems, j, peer) for j, peer in enumerate(peers)]
        for cp in cps:
            cp.start()
        for j, (px, py, pc) in enumerate(peers):
            slot = out_ref.at[4 * px + 2 * py + pc]
            _remote(slot, slot, send_sems, recv_sems, j, (x, y, c)).wait_recv()
        for cp in cps:
            cp.wait_send()
        local.wait()

    return pl.pallas_call(
        body, name=name, in_specs=[HBM_SPEC], out_specs=HBM_SPEC,
        out_shape=jax.ShapeDtypeStruct((8, R, C), v.dtype),
        scratch_shapes=[pltpu.SemaphoreType.DMA((7,)), pltpu.SemaphoreType.DMA((7,)), pltpu.SemaphoreType.DMA],
    )(v)


def _row_block(rows, cols, limit=256 * 1024):
    if rows * cols <= limit:
        return rows
    for br in range(limit // cols // 8 * 8, 0, -8):
        if rows % br == 0:
            return br
    return rows


def _sum_slots(buf, *, name):
    n, R, C = buf.shape
    br = _row_block(R, C)

    def kern(b_ref, o_ref):
        acc = b_ref[0].astype(F32)
        for s in range(1, n):
            acc = acc + b_ref[s].astype(F32)
        o_ref[...] = acc

    return pl.pallas_call(
        kern, name=name, grid=(pl.cdiv(R, br),),
        in_specs=[pl.BlockSpec((n, br, C), lambda i: (0, i, 0))],
        out_specs=pl.BlockSpec((br, C), lambda i: (i, 0)),
        out_shape=jax.ShapeDtypeStruct((R, C), F32),
        compiler_params=_cp(("arbitrary",)),
    )(buf)


def _scalar(s):
    return jnp.reshape(s, (1,)).astype(jnp.int32)


def _sum_pair(g0, g1, other, c, *, name):
    _, R, C = g0.shape
    br = _row_block(R, C)

    def kern(c_ref, g0_ref, g1_ref, o_ref, out_ref):
        own = jnp.where(c_ref[0] == 0, g0_ref[...], g1_ref[...])
        out_ref[...] = (own + o_ref[...]).astype(out_ref.dtype)

    blk = (None, br, C)
    return pl.pallas_call(
        kern, name=name,
        grid_spec=pltpu.PrefetchScalarGridSpec(
            num_scalar_prefetch=1, grid=(N_CHIPS, R // br),
            in_specs=[pl.BlockSpec(blk, lambda k, i, cr: (k, i * (1 - cr[0]), 0)),
                      pl.BlockSpec(blk, lambda k, i, cr: (k, i * cr[0], 0)),
                      pl.BlockSpec(blk, lambda k, i, cr: (k, i, 0))],
            out_specs=pl.BlockSpec(blk, lambda k, i, cr: (k, i, 0))),
        out_shape=jax.ShapeDtypeStruct((N_CHIPS, R, C), BF16),
        compiler_params=_cp(("arbitrary", "arbitrary")),
    )(_scalar(c), g0, g1, other)


def _sum_chips(s, got, k, *, name):
    _, R, C = s.shape
    br = _row_block(R, C)

    def kern(k_ref, s_ref, a_ref, b_ref, c_ref, out_ref):
        out_ref[...] = ((s_ref[...].astype(F32) + a_ref[...].astype(F32)) + b_ref[...].astype(F32)) \
            + c_ref[...].astype(F32)

    blk = (None, br, C)
    peer = lambda d: pl.BlockSpec(blk, lambda i, kr: ((kr[0] + d) % N_CHIPS, i, 0))
    return pl.pallas_call(
        kern, name=name,
        grid_spec=pltpu.PrefetchScalarGridSpec(
            num_scalar_prefetch=1, grid=(R // br,),
            in_specs=[peer(0), peer(1), peer(2), peer(3)],
            out_specs=pl.BlockSpec((br, C), lambda i, kr: (i, 0))),
        out_shape=jax.ShapeDtypeStruct((R, C), F32),
        compiler_params=_cp(("arbitrary",)),
    )(_scalar(k), s, got, got, got)


def _adam_math(w, g, m, v):
    nm = ADAM_B1 * m + (1.0 - ADAM_B1) * g
    nv = ADAM_B2 * v + (1.0 - ADAM_B2) * jnp.square(g)
    m_hat = nm / (1.0 - ADAM_B1 ** ADAM_STEP)
    v_hat = nv / (1.0 - ADAM_B2 ** ADAM_STEP)
    return -ADAM_LR * (m_hat / (jnp.sqrt(v_hat) + ADAM_EPS) + ADAM_WD * w), nm, nv


def _adamw_layers(w, mine, theirs, m, v, c, *, name):
    shape = w.shape
    R, C = mine.shape
    w3, m3, v3 = (a.reshape(DEPTH, R, C) for a in (w, m, v))
    br = _row_block(R, C)

    def kern(c_ref, w_ref, a_ref, b_ref, m_ref, v_ref, g_ref, d_ref, nm_ref, nv_ref):
        g = jnp.where(pl.program_id(0) == c_ref[0], a_ref[...], b_ref[...])
        g_ref[...] = g
        d_ref[...], nm_ref[...], nv_ref[...] = _adam_math(w_ref[...], g, m_ref[...], v_ref[...])

    lay = pl.BlockSpec((None, br, C), lambda l, i, cr: (l, i, 0))
    outs = pl.pallas_call(
        kern, name=name,
        grid_spec=pltpu.PrefetchScalarGridSpec(
            num_scalar_prefetch=1, grid=(DEPTH, R // br),
            in_specs=[lay,
                      pl.BlockSpec((br, C), lambda l, i, cr: (jnp.where(l == cr[0], i, 0), 0)),
                      pl.BlockSpec((br, C), lambda l, i, cr: (jnp.where(l == cr[0], 0, i), 0)),
                      lay, lay],
            out_specs=[lay] * 4),
        out_shape=[jax.ShapeDtypeStruct((DEPTH, R, C), F32)] * 4,
        compiler_params=_cp(("arbitrary", "arbitrary")),
    )(_scalar(c), w3, mine, theirs, m3, v3)
    return [o.reshape(shape) for o in outs]


def _adamw(w, g, m, v, *, name):
    shape = w.shape
    cols = shape[-1]
    w2, g2, m2, v2 = (a.reshape(-1, cols) for a in (w, g, m, v))
    rows = w2.shape[0]
    br = _row_block(rows, cols)

    def kern(w_ref, g_ref, m_ref, v_ref, d_ref, nm_ref, nv_ref):
        gv = g_ref[...]
        nm = ADAM_B1 * m_ref[...] + (1.0 - ADAM_B1) * gv
        nv = ADAM_B2 * v_ref[...] + (1.0 - ADAM_B2) * jnp.square(gv)
        m_hat = nm / (1.0 - ADAM_B1 ** ADAM_STEP)
        v_hat = nv / (1.0 - ADAM_B2 ** ADAM_STEP)
        d_ref[...] = -ADAM_LR * (m_hat / (jnp.sqrt(v_hat) + ADAM_EPS) + ADAM_WD * w_ref[...])
        nm_ref[...] = nm
        nv_ref[...] = nv

    spec = pl.BlockSpec((br, cols), lambda i: (i, 0))
    outs = pl.pallas_call(
        kern, name=name, grid=(rows // br,), in_specs=[spec] * 4, out_specs=[spec] * 3,
        out_shape=[jax.ShapeDtypeStruct((rows, cols), F32)] * 3,
        compiler_params=_cp(("arbitrary",)),
    )(w2, g2, m2, v2)
    return [o.reshape(shape) for o in outs]


_NAMES = ['ln_in_g', 'ln_in_b', 'w_in', 'b_forget', 'conv_w', 'conv_b', 'w_r', 'b_r', 'w_i', 'b_i', 'lru_lambda',
          'rel_bias', 'w_branch', 'w_gate', 'b_gate', 'w_out', 'ln1_g', 'ln1_b', 'w_ff1', 'w_ff2', 'ln2_g', 'ln2_b']
_BIG = {'w_in': 2, 'w_branch': 3, 'w_gate': 2, 'w_out': 1, 'w_ff1': 2, 'w_ff2': 1}
_SMALL_SHARDED = {'b_gate': 2, 'conv_w': 2, 'rel_bias': 2}
_SHARDED = {**_BIG, **_SMALL_SHARDED}
_REPLICATED = [n for n in _NAMES if n not in _SHARDED]
_TILE = 8 * LANES


def _tiles(a, cols):
    flat = a.reshape(-1)
    per = 8 * cols
    flat = jnp.pad(flat, (0, (-flat.shape[0]) % per))
    return flat.reshape(-1, cols)


def _pack(arrs, cols):
    return jnp.concatenate([_tiles(a, cols) for a in arrs], axis=0)


def _unpack(packed, like, cols):
    out, r0 = [], 0
    for a in like:
        n = math.prod(a.shape)
        rows = -(-n // (8 * cols)) * 8
        out.append(packed[r0:r0 + rows].reshape(-1)[:n].reshape(a.shape))
        r0 += rows
    return out


_EARLY = ['w_branch', 'w_gate', 'w_out', 'w_ff1', 'w_ff2']


def _chip_major_early(G, l):
    return [G[('w_branch', l)].reshape(N_CHIPS, N_BRANCH * BRANCH_WIDTH, BRANCH_WIDTH),
            G[('w_gate', l)].reshape(N_CHIPS, N_BRANCH * (D_MODEL // N_CHIPS), D_MODEL),
            G[('w_out', l)], G[('w_ff1', l)], G[('w_ff2', l)]]


def _chip_major_late(G, l):
    g_att, g_rec = G[('w_att', l)], G[('w_rec', l)]
    w_in = jnp.concatenate([g_att[:, :1536], g_rec[:, 1024:1024 + N_HEADS], g_rec[:, :1024], g_att[:, 1536:]], -1)
    per_chip = lambda g, rows: g.reshape(rows, N_CHIPS, -1).transpose(1, 0, 2)
    bg = per_chip(G[('b_gate', l)], N_BRANCH)
    cw = per_chip(G[('conv_w', l)], CONV_WIDTH)
    rb = per_chip(G[('rel_bias', l)], N_HEADS)
    small = jnp.stack([_pack([bg[j], cw[j], rb[j]], LANES) for j in range(N_CHIPS)])
    return [w_in.reshape(D_MODEL, N_CHIPS, D_IN // N_CHIPS).transpose(1, 0, 2), small]


class _Hook:
    def __init__(self, spec, done):
        self.spec, self.done = spec, done


def _unshard(blocks, axis):
    return jnp.concatenate([blocks[k] for k in range(N_CHIPS)], axis=axis)


def kernel(x, ln_in_g, ln_in_b, w_in, b_forget, conv_w, conv_b, w_r, b_r, w_i, b_i, lru_lambda, rel_bias, w_branch, w_gate, b_gate, w_out, ln1_g, ln1_b, w_ff1, w_ff2, ln2_g, ln2_b, loss_target, m_ln_in_g, m_ln_in_b, m_w_in, m_b_forget, m_conv_w, m_conv_b, m_w_r, m_b_r, m_w_i, m_b_i, m_lru_lambda, m_rel_bias, m_w_branch, m_w_gate, m_b_gate, m_w_out, m_ln1_g, m_ln1_b, m_w_ff1, m_w_ff2, m_ln2_g, m_ln2_b, v_ln_in_g, v_ln_in_b, v_w_in, v_b_forget, v_conv_w, v_conv_b, v_w_r, v_b_r, v_w_i, v_b_i, v_lru_lambda, v_rel_bias, v_w_branch, v_w_gate, v_b_gate, v_w_out, v_ln1_g, v_ln1_b, v_w_ff1, v_w_ff2, v_ln2_g, v_ln2_b):
    w = dict(zip(_NAMES, (ln_in_g, ln_in_b, w_in, b_forget, conv_w, conv_b, w_r, b_r, w_i, b_i, lru_lambda, rel_bias,
                          w_branch, w_gate, b_gate, w_out, ln1_g, ln1_b, w_ff1, w_ff2, ln2_g, ln2_b)))
    m = dict(zip(_NAMES, (m_ln_in_g, m_ln_in_b, m_w_in, m_b_forget, m_conv_w, m_conv_b, m_w_r, m_b_r, m_w_i, m_b_i,
                          m_lru_lambda, m_rel_bias, m_w_branch, m_w_gate, m_b_gate, m_w_out, m_ln1_g, m_ln1_b,
                          m_w_ff1, m_w_ff2, m_ln2_g, m_ln2_b)))
    v = dict(zip(_NAMES, (v_ln_in_g, v_ln_in_b, v_w_in, v_b_forget, v_conv_w, v_conv_b, v_w_r, v_b_r, v_w_i, v_b_i,
                          v_lru_lambda, v_rel_bias, v_w_branch, v_w_gate, v_b_gate, v_w_out, v_ln1_g, v_ln1_b,
                          v_w_ff1, v_w_ff2, v_ln2_g, v_ln2_b)))
    c = lax.axis_index("c")

    k = 2 * lax.axis_index("x") + lax.axis_index("y")
    state = {}

    small_like = [w[n] for n in _SMALL_SHARDED]
    small_pack = jnp.stack([_pack([a[l] for a in small_like], LANES) for l in range(DEPTH)])
    W = _prep_weights({n: w[n] for n in _REPLICATED})
    got_in, got_small = _exchange(_gather_spec([w['w_in'].astype(BF16), small_pack]), name='gather_first')
    small_blocks = [[_unpack(got_small[j, l], [a[l] for a in small_like], LANES) for l in range(DEPTH)]
                    for j in range(N_CHIPS)]
    first = {'w_in': _unshard(got_in, _BIG['w_in'])}
    for i, n in enumerate(_SMALL_SHARDED):
        first[n] = jnp.concatenate([jnp.stack([small_blocks[j][l][i] for l in range(DEPTH)])
                                    for j in range(N_CHIPS)], axis=_SMALL_SHARDED[n])
    _prep_weights(first, W)

    def gather_on(names):
        return _Hook(lambda W_, G_: _gather_spec([w[n].astype(BF16) for n in names]),
                     lambda outs, W_, G_: _prep_weights({n: _unshard(o, _BIG[n]) for n, o in zip(names, outs)}, W_))

    def pair_spec(W_, G_):
        state['early'] = [_chip_major_early(G_, l) for l in range(DEPTH)]
        return _pair_spec(*state['early'])

    def pair_done(outs, W_, G_):
        state['pair_sum'] = [_sum_pair(a0, a1, o, c, name='grad_pair_sum')
                             for a0, a1, o in zip(*state['early'], outs)]

    hooks = {('fox_fwd', 0): gather_on(['w_gate', 'w_branch', 'w_out']),
             ('sb_fwd', 0): gather_on(['w_ff1', 'w_ff2']),
             ('fox_bwd', 0): _Hook(pair_spec, pair_done),
             ('sb_bwd', 0): _Hook(lambda W_, G_: _chip_spec(state['pair_sum']),
                                  lambda outs, W_, G_: state.update(from_chips=outs))}
    loss, gx, G = _device_step(x[0], loss_target[0], W, hooks)

    late = [_chip_major_late(G, l) for l in range(DEPTH)]
    late_sibling = _exchange(_pair_spec(*late), name='grad_pair_exchange')
    late_sum = [_sum_pair(a0, a1, o, c, name='grad_pair_sum') for a0, a1, o in zip(*late, late_sibling)]
    late_chips = _exchange(_chip_spec(late_sum), name='grad_chip_exchange')
    pair_sum = [late_sum[0]] + state['pair_sum'] + [late_sum[1]]
    from_chips = [late_chips[0]] + state['from_chips'] + [late_chips[1]]
    mine = [_sum_chips(s, got, k, name='grad_chip_sum') for s, got in zip(pair_sum, from_chips)]
    theirs = _pair_swap(mine, name='grad_pair_swap')

    rep_like = [w[n] for n in _REPLICATED]
    g_rep_dev = {'ln_in_g': G[('ln_in_g', -1)][0], 'ln_in_b': G[('ln_in_b', -1)][0]}
    for n in _REPLICATED[2:]:
        g_rep_dev[n] = jnp.stack([G[(n, l)].reshape(w[n].shape[1:]) for l in range(DEPTH)])
    rep_all = _gather8(_pack([g_rep_dev[n] for n in _REPLICATED], LANES), name='grad_gather8')
    g_rep = dict(zip(_REPLICATED, _unpack(_sum_slots(rep_all, name='grad_sum8'), rep_like, LANES)))

    grads, delta, new_m, new_v = {}, {}, {}, {}
    for n, a, b in zip(_BIG, mine, theirs):
        grads[n], delta[n], new_m[n], new_v[n] = _adamw_layers(w[n], a, b, m[n], v[n], c, name='adamw')
    small_layers = [jnp.where(c == l, mine[-1], theirs[-1]) for l in range(DEPTH)]
    small_shards = [_unpack(s, [w[n][0] for n in _SMALL_SHARDED], LANES) for s in small_layers]
    g_shard = {n: jnp.stack([small_shards[l][i] for l in range(DEPTH)]) for i, n in enumerate(_SMALL_SHARDED)}
    small = _REPLICATED + list(_SMALL_SHARDED)
    for n in small:
        grads[n] = g_rep[n] if n in g_rep else g_shard[n]
    packs = [_pack([d[n] for n in small], LANES) for d in (w, grads, m, v)]
    outs = _adamw(*packs, name='adamw_small')
    small_like_all = [w[n] for n in small]
    for d, o in zip((delta, new_m, new_v), outs):
        d.update(zip(small, _unpack(o, small_like_all, LANES)))

    loss = lax.psum(loss, ("x", "y", "c"))
    return (loss, gx[None], *[grads[n] for n in _NAMES], *[delta[n] for n in _NAMES],
            *[new_m[n] for n in _NAMES], *[new_v[n] for n in _NAMES])
```

```python
import functools
import math

import jax
import jax.numpy as jnp
from jax import lax
from jax.experimental import pallas as pl
from jax.experimental.pallas import tpu as pltpu

F32 = jnp.float32
BF16 = jnp.bfloat16

D_MODEL = 2048
DEPTH = 2
CHUNK = 64
HEAD_DIM = 128
N_BRANCH = 4
BRANCH_WIDTH = 512
N_HEADS = 4
CONV_WIDTH = 4
LRU_C = 8.0
LOOKBACK_CHUNKS = 8
BAND = (LOOKBACK_CHUNKS + 1) * CHUNK
PAD_ROWS = LOOKBACK_CHUNKS * CHUNK
REL_CLIP = 256
REL_TABLE = REL_CLIP + CHUNK
REL_PAD = 384
D_FF = 4 * D_MODEL
FF_SHARD = D_FF // 4
D_IN = 5636
ALPHA = (2.0 * DEPTH) ** 0.25
LN_EPS = 1e-5
SCALE = HEAD_DIM ** -0.5

ADAM_LR = 0.001
ADAM_B1 = 0.9
ADAM_B2 = 0.999
ADAM_EPS = 1e-08
ADAM_WD = 0.01
ADAM_STEP = 10

N_ATT = 9 * BRANCH_WIDTH
N_REC = 2 * BRANCH_WIDTH + 128

V7X_VMEM_LIMIT = 56 * 1024 * 1024
LANES = 128
ATT_BLOCK = 256
ATT_KEYS = 1024

NT = (((1,), (1,)), ((), ()))
TN = (((0,), (0,)), ((), ()))
NN = (((1,), (0,)), ((), ()))

MESH = pl.DeviceIdType.MESH


def _cp(sem=None):
    return pltpu.CompilerParams(dimension_semantics=sem, vmem_limit_bytes=V7X_VMEM_LIMIT)


def _dot(a, b, dims=NN):
    return lax.dot_general(a, b, dims, preferred_element_type=F32)


def _pick(n, prefs):
    for p in prefs:
        if n % p == 0:
            return p
    return n


def _split3(x):
    hi = x.astype(BF16)
    r1 = x - hi.astype(F32)
    mid = r1.astype(BF16)
    lo = (r1 - mid.astype(F32)).astype(BF16)
    return hi, mid, lo


def _split2(x):
    hi = x.astype(BF16)
    lo = (x - hi.astype(F32)).astype(BF16)
    return hi, lo


def _sigmoid(z):
    return 1.0 / (1.0 + jnp.exp(-z))


def _log_sigmoid(z):
    return jnp.minimum(z, 0.0) - jnp.log(1.0 + jnp.exp(-jnp.abs(z)))


def _mm(a, b, *, name, ta=False, tb=False, out_dtypes=(F32,), epilogue=None, extras=(),
        bm=None, bn=None, bk=None, out_map=None, b_view=None):
    M, K = (a.shape[1], a.shape[0]) if ta else a.shape
    N = b.shape[0] if tb else b.shape[1]
    if b_view is not None:
        K, N = b_view[:2]
    bm = bm or _pick(M, (1024, 512, 256, 128))
    bn = bn or _pick(N, (1024, 1536, 1152, 512, 256, 128))
    bk = bk or _pick(K, (2048, 1536, 1024, 1152, 512, 256, 128))
    nk = K // bk
    a_spec = pl.BlockSpec((bk, bm), lambda i, j, k: (k, i)) if ta else pl.BlockSpec((bm, bk), lambda i, j, k: (i, k))
    b_spec = pl.BlockSpec((bn, bk), lambda i, j, k: (j, k)) if tb else pl.BlockSpec((bk, bn), lambda i, j, k: (k, j))
    if b_view is not None:
        b_spec = pl.BlockSpec(b_view[2], b_view[3])
    ex_specs = [pl.BlockSpec((bm, bn), lambda i, j, k: (i, j)) if kind == 'mn'
                else pl.BlockSpec((1, bn), lambda i, j, k: (0, j)) for _, kind in extras]
    n_ex, n_out = len(extras), len(out_dtypes)
    dims = TN if ta else (NT if tb else NN)

    def kern(*refs):
        a_ref, b_ref = refs[0], refs[1]
        ex_refs = refs[2:2 + n_ex]
        out_refs = refs[2 + n_ex:2 + n_ex + n_out]
        acc_ref = refs[-1]
        k = pl.program_id(2)
        part = _dot(a_ref[...].astype(BF16), b_ref[...].astype(BF16), dims)

        @pl.when(k == 0)
        def _():
            acc_ref[...] = part

        @pl.when(k > 0)
        def _():
            acc_ref[...] += part

        @pl.when(k == nk - 1)
        def _():
            acc = acc_ref[...]
            outs = (acc,) if epilogue is None else epilogue(acc, *[r[...] for r in ex_refs])
            for o_ref, o in zip(out_refs, outs):
                o_ref[...] = o.astype(o_ref.dtype).reshape(o_ref.shape)

    if out_map is None:
        out_specs = [pl.BlockSpec((bm, bn), lambda i, j, k: (i, j)) for _ in out_dtypes]
        out_shape = [jax.ShapeDtypeStruct((M, N), dt) for dt in out_dtypes]
    else:
        shape, block, index = out_map
        out_specs = [pl.BlockSpec(block, lambda i, j, k: index(i, j))]
        out_shape = [jax.ShapeDtypeStruct(shape, out_dtypes[0])]
    res = pl.pallas_call(
        kern, name=name, grid=(M // bm, N // bn, nk),
        in_specs=[a_spec, b_spec] + ex_specs,
        out_specs=out_specs,
        out_shape=out_shape,
        scratch_shapes=[pltpu.VMEM((bm, bn), F32)],
        compiler_params=_cp(("parallel", "parallel", "arbitrary")),
    )(a, b, *[e for e, _ in extras])
    return res[0] if n_out == 1 else res


def _ln_fwd(h, g, b, *, name):
    T, D = h.shape
    bt = _pick(T, (512, 256, 128))

    def kern(h_ref, g_ref, b_ref, y_ref, yb_ref):
        x = h_ref[...]
        mu = jnp.mean(x, axis=-1, keepdims=True)
        xc = x - mu
        var = jnp.mean(xc * xc, axis=-1, keepdims=True)
        y = xc * lax.rsqrt(var + LN_EPS) * g_ref[...] + b_ref[...]
        y_ref[...] = y
        yb_ref[...] = y.astype(BF16)

    row = pl.BlockSpec((bt, D), lambda i: (i, 0))
    vec = pl.BlockSpec((1, D), lambda i: (0, 0))
    return pl.pallas_call(
        kern, name=name, grid=(T // bt,), in_specs=[row, vec, vec], out_specs=[row, row],
        out_shape=[jax.ShapeDtypeStruct((T, D), F32), jax.ShapeDtypeStruct((T, D), BF16)],
        compiler_params=_cp(("arbitrary",)),
    )(h, g.reshape(1, D), b.reshape(1, D))


def _ln_bwd(h, dy, g, *, name):
    T, D = h.shape
    bt = _pick(T, (512, 256, 128))

    def kern(h_ref, dy_ref, g_ref, dh_ref, dhb_ref, dg_ref, db_ref):
        i = pl.program_id(0)
        x = h_ref[...]
        dyv = dy_ref[...]
        mu = jnp.mean(x, axis=-1, keepdims=True)
        xc = x - mu
        var = jnp.mean(xc * xc, axis=-1, keepdims=True)
        rstd = lax.rsqrt(var + LN_EPS)
        xhat = xc * rstd
        dxh = dyv * g_ref[...]
        m1 = jnp.mean(dxh, axis=-1, keepdims=True)
        m2 = jnp.mean(dxh * xhat, axis=-1, keepdims=True)
        dh = rstd * (dxh - m1 - xhat * m2)
        dh_ref[...] = dh
        dhb_ref[...] = dh.astype(BF16)
        pg = jnp.sum(dyv * xhat, axis=0, keepdims=True)
        pb = jnp.sum(dyv, axis=0, keepdims=True)

        @pl.when(i == 0)
        def _():
            dg_ref[...] = pg
            db_ref[...] = pb

        @pl.when(i > 0)
        def _():
            dg_ref[...] += pg
            db_ref[...] += pb

    row = pl.BlockSpec((bt, D), lambda i: (i, 0))
    vec = pl.BlockSpec((1, D), lambda i: (0, 0))
    return pl.pallas_call(
        kern, name=name, grid=(T // bt,), in_specs=[row, row, vec], out_specs=[row, row, vec, vec],
        out_shape=[jax.ShapeDtypeStruct((T, D), F32), jax.ShapeDtypeStruct((T, D), BF16),
                   jax.ShapeDtypeStruct((1, D), F32), jax.ShapeDtypeStruct((1, D), F32)],
        compiler_params=_cp(("arbitrary",)),
    )(h, dy, g.reshape(1, D))


def _loss_head(y, tgt, *, name):
    T, D = y.shape
    bt = _pick(T, (512, 256, 128))

    def kern(y_ref, t_ref, dy_ref, loss_ref):
        i = pl.program_id(0)
        e = y_ref[...] - t_ref[...]
        dy_ref[...] = e * (1.0 / D)
        part = 0.5 * jnp.sum(jnp.sum(e * e, axis=-1, keepdims=True) * (1.0 / D), axis=0, keepdims=True)
        part = jnp.broadcast_to(part, (8, LANES))

        @pl.when(i == 0)
        def _():
            loss_ref[...] = part

        @pl.when(i > 0)
        def _():
            loss_ref[...] += part

    row = pl.BlockSpec((bt, D), lambda i: (i, 0))
    return pl.pallas_call(
        kern, name=name, grid=(T // bt,), in_specs=[row, row],
        out_specs=[row, pl.BlockSpec((8, LANES), lambda i: (0, 0))],
        out_shape=[jax.ShapeDtypeStruct((T, D), F32), jax.ShapeDtypeStruct((8, LANES), F32)],
        compiler_params=_cp(("arbitrary",)),
    )(y, tgt)


def _tri(n, upper):
    r = lax.broadcasted_iota(jnp.int32, (n, n), 0)
    c = lax.broadcasted_iota(jnp.int32, (n, n), 1)
    return jnp.where((c >= r) if upper else (c <= r), 1.0, 0.0).astype(BF16)


def _forget_fwd(ff, bf, *, name):
    T = ff.shape[0]
    bt = 256

    def kern(ff_ref, bf_ref, out_ref, carry):
        i = pl.program_id(0)

        @pl.when(i == 0)
        def _():
            carry[...] = jnp.zeros_like(carry)

        ls = _log_sigmoid(ff_ref[...] + bf_ref[...])
        tri = _tri(bt, upper=False)
        hi, mid, lo = _split3(ls)
        cs = _dot(tri, hi) + _dot(tri, mid) + _dot(tri, lo) + carry[0:1, :]
        out_ref[...] = cs
        carry[...] = jnp.broadcast_to(cs[bt - 1:bt, :], carry.shape)

    return pl.pallas_call(
        kern, name=name, grid=(T // bt,),
        in_specs=[pl.BlockSpec((bt, LANES), lambda i: (i, 0)), pl.BlockSpec((1, LANES), lambda i: (0, 0))],
        out_specs=pl.BlockSpec((bt, LANES), lambda i: (i, 0)),
        out_shape=jax.ShapeDtypeStruct((T, LANES), F32),
        scratch_shapes=[pltpu.VMEM((8, LANES), F32)],
        compiler_params=_cp(("arbitrary",)),
    )(ff, bf)


def _forget_bwd(dFk, dFq, ff, bf, *, name):
    T = ff.shape[0]
    bt = 256
    nb = T // bt

    def kern(dFk_ref, dFq_ref, ff_ref, bf_ref, dff_ref, dbf_ref, carry):
        i = pl.program_id(0)

        @pl.when(i == 0)
        def _():
            carry[...] = jnp.zeros_like(carry)
            dbf_ref[...] = jnp.zeros_like(dbf_ref)

        tri = _tri(bt, upper=True)
        hi, mid, lo = _split3(dFk_ref[...] + dFq_ref[...])
        rs = _dot(tri, hi) + _dot(tri, mid) + _dot(tri, lo) + carry[0:1, :]
        carry[...] = jnp.broadcast_to(rs[0:1, :], carry.shape)
        z = ff_ref[...] + bf_ref[...]
        dff = rs * _sigmoid(-z)
        dff_ref[...] = dff.astype(dff_ref.dtype)
        dbf_ref[...] += jnp.sum(dff, axis=0, keepdims=True)

    rev = pl.BlockSpec((bt, LANES), lambda i: (nb - 1 - i, 0))
    vec = pl.BlockSpec((1, LANES), lambda i: (0, 0))
    return pl.pallas_call(
        kern, name=name, grid=(nb,), in_specs=[rev, rev, rev, vec], out_specs=[rev, vec],
        out_shape=[jax.ShapeDtypeStruct((T, LANES), BF16), jax.ShapeDtypeStruct((1, LANES), F32)],
        scratch_shapes=[pltpu.VMEM((8, LANES), F32)],
        compiler_params=_cp(("arbitrary",)),
    )(dFk, dFq, ff, bf)


def _head_lane(x, h):
    lane = lax.broadcasted_iota(jnp.int32, x.shape, 1)
    return jnp.sum(jnp.where(lane == h, x, 0.0), axis=1, keepdims=True)


def _att_blocks(T):
    return min(ATT_BLOCK, T), min(ATT_KEYS, T)


def _positions(i, j, bq, bk):
    r = i * bq + lax.broadcasted_iota(jnp.int32, (bq, bk), 0)
    c = j * bk + lax.broadcasted_iota(jnp.int32, (bq, bk), 1)
    return r, c


def _fox_fwd(u_att, fcum, frow, *, name, carry=None):
    T = u_att.shape[0]
    bq, bk = _att_blocks(T)
    nq, nk = T // bq, T // bk
    H = N_HEADS

    def kern(q_ref, k_ref, v_ref, fc_ref, fr_ref, o_ref, lse_ref):
        i = pl.program_id(1)
        q = q_ref[...]
        fq = _head_lane(fc_ref[...], pl.program_id(0))

        def step(j, carry, masked):
            m, l, acc = carry
            off = pl.multiple_of(j * bk, bk)
            k = k_ref[pl.ds(off, bk), :]
            v = v_ref[pl.ds(off, bk), :]
            s = _dot(q, k, NT) * SCALE + (fq - fr_ref[j])
            if masked:
                r, c = _positions(i, j, bq, bk)
                s = jnp.where(c <= r, s, -jnp.inf)
            m_new = jnp.maximum(m, jnp.max(s, axis=1, keepdims=True))
            a = jnp.exp(m - m_new)
            p = jnp.exp(s - m_new)
            l = a * l + jnp.sum(p, axis=1, keepdims=True)
            acc = a * acc + _dot(p.astype(BF16), v)
            return m_new, l, acc

        init = (jnp.full((bq, 1), -1e30, F32), jnp.zeros((bq, 1), F32), jnp.zeros((bq, HEAD_DIM), F32))
        nfull = (i * bq) // bk
        carry = lax.fori_loop(0, nfull, lambda j, cr: step(j, cr, False), init)
        m, l, acc = step(nfull, carry, True)
        o_ref[...] = (acc / l).astype(o_ref.dtype)
        lse_ref[...] = m + jnp.log(l)

    return _call_with_carry(
        kern, name=name, grid=(H, nq), carry=carry,
        in_specs=[pl.BlockSpec((bq, HEAD_DIM), lambda h, i: (i, h)),
                  pl.BlockSpec((T, HEAD_DIM), lambda h, i: (0, 4 + h)),
                  pl.BlockSpec((T, HEAD_DIM), lambda h, i: (0, 8 + h)),
                  pl.BlockSpec((bq, LANES), lambda h, i: (i, 0)),
                  pl.BlockSpec((None, nk, 1, bk), lambda h, i: (h, 0, 0, 0))],
        out_specs=[pl.BlockSpec((bq, HEAD_DIM), lambda h, i: (i, h)),
                   pl.BlockSpec((None, bq, 1), lambda h, i: (h, i, 0))],
        out_shape=[jax.ShapeDtypeStruct((T, BRANCH_WIDTH), BF16), jax.ShapeDtypeStruct((H, T, 1), F32)],
        scratch_shapes=[], args=(u_att, u_att, u_att, fcum, frow))


def _row_dot(a, b, *, name):
    T = a.shape[0]
    bt = _pick(T, (512, 256, 128))

    def kern(a_ref, b_ref, o_ref):
        p = a_ref[...].astype(F32) * b_ref[...].astype(F32)
        for h in range(N_HEADS):
            o_ref[h] = jnp.sum(p[:, h * HEAD_DIM:(h + 1) * HEAD_DIM], axis=1, keepdims=True)

    row = pl.BlockSpec((bt, BRANCH_WIDTH), lambda i: (i, 0))
    return pl.pallas_call(
        kern, name=name, grid=(T // bt,), in_specs=[row, row],
        out_specs=pl.BlockSpec((N_HEADS, bt, 1), lambda i: (0, i, 0)),
        out_shape=jax.ShapeDtypeStruct((N_HEADS, T, 1), F32),
        compiler_params=_cp(("arbitrary",)),
    )(a, b)


def _fox_bwd(u_att, do, lse, delta, fcum, frow, *, name, carry=None):
    T = u_att.shape[0]
    bq, bk = _att_blocks(T)
    nq, nk = T // bq, T // bk
    H = N_HEADS

    def kern(q_ref, k_ref, v_ref, do_ref, lse_ref, dl_ref, fc_ref, fr_ref,
             dq_ref, dk_ref, dv_ref, df_ref, dfq_ref, dk_acc, dv_acc, df_acc):
        i = pl.program_id(1)

        @pl.when(i == 0)
        def _():
            dk_acc[...] = jnp.zeros_like(dk_acc)
            dv_acc[...] = jnp.zeros_like(dv_acc)
            df_acc[...] = jnp.zeros_like(df_acc)

        q = q_ref[...]
        dov = do_ref[...]
        fq = _head_lane(fc_ref[...], pl.program_id(0))
        lsev = lse_ref[...]
        dlt = dl_ref[...]

        def step(j, carry, masked):
            dq, dfq = carry
            off = pl.multiple_of(j * bk, bk)
            k = k_ref[pl.ds(off, bk), :]
            v = v_ref[pl.ds(off, bk), :]
            s = _dot(q, k, NT) * SCALE + (fq - fr_ref[j])
            p = jnp.exp(s - lsev)
            if masked:
                r, c = _positions(i, j, bq, bk)
                p = jnp.where(c <= r, p, 0.0)
            dp = _dot(dov, v, NT)
            ds = p * (dp - dlt)
            dsb = ds.astype(BF16)
            dq = dq + _dot(dsb, k)
            dk_acc[pl.ds(off, bk), :] += _dot(dsb, q, TN)
            dv_acc[pl.ds(off, bk), :] += _dot(p.astype(BF16), dov, TN)
            df_acc[j] += -jnp.sum(ds, axis=0, keepdims=True)
            return dq, dfq + jnp.sum(ds, axis=1, keepdims=True)

        nfull = (i * bq) // bk
        carry = lax.fori_loop(0, nfull, lambda j, cr: step(j, cr, False),
                              (jnp.zeros((bq, HEAD_DIM), F32), jnp.zeros((bq, 1), F32)))
        dq, dfq = step(nfull, carry, True)
        dq_ref[...] = (dq * SCALE).astype(dq_ref.dtype)
        dfq_ref[...] = dfq

        @pl.when(i == nq - 1)
        def _():
            dk_ref[...] = (dk_acc[...] * SCALE).astype(dk_ref.dtype)
            dv_ref[...] = dv_acc[...].astype(dv_ref.dtype)
            df_ref[...] = df_acc[...]

    col = lambda: pl.BlockSpec((None, bq, 1), lambda h, i: (h, i, 0))
    return _call_with_carry(
        kern, name=name, grid=(H, nq), carry=carry,
        in_specs=[pl.BlockSpec((bq, HEAD_DIM), lambda h, i: (i, h)),
                  pl.BlockSpec((T, HEAD_DIM), lambda h, i: (0, 4 + h)),
                  pl.BlockSpec((T, HEAD_DIM), lambda h, i: (0, 8 + h)),
                  pl.BlockSpec((bq, HEAD_DIM), lambda h, i: (i, h)),
                  col(), col(), pl.BlockSpec((bq, LANES), lambda h, i: (i, 0)),
                  pl.BlockSpec((None, nk, 1, bk), lambda h, i: (h, 0, 0, 0))],
        out_specs=[pl.BlockSpec((bq, HEAD_DIM), lambda h, i: (i, h)),
                   pl.BlockSpec((T, HEAD_DIM), lambda h, i: (0, h)),
                   pl.BlockSpec((T, HEAD_DIM), lambda h, i: (0, h)),
                   pl.BlockSpec((None, nk, 1, bk), lambda h, i: (h, 0, 0, 0)),
                   pl.BlockSpec((None, bq, 1), lambda h, i: (h, i, 0))],
        out_shape=[jax.ShapeDtypeStruct((T, BRANCH_WIDTH), BF16)] * 3
                  + [jax.ShapeDtypeStruct((H, nk, 1, bk), F32), jax.ShapeDtypeStruct((H, T, 1), F32)],
        scratch_shapes=[pltpu.VMEM((T, HEAD_DIM), F32), pltpu.VMEM((T, HEAD_DIM), F32),
                        pltpu.VMEM((nk, 1, bk), F32)],
        args=(u_att, u_att, u_att, do, lse, delta, fcum, frow))


def _softplus_parts(z):
    t = jnp.exp(-jnp.abs(z))
    sp = jnp.maximum(z, 0.0) + jnp.log(1.0 + t)
    return t, sp


def _sb_tri(B):
    r = lax.broadcasted_iota(jnp.int32, (B, B), 0)
    c = lax.broadcasted_iota(jnp.int32, (B, B), 1)
    suffix = jnp.where(r >= c, 1.0, 0.0).astype(BF16)
    prefix = jnp.where(r <= c, 1.0, 0.0).astype(BF16)
    return suffix, prefix


def _sb_fwd(u_att, *, name, carry=None):
    T = u_att.shape[0]
    B, bk = _att_blocks(T)
    nq, nsub = T // B, bk // B
    H = N_HEADS

    def kern(q_ref, k_ref, v_ref, o_ref):
        i = pl.program_id(1)
        q = q_ref[...]
        suffix, _ = _sb_tri(B)

        def step(j, carry, masked):
            run, acc = carry
            parts = []
            for s in reversed(range(nsub)):
                jb = j * nsub + s
                off = pl.multiple_of(jb * B, B)
                k = k_ref[pl.ds(off, B), :]
                z = _dot(q, k, NT) * SCALE
                _, sp = _softplus_parts(z)
                lg = -sp
                valid = None
                if masked:
                    r, c = _positions(i, jb, B, B)
                    valid = c < r
                    lg = jnp.where(valid, lg, 0.0)
                hi, lo = _split2(lg)
                cum = _dot(hi, suffix) + _dot(lo, suffix)
                parts.append((off, z, cum, jnp.sum(lg, axis=1, keepdims=True), valid))
            for off, z, cum, rs, valid in parts:
                a = jnp.exp(z + cum + run)
                if masked:
                    a = jnp.where(valid, a, 0.0)
                acc = acc + _dot(a.astype(BF16), v_ref[pl.ds(off, B), :])
                run = run + rs
            return run, acc

        nfull = (i * B) // bk
        carry = step(nfull, (jnp.zeros((B, 1), F32), jnp.zeros((B, HEAD_DIM), F32)), True)
        _, acc = lax.fori_loop(0, nfull, lambda jj, cr: step(nfull - 1 - jj, cr, False), carry)
        o_ref[...] = acc.astype(o_ref.dtype)

    return _call_with_carry(
        kern, name=name, grid=(H, nq), carry=carry,
        in_specs=[pl.BlockSpec((B, HEAD_DIM), lambda h, i: (i, 12 + h)),
                  pl.BlockSpec((T, HEAD_DIM), lambda h, i: (0, 16 + h)),
                  pl.BlockSpec((T, HEAD_DIM), lambda h, i: (0, 20 + h))],
        out_specs=[pl.BlockSpec((B, HEAD_DIM), lambda h, i: (i, h))],
        out_shape=[jax.ShapeDtypeStruct((T, BRANCH_WIDTH), BF16)],
        scratch_shapes=[], args=(u_att, u_att, u_att))


def _sb_bwd(u_att, do, *, name, carry=None):
    T = u_att.shape[0]
    B, bk = _att_blocks(T)
    nq, nsub = T // B, bk // B
    H = N_HEADS

    def kern(q_ref, k_ref, v_ref, do_ref, dq_ref, dk_ref, dv_ref, dk_acc, dv_acc, de_s, sg_s):
        i = pl.program_id(1)

        @pl.when(i == 0)
        def _():
            dk_acc[...] = jnp.zeros_like(dk_acc)
            dv_acc[...] = jnp.zeros_like(dv_acc)

        q = q_ref[...]
        dov = do_ref[...]
        suffix, prefix = _sb_tri(B)

        def sweep1(j, run, masked):
            parts = []
            for s in reversed(range(nsub)):
                jb = j * nsub + s
                off = pl.multiple_of(jb * B, B)
                k = k_ref[pl.ds(off, B), :]
                z = _dot(q, k, NT) * SCALE
                t, sp = _softplus_parts(z)
                lg = -sp
                sg = jnp.exp(z + lg)
                valid = None
                if masked:
                    r, c = _positions(i, jb, B, B)
                    valid = c < r
                    lg = jnp.where(valid, lg, 0.0)
                    sg = jnp.where(valid, sg, 0.0)
                sg_s[jb] = sg.astype(sg_s.dtype)
                hi, lo = _split2(lg)
                cum = _dot(hi, suffix) + _dot(lo, suffix)
                da = _dot(dov, v_ref[pl.ds(off, B), :], NT)
                parts.append((jb, off, z, cum, da, jnp.sum(lg, axis=1, keepdims=True), valid))
            for jb, off, z, cum, da, rs, valid in parts:
                a = jnp.exp(z + cum + run)
                if masked:
                    a = jnp.where(valid, a, 0.0)
                de_s[jb] = a * da
                dv_acc[pl.ds(off, B), :] += _dot(a.astype(BF16), dov, TN)
                run = run + rs
            return run

        nfull = (i * B) // bk
        run = sweep1(nfull, jnp.zeros((B, 1), F32), True)
        lax.fori_loop(0, nfull, lambda jj, cr: sweep1(nfull - 1 - jj, cr, False), run)

        def sweep2(j, carry):
            pre, dq = carry
            parts = []
            for s in range(nsub):
                jb = j * nsub + s
                de = de_s[jb]
                hi, lo = _split2(de)
                parts.append((jb, de, _dot(hi, prefix) + _dot(lo, prefix), jnp.sum(de, axis=1, keepdims=True)))
            for jb, de, g, rs in parts:
                off = pl.multiple_of(jb * B, B)
                dz = (de - sg_s[jb].astype(F32) * (g + pre)).astype(BF16)
                dq = dq + _dot(dz, k_ref[pl.ds(off, B), :])
                dk_acc[pl.ds(off, B), :] += _dot(dz, q, TN)
                pre = pre + rs
            return pre, dq

        _, dq = lax.fori_loop(0, nfull + 1, sweep2, (jnp.zeros((B, 1), F32), jnp.zeros((B, HEAD_DIM), F32)))
        dq_ref[...] = (dq * SCALE).astype(dq_ref.dtype)

        @pl.when(i == nq - 1)
        def _():
            dk_ref[...] = (dk_acc[...] * SCALE).astype(dk_ref.dtype)
            dv_ref[...] = dv_acc[...].astype(dv_ref.dtype)

    return _call_with_carry(
        kern, name=name, grid=(H, nq), carry=carry,
        in_specs=[pl.BlockSpec((B, HEAD_DIM), lambda h, i: (i, 12 + h)),
                  pl.BlockSpec((T, HEAD_DIM), lambda h, i: (0, 16 + h)),
                  pl.BlockSpec((T, HEAD_DIM), lambda h, i: (0, 20 + h)),
                  pl.BlockSpec((B, HEAD_DIM), lambda h, i: (i, h))],
        out_specs=[pl.BlockSpec((B, HEAD_DIM), lambda h, i: (i, h)),
                   pl.BlockSpec((T, HEAD_DIM), lambda h, i: (0, h)),
                   pl.BlockSpec((T, HEAD_DIM), lambda h, i: (0, h))],
        out_shape=[jax.ShapeDtypeStruct((T, BRANCH_WIDTH), BF16)] * 3,
        scratch_shapes=[pltpu.VMEM((T, HEAD_DIM), F32), pltpu.VMEM((T, HEAD_DIM), F32),
                        pltpu.VMEM((T // B, B, B), F32), pltpu.VMEM((T // B, B, B), BF16)],
        args=(u_att, u_att, u_att, do))


def _rel_onehot(qrow):
    k = lax.broadcasted_iota(jnp.int32, (BAND, REL_PAD), 0)
    rr = lax.broadcasted_iota(jnp.int32, (BAND, REL_PAD), 1)
    idx = jnp.clip(PAD_ROWS + qrow - k, -(CHUNK - 1), REL_CLIP) + (CHUNK - 1)
    return jnp.where(idx == rr, 1.0, 0.0).astype(BF16)


def _band_bias(table, *, name):
    def kern(t_ref, o_ref):
        hi, mid, lo = _split3(t_ref[...])

        def body(qrow, _):
            oh = _rel_onehot(qrow)
            o_ref[qrow] = _dot(hi, oh, NT) + _dot(mid, oh, NT) + _dot(lo, oh, NT)
            return 0

        lax.fori_loop(0, CHUNK, body, 0)

    return pl.pallas_call(
        kern, name=name, out_shape=jax.ShapeDtypeStruct((CHUNK, 8, BAND), F32),
        compiler_params=_cp(),
    )(table)


def _band_bias_bwd(dbias, *, name):
    def kern(d_ref, o_ref):
        def body(qrow, acc):
            oh = _rel_onehot(qrow)
            hi, mid, lo = _split3(d_ref[qrow])
            return acc + _dot(hi, oh) + _dot(mid, oh) + _dot(lo, oh)

        o_ref[...] = lax.fori_loop(0, CHUNK, body, jnp.zeros((8, REL_PAD), F32))

    return pl.pallas_call(
        kern, name=name, out_shape=jax.ShapeDtypeStruct((8, REL_PAD), F32),
        compiler_params=_cp(),
    )(dbias)


def _chunk_rows(T):
    return _pick(T, (512, 256, 128, 64))


def _chunk_scores(q, kw, bias, c_global):
    s = _dot(q, kw, NT) * SCALE + bias
    col = lax.broadcasted_iota(jnp.int32, (CHUNK, BAND), 1)
    valid = (c_global * CHUNK + col) >= PAD_ROWS
    s = jnp.where(valid, s, -jnp.inf)
    m = jnp.max(s, axis=1, keepdims=True)
    e = jnp.exp(s - m)
    return e / jnp.sum(e, axis=1, keepdims=True)


def _chunk_fwd(u_att, bias, *, name, carry=None):
    T = u_att.shape[0]
    R = _chunk_rows(T)
    nr = T // R
    H = N_HEADS

    def kern(q_ref, k_ref, v_ref, b_ref, o_ref, kpad, vpad):
        i = pl.program_id(1)

        @pl.when(i == 0)
        def _():
            kpad[0:PAD_ROWS, :] = jnp.zeros((PAD_ROWS, HEAD_DIM), BF16)
            vpad[0:PAD_ROWS, :] = jnp.zeros((PAD_ROWS, HEAD_DIM), BF16)
            kpad[PAD_ROWS:, :] = k_ref[...]
            vpad[PAD_ROWS:, :] = v_ref[...]

        bias_v = b_ref[...]
        for cc in range(R // CHUNK):
            cg = i * (R // CHUNK) + cc
            off = pl.multiple_of(cg * CHUNK, CHUNK)
            q = q_ref[cc * CHUNK:(cc + 1) * CHUNK, :]
            kw = kpad[pl.ds(off, BAND), :]
            vw = vpad[pl.ds(off, BAND), :]
            p = _chunk_scores(q, kw, bias_v, cg)
            o_ref[cc * CHUNK:(cc + 1) * CHUNK, :] = _dot(p.astype(BF16), vw).astype(o_ref.dtype)

    return _call_with_carry(
        kern, name=name, grid=(H, nr), carry=carry,
        in_specs=[pl.BlockSpec((R, HEAD_DIM), lambda h, i: (i, 24 + h)),
                  pl.BlockSpec((T, HEAD_DIM), lambda h, i: (0, 28 + h)),
                  pl.BlockSpec((T, HEAD_DIM), lambda h, i: (0, 32 + h)),
                  pl.BlockSpec((None, CHUNK, BAND), lambda h, i: (h, 0, 0))],
        out_specs=[pl.BlockSpec((R, HEAD_DIM), lambda h, i: (i, h))],
        out_shape=[jax.ShapeDtypeStruct((T, BRANCH_WIDTH), BF16)],
        scratch_shapes=[pltpu.VMEM((T + PAD_ROWS, HEAD_DIM), BF16), pltpu.VMEM((T + PAD_ROWS, HEAD_DIM), BF16)],
        args=(u_att, u_att, u_att, bias))


def _chunk_bwd(u_att, bias, do, *, name, carry=None):
    T = u_att.shape[0]
    R = _chunk_rows(T)
    nr = T // R
    H = N_HEADS

    def kern(q_ref, k_ref, v_ref, b_ref, do_ref, dq_ref, dk_ref, dv_ref, db_ref, kpad, vpad, dkp, dvp):
        i = pl.program_id(1)

        @pl.when(i == 0)
        def _():
            kpad[0:PAD_ROWS, :] = jnp.zeros((PAD_ROWS, HEAD_DIM), BF16)
            vpad[0:PAD_ROWS, :] = jnp.zeros((PAD_ROWS, HEAD_DIM), BF16)
            kpad[PAD_ROWS:, :] = k_ref[...]
            vpad[PAD_ROWS:, :] = v_ref[...]
            dkp[...] = jnp.zeros_like(dkp)
            dvp[...] = jnp.zeros_like(dvp)
            db_ref[...] = jnp.zeros_like(db_ref)

        bias_v = b_ref[...]
        for cc in range(R // CHUNK):
            cg = i * (R // CHUNK) + cc
            off = pl.multiple_of(cg * CHUNK, CHUNK)
            q = q_ref[cc * CHUNK:(cc + 1) * CHUNK, :]
            dov = do_ref[cc * CHUNK:(cc + 1) * CHUNK, :]
            kw = kpad[pl.ds(off, BAND), :]
            vw = vpad[pl.ds(off, BAND), :]
            p = _chunk_scores(q, kw, bias_v, cg)
            dp = _dot(dov, vw, NT)
            ds = p * (dp - jnp.sum(p * dp, axis=1, keepdims=True))
            dsb = ds.astype(BF16)
            dq_ref[cc * CHUNK:(cc + 1) * CHUNK, :] = (_dot(dsb, kw) * SCALE).astype(dq_ref.dtype)
            dkp[pl.ds(off, BAND), :] += _dot(dsb, q, TN)
            dvp[pl.ds(off, BAND), :] += _dot(p.astype(BF16), dov, TN)
            db_ref[...] += ds

        @pl.when(i == nr - 1)
        def _():
            dk_ref[...] = (dkp[PAD_ROWS:, :] * SCALE).astype(dk_ref.dtype)
            dv_ref[...] = dvp[PAD_ROWS:, :].astype(dv_ref.dtype)

    return _call_with_carry(
        kern, name=name, grid=(H, nr), carry=carry,
        in_specs=[pl.BlockSpec((R, HEAD_DIM), lambda h, i: (i, 24 + h)),
                  pl.BlockSpec((T, HEAD_DIM), lambda h, i: (0, 28 + h)),
                  pl.BlockSpec((T, HEAD_DIM), lambda h, i: (0, 32 + h)),
                  pl.BlockSpec((None, CHUNK, BAND), lambda h, i: (h, 0, 0)),
                  pl.BlockSpec((R, HEAD_DIM), lambda h, i: (i, h))],
        out_specs=[pl.BlockSpec((R, HEAD_DIM), lambda h, i: (i, h)),
                   pl.BlockSpec((T, HEAD_DIM), lambda h, i: (0, h)),
                   pl.BlockSpec((T, HEAD_DIM), lambda h, i: (0, h)),
                   pl.BlockSpec((None, CHUNK, BAND), lambda h, i: (h, 0, 0))],
        out_shape=[jax.ShapeDtypeStruct((T, BRANCH_WIDTH), BF16)] * 3
                  + [jax.ShapeDtypeStruct((H, CHUNK, BAND), F32)],
        scratch_shapes=[pltpu.VMEM((T + PAD_ROWS, HEAD_DIM), BF16), pltpu.VMEM((T + PAD_ROWS, HEAD_DIM), BF16),
                        pltpu.VMEM((T + PAD_ROWS, HEAD_DIM), F32), pltpu.VMEM((T + PAD_ROWS, HEAD_DIM), F32)],
        args=(u_att, u_att, u_att, bias, do))


LRU_ROWS = 256
HALO = 8


def _gelu(y):
    k0 = math.sqrt(2.0 / math.pi)
    t = jnp.tanh(k0 * (y + 0.044715 * y * y * y))
    return 0.5 * y * (1.0 + t), t


def _gelu_grad(y, t):
    k0 = math.sqrt(2.0 / math.pi)
    return 0.5 * (1.0 + t) + 0.5 * y * (1.0 - t * t) * k0 * (1.0 + 3.0 * 0.044715 * y * y)


def _neg_expm1(y):
    poly = -y * (1.0 + y * (1.0 / 2 + y * (1.0 / 6 + y * (1.0 / 24 + y * (1.0 / 120 + y * (1.0 / 720 + y * (1.0 / 5040)))))))
    return jnp.where(y > -0.5, poly, 1.0 - jnp.exp(y))


def _lru_gates(ext, cw_ref, cb_ref, wr_ref, br_ref, wi_ref, bi_ref, lam_ref, rows):
    xc = cb_ref[...] + jnp.zeros((rows, BRANCH_WIDTH), F32)
    for j in range(CONV_WIDTH):
        xc = xc + ext[pl.ds(HALO - (CONV_WIDTH - 1) + j, rows), :] * cw_ref[j:j + 1, :]
    xcb = xc.astype(BF16)
    zr = jnp.concatenate([_dot(xcb[:, n * 128:(n + 1) * 128], wr_ref[n]) for n in range(4)], axis=1) + br_ref[...]
    zi = jnp.concatenate([_dot(xcb[:, n * 128:(n + 1) * 128], wi_ref[n]) for n in range(4)], axis=1) + bi_ref[...]
    r = _sigmoid(zr)
    gi = _sigmoid(zi)
    ls = _log_sigmoid(lam_ref[...])
    la = LRU_C * r * ls
    a = jnp.exp(la)
    mult = jnp.sqrt(_neg_expm1(2.0 * la))
    return xc, xcb, r, gi, ls, a, mult


def _lru_param_specs():
    full2 = lambda s: pl.BlockSpec(s, lambda i: (0, 0))
    full3 = lambda s: pl.BlockSpec(s, lambda i: (0, 0, 0))
    return [full2((8, BRANCH_WIDTH)), full2((1, BRANCH_WIDTH)), full3((4, 128, 128)), full2((1, BRANCH_WIDTH)),
            full3((4, 128, 128)), full2((1, BRANCH_WIDTH)), full2((1, BRANCH_WIDTH))]


def _lru_fwd(u_rec, p, *, name):
    T = u_rec.shape[0]
    R = min(LRU_ROWS, T)
    nb = T // R
    W = BRANCH_WIDTH
    hb = R // HALO

    def kern(rx_ref, halo_ref, ry_ref, cw_ref, cb_ref, wr_ref, br_ref, wi_ref, bi_ref, lam_ref,
             o_ref, h_ref, ext, a_s, b_s, hc):
        i = pl.program_id(0)

        @pl.when(i == 0)
        def _():
            hc[...] = jnp.zeros_like(hc)

        ext[0:HALO, :] = jnp.where(i == 0, 0.0, halo_ref[...])
        ext[HALO:, :] = rx_ref[...]
        xc, _, r, gi, ls, a, mult = _lru_gates(ext, cw_ref, cb_ref, wr_ref, br_ref, wi_ref, bi_ref, lam_ref, R)
        a_s[...] = a
        b_s[...] = mult * (gi * xc)

        def body(t, h):
            h = a_s[pl.ds(t, 1), :] * h + b_s[pl.ds(t, 1), :]
            h_ref[pl.ds(t, 1), :] = h
            return h

        h = lax.fori_loop(0, R, body, hc[0:1, :], unroll=8)
        hc[...] = jnp.broadcast_to(h, hc.shape)
        g, _ = _gelu(ry_ref[...])
        o_ref[...] = (h_ref[...] * g).astype(o_ref.dtype)

    return pl.pallas_call(
        kern, name=name, grid=(nb,),
        in_specs=[pl.BlockSpec((R, W), lambda i: (i, 0)),
                  pl.BlockSpec((HALO, W), lambda i: (jnp.maximum(i * hb - 1, 0), 0)),
                  pl.BlockSpec((R, W), lambda i: (i, 1))] + _lru_param_specs(),
        out_specs=[pl.BlockSpec((R, W), lambda i: (i, 0)), pl.BlockSpec((R, W), lambda i: (i, 0))],
        out_shape=[jax.ShapeDtypeStruct((T, W), BF16), jax.ShapeDtypeStruct((T, W), F32)],
        scratch_shapes=[pltpu.VMEM((R + HALO, W), F32), pltpu.VMEM((R, W), F32), pltpu.VMEM((R, W), F32),
                        pltpu.VMEM((8, W), F32)],
        compiler_params=_cp(("arbitrary",)),
    )(u_rec, u_rec, u_rec, *p)


def _lru_bwd(u_rec, hs, do, p, *, name):
    T = u_rec.shape[0]
    R = min(LRU_ROWS, T)
    nb = T // R
    W = BRANCH_WIDTH
    hb = R // HALO

    def kern(rx_ref, halo_ref, ry_ref, h_ref, hh_ref, do_ref, cw_ref, cb_ref, wr_ref, br_ref, wi_ref, bi_ref, lam_ref,
             drx_ref, dry_ref, dcw_ref, dcb_ref, dwr_ref, dbr_ref, dwi_ref, dbi_ref, dlam_ref,
             ext, hext, a_s, g_s, dext, gc):
        s = pl.program_id(0)
        first_block = s == nb - 1

        @pl.when(s == 0)
        def _():
            gc[...] = jnp.zeros_like(gc)
            dext[R:, :] = jnp.zeros((HALO, W), F32)
            for ref in (dcw_ref, dcb_ref, dwr_ref, dbr_ref, dwi_ref, dbi_ref, dlam_ref):
                ref[...] = jnp.zeros_like(ref)

        ext[0:HALO, :] = jnp.where(first_block, 0.0, halo_ref[...])
        ext[HALO:, :] = rx_ref[...]
        hext[0:HALO, :] = jnp.where(first_block, 0.0, hh_ref[...])
        hext[HALO:, :] = h_ref[...]
        xc, xcb, r, gi, ls, a, mult = _lru_gates(ext, cw_ref, cb_ref, wr_ref, br_ref, wi_ref, bi_ref, lam_ref, R)
        ry = ry_ref[...]
        gel, th = _gelu(ry)
        dov = do_ref[...].astype(F32)
        dry_ref[...] = (dov * h_ref[...] * _gelu_grad(ry, th)).astype(dry_ref.dtype)
        a_s[...] = a
        g_s[...] = dov * gel

        def body(tt, g):
            t = R - 1 - tt
            dh = g_s[pl.ds(t, 1), :] + g
            g_s[pl.ds(t, 1), :] = dh
            return a_s[pl.ds(t, 1), :] * dh

        g = lax.fori_loop(0, R, body, gc[0:1, :], unroll=8)
        gc[...] = jnp.broadcast_to(g, gc.shape)
        dh = g_s[...]
        hprev = hext[pl.ds(HALO - 1, R), :]
        da = dh * hprev
        gx = gi * xc
        dmult = dh * gx
        dgx = dh * mult
        dgi = dgx * xc
        dxc = dgx * gi
        dla = da * a - dmult * (a * a) / mult
        dr = dla * (LRU_C * ls)
        dlam_ref[...] += jnp.sum(dla * (LRU_C * r), axis=0, keepdims=True)
        dzr = dr * r * (1.0 - r)
        dzi = dgi * gi * (1.0 - gi)
        dbr_ref[...] += jnp.sum(dzr, axis=0, keepdims=True)
        dbi_ref[...] += jnp.sum(dzi, axis=0, keepdims=True)
        dzrb = dzr.astype(BF16)
        dzib = dzi.astype(BF16)
        back = []
        for n in range(4):
            sl = slice(n * 128, (n + 1) * 128)
            dwr_ref[n] += _dot(xcb[:, sl], dzrb[:, sl], TN)
            dwi_ref[n] += _dot(xcb[:, sl], dzib[:, sl], TN)
            back.append(_dot(dzrb[:, sl], wr_ref[n], NT) + _dot(dzib[:, sl], wi_ref[n], NT))
        dxc = dxc + jnp.concatenate(back, axis=1)
        dcb_ref[...] += jnp.sum(dxc, axis=0, keepdims=True)
        for j in range(CONV_WIDTH):
            dcw_ref[j:j + 1, :] += jnp.sum(dxc * ext[pl.ds(HALO - (CONV_WIDTH - 1) + j, R), :], axis=0, keepdims=True)
        dext[0:R, :] = dxc
        drx = jnp.zeros((R, W), F32)
        for j in range(CONV_WIDTH):
            drx = drx + dext[pl.ds(CONV_WIDTH - 1 - j, R), :] * cw_ref[j:j + 1, :]
        drx_ref[...] = drx.astype(drx_ref.dtype)
        dext[R:, :] = dxc[0:HALO, :]

        @pl.when(s == nb - 1)
        def _():
            dlam_ref[...] = dlam_ref[...] * _sigmoid(-lam_ref[...])

    rev = lambda c: pl.BlockSpec((R, W), lambda s: (nb - 1 - s, c))
    halo = lambda: pl.BlockSpec((HALO, W), lambda s: (jnp.maximum((nb - 1 - s) * hb - 1, 0), 0))
    v2 = lambda shp: pl.BlockSpec(shp, lambda s: (0, 0))
    v3 = lambda shp: pl.BlockSpec(shp, lambda s: (0, 0, 0))
    return pl.pallas_call(
        kern, name=name, grid=(nb,),
        in_specs=[rev(0), halo(), rev(1), rev(0), halo(), rev(0)] + _lru_param_specs(),
        out_specs=[rev(0), rev(0), v2((8, W)), v2((1, W)), v3((4, 128, 128)), v2((1, W)), v3((4, 128, 128)),
                   v2((1, W)), v2((1, W))],
        out_shape=[jax.ShapeDtypeStruct((T, W), BF16), jax.ShapeDtypeStruct((T, W), BF16),
                   jax.ShapeDtypeStruct((8, W), F32), jax.ShapeDtypeStruct((1, W), F32),
                   jax.ShapeDtypeStruct((4, 128, 128), F32), jax.ShapeDtypeStruct((1, W), F32),
                   jax.ShapeDtypeStruct((4, 128, 128), F32), jax.ShapeDtypeStruct((1, W), F32),
                   jax.ShapeDtypeStruct((1, W), F32)],
        scratch_shapes=[pltpu.VMEM((R + HALO, W), F32), pltpu.VMEM((R + HALO, W), F32), pltpu.VMEM((R, W), F32),
                        pltpu.VMEM((R, W), F32), pltpu.VMEM((R + HALO, W), F32), pltpu.VMEM((8, W), F32)],
        compiler_params=_cp(("arbitrary",)),
    )(u_rec, u_rec, u_rec, hs, hs, do, *p)


def _merge_fwd(o_all, wb, gate, *, name):
    T = o_all.shape[1]
    D = D_MODEL
    bm = _pick(T, (1024, 512, 256, 128))
    bn = 1024
    nj = D // bn

    def kern(o_ref, w_ref, g_ref, m_ref, pb_ref, acc):
        g = pl.program_id(2)
        pbv = _dot(o_ref[...], w_ref[...])
        pb_ref[...] = pbv.astype(pb_ref.dtype)
        term = g_ref[...].astype(F32) * pbv

        @pl.when(g == 0)
        def _():
            acc[...] = term

        @pl.when(g > 0)
        def _():
            acc[...] += term

        @pl.when(g == N_BRANCH - 1)
        def _():
            m_ref[...] = acc[...].astype(m_ref.dtype)

    return pl.pallas_call(
        kern, name=name, grid=(T // bm, nj, N_BRANCH),
        in_specs=[pl.BlockSpec((None, bm, BRANCH_WIDTH), lambda i, j, g: (g, i, 0)),
                  pl.BlockSpec((None, BRANCH_WIDTH, bn), lambda i, j, g: (g, 0, j)),
                  pl.BlockSpec((bm, bn), lambda i, j, g: (i, g * nj + j))],
        out_specs=[pl.BlockSpec((bm, bn), lambda i, j, g: (i, j)),
                   pl.BlockSpec((bm, bn), lambda i, j, g: (i, g * nj + j))],
        out_shape=[jax.ShapeDtypeStruct((T, D), BF16), jax.ShapeDtypeStruct((T, N_BRANCH * D), BF16)],
        scratch_shapes=[pltpu.VMEM((bm, bn), F32)],
        compiler_params=_cp(("parallel", "parallel", "arbitrary")),
    )(o_all, wb, gate)


def _merge_bwd(dm, gate, pb, *, name):
    T = dm.shape[0]
    D = D_MODEL
    bt = _pick(T, (256, 128))

    def kern(dm_ref, g_ref, pb_ref, dpb_ref, dzg_ref, dbg_ref):
        i = pl.program_id(1)
        dmv = dm_ref[...]
        gv = g_ref[...].astype(F32)
        dpb_ref[...] = (dmv * gv).astype(dpb_ref.dtype)
        dzg = dmv * pb_ref[...].astype(F32) * gv * (1.0 - gv)
        dzg_ref[...] = dzg.astype(dzg_ref.dtype)
        part = jnp.sum(dzg, axis=0, keepdims=True)

        @pl.when(i == 0)
        def _():
            dbg_ref[...] = part

        @pl.when(i > 0)
        def _():
            dbg_ref[...] += part

    return pl.pallas_call(
        kern, name=name, grid=(N_BRANCH, T // bt),
        in_specs=[pl.BlockSpec((bt, D), lambda g, i: (i, 0)),
                  pl.BlockSpec((bt, D), lambda g, i: (i, g)),
                  pl.BlockSpec((bt, D), lambda g, i: (i, g))],
        out_specs=[pl.BlockSpec((None, bt, D), lambda g, i: (g, i, 0)),
                   pl.BlockSpec((bt, D), lambda g, i: (i, g)),
                   pl.BlockSpec((1, D), lambda g, i: (0, g))],
        out_shape=[jax.ShapeDtypeStruct((N_BRANCH, T, D), BF16), jax.ShapeDtypeStruct((T, N_BRANCH * D), BF16),
                   jax.ShapeDtypeStruct((1, N_BRANCH * D), F32)],
        compiler_params=_cp(("parallel", "arbitrary")),
    )(dm, gate, pb)


def _col_split(M, N, bm, bn):
    per = N // N_CHIPS // bn
    return (N_CHIPS, M, N // N_CHIPS), (None, bm, bn), lambda i, j: (j // per, i, j % per)


def _pad_lanes(v, n):
    return jnp.pad(v, [(0, 0)] * (v.ndim - 1) + [(0, n - v.shape[-1])])


def _rows8(v):
    return jnp.pad(v, ((0, 8 - v.shape[0]), (0, 0)))


def _device_step(x, tgt, W, hooks=None):
    hooks = hooks or {}

    def carried(fn, key, *args, **kw):
        hook = hooks.get(key)
        outs, extra = fn(*args, name=key[0], carry=hook.spec(W, G) if hook else None, **kw)
        if hook:
            hook.done(extra, W, G)
        return outs

    T = x.shape[0]
    _, bk = _att_blocks(T)
    H = N_HEADS
    G = {}
    saved = []

    xf, xb = _ln_fwd(x, W['ln_in_g'], W['ln_in_b'], name='ln_in_fwd')
    for l in range(DEPTH):
        w_att, w_rec = W['w_att'][l], W['w_rec'][l]
        u_att = _mm(xb, w_att, name='in_proj_att', out_dtypes=(BF16,))
        u_rec = _mm(xb, w_rec, name='in_proj_rec', out_dtypes=(F32,))
        ffl = u_rec[:, 2 * BRANCH_WIDTH:]
        bf = _pad_lanes(W['b_forget'][l].reshape(1, H), LANES)
        Fc = _forget_fwd(ffl, bf, name='forget_fwd')
        Fh = Fc[:, :H].T
        frow = Fh.reshape(H, T // bk, 1, bk)
        o_fox, lse = carried(_fox_fwd, ('fox_fwd', l), u_att, Fc, frow)
        lp = (_rows8(W['conv_w'][l]), W['conv_b'][l].reshape(1, -1), W['w_r'][l].astype(BF16),
              W['b_r'][l].reshape(1, -1), W['w_i'][l].astype(BF16), W['b_i'][l].reshape(1, -1),
              W['lru_lambda'][l].reshape(1, -1))
        o_lru, hs = _lru_fwd(u_rec, lp, name='lru_fwd')
        o_sb, = carried(_sb_fwd, ('sb_fwd', l), u_att)
        table = _rows8(_pad_lanes(W['rel_bias'][l], REL_PAD))
        bias = _band_bias(table, name='band_bias').transpose(1, 0, 2)[:H]
        o_ch, = carried(_chunk_fwd, ('chunk_fwd', l), u_att, bias)
        o_all = jnp.stack([o_fox, o_lru, o_sb, o_ch])
        gate = _mm(xb, W['w_gate_cat'][l], name='gate_proj', out_dtypes=(BF16,),
                   extras=[(W['b_gate'][l].reshape(1, -1), 'n')],
                   epilogue=lambda acc, b: (_sigmoid(acc + b),))
        merged, pb = _merge_fwd(o_all, W['w_branch'][l], gate, name='merge_fwd')
        h1 = _mm(merged, W['w_out'][l], name='out_proj', extras=[(xf, 'mn')],
                 epilogue=lambda acc, xr: (ALPHA * xr + acc,))
        xmf, xmb = _ln_fwd(h1, W['ln1_g'][l], W['ln1_b'][l], name='ln_fwd')
        hid, ra = _mm(xmb, W['w_ff1'], name='ff1', out_dtypes=(BF16, BF16), bn=1024, bk=FF_SHARD,
                      b_view=(D_MODEL, D_FF, (None, None, FF_SHARD, 1024), lambda i, j, k: (j // 2, l, 0, j % 2)),
                      epilogue=lambda acc: (jnp.square(jnp.maximum(acc, 0.0)), jnp.maximum(acc, 0.0)))
        h2 = _mm(hid, W['w_ff2'], name='ff2', extras=[(xmf, 'mn')], bn=1024, bk=FF_SHARD,
                 b_view=(D_FF, D_MODEL, (None, None, FF_SHARD, 1024), lambda i, j, k: (k, l, 0, j)),
                 epilogue=lambda acc, xr: (ALPHA * xr + acc,))
        saved.append(dict(xb=xb, u_att=u_att, u_rec=u_rec, ffl=ffl, bf=bf, fcum=Fc, frow=frow, lse=lse, lp=lp,
                          hs=hs, bias=bias, o_all=o_all, gate=gate, merged=merged, pb=pb, h1=h1, xmb=xmb,
                          hid=hid, ra=ra, h2=h2))
        xf, xb = _ln_fwd(h2, W['ln2_g'][l], W['ln2_b'][l], name='ln_fwd')

    dx, loss_tile = _loss_head(xf, tgt, name='loss_head')
    loss = loss_tile[0, 0]

    for l in reversed(range(DEPTH)):
        S = saved[l]
        dh2, dh2b, G[('ln2_g', l)], G[('ln2_b', l)] = _ln_bwd(S['h2'], dx, W['ln2_g'][l], name='ln_bwd')
        da = _mm(dh2b, W['w_ff2'], tb=True, name='ff2_dx', out_dtypes=(BF16,), extras=[(S['ra'], 'mn')],
                 bn=1024, bk=FF_SHARD,
                 b_view=(D_MODEL, D_FF, (None, None, 1024, FF_SHARD), lambda i, j, k: (j // 2, l, j % 2, 0)),
                 epilogue=lambda acc, rav: (acc * (2.0 * rav.astype(F32)),))
        G[('w_ff2', l)] = _mm(S['hid'], dh2b, ta=True, name='ff2_dw').reshape(N_CHIPS, D_FF // N_CHIPS, D_MODEL)
        G[('w_ff1', l)] = _mm(S['xmb'], da, ta=True, name='ff1_dw', bm=1024, bn=1024,
                              out_map=_col_split(D_MODEL, D_FF, 1024, 1024))
        dxm = _mm(da, W['w_ff1'], tb=True, name='ff1_dx', extras=[(dh2, 'mn')], bn=1024, bk=FF_SHARD,
                  b_view=(D_FF, D_MODEL, (None, None, 1024, FF_SHARD), lambda i, j, k: (k, l, j, 0)),
                  epilogue=lambda acc, d: (ALPHA * d + acc,))
        dh1, dh1b, G[('ln1_g', l)], G[('ln1_b', l)] = _ln_bwd(S['h1'], dxm, W['ln1_g'][l], name='ln_bwd')
        dm = _mm(dh1b, W['w_out'][l], tb=True, name='out_dx')
        G[('w_out', l)] = _mm(S['merged'], dh1b, ta=True, name='out_dw').reshape(
            N_CHIPS, D_MODEL // N_CHIPS, D_MODEL)
        dpb, dzg, G[('b_gate', l)] = _merge_bwd(dm, S['gate'], S['pb'], name='merge_bwd')
        do = [_mm(dpb[g], W['w_branch'][l][g], tb=True, name='branch_dx', out_dtypes=(BF16,)) for g in range(N_BRANCH)]
        G[('w_branch', l)] = jnp.stack(
            [_mm(S['o_all'][g], dpb[g], ta=True, name='branch_dw', bm=BRANCH_WIDTH, bn=BRANCH_WIDTH,
                 out_map=_col_split(BRANCH_WIDTH, D_MODEL, BRANCH_WIDTH, BRANCH_WIDTH))
             for g in range(N_BRANCH)], axis=1)
        G[('w_gate', l)] = _mm(S['xb'], dzg, ta=True, name='gate_dw', bm=1024, bn=1024,
                               out_map=((N_CHIPS, N_BRANCH, D_MODEL // N_CHIPS, D_MODEL),
                                        (2, None, D_MODEL // N_CHIPS, 1024),
                                        lambda i, j: (i, j // 2, 0, j % 2)))
        u_att, u_rec = S['u_att'], S['u_rec']
        delta = _row_dot(do[0], S['o_all'][0], name='row_dot')
        fdq, fdk, fdv, dfk, dfq = carried(_fox_bwd, ('fox_bwd', l), u_att, do[0], S['lse'], delta, S['fcum'],
                                          S['frow'])
        dff, dbf = _forget_bwd(_pad_lanes(dfk.reshape(H, T).T, LANES), _pad_lanes(dfq.reshape(H, T).T, LANES),
                               S['ffl'], S['bf'], name='forget_bwd')
        G[('b_forget', l)] = dbf[0, :H]
        (drx, dry, dcw, dcb, G[('w_r', l)], dbr, G[('w_i', l)], dbi, dlam) = _lru_bwd(
            u_rec, S['hs'], do[1], S['lp'], name='lru_bwd')
        G[('conv_w', l)], G[('conv_b', l)] = dcw[:CONV_WIDTH], dcb[0]
        G[('b_r', l)], G[('b_i', l)], G[('lru_lambda', l)] = dbr[0], dbi[0], dlam[0]
        sdq, sdk, sdv = carried(_sb_bwd, ('sb_bwd', l), u_att, do[2])
        cdq, cdk, cdv, dbias = carried(_chunk_bwd, ('chunk_bwd', l), u_att, S['bias'], do[3])
        dtab = _band_bias_bwd(jnp.pad(dbias, ((0, 8 - H), (0, 0), (0, 0))).transpose(1, 0, 2), name='band_bias_bwd')
        G[('rel_bias', l)] = dtab[:H, :REL_TABLE]
        du_att = jnp.concatenate([fdq, fdk, fdv, sdq, sdk, sdv, cdq, cdk, cdv], axis=1)
        du_rec = jnp.concatenate([drx, dry, dff], axis=1)
        G[('w_att', l)] = _mm(S['xb'], du_att, ta=True, name='in_att_dw')
        G[('w_rec', l)] = _mm(S['xb'], du_rec, ta=True, name='in_rec_dw')
        t1 = _mm(dzg, W['w_gate_cat'][l], tb=True, name='gate_dx', extras=[(dh1, 'mn')],
                 epilogue=lambda acc, d: (ALPHA * d + acc,))
        t2 = _mm(du_att, W['w_att'][l], tb=True, name='in_att_dx', extras=[(t1, 'mn')],
                 epilogue=lambda acc, d: (d + acc,))
        dx = _mm(du_rec, W['w_rec'][l], tb=True, name='in_rec_dx', extras=[(t2, 'mn')],
                 epilogue=lambda acc, d: (d + acc,))

    gx, _, G[('ln_in_g', -1)], G[('ln_in_b', -1)] = _ln_bwd(x, dx, W['ln_in_g'], name='ln_in_bwd')
    return loss, gx, G


_IN_FQKV = (0, 1536)
_IN_FF = (1536, 1540)
_IN_REC = (1540, 2564)
_IN_REST = (2564, D_IN)


def _prep_weights(full, W=None):
    W = {} if W is None else W
    for n, a in full.items():
        if n == 'w_in':
            L = a.shape[0]
            W['w_att'] = jnp.concatenate([a[..., _IN_FQKV[0]:_IN_FQKV[1]], a[..., _IN_REST[0]:_IN_REST[1]]],
                                         -1).astype(BF16)
            W['w_rec'] = jnp.concatenate([a[..., _IN_REC[0]:_IN_REC[1]], a[..., _IN_FF[0]:_IN_FF[1]],
                                          jnp.zeros((L, D_MODEL, N_REC - 1024 - N_HEADS), a.dtype)], -1).astype(BF16)
        elif n == 'w_gate':
            W['w_gate_cat'] = a.transpose(0, 2, 1, 3).reshape(a.shape[0], D_MODEL, N_BRANCH * D_MODEL).astype(BF16)
        elif n == 'b_gate':
            W['b_gate'] = a.reshape(a.shape[0], N_BRANCH * D_MODEL)
        elif n == 'w_ff1' and a.ndim == 3:
            W[n] = a.reshape(a.shape[0], D_MODEL, 4, FF_SHARD).transpose(2, 0, 1, 3).astype(BF16)
        elif n == 'w_ff2' and a.ndim == 3:
            W[n] = a.reshape(a.shape[0], 4, FF_SHARD, D_MODEL).transpose(1, 0, 2, 3).astype(BF16)
        elif n in ('w_branch', 'w_out', 'w_ff1', 'w_ff2'):
            W[n] = a.astype(BF16)
        else:
            W[n] = a
    return W


def _grads_to_reference_layout(G):
    out = {'ln_in_g': G[('ln_in_g', -1)][0], 'ln_in_b': G[('ln_in_b', -1)][0]}
    st = lambda n: jnp.stack([G[(n, l)] for l in range(DEPTH)])
    g_att, g_rec = st('w_att'), st('w_rec')
    out['w_in'] = jnp.concatenate([g_att[..., :1536], g_rec[..., 1024:1024 + N_HEADS], g_rec[..., :1024],
                                   g_att[..., 1536:]], -1)
    out['w_gate'] = st('w_gate').transpose(0, 2, 1, 3, 4).reshape(DEPTH, N_BRANCH, D_MODEL, D_MODEL)
    out['w_branch'] = st('w_branch').transpose(0, 2, 3, 1, 4).reshape(DEPTH, N_BRANCH, BRANCH_WIDTH, D_MODEL)
    out['w_ff1'] = st('w_ff1').transpose(0, 2, 1, 3).reshape(DEPTH, D_MODEL, D_FF)
    out['w_ff2'] = st('w_ff2').reshape(DEPTH, D_FF, D_MODEL)
    out['w_out'] = st('w_out').reshape(DEPTH, D_MODEL, D_MODEL)
    out['b_gate'] = st('b_gate').reshape(DEPTH, N_BRANCH, D_MODEL)
    for n in ('ln1_g', 'ln1_b', 'ln2_g', 'ln2_b'):
        out[n] = st(n)[:, 0]
    for n in ('b_forget', 'conv_w', 'conv_b', 'w_r', 'b_r', 'w_i', 'b_i', 'lru_lambda', 'rel_bias'):
        out[n] = st(n)
    return out


HBM_SPEC = pl.BlockSpec(memory_space=pl.ANY)
N_CHIPS = 4
PACK_COLS = 1024


def _place():
    x, y, c = lax.axis_index("x"), lax.axis_index("y"), lax.axis_index("c")
    chips = [(1 - x, y), (x, 1 - y), (1 - x, 1 - y)]
    return x, y, c, chips


def _remote(src, dst, send_sems, recv_sems, k, to):
    return pltpu.make_async_remote_copy(src_ref=src, dst_ref=dst, send_sem=send_sems.at[k], recv_sem=recv_sems.at[k],
                                        device_id=to, device_id_type=MESH)


class _Exchange:
    def __init__(self, ins, out_shapes, n_sems, start, finish, mid=None):
        self.ins, self.out_shapes, self.n_sems = list(ins), list(out_shapes), n_sems
        self.start, self.mid, self.finish = start, mid, finish


def _gather_spec(params):
    n = len(params)

    def start(ins, outs, ss, rs):
        x, y, c, chips = _place()
        for p in range(n):
            _remote(ins[p], outs[p].at[2 * x + y], ss, rs, 6 * n + p, (x, y, 1 - c)).start()
            for j, (cx, cy) in enumerate(chips):
                _remote(ins[p].at[c], outs[p].at[2 * x + y, c], ss, rs, 6 * p + j, (cx, cy, c)).start()

    def mid(ins, outs, ss, rs):
        x, y, c, chips = _place()
        for p in range(n):
            for j, (cx, cy) in enumerate(chips):
                blk = outs[p].at[2 * cx + cy, c]
                _remote(blk, blk, ss, rs, 6 * p + j, (x, y, c)).wait_recv()
                _remote(blk, blk, ss, rs, 6 * p + 3 + j, (x, y, 1 - c)).start()

    def finish(ins, outs, ss, rs):
        x, y, c, chips = _place()
        me = (x, y, c)
        for p in range(n):
            for j, (cx, cy) in enumerate(chips):
                theirs = outs[p].at[2 * cx + cy, 1 - c]
                _remote(theirs, theirs, ss, rs, 6 * p + 3 + j, me).wait_recv()
        for p in range(n):
            for j, (cx, cy) in enumerate(chips):
                _remote(ins[p].at[c], outs[p].at[2 * x + y, c], ss, rs, 6 * p + j, me).wait_send()
                blk = outs[p].at[2 * cx + cy, c]
                _remote(blk, blk, ss, rs, 6 * p + 3 + j, me).wait_send()
            _remote(ins[p], outs[p].at[2 * x + y], ss, rs, 6 * n + p, me).wait()

    shapes = [jax.ShapeDtypeStruct((N_CHIPS,) + a.shape, a.dtype) for a in params]
    return _Exchange(params, shapes, 7 * n, start, finish, mid)


def _pair_spec(g0, g1):
    n = len(g0)

    def start(ins, outs, ss, rs):
        x, y, c, _ = _place()

        @pl.when(c == 0)
        def _():
            for p in range(n):
                _remote(ins[n + p], outs[p], ss, rs, p, (x, y, 1 - c)).start()

        @pl.when(c == 1)
        def _():
            for p in range(n):
                _remote(ins[p], outs[p], ss, rs, p, (x, y, 1 - c)).start()

    def finish(ins, outs, ss, rs):
        x, y, c, _ = _place()
        for p in range(n):
            _remote(ins[p], outs[p], ss, rs, p, (x, y, 1 - c)).wait()

    return _Exchange(list(g0) + list(g1), [jax.ShapeDtypeStruct(a.shape, a.dtype) for a in g0], n, start, finish)


def _chip_spec(s):
    n = len(s)

    def start(ins, outs, ss, rs):
        x, y, c, chips = _place()
        for p in range(n):
            for j, (cx, cy) in enumerate(chips):
                _remote(ins[p].at[2 * cx + cy], outs[p].at[2 * x + y], ss, rs, 3 * p + j, (cx, cy, c)).start()

    def finish(ins, outs, ss, rs):
        x, y, c, chips = _place()
        for p in range(n):
            for j, (cx, cy) in enumerate(chips):
                slot = outs[p].at[2 * cx + cy]
                _remote(slot, slot, ss, rs, 3 * p + j, (x, y, c)).wait_recv()
        for p in range(n):
            for j, (cx, cy) in enumerate(chips):
                _remote(ins[p].at[2 * cx + cy], outs[p].at[2 * x + y], ss, rs, 3 * p + j, (x, y, c)).wait_send()

    return _Exchange(s, [jax.ShapeDtypeStruct(a.shape, a.dtype) for a in s], 3 * n, start, finish)


def _exchange(ex, *, name):
    ni, no = len(ex.ins), len(ex.out_shapes)

    def body(*refs):
        ins, outs = refs[:ni], refs[ni:ni + no]
        ss, rs = refs[ni + no:]
        ex.start(ins, outs, ss, rs)
        if ex.mid is not None:
            ex.mid(ins, outs, ss, rs)
        ex.finish(ins, outs, ss, rs)

    return list(pl.pallas_call(
        body, name=name, in_specs=[HBM_SPEC] * ni, out_specs=[HBM_SPEC] * no, out_shape=ex.out_shapes,
        scratch_shapes=[pltpu.SemaphoreType.DMA((ex.n_sems,)), pltpu.SemaphoreType.DMA((ex.n_sems,))],
    )(*ex.ins))


def _call_with_carry(kern, *, name, grid, in_specs, out_specs, out_shape, scratch_shapes, args, carry=None):
    out_specs, out_shape = list(out_specs), list(out_shape)
    if carry is None:
        res = pl.pallas_call(kern, name=name, grid=grid, in_specs=in_specs, out_specs=out_specs, out_shape=out_shape,
                             scratch_shapes=scratch_shapes, compiler_params=_cp(("parallel", "arbitrary")))(*args)
        return list(res), []
    ni, no, ns = len(in_specs), len(out_specs), len(scratch_shapes)
    ci, co = len(carry.ins), len(carry.out_shapes)

    def wrapped(*refs):
        ins, cins = refs[:ni], refs[ni:ni + ci]
        outs, couts = refs[ni + ci:ni + ci + no], refs[ni + ci + no:ni + ci + no + co]
        scratch = refs[ni + ci + no + co:ni + ci + no + co + ns]
        ss, rs = refs[-2:]
        row, col = pl.program_id(0), pl.program_id(1)

        @pl.when((row == 0) & (col == 0))
        def _():
            carry.start(cins, couts, ss, rs)

        kern(*ins, *outs, *scratch)

        if carry.mid is not None:
            @pl.when((row == grid[0] - 1) & (col == grid[1] // 2))
            def _():
                carry.mid(cins, couts, ss, rs)

        @pl.when((row == grid[0] - 1) & (col == grid[1] - 1))
        def _():
            carry.finish(cins, couts, ss, rs)

    res = pl.pallas_call(
        wrapped, name=name, grid=grid, in_specs=list(in_specs) + [HBM_SPEC] * ci,
        out_specs=out_specs + [HBM_SPEC] * co, out_shape=out_shape + carry.out_shapes,
        scratch_shapes=list(scratch_shapes) + [pltpu.SemaphoreType.DMA((carry.n_sems,)),
                                               pltpu.SemaphoreType.DMA((carry.n_sems,))],
        compiler_params=_cp(("arbitrary", "arbitrary")))(*args, *carry.ins)
    return list(res[:no]), list(res[no:])


def _pair_swap(r, *, name):
    n = len(r)

    def body(*refs):
        ins, outs = refs[:n], refs[n:2 * n]
        send_sems, recv_sems = refs[2 * n:]
        x, y, c, _ = _place()
        cps = [_remote(ins[p], outs[p], send_sems, recv_sems, p, (x, y, 1 - c)) for p in range(n)]
        for cp in cps:
            cp.start()
        for cp in cps:
            cp.wait()

    return pl.pallas_call(
        body, name=name, in_specs=[HBM_SPEC] * n, out_specs=[HBM_SPEC] * n,
        out_shape=[jax.ShapeDtypeStruct(a.shape, a.dtype) for a in r],
        scratch_shapes=[pltpu.SemaphoreType.DMA((n,)), pltpu.SemaphoreType.DMA((n,))],
    )(*r)


def _gather8_spec(v):
    R, C = v.shape
    flips = [(bx, by, bc) for bx in (0, 1) for by in (0, 1) for bc in (0, 1)][1:]
    flip = lambda a_, b_: 1 - a_ if b_ else a_

    def start(ins, outs, ss, rs):
        x, y, c, _ = _place()
        mine = outs[0].at[4 * x + 2 * y + c]
        pltpu.make_async_copy(ins[0], mine, ss.at[7]).start()
        for j, (bx, by, bc) in enumerate(flips):
            _remote(ins[0], mine, ss, rs, j, (flip(x, bx), flip(y, by), flip(c, bc))).start()

    def finish(ins, outs, ss, rs):
        x, y, c, _ = _place()
        mine = outs[0].at[4 * x + 2 * y + c]
        for j, (bx, by, bc) in enumerate(flips):
            slot = outs[0].at[4 * flip(x, bx) + 2 * flip(y, by) + flip(c, bc)]
            _remote(slot, slot, ss, rs, j, (x, y, c)).wait_recv()
        for j in range(7):
            _remote(ins[0], mine, ss, rs, j, (x, y, c)).wait_send()
        pltpu.make_async_copy(ins[0], mine, ss.at[7]).wait()

    return _Exchange([v], [jax.ShapeDtypeStruct((8, R, C), v.dtype)], 8, start, finish)


def _row_block(rows, cols, limit=256 * 1024):
    if rows * cols <= limit:
        return rows
    for br in range(limit // cols // 8 * 8, 0, -8):
        if rows % br == 0:
            return br
    return rows


def _sum_slots(buf, *, name):
    n, R, C = buf.shape
    br = _row_block(R, C)

    def kern(b_ref, o_ref):
        acc = b_ref[0].astype(F32)
        for s in range(1, n):
            acc = acc + b_ref[s].astype(F32)
        o_ref[...] = acc

    return pl.pallas_call(
        kern, name=name, grid=(pl.cdiv(R, br),),
        in_specs=[pl.BlockSpec((n, br, C), lambda i: (0, i, 0))],
        out_specs=pl.BlockSpec((br, C), lambda i: (i, 0)),
        out_shape=jax.ShapeDtypeStruct((R, C), F32),
        compiler_params=_cp(("arbitrary",)),
    )(buf)


def _scalar(s):
    return jnp.reshape(s, (1,)).astype(jnp.int32)


def _sum_pair(g0, g1, other, c, *, name):
    _, R, C = g0.shape
    br = _row_block(R, C)

    def kern(c_ref, g0_ref, g1_ref, o_ref, out_ref):
        own = jnp.where(c_ref[0] == 0, g0_ref[...], g1_ref[...])
        out_ref[...] = (own + o_ref[...]).astype(out_ref.dtype)

    blk = (None, br, C)
    return pl.pallas_call(
        kern, name=name,
        grid_spec=pltpu.PrefetchScalarGridSpec(
            num_scalar_prefetch=1, grid=(N_CHIPS, R // br),
            in_specs=[pl.BlockSpec(blk, lambda k, i, cr: (k, i * (1 - cr[0]), 0)),
                      pl.BlockSpec(blk, lambda k, i, cr: (k, i * cr[0], 0)),
                      pl.BlockSpec(blk, lambda k, i, cr: (k, i, 0))],
            out_specs=pl.BlockSpec(blk, lambda k, i, cr: (k, i, 0))),
        out_shape=jax.ShapeDtypeStruct((N_CHIPS, R, C), BF16),
        compiler_params=_cp(("arbitrary", "arbitrary")),
    )(_scalar(c), g0, g1, other)


def _sum_chips(s, got, k, *, name):
    _, R, C = s.shape
    br = _row_block(R, C)

    def kern(k_ref, s_ref, a_ref, b_ref, c_ref, out_ref):
        out_ref[...] = ((s_ref[...].astype(F32) + a_ref[...].astype(F32)) + b_ref[...].astype(F32)) \
            + c_ref[...].astype(F32)

    blk = (None, br, C)
    peer = lambda d: pl.BlockSpec(blk, lambda i, kr: ((kr[0] + d) % N_CHIPS, i, 0))
    return pl.pallas_call(
        kern, name=name,
        grid_spec=pltpu.PrefetchScalarGridSpec(
            num_scalar_prefetch=1, grid=(R // br,),
            in_specs=[peer(0), peer(1), peer(2), peer(3)],
            out_specs=pl.BlockSpec((br, C), lambda i, kr: (i, 0))),
        out_shape=jax.ShapeDtypeStruct((R, C), F32),
        compiler_params=_cp(("arbitrary",)),
    )(_scalar(k), s, got, got, got)


def _adam_math(w, g, m, v):
    nm = ADAM_B1 * m + (1.0 - ADAM_B1) * g
    nv = ADAM_B2 * v + (1.0 - ADAM_B2) * jnp.square(g)
    m_hat = nm / (1.0 - ADAM_B1 ** ADAM_STEP)
    v_hat = nv / (1.0 - ADAM_B2 ** ADAM_STEP)
    return -ADAM_LR * (m_hat / (jnp.sqrt(v_hat) + ADAM_EPS) + ADAM_WD * w), nm, nv


def _adamw_layers(w, mine, theirs, m, v, c, *, name):
    shape = w.shape
    R, C = mine.shape
    w3, m3, v3 = (a.reshape(DEPTH, R, C) for a in (w, m, v))
    br = _row_block(R, C)

    def kern(c_ref, w_ref, a_ref, b_ref, m_ref, v_ref, g_ref, d_ref, nm_ref, nv_ref):
        g = jnp.where(pl.program_id(0) == c_ref[0], a_ref[...], b_ref[...])
        g_ref[...] = g
        d_ref[...], nm_ref[...], nv_ref[...] = _adam_math(w_ref[...], g, m_ref[...], v_ref[...])

    lay = pl.BlockSpec((None, br, C), lambda l, i, cr: (l, i, 0))
    outs = pl.pallas_call(
        kern, name=name,
        grid_spec=pltpu.PrefetchScalarGridSpec(
            num_scalar_prefetch=1, grid=(DEPTH, R // br),
            in_specs=[lay,
                      pl.BlockSpec((br, C), lambda l, i, cr: (jnp.where(l == cr[0], i, 0), 0)),
                      pl.BlockSpec((br, C), lambda l, i, cr: (jnp.where(l == cr[0], 0, i), 0)),
                      lay, lay],
            out_specs=[lay] * 4),
        out_shape=[jax.ShapeDtypeStruct((DEPTH, R, C), F32)] * 4,
        compiler_params=_cp(("arbitrary", "arbitrary")),
    )(_scalar(c), w3, mine, theirs, m3, v3)
    return [o.reshape(shape) for o in outs]


def _adamw(w, g, m, v, *, name):
    shape = w.shape
    cols = shape[-1]
    w2, g2, m2, v2 = (a.reshape(-1, cols) for a in (w, g, m, v))
    rows = w2.shape[0]
    br = _row_block(rows, cols)

    def kern(w_ref, g_ref, m_ref, v_ref, d_ref, nm_ref, nv_ref):
        gv = g_ref[...]
        nm = ADAM_B1 * m_ref[...] + (1.0 - ADAM_B1) * gv
        nv = ADAM_B2 * v_ref[...] + (1.0 - ADAM_B2) * jnp.square(gv)
        m_hat = nm / (1.0 - ADAM_B1 ** ADAM_STEP)
        v_hat = nv / (1.0 - ADAM_B2 ** ADAM_STEP)
        d_ref[...] = -ADAM_LR * (m_hat / (jnp.sqrt(v_hat) + ADAM_EPS) + ADAM_WD * w_ref[...])
        nm_ref[...] = nm
        nv_ref[...] = nv

    spec = pl.BlockSpec((br, cols), lambda i: (i, 0))
    outs = pl.pallas_call(
        kern, name=name, grid=(rows // br,), in_specs=[spec] * 4, out_specs=[spec] * 3,
        out_shape=[jax.ShapeDtypeStruct((rows, cols), F32)] * 3,
        compiler_params=_cp(("arbitrary",)),
    )(w2, g2, m2, v2)
    return [o.reshape(shape) for o in outs]


_NAMES = ['ln_in_g', 'ln_in_b', 'w_in', 'b_forget', 'conv_w', 'conv_b', 'w_r', 'b_r', 'w_i', 'b_i', 'lru_lambda',
          'rel_bias', 'w_branch', 'w_gate', 'b_gate', 'w_out', 'ln1_g', 'ln1_b', 'w_ff1', 'w_ff2', 'ln2_g', 'ln2_b']
_BIG = {'w_in': 2, 'w_branch': 3, 'w_gate': 2, 'w_out': 1, 'w_ff1': 2, 'w_ff2': 1}
_SMALL_SHARDED = {'b_gate': 2, 'conv_w': 2, 'rel_bias': 2}
_SHARDED = {**_BIG, **_SMALL_SHARDED}
_REPLICATED = [n for n in _NAMES if n not in _SHARDED]
_TILE = 8 * LANES


def _tiles(a, cols):
    flat = a.reshape(-1)
    per = 8 * cols
    flat = jnp.pad(flat, (0, (-flat.shape[0]) % per))
    return flat.reshape(-1, cols)


def _pack(arrs, cols):
    return jnp.concatenate([_tiles(a, cols) for a in arrs], axis=0)


def _unpack(packed, like, cols):
    out, r0 = [], 0
    for a in like:
        n = math.prod(a.shape)
        rows = -(-n // (8 * cols)) * 8
        out.append(packed[r0:r0 + rows].reshape(-1)[:n].reshape(a.shape))
        r0 += rows
    return out


_EARLY = ['w_branch', 'w_gate', 'w_out', 'w_ff1', 'w_ff2']


def _chip_major_early(G, l):
    return [G[('w_branch', l)].reshape(N_CHIPS, N_BRANCH * BRANCH_WIDTH, BRANCH_WIDTH),
            G[('w_gate', l)].reshape(N_CHIPS, N_BRANCH * (D_MODEL // N_CHIPS), D_MODEL),
            G[('w_out', l)], G[('w_ff1', l)], G[('w_ff2', l)]]


def _chip_major_late(G, l):
    g_att, g_rec = G[('w_att', l)], G[('w_rec', l)]
    w_in = jnp.concatenate([g_att[:, :1536], g_rec[:, 1024:1024 + N_HEADS], g_rec[:, :1024], g_att[:, 1536:]], -1)
    per_chip = lambda g, rows: g.reshape(rows, N_CHIPS, -1).transpose(1, 0, 2)
    bg = per_chip(G[('b_gate', l)], N_BRANCH)
    cw = per_chip(G[('conv_w', l)], CONV_WIDTH)
    rb = per_chip(G[('rel_bias', l)], N_HEADS)
    small = jnp.stack([_pack([bg[j], cw[j], rb[j]], LANES) for j in range(N_CHIPS)])
    return [w_in.reshape(D_MODEL, N_CHIPS, D_IN // N_CHIPS).transpose(1, 0, 2), small]


class _Hook:
    def __init__(self, spec, done):
        self.spec, self.done = spec, done


def _unshard(blocks, axis):
    return jnp.concatenate([blocks[k] for k in range(N_CHIPS)], axis=axis)


def kernel(x, ln_in_g, ln_in_b, w_in, b_forget, conv_w, conv_b, w_r, b_r, w_i, b_i, lru_lambda, rel_bias, w_branch, w_gate, b_gate, w_out, ln1_g, ln1_b, w_ff1, w_ff2, ln2_g, ln2_b, loss_target, m_ln_in_g, m_ln_in_b, m_w_in, m_b_forget, m_conv_w, m_conv_b, m_w_r, m_b_r, m_w_i, m_b_i, m_lru_lambda, m_rel_bias, m_w_branch, m_w_gate, m_b_gate, m_w_out, m_ln1_g, m_ln1_b, m_w_ff1, m_w_ff2, m_ln2_g, m_ln2_b, v_ln_in_g, v_ln_in_b, v_w_in, v_b_forget, v_conv_w, v_conv_b, v_w_r, v_b_r, v_w_i, v_b_i, v_lru_lambda, v_rel_bias, v_w_branch, v_w_gate, v_b_gate, v_w_out, v_ln1_g, v_ln1_b, v_w_ff1, v_w_ff2, v_ln2_g, v_ln2_b):
    w = dict(zip(_NAMES, (ln_in_g, ln_in_b, w_in, b_forget, conv_w, conv_b, w_r, b_r, w_i, b_i, lru_lambda, rel_bias,
                          w_branch, w_gate, b_gate, w_out, ln1_g, ln1_b, w_ff1, w_ff2, ln2_g, ln2_b)))
    m = dict(zip(_NAMES, (m_ln_in_g, m_ln_in_b, m_w_in, m_b_forget, m_conv_w, m_conv_b, m_w_r, m_b_r, m_w_i, m_b_i,
                          m_lru_lambda, m_rel_bias, m_w_branch, m_w_gate, m_b_gate, m_w_out, m_ln1_g, m_ln1_b,
                          m_w_ff1, m_w_ff2, m_ln2_g, m_ln2_b)))
    v = dict(zip(_NAMES, (v_ln_in_g, v_ln_in_b, v_w_in, v_b_forget, v_conv_w, v_conv_b, v_w_r, v_b_r, v_w_i, v_b_i,
                          v_lru_lambda, v_rel_bias, v_w_branch, v_w_gate, v_b_gate, v_w_out, v_ln1_g, v_ln1_b,
                          v_w_ff1, v_w_ff2, v_ln2_g, v_ln2_b)))
    c = lax.axis_index("c")

    k = 2 * lax.axis_index("x") + lax.axis_index("y")
    state = {}

    small_like = [w[n] for n in _SMALL_SHARDED]
    small_pack = jnp.stack([_pack([a[l] for a in small_like], LANES) for l in range(DEPTH)])
    W = _prep_weights({n: w[n] for n in _REPLICATED})
    got_in, got_small = _exchange(_gather_spec([w['w_in'].astype(BF16), small_pack]), name='gather_first')
    small_blocks = [[_unpack(got_small[j, l], [a[l] for a in small_like], LANES) for l in range(DEPTH)]
                    for j in range(N_CHIPS)]
    first = {'w_in': _unshard(got_in, _BIG['w_in'])}
    for i, n in enumerate(_SMALL_SHARDED):
        first[n] = jnp.concatenate([jnp.stack([small_blocks[j][l][i] for l in range(DEPTH)])
                                    for j in range(N_CHIPS)], axis=_SMALL_SHARDED[n])
    _prep_weights(first, W)

    def gather_on(names):
        chip_major = ('w_ff1', 'w_ff2')
        return _Hook(lambda W_, G_: _gather_spec([w[n].astype(BF16) for n in names]),
                     lambda outs, W_, G_: _prep_weights(
                         {n: o if n in chip_major else _unshard(o, _BIG[n]) for n, o in zip(names, outs)}, W_))

    def pair_spec(W_, G_):
        state['early'] = [_chip_major_early(G_, l) for l in range(DEPTH)]
        return _pair_spec(*state['early'])

    def pair_done(outs, W_, G_):
        state['pair_sum'] = [_sum_pair(a0, a1, o, c, name='grad_pair_sum')
                             for a0, a1, o in zip(*state['early'], outs)]

    rep_main = _REPLICATED[2:]

    def rep_spec(W_, G_):
        dev = [jnp.stack([G_[(n, l)].reshape(w[n].shape[1:]) for l in range(DEPTH)]) for n in rep_main]
        return _gather8_spec(_pack(dev, LANES))

    hooks = {('fox_fwd', 0): gather_on(['w_gate', 'w_out']),
             ('sb_fwd', 0): gather_on(['w_ff1', 'w_ff2']),
             ('chunk_fwd', 0): gather_on(['w_branch']),
             ('chunk_bwd', 0): _Hook(rep_spec, lambda outs, W_, G_: state.update(rep_all=outs[0])),
             ('fox_bwd', 0): _Hook(pair_spec, pair_done),
             ('sb_bwd', 0): _Hook(lambda W_, G_: _chip_spec(state['pair_sum']),
                                  lambda outs, W_, G_: state.update(from_chips=outs))}
    loss, gx, G = _device_step(x[0], loss_target[0], W, hooks)

    late = [_chip_major_late(G, l) for l in range(DEPTH)]
    late_sibling = _exchange(_pair_spec(*late), name='grad_pair_exchange')
    late_sum = [_sum_pair(a0, a1, o, c, name='grad_pair_sum') for a0, a1, o in zip(*late, late_sibling)]
    late_chips = _exchange(_chip_spec(late_sum), name='grad_chip_exchange')
    pair_sum = [late_sum[0]] + state['pair_sum'] + [late_sum[1]]
    from_chips = [late_chips[0]] + state['from_chips'] + [late_chips[1]]
    mine = [_sum_chips(s, got, k, name='grad_chip_sum') for s, got in zip(pair_sum, from_chips)]
    theirs = _pair_swap(mine, name='grad_pair_swap')

    g_rep = dict(zip(rep_main, _unpack(_sum_slots(state['rep_all'], name='grad_sum8'), [w[n] for n in rep_main], LANES)))
    entry = _exchange(_gather8_spec(_pack([G[('ln_in_g', -1)][0], G[('ln_in_b', -1)][0]], LANES)),
                      name='grad_gather8')[0]
    g_rep.update(zip(_REPLICATED[:2], _unpack(_sum_slots(entry, name='grad_sum8'), [w[n] for n in _REPLICATED[:2]],
                                              LANES)))

    grads, delta, new_m, new_v = {}, {}, {}, {}
    for n, a, b in zip(_BIG, mine, theirs):
        grads[n], delta[n], new_m[n], new_v[n] = _adamw_layers(w[n], a, b, m[n], v[n], c, name='adamw')
    small_layers = [jnp.where(c == l, mine[-1], theirs[-1]) for l in range(DEPTH)]
    small_shards = [_unpack(s, [w[n][0] for n in _SMALL_SHARDED], LANES) for s in small_layers]
    g_shard = {n: jnp.stack([small_shards[l][i] for l in range(DEPTH)]) for i, n in enumerate(_SMALL_SHARDED)}
    small = _REPLICATED + list(_SMALL_SHARDED)
    for n in small:
        grads[n] = g_rep[n] if n in g_rep else g_shard[n]
    packs = [_pack([d[n] for n in small], LANES) for d in (w, grads, m, v)]
    outs = _adamw(*packs, name='adamw_small')
    small_like_all = [w[n] for n in small]
    for d, o in zip((delta, new_m, new_v), outs):
        d.update(zip(small, _unpack(o, small_like_all, LANES)))

    loss = lax.psum(loss, ("x", "y", "c"))
    return (loss, gx[None], *[grads[n] for n in _NAMES], *[delta[n] for n in _NAMES],
            *[new_m[n] for n in _NAMES], *[new_v[n] for n in _NAMES])
```

```python
import functools
import math

import jax
import jax.numpy as jnp
from jax import lax
from jax.experimental import pallas as pl
from jax.experimental.pallas import tpu as pltpu

F32 = jnp.float32
BF16 = jnp.bfloat16

D_MODEL = 2048
DEPTH = 2
CHUNK = 64
HEAD_DIM = 128
N_BRANCH = 4
BRANCH_WIDTH = 512
N_HEADS = 4
CONV_WIDTH = 4
LRU_C = 8.0
LOOKBACK_CHUNKS = 8
BAND = (LOOKBACK_CHUNKS + 1) * CHUNK
PAD_ROWS = LOOKBACK_CHUNKS * CHUNK
REL_CLIP = 256
REL_TABLE = REL_CLIP + CHUNK
REL_PAD = 384
D_FF = 4 * D_MODEL
FF_SHARD = D_FF // 4
D_IN = 5636
ALPHA = (2.0 * DEPTH) ** 0.25
LN_EPS = 1e-5
SCALE = HEAD_DIM ** -0.5

ADAM_LR = 0.001
ADAM_B1 = 0.9
ADAM_B2 = 0.999
ADAM_EPS = 1e-08
ADAM_WD = 0.01
ADAM_STEP = 10

N_ATT = 9 * BRANCH_WIDTH
N_REC = 2 * BRANCH_WIDTH + 128

V7X_VMEM_LIMIT = 56 * 1024 * 1024
LANES = 128
ATT_BLOCK = 256
ATT_KEYS = 1024

NT = (((1,), (1,)), ((), ()))
TN = (((0,), (0,)), ((), ()))
NN = (((1,), (0,)), ((), ()))

MESH = pl.DeviceIdType.MESH


def _cp(sem=None):
    return pltpu.CompilerParams(dimension_semantics=sem, vmem_limit_bytes=V7X_VMEM_LIMIT)


def _dot(a, b, dims=NN):
    return lax.dot_general(a, b, dims, preferred_element_type=F32)


def _pick(n, prefs):
    for p in prefs:
        if n % p == 0:
            return p
    return n


def _split3(x):
    hi = x.astype(BF16)
    r1 = x - hi.astype(F32)
    mid = r1.astype(BF16)
    lo = (r1 - mid.astype(F32)).astype(BF16)
    return hi, mid, lo


def _split2(x):
    hi = x.astype(BF16)
    lo = (x - hi.astype(F32)).astype(BF16)
    return hi, lo


def _sigmoid(z):
    return 1.0 / (1.0 + jnp.exp(-z))


def _log_sigmoid(z):
    return jnp.minimum(z, 0.0) - jnp.log(1.0 + jnp.exp(-jnp.abs(z)))


def _mm(a, b, *, name, ta=False, tb=False, out_dtypes=(F32,), epilogue=None, extras=(),
        bm=None, bn=None, bk=None, out_map=None, b_view=None, carry=None):
    M, K = (a.shape[1], a.shape[0]) if ta else a.shape
    N = b.shape[0] if tb else b.shape[1]
    if b_view is not None:
        K, N = b_view[:2]
    bm = bm or _pick(M, (1024, 512, 256, 128))
    bn = bn or _pick(N, (1024, 1536, 1152, 512, 256, 128))
    bk = bk or _pick(K, (2048, 1536, 1024, 1152, 512, 256, 128))
    nk = K // bk
    a_spec = pl.BlockSpec((bk, bm), lambda i, j, k: (k, i)) if ta else pl.BlockSpec((bm, bk), lambda i, j, k: (i, k))
    b_spec = pl.BlockSpec((bn, bk), lambda i, j, k: (j, k)) if tb else pl.BlockSpec((bk, bn), lambda i, j, k: (k, j))
    if b_view is not None:
        b_spec = pl.BlockSpec(b_view[2], b_view[3])
    ex_specs = [pl.BlockSpec((bm, bn), lambda i, j, k: (i, j)) if kind == 'mn'
                else pl.BlockSpec((1, bn), lambda i, j, k: (0, j)) for _, kind in extras]
    n_ex, n_out = len(extras), len(out_dtypes)
    dims = TN if ta else (NT if tb else NN)

    def kern(*refs):
        a_ref, b_ref = refs[0], refs[1]
        ex_refs = refs[2:2 + n_ex]
        out_refs = refs[2 + n_ex:2 + n_ex + n_out]
        acc_ref = refs[-1]
        k = pl.program_id(2)
        part = _dot(a_ref[...].astype(BF16), b_ref[...].astype(BF16), dims)

        @pl.when(k == 0)
        def _():
            acc_ref[...] = part

        @pl.when(k > 0)
        def _():
            acc_ref[...] += part

        @pl.when(k == nk - 1)
        def _():
            acc = acc_ref[...]
            outs = (acc,) if epilogue is None else epilogue(acc, *[r[...] for r in ex_refs])
            for o_ref, o in zip(out_refs, outs):
                o_ref[...] = o.astype(o_ref.dtype).reshape(o_ref.shape)

    if out_map is None:
        out_specs = [pl.BlockSpec((bm, bn), lambda i, j, k: (i, j)) for _ in out_dtypes]
        out_shape = [jax.ShapeDtypeStruct((M, N), dt) for dt in out_dtypes]
    else:
        shape, block, index = out_map
        out_specs = [pl.BlockSpec(block, lambda i, j, k: index(i, j))]
        out_shape = [jax.ShapeDtypeStruct(shape, out_dtypes[0])]
    res, extra = _call_with_carry(
        kern, name=name, grid=(M // bm, N // bn, nk), carry=carry,
        in_specs=[a_spec, b_spec] + ex_specs, out_specs=out_specs, out_shape=out_shape,
        scratch_shapes=[pltpu.VMEM((bm, bn), F32)], args=(a, b, *[e for e, _ in extras]),
        semantics=("parallel", "parallel", "arbitrary"))
    res = res[0] if n_out == 1 else res
    return res if carry is None else (res, extra)


def _ln_fwd(h, g, b, *, name):
    T, D = h.shape
    bt = _pick(T, (512, 256, 128))

    def kern(h_ref, g_ref, b_ref, y_ref, yb_ref):
        x = h_ref[...]
        mu = jnp.mean(x, axis=-1, keepdims=True)
        xc = x - mu
        var = jnp.mean(xc * xc, axis=-1, keepdims=True)
        y = xc * lax.rsqrt(var + LN_EPS) * g_ref[...] + b_ref[...]
        y_ref[...] = y
        yb_ref[...] = y.astype(BF16)

    row = pl.BlockSpec((bt, D), lambda i: (i, 0))
    vec = pl.BlockSpec((1, D), lambda i: (0, 0))
    return pl.pallas_call(
        kern, name=name, grid=(T // bt,), in_specs=[row, vec, vec], out_specs=[row, row],
        out_shape=[jax.ShapeDtypeStruct((T, D), F32), jax.ShapeDtypeStruct((T, D), BF16)],
        compiler_params=_cp(("arbitrary",)),
    )(h, g.reshape(1, D), b.reshape(1, D))


def _ln_bwd(h, dy, g, *, name):
    T, D = h.shape
    bt = _pick(T, (512, 256, 128))

    def kern(h_ref, dy_ref, g_ref, dh_ref, dhb_ref, dg_ref, db_ref):
        i = pl.program_id(0)
        x = h_ref[...]
        dyv = dy_ref[...]
        mu = jnp.mean(x, axis=-1, keepdims=True)
        xc = x - mu
        var = jnp.mean(xc * xc, axis=-1, keepdims=True)
        rstd = lax.rsqrt(var + LN_EPS)
        xhat = xc * rstd
        dxh = dyv * g_ref[...]
        m1 = jnp.mean(dxh, axis=-1, keepdims=True)
        m2 = jnp.mean(dxh * xhat, axis=-1, keepdims=True)
        dh = rstd * (dxh - m1 - xhat * m2)
        dh_ref[...] = dh
        dhb_ref[...] = dh.astype(BF16)
        pg = jnp.sum(dyv * xhat, axis=0, keepdims=True)
        pb = jnp.sum(dyv, axis=0, keepdims=True)

        @pl.when(i == 0)
        def _():
            dg_ref[...] = pg
            db_ref[...] = pb

        @pl.when(i > 0)
        def _():
            dg_ref[...] += pg
            db_ref[...] += pb

    row = pl.BlockSpec((bt, D), lambda i: (i, 0))
    vec = pl.BlockSpec((1, D), lambda i: (0, 0))
    return pl.pallas_call(
        kern, name=name, grid=(T // bt,), in_specs=[row, row, vec], out_specs=[row, row, vec, vec],
        out_shape=[jax.ShapeDtypeStruct((T, D), F32), jax.ShapeDtypeStruct((T, D), BF16),
                   jax.ShapeDtypeStruct((1, D), F32), jax.ShapeDtypeStruct((1, D), F32)],
        compiler_params=_cp(("arbitrary",)),
    )(h, dy, g.reshape(1, D))


def _loss_head(y, tgt, *, name):
    T, D = y.shape
    bt = _pick(T, (512, 256, 128))

    def kern(y_ref, t_ref, dy_ref, loss_ref):
        i = pl.program_id(0)
        e = y_ref[...] - t_ref[...]
        dy_ref[...] = e * (1.0 / D)
        part = 0.5 * jnp.sum(jnp.sum(e * e, axis=-1, keepdims=True) * (1.0 / D), axis=0, keepdims=True)
        part = jnp.broadcast_to(part, (8, LANES))

        @pl.when(i == 0)
        def _():
            loss_ref[...] = part

        @pl.when(i > 0)
        def _():
            loss_ref[...] += part

    row = pl.BlockSpec((bt, D), lambda i: (i, 0))
    return pl.pallas_call(
        kern, name=name, grid=(T // bt,), in_specs=[row, row],
        out_specs=[row, pl.BlockSpec((8, LANES), lambda i: (0, 0))],
        out_shape=[jax.ShapeDtypeStruct((T, D), F32), jax.ShapeDtypeStruct((8, LANES), F32)],
        compiler_params=_cp(("arbitrary",)),
    )(y, tgt)


def _tri(n, upper):
    r = lax.broadcasted_iota(jnp.int32, (n, n), 0)
    c = lax.broadcasted_iota(jnp.int32, (n, n), 1)
    return jnp.where((c >= r) if upper else (c <= r), 1.0, 0.0).astype(BF16)


def _forget_fwd(ff, bf, *, name):
    T = ff.shape[0]
    bt = 256

    def kern(ff_ref, bf_ref, out_ref, carry):
        i = pl.program_id(0)

        @pl.when(i == 0)
        def _():
            carry[...] = jnp.zeros_like(carry)

        ls = _log_sigmoid(ff_ref[...] + bf_ref[...])
        tri = _tri(bt, upper=False)
        hi, mid, lo = _split3(ls)
        cs = _dot(tri, hi) + _dot(tri, mid) + _dot(tri, lo) + carry[0:1, :]
        out_ref[...] = cs
        carry[...] = jnp.broadcast_to(cs[bt - 1:bt, :], carry.shape)

    return pl.pallas_call(
        kern, name=name, grid=(T // bt,),
        in_specs=[pl.BlockSpec((bt, LANES), lambda i: (i, 0)), pl.BlockSpec((1, LANES), lambda i: (0, 0))],
        out_specs=pl.BlockSpec((bt, LANES), lambda i: (i, 0)),
        out_shape=jax.ShapeDtypeStruct((T, LANES), F32),
        scratch_shapes=[pltpu.VMEM((8, LANES), F32)],
        compiler_params=_cp(("arbitrary",)),
    )(ff, bf)


def _forget_bwd(dFk, dFq, ff, bf, *, name):
    T = ff.shape[0]
    bt = 256
    nb = T // bt

    def kern(dFk_ref, dFq_ref, ff_ref, bf_ref, dff_ref, dbf_ref, carry):
        i = pl.program_id(0)

        @pl.when(i == 0)
        def _():
            carry[...] = jnp.zeros_like(carry)
            dbf_ref[...] = jnp.zeros_like(dbf_ref)

        tri = _tri(bt, upper=True)
        hi, mid, lo = _split3(dFk_ref[...] + dFq_ref[...])
        rs = _dot(tri, hi) + _dot(tri, mid) + _dot(tri, lo) + carry[0:1, :]
        carry[...] = jnp.broadcast_to(rs[0:1, :], carry.shape)
        z = ff_ref[...] + bf_ref[...]
        dff = rs * _sigmoid(-z)
        dff_ref[...] = dff.astype(dff_ref.dtype)
        dbf_ref[...] += jnp.sum(dff, axis=0, keepdims=True)

    rev = pl.BlockSpec((bt, LANES), lambda i: (nb - 1 - i, 0))
    vec = pl.BlockSpec((1, LANES), lambda i: (0, 0))
    return pl.pallas_call(
        kern, name=name, grid=(nb,), in_specs=[rev, rev, rev, vec], out_specs=[rev, vec],
        out_shape=[jax.ShapeDtypeStruct((T, LANES), BF16), jax.ShapeDtypeStruct((1, LANES), F32)],
        scratch_shapes=[pltpu.VMEM((8, LANES), F32)],
        compiler_params=_cp(("arbitrary",)),
    )(dFk, dFq, ff, bf)


def _head_lane(x, h):
    lane = lax.broadcasted_iota(jnp.int32, x.shape, 1)
    return jnp.sum(jnp.where(lane == h, x, 0.0), axis=1, keepdims=True)


def _att_blocks(T):
    return min(ATT_BLOCK, T), min(ATT_KEYS, T)


def _positions(i, j, bq, bk):
    r = i * bq + lax.broadcasted_iota(jnp.int32, (bq, bk), 0)
    c = j * bk + lax.broadcasted_iota(jnp.int32, (bq, bk), 1)
    return r, c


def _fox_fwd(u_att, fcum, frow, *, name, carry=None):
    T = u_att.shape[0]
    bq, bk = _att_blocks(T)
    nq, nk = T // bq, T // bk
    H = N_HEADS

    def kern(q_ref, k_ref, v_ref, fc_ref, fr_ref, o_ref, lse_ref):
        i = pl.program_id(1)
        q = q_ref[...]
        fq = _head_lane(fc_ref[...], pl.program_id(0))

        def step(j, carry, masked):
            m, l, acc = carry
            off = pl.multiple_of(j * bk, bk)
            k = k_ref[pl.ds(off, bk), :]
            v = v_ref[pl.ds(off, bk), :]
            s = _dot(q, k, NT) * SCALE + (fq - fr_ref[j])
            if masked:
                r, c = _positions(i, j, bq, bk)
                s = jnp.where(c <= r, s, -jnp.inf)
            m_new = jnp.maximum(m, jnp.max(s, axis=1, keepdims=True))
            a = jnp.exp(m - m_new)
            p = jnp.exp(s - m_new)
            l = a * l + jnp.sum(p, axis=1, keepdims=True)
            acc = a * acc + _dot(p.astype(BF16), v)
            return m_new, l, acc

        init = (jnp.full((bq, 1), -1e30, F32), jnp.zeros((bq, 1), F32), jnp.zeros((bq, HEAD_DIM), F32))
        nfull = (i * bq) // bk
        carry = lax.fori_loop(0, nfull, lambda j, cr: step(j, cr, False), init)
        m, l, acc = step(nfull, carry, True)
        o_ref[...] = (acc / l).astype(o_ref.dtype)
        lse_ref[...] = m + jnp.log(l)

    return _call_with_carry(
        kern, name=name, grid=(H, nq), carry=carry,
        in_specs=[pl.BlockSpec((bq, HEAD_DIM), lambda h, i: (i, h)),
                  pl.BlockSpec((T, HEAD_DIM), lambda h, i: (0, 4 + h)),
                  pl.BlockSpec((T, HEAD_DIM), lambda h, i: (0, 8 + h)),
                  pl.BlockSpec((bq, LANES), lambda h, i: (i, 0)),
                  pl.BlockSpec((None, nk, 1, bk), lambda h, i: (h, 0, 0, 0))],
        out_specs=[pl.BlockSpec((bq, HEAD_DIM), lambda h, i: (i, h)),
                   pl.BlockSpec((None, bq, 1), lambda h, i: (h, i, 0))],
        out_shape=[jax.ShapeDtypeStruct((T, BRANCH_WIDTH), BF16), jax.ShapeDtypeStruct((H, T, 1), F32)],
        scratch_shapes=[], args=(u_att, u_att, u_att, fcum, frow))


def _row_dot(a, b, *, name):
    T = a.shape[0]
    bt = _pick(T, (512, 256, 128))

    def kern(a_ref, b_ref, o_ref):
        p = a_ref[...].astype(F32) * b_ref[...].astype(F32)
        for h in range(N_HEADS):
            o_ref[h] = jnp.sum(p[:, h * HEAD_DIM:(h + 1) * HEAD_DIM], axis=1, keepdims=True)

    row = pl.BlockSpec((bt, BRANCH_WIDTH), lambda i: (i, 0))
    return pl.pallas_call(
        kern, name=name, grid=(T // bt,), in_specs=[row, row],
        out_specs=pl.BlockSpec((N_HEADS, bt, 1), lambda i: (0, i, 0)),
        out_shape=jax.ShapeDtypeStruct((N_HEADS, T, 1), F32),
        compiler_params=_cp(("arbitrary",)),
    )(a, b)


def _fox_bwd(u_att, do, lse, delta, fcum, frow, *, name, carry=None):
    T = u_att.shape[0]
    bq, bk = _att_blocks(T)
    nq, nk = T // bq, T // bk
    H = N_HEADS

    def kern(q_ref, k_ref, v_ref, do_ref, lse_ref, dl_ref, fc_ref, fr_ref,
             dq_ref, dk_ref, dv_ref, df_ref, dfq_ref, dk_acc, dv_acc, df_acc):
        i = pl.program_id(1)

        @pl.when(i == 0)
        def _():
            dk_acc[...] = jnp.zeros_like(dk_acc)
            dv_acc[...] = jnp.zeros_like(dv_acc)
            df_acc[...] = jnp.zeros_like(df_acc)

        q = q_ref[...]
        dov = do_ref[...]
        fq = _head_lane(fc_ref[...], pl.program_id(0))
        lsev = lse_ref[...]
        dlt = dl_ref[...]

        def step(j, carry, masked):
            dq, dfq = carry
            off = pl.multiple_of(j * bk, bk)
            k = k_ref[pl.ds(off, bk), :]
            v = v_ref[pl.ds(off, bk), :]
            s = _dot(q, k, NT) * SCALE + (fq - fr_ref[j])
            p = jnp.exp(s - lsev)
            if masked:
                r, c = _positions(i, j, bq, bk)
                p = jnp.where(c <= r, p, 0.0)
            dp = _dot(dov, v, NT)
            ds = p * (dp - dlt)
            dsb = ds.astype(BF16)
            dq = dq + _dot(dsb, k)
            dk_acc[pl.ds(off, bk), :] += _dot(dsb, q, TN)
            dv_acc[pl.ds(off, bk), :] += _dot(p.astype(BF16), dov, TN)
            df_acc[j] += -jnp.sum(ds, axis=0, keepdims=True)
            return dq, dfq + jnp.sum(ds, axis=1, keepdims=True)

        nfull = (i * bq) // bk
        carry = lax.fori_loop(0, nfull, lambda j, cr: step(j, cr, False),
                              (jnp.zeros((bq, HEAD_DIM), F32), jnp.zeros((bq, 1), F32)))
        dq, dfq = step(nfull, carry, True)
        dq_ref[...] = (dq * SCALE).astype(dq_ref.dtype)
        dfq_ref[...] = dfq

        @pl.when(i == nq - 1)
        def _():
            dk_ref[...] = (dk_acc[...] * SCALE).astype(dk_ref.dtype)
            dv_ref[...] = dv_acc[...].astype(dv_ref.dtype)
            df_ref[...] = df_acc[...]

    col = lambda: pl.BlockSpec((None, bq, 1), lambda h, i: (h, i, 0))
    return _call_with_carry(
        kern, name=name, grid=(H, nq), carry=carry,
        in_specs=[pl.BlockSpec((bq, HEAD_DIM), lambda h, i: (i, h)),
                  pl.BlockSpec((T, HEAD_DIM), lambda h, i: (0, 4 + h)),
                  pl.BlockSpec((T, HEAD_DIM), lambda h, i: (0, 8 + h)),
                  pl.BlockSpec((bq, HEAD_DIM), lambda h, i: (i, h)),
                  col(), col(), pl.BlockSpec((bq, LANES), lambda h, i: (i, 0)),
                  pl.BlockSpec((None, nk, 1, bk), lambda h, i: (h, 0, 0, 0))],
        out_specs=[pl.BlockSpec((bq, HEAD_DIM), lambda h, i: (i, h)),
                   pl.BlockSpec((T, HEAD_DIM), lambda h, i: (0, h)),
                   pl.BlockSpec((T, HEAD_DIM), lambda h, i: (0, h)),
                   pl.BlockSpec((None, nk, 1, bk), lambda h, i: (h, 0, 0, 0)),
                   pl.BlockSpec((None, bq, 1), lambda h, i: (h, i, 0))],
        out_shape=[jax.ShapeDtypeStruct((T, BRANCH_WIDTH), BF16)] * 3
                  + [jax.ShapeDtypeStruct((H, nk, 1, bk), F32), jax.ShapeDtypeStruct((H, T, 1), F32)],
        scratch_shapes=[pltpu.VMEM((T, HEAD_DIM), F32), pltpu.VMEM((T, HEAD_DIM), F32),
                        pltpu.VMEM((nk, 1, bk), F32)],
        args=(u_att, u_att, u_att, do, lse, delta, fcum, frow))


def _softplus_parts(z):
    t = jnp.exp(-jnp.abs(z))
    sp = jnp.maximum(z, 0.0) + jnp.log(1.0 + t)
    return t, sp


def _sb_tri(B):
    r = lax.broadcasted_iota(jnp.int32, (B, B), 0)
    c = lax.broadcasted_iota(jnp.int32, (B, B), 1)
    suffix = jnp.where(r >= c, 1.0, 0.0).astype(BF16)
    prefix = jnp.where(r <= c, 1.0, 0.0).astype(BF16)
    return suffix, prefix


def _sb_fwd(u_att, *, name, carry=None):
    T = u_att.shape[0]
    B, bk = _att_blocks(T)
    nq, nsub = T // B, bk // B
    H = N_HEADS

    def kern(q_ref, k_ref, v_ref, o_ref):
        i = pl.program_id(1)
        q = q_ref[...]
        suffix, _ = _sb_tri(B)

        def step(j, carry, masked):
            run, acc = carry
            parts = []
            for s in reversed(range(nsub)):
                jb = j * nsub + s
                off = pl.multiple_of(jb * B, B)
                k = k_ref[pl.ds(off, B), :]
                z = _dot(q, k, NT) * SCALE
                _, sp = _softplus_parts(z)
                lg = -sp
                valid = None
                if masked:
                    r, c = _positions(i, jb, B, B)
                    valid = c < r
                    lg = jnp.where(valid, lg, 0.0)
                hi, lo = _split2(lg)
                cum = _dot(hi, suffix) + _dot(lo, suffix)
                parts.append((off, z, cum, jnp.sum(lg, axis=1, keepdims=True), valid))
            for off, z, cum, rs, valid in parts:
                a = jnp.exp(z + cum + run)
                if masked:
                    a = jnp.where(valid, a, 0.0)
                acc = acc + _dot(a.astype(BF16), v_ref[pl.ds(off, B), :])
                run = run + rs
            return run, acc

        nfull = (i * B) // bk
        carry = step(nfull, (jnp.zeros((B, 1), F32), jnp.zeros((B, HEAD_DIM), F32)), True)
        _, acc = lax.fori_loop(0, nfull, lambda jj, cr: step(nfull - 1 - jj, cr, False), carry)
        o_ref[...] = acc.astype(o_ref.dtype)

    return _call_with_carry(
        kern, name=name, grid=(H, nq), carry=carry,
        in_specs=[pl.BlockSpec((B, HEAD_DIM), lambda h, i: (i, 12 + h)),
                  pl.BlockSpec((T, HEAD_DIM), lambda h, i: (0, 16 + h)),
                  pl.BlockSpec((T, HEAD_DIM), lambda h, i: (0, 20 + h))],
        out_specs=[pl.BlockSpec((B, HEAD_DIM), lambda h, i: (i, h))],
        out_shape=[jax.ShapeDtypeStruct((T, BRANCH_WIDTH), BF16)],
        scratch_shapes=[], args=(u_att, u_att, u_att))


def _sb_bwd(u_att, do, *, name, carry=None):
    T = u_att.shape[0]
    B, bk = _att_blocks(T)
    nq, nsub = T // B, bk // B
    H = N_HEADS

    def kern(q_ref, k_ref, v_ref, do_ref, dq_ref, dk_ref, dv_ref, dk_acc, dv_acc, de_s, sg_s):
        i = pl.program_id(1)

        @pl.when(i == 0)
        def _():
            dk_acc[...] = jnp.zeros_like(dk_acc)
            dv_acc[...] = jnp.zeros_like(dv_acc)

        q = q_ref[...]
        dov = do_ref[...]
        suffix, prefix = _sb_tri(B)

        def sweep1(j, run, masked):
            parts = []
            for s in reversed(range(nsub)):
                jb = j * nsub + s
                off = pl.multiple_of(jb * B, B)
                k = k_ref[pl.ds(off, B), :]
                z = _dot(q, k, NT) * SCALE
                t, sp = _softplus_parts(z)
                lg = -sp
                sg = jnp.exp(z + lg)
                valid = None
                if masked:
                    r, c = _positions(i, jb, B, B)
                    valid = c < r
                    lg = jnp.where(valid, lg, 0.0)
                    sg = jnp.where(valid, sg, 0.0)
                sg_s[jb] = sg.astype(sg_s.dtype)
                hi, lo = _split2(lg)
                cum = _dot(hi, suffix) + _dot(lo, suffix)
                da = _dot(dov, v_ref[pl.ds(off, B), :], NT)
                parts.append((jb, off, z, cum, da, jnp.sum(lg, axis=1, keepdims=True), valid))
            for jb, off, z, cum, da, rs, valid in parts:
                a = jnp.exp(z + cum + run)
                if masked:
                    a = jnp.where(valid, a, 0.0)
                de_s[jb] = a * da
                dv_acc[pl.ds(off, B), :] += _dot(a.astype(BF16), dov, TN)
                run = run + rs
            return run

        nfull = (i * B) // bk
        run = sweep1(nfull, jnp.zeros((B, 1), F32), True)
        lax.fori_loop(0, nfull, lambda jj, cr: sweep1(nfull - 1 - jj, cr, False), run)

        def sweep2(j, carry):
            pre, dq = carry
            parts = []
            for s in range(nsub):
                jb = j * nsub + s
                de = de_s[jb]
                hi, lo = _split2(de)
                parts.append((jb, de, _dot(hi, prefix) + _dot(lo, prefix), jnp.sum(de, axis=1, keepdims=True)))
            for jb, de, g, rs in parts:
                off = pl.multiple_of(jb * B, B)
                dz = (de - sg_s[jb].astype(F32) * (g + pre)).astype(BF16)
                dq = dq + _dot(dz, k_ref[pl.ds(off, B), :])
                dk_acc[pl.ds(off, B), :] += _dot(dz, q, TN)
                pre = pre + rs
            return pre, dq

        _, dq = lax.fori_loop(0, nfull + 1, sweep2, (jnp.zeros((B, 1), F32), jnp.zeros((B, HEAD_DIM), F32)))
        dq_ref[...] = (dq * SCALE).astype(dq_ref.dtype)

        @pl.when(i == nq - 1)
        def _():
            dk_ref[...] = (dk_acc[...] * SCALE).astype(dk_ref.dtype)
            dv_ref[...] = dv_acc[...].astype(dv_ref.dtype)

    return _call_with_carry(
        kern, name=name, grid=(H, nq), carry=carry,
        in_specs=[pl.BlockSpec((B, HEAD_DIM), lambda h, i: (i, 12 + h)),
                  pl.BlockSpec((T, HEAD_DIM), lambda h, i: (0, 16 + h)),
                  pl.BlockSpec((T, HEAD_DIM), lambda h, i: (0, 20 + h)),
                  pl.BlockSpec((B, HEAD_DIM), lambda h, i: (i, h))],
        out_specs=[pl.BlockSpec((B, HEAD_DIM), lambda h, i: (i, h)),
                   pl.BlockSpec((T, HEAD_DIM), lambda h, i: (0, h)),
                   pl.BlockSpec((T, HEAD_DIM), lambda h, i: (0, h))],
        out_shape=[jax.ShapeDtypeStruct((T, BRANCH_WIDTH), BF16)] * 3,
        scratch_shapes=[pltpu.VMEM((T, HEAD_DIM), F32), pltpu.VMEM((T, HEAD_DIM), F32),
                        pltpu.VMEM((T // B, B, B), F32), pltpu.VMEM((T // B, B, B), BF16)],
        args=(u_att, u_att, u_att, do))


def _rel_onehot(qrow):
    k = lax.broadcasted_iota(jnp.int32, (BAND, REL_PAD), 0)
    rr = lax.broadcasted_iota(jnp.int32, (BAND, REL_PAD), 1)
    idx = jnp.clip(PAD_ROWS + qrow - k, -(CHUNK - 1), REL_CLIP) + (CHUNK - 1)
    return jnp.where(idx == rr, 1.0, 0.0).astype(BF16)


def _band_bias(table, *, name):
    def kern(t_ref, o_ref):
        hi, mid, lo = _split3(t_ref[...])

        def body(qrow, _):
            oh = _rel_onehot(qrow)
            o_ref[qrow] = _dot(hi, oh, NT) + _dot(mid, oh, NT) + _dot(lo, oh, NT)
            return 0

        lax.fori_loop(0, CHUNK, body, 0)

    return pl.pallas_call(
        kern, name=name, out_shape=jax.ShapeDtypeStruct((CHUNK, 8, BAND), F32),
        compiler_params=_cp(),
    )(table)


def _band_bias_bwd(dbias, *, name):
    def kern(d_ref, o_ref):
        def body(qrow, acc):
            oh = _rel_onehot(qrow)
            hi, mid, lo = _split3(d_ref[qrow])
            return acc + _dot(hi, oh) + _dot(mid, oh) + _dot(lo, oh)

        o_ref[...] = lax.fori_loop(0, CHUNK, body, jnp.zeros((8, REL_PAD), F32))

    return pl.pallas_call(
        kern, name=name, out_shape=jax.ShapeDtypeStruct((8, REL_PAD), F32),
        compiler_params=_cp(),
    )(dbias)


def _chunk_rows(T):
    return _pick(T, (512, 256, 128, 64))


def _chunk_scores(q, kw, bias, c_global):
    s = _dot(q, kw, NT) * SCALE + bias
    col = lax.broadcasted_iota(jnp.int32, (CHUNK, BAND), 1)
    valid = (c_global * CHUNK + col) >= PAD_ROWS
    s = jnp.where(valid, s, -jnp.inf)
    m = jnp.max(s, axis=1, keepdims=True)
    e = jnp.exp(s - m)
    return e / jnp.sum(e, axis=1, keepdims=True)


def _chunk_fwd(u_att, bias, *, name, carry=None):
    T = u_att.shape[0]
    R = _chunk_rows(T)
    nr = T // R
    H = N_HEADS

    def kern(q_ref, k_ref, v_ref, b_ref, o_ref, kpad, vpad):
        i = pl.program_id(1)

        @pl.when(i == 0)
        def _():
            kpad[0:PAD_ROWS, :] = jnp.zeros((PAD_ROWS, HEAD_DIM), BF16)
            vpad[0:PAD_ROWS, :] = jnp.zeros((PAD_ROWS, HEAD_DIM), BF16)
            kpad[PAD_ROWS:, :] = k_ref[...]
            vpad[PAD_ROWS:, :] = v_ref[...]

        bias_v = b_ref[...]
        for cc in range(R // CHUNK):
            cg = i * (R // CHUNK) + cc
            off = pl.multiple_of(cg * CHUNK, CHUNK)
            q = q_ref[cc * CHUNK:(cc + 1) * CHUNK, :]
            kw = kpad[pl.ds(off, BAND), :]
            vw = vpad[pl.ds(off, BAND), :]
            p = _chunk_scores(q, kw, bias_v, cg)
            o_ref[cc * CHUNK:(cc + 1) * CHUNK, :] = _dot(p.astype(BF16), vw).astype(o_ref.dtype)

    return _call_with_carry(
        kern, name=name, grid=(H, nr), carry=carry,
        in_specs=[pl.BlockSpec((R, HEAD_DIM), lambda h, i: (i, 24 + h)),
                  pl.BlockSpec((T, HEAD_DIM), lambda h, i: (0, 28 + h)),
                  pl.BlockSpec((T, HEAD_DIM), lambda h, i: (0, 32 + h)),
                  pl.BlockSpec((None, CHUNK, BAND), lambda h, i: (h, 0, 0))],
        out_specs=[pl.BlockSpec((R, HEAD_DIM), lambda h, i: (i, h))],
        out_shape=[jax.ShapeDtypeStruct((T, BRANCH_WIDTH), BF16)],
        scratch_shapes=[pltpu.VMEM((T + PAD_ROWS, HEAD_DIM), BF16), pltpu.VMEM((T + PAD_ROWS, HEAD_DIM), BF16)],
        args=(u_att, u_att, u_att, bias))


def _chunk_bwd(u_att, bias, do, *, name, carry=None):
    T = u_att.shape[0]
    R = _chunk_rows(T)
    nr = T // R
    H = N_HEADS

    def kern(q_ref, k_ref, v_ref, b_ref, do_ref, dq_ref, dk_ref, dv_ref, db_ref, kpad, vpad, dkp, dvp):
        i = pl.program_id(1)

        @pl.when(i == 0)
        def _():
            kpad[0:PAD_ROWS, :] = jnp.zeros((PAD_ROWS, HEAD_DIM), BF16)
            vpad[0:PAD_ROWS, :] = jnp.zeros((PAD_ROWS, HEAD_DIM), BF16)
            kpad[PAD_ROWS:, :] = k_ref[...]
            vpad[PAD_ROWS:, :] = v_ref[...]
            dkp[...] = jnp.zeros_like(dkp)
            dvp[...] = jnp.zeros_like(dvp)
            db_ref[...] = jnp.zeros_like(db_ref)

        bias_v = b_ref[...]
        for cc in range(R // CHUNK):
            cg = i * (R // CHUNK) + cc
            off = pl.multiple_of(cg * CHUNK, CHUNK)
            q = q_ref[cc * CHUNK:(cc + 1) * CHUNK, :]
            dov = do_ref[cc * CHUNK:(cc + 1) * CHUNK, :]
            kw = kpad[pl.ds(off, BAND), :]
            vw = vpad[pl.ds(off, BAND), :]
            p = _chunk_scores(q, kw, bias_v, cg)
            dp = _dot(dov, vw, NT)
            ds = p * (dp - jnp.sum(p * dp, axis=1, keepdims=True))
            dsb = ds.astype(BF16)
            dq_ref[cc * CHUNK:(cc + 1) * CHUNK, :] = (_dot(dsb, kw) * SCALE).astype(dq_ref.dtype)
            dkp[pl.ds(off, BAND), :] += _dot(dsb, q, TN)
            dvp[pl.ds(off, BAND), :] += _dot(p.astype(BF16), dov, TN)
            db_ref[...] += ds

        @pl.when(i == nr - 1)
        def _():
            dk_ref[...] = (dkp[PAD_ROWS:, :] * SCALE).astype(dk_ref.dtype)
            dv_ref[...] = dvp[PAD_ROWS:, :].astype(dv_ref.dtype)

    return _call_with_carry(
        kern, name=name, grid=(H, nr), carry=carry,
        in_specs=[pl.BlockSpec((R, HEAD_DIM), lambda h, i: (i, 24 + h)),
                  pl.BlockSpec((T, HEAD_DIM), lambda h, i: (0, 28 + h)),
                  pl.BlockSpec((T, HEAD_DIM), lambda h, i: (0, 32 + h)),
                  pl.BlockSpec((None, CHUNK, BAND), lambda h, i: (h, 0, 0)),
                  pl.BlockSpec((R, HEAD_DIM), lambda h, i: (i, h))],
        out_specs=[pl.BlockSpec((R, HEAD_DIM), lambda h, i: (i, h)),
                   pl.BlockSpec((T, HEAD_DIM), lambda h, i: (0, h)),
                   pl.BlockSpec((T, HEAD_DIM), lambda h, i: (0, h)),
                   pl.BlockSpec((None, CHUNK, BAND), lambda h, i: (h, 0, 0))],
        out_shape=[jax.ShapeDtypeStruct((T, BRANCH_WIDTH), BF16)] * 3
                  + [jax.ShapeDtypeStruct((H, CHUNK, BAND), F32)],
        scratch_shapes=[pltpu.VMEM((T + PAD_ROWS, HEAD_DIM), BF16), pltpu.VMEM((T + PAD_ROWS, HEAD_DIM), BF16),
                        pltpu.VMEM((T + PAD_ROWS, HEAD_DIM), F32), pltpu.VMEM((T + PAD_ROWS, HEAD_DIM), F32)],
        args=(u_att, u_att, u_att, bias, do))


LRU_ROWS = 256
HALO = 8


def _gelu(y):
    k0 = math.sqrt(2.0 / math.pi)
    t = jnp.tanh(k0 * (y + 0.044715 * y * y * y))
    return 0.5 * y * (1.0 + t), t


def _gelu_grad(y, t):
    k0 = math.sqrt(2.0 / math.pi)
    return 0.5 * (1.0 + t) + 0.5 * y * (1.0 - t * t) * k0 * (1.0 + 3.0 * 0.044715 * y * y)


def _neg_expm1(y):
    poly = -y * (1.0 + y * (1.0 / 2 + y * (1.0 / 6 + y * (1.0 / 24 + y * (1.0 / 120 + y * (1.0 / 720 + y * (1.0 / 5040)))))))
    return jnp.where(y > -0.5, poly, 1.0 - jnp.exp(y))


def _lru_gates(ext, cw_ref, cb_ref, wr_ref, br_ref, wi_ref, bi_ref, lam_ref, rows):
    xc = cb_ref[...] + jnp.zeros((rows, BRANCH_WIDTH), F32)
    for j in range(CONV_WIDTH):
        xc = xc + ext[pl.ds(HALO - (CONV_WIDTH - 1) + j, rows), :] * cw_ref[j:j + 1, :]
    xcb = xc.astype(BF16)
    zr = jnp.concatenate([_dot(xcb[:, n * 128:(n + 1) * 128], wr_ref[n]) for n in range(4)], axis=1) + br_ref[...]
    zi = jnp.concatenate([_dot(xcb[:, n * 128:(n + 1) * 128], wi_ref[n]) for n in range(4)], axis=1) + bi_ref[...]
    r = _sigmoid(zr)
    gi = _sigmoid(zi)
    ls = _log_sigmoid(lam_ref[...])
    la = LRU_C * r * ls
    a = jnp.exp(la)
    mult = jnp.sqrt(_neg_expm1(2.0 * la))
    return xc, xcb, r, gi, ls, a, mult


def _lru_param_specs():
    full2 = lambda s: pl.BlockSpec(s, lambda i: (0, 0))
    full3 = lambda s: pl.BlockSpec(s, lambda i: (0, 0, 0))
    return [full2((8, BRANCH_WIDTH)), full2((1, BRANCH_WIDTH)), full3((4, 128, 128)), full2((1, BRANCH_WIDTH)),
            full3((4, 128, 128)), full2((1, BRANCH_WIDTH)), full2((1, BRANCH_WIDTH))]


def _lru_fwd(u_rec, p, *, name):
    T = u_rec.shape[0]
    R = min(LRU_ROWS, T)
    nb = T // R
    W = BRANCH_WIDTH
    hb = R // HALO

    def kern(rx_ref, halo_ref, ry_ref, cw_ref, cb_ref, wr_ref, br_ref, wi_ref, bi_ref, lam_ref,
             o_ref, h_ref, ext, a_s, b_s, hc):
        i = pl.program_id(0)

        @pl.when(i == 0)
        def _():
            hc[...] = jnp.zeros_like(hc)

        ext[0:HALO, :] = jnp.where(i == 0, 0.0, halo_ref[...])
        ext[HALO:, :] = rx_ref[...]
        xc, _, r, gi, ls, a, mult = _lru_gates(ext, cw_ref, cb_ref, wr_ref, br_ref, wi_ref, bi_ref, lam_ref, R)
        a_s[...] = a
        b_s[...] = mult * (gi * xc)

        def body(t, h):
            h = a_s[pl.ds(t, 1), :] * h + b_s[pl.ds(t, 1), :]
            h_ref[pl.ds(t, 1), :] = h
            return h

        h = lax.fori_loop(0, R, body, hc[0:1, :], unroll=8)
        hc[...] = jnp.broadcast_to(h, hc.shape)
        g, _ = _gelu(ry_ref[...])
        o_ref[...] = (h_ref[...] * g).astype(o_ref.dtype)

    return pl.pallas_call(
        kern, name=name, grid=(nb,),
        in_specs=[pl.BlockSpec((R, W), lambda i: (i, 0)),
                  pl.BlockSpec((HALO, W), lambda i: (jnp.maximum(i * hb - 1, 0), 0)),
                  pl.BlockSpec((R, W), lambda i: (i, 1))] + _lru_param_specs(),
        out_specs=[pl.BlockSpec((R, W), lambda i: (i, 0)), pl.BlockSpec((R, W), lambda i: (i, 0))],
        out_shape=[jax.ShapeDtypeStruct((T, W), BF16), jax.ShapeDtypeStruct((T, W), F32)],
        scratch_shapes=[pltpu.VMEM((R + HALO, W), F32), pltpu.VMEM((R, W), F32), pltpu.VMEM((R, W), F32),
                        pltpu.VMEM((8, W), F32)],
        compiler_params=_cp(("arbitrary",)),
    )(u_rec, u_rec, u_rec, *p)


def _lru_bwd(u_rec, hs, do, p, *, name):
    T = u_rec.shape[0]
    R = min(LRU_ROWS, T)
    nb = T // R
    W = BRANCH_WIDTH
    hb = R // HALO

    def kern(rx_ref, halo_ref, ry_ref, h_ref, hh_ref, do_ref, cw_ref, cb_ref, wr_ref, br_ref, wi_ref, bi_ref, lam_ref,
             drx_ref, dry_ref, dcw_ref, dcb_ref, dwr_ref, dbr_ref, dwi_ref, dbi_ref, dlam_ref,
             ext, hext, a_s, g_s, dext, gc):
        s = pl.program_id(0)
        first_block = s == nb - 1

        @pl.when(s == 0)
        def _():
            gc[...] = jnp.zeros_like(gc)
            dext[R:, :] = jnp.zeros((HALO, W), F32)
            for ref in (dcw_ref, dcb_ref, dwr_ref, dbr_ref, dwi_ref, dbi_ref, dlam_ref):
                ref[...] = jnp.zeros_like(ref)

        ext[0:HALO, :] = jnp.where(first_block, 0.0, halo_ref[...])
        ext[HALO:, :] = rx_ref[...]
        hext[0:HALO, :] = jnp.where(first_block, 0.0, hh_ref[...])
        hext[HALO:, :] = h_ref[...]
        xc, xcb, r, gi, ls, a, mult = _lru_gates(ext, cw_ref, cb_ref, wr_ref, br_ref, wi_ref, bi_ref, lam_ref, R)
        ry = ry_ref[...]
        gel, th = _gelu(ry)
        dov = do_ref[...].astype(F32)
        dry_ref[...] = (dov * h_ref[...] * _gelu_grad(ry, th)).astype(dry_ref.dtype)
        a_s[...] = a
        g_s[...] = dov * gel

        def body(tt, g):
            t = R - 1 - tt
            dh = g_s[pl.ds(t, 1), :] + g
            g_s[pl.ds(t, 1), :] = dh
            return a_s[pl.ds(t, 1), :] * dh

        g = lax.fori_loop(0, R, body, gc[0:1, :], unroll=8)
        gc[...] = jnp.broadcast_to(g, gc.shape)
        dh = g_s[...]
        hprev = hext[pl.ds(HALO - 1, R), :]
        da = dh * hprev
        gx = gi * xc
        dmult = dh * gx
        dgx = dh * mult
        dgi = dgx * xc
        dxc = dgx * gi
        dla = da * a - dmult * (a * a) / mult
        dr = dla * (LRU_C * ls)
        dlam_ref[...] += jnp.sum(dla * (LRU_C * r), axis=0, keepdims=True)
        dzr = dr * r * (1.0 - r)
        dzi = dgi * gi * (1.0 - gi)
        dbr_ref[...] += jnp.sum(dzr, axis=0, keepdims=True)
        dbi_ref[...] += jnp.sum(dzi, axis=0, keepdims=True)
        dzrb = dzr.astype(BF16)
        dzib = dzi.astype(BF16)
        back = []
        for n in range(4):
            sl = slice(n * 128, (n + 1) * 128)
            dwr_ref[n] += _dot(xcb[:, sl], dzrb[:, sl], TN)
            dwi_ref[n] += _dot(xcb[:, sl], dzib[:, sl], TN)
            back.append(_dot(dzrb[:, sl], wr_ref[n], NT) + _dot(dzib[:, sl], wi_ref[n], NT))
        dxc = dxc + jnp.concatenate(back, axis=1)
        dcb_ref[...] += jnp.sum(dxc, axis=0, keepdims=True)
        for j in range(CONV_WIDTH):
            dcw_ref[j:j + 1, :] += jnp.sum(dxc * ext[pl.ds(HALO - (CONV_WIDTH - 1) + j, R), :], axis=0, keepdims=True)
        dext[0:R, :] = dxc
        drx = jnp.zeros((R, W), F32)
        for j in range(CONV_WIDTH):
            drx = drx + dext[pl.ds(CONV_WIDTH - 1 - j, R), :] * cw_ref[j:j + 1, :]
        drx_ref[...] = drx.astype(drx_ref.dtype)
        dext[R:, :] = dxc[0:HALO, :]

        @pl.when(s == nb - 1)
        def _():
            dlam_ref[...] = dlam_ref[...] * _sigmoid(-lam_ref[...])

    rev = lambda c: pl.BlockSpec((R, W), lambda s: (nb - 1 - s, c))
    halo = lambda: pl.BlockSpec((HALO, W), lambda s: (jnp.maximum((nb - 1 - s) * hb - 1, 0), 0))
    v2 = lambda shp: pl.BlockSpec(shp, lambda s: (0, 0))
    v3 = lambda shp: pl.BlockSpec(shp, lambda s: (0, 0, 0))
    return pl.pallas_call(
        kern, name=name, grid=(nb,),
        in_specs=[rev(0), halo(), rev(1), rev(0), halo(), rev(0)] + _lru_param_specs(),
        out_specs=[rev(0), rev(0), v2((8, W)), v2((1, W)), v3((4, 128, 128)), v2((1, W)), v3((4, 128, 128)),
                   v2((1, W)), v2((1, W))],
        out_shape=[jax.ShapeDtypeStruct((T, W), BF16), jax.ShapeDtypeStruct((T, W), BF16),
                   jax.ShapeDtypeStruct((8, W), F32), jax.ShapeDtypeStruct((1, W), F32),
                   jax.ShapeDtypeStruct((4, 128, 128), F32), jax.ShapeDtypeStruct((1, W), F32),
                   jax.ShapeDtypeStruct((4, 128, 128), F32), jax.ShapeDtypeStruct((1, W), F32),
                   jax.ShapeDtypeStruct((1, W), F32)],
        scratch_shapes=[pltpu.VMEM((R + HALO, W), F32), pltpu.VMEM((R + HALO, W), F32), pltpu.VMEM((R, W), F32),
                        pltpu.VMEM((R, W), F32), pltpu.VMEM((R + HALO, W), F32), pltpu.VMEM((8, W), F32)],
        compiler_params=_cp(("arbitrary",)),
    )(u_rec, u_rec, u_rec, hs, hs, do, *p)


def _merge_fwd(o_all, wb, gate, *, name):
    T = o_all.shape[1]
    D = D_MODEL
    bm = _pick(T, (1024, 512, 256, 128))
    bn = 1024
    nj = D // bn

    def kern(o_ref, w_ref, g_ref, m_ref, pb_ref, acc):
        g = pl.program_id(2)
        pbv = _dot(o_ref[...], w_ref[...])
        pb_ref[...] = pbv.astype(pb_ref.dtype)
        term = g_ref[...].astype(F32) * pbv

        @pl.when(g == 0)
        def _():
            acc[...] = term

        @pl.when(g > 0)
        def _():
            acc[...] += term

        @pl.when(g == N_BRANCH - 1)
        def _():
            m_ref[...] = acc[...].astype(m_ref.dtype)

    return pl.pallas_call(
        kern, name=name, grid=(T // bm, nj, N_BRANCH),
        in_specs=[pl.BlockSpec((None, bm, BRANCH_WIDTH), lambda i, j, g: (g, i, 0)),
                  pl.BlockSpec((None, BRANCH_WIDTH, bn), lambda i, j, g: (g, 0, j)),
                  pl.BlockSpec((bm, bn), lambda i, j, g: (i, g * nj + j))],
        out_specs=[pl.BlockSpec((bm, bn), lambda i, j, g: (i, j)),
                   pl.BlockSpec((bm, bn), lambda i, j, g: (i, g * nj + j))],
        out_shape=[jax.ShapeDtypeStruct((T, D), BF16), jax.ShapeDtypeStruct((T, N_BRANCH * D), BF16)],
        scratch_shapes=[pltpu.VMEM((bm, bn), F32)],
        compiler_params=_cp(("parallel", "parallel", "arbitrary")),
    )(o_all, wb, gate)


def _merge_bwd(dm, gate, pb, *, name):
    T = dm.shape[0]
    D = D_MODEL
    bt = _pick(T, (256, 128))

    def kern(dm_ref, g_ref, pb_ref, dpb_ref, dzg_ref, dbg_ref):
        i = pl.program_id(1)
        dmv = dm_ref[...]
        gv = g_ref[...].astype(F32)
        dpb_ref[...] = (dmv * gv).astype(dpb_ref.dtype)
        dzg = dmv * pb_ref[...].astype(F32) * gv * (1.0 - gv)
        dzg_ref[...] = dzg.astype(dzg_ref.dtype)
        part = jnp.sum(dzg, axis=0, keepdims=True)

        @pl.when(i == 0)
        def _():
            dbg_ref[...] = part

        @pl.when(i > 0)
        def _():
            dbg_ref[...] += part

    return pl.pallas_call(
        kern, name=name, grid=(N_BRANCH, T // bt),
        in_specs=[pl.BlockSpec((bt, D), lambda g, i: (i, 0)),
                  pl.BlockSpec((bt, D), lambda g, i: (i, g)),
                  pl.BlockSpec((bt, D), lambda g, i: (i, g))],
        out_specs=[pl.BlockSpec((None, bt, D), lambda g, i: (g, i, 0)),
                   pl.BlockSpec((bt, D), lambda g, i: (i, g)),
                   pl.BlockSpec((1, D), lambda g, i: (0, g))],
        out_shape=[jax.ShapeDtypeStruct((N_BRANCH, T, D), BF16), jax.ShapeDtypeStruct((T, N_BRANCH * D), BF16),
                   jax.ShapeDtypeStruct((1, N_BRANCH * D), F32)],
        compiler_params=_cp(("parallel", "arbitrary")),
    )(dm, gate, pb)


def _col_split(M, N, bm, bn):
    per = N // N_CHIPS // bn
    return (N_CHIPS, M, N // N_CHIPS), (None, bm, bn), lambda i, j: (j // per, i, j % per)


def _pad_lanes(v, n):
    return jnp.pad(v, [(0, 0)] * (v.ndim - 1) + [(0, n - v.shape[-1])])


def _rows8(v):
    return jnp.pad(v, ((0, 8 - v.shape[0]), (0, 0)))


def _device_step(x, tgt, W, hooks=None):
    hooks = hooks or {}

    def carried(fn, key, *args, **kw):
        hook = hooks.get(key)
        outs, extra = fn(*args, name=key[0], carry=hook.spec(W, G) if hook else None, **kw)
        if hook:
            hook.done(extra, W, G)
        return outs

    def mm_carried(key, *args, **kw):
        hook = hooks.get(key)
        if hook is None:
            return _mm(*args, name=key[0], **kw)
        res, extra = _mm(*args, name=key[0], carry=hook.spec(W, G), **kw)
        hook.done(extra, W, G)
        return res

    T = x.shape[0]
    _, bk = _att_blocks(T)
    H = N_HEADS
    G = {}
    saved = []

    xf, xb = _ln_fwd(x, W['ln_in_g'], W['ln_in_b'], name='ln_in_fwd')
    for l in range(DEPTH):
        w_att, w_rec = W['w_att'][l], W['w_rec'][l]
        u_att = _mm(xb, w_att, name='in_proj_att', out_dtypes=(BF16,))
        u_rec = _mm(xb, w_rec, name='in_proj_rec', out_dtypes=(F32,))
        ffl = u_rec[:, 2 * BRANCH_WIDTH:]
        bf = _pad_lanes(W['b_forget'][l].reshape(1, H), LANES)
        Fc = _forget_fwd(ffl, bf, name='forget_fwd')
        Fh = Fc[:, :H].T
        frow = Fh.reshape(H, T // bk, 1, bk)
        o_fox, lse = carried(_fox_fwd, ('fox_fwd', l), u_att, Fc, frow)
        lp = (_rows8(W['conv_w'][l]), W['conv_b'][l].reshape(1, -1), W['w_r'][l].astype(BF16),
              W['b_r'][l].reshape(1, -1), W['w_i'][l].astype(BF16), W['b_i'][l].reshape(1, -1),
              W['lru_lambda'][l].reshape(1, -1))
        o_lru, hs = _lru_fwd(u_rec, lp, name='lru_fwd')
        o_sb, = carried(_sb_fwd, ('sb_fwd', l), u_att)
        table = _rows8(_pad_lanes(W['rel_bias'][l], REL_PAD))
        bias = _band_bias(table, name='band_bias').transpose(1, 0, 2)[:H]
        o_ch, = carried(_chunk_fwd, ('chunk_fwd', l), u_att, bias)
        o_all = jnp.stack([o_fox, o_lru, o_sb, o_ch])
        gate = _mm(xb, W['w_gate_cat'][l], name='gate_proj', out_dtypes=(BF16,),
                   extras=[(W['b_gate'][l].reshape(1, -1), 'n')],
                   epilogue=lambda acc, b: (_sigmoid(acc + b),))
        merged, pb = _merge_fwd(o_all, W['w_branch'][l], gate, name='merge_fwd')
        h1 = _mm(merged, W['w_out'][l], name='out_proj', extras=[(xf, 'mn')],
                 epilogue=lambda acc, xr: (ALPHA * xr + acc,))
        xmf, xmb = _ln_fwd(h1, W['ln1_g'][l], W['ln1_b'][l], name='ln_fwd')
        hid, ra = _mm(xmb, W['w_ff1'], name='ff1', out_dtypes=(BF16, BF16), bn=1024, bk=FF_SHARD,
                      b_view=(D_MODEL, D_FF, (None, None, FF_SHARD, 1024), lambda i, j, k: (j // 2, l, 0, j % 2)),
                      epilogue=lambda acc: (jnp.square(jnp.maximum(acc, 0.0)), jnp.maximum(acc, 0.0)))
        h2 = _mm(hid, W['w_ff2'], name='ff2', extras=[(xmf, 'mn')], bn=1024, bk=FF_SHARD,
                 b_view=(D_FF, D_MODEL, (None, None, FF_SHARD, 1024), lambda i, j, k: (k, l, 0, j)),
                 epilogue=lambda acc, xr: (ALPHA * xr + acc,))
        saved.append(dict(xb=xb, u_att=u_att, u_rec=u_rec, ffl=ffl, bf=bf, fcum=Fc, frow=frow, lse=lse, lp=lp,
                          hs=hs, bias=bias, o_all=o_all, gate=gate, merged=merged, pb=pb, h1=h1, xmb=xmb,
                          hid=hid, ra=ra, h2=h2))
        xf, xb = _ln_fwd(h2, W['ln2_g'][l], W['ln2_b'][l], name='ln_fwd')

    dx, loss_tile = _loss_head(xf, tgt, name='loss_head')
    loss = loss_tile[0, 0]

    for l in reversed(range(DEPTH)):
        S = saved[l]
        dh2, dh2b, G[('ln2_g', l)], G[('ln2_b', l)] = _ln_bwd(S['h2'], dx, W['ln2_g'][l], name='ln_bwd')
        da = _mm(dh2b, W['w_ff2'], tb=True, name='ff2_dx', out_dtypes=(BF16,), extras=[(S['ra'], 'mn')],
                 bn=1024, bk=FF_SHARD,
                 b_view=(D_MODEL, D_FF, (None, None, 1024, FF_SHARD), lambda i, j, k: (j // 2, l, j % 2, 0)),
                 epilogue=lambda acc, rav: (acc * (2.0 * rav.astype(F32)),))
        G[('w_ff2', l)] = _mm(S['hid'], dh2b, ta=True, name='ff2_dw').reshape(N_CHIPS, D_FF // N_CHIPS, D_MODEL)
        G[('w_ff1', l)] = _mm(S['xmb'], da, ta=True, name='ff1_dw', bm=1024, bn=1024,
                              out_map=_col_split(D_MODEL, D_FF, 1024, 1024))
        dxm = _mm(da, W['w_ff1'], tb=True, name='ff1_dx', extras=[(dh2, 'mn')], bn=1024, bk=FF_SHARD,
                  b_view=(D_FF, D_MODEL, (None, None, 1024, FF_SHARD), lambda i, j, k: (k, l, j, 0)),
                  epilogue=lambda acc, d: (ALPHA * d + acc,))
        dh1, dh1b, G[('ln1_g', l)], G[('ln1_b', l)] = _ln_bwd(S['h1'], dxm, W['ln1_g'][l], name='ln_bwd')
        dm = _mm(dh1b, W['w_out'][l], tb=True, name='out_dx')
        G[('w_out', l)] = _mm(S['merged'], dh1b, ta=True, name='out_dw').reshape(
            N_CHIPS, D_MODEL // N_CHIPS, D_MODEL)
        dpb, dzg, G[('b_gate', l)] = _merge_bwd(dm, S['gate'], S['pb'], name='merge_bwd')
        do = [_mm(dpb[g], W['w_branch'][l][g], tb=True, name='branch_dx', out_dtypes=(BF16,)) for g in range(N_BRANCH)]
        G[('w_branch', l)] = jnp.stack(
            [_mm(S['o_all'][g], dpb[g], ta=True, name='branch_dw', bm=BRANCH_WIDTH, bn=BRANCH_WIDTH,
                 out_map=_col_split(BRANCH_WIDTH, D_MODEL, BRANCH_WIDTH, BRANCH_WIDTH))
             for g in range(N_BRANCH)], axis=1)
        G[('w_gate', l)] = _mm(S['xb'], dzg, ta=True, name='gate_dw', bm=1024, bn=1024,
                               out_map=((N_CHIPS, N_BRANCH, D_MODEL // N_CHIPS, D_MODEL),
                                        (2, None, D_MODEL // N_CHIPS, 1024),
                                        lambda i, j: (i, j // 2, 0, j % 2)))
        u_att, u_rec = S['u_att'], S['u_rec']
        delta = _row_dot(do[0], S['o_all'][0], name='row_dot')
        fdq, fdk, fdv, dfk, dfq = carried(_fox_bwd, ('fox_bwd', l), u_att, do[0], S['lse'], delta, S['fcum'],
                                          S['frow'])
        dff, dbf = _forget_bwd(_pad_lanes(dfk.reshape(H, T).T, LANES), _pad_lanes(dfq.reshape(H, T).T, LANES),
                               S['ffl'], S['bf'], name='forget_bwd')
        G[('b_forget', l)] = dbf[0, :H]
        (drx, dry, dcw, dcb, G[('w_r', l)], dbr, G[('w_i', l)], dbi, dlam) = _lru_bwd(
            u_rec, S['hs'], do[1], S['lp'], name='lru_bwd')
        G[('conv_w', l)], G[('conv_b', l)] = dcw[:CONV_WIDTH], dcb[0]
        G[('b_r', l)], G[('b_i', l)], G[('lru_lambda', l)] = dbr[0], dbi[0], dlam[0]
        sdq, sdk, sdv = carried(_sb_bwd, ('sb_bwd', l), u_att, do[2])
        cdq, cdk, cdv, dbias = carried(_chunk_bwd, ('chunk_bwd', l), u_att, S['bias'], do[3])
        dtab = _band_bias_bwd(jnp.pad(dbias, ((0, 8 - H), (0, 0), (0, 0))).transpose(1, 0, 2), name='band_bias_bwd')
        G[('rel_bias', l)] = dtab[:H, :REL_TABLE]
        du_att = jnp.concatenate([fdq, fdk, fdv, sdq, sdk, sdv, cdq, cdk, cdv], axis=1)
        du_rec = jnp.concatenate([drx, dry, dff], axis=1)
        G[('w_att', l)] = _mm(S['xb'], du_att, ta=True, name='in_att_dw')
        G[('w_rec', l)] = _mm(S['xb'], du_rec, ta=True, name='in_rec_dw')
        t1 = mm_carried(('gate_dx', l), dzg, W['w_gate_cat'][l], tb=True, extras=[(dh1, 'mn')],
                        epilogue=lambda acc, d: (ALPHA * d + acc,))
        t2 = mm_carried(('in_att_dx', l), du_att, W['w_att'][l], tb=True, extras=[(t1, 'mn')],
                        epilogue=lambda acc, d: (d + acc,))
        dx = _mm(du_rec, W['w_rec'][l], tb=True, name='in_rec_dx', extras=[(t2, 'mn')],
                 epilogue=lambda acc, d: (d + acc,))

    gx, _, G[('ln_in_g', -1)], G[('ln_in_b', -1)] = _ln_bwd(x, dx, W['ln_in_g'], name='ln_in_bwd')
    return loss, gx, G


_IN_FQKV = (0, 1536)
_IN_FF = (1536, 1540)
_IN_REC = (1540, 2564)
_IN_REST = (2564, D_IN)


def _prep_weights(full, W=None):
    W = {} if W is None else W
    for n, a in full.items():
        if n == 'w_in':
            L = a.shape[0]
            W['w_att'] = jnp.concatenate([a[..., _IN_FQKV[0]:_IN_FQKV[1]], a[..., _IN_REST[0]:_IN_REST[1]]],
                                         -1).astype(BF16)
            W['w_rec'] = jnp.concatenate([a[..., _IN_REC[0]:_IN_REC[1]], a[..., _IN_FF[0]:_IN_FF[1]],
                                          jnp.zeros((L, D_MODEL, N_REC - 1024 - N_HEADS), a.dtype)], -1).astype(BF16)
        elif n == 'w_gate':
            W['w_gate_cat'] = a.transpose(0, 2, 1, 3).reshape(a.shape[0], D_MODEL, N_BRANCH * D_MODEL).astype(BF16)
        elif n == 'b_gate':
            W['b_gate'] = a.reshape(a.shape[0], N_BRANCH * D_MODEL)
        elif n == 'w_ff1' and a.ndim == 3:
            W[n] = a.reshape(a.shape[0], D_MODEL, 4, FF_SHARD).transpose(2, 0, 1, 3).astype(BF16)
        elif n == 'w_ff2' and a.ndim == 3:
            W[n] = a.reshape(a.shape[0], 4, FF_SHARD, D_MODEL).transpose(1, 0, 2, 3).astype(BF16)
        elif n in ('w_branch', 'w_out', 'w_ff1', 'w_ff2'):
            W[n] = a.astype(BF16)
        else:
            W[n] = a
    return W


def _grads_to_reference_layout(G):
    out = {'ln_in_g': G[('ln_in_g', -1)][0], 'ln_in_b': G[('ln_in_b', -1)][0]}
    st = lambda n: jnp.stack([G[(n, l)] for l in range(DEPTH)])
    g_att, g_rec = st('w_att'), st('w_rec')
    out['w_in'] = jnp.concatenate([g_att[..., :1536], g_rec[..., 1024:1024 + N_HEADS], g_rec[..., :1024],
                                   g_att[..., 1536:]], -1)
    out['w_gate'] = st('w_gate').transpose(0, 2, 1, 3, 4).reshape(DEPTH, N_BRANCH, D_MODEL, D_MODEL)
    out['w_branch'] = st('w_branch').transpose(0, 2, 3, 1, 4).reshape(DEPTH, N_BRANCH, BRANCH_WIDTH, D_MODEL)
    out['w_ff1'] = st('w_ff1').transpose(0, 2, 1, 3).reshape(DEPTH, D_MODEL, D_FF)
    out['w_ff2'] = st('w_ff2').reshape(DEPTH, D_FF, D_MODEL)
    out['w_out'] = st('w_out').reshape(DEPTH, D_MODEL, D_MODEL)
    out['b_gate'] = st('b_gate').reshape(DEPTH, N_BRANCH, D_MODEL)
    for n in ('ln1_g', 'ln1_b', 'ln2_g', 'ln2_b'):
        out[n] = st(n)[:, 0]
    for n in ('b_forget', 'conv_w', 'conv_b', 'w_r', 'b_r', 'w_i', 'b_i', 'lru_lambda', 'rel_bias'):
        out[n] = st(n)
    return out


HBM_SPEC = pl.BlockSpec(memory_space=pl.ANY)
N_CHIPS = 4
PACK_COLS = 1024


def _place():
    x, y, c = lax.axis_index("x"), lax.axis_index("y"), lax.axis_index("c")
    chips = [(1 - x, y), (x, 1 - y), (1 - x, 1 - y)]
    return x, y, c, chips


def _remote(src, dst, send_sems, recv_sems, k, to):
    return pltpu.make_async_remote_copy(src_ref=src, dst_ref=dst, send_sem=send_sems.at[k], recv_sem=recv_sems.at[k],
                                        device_id=to, device_id_type=MESH)


class _Exchange:
    def __init__(self, ins, out_shapes, n_sems, start, finish, mid=None):
        self.ins, self.out_shapes, self.n_sems = list(ins), list(out_shapes), n_sems
        self.start, self.mid, self.finish = start, mid, finish


def _gather_spec(params):
    n = len(params)

    def start(ins, outs, ss, rs):
        x, y, c, chips = _place()
        for p in range(n):
            _remote(ins[p], outs[p].at[2 * x + y], ss, rs, 6 * n + p, (x, y, 1 - c)).start()
            for j, (cx, cy) in enumerate(chips):
                _remote(ins[p].at[c], outs[p].at[2 * x + y, c], ss, rs, 6 * p + j, (cx, cy, c)).start()

    def mid(ins, outs, ss, rs):
        x, y, c, chips = _place()
        for p in range(n):
            for j, (cx, cy) in enumerate(chips):
                blk = outs[p].at[2 * cx + cy, c]
                _remote(blk, blk, ss, rs, 6 * p + j, (x, y, c)).wait_recv()
                _remote(blk, blk, ss, rs, 6 * p + 3 + j, (x, y, 1 - c)).start()

    def finish(ins, outs, ss, rs):
        x, y, c, chips = _place()
        me = (x, y, c)
        for p in range(n):
            for j, (cx, cy) in enumerate(chips):
                theirs = outs[p].at[2 * cx + cy, 1 - c]
                _remote(theirs, theirs, ss, rs, 6 * p + 3 + j, me).wait_recv()
        for p in range(n):
            for j, (cx, cy) in enumerate(chips):
                _remote(ins[p].at[c], outs[p].at[2 * x + y, c], ss, rs, 6 * p + j, me).wait_send()
                blk = outs[p].at[2 * cx + cy, c]
                _remote(blk, blk, ss, rs, 6 * p + 3 + j, me).wait_send()
            _remote(ins[p], outs[p].at[2 * x + y], ss, rs, 6 * n + p, me).wait()

    shapes = [jax.ShapeDtypeStruct((N_CHIPS,) + a.shape, a.dtype) for a in params]
    return _Exchange(params, shapes, 7 * n, start, finish, mid)


def _pair_spec(g0, g1):
    n = len(g0)

    def start(ins, outs, ss, rs):
        x, y, c, _ = _place()

        @pl.when(c == 0)
        def _():
            for p in range(n):
                _remote(ins[n + p], outs[p], ss, rs, p, (x, y, 1 - c)).start()

        @pl.when(c == 1)
        def _():
            for p in range(n):
                _remote(ins[p], outs[p], ss, rs, p, (x, y, 1 - c)).start()

    def finish(ins, outs, ss, rs):
        x, y, c, _ = _place()
        for p in range(n):
            _remote(ins[p], outs[p], ss, rs, p, (x, y, 1 - c)).wait()

    return _Exchange(list(g0) + list(g1), [jax.ShapeDtypeStruct(a.shape, a.dtype) for a in g0], n, start, finish)


def _chip_spec(s):
    n = len(s)

    def start(ins, outs, ss, rs):
        x, y, c, chips = _place()
        for p in range(n):
            for j, (cx, cy) in enumerate(chips):
                _remote(ins[p].at[2 * cx + cy], outs[p].at[2 * x + y], ss, rs, 3 * p + j, (cx, cy, c)).start()

    def finish(ins, outs, ss, rs):
        x, y, c, chips = _place()
        for p in range(n):
            for j, (cx, cy) in enumerate(chips):
                slot = outs[p].at[2 * cx + cy]
                _remote(slot, slot, ss, rs, 3 * p + j, (x, y, c)).wait_recv()
        for p in range(n):
            for j, (cx, cy) in enumerate(chips):
                _remote(ins[p].at[2 * cx + cy], outs[p].at[2 * x + y], ss, rs, 3 * p + j, (x, y, c)).wait_send()

    return _Exchange(s, [jax.ShapeDtypeStruct(a.shape, a.dtype) for a in s], 3 * n, start, finish)


def _exchange(ex, *, name):
    ni, no = len(ex.ins), len(ex.out_shapes)

    def body(*refs):
        ins, outs = refs[:ni], refs[ni:ni + no]
        ss, rs = refs[ni + no:]
        ex.start(ins, outs, ss, rs)
        if ex.mid is not None:
            ex.mid(ins, outs, ss, rs)
        ex.finish(ins, outs, ss, rs)

    return list(pl.pallas_call(
        body, name=name, in_specs=[HBM_SPEC] * ni, out_specs=[HBM_SPEC] * no, out_shape=ex.out_shapes,
        scratch_shapes=[pltpu.SemaphoreType.DMA((ex.n_sems,)), pltpu.SemaphoreType.DMA((ex.n_sems,))],
    )(*ex.ins))


def _call_with_carry(kern, *, name, grid, in_specs, out_specs, out_shape, scratch_shapes, args, carry=None,
                     semantics=("parallel", "arbitrary")):
    out_specs, out_shape = list(out_specs), list(out_shape)
    if carry is None:
        res = pl.pallas_call(kern, name=name, grid=grid, in_specs=in_specs, out_specs=out_specs, out_shape=out_shape,
                             scratch_shapes=scratch_shapes, compiler_params=_cp(semantics))(*args)
        return list(res), []
    ni, no, ns = len(in_specs), len(out_specs), len(scratch_shapes)
    ci, co = len(carry.ins), len(carry.out_shapes)

    def wrapped(*refs):
        ins, cins = refs[:ni], refs[ni:ni + ci]
        outs, couts = refs[ni + ci:ni + ci + no], refs[ni + ci + no:ni + ci + no + co]
        scratch = refs[ni + ci + no + co:ni + ci + no + co + ns]
        ss, rs = refs[-2:]
        ids = [pl.program_id(d) for d in range(len(grid))]
        at = lambda pos: functools.reduce(lambda p, q: p & q, [i == v for i, v in zip(ids, pos)])

        @pl.when(at([0] * len(grid)))
        def _():
            carry.start(cins, couts, ss, rs)

        kern(*ins, *outs, *scratch)

        if carry.mid is not None:
            @pl.when(at([grid[0] - 1, grid[1] // 2] + [0] * (len(grid) - 2)))
            def _():
                carry.mid(cins, couts, ss, rs)

        @pl.when(at([g - 1 for g in grid]))
        def _():
            carry.finish(cins, couts, ss, rs)

    res = pl.pallas_call(
        wrapped, name=name, grid=grid, in_specs=list(in_specs) + [HBM_SPEC] * ci,
        out_specs=out_specs + [HBM_SPEC] * co, out_shape=out_shape + carry.out_shapes,
        scratch_shapes=list(scratch_shapes) + [pltpu.SemaphoreType.DMA((carry.n_sems,)),
                                               pltpu.SemaphoreType.DMA((carry.n_sems,))],
        compiler_params=_cp(("arbitrary",) * len(grid)))(*args, *carry.ins)
    return list(res[:no]), list(res[no:])


def _pair_swap(r, *, name):
    n = len(r)

    def body(*refs):
        ins, outs = refs[:n], refs[n:2 * n]
        send_sems, recv_sems = refs[2 * n:]
        x, y, c, _ = _place()
        cps = [_remote(ins[p], outs[p], send_sems, recv_sems, p, (x, y, 1 - c)) for p in range(n)]
        for cp in cps:
            cp.start()
        for cp in cps:
            cp.wait()

    return pl.pallas_call(
        body, name=name, in_specs=[HBM_SPEC] * n, out_specs=[HBM_SPEC] * n,
        out_shape=[jax.ShapeDtypeStruct(a.shape, a.dtype) for a in r],
        scratch_shapes=[pltpu.SemaphoreType.DMA((n,)), pltpu.SemaphoreType.DMA((n,))],
    )(*r)


def _gather8_spec(v):
    R, C = v.shape
    flips = [(bx, by, bc) for bx in (0, 1) for by in (0, 1) for bc in (0, 1)][1:]
    flip = lambda a_, b_: 1 - a_ if b_ else a_

    def start(ins, outs, ss, rs):
        x, y, c, _ = _place()
        mine = outs[0].at[4 * x + 2 * y + c]
        pltpu.make_async_copy(ins[0], mine, ss.at[7]).start()
        for j, (bx, by, bc) in enumerate(flips):
            _remote(ins[0], mine, ss, rs, j, (flip(x, bx), flip(y, by), flip(c, bc))).start()

    def finish(ins, outs, ss, rs):
        x, y, c, _ = _place()
        mine = outs[0].at[4 * x + 2 * y + c]
        for j, (bx, by, bc) in enumerate(flips):
            slot = outs[0].at[4 * flip(x, bx) + 2 * flip(y, by) + flip(c, bc)]
            _remote(slot, slot, ss, rs, j, (x, y, c)).wait_recv()
        for j in range(7):
            _remote(ins[0], mine, ss, rs, j, (x, y, c)).wait_send()
        pltpu.make_async_copy(ins[0], mine, ss.at[7]).wait()

    return _Exchange([v], [jax.ShapeDtypeStruct((8, R, C), v.dtype)], 8, start, finish)


def _row_block(rows, cols, limit=256 * 1024):
    if rows * cols <= limit:
        return rows
    for br in range(limit // cols // 8 * 8, 0, -8):
        if rows % br == 0:
            return br
    return rows


def _sum_slots(buf, *, name):
    n, R, C = buf.shape
    br = _row_block(R, C, limit=64 * 1024)

    def kern(b_ref, o_ref):
        acc = b_ref[0].astype(F32)
        for s in range(1, n):
            acc = acc + b_ref[s].astype(F32)
        o_ref[...] = acc

    return pl.pallas_call(
        kern, name=name, grid=(pl.cdiv(R, br),),
        in_specs=[pl.BlockSpec((n, br, C), lambda i: (0, i, 0))],
        out_specs=pl.BlockSpec((br, C), lambda i: (i, 0)),
        out_shape=jax.ShapeDtypeStruct((R, C), F32),
        compiler_params=_cp(("arbitrary",)),
    )(buf)


def _scalar(s):
    return jnp.reshape(s, (1,)).astype(jnp.int32)


def _sum_pair(g0, g1, other, c, *, name):
    _, R, C = g0.shape
    br = _row_block(R, C)

    def kern(c_ref, g0_ref, g1_ref, o_ref, out_ref):
        own = jnp.where(c_ref[0] == 0, g0_ref[...], g1_ref[...])
        out_ref[...] = (own + o_ref[...]).astype(out_ref.dtype)

    blk = (None, br, C)
    return pl.pallas_call(
        kern, name=name,
        grid_spec=pltpu.PrefetchScalarGridSpec(
            num_scalar_prefetch=1, grid=(N_CHIPS, R // br),
            in_specs=[pl.BlockSpec(blk, lambda k, i, cr: (k, i * (1 - cr[0]), 0)),
                      pl.BlockSpec(blk, lambda k, i, cr: (k, i * cr[0], 0)),
                      pl.BlockSpec(blk, lambda k, i, cr: (k, i, 0))],
            out_specs=pl.BlockSpec(blk, lambda k, i, cr: (k, i, 0))),
        out_shape=jax.ShapeDtypeStruct((N_CHIPS, R, C), BF16),
        compiler_params=_cp(("arbitrary", "arbitrary")),
    )(_scalar(c), g0, g1, other)


def _sum_chips(s, got, k, *, name):
    _, R, C = s.shape
    br = _row_block(R, C)

    def kern(k_ref, s_ref, a_ref, b_ref, c_ref, out_ref):
        out_ref[...] = ((s_ref[...].astype(F32) + a_ref[...].astype(F32)) + b_ref[...].astype(F32)) \
            + c_ref[...].astype(F32)

    blk = (None, br, C)
    peer = lambda d: pl.BlockSpec(blk, lambda i, kr: ((kr[0] + d) % N_CHIPS, i, 0))
    return pl.pallas_call(
        kern, name=name,
        grid_spec=pltpu.PrefetchScalarGridSpec(
            num_scalar_prefetch=1, grid=(R // br,),
            in_specs=[peer(0), peer(1), peer(2), peer(3)],
            out_specs=pl.BlockSpec((br, C), lambda i, kr: (i, 0))),
        out_shape=jax.ShapeDtypeStruct((R, C), F32),
        compiler_params=_cp(("arbitrary",)),
    )(_scalar(k), s, got, got, got)


def _adam_math(w, g, m, v):
    nm = ADAM_B1 * m + (1.0 - ADAM_B1) * g
    nv = ADAM_B2 * v + (1.0 - ADAM_B2) * jnp.square(g)
    m_hat = nm / (1.0 - ADAM_B1 ** ADAM_STEP)
    v_hat = nv / (1.0 - ADAM_B2 ** ADAM_STEP)
    return -ADAM_LR * (m_hat / (jnp.sqrt(v_hat) + ADAM_EPS) + ADAM_WD * w), nm, nv


def _adamw_layers(w, mine, theirs, m, v, c, *, name):
    shape = w.shape
    R, C = mine.shape
    w3, m3, v3 = (a.reshape(DEPTH, R, C) for a in (w, m, v))
    br = _row_block(R, C)

    def kern(c_ref, w_ref, a_ref, b_ref, m_ref, v_ref, g_ref, d_ref, nm_ref, nv_ref):
        g = jnp.where(pl.program_id(0) == c_ref[0], a_ref[...], b_ref[...])
        g_ref[...] = g
        d_ref[...], nm_ref[...], nv_ref[...] = _adam_math(w_ref[...], g, m_ref[...], v_ref[...])

    lay = pl.BlockSpec((None, br, C), lambda l, i, cr: (l, i, 0))
    outs = pl.pallas_call(
        kern, name=name,
        grid_spec=pltpu.PrefetchScalarGridSpec(
            num_scalar_prefetch=1, grid=(DEPTH, R // br),
            in_specs=[lay,
                      pl.BlockSpec((br, C), lambda l, i, cr: (jnp.where(l == cr[0], i, 0), 0)),
                      pl.BlockSpec((br, C), lambda l, i, cr: (jnp.where(l == cr[0], 0, i), 0)),
                      lay, lay],
            out_specs=[lay] * 4),
        out_shape=[jax.ShapeDtypeStruct((DEPTH, R, C), F32)] * 4,
        compiler_params=_cp(("arbitrary", "arbitrary")),
    )(_scalar(c), w3, mine, theirs, m3, v3)
    return [o.reshape(shape) for o in outs]


def _adamw(w, g, m, v, *, name):
    shape = w.shape
    cols = shape[-1]
    w2, g2, m2, v2 = (a.reshape(-1, cols) for a in (w, g, m, v))
    rows = w2.shape[0]
    br = _row_block(rows, cols)

    def kern(w_ref, g_ref, m_ref, v_ref, d_ref, nm_ref, nv_ref):
        gv = g_ref[...]
        nm = ADAM_B1 * m_ref[...] + (1.0 - ADAM_B1) * gv
        nv = ADAM_B2 * v_ref[...] + (1.0 - ADAM_B2) * jnp.square(gv)
        m_hat = nm / (1.0 - ADAM_B1 ** ADAM_STEP)
        v_hat = nv / (1.0 - ADAM_B2 ** ADAM_STEP)
        d_ref[...] = -ADAM_LR * (m_hat / (jnp.sqrt(v_hat) + ADAM_EPS) + ADAM_WD * w_ref[...])
        nm_ref[...] = nm
        nv_ref[...] = nv

    spec = pl.BlockSpec((br, cols), lambda i: (i, 0))
    outs = pl.pallas_call(
        kern, name=name, grid=(rows // br,), in_specs=[spec] * 4, out_specs=[spec] * 3,
        out_shape=[jax.ShapeDtypeStruct((rows, cols), F32)] * 3,
        compiler_params=_cp(("arbitrary",)),
    )(w2, g2, m2, v2)
    return [o.reshape(shape) for o in outs]


_NAMES = ['ln_in_g', 'ln_in_b', 'w_in', 'b_forget', 'conv_w', 'conv_b', 'w_r', 'b_r', 'w_i', 'b_i', 'lru_lambda',
          'rel_bias', 'w_branch', 'w_gate', 'b_gate', 'w_out', 'ln1_g', 'ln1_b', 'w_ff1', 'w_ff2', 'ln2_g', 'ln2_b']
_BIG = {'w_in': 2, 'w_branch': 3, 'w_gate': 2, 'w_out': 1, 'w_ff1': 2, 'w_ff2': 1}
_SMALL_SHARDED = {'b_gate': 2, 'conv_w': 2, 'rel_bias': 2}
_SHARDED = {**_BIG, **_SMALL_SHARDED}
_REPLICATED = [n for n in _NAMES if n not in _SHARDED]
_TILE = 8 * LANES


def _tiles(a, cols):
    flat = a.reshape(-1)
    per = 8 * cols
    flat = jnp.pad(flat, (0, (-flat.shape[0]) % per))
    return flat.reshape(-1, cols)


def _pack(arrs, cols):
    return jnp.concatenate([_tiles(a, cols) for a in arrs], axis=0)


def _unpack(packed, like, cols):
    out, r0 = [], 0
    for a in like:
        n = math.prod(a.shape)
        rows = -(-n // (8 * cols)) * 8
        out.append(packed[r0:r0 + rows].reshape(-1)[:n].reshape(a.shape))
        r0 += rows
    return out


_EARLY = ['w_branch', 'w_gate', 'w_out', 'w_ff1', 'w_ff2']


def _chip_major_early(G, l):
    return [G[('w_branch', l)].reshape(N_CHIPS, N_BRANCH * BRANCH_WIDTH, BRANCH_WIDTH),
            G[('w_gate', l)].reshape(N_CHIPS, N_BRANCH * (D_MODEL // N_CHIPS), D_MODEL),
            G[('w_out', l)], G[('w_ff1', l)], G[('w_ff2', l)]]


def _chip_major_late(G, l):
    g_att, g_rec = G[('w_att', l)], G[('w_rec', l)]
    w_in = jnp.concatenate([g_att[:, :1536], g_rec[:, 1024:1024 + N_HEADS], g_rec[:, :1024], g_att[:, 1536:]], -1)
    per_chip = lambda g, rows: g.reshape(rows, N_CHIPS, -1).transpose(1, 0, 2)
    bg = per_chip(G[('b_gate', l)], N_BRANCH)
    cw = per_chip(G[('conv_w', l)], CONV_WIDTH)
    rb = per_chip(G[('rel_bias', l)], N_HEADS)
    small = jnp.stack([_pack([bg[j], cw[j], rb[j]], LANES) for j in range(N_CHIPS)])
    return [w_in.reshape(D_MODEL, N_CHIPS, D_IN // N_CHIPS).transpose(1, 0, 2), small]


class _Hook:
    def __init__(self, spec, done):
        self.spec, self.done = spec, done


def _unshard(blocks, axis):
    return jnp.concatenate([blocks[k] for k in range(N_CHIPS)], axis=axis)


def kernel(x, ln_in_g, ln_in_b, w_in, b_forget, conv_w, conv_b, w_r, b_r, w_i, b_i, lru_lambda, rel_bias, w_branch, w_gate, b_gate, w_out, ln1_g, ln1_b, w_ff1, w_ff2, ln2_g, ln2_b, loss_target, m_ln_in_g, m_ln_in_b, m_w_in, m_b_forget, m_conv_w, m_conv_b, m_w_r, m_b_r, m_w_i, m_b_i, m_lru_lambda, m_rel_bias, m_w_branch, m_w_gate, m_b_gate, m_w_out, m_ln1_g, m_ln1_b, m_w_ff1, m_w_ff2, m_ln2_g, m_ln2_b, v_ln_in_g, v_ln_in_b, v_w_in, v_b_forget, v_conv_w, v_conv_b, v_w_r, v_b_r, v_w_i, v_b_i, v_lru_lambda, v_rel_bias, v_w_branch, v_w_gate, v_b_gate, v_w_out, v_ln1_g, v_ln1_b, v_w_ff1, v_w_ff2, v_ln2_g, v_ln2_b):
    w = dict(zip(_NAMES, (ln_in_g, ln_in_b, w_in, b_forget, conv_w, conv_b, w_r, b_r, w_i, b_i, lru_lambda, rel_bias,
                          w_branch, w_gate, b_gate, w_out, ln1_g, ln1_b, w_ff1, w_ff2, ln2_g, ln2_b)))
    m = dict(zip(_NAMES, (m_ln_in_g, m_ln_in_b, m_w_in, m_b_forget, m_conv_w, m_conv_b, m_w_r, m_b_r, m_w_i, m_b_i,
                          m_lru_lambda, m_rel_bias, m_w_branch, m_w_gate, m_b_gate, m_w_out, m_ln1_g, m_ln1_b,
                          m_w_ff1, m_w_ff2, m_ln2_g, m_ln2_b)))
    v = dict(zip(_NAMES, (v_ln_in_g, v_ln_in_b, v_w_in, v_b_forget, v_conv_w, v_conv_b, v_w_r, v_b_r, v_w_i, v_b_i,
                          v_lru_lambda, v_rel_bias, v_w_branch, v_w_gate, v_b_gate, v_w_out, v_ln1_g, v_ln1_b,
                          v_w_ff1, v_w_ff2, v_ln2_g, v_ln2_b)))
    c = lax.axis_index("c")

    k = 2 * lax.axis_index("x") + lax.axis_index("y")
    state = {}

    small_like = [w[n] for n in _SMALL_SHARDED]
    small_pack = jnp.stack([_pack([a[l] for a in small_like], LANES) for l in range(DEPTH)])
    W = _prep_weights({n: w[n] for n in _REPLICATED})
    got_in, got_small = _exchange(_gather_spec([w['w_in'].astype(BF16), small_pack]), name='gather_first')
    small_blocks = [[_unpack(got_small[j, l], [a[l] for a in small_like], LANES) for l in range(DEPTH)]
                    for j in range(N_CHIPS)]
    first = {'w_in': _unshard(got_in, _BIG['w_in'])}
    for i, n in enumerate(_SMALL_SHARDED):
        first[n] = jnp.concatenate([jnp.stack([small_blocks[j][l][i] for l in range(DEPTH)])
                                    for j in range(N_CHIPS)], axis=_SMALL_SHARDED[n])
    _prep_weights(first, W)

    def gather_on(names):
        chip_major = ('w_ff1', 'w_ff2')
        return _Hook(lambda W_, G_: _gather_spec([w[n].astype(BF16) for n in names]),
                     lambda outs, W_, G_: _prep_weights(
                         {n: o if n in chip_major else _unshard(o, _BIG[n]) for n, o in zip(names, outs)}, W_))

    def pair_spec(W_, G_):
        state['early'] = [_chip_major_early(G_, l) for l in range(DEPTH)]
        return _pair_spec(*state['early'])

    def pair_done(outs, W_, G_):
        state['pair_sum'] = [_sum_pair(a0, a1, o, c, name='grad_pair_sum')
                             for a0, a1, o in zip(*state['early'], outs)]

    def late_spec(W_, G_):
        state['late'] = [_chip_major_late(G_, l) for l in range(DEPTH)]
        return _pair_spec(*state['late'])

    def late_done(outs, W_, G_):
        state['late_sum'] = [_sum_pair(a0, a1, o, c, name='grad_pair_sum') for a0, a1, o in zip(*state['late'], outs)]

    rep_main = _REPLICATED[2:]

    def rep_spec(W_, G_):
        dev = [jnp.stack([G_[(n, l)].reshape(w[n].shape[1:]) for l in range(DEPTH)]) for n in rep_main]
        packed = _pack(dev, LANES)
        return _gather8_spec(jnp.pad(packed, ((0, (-packed.shape[0]) % 256), (0, 0))))

    hooks = {('fox_fwd', 0): gather_on(['w_gate', 'w_out']),
             ('sb_fwd', 0): gather_on(['w_ff1', 'w_ff2']),
             ('chunk_fwd', 0): gather_on(['w_branch']),
             ('chunk_bwd', 0): _Hook(rep_spec, lambda outs, W_, G_: state.update(rep_all=outs[0])),
             ('gate_dx', 0): _Hook(late_spec, late_done),
             ('in_att_dx', 0): _Hook(lambda W_, G_: _chip_spec(state['late_sum']),
                                     lambda outs, W_, G_: state.update(late_chips=outs)),
             ('fox_bwd', 0): _Hook(pair_spec, pair_done),
             ('sb_bwd', 0): _Hook(lambda W_, G_: _chip_spec(state['pair_sum']),
                                  lambda outs, W_, G_: state.update(from_chips=outs))}
    loss, gx, G = _device_step(x[0], loss_target[0], W, hooks)

    late_sum, late_chips = state['late_sum'], state['late_chips']
    pair_sum = [late_sum[0]] + state['pair_sum'] + [late_sum[1]]
    from_chips = [late_chips[0]] + state['from_chips'] + [late_chips[1]]
    mine = [_sum_chips(s, got, k, name='grad_chip_sum') for s, got in zip(pair_sum, from_chips)]
    theirs = _pair_swap(mine, name='grad_pair_swap')

    g_rep = dict(zip(rep_main, _unpack(_sum_slots(state['rep_all'], name='grad_sum8'), [w[n] for n in rep_main], LANES)))
    entry = _exchange(_gather8_spec(_pack([G[('ln_in_g', -1)][0], G[('ln_in_b', -1)][0]], LANES)),
                      name='grad_gather8')[0]
    g_rep.update(zip(_REPLICATED[:2], _unpack(_sum_slots(entry, name='grad_sum8'), [w[n] for n in _REPLICATED[:2]],
                                              LANES)))

    grads, delta, new_m, new_v = {}, {}, {}, {}
    for n, a, b in zip(_BIG, mine, theirs):
        grads[n], delta[n], new_m[n], new_v[n] = _adamw_layers(w[n], a, b, m[n], v[n], c, name='adamw')
    small_layers = [jnp.where(c == l, mine[-1], theirs[-1]) for l in range(DEPTH)]
    small_shards = [_unpack(s, [w[n][0] for n in _SMALL_SHARDED], LANES) for s in small_layers]
    g_shard = {n: jnp.stack([small_shards[l][i] for l in range(DEPTH)]) for i, n in enumerate(_SMALL_SHARDED)}
    small = _REPLICATED + list(_SMALL_SHARDED)
    for n in small:
        grads[n] = g_rep[n] if n in g_rep else g_shard[n]
    packs = [_pack([d[n] for n in small], LANES) for d in (w, grads, m, v)]
    outs = _adamw(*packs, name='adamw_small')
    small_like_all = [w[n] for n in small]
    for d, o in zip((delta, new_m, new_v), outs):
        d.update(zip(small, _unpack(o, small_like_all, LANES)))

    loss = lax.psum(loss, ("x", "y", "c"))
    return (loss, gx[None], *[grads[n] for n in _NAMES], *[delta[n] for n in _NAMES],
            *[new_m[n] for n in _NAMES], *[new_v[n] for n in _NAMES])
```

```python
import functools
import math

import jax
import jax.numpy as jnp
from jax import lax
from jax.experimental import pallas as pl
from jax.experimental.pallas import tpu as pltpu

F32 = jnp.float32
BF16 = jnp.bfloat16

D_MODEL = 2048
DEPTH = 2
CHUNK = 64
HEAD_DIM = 128
N_BRANCH = 4
BRANCH_WIDTH = 512
N_HEADS = 4
CONV_WIDTH = 4
LRU_C = 8.0
LOOKBACK_CHUNKS = 8
BAND = (LOOKBACK_CHUNKS + 1) * CHUNK
PAD_ROWS = LOOKBACK_CHUNKS * CHUNK
REL_CLIP = 256
REL_TABLE = REL_CLIP + CHUNK
REL_PAD = 384
D_FF = 4 * D_MODEL
FF_SHARD = D_FF // 4
D_IN = 5636
ALPHA = (2.0 * DEPTH) ** 0.25
LN_EPS = 1e-5
SCALE = HEAD_DIM ** -0.5

ADAM_LR = 0.001
ADAM_B1 = 0.9
ADAM_B2 = 0.999
ADAM_EPS = 1e-08
ADAM_WD = 0.01
ADAM_STEP = 10

N_ATT = 9 * BRANCH_WIDTH
N_REC = 2 * BRANCH_WIDTH + 128

V7X_VMEM_LIMIT = 56 * 1024 * 1024
LANES = 128
ATT_BLOCK = 256
ATT_KEYS = 1024

NT = (((1,), (1,)), ((), ()))
TN = (((0,), (0,)), ((), ()))
NN = (((1,), (0,)), ((), ()))

MESH = pl.DeviceIdType.MESH


def _cp(sem=None):
    return pltpu.CompilerParams(dimension_semantics=sem, vmem_limit_bytes=V7X_VMEM_LIMIT)


def _dot(a, b, dims=NN):
    return lax.dot_general(a, b, dims, preferred_element_type=F32)


def _pick(n, prefs):
    for p in prefs:
        if n % p == 0:
            return p
    return n


def _split3(x):
    hi = x.astype(BF16)
    r1 = x - hi.astype(F32)
    mid = r1.astype(BF16)
    lo = (r1 - mid.astype(F32)).astype(BF16)
    return hi, mid, lo


def _split2(x):
    hi = x.astype(BF16)
    lo = (x - hi.astype(F32)).astype(BF16)
    return hi, lo


def _sigmoid(z):
    return 1.0 / (1.0 + jnp.exp(-z))


def _log_sigmoid(z):
    return jnp.minimum(z, 0.0) - jnp.log(1.0 + jnp.exp(-jnp.abs(z)))


def _mm(a, b, *, name, ta=False, tb=False, out_dtypes=(F32,), epilogue=None, extras=(),
        bm=None, bn=None, bk=None, out_map=None, b_view=None, carry=None):
    M, K = (a.shape[1], a.shape[0]) if ta else a.shape
    N = b.shape[0] if tb else b.shape[1]
    if b_view is not None:
        K, N = b_view[:2]
    bm = bm or _pick(M, (1024, 512, 256, 128))
    bn = bn or _pick(N, (1024, 1536, 1152, 512, 256, 128))
    bk = bk or _pick(K, (2048, 1536, 1024, 1152, 512, 256, 128))
    nk = K // bk
    a_spec = pl.BlockSpec((bk, bm), lambda i, j, k: (k, i)) if ta else pl.BlockSpec((bm, bk), lambda i, j, k: (i, k))
    b_spec = pl.BlockSpec((bn, bk), lambda i, j, k: (j, k)) if tb else pl.BlockSpec((bk, bn), lambda i, j, k: (k, j))
    if b_view is not None:
        b_spec = pl.BlockSpec(b_view[2], b_view[3])
    ex_specs = [pl.BlockSpec((bm, bn), lambda i, j, k: (i, j)) if kind == 'mn'
                else pl.BlockSpec((1, bn), lambda i, j, k: (0, j)) for _, kind in extras]
    n_ex, n_out = len(extras), len(out_dtypes)
    dims = TN if ta else (NT if tb else NN)

    def kern(*refs):
        a_ref, b_ref = refs[0], refs[1]
        ex_refs = refs[2:2 + n_ex]
        out_refs = refs[2 + n_ex:2 + n_ex + n_out]
        acc_ref = refs[-1]
        k = pl.program_id(2)
        part = _dot(a_ref[...].astype(BF16), b_ref[...].astype(BF16), dims)

        @pl.when(k == 0)
        def _():
            acc_ref[...] = part

        @pl.when(k > 0)
        def _():
            acc_ref[...] += part

        @pl.when(k == nk - 1)
        def _():
            acc = acc_ref[...]
            outs = (acc,) if epilogue is None else epilogue(acc, *[r[...] for r in ex_refs])
            for o_ref, o in zip(out_refs, outs):
                o_ref[...] = o.astype(o_ref.dtype).reshape(o_ref.shape)

    if out_map is None:
        out_specs = [pl.BlockSpec((bm, bn), lambda i, j, k: (i, j)) for _ in out_dtypes]
        out_shape = [jax.ShapeDtypeStruct((M, N), dt) for dt in out_dtypes]
    else:
        shape, block, index = out_map
        out_specs = [pl.BlockSpec(block, lambda i, j, k: index(i, j))]
        out_shape = [jax.ShapeDtypeStruct(shape, out_dtypes[0])]
    res, extra = _call_with_carry(
        kern, name=name, grid=(M // bm, N // bn, nk), carry=carry,
        in_specs=[a_spec, b_spec] + ex_specs, out_specs=out_specs, out_shape=out_shape,
        scratch_shapes=[pltpu.VMEM((bm, bn), F32)], args=(a, b, *[e for e, _ in extras]),
        semantics=("parallel", "parallel", "arbitrary"))
    res = res[0] if n_out == 1 else res
    return res if carry is None else (res, extra)


def _ln_fwd(h, g, b, *, name):
    T, D = h.shape
    bt = _pick(T, (512, 256, 128))

    def kern(h_ref, g_ref, b_ref, y_ref, yb_ref):
        x = h_ref[...]
        mu = jnp.mean(x, axis=-1, keepdims=True)
        xc = x - mu
        var = jnp.mean(xc * xc, axis=-1, keepdims=True)
        y = xc * lax.rsqrt(var + LN_EPS) * g_ref[...] + b_ref[...]
        y_ref[...] = y
        yb_ref[...] = y.astype(BF16)

    row = pl.BlockSpec((bt, D), lambda i: (i, 0))
    vec = pl.BlockSpec((1, D), lambda i: (0, 0))
    return pl.pallas_call(
        kern, name=name, grid=(T // bt,), in_specs=[row, vec, vec], out_specs=[row, row],
        out_shape=[jax.ShapeDtypeStruct((T, D), F32), jax.ShapeDtypeStruct((T, D), BF16)],
        compiler_params=_cp(("arbitrary",)),
    )(h, g.reshape(1, D), b.reshape(1, D))


def _ln_bwd(h, dy, g, *, name):
    T, D = h.shape
    bt = _pick(T, (512, 256, 128))

    def kern(h_ref, dy_ref, g_ref, dh_ref, dhb_ref, dg_ref, db_ref):
        i = pl.program_id(0)
        x = h_ref[...]
        dyv = dy_ref[...]
        mu = jnp.mean(x, axis=-1, keepdims=True)
        xc = x - mu
        var = jnp.mean(xc * xc, axis=-1, keepdims=True)
        rstd = lax.rsqrt(var + LN_EPS)
        xhat = xc * rstd
        dxh = dyv * g_ref[...]
        m1 = jnp.mean(dxh, axis=-1, keepdims=True)
        m2 = jnp.mean(dxh * xhat, axis=-1, keepdims=True)
        dh = rstd * (dxh - m1 - xhat * m2)
        dh_ref[...] = dh
        dhb_ref[...] = dh.astype(BF16)
        pg = jnp.sum(dyv * xhat, axis=0, keepdims=True)
        pb = jnp.sum(dyv, axis=0, keepdims=True)

        @pl.when(i == 0)
        def _():
            dg_ref[...] = pg
            db_ref[...] = pb

        @pl.when(i > 0)
        def _():
            dg_ref[...] += pg
            db_ref[...] += pb

    row = pl.BlockSpec((bt, D), lambda i: (i, 0))
    vec = pl.BlockSpec((1, D), lambda i: (0, 0))
    return pl.pallas_call(
        kern, name=name, grid=(T // bt,), in_specs=[row, row, vec], out_specs=[row, row, vec, vec],
        out_shape=[jax.ShapeDtypeStruct((T, D), F32), jax.ShapeDtypeStruct((T, D), BF16),
                   jax.ShapeDtypeStruct((1, D), F32), jax.ShapeDtypeStruct((1, D), F32)],
        compiler_params=_cp(("arbitrary",)),
    )(h, dy, g.reshape(1, D))


def _loss_head(y, tgt, *, name):
    T, D = y.shape
    bt = _pick(T, (512, 256, 128))

    def kern(y_ref, t_ref, dy_ref, loss_ref):
        i = pl.program_id(0)
        e = y_ref[...] - t_ref[...]
        dy_ref[...] = e * (1.0 / D)
        part = 0.5 * jnp.sum(jnp.sum(e * e, axis=-1, keepdims=True) * (1.0 / D), axis=0, keepdims=True)
        part = jnp.broadcast_to(part, (8, LANES))

        @pl.when(i == 0)
        def _():
            loss_ref[...] = part

        @pl.when(i > 0)
        def _():
            loss_ref[...] += part

    row = pl.BlockSpec((bt, D), lambda i: (i, 0))
    return pl.pallas_call(
        kern, name=name, grid=(T // bt,), in_specs=[row, row],
        out_specs=[row, pl.BlockSpec((8, LANES), lambda i: (0, 0))],
        out_shape=[jax.ShapeDtypeStruct((T, D), F32), jax.ShapeDtypeStruct((8, LANES), F32)],
        compiler_params=_cp(("arbitrary",)),
    )(y, tgt)


def _tri(n, upper):
    r = lax.broadcasted_iota(jnp.int32, (n, n), 0)
    c = lax.broadcasted_iota(jnp.int32, (n, n), 1)
    return jnp.where((c >= r) if upper else (c <= r), 1.0, 0.0).astype(BF16)


def _forget_fwd(ff, bf, *, name):
    T = ff.shape[0]
    bt = 256

    def kern(ff_ref, bf_ref, out_ref, carry):
        i = pl.program_id(0)

        @pl.when(i == 0)
        def _():
            carry[...] = jnp.zeros_like(carry)

        ls = _log_sigmoid(ff_ref[...] + bf_ref[...])
        tri = _tri(bt, upper=False)
        hi, mid, lo = _split3(ls)
        cs = _dot(tri, hi) + _dot(tri, mid) + _dot(tri, lo) + carry[0:1, :]
        out_ref[...] = cs
        carry[...] = jnp.broadcast_to(cs[bt - 1:bt, :], carry.shape)

    return pl.pallas_call(
        kern, name=name, grid=(T // bt,),
        in_specs=[pl.BlockSpec((bt, LANES), lambda i: (i, 0)), pl.BlockSpec((1, LANES), lambda i: (0, 0))],
        out_specs=pl.BlockSpec((bt, LANES), lambda i: (i, 0)),
        out_shape=jax.ShapeDtypeStruct((T, LANES), F32),
        scratch_shapes=[pltpu.VMEM((8, LANES), F32)],
        compiler_params=_cp(("arbitrary",)),
    )(ff, bf)


def _forget_bwd(dFk, dFq, ff, bf, *, name):
    T = ff.shape[0]
    bt = 256
    nb = T // bt

    def kern(dFk_ref, dFq_ref, ff_ref, bf_ref, dff_ref, dbf_ref, carry):
        i = pl.program_id(0)

        @pl.when(i == 0)
        def _():
            carry[...] = jnp.zeros_like(carry)
            dbf_ref[...] = jnp.zeros_like(dbf_ref)

        tri = _tri(bt, upper=True)
        hi, mid, lo = _split3(dFk_ref[...] + dFq_ref[...])
        rs = _dot(tri, hi) + _dot(tri, mid) + _dot(tri, lo) + carry[0:1, :]
        carry[...] = jnp.broadcast_to(rs[0:1, :], carry.shape)
        z = ff_ref[...] + bf_ref[...]
        dff = rs * _sigmoid(-z)
        dff_ref[...] = dff.astype(dff_ref.dtype)
        dbf_ref[...] += jnp.sum(dff, axis=0, keepdims=True)

    rev = pl.BlockSpec((bt, LANES), lambda i: (nb - 1 - i, 0))
    vec = pl.BlockSpec((1, LANES), lambda i: (0, 0))
    return pl.pallas_call(
        kern, name=name, grid=(nb,), in_specs=[rev, rev, rev, vec], out_specs=[rev, vec],
        out_shape=[jax.ShapeDtypeStruct((T, LANES), BF16), jax.ShapeDtypeStruct((1, LANES), F32)],
        scratch_shapes=[pltpu.VMEM((8, LANES), F32)],
        compiler_params=_cp(("arbitrary",)),
    )(dFk, dFq, ff, bf)


def _head_lane(x, h):
    lane = lax.broadcasted_iota(jnp.int32, x.shape, 1)
    return jnp.sum(jnp.where(lane == h, x, 0.0), axis=1, keepdims=True)


def _att_blocks(T):
    return min(ATT_BLOCK, T), min(ATT_KEYS, T)


def _positions(i, j, bq, bk):
    r = i * bq + lax.broadcasted_iota(jnp.int32, (bq, bk), 0)
    c = j * bk + lax.broadcasted_iota(jnp.int32, (bq, bk), 1)
    return r, c


def _fox_fwd(u_att, fcum, frow, *, name, carry=None):
    T = u_att.shape[0]
    bq, bk = _att_blocks(T)
    nq, nk = T // bq, T // bk
    H = N_HEADS

    def kern(q_ref, k_ref, v_ref, fc_ref, fr_ref, o_ref, lse_ref):
        i = pl.program_id(1)
        q = q_ref[...]
        fq = _head_lane(fc_ref[...], pl.program_id(0))

        def step(j, carry, masked):
            m, l, acc = carry
            off = pl.multiple_of(j * bk, bk)
            k = k_ref[pl.ds(off, bk), :]
            v = v_ref[pl.ds(off, bk), :]
            s = _dot(q, k, NT) * SCALE + (fq - fr_ref[j])
            if masked:
                r, c = _positions(i, j, bq, bk)
                s = jnp.where(c <= r, s, -jnp.inf)
            m_new = jnp.maximum(m, jnp.max(s, axis=1, keepdims=True))
            a = jnp.exp(m - m_new)
            p = jnp.exp(s - m_new)
            l = a * l + jnp.sum(p, axis=1, keepdims=True)
            acc = a * acc + _dot(p.astype(BF16), v)
            return m_new, l, acc

        init = (jnp.full((bq, 1), -1e30, F32), jnp.zeros((bq, 1), F32), jnp.zeros((bq, HEAD_DIM), F32))
        nfull = (i * bq) // bk
        carry = lax.fori_loop(0, nfull, lambda j, cr: step(j, cr, False), init)
        m, l, acc = step(nfull, carry, True)
        o_ref[...] = (acc / l).astype(o_ref.dtype)
        lse_ref[...] = m + jnp.log(l)

    return _call_with_carry(
        kern, name=name, grid=(H, nq), carry=carry,
        in_specs=[pl.BlockSpec((bq, HEAD_DIM), lambda h, i: (i, h)),
                  pl.BlockSpec((T, HEAD_DIM), lambda h, i: (0, 4 + h)),
                  pl.BlockSpec((T, HEAD_DIM), lambda h, i: (0, 8 + h)),
                  pl.BlockSpec((bq, LANES), lambda h, i: (i, 0)),
                  pl.BlockSpec((None, nk, 1, bk), lambda h, i: (h, 0, 0, 0))],
        out_specs=[pl.BlockSpec((bq, HEAD_DIM), lambda h, i: (i, h)),
                   pl.BlockSpec((None, bq, 1), lambda h, i: (h, i, 0))],
        out_shape=[jax.ShapeDtypeStruct((T, BRANCH_WIDTH), BF16), jax.ShapeDtypeStruct((H, T, 1), F32)],
        scratch_shapes=[], args=(u_att, u_att, u_att, fcum, frow))


def _row_dot(a, b, *, name):
    T = a.shape[0]
    bt = _pick(T, (512, 256, 128))

    def kern(a_ref, b_ref, o_ref):
        p = a_ref[...].astype(F32) * b_ref[...].astype(F32)
        for h in range(N_HEADS):
            o_ref[h] = jnp.sum(p[:, h * HEAD_DIM:(h + 1) * HEAD_DIM], axis=1, keepdims=True)

    row = pl.BlockSpec((bt, BRANCH_WIDTH), lambda i: (i, 0))
    return pl.pallas_call(
        kern, name=name, grid=(T // bt,), in_specs=[row, row],
        out_specs=pl.BlockSpec((N_HEADS, bt, 1), lambda i: (0, i, 0)),
        out_shape=jax.ShapeDtypeStruct((N_HEADS, T, 1), F32),
        compiler_params=_cp(("arbitrary",)),
    )(a, b)


def _fox_bwd(u_att, do, lse, delta, fcum, frow, *, name, carry=None):
    T = u_att.shape[0]
    bq, bk = _att_blocks(T)
    nq, nk = T // bq, T // bk
    H = N_HEADS

    def kern(q_ref, k_ref, v_ref, do_ref, lse_ref, dl_ref, fc_ref, fr_ref,
             dq_ref, dk_ref, dv_ref, df_ref, dfq_ref, dk_acc, dv_acc, df_acc):
        i = pl.program_id(1)

        @pl.when(i == 0)
        def _():
            dk_acc[...] = jnp.zeros_like(dk_acc)
            dv_acc[...] = jnp.zeros_like(dv_acc)
            df_acc[...] = jnp.zeros_like(df_acc)

        q = q_ref[...]
        dov = do_ref[...]
        fq = _head_lane(fc_ref[...], pl.program_id(0))
        lsev = lse_ref[...]
        dlt = dl_ref[...]

        def step(j, carry, masked):
            dq, dfq = carry
            off = pl.multiple_of(j * bk, bk)
            k = k_ref[pl.ds(off, bk), :]
            v = v_ref[pl.ds(off, bk), :]
            s = _dot(q, k, NT) * SCALE + (fq - fr_ref[j])
            p = jnp.exp(s - lsev)
            if masked:
                r, c = _positions(i, j, bq, bk)
                p = jnp.where(c <= r, p, 0.0)
            dp = _dot(dov, v, NT)
            ds = p * (dp - dlt)
            dsb = ds.astype(BF16)
            dq = dq + _dot(dsb, k)
            dk_acc[pl.ds(off, bk), :] += _dot(dsb, q, TN)
            dv_acc[pl.ds(off, bk), :] += _dot(p.astype(BF16), dov, TN)
            df_acc[j] += -jnp.sum(ds, axis=0, keepdims=True)
            return dq, dfq + jnp.sum(ds, axis=1, keepdims=True)

        nfull = (i * bq) // bk
        carry = lax.fori_loop(0, nfull, lambda j, cr: step(j, cr, False),
                              (jnp.zeros((bq, HEAD_DIM), F32), jnp.zeros((bq, 1), F32)))
        dq, dfq = step(nfull, carry, True)
        dq_ref[...] = (dq * SCALE).astype(dq_ref.dtype)
        dfq_ref[...] = dfq

        @pl.when(i == nq - 1)
        def _():
            dk_ref[...] = (dk_acc[...] * SCALE).astype(dk_ref.dtype)
            dv_ref[...] = dv_acc[...].astype(dv_ref.dtype)
            df_ref[...] = df_acc[...]

    col = lambda: pl.BlockSpec((None, bq, 1), lambda h, i: (h, i, 0))
    return _call_with_carry(
        kern, name=name, grid=(H, nq), carry=carry,
        in_specs=[pl.BlockSpec((bq, HEAD_DIM), lambda h, i: (i, h)),
                  pl.BlockSpec((T, HEAD_DIM), lambda h, i: (0, 4 + h)),
                  pl.BlockSpec((T, HEAD_DIM), lambda h, i: (0, 8 + h)),
                  pl.BlockSpec((bq, HEAD_DIM), lambda h, i: (i, h)),
                  col(), col(), pl.BlockSpec((bq, LANES), lambda h, i: (i, 0)),
                  pl.BlockSpec((None, nk, 1, bk), lambda h, i: (h, 0, 0, 0))],
        out_specs=[pl.BlockSpec((bq, HEAD_DIM), lambda h, i: (i, h)),
                   pl.BlockSpec((T, HEAD_DIM), lambda h, i: (0, h)),
                   pl.BlockSpec((T, HEAD_DIM), lambda h, i: (0, h)),
                   pl.BlockSpec((None, nk, 1, bk), lambda h, i: (h, 0, 0, 0)),
                   pl.BlockSpec((None, bq, 1), lambda h, i: (h, i, 0))],
        out_shape=[jax.ShapeDtypeStruct((T, BRANCH_WIDTH), BF16)] * 3
                  + [jax.ShapeDtypeStruct((H, nk, 1, bk), F32), jax.ShapeDtypeStruct((H, T, 1), F32)],
        scratch_shapes=[pltpu.VMEM((T, HEAD_DIM), F32), pltpu.VMEM((T, HEAD_DIM), F32),
                        pltpu.VMEM((nk, 1, bk), F32)],
        args=(u_att, u_att, u_att, do, lse, delta, fcum, frow))


def _softplus_parts(z):
    t = jnp.exp(-jnp.abs(z))
    sp = jnp.maximum(z, 0.0) + jnp.log(1.0 + t)
    return t, sp


def _sb_tri(B):
    r = lax.broadcasted_iota(jnp.int32, (B, B), 0)
    c = lax.broadcasted_iota(jnp.int32, (B, B), 1)
    suffix = jnp.where(r >= c, 1.0, 0.0).astype(BF16)
    prefix = jnp.where(r <= c, 1.0, 0.0).astype(BF16)
    return suffix, prefix


def _sb_fwd(u_att, *, name, carry=None):
    T = u_att.shape[0]
    B, bk = _att_blocks(T)
    nq, nsub = T // B, bk // B
    H = N_HEADS

    def kern(q_ref, k_ref, v_ref, o_ref):
        i = pl.program_id(1)
        q = q_ref[...]
        suffix, _ = _sb_tri(B)

        def step(j, carry, masked):
            run, acc = carry
            parts = []
            for s in reversed(range(nsub)):
                jb = j * nsub + s
                off = pl.multiple_of(jb * B, B)
                k = k_ref[pl.ds(off, B), :]
                z = _dot(q, k, NT) * SCALE
                _, sp = _softplus_parts(z)
                lg = -sp
                valid = None
                if masked:
                    r, c = _positions(i, jb, B, B)
                    valid = c < r
                    lg = jnp.where(valid, lg, 0.0)
                hi, lo = _split2(lg)
                cum = _dot(hi, suffix) + _dot(lo, suffix)
                parts.append((off, z, cum, jnp.sum(lg, axis=1, keepdims=True), valid))
            for off, z, cum, rs, valid in parts:
                a = jnp.exp(z + cum + run)
                if masked:
                    a = jnp.where(valid, a, 0.0)
                acc = acc + _dot(a.astype(BF16), v_ref[pl.ds(off, B), :])
                run = run + rs
            return run, acc

        nfull = (i * B) // bk
        carry = step(nfull, (jnp.zeros((B, 1), F32), jnp.zeros((B, HEAD_DIM), F32)), True)
        _, acc = lax.fori_loop(0, nfull, lambda jj, cr: step(nfull - 1 - jj, cr, False), carry)
        o_ref[...] = acc.astype(o_ref.dtype)

    return _call_with_carry(
        kern, name=name, grid=(H, nq), carry=carry,
        in_specs=[pl.BlockSpec((B, HEAD_DIM), lambda h, i: (i, 12 + h)),
                  pl.BlockSpec((T, HEAD_DIM), lambda h, i: (0, 16 + h)),
                  pl.BlockSpec((T, HEAD_DIM), lambda h, i: (0, 20 + h))],
        out_specs=[pl.BlockSpec((B, HEAD_DIM), lambda h, i: (i, h))],
        out_shape=[jax.ShapeDtypeStruct((T, BRANCH_WIDTH), BF16)],
        scratch_shapes=[], args=(u_att, u_att, u_att))


def _sb_bwd(u_att, do, *, name, carry=None):
    T = u_att.shape[0]
    B, bk = _att_blocks(T)
    nq, nsub = T // B, bk // B
    H = N_HEADS

    def kern(q_ref, k_ref, v_ref, do_ref, dq_ref, dk_ref, dv_ref, dk_acc, dv_acc, de_s, sg_s):
        i = pl.program_id(1)

        @pl.when(i == 0)
        def _():
            dk_acc[...] = jnp.zeros_like(dk_acc)
            dv_acc[...] = jnp.zeros_like(dv_acc)

        q = q_ref[...]
        dov = do_ref[...]
        suffix, prefix = _sb_tri(B)

        def sweep1(j, run, masked):
            parts = []
            for s in reversed(range(nsub)):
                jb = j * nsub + s
                off = pl.multiple_of(jb * B, B)
                k = k_ref[pl.ds(off, B), :]
                z = _dot(q, k, NT) * SCALE
                t, sp = _softplus_parts(z)
                lg = -sp
                sg = jnp.exp(z + lg)
                valid = None
                if masked:
                    r, c = _positions(i, jb, B, B)
                    valid = c < r
                    lg = jnp.where(valid, lg, 0.0)
                    sg = jnp.where(valid, sg, 0.0)
                sg_s[jb] = sg.astype(sg_s.dtype)
                hi, lo = _split2(lg)
                cum = _dot(hi, suffix) + _dot(lo, suffix)
                da = _dot(dov, v_ref[pl.ds(off, B), :], NT)
                parts.append((jb, off, z, cum, da, jnp.sum(lg, axis=1, keepdims=True), valid))
            for jb, off, z, cum, da, rs, valid in parts:
                a = jnp.exp(z + cum + run)
                if masked:
                    a = jnp.where(valid, a, 0.0)
                de_s[jb] = a * da
                dv_acc[pl.ds(off, B), :] += _dot(a.astype(BF16), dov, TN)
                run = run + rs
            return run

        nfull = (i * B) // bk
        run = sweep1(nfull, jnp.zeros((B, 1), F32), True)
        lax.fori_loop(0, nfull, lambda jj, cr: sweep1(nfull - 1 - jj, cr, False), run)

        def sweep2(j, carry):
            pre, dq = carry
            parts = []
            for s in range(nsub):
                jb = j * nsub + s
                de = de_s[jb]
                hi, lo = _split2(de)
                parts.append((jb, de, _dot(hi, prefix) + _dot(lo, prefix), jnp.sum(de, axis=1, keepdims=True)))
            for jb, de, g, rs in parts:
                off = pl.multiple_of(jb * B, B)
                dz = (de - sg_s[jb].astype(F32) * (g + pre)).astype(BF16)
                dq = dq + _dot(dz, k_ref[pl.ds(off, B), :])
                dk_acc[pl.ds(off, B), :] += _dot(dz, q, TN)
                pre = pre + rs
            return pre, dq

        _, dq = lax.fori_loop(0, nfull + 1, sweep2, (jnp.zeros((B, 1), F32), jnp.zeros((B, HEAD_DIM), F32)))
        dq_ref[...] = (dq * SCALE).astype(dq_ref.dtype)

        @pl.when(i == nq - 1)
        def _():
            dk_ref[...] = (dk_acc[...] * SCALE).astype(dk_ref.dtype)
            dv_ref[...] = dv_acc[...].astype(dv_ref.dtype)

    return _call_with_carry(
        kern, name=name, grid=(H, nq), carry=carry,
        in_specs=[pl.BlockSpec((B, HEAD_DIM), lambda h, i: (i, 12 + h)),
                  pl.BlockSpec((T, HEAD_DIM), lambda h, i: (0, 16 + h)),
                  pl.BlockSpec((T, HEAD_DIM), lambda h, i: (0, 20 + h)),
                  pl.BlockSpec((B, HEAD_DIM), lambda h, i: (i, h))],
        out_specs=[pl.BlockSpec((B, HEAD_DIM), lambda h, i: (i, h)),
                   pl.BlockSpec((T, HEAD_DIM), lambda h, i: (0, h)),
                   pl.BlockSpec((T, HEAD_DIM), lambda h, i: (0, h))],
        out_shape=[jax.ShapeDtypeStruct((T, BRANCH_WIDTH), BF16)] * 3,
        scratch_shapes=[pltpu.VMEM((T, HEAD_DIM), F32), pltpu.VMEM((T, HEAD_DIM), F32),
                        pltpu.VMEM((T // B, B, B), F32), pltpu.VMEM((T // B, B, B), BF16)],
        args=(u_att, u_att, u_att, do))


def _rel_onehot(qrow):
    k = lax.broadcasted_iota(jnp.int32, (BAND, REL_PAD), 0)
    rr = lax.broadcasted_iota(jnp.int32, (BAND, REL_PAD), 1)
    idx = jnp.clip(PAD_ROWS + qrow - k, -(CHUNK - 1), REL_CLIP) + (CHUNK - 1)
    return jnp.where(idx == rr, 1.0, 0.0).astype(BF16)


def _band_bias(table, *, name):
    def kern(t_ref, o_ref):
        hi, mid, lo = _split3(t_ref[...])

        def body(qrow, _):
            oh = _rel_onehot(qrow)
            o_ref[qrow] = _dot(hi, oh, NT) + _dot(mid, oh, NT) + _dot(lo, oh, NT)
            return 0

        lax.fori_loop(0, CHUNK, body, 0)

    return pl.pallas_call(
        kern, name=name, out_shape=jax.ShapeDtypeStruct((CHUNK, 8, BAND), F32),
        compiler_params=_cp(),
    )(table)


def _band_bias_bwd(dbias, *, name):
    def kern(d_ref, o_ref):
        def body(qrow, acc):
            oh = _rel_onehot(qrow)
            hi, mid, lo = _split3(d_ref[qrow])
            return acc + _dot(hi, oh) + _dot(mid, oh) + _dot(lo, oh)

        o_ref[...] = lax.fori_loop(0, CHUNK, body, jnp.zeros((8, REL_PAD), F32))

    return pl.pallas_call(
        kern, name=name, out_shape=jax.ShapeDtypeStruct((8, REL_PAD), F32),
        compiler_params=_cp(),
    )(dbias)


def _chunk_rows(T):
    return _pick(T, (512, 256, 128, 64))


def _chunk_scores(q, kw, bias, c_global):
    s = _dot(q, kw, NT) * SCALE + bias
    col = lax.broadcasted_iota(jnp.int32, (CHUNK, BAND), 1)
    valid = (c_global * CHUNK + col) >= PAD_ROWS
    s = jnp.where(valid, s, -jnp.inf)
    m = jnp.max(s, axis=1, keepdims=True)
    e = jnp.exp(s - m)
    return e / jnp.sum(e, axis=1, keepdims=True)


def _chunk_fwd(u_att, bias, *, name, carry=None):
    T = u_att.shape[0]
    R = _chunk_rows(T)
    nr = T // R
    H = N_HEADS

    def kern(q_ref, k_ref, v_ref, b_ref, o_ref, kpad, vpad):
        i = pl.program_id(1)

        @pl.when(i == 0)
        def _():
            kpad[0:PAD_ROWS, :] = jnp.zeros((PAD_ROWS, HEAD_DIM), BF16)
            vpad[0:PAD_ROWS, :] = jnp.zeros((PAD_ROWS, HEAD_DIM), BF16)
            kpad[PAD_ROWS:, :] = k_ref[...]
            vpad[PAD_ROWS:, :] = v_ref[...]

        bias_v = b_ref[...]
        for cc in range(R // CHUNK):
            cg = i * (R // CHUNK) + cc
            off = pl.multiple_of(cg * CHUNK, CHUNK)
            q = q_ref[cc * CHUNK:(cc + 1) * CHUNK, :]
            kw = kpad[pl.ds(off, BAND), :]
            vw = vpad[pl.ds(off, BAND), :]
            p = _chunk_scores(q, kw, bias_v, cg)
            o_ref[cc * CHUNK:(cc + 1) * CHUNK, :] = _dot(p.astype(BF16), vw).astype(o_ref.dtype)

    return _call_with_carry(
        kern, name=name, grid=(H, nr), carry=carry,
        in_specs=[pl.BlockSpec((R, HEAD_DIM), lambda h, i: (i, 24 + h)),
                  pl.BlockSpec((T, HEAD_DIM), lambda h, i: (0, 28 + h)),
                  pl.BlockSpec((T, HEAD_DIM), lambda h, i: (0, 32 + h)),
                  pl.BlockSpec((None, CHUNK, BAND), lambda h, i: (h, 0, 0))],
        out_specs=[pl.BlockSpec((R, HEAD_DIM), lambda h, i: (i, h))],
        out_shape=[jax.ShapeDtypeStruct((T, BRANCH_WIDTH), BF16)],
        scratch_shapes=[pltpu.VMEM((T + PAD_ROWS, HEAD_DIM), BF16), pltpu.VMEM((T + PAD_ROWS, HEAD_DIM), BF16)],
        args=(u_att, u_att, u_att, bias))


def _chunk_bwd(u_att, bias, do, *, name, carry=None):
    T = u_att.shape[0]
    R = _chunk_rows(T)
    nr = T // R
    H = N_HEADS

    def kern(q_ref, k_ref, v_ref, b_ref, do_ref, dq_ref, dk_ref, dv_ref, db_ref, kpad, vpad, dkp, dvp):
        i = pl.program_id(1)

        @pl.when(i == 0)
        def _():
            kpad[0:PAD_ROWS, :] = jnp.zeros((PAD_ROWS, HEAD_DIM), BF16)
            vpad[0:PAD_ROWS, :] = jnp.zeros((PAD_ROWS, HEAD_DIM), BF16)
            kpad[PAD_ROWS:, :] = k_ref[...]
            vpad[PAD_ROWS:, :] = v_ref[...]
            dkp[...] = jnp.zeros_like(dkp)
            dvp[...] = jnp.zeros_like(dvp)
            db_ref[...] = jnp.zeros_like(db_ref)

        bias_v = b_ref[...]
        for cc in range(R // CHUNK):
            cg = i * (R // CHUNK) + cc
            off = pl.multiple_of(cg * CHUNK, CHUNK)
            q = q_ref[cc * CHUNK:(cc + 1) * CHUNK, :]
            dov = do_ref[cc * CHUNK:(cc + 1) * CHUNK, :]
            kw = kpad[pl.ds(off, BAND), :]
            vw = vpad[pl.ds(off, BAND), :]
            p = _chunk_scores(q, kw, bias_v, cg)
            dp = _dot(dov, vw, NT)
            ds = p * (dp - jnp.sum(p * dp, axis=1, keepdims=True))
            dsb = ds.astype(BF16)
            dq_ref[cc * CHUNK:(cc + 1) * CHUNK, :] = (_dot(dsb, kw) * SCALE).astype(dq_ref.dtype)
            dkp[pl.ds(off, BAND), :] += _dot(dsb, q, TN)
            dvp[pl.ds(off, BAND), :] += _dot(p.astype(BF16), dov, TN)
            db_ref[...] += ds

        @pl.when(i == nr - 1)
        def _():
            dk_ref[...] = (dkp[PAD_ROWS:, :] * SCALE).astype(dk_ref.dtype)
            dv_ref[...] = dvp[PAD_ROWS:, :].astype(dv_ref.dtype)

    return _call_with_carry(
        kern, name=name, grid=(H, nr), carry=carry,
        in_specs=[pl.BlockSpec((R, HEAD_DIM), lambda h, i: (i, 24 + h)),
                  pl.BlockSpec((T, HEAD_DIM), lambda h, i: (0, 28 + h)),
                  pl.BlockSpec((T, HEAD_DIM), lambda h, i: (0, 32 + h)),
                  pl.BlockSpec((None, CHUNK, BAND), lambda h, i: (h, 0, 0)),
                  pl.BlockSpec((R, HEAD_DIM), lambda h, i: (i, h))],
        out_specs=[pl.BlockSpec((R, HEAD_DIM), lambda h, i: (i, h)),
                   pl.BlockSpec((T, HEAD_DIM), lambda h, i: (0, h)),
                   pl.BlockSpec((T, HEAD_DIM), lambda h, i: (0, h)),
                   pl.BlockSpec((None, CHUNK, BAND), lambda h, i: (h, 0, 0))],
        out_shape=[jax.ShapeDtypeStruct((T, BRANCH_WIDTH), BF16)] * 3
                  + [jax.ShapeDtypeStruct((H, CHUNK, BAND), F32)],
        scratch_shapes=[pltpu.VMEM((T + PAD_ROWS, HEAD_DIM), BF16), pltpu.VMEM((T + PAD_ROWS, HEAD_DIM), BF16),
                        pltpu.VMEM((T + PAD_ROWS, HEAD_DIM), F32), pltpu.VMEM((T + PAD_ROWS, HEAD_DIM), F32)],
        args=(u_att, u_att, u_att, bias, do))


LRU_ROWS = 256
HALO = 8


def _gelu(y):
    k0 = math.sqrt(2.0 / math.pi)
    t = jnp.tanh(k0 * (y + 0.044715 * y * y * y))
    return 0.5 * y * (1.0 + t), t


def _gelu_grad(y, t):
    k0 = math.sqrt(2.0 / math.pi)
    return 0.5 * (1.0 + t) + 0.5 * y * (1.0 - t * t) * k0 * (1.0 + 3.0 * 0.044715 * y * y)


def _neg_expm1(y):
    poly = -y * (1.0 + y * (1.0 / 2 + y * (1.0 / 6 + y * (1.0 / 24 + y * (1.0 / 120 + y * (1.0 / 720 + y * (1.0 / 5040)))))))
    return jnp.where(y > -0.5, poly, 1.0 - jnp.exp(y))


def _lru_gates(ext, cw_ref, cb_ref, wr_ref, br_ref, wi_ref, bi_ref, lam_ref, rows):
    xc = cb_ref[...] + jnp.zeros((rows, BRANCH_WIDTH), F32)
    for j in range(CONV_WIDTH):
        xc = xc + ext[pl.ds(HALO - (CONV_WIDTH - 1) + j, rows), :] * cw_ref[j:j + 1, :]
    xcb = xc.astype(BF16)
    zr = jnp.concatenate([_dot(xcb[:, n * 128:(n + 1) * 128], wr_ref[n]) for n in range(4)], axis=1) + br_ref[...]
    zi = jnp.concatenate([_dot(xcb[:, n * 128:(n + 1) * 128], wi_ref[n]) for n in range(4)], axis=1) + bi_ref[...]
    r = _sigmoid(zr)
    gi = _sigmoid(zi)
    ls = _log_sigmoid(lam_ref[...])
    la = LRU_C * r * ls
    a = jnp.exp(la)
    mult = jnp.sqrt(_neg_expm1(2.0 * la))
    return xc, xcb, r, gi, ls, a, mult


def _lru_param_specs():
    full2 = lambda s: pl.BlockSpec(s, lambda i: (0, 0))
    full3 = lambda s: pl.BlockSpec(s, lambda i: (0, 0, 0))
    return [full2((8, BRANCH_WIDTH)), full2((1, BRANCH_WIDTH)), full3((4, 128, 128)), full2((1, BRANCH_WIDTH)),
            full3((4, 128, 128)), full2((1, BRANCH_WIDTH)), full2((1, BRANCH_WIDTH))]


def _lru_fwd(u_rec, p, *, name):
    T = u_rec.shape[0]
    R = min(LRU_ROWS, T)
    nb = T // R
    W = BRANCH_WIDTH
    hb = R // HALO

    def kern(rx_ref, halo_ref, ry_ref, cw_ref, cb_ref, wr_ref, br_ref, wi_ref, bi_ref, lam_ref,
             o_ref, h_ref, ext, a_s, b_s, hc):
        i = pl.program_id(0)

        @pl.when(i == 0)
        def _():
            hc[...] = jnp.zeros_like(hc)

        ext[0:HALO, :] = jnp.where(i == 0, 0.0, halo_ref[...])
        ext[HALO:, :] = rx_ref[...]
        xc, _, r, gi, ls, a, mult = _lru_gates(ext, cw_ref, cb_ref, wr_ref, br_ref, wi_ref, bi_ref, lam_ref, R)
        a_s[...] = a
        b_s[...] = mult * (gi * xc)

        def body(t, h):
            h = a_s[pl.ds(t, 1), :] * h + b_s[pl.ds(t, 1), :]
            h_ref[pl.ds(t, 1), :] = h
            return h

        h = lax.fori_loop(0, R, body, hc[0:1, :], unroll=8)
        hc[...] = jnp.broadcast_to(h, hc.shape)
        g, _ = _gelu(ry_ref[...])
        o_ref[...] = (h_ref[...] * g).astype(o_ref.dtype)

    return pl.pallas_call(
        kern, name=name, grid=(nb,),
        in_specs=[pl.BlockSpec((R, W), lambda i: (i, 0)),
                  pl.BlockSpec((HALO, W), lambda i: (jnp.maximum(i * hb - 1, 0), 0)),
                  pl.BlockSpec((R, W), lambda i: (i, 1))] + _lru_param_specs(),
        out_specs=[pl.BlockSpec((R, W), lambda i: (i, 0)), pl.BlockSpec((R, W), lambda i: (i, 0))],
        out_shape=[jax.ShapeDtypeStruct((T, W), BF16), jax.ShapeDtypeStruct((T, W), F32)],
        scratch_shapes=[pltpu.VMEM((R + HALO, W), F32), pltpu.VMEM((R, W), F32), pltpu.VMEM((R, W), F32),
                        pltpu.VMEM((8, W), F32)],
        compiler_params=_cp(("arbitrary",)),
    )(u_rec, u_rec, u_rec, *p)


def _lru_bwd(u_rec, hs, do, p, *, name):
    T = u_rec.shape[0]
    R = min(LRU_ROWS, T)
    nb = T // R
    W = BRANCH_WIDTH
    hb = R // HALO

    def kern(rx_ref, halo_ref, ry_ref, h_ref, hh_ref, do_ref, cw_ref, cb_ref, wr_ref, br_ref, wi_ref, bi_ref, lam_ref,
             drx_ref, dry_ref, dcw_ref, dcb_ref, dwr_ref, dbr_ref, dwi_ref, dbi_ref, dlam_ref,
             ext, hext, a_s, g_s, dext, gc):
        s = pl.program_id(0)
        first_block = s == nb - 1

        @pl.when(s == 0)
        def _():
            gc[...] = jnp.zeros_like(gc)
            dext[R:, :] = jnp.zeros((HALO, W), F32)
            for ref in (dcw_ref, dcb_ref, dwr_ref, dbr_ref, dwi_ref, dbi_ref, dlam_ref):
                ref[...] = jnp.zeros_like(ref)

        ext[0:HALO, :] = jnp.where(first_block, 0.0, halo_ref[...])
        ext[HALO:, :] = rx_ref[...]
        hext[0:HALO, :] = jnp.where(first_block, 0.0, hh_ref[...])
        hext[HALO:, :] = h_ref[...]
        xc, xcb, r, gi, ls, a, mult = _lru_gates(ext, cw_ref, cb_ref, wr_ref, br_ref, wi_ref, bi_ref, lam_ref, R)
        ry = ry_ref[...]
        gel, th = _gelu(ry)
        dov = do_ref[...].astype(F32)
        dry_ref[...] = (dov * h_ref[...] * _gelu_grad(ry, th)).astype(dry_ref.dtype)
        a_s[...] = a
        g_s[...] = dov * gel

        def body(tt, g):
            t = R - 1 - tt
            dh = g_s[pl.ds(t, 1), :] + g
            g_s[pl.ds(t, 1), :] = dh
            return a_s[pl.ds(t, 1), :] * dh

        g = lax.fori_loop(0, R, body, gc[0:1, :], unroll=8)
        gc[...] = jnp.broadcast_to(g, gc.shape)
        dh = g_s[...]
        hprev = hext[pl.ds(HALO - 1, R), :]
        da = dh * hprev
        gx = gi * xc
        dmult = dh * gx
        dgx = dh * mult
        dgi = dgx * xc
        dxc = dgx * gi
        dla = da * a - dmult * (a * a) / mult
        dr = dla * (LRU_C * ls)
        dlam_ref[...] += jnp.sum(dla * (LRU_C * r), axis=0, keepdims=True)
        dzr = dr * r * (1.0 - r)
        dzi = dgi * gi * (1.0 - gi)
        dbr_ref[...] += jnp.sum(dzr, axis=0, keepdims=True)
        dbi_ref[...] += jnp.sum(dzi, axis=0, keepdims=True)
        dzrb = dzr.astype(BF16)
        dzib = dzi.astype(BF16)
        back = []
        for n in range(4):
            sl = slice(n * 128, (n + 1) * 128)
            dwr_ref[n] += _dot(xcb[:, sl], dzrb[:, sl], TN)
            dwi_ref[n] += _dot(xcb[:, sl], dzib[:, sl], TN)
            back.append(_dot(dzrb[:, sl], wr_ref[n], NT) + _dot(dzib[:, sl], wi_ref[n], NT))
        dxc = dxc + jnp.concatenate(back, axis=1)
        dcb_ref[...] += jnp.sum(dxc, axis=0, keepdims=True)
        for j in range(CONV_WIDTH):
            dcw_ref[j:j + 1, :] += jnp.sum(dxc * ext[pl.ds(HALO - (CONV_WIDTH - 1) + j, R), :], axis=0, keepdims=True)
        dext[0:R, :] = dxc
        drx = jnp.zeros((R, W), F32)
        for j in range(CONV_WIDTH):
            drx = drx + dext[pl.ds(CONV_WIDTH - 1 - j, R), :] * cw_ref[j:j + 1, :]
        drx_ref[...] = drx.astype(drx_ref.dtype)
        dext[R:, :] = dxc[0:HALO, :]

        @pl.when(s == nb - 1)
        def _():
            dlam_ref[...] = dlam_ref[...] * _sigmoid(-lam_ref[...])

    rev = lambda c: pl.BlockSpec((R, W), lambda s: (nb - 1 - s, c))
    halo = lambda: pl.BlockSpec((HALO, W), lambda s: (jnp.maximum((nb - 1 - s) * hb - 1, 0), 0))
    v2 = lambda shp: pl.BlockSpec(shp, lambda s: (0, 0))
    v3 = lambda shp: pl.BlockSpec(shp, lambda s: (0, 0, 0))
    return pl.pallas_call(
        kern, name=name, grid=(nb,),
        in_specs=[rev(0), halo(), rev(1), rev(0), halo(), rev(0)] + _lru_param_specs(),
        out_specs=[rev(0), rev(0), v2((8, W)), v2((1, W)), v3((4, 128, 128)), v2((1, W)), v3((4, 128, 128)),
                   v2((1, W)), v2((1, W))],
        out_shape=[jax.ShapeDtypeStruct((T, W), BF16), jax.ShapeDtypeStruct((T, W), BF16),
                   jax.ShapeDtypeStruct((8, W), F32), jax.ShapeDtypeStruct((1, W), F32),
                   jax.ShapeDtypeStruct((4, 128, 128), F32), jax.ShapeDtypeStruct((1, W), F32),
                   jax.ShapeDtypeStruct((4, 128, 128), F32), jax.ShapeDtypeStruct((1, W), F32),
                   jax.ShapeDtypeStruct((1, W), F32)],
        scratch_shapes=[pltpu.VMEM((R + HALO, W), F32), pltpu.VMEM((R + HALO, W), F32), pltpu.VMEM((R, W), F32),
                        pltpu.VMEM((R, W), F32), pltpu.VMEM((R + HALO, W), F32), pltpu.VMEM((8, W), F32)],
        compiler_params=_cp(("arbitrary",)),
    )(u_rec, u_rec, u_rec, hs, hs, do, *p)


def _merge_fwd(o_all, wb, gate, *, name):
    T = o_all.shape[1]
    D = D_MODEL
    bm = _pick(T, (1024, 512, 256, 128))
    bn = 1024
    nj = D // bn

    def kern(o_ref, w_ref, g_ref, m_ref, pb_ref, acc):
        g = pl.program_id(2)
        pbv = _dot(o_ref[...], w_ref[...])
        pb_ref[...] = pbv.astype(pb_ref.dtype)
        term = g_ref[...].astype(F32) * pbv

        @pl.when(g == 0)
        def _():
            acc[...] = term

        @pl.when(g > 0)
        def _():
            acc[...] += term

        @pl.when(g == N_BRANCH - 1)
        def _():
            m_ref[...] = acc[...].astype(m_ref.dtype)

    return pl.pallas_call(
        kern, name=name, grid=(T // bm, nj, N_BRANCH),
        in_specs=[pl.BlockSpec((None, bm, BRANCH_WIDTH), lambda i, j, g: (g, i, 0)),
                  pl.BlockSpec((None, BRANCH_WIDTH, bn), lambda i, j, g: (g, 0, j)),
                  pl.BlockSpec((bm, bn), lambda i, j, g: (i, g * nj + j))],
        out_specs=[pl.BlockSpec((bm, bn), lambda i, j, g: (i, j)),
                   pl.BlockSpec((bm, bn), lambda i, j, g: (i, g * nj + j))],
        out_shape=[jax.ShapeDtypeStruct((T, D), BF16), jax.ShapeDtypeStruct((T, N_BRANCH * D), BF16)],
        scratch_shapes=[pltpu.VMEM((bm, bn), F32)],
        compiler_params=_cp(("parallel", "parallel", "arbitrary")),
    )(o_all, wb, gate)


def _merge_bwd(dm, gate, pb, *, name):
    T = dm.shape[0]
    D = D_MODEL
    bt = _pick(T, (256, 128))

    def kern(dm_ref, g_ref, pb_ref, dpb_ref, dzg_ref, dbg_ref):
        i = pl.program_id(1)
        dmv = dm_ref[...]
        gv = g_ref[...].astype(F32)
        dpb_ref[...] = (dmv * gv).astype(dpb_ref.dtype)
        dzg = dmv * pb_ref[...].astype(F32) * gv * (1.0 - gv)
        dzg_ref[...] = dzg.astype(dzg_ref.dtype)
        part = jnp.sum(dzg, axis=0, keepdims=True)

        @pl.when(i == 0)
        def _():
            dbg_ref[...] = part

        @pl.when(i > 0)
        def _():
            dbg_ref[...] += part

    return pl.pallas_call(
        kern, name=name, grid=(N_BRANCH, T // bt),
        in_specs=[pl.BlockSpec((bt, D), lambda g, i: (i, 0)),
                  pl.BlockSpec((bt, D), lambda g, i: (i, g)),
                  pl.BlockSpec((bt, D), lambda g, i: (i, g))],
        out_specs=[pl.BlockSpec((None, bt, D), lambda g, i: (g, i, 0)),
                   pl.BlockSpec((bt, D), lambda g, i: (i, g)),
                   pl.BlockSpec((1, D), lambda g, i: (0, g))],
        out_shape=[jax.ShapeDtypeStruct((N_BRANCH, T, D), BF16), jax.ShapeDtypeStruct((T, N_BRANCH * D), BF16),
                   jax.ShapeDtypeStruct((1, N_BRANCH * D), F32)],
        compiler_params=_cp(("parallel", "arbitrary")),
    )(dm, gate, pb)


def _col_split(M, N, bm, bn):
    per = N // N_CHIPS // bn
    return (N_CHIPS, M, N // N_CHIPS), (None, bm, bn), lambda i, j: (j // per, i, j % per)


def _pad_lanes(v, n):
    return jnp.pad(v, [(0, 0)] * (v.ndim - 1) + [(0, n - v.shape[-1])])


def _rows8(v):
    return jnp.pad(v, ((0, 8 - v.shape[0]), (0, 0)))


def _device_step(x, tgt, W, hooks=None):
    hooks = hooks or {}

    def carried(fn, key, *args, **kw):
        hook = hooks.get(key)
        outs, extra = fn(*args, name=key[0], carry=hook.spec(W, G) if hook else None, **kw)
        if hook:
            hook.done(extra, W, G)
        return outs

    def mm_carried(key, *args, **kw):
        hook = hooks.get(key)
        if hook is None:
            return _mm(*args, name=key[0], **kw)
        res, extra = _mm(*args, name=key[0], carry=hook.spec(W, G), **kw)
        hook.done(extra, W, G)
        return res

    T = x.shape[0]
    _, bk = _att_blocks(T)
    H = N_HEADS
    G = {}
    saved = []

    xf, xb = _ln_fwd(x, W['ln_in_g'], W['ln_in_b'], name='ln_in_fwd')
    for l in range(DEPTH):
        w_att, w_rec = W['w_att'][l], W['w_rec'][l]
        u_att = mm_carried(('in_proj_att', l), xb, w_att, out_dtypes=(BF16,))
        u_rec = _mm(xb, w_rec, name='in_proj_rec', out_dtypes=(F32,))
        ffl = u_rec[:, 2 * BRANCH_WIDTH:]
        bf = _pad_lanes(W['b_forget'][l].reshape(1, H), LANES)
        Fc = _forget_fwd(ffl, bf, name='forget_fwd')
        Fh = Fc[:, :H].T
        frow = Fh.reshape(H, T // bk, 1, bk)
        o_fox, lse = carried(_fox_fwd, ('fox_fwd', l), u_att, Fc, frow)
        lp = (_rows8(W['conv_w'][l]), W['conv_b'][l].reshape(1, -1), W['w_r'][l].astype(BF16),
              W['b_r'][l].reshape(1, -1), W['w_i'][l].astype(BF16), W['b_i'][l].reshape(1, -1),
              W['lru_lambda'][l].reshape(1, -1))
        o_lru, hs = _lru_fwd(u_rec, lp, name='lru_fwd')
        o_sb, = carried(_sb_fwd, ('sb_fwd', l), u_att)
        table = _rows8(_pad_lanes(W['rel_bias'][l], REL_PAD))
        bias = _band_bias(table, name='band_bias').transpose(1, 0, 2)[:H]
        o_ch, = carried(_chunk_fwd, ('chunk_fwd', l), u_att, bias)
        o_all = jnp.stack([o_fox, o_lru, o_sb, o_ch])
        gate = mm_carried(('gate_proj', l), xb, W['w_gate_cat'][l], out_dtypes=(BF16,),
                          extras=[(W['b_gate'][l].reshape(1, -1), 'n')],
                          epilogue=lambda acc, b: (_sigmoid(acc + b),))
        merged, pb = _merge_fwd(o_all, W['w_branch'][l], gate, name='merge_fwd')
        h1 = _mm(merged, W['w_out'][l], name='out_proj', extras=[(xf, 'mn')],
                 epilogue=lambda acc, xr: (ALPHA * xr + acc,))
        xmf, xmb = _ln_fwd(h1, W['ln1_g'][l], W['ln1_b'][l], name='ln_fwd')
        hid, ra = _mm(xmb, W['w_ff1'], name='ff1', out_dtypes=(BF16, BF16), bn=1024, bk=FF_SHARD,
                      b_view=(D_MODEL, D_FF, (None, None, FF_SHARD, 1024), lambda i, j, k: (j // 2, l, 0, j % 2)),
                      epilogue=lambda acc: (jnp.square(jnp.maximum(acc, 0.0)), jnp.maximum(acc, 0.0)))
        h2 = _mm(hid, W['w_ff2'], name='ff2', extras=[(xmf, 'mn')], bn=1024, bk=FF_SHARD,
                 b_view=(D_FF, D_MODEL, (None, None, FF_SHARD, 1024), lambda i, j, k: (k, l, 0, j)),
                 epilogue=lambda acc, xr: (ALPHA * xr + acc,))
        saved.append(dict(xb=xb, u_att=u_att, u_rec=u_rec, ffl=ffl, bf=bf, fcum=Fc, frow=frow, lse=lse, lp=lp,
                          hs=hs, bias=bias, o_all=o_all, gate=gate, merged=merged, pb=pb, h1=h1, xmb=xmb,
                          hid=hid, ra=ra, h2=h2))
        xf, xb = _ln_fwd(h2, W['ln2_g'][l], W['ln2_b'][l], name='ln_fwd')

    dx, loss_tile = _loss_head(xf, tgt, name='loss_head')
    loss = loss_tile[0, 0]

    for l in reversed(range(DEPTH)):
        S = saved[l]
        dh2, dh2b, G[('ln2_g', l)], G[('ln2_b', l)] = _ln_bwd(S['h2'], dx, W['ln2_g'][l], name='ln_bwd')
        da = _mm(dh2b, W['w_ff2'], tb=True, name='ff2_dx', out_dtypes=(BF16,), extras=[(S['ra'], 'mn')],
                 bn=1024, bk=FF_SHARD,
                 b_view=(D_MODEL, D_FF, (None, None, 1024, FF_SHARD), lambda i, j, k: (j // 2, l, j % 2, 0)),
                 epilogue=lambda acc, rav: (acc * (2.0 * rav.astype(F32)),))
        G[('w_ff2', l)] = _mm(S['hid'], dh2b, ta=True, name='ff2_dw').reshape(N_CHIPS, D_FF // N_CHIPS, D_MODEL)
        G[('w_ff1', l)] = _mm(S['xmb'], da, ta=True, name='ff1_dw', bm=1024, bn=1024,
                              out_map=_col_split(D_MODEL, D_FF, 1024, 1024))
        dxm = _mm(da, W['w_ff1'], tb=True, name='ff1_dx', extras=[(dh2, 'mn')], bn=1024, bk=FF_SHARD,
                  b_view=(D_FF, D_MODEL, (None, None, 1024, FF_SHARD), lambda i, j, k: (k, l, j, 0)),
                  epilogue=lambda acc, d: (ALPHA * d + acc,))
        dh1, dh1b, G[('ln1_g', l)], G[('ln1_b', l)] = _ln_bwd(S['h1'], dxm, W['ln1_g'][l], name='ln_bwd')
        dm = _mm(dh1b, W['w_out'][l], tb=True, name='out_dx')
        G[('w_out', l)] = _mm(S['merged'], dh1b, ta=True, name='out_dw').reshape(
            N_CHIPS, D_MODEL // N_CHIPS, D_MODEL)
        dpb, dzg, G[('b_gate', l)] = _merge_bwd(dm, S['gate'], S['pb'], name='merge_bwd')
        do = [_mm(dpb[g], W['w_branch'][l][g], tb=True, name='branch_dx', out_dtypes=(BF16,)) for g in range(N_BRANCH)]
        G[('w_branch', l)] = jnp.stack(
            [_mm(S['o_all'][g], dpb[g], ta=True, name='branch_dw', bm=BRANCH_WIDTH, bn=BRANCH_WIDTH,
                 out_map=_col_split(BRANCH_WIDTH, D_MODEL, BRANCH_WIDTH, BRANCH_WIDTH))
             for g in range(N_BRANCH)], axis=1)
        G[('w_gate', l)] = _mm(S['xb'], dzg, ta=True, name='gate_dw', bm=1024, bn=1024,
                               out_map=((N_CHIPS, N_BRANCH, D_MODEL // N_CHIPS, D_MODEL),
                                        (2, None, D_MODEL // N_CHIPS, 1024),
                                        lambda i, j: (i, j // 2, 0, j % 2)))
        u_att, u_rec = S['u_att'], S['u_rec']
        delta = _row_dot(do[0], S['o_all'][0], name='row_dot')
        fdq, fdk, fdv, dfk, dfq = carried(_fox_bwd, ('fox_bwd', l), u_att, do[0], S['lse'], delta, S['fcum'],
                                          S['frow'])
        dff, dbf = _forget_bwd(_pad_lanes(dfk.reshape(H, T).T, LANES), _pad_lanes(dfq.reshape(H, T).T, LANES),
                               S['ffl'], S['bf'], name='forget_bwd')
        G[('b_forget', l)] = dbf[0, :H]
        (drx, dry, dcw, dcb, G[('w_r', l)], dbr, G[('w_i', l)], dbi, dlam) = _lru_bwd(
            u_rec, S['hs'], do[1], S['lp'], name='lru_bwd')
        G[('conv_w', l)], G[('conv_b', l)] = dcw[:CONV_WIDTH], dcb[0]
        G[('b_r', l)], G[('b_i', l)], G[('lru_lambda', l)] = dbr[0], dbi[0], dlam[0]
        sdq, sdk, sdv = carried(_sb_bwd, ('sb_bwd', l), u_att, do[2])
        cdq, cdk, cdv, dbias = carried(_chunk_bwd, ('chunk_bwd', l), u_att, S['bias'], do[3])
        dtab = _band_bias_bwd(jnp.pad(dbias, ((0, 8 - H), (0, 0), (0, 0))).transpose(1, 0, 2), name='band_bias_bwd')
        G[('rel_bias', l)] = dtab[:H, :REL_TABLE]
        du_att = jnp.concatenate([fdq, fdk, fdv, sdq, sdk, sdv, cdq, cdk, cdv], axis=1)
        du_rec = jnp.concatenate([drx, dry, dff], axis=1)
        G[('w_att', l)] = _mm(S['xb'], du_att, ta=True, name='in_att_dw')
        G[('w_rec', l)] = _mm(S['xb'], du_rec, ta=True, name='in_rec_dw')
        t1 = mm_carried(('gate_dx', l), dzg, W['w_gate_cat'][l], tb=True, extras=[(dh1, 'mn')],
                        epilogue=lambda acc, d: (ALPHA * d + acc,))
        t2 = mm_carried(('in_att_dx', l), du_att, W['w_att'][l], tb=True, extras=[(t1, 'mn')],
                        epilogue=lambda acc, d: (d + acc,))
        dx = _mm(du_rec, W['w_rec'][l], tb=True, name='in_rec_dx', extras=[(t2, 'mn')],
                 epilogue=lambda acc, d: (d + acc,))

    gx, _, G[('ln_in_g', -1)], G[('ln_in_b', -1)] = _ln_bwd(x, dx, W['ln_in_g'], name='ln_in_bwd')
    return loss, gx, G


_IN_FQKV = (0, 1536)
_IN_FF = (1536, 1540)
_IN_REC = (1540, 2564)
_IN_REST = (2564, D_IN)


def _prep_weights(full, W=None):
    W = {} if W is None else W
    for n, a in full.items():
        if n == 'w_in':
            L = a.shape[0]
            W['w_att'] = jnp.concatenate([a[..., _IN_FQKV[0]:_IN_FQKV[1]], a[..., _IN_REST[0]:_IN_REST[1]]],
                                         -1).astype(BF16)
            W['w_rec'] = jnp.concatenate([a[..., _IN_REC[0]:_IN_REC[1]], a[..., _IN_FF[0]:_IN_FF[1]],
                                          jnp.zeros((L, D_MODEL, N_REC - 1024 - N_HEADS), a.dtype)], -1).astype(BF16)
        elif n == 'w_gate':
            W['w_gate_cat'] = a.transpose(0, 2, 1, 3).reshape(a.shape[0], D_MODEL, N_BRANCH * D_MODEL).astype(BF16)
        elif n == 'b_gate':
            W['b_gate'] = a.reshape(a.shape[0], N_BRANCH * D_MODEL)
        elif n == 'w_ff1' and a.ndim == 3:
            W[n] = a.reshape(a.shape[0], D_MODEL, 4, FF_SHARD).transpose(2, 0, 1, 3).astype(BF16)
        elif n == 'w_ff2' and a.ndim == 3:
            W[n] = a.reshape(a.shape[0], 4, FF_SHARD, D_MODEL).transpose(1, 0, 2, 3).astype(BF16)
        elif n in ('w_branch', 'w_out', 'w_ff1', 'w_ff2'):
            W[n] = a.astype(BF16)
        else:
            W[n] = a
    return W


def _grads_to_reference_layout(G):
    out = {'ln_in_g': G[('ln_in_g', -1)][0], 'ln_in_b': G[('ln_in_b', -1)][0]}
    st = lambda n: jnp.stack([G[(n, l)] for l in range(DEPTH)])
    g_att, g_rec = st('w_att'), st('w_rec')
    out['w_in'] = jnp.concatenate([g_att[..., :1536], g_rec[..., 1024:1024 + N_HEADS], g_rec[..., :1024],
                                   g_att[..., 1536:]], -1)
    out['w_gate'] = st('w_gate').transpose(0, 2, 1, 3, 4).reshape(DEPTH, N_BRANCH, D_MODEL, D_MODEL)
    out['w_branch'] = st('w_branch').transpose(0, 2, 3, 1, 4).reshape(DEPTH, N_BRANCH, BRANCH_WIDTH, D_MODEL)
    out['w_ff1'] = st('w_ff1').transpose(0, 2, 1, 3).reshape(DEPTH, D_MODEL, D_FF)
    out['w_ff2'] = st('w_ff2').reshape(DEPTH, D_FF, D_MODEL)
    out['w_out'] = st('w_out').reshape(DEPTH, D_MODEL, D_MODEL)
    out['b_gate'] = st('b_gate').reshape(DEPTH, N_BRANCH, D_MODEL)
    for n in ('ln1_g', 'ln1_b', 'ln2_g', 'ln2_b'):
        out[n] = st(n)[:, 0]
    for n in ('b_forget', 'conv_w', 'conv_b', 'w_r', 'b_r', 'w_i', 'b_i', 'lru_lambda', 'rel_bias'):
        out[n] = st(n)
    return out


HBM_SPEC = pl.BlockSpec(memory_space=pl.ANY)
N_CHIPS = 4
PACK_COLS = 1024


def _place():
    x, y, c = lax.axis_index("x"), lax.axis_index("y"), lax.axis_index("c")
    chips = [(1 - x, y), (x, 1 - y), (1 - x, 1 - y)]
    return x, y, c, chips


def _remote(src, dst, send_sems, recv_sems, k, to):
    return pltpu.make_async_remote_copy(src_ref=src, dst_ref=dst, send_sem=send_sems.at[k], recv_sem=recv_sems.at[k],
                                        device_id=to, device_id_type=MESH)


class _Exchange:
    def __init__(self, ins, out_shapes, n_sems, start, finish, mid=None):
        self.ins, self.out_shapes, self.n_sems = list(ins), list(out_shapes), n_sems
        self.start, self.mid, self.finish = start, mid, finish


def _gather_spec(params):
    n = len(params)

    def start(ins, outs, ss, rs):
        x, y, c, chips = _place()
        for p in range(n):
            _remote(ins[p], outs[p].at[2 * x + y], ss, rs, 6 * n + p, (x, y, 1 - c)).start()
            for j, (cx, cy) in enumerate(chips):
                _remote(ins[p].at[c], outs[p].at[2 * x + y, c], ss, rs, 6 * p + j, (cx, cy, c)).start()

    def mid(ins, outs, ss, rs):
        x, y, c, chips = _place()
        for p in range(n):
            for j, (cx, cy) in enumerate(chips):
                blk = outs[p].at[2 * cx + cy, c]
                _remote(blk, blk, ss, rs, 6 * p + j, (x, y, c)).wait_recv()
                _remote(blk, blk, ss, rs, 6 * p + 3 + j, (x, y, 1 - c)).start()

    def finish(ins, outs, ss, rs):
        x, y, c, chips = _place()
        me = (x, y, c)
        for p in range(n):
            for j, (cx, cy) in enumerate(chips):
                theirs = outs[p].at[2 * cx + cy, 1 - c]
                _remote(theirs, theirs, ss, rs, 6 * p + 3 + j, me).wait_recv()
        for p in range(n):
            for j, (cx, cy) in enumerate(chips):
                _remote(ins[p].at[c], outs[p].at[2 * x + y, c], ss, rs, 6 * p + j, me).wait_send()
                blk = outs[p].at[2 * cx + cy, c]
                _remote(blk, blk, ss, rs, 6 * p + 3 + j, me).wait_send()
            _remote(ins[p], outs[p].at[2 * x + y], ss, rs, 6 * n + p, me).wait()

    shapes = [jax.ShapeDtypeStruct((N_CHIPS,) + a.shape, a.dtype) for a in params]
    return _Exchange(params, shapes, 7 * n, start, finish, mid)


def _pair_spec(g0, g1):
    n = len(g0)

    def start(ins, outs, ss, rs):
        x, y, c, _ = _place()

        @pl.when(c == 0)
        def _():
            for p in range(n):
                _remote(ins[n + p], outs[p], ss, rs, p, (x, y, 1 - c)).start()

        @pl.when(c == 1)
        def _():
            for p in range(n):
                _remote(ins[p], outs[p], ss, rs, p, (x, y, 1 - c)).start()

    def finish(ins, outs, ss, rs):
        x, y, c, _ = _place()
        for p in range(n):
            _remote(ins[p], outs[p], ss, rs, p, (x, y, 1 - c)).wait()

    return _Exchange(list(g0) + list(g1), [jax.ShapeDtypeStruct(a.shape, a.dtype) for a in g0], n, start, finish)


def _chip_spec(s):
    n = len(s)

    def start(ins, outs, ss, rs):
        x, y, c, chips = _place()
        for p in range(n):
            for j, (cx, cy) in enumerate(chips):
                _remote(ins[p].at[2 * cx + cy], outs[p].at[2 * x + y], ss, rs, 3 * p + j, (cx, cy, c)).start()

    def finish(ins, outs, ss, rs):
        x, y, c, chips = _place()
        for p in range(n):
            for j, (cx, cy) in enumerate(chips):
                slot = outs[p].at[2 * cx + cy]
                _remote(slot, slot, ss, rs, 3 * p + j, (x, y, c)).wait_recv()
        for p in range(n):
            for j, (cx, cy) in enumerate(chips):
                _remote(ins[p].at[2 * cx + cy], outs[p].at[2 * x + y], ss, rs, 3 * p + j, (x, y, c)).wait_send()

    return _Exchange(s, [jax.ShapeDtypeStruct(a.shape, a.dtype) for a in s], 3 * n, start, finish)


def _exchange(ex, *, name):
    ni, no = len(ex.ins), len(ex.out_shapes)

    def body(*refs):
        ins, outs = refs[:ni], refs[ni:ni + no]
        ss, rs = refs[ni + no:]
        ex.start(ins, outs, ss, rs)
        if ex.mid is not None:
            ex.mid(ins, outs, ss, rs)
        ex.finish(ins, outs, ss, rs)

    return list(pl.pallas_call(
        body, name=name, in_specs=[HBM_SPEC] * ni, out_specs=[HBM_SPEC] * no, out_shape=ex.out_shapes,
        scratch_shapes=[pltpu.SemaphoreType.DMA((ex.n_sems,)), pltpu.SemaphoreType.DMA((ex.n_sems,))],
    )(*ex.ins))


def _call_with_carry(kern, *, name, grid, in_specs, out_specs, out_shape, scratch_shapes, args, carry=None,
                     semantics=("parallel", "arbitrary")):
    out_specs, out_shape = list(out_specs), list(out_shape)
    if carry is None:
        res = pl.pallas_call(kern, name=name, grid=grid, in_specs=in_specs, out_specs=out_specs, out_shape=out_shape,
                             scratch_shapes=scratch_shapes, compiler_params=_cp(semantics))(*args)
        return list(res), []
    ni, no, ns = len(in_specs), len(out_specs), len(scratch_shapes)
    ci, co = len(carry.ins), len(carry.out_shapes)

    def wrapped(*refs):
        ins, cins = refs[:ni], refs[ni:ni + ci]
        outs, couts = refs[ni + ci:ni + ci + no], refs[ni + ci + no:ni + ci + no + co]
        scratch = refs[ni + ci + no + co:ni + ci + no + co + ns]
        ss, rs = refs[-2:]
        ids = [pl.program_id(d) for d in range(len(grid))]
        at = lambda pos: functools.reduce(lambda p, q: p & q, [i == v for i, v in zip(ids, pos)])

        @pl.when(at([0] * len(grid)))
        def _():
            carry.start(cins, couts, ss, rs)

        kern(*ins, *outs, *scratch)

        if carry.mid is not None:
            @pl.when(at([grid[0] - 1, grid[1] // 2] + [0] * (len(grid) - 2)))
            def _():
                carry.mid(cins, couts, ss, rs)

        @pl.when(at([g - 1 for g in grid]))
        def _():
            carry.finish(cins, couts, ss, rs)

    res = pl.pallas_call(
        wrapped, name=name, grid=grid, in_specs=list(in_specs) + [HBM_SPEC] * ci,
        out_specs=out_specs + [HBM_SPEC] * co, out_shape=out_shape + carry.out_shapes,
        scratch_shapes=list(scratch_shapes) + [pltpu.SemaphoreType.DMA((carry.n_sems,)),
                                               pltpu.SemaphoreType.DMA((carry.n_sems,))],
        compiler_params=_cp(("arbitrary",) * len(grid)))(*args, *carry.ins)
    return list(res[:no]), list(res[no:])


def _pair_swap(r, *, name):
    n = len(r)

    def body(*refs):
        ins, outs = refs[:n], refs[n:2 * n]
        send_sems, recv_sems = refs[2 * n:]
        x, y, c, _ = _place()
        cps = [_remote(ins[p], outs[p], send_sems, recv_sems, p, (x, y, 1 - c)) for p in range(n)]
        for cp in cps:
            cp.start()
        for cp in cps:
            cp.wait()

    return pl.pallas_call(
        body, name=name, in_specs=[HBM_SPEC] * n, out_specs=[HBM_SPEC] * n,
        out_shape=[jax.ShapeDtypeStruct(a.shape, a.dtype) for a in r],
        scratch_shapes=[pltpu.SemaphoreType.DMA((n,)), pltpu.SemaphoreType.DMA((n,))],
    )(*r)


def _gather8_spec(v):
    R, C = v.shape
    flips = [(bx, by, bc) for bx in (0, 1) for by in (0, 1) for bc in (0, 1)][1:]
    flip = lambda a_, b_: 1 - a_ if b_ else a_

    def start(ins, outs, ss, rs):
        x, y, c, _ = _place()
        mine = outs[0].at[4 * x + 2 * y + c]
        pltpu.make_async_copy(ins[0], mine, ss.at[7]).start()
        for j, (bx, by, bc) in enumerate(flips):
            _remote(ins[0], mine, ss, rs, j, (flip(x, bx), flip(y, by), flip(c, bc))).start()

    def finish(ins, outs, ss, rs):
        x, y, c, _ = _place()
        mine = outs[0].at[4 * x + 2 * y + c]
        for j, (bx, by, bc) in enumerate(flips):
            slot = outs[0].at[4 * flip(x, bx) + 2 * flip(y, by) + flip(c, bc)]
            _remote(slot, slot, ss, rs, j, (x, y, c)).wait_recv()
        for j in range(7):
            _remote(ins[0], mine, ss, rs, j, (x, y, c)).wait_send()
        pltpu.make_async_copy(ins[0], mine, ss.at[7]).wait()

    return _Exchange([v], [jax.ShapeDtypeStruct((8, R, C), v.dtype)], 8, start, finish)


def _row_block(rows, cols, limit=256 * 1024):
    if rows * cols <= limit:
        return rows
    for br in range(limit // cols // 8 * 8, 0, -8):
        if rows % br == 0:
            return br
    return rows


def _sum_slots(buf, *, name):
    n, R, C = buf.shape
    br = _row_block(R, C, limit=64 * 1024)

    def kern(b_ref, o_ref):
        acc = b_ref[0].astype(F32)
        for s in range(1, n):
            acc = acc + b_ref[s].astype(F32)
        o_ref[...] = acc

    return pl.pallas_call(
        kern, name=name, grid=(pl.cdiv(R, br),),
        in_specs=[pl.BlockSpec((n, br, C), lambda i: (0, i, 0))],
        out_specs=pl.BlockSpec((br, C), lambda i: (i, 0)),
        out_shape=jax.ShapeDtypeStruct((R, C), F32),
        compiler_params=_cp(("arbitrary",)),
    )(buf)


def _scalar(s):
    return jnp.reshape(s, (1,)).astype(jnp.int32)


def _sum_pair(g0, g1, other, c, *, name):
    _, R, C = g0.shape
    br = _row_block(R, C)

    def kern(c_ref, g0_ref, g1_ref, o_ref, out_ref):
        own = jnp.where(c_ref[0] == 0, g0_ref[...], g1_ref[...])
        out_ref[...] = (own + o_ref[...]).astype(out_ref.dtype)

    blk = (None, br, C)
    return pl.pallas_call(
        kern, name=name,
        grid_spec=pltpu.PrefetchScalarGridSpec(
            num_scalar_prefetch=1, grid=(N_CHIPS, R // br),
            in_specs=[pl.BlockSpec(blk, lambda k, i, cr: (k, i * (1 - cr[0]), 0)),
                      pl.BlockSpec(blk, lambda k, i, cr: (k, i * cr[0], 0)),
                      pl.BlockSpec(blk, lambda k, i, cr: (k, i, 0))],
            out_specs=pl.BlockSpec(blk, lambda k, i, cr: (k, i, 0))),
        out_shape=jax.ShapeDtypeStruct((N_CHIPS, R, C), BF16),
        compiler_params=_cp(("arbitrary", "arbitrary")),
    )(_scalar(c), g0, g1, other)


def _sum_chips(s, got, k, *, name):
    _, R, C = s.shape
    br = _row_block(R, C)

    def kern(k_ref, s_ref, a_ref, b_ref, c_ref, out_ref):
        out_ref[...] = ((s_ref[...].astype(F32) + a_ref[...].astype(F32)) + b_ref[...].astype(F32)) \
            + c_ref[...].astype(F32)

    blk = (None, br, C)
    peer = lambda d: pl.BlockSpec(blk, lambda i, kr: ((kr[0] + d) % N_CHIPS, i, 0))
    return pl.pallas_call(
        kern, name=name,
        grid_spec=pltpu.PrefetchScalarGridSpec(
            num_scalar_prefetch=1, grid=(R // br,),
            in_specs=[peer(0), peer(1), peer(2), peer(3)],
            out_specs=pl.BlockSpec((br, C), lambda i, kr: (i, 0))),
        out_shape=jax.ShapeDtypeStruct((R, C), F32),
        compiler_params=_cp(("arbitrary",)),
    )(_scalar(k), s, got, got, got)


def _adam_math(w, g, m, v):
    nm = ADAM_B1 * m + (1.0 - ADAM_B1) * g
    nv = ADAM_B2 * v + (1.0 - ADAM_B2) * jnp.square(g)
    m_hat = nm / (1.0 - ADAM_B1 ** ADAM_STEP)
    v_hat = nv / (1.0 - ADAM_B2 ** ADAM_STEP)
    return -ADAM_LR * (m_hat / (jnp.sqrt(v_hat) + ADAM_EPS) + ADAM_WD * w), nm, nv


def _adamw_layers(w, mine, theirs, m, v, c, *, name):
    shape = w.shape
    R, C = mine.shape
    w3, m3, v3 = (a.reshape(DEPTH, R, C) for a in (w, m, v))
    br = _row_block(R, C)

    def kern(c_ref, w_ref, a_ref, b_ref, m_ref, v_ref, g_ref, d_ref, nm_ref, nv_ref):
        g = jnp.where(pl.program_id(0) == c_ref[0], a_ref[...], b_ref[...])
        g_ref[...] = g
        d_ref[...], nm_ref[...], nv_ref[...] = _adam_math(w_ref[...], g, m_ref[...], v_ref[...])

    lay = pl.BlockSpec((None, br, C), lambda l, i, cr: (l, i, 0))
    outs = pl.pallas_call(
        kern, name=name,
        grid_spec=pltpu.PrefetchScalarGridSpec(
            num_scalar_prefetch=1, grid=(DEPTH, R // br),
            in_specs=[lay,
                      pl.BlockSpec((br, C), lambda l, i, cr: (jnp.where(l == cr[0], i, 0), 0)),
                      pl.BlockSpec((br, C), lambda l, i, cr: (jnp.where(l == cr[0], 0, i), 0)),
                      lay, lay],
            out_specs=[lay] * 4),
        out_shape=[jax.ShapeDtypeStruct((DEPTH, R, C), F32)] * 4,
        compiler_params=_cp(("arbitrary", "arbitrary")),
    )(_scalar(c), w3, mine, theirs, m3, v3)
    return [o.reshape(shape) for o in outs]


def _adamw(w, g, m, v, *, name):
    shape = w.shape
    cols = shape[-1]
    w2, g2, m2, v2 = (a.reshape(-1, cols) for a in (w, g, m, v))
    rows = w2.shape[0]
    br = _row_block(rows, cols)

    def kern(w_ref, g_ref, m_ref, v_ref, d_ref, nm_ref, nv_ref):
        gv = g_ref[...]
        nm = ADAM_B1 * m_ref[...] + (1.0 - ADAM_B1) * gv
        nv = ADAM_B2 * v_ref[...] + (1.0 - ADAM_B2) * jnp.square(gv)
        m_hat = nm / (1.0 - ADAM_B1 ** ADAM_STEP)
        v_hat = nv / (1.0 - ADAM_B2 ** ADAM_STEP)
        d_ref[...] = -ADAM_LR * (m_hat / (jnp.sqrt(v_hat) + ADAM_EPS) + ADAM_WD * w_ref[...])
        nm_ref[...] = nm
        nv_ref[...] = nv

    spec = pl.BlockSpec((br, cols), lambda i: (i, 0))
    outs = pl.pallas_call(
        kern, name=name, grid=(rows // br,), in_specs=[spec] * 4, out_specs=[spec] * 3,
        out_shape=[jax.ShapeDtypeStruct((rows, cols), F32)] * 3,
        compiler_params=_cp(("arbitrary",)),
    )(w2, g2, m2, v2)
    return [o.reshape(shape) for o in outs]


_NAMES = ['ln_in_g', 'ln_in_b', 'w_in', 'b_forget', 'conv_w', 'conv_b', 'w_r', 'b_r', 'w_i', 'b_i', 'lru_lambda',
          'rel_bias', 'w_branch', 'w_gate', 'b_gate', 'w_out', 'ln1_g', 'ln1_b', 'w_ff1', 'w_ff2', 'ln2_g', 'ln2_b']
_BIG = {'w_in': 2, 'w_branch': 3, 'w_gate': 2, 'w_out': 1, 'w_ff1': 2, 'w_ff2': 1}
_SMALL_SHARDED = {'b_gate': 2, 'conv_w': 2, 'rel_bias': 2}
_SHARDED = {**_BIG, **_SMALL_SHARDED}
_REPLICATED = [n for n in _NAMES if n not in _SHARDED]
_TILE = 8 * LANES


def _tiles(a, cols):
    flat = a.reshape(-1)
    per = 8 * cols
    flat = jnp.pad(flat, (0, (-flat.shape[0]) % per))
    return flat.reshape(-1, cols)


def _pack(arrs, cols):
    return jnp.concatenate([_tiles(a, cols) for a in arrs], axis=0)


def _unpack(packed, like, cols):
    out, r0 = [], 0
    for a in like:
        n = math.prod(a.shape)
        rows = -(-n // (8 * cols)) * 8
        out.append(packed[r0:r0 + rows].reshape(-1)[:n].reshape(a.shape))
        r0 += rows
    return out


_EARLY = ['w_branch', 'w_gate', 'w_out', 'w_ff1', 'w_ff2']


def _chip_major_early(G, l):
    return [G[('w_branch', l)].reshape(N_CHIPS, N_BRANCH * BRANCH_WIDTH, BRANCH_WIDTH),
            G[('w_gate', l)].reshape(N_CHIPS, N_BRANCH * (D_MODEL // N_CHIPS), D_MODEL),
            G[('w_out', l)], G[('w_ff1', l)], G[('w_ff2', l)]]


def _chip_major_late(G, l):
    g_att, g_rec = G[('w_att', l)], G[('w_rec', l)]
    w_in = jnp.concatenate([g_att[:, :1536], g_rec[:, 1024:1024 + N_HEADS], g_rec[:, :1024], g_att[:, 1536:]], -1)
    per_chip = lambda g, rows: g.reshape(rows, N_CHIPS, -1).transpose(1, 0, 2)
    bg = per_chip(G[('b_gate', l)], N_BRANCH)
    cw = per_chip(G[('conv_w', l)], CONV_WIDTH)
    rb = per_chip(G[('rel_bias', l)], N_HEADS)
    small = jnp.stack([_pack([bg[j], cw[j], rb[j]], LANES) for j in range(N_CHIPS)])
    return [w_in.reshape(D_MODEL, N_CHIPS, D_IN // N_CHIPS).transpose(1, 0, 2), small]


class _Hook:
    def __init__(self, spec, done):
        self.spec, self.done = spec, done


def _unshard(blocks, axis):
    return jnp.concatenate([blocks[k] for k in range(N_CHIPS)], axis=axis)


def kernel(x, ln_in_g, ln_in_b, w_in, b_forget, conv_w, conv_b, w_r, b_r, w_i, b_i, lru_lambda, rel_bias, w_branch, w_gate, b_gate, w_out, ln1_g, ln1_b, w_ff1, w_ff2, ln2_g, ln2_b, loss_target, m_ln_in_g, m_ln_in_b, m_w_in, m_b_forget, m_conv_w, m_conv_b, m_w_r, m_b_r, m_w_i, m_b_i, m_lru_lambda, m_rel_bias, m_w_branch, m_w_gate, m_b_gate, m_w_out, m_ln1_g, m_ln1_b, m_w_ff1, m_w_ff2, m_ln2_g, m_ln2_b, v_ln_in_g, v_ln_in_b, v_w_in, v_b_forget, v_conv_w, v_conv_b, v_w_r, v_b_r, v_w_i, v_b_i, v_lru_lambda, v_rel_bias, v_w_branch, v_w_gate, v_b_gate, v_w_out, v_ln1_g, v_ln1_b, v_w_ff1, v_w_ff2, v_ln2_g, v_ln2_b):
    w = dict(zip(_NAMES, (ln_in_g, ln_in_b, w_in, b_forget, conv_w, conv_b, w_r, b_r, w_i, b_i, lru_lambda, rel_bias,
                          w_branch, w_gate, b_gate, w_out, ln1_g, ln1_b, w_ff1, w_ff2, ln2_g, ln2_b)))
    m = dict(zip(_NAMES, (m_ln_in_g, m_ln_in_b, m_w_in, m_b_forget, m_conv_w, m_conv_b, m_w_r, m_b_r, m_w_i, m_b_i,
                          m_lru_lambda, m_rel_bias, m_w_branch, m_w_gate, m_b_gate, m_w_out, m_ln1_g, m_ln1_b,
                          m_w_ff1, m_w_ff2, m_ln2_g, m_ln2_b)))
    v = dict(zip(_NAMES, (v_ln_in_g, v_ln_in_b, v_w_in, v_b_forget, v_conv_w, v_conv_b, v_w_r, v_b_r, v_w_i, v_b_i,
                          v_lru_lambda, v_rel_bias, v_w_branch, v_w_gate, v_b_gate, v_w_out, v_ln1_g, v_ln1_b,
                          v_w_ff1, v_w_ff2, v_ln2_g, v_ln2_b)))
    c = lax.axis_index("c")

    k = 2 * lax.axis_index("x") + lax.axis_index("y")
    state = {}

    small_like = [w[n] for n in _SMALL_SHARDED]
    small_pack = jnp.stack([_pack([a[l] for a in small_like], LANES) for l in range(DEPTH)])
    W = _prep_weights({n: w[n] for n in _REPLICATED})
    got_in, got_small = _exchange(_gather_spec([w['w_in'].astype(BF16), small_pack]), name='gather_first')
    small_blocks = [[_unpack(got_small[j, l], [a[l] for a in small_like], LANES) for l in range(DEPTH)]
                    for j in range(N_CHIPS)]
    first = {'w_in': _unshard(got_in, _BIG['w_in'])}
    for i, n in enumerate(_SMALL_SHARDED):
        first[n] = jnp.concatenate([jnp.stack([small_blocks[j][l][i] for l in range(DEPTH)])
                                    for j in range(N_CHIPS)], axis=_SMALL_SHARDED[n])
    _prep_weights(first, W)

    def gather_on(names):
        chip_major = ('w_ff1', 'w_ff2')
        return _Hook(lambda W_, G_: _gather_spec([w[n].astype(BF16) for n in names]),
                     lambda outs, W_, G_: _prep_weights(
                         {n: o if n in chip_major else _unshard(o, _BIG[n]) for n, o in zip(names, outs)}, W_))

    def pair_spec(W_, G_):
        state['early'] = [_chip_major_early(G_, l) for l in range(DEPTH)]
        return _pair_spec(*state['early'])

    def pair_done(outs, W_, G_):
        state['pair_sum'] = [_sum_pair(a0, a1, o, c, name='grad_pair_sum')
                             for a0, a1, o in zip(*state['early'], outs)]

    def late_spec(W_, G_):
        state['late'] = [_chip_major_late(G_, l) for l in range(DEPTH)]
        return _pair_spec(*state['late'])

    def late_done(outs, W_, G_):
        state['late_sum'] = [_sum_pair(a0, a1, o, c, name='grad_pair_sum') for a0, a1, o in zip(*state['late'], outs)]

    rep_main = _REPLICATED[2:]

    def rep_spec(W_, G_):
        dev = [jnp.stack([G_[(n, l)].reshape(w[n].shape[1:]) for l in range(DEPTH)]) for n in rep_main]
        packed = _pack(dev, LANES)
        return _gather8_spec(jnp.pad(packed, ((0, (-packed.shape[0]) % 256), (0, 0))))

    hooks = {('in_proj_att', 0): gather_on(['w_out']),
             ('fox_fwd', 0): gather_on(['w_gate']),
             ('sb_fwd', 0): gather_on(['w_ff1']),
             ('chunk_fwd', 0): gather_on(['w_branch']),
             ('gate_proj', 0): gather_on(['w_ff2']),
             ('chunk_bwd', 0): _Hook(rep_spec, lambda outs, W_, G_: state.update(rep_all=outs[0])),
             ('gate_dx', 0): _Hook(late_spec, late_done),
             ('in_att_dx', 0): _Hook(lambda W_, G_: _chip_spec(state['late_sum']),
                                     lambda outs, W_, G_: state.update(late_chips=outs)),
             ('fox_bwd', 0): _Hook(pair_spec, pair_done),
             ('sb_bwd', 0): _Hook(lambda W_, G_: _chip_spec(state['pair_sum']),
                                  lambda outs, W_, G_: state.update(from_chips=outs))}
    loss, gx, G = _device_step(x[0], loss_target[0], W, hooks)

    late_sum, late_chips = state['late_sum'], state['late_chips']
    pair_sum = [late_sum[0]] + state['pair_sum'] + [late_sum[1]]
    from_chips = [late_chips[0]] + state['from_chips'] + [late_chips[1]]
    mine = [_sum_chips(s, got, k, name='grad_chip_sum') for s, got in zip(pair_sum, from_chips)]
    theirs = _pair_swap(mine, name='grad_pair_swap')

    g_rep = dict(zip(rep_main, _unpack(_sum_slots(state['rep_all'], name='grad_sum8'), [w[n] for n in rep_main], LANES)))
    entry = _exchange(_gather8_spec(_pack([G[('ln_in_g', -1)][0], G[('ln_in_b', -1)][0]], LANES)),
                      name='grad_gather8')[0]
    g_rep.update(zip(_REPLICATED[:2], _unpack(_sum_slots(entry, name='grad_sum8'), [w[n] for n in _REPLICATED[:2]],
                                              LANES)))

    grads, delta, new_m, new_v = {}, {}, {}, {}
    for n, a, b in zip(_BIG, mine, theirs):
        grads[n], delta[n], new_m[n], new_v[n] = _adamw_layers(w[n], a, b, m[n], v[n], c, name='adamw')
    small_layers = [jnp.where(c == l, mine[-1], theirs[-1]) for l in range(DEPTH)]
    small_shards = [_unpack(s, [w[n][0] for n in _SMALL_SHARDED], LANES) for s in small_layers]
    g_shard = {n: jnp.stack([small_shards[l][i] for l in range(DEPTH)]) for i, n in enumerate(_SMALL_SHARDED)}
    small = _REPLICATED + list(_SMALL_SHARDED)
    for n in small:
        grads[n] = g_rep[n] if n in g_rep else g_shard[n]
    packs = [_pack([d[n] for n in small], LANES) for d in (w, grads, m, v)]
    outs = _adamw(*packs, name='adamw_small')
    small_like_all = [w[n] for n in small]
    for d, o in zip((delta, new_m, new_v), outs):
        d.update(zip(small, _unpack(o, small_like_all, LANES)))

    loss = lax.psum(loss, ("x", "y", "c"))
    return (loss, gx[None], *[grads[n] for n in _NAMES], *[delta[n] for n in _NAMES],
            *[new_m[n] for n in _NAMES], *[new_v[n] for n in _NAMES])
```

```python
import functools
import math

import jax
import jax.numpy as jnp
from jax import lax
from jax.experimental import pallas as pl
from jax.experimental.pallas import tpu as pltpu

F32 = jnp.float32
BF16 = jnp.bfloat16

D_MODEL = 2048
DEPTH = 2
CHUNK = 64
HEAD_DIM = 128
N_BRANCH = 4
BRANCH_WIDTH = 512
N_HEADS = 4
CONV_WIDTH = 4
LRU_C = 8.0
LOOKBACK_CHUNKS = 8
BAND = (LOOKBACK_CHUNKS + 1) * CHUNK
PAD_ROWS = LOOKBACK_CHUNKS * CHUNK
REL_CLIP = 256
REL_TABLE = REL_CLIP + CHUNK
REL_PAD = 384
D_FF = 4 * D_MODEL
FF_SHARD = D_FF // 4
D_IN = 5636
ALPHA = (2.0 * DEPTH) ** 0.25
LN_EPS = 1e-5
SCALE = HEAD_DIM ** -0.5

ADAM_LR = 0.001
ADAM_B1 = 0.9
ADAM_B2 = 0.999
ADAM_EPS = 1e-08
ADAM_WD = 0.01
ADAM_STEP = 10

N_ATT = 9 * BRANCH_WIDTH
N_REC = 2 * BRANCH_WIDTH + 128

V7X_VMEM_LIMIT = 56 * 1024 * 1024
LANES = 128
ATT_BLOCK = 256
ATT_KEYS = 1024

NT = (((1,), (1,)), ((), ()))
TN = (((0,), (0,)), ((), ()))
NN = (((1,), (0,)), ((), ()))

MESH = pl.DeviceIdType.MESH


def _cp(sem=None):
    return pltpu.CompilerParams(dimension_semantics=sem, vmem_limit_bytes=V7X_VMEM_LIMIT)


def _dot(a, b, dims=NN):
    return lax.dot_general(a, b, dims, preferred_element_type=F32)


def _pick(n, prefs):
    for p in prefs:
        if n % p == 0:
            return p
    return n


def _split3(x):
    hi = x.astype(BF16)
    r1 = x - hi.astype(F32)
    mid = r1.astype(BF16)
    lo = (r1 - mid.astype(F32)).astype(BF16)
    return hi, mid, lo


def _split2(x):
    hi = x.astype(BF16)
    lo = (x - hi.astype(F32)).astype(BF16)
    return hi, lo


def _sigmoid(z):
    return 1.0 / (1.0 + jnp.exp(-z))


def _log_sigmoid(z):
    return jnp.minimum(z, 0.0) - jnp.log(1.0 + jnp.exp(-jnp.abs(z)))


def _mm(a, b, *, name, ta=False, tb=False, out_dtypes=(F32,), epilogue=None, extras=(),
        bm=None, bn=None, bk=None, out_map=None, b_view=None, carry=None):
    M, K = (a.shape[1], a.shape[0]) if ta else a.shape
    N = b.shape[0] if tb else b.shape[1]
    if b_view is not None:
        K, N = b_view[:2]
    bm = bm or _pick(M, (1024, 512, 256, 128))
    bn = bn or _pick(N, (1024, 1536, 1152, 512, 256, 128))
    bk = bk or _pick(K, (2048, 1536, 1024, 1152, 512, 256, 128))
    nk = K // bk
    a_spec = pl.BlockSpec((bk, bm), lambda i, j, k: (k, i)) if ta else pl.BlockSpec((bm, bk), lambda i, j, k: (i, k))
    b_spec = pl.BlockSpec((bn, bk), lambda i, j, k: (j, k)) if tb else pl.BlockSpec((bk, bn), lambda i, j, k: (k, j))
    if b_view is not None:
        b_spec = pl.BlockSpec(b_view[2], b_view[3])
    ex_specs = [pl.BlockSpec((bm, bn), lambda i, j, k: (i, j)) if kind == 'mn'
                else pl.BlockSpec((1, bn), lambda i, j, k: (0, j)) for _, kind in extras]
    n_ex, n_out = len(extras), len(out_dtypes)
    dims = TN if ta else (NT if tb else NN)

    def kern(*refs):
        a_ref, b_ref = refs[0], refs[1]
        ex_refs = refs[2:2 + n_ex]
        out_refs = refs[2 + n_ex:2 + n_ex + n_out]
        acc_ref = refs[-1]
        k = pl.program_id(2)
        part = _dot(a_ref[...].astype(BF16), b_ref[...].astype(BF16), dims)

        @pl.when(k == 0)
        def _():
            acc_ref[...] = part

        @pl.when(k > 0)
        def _():
            acc_ref[...] += part

        @pl.when(k == nk - 1)
        def _():
            acc = acc_ref[...]
            outs = (acc,) if epilogue is None else epilogue(acc, *[r[...] for r in ex_refs])
            for o_ref, o in zip(out_refs, outs):
                o_ref[...] = o.astype(o_ref.dtype).reshape(o_ref.shape)

    if out_map is None:
        out_specs = [pl.BlockSpec((bm, bn), lambda i, j, k: (i, j)) for _ in out_dtypes]
        out_shape = [jax.ShapeDtypeStruct((M, N), dt) for dt in out_dtypes]
    else:
        shape, block, index = out_map
        out_specs = [pl.BlockSpec(block, lambda i, j, k: index(i, j))]
        out_shape = [jax.ShapeDtypeStruct(shape, out_dtypes[0])]
    res, extra = _call_with_carry(
        kern, name=name, grid=(M // bm, N // bn, nk), carry=carry,
        in_specs=[a_spec, b_spec] + ex_specs, out_specs=out_specs, out_shape=out_shape,
        scratch_shapes=[pltpu.VMEM((bm, bn), F32)], args=(a, b, *[e for e, _ in extras]),
        semantics=("parallel", "parallel", "arbitrary"))
    res = res[0] if n_out == 1 else res
    return res if carry is None else (res, extra)


def _ln_fwd(h, g, b, *, name):
    T, D = h.shape
    bt = _pick(T, (512, 256, 128))

    def kern(h_ref, g_ref, b_ref, y_ref, yb_ref):
        x = h_ref[...]
        mu = jnp.mean(x, axis=-1, keepdims=True)
        xc = x - mu
        var = jnp.mean(xc * xc, axis=-1, keepdims=True)
        y = xc * lax.rsqrt(var + LN_EPS) * g_ref[...] + b_ref[...]
        y_ref[...] = y
        yb_ref[...] = y.astype(BF16)

    row = pl.BlockSpec((bt, D), lambda i: (i, 0))
    vec = pl.BlockSpec((1, D), lambda i: (0, 0))
    return pl.pallas_call(
        kern, name=name, grid=(T // bt,), in_specs=[row, vec, vec], out_specs=[row, row],
        out_shape=[jax.ShapeDtypeStruct((T, D), F32), jax.ShapeDtypeStruct((T, D), BF16)],
        compiler_params=_cp(("arbitrary",)),
    )(h, g.reshape(1, D), b.reshape(1, D))


def _ln_bwd(h, dy, g, *, name):
    T, D = h.shape
    bt = _pick(T, (512, 256, 128))

    def kern(h_ref, dy_ref, g_ref, dh_ref, dhb_ref, dg_ref, db_ref):
        i = pl.program_id(0)
        x = h_ref[...]
        dyv = dy_ref[...]
        mu = jnp.mean(x, axis=-1, keepdims=True)
        xc = x - mu
        var = jnp.mean(xc * xc, axis=-1, keepdims=True)
        rstd = lax.rsqrt(var + LN_EPS)
        xhat = xc * rstd
        dxh = dyv * g_ref[...]
        m1 = jnp.mean(dxh, axis=-1, keepdims=True)
        m2 = jnp.mean(dxh * xhat, axis=-1, keepdims=True)
        dh = rstd * (dxh - m1 - xhat * m2)
        dh_ref[...] = dh
        dhb_ref[...] = dh.astype(BF16)
        pg = jnp.sum(dyv * xhat, axis=0, keepdims=True)
        pb = jnp.sum(dyv, axis=0, keepdims=True)

        @pl.when(i == 0)
        def _():
            dg_ref[...] = pg
            db_ref[...] = pb

        @pl.when(i > 0)
        def _():
            dg_ref[...] += pg
            db_ref[...] += pb

    row = pl.BlockSpec((bt, D), lambda i: (i, 0))
    vec = pl.BlockSpec((1, D), lambda i: (0, 0))
    return pl.pallas_call(
        kern, name=name, grid=(T // bt,), in_specs=[row, row, vec], out_specs=[row, row, vec, vec],
        out_shape=[jax.ShapeDtypeStruct((T, D), F32), jax.ShapeDtypeStruct((T, D), BF16),
                   jax.ShapeDtypeStruct((1, D), F32), jax.ShapeDtypeStruct((1, D), F32)],
        compiler_params=_cp(("arbitrary",)),
    )(h, dy, g.reshape(1, D))


def _loss_head(y, tgt, *, name):
    T, D = y.shape
    bt = _pick(T, (512, 256, 128))

    def kern(y_ref, t_ref, dy_ref, loss_ref):
        i = pl.program_id(0)
        e = y_ref[...] - t_ref[...]
        dy_ref[...] = e * (1.0 / D)
        part = 0.5 * jnp.sum(jnp.sum(e * e, axis=-1, keepdims=True) * (1.0 / D), axis=0, keepdims=True)
        part = jnp.broadcast_to(part, (8, LANES))

        @pl.when(i == 0)
        def _():
            loss_ref[...] = part

        @pl.when(i > 0)
        def _():
            loss_ref[...] += part

    row = pl.BlockSpec((bt, D), lambda i: (i, 0))
    return pl.pallas_call(
        kern, name=name, grid=(T // bt,), in_specs=[row, row],
        out_specs=[row, pl.BlockSpec((8, LANES), lambda i: (0, 0))],
        out_shape=[jax.ShapeDtypeStruct((T, D), F32), jax.ShapeDtypeStruct((8, LANES), F32)],
        compiler_params=_cp(("arbitrary",)),
    )(y, tgt)


def _tri(n, upper):
    r = lax.broadcasted_iota(jnp.int32, (n, n), 0)
    c = lax.broadcasted_iota(jnp.int32, (n, n), 1)
    return jnp.where((c >= r) if upper else (c <= r), 1.0, 0.0).astype(BF16)


def _forget_fwd(ff, bf, *, name):
    T = ff.shape[0]
    bt = 256

    def kern(ff_ref, bf_ref, out_ref, carry):
        i = pl.program_id(0)

        @pl.when(i == 0)
        def _():
            carry[...] = jnp.zeros_like(carry)

        ls = _log_sigmoid(ff_ref[...] + bf_ref[...])
        tri = _tri(bt, upper=False)
        hi, mid, lo = _split3(ls)
        cs = _dot(tri, hi) + _dot(tri, mid) + _dot(tri, lo) + carry[0:1, :]
        out_ref[...] = cs
        carry[...] = jnp.broadcast_to(cs[bt - 1:bt, :], carry.shape)

    return pl.pallas_call(
        kern, name=name, grid=(T // bt,),
        in_specs=[pl.BlockSpec((bt, LANES), lambda i: (i, 0)), pl.BlockSpec((1, LANES), lambda i: (0, 0))],
        out_specs=pl.BlockSpec((bt, LANES), lambda i: (i, 0)),
        out_shape=jax.ShapeDtypeStruct((T, LANES), F32),
        scratch_shapes=[pltpu.VMEM((8, LANES), F32)],
        compiler_params=_cp(("arbitrary",)),
    )(ff, bf)


def _forget_bwd(dFk, dFq, ff, bf, *, name):
    T = ff.shape[0]
    bt = 256
    nb = T // bt

    def kern(dFk_ref, dFq_ref, ff_ref, bf_ref, dff_ref, dbf_ref, carry):
        i = pl.program_id(0)

        @pl.when(i == 0)
        def _():
            carry[...] = jnp.zeros_like(carry)
            dbf_ref[...] = jnp.zeros_like(dbf_ref)

        tri = _tri(bt, upper=True)
        hi, mid, lo = _split3(dFk_ref[...] + dFq_ref[...])
        rs = _dot(tri, hi) + _dot(tri, mid) + _dot(tri, lo) + carry[0:1, :]
        carry[...] = jnp.broadcast_to(rs[0:1, :], carry.shape)
        z = ff_ref[...] + bf_ref[...]
        dff = rs * _sigmoid(-z)
        dff_ref[...] = dff.astype(dff_ref.dtype)
        dbf_ref[...] += jnp.sum(dff, axis=0, keepdims=True)

    rev = pl.BlockSpec((bt, LANES), lambda i: (nb - 1 - i, 0))
    vec = pl.BlockSpec((1, LANES), lambda i: (0, 0))
    return pl.pallas_call(
        kern, name=name, grid=(nb,), in_specs=[rev, rev, rev, vec], out_specs=[rev, vec],
        out_shape=[jax.ShapeDtypeStruct((T, LANES), BF16), jax.ShapeDtypeStruct((1, LANES), F32)],
        scratch_shapes=[pltpu.VMEM((8, LANES), F32)],
        compiler_params=_cp(("arbitrary",)),
    )(dFk, dFq, ff, bf)


def _head_lane(x, h):
    lane = lax.broadcasted_iota(jnp.int32, x.shape, 1)
    return jnp.sum(jnp.where(lane == h, x, 0.0), axis=1, keepdims=True)


def _att_blocks(T):
    return min(ATT_BLOCK, T), min(ATT_KEYS, T)


def _positions(i, j, bq, bk):
    r = i * bq + lax.broadcasted_iota(jnp.int32, (bq, bk), 0)
    c = j * bk + lax.broadcasted_iota(jnp.int32, (bq, bk), 1)
    return r, c


def _fox_fwd(u_att, fcum, frow, *, name, carry=None):
    T = u_att.shape[0]
    bq, bk = _att_blocks(T)
    nq, nk = T // bq, T // bk
    H = N_HEADS

    def kern(q_ref, k_ref, v_ref, fc_ref, fr_ref, o_ref, lse_ref):
        i = pl.program_id(1)
        q = q_ref[...]
        fq = _head_lane(fc_ref[...], pl.program_id(0))

        def step(j, carry, masked):
            m, l, acc = carry
            off = pl.multiple_of(j * bk, bk)
            k = k_ref[pl.ds(off, bk), :]
            v = v_ref[pl.ds(off, bk), :]
            s = _dot(q, k, NT) * SCALE + (fq - fr_ref[j])
            if masked:
                r, c = _positions(i, j, bq, bk)
                s = jnp.where(c <= r, s, -jnp.inf)
            m_new = jnp.maximum(m, jnp.max(s, axis=1, keepdims=True))
            a = jnp.exp(m - m_new)
            p = jnp.exp(s - m_new)
            l = a * l + jnp.sum(p, axis=1, keepdims=True)
            acc = a * acc + _dot(p.astype(BF16), v)
            return m_new, l, acc

        init = (jnp.full((bq, 1), -1e30, F32), jnp.zeros((bq, 1), F32), jnp.zeros((bq, HEAD_DIM), F32))
        nfull = (i * bq) // bk
        carry = lax.fori_loop(0, nfull, lambda j, cr: step(j, cr, False), init)
        m, l, acc = step(nfull, carry, True)
        o_ref[...] = (acc / l).astype(o_ref.dtype)
        lse_ref[...] = m + jnp.log(l)

    return _call_with_carry(
        kern, name=name, grid=(H, nq), carry=carry,
        in_specs=[pl.BlockSpec((bq, HEAD_DIM), lambda h, i: (i, h)),
                  pl.BlockSpec((T, HEAD_DIM), lambda h, i: (0, 4 + h)),
                  pl.BlockSpec((T, HEAD_DIM), lambda h, i: (0, 8 + h)),
                  pl.BlockSpec((bq, LANES), lambda h, i: (i, 0)),
                  pl.BlockSpec((None, nk, 1, bk), lambda h, i: (h, 0, 0, 0))],
        out_specs=[pl.BlockSpec((bq, HEAD_DIM), lambda h, i: (i, h)),
                   pl.BlockSpec((None, bq, 1), lambda h, i: (h, i, 0))],
        out_shape=[jax.ShapeDtypeStruct((T, BRANCH_WIDTH), BF16), jax.ShapeDtypeStruct((H, T, 1), F32)],
        scratch_shapes=[], args=(u_att, u_att, u_att, fcum, frow))


def _row_dot(a, b, *, name):
    T = a.shape[0]
    bt = _pick(T, (512, 256, 128))

    def kern(a_ref, b_ref, o_ref):
        p = a_ref[...].astype(F32) * b_ref[...].astype(F32)
        for h in range(N_HEADS):
            o_ref[h] = jnp.sum(p[:, h * HEAD_DIM:(h + 1) * HEAD_DIM], axis=1, keepdims=True)

    row = pl.BlockSpec((bt, BRANCH_WIDTH), lambda i: (i, 0))
    return pl.pallas_call(
        kern, name=name, grid=(T // bt,), in_specs=[row, row],
        out_specs=pl.BlockSpec((N_HEADS, bt, 1), lambda i: (0, i, 0)),
        out_shape=jax.ShapeDtypeStruct((N_HEADS, T, 1), F32),
        compiler_params=_cp(("arbitrary",)),
    )(a, b)


def _fox_bwd(u_att, do, lse, delta, fcum, frow, *, name, carry=None):
    T = u_att.shape[0]
    bq, bk = _att_blocks(T)
    nq, nk = T // bq, T // bk
    H = N_HEADS

    def kern(q_ref, k_ref, v_ref, do_ref, lse_ref, dl_ref, fc_ref, fr_ref,
             dq_ref, dk_ref, dv_ref, df_ref, dfq_ref, dk_acc, dv_acc, df_acc):
        i = pl.program_id(1)

        @pl.when(i == 0)
        def _():
            dk_acc[...] = jnp.zeros_like(dk_acc)
            dv_acc[...] = jnp.zeros_like(dv_acc)
            df_acc[...] = jnp.zeros_like(df_acc)

        q = q_ref[...]
        dov = do_ref[...]
        fq = _head_lane(fc_ref[...], pl.program_id(0))
        lsev = lse_ref[...]
        dlt = dl_ref[...]

        def step(j, carry, masked):
            dq, dfq = carry
            off = pl.multiple_of(j * bk, bk)
            k = k_ref[pl.ds(off, bk), :]
            v = v_ref[pl.ds(off, bk), :]
            s = _dot(q, k, NT) * SCALE + (fq - fr_ref[j])
            p = jnp.exp(s - lsev)
            if masked:
                r, c = _positions(i, j, bq, bk)
                p = jnp.where(c <= r, p, 0.0)
            dp = _dot(dov, v, NT)
            ds = p * (dp - dlt)
            dsb = ds.astype(BF16)
            dq = dq + _dot(dsb, k)
            dk_acc[pl.ds(off, bk), :] += _dot(dsb, q, TN)
            dv_acc[pl.ds(off, bk), :] += _dot(p.astype(BF16), dov, TN)
            df_acc[j] += -jnp.sum(ds, axis=0, keepdims=True)
            return dq, dfq + jnp.sum(ds, axis=1, keepdims=True)

        nfull = (i * bq) // bk
        carry = lax.fori_loop(0, nfull, lambda j, cr: step(j, cr, False),
                              (jnp.zeros((bq, HEAD_DIM), F32), jnp.zeros((bq, 1), F32)))
        dq, dfq = step(nfull, carry, True)
        dq_ref[...] = (dq * SCALE).astype(dq_ref.dtype)
        dfq_ref[...] = dfq

        @pl.when(i == nq - 1)
        def _():
            dk_ref[...] = (dk_acc[...] * SCALE).astype(dk_ref.dtype)
            dv_ref[...] = dv_acc[...].astype(dv_ref.dtype)
            df_ref[...] = df_acc[...]

    col = lambda: pl.BlockSpec((None, bq, 1), lambda h, i: (h, i, 0))
    return _call_with_carry(
        kern, name=name, grid=(H, nq), carry=carry,
        in_specs=[pl.BlockSpec((bq, HEAD_DIM), lambda h, i: (i, h)),
                  pl.BlockSpec((T, HEAD_DIM), lambda h, i: (0, 4 + h)),
                  pl.BlockSpec((T, HEAD_DIM), lambda h, i: (0, 8 + h)),
                  pl.BlockSpec((bq, HEAD_DIM), lambda h, i: (i, h)),
                  col(), col(), pl.BlockSpec((bq, LANES), lambda h, i: (i, 0)),
                  pl.BlockSpec((None, nk, 1, bk), lambda h, i: (h, 0, 0, 0))],
        out_specs=[pl.BlockSpec((bq, HEAD_DIM), lambda h, i: (i, h)),
                   pl.BlockSpec((T, HEAD_DIM), lambda h, i: (0, h)),
                   pl.BlockSpec((T, HEAD_DIM), lambda h, i: (0, h)),
                   pl.BlockSpec((None, nk, 1, bk), lambda h, i: (h, 0, 0, 0)),
                   pl.BlockSpec((None, bq, 1), lambda h, i: (h, i, 0))],
        out_shape=[jax.ShapeDtypeStruct((T, BRANCH_WIDTH), BF16)] * 3
                  + [jax.ShapeDtypeStruct((H, nk, 1, bk), F32), jax.ShapeDtypeStruct((H, T, 1), F32)],
        scratch_shapes=[pltpu.VMEM((T, HEAD_DIM), F32), pltpu.VMEM((T, HEAD_DIM), F32),
                        pltpu.VMEM((nk, 1, bk), F32)],
        args=(u_att, u_att, u_att, do, lse, delta, fcum, frow))


def _softplus_parts(z):
    t = jnp.exp(-jnp.abs(z))
    sp = jnp.maximum(z, 0.0) + jnp.log(1.0 + t)
    return t, sp


def _sb_tri(B):
    r = lax.broadcasted_iota(jnp.int32, (B, B), 0)
    c = lax.broadcasted_iota(jnp.int32, (B, B), 1)
    suffix = jnp.where(r >= c, 1.0, 0.0).astype(BF16)
    prefix = jnp.where(r <= c, 1.0, 0.0).astype(BF16)
    return suffix, prefix


def _sb_fwd(u_att, *, name, carry=None):
    T = u_att.shape[0]
    B, bk = _att_blocks(T)
    nq, nsub = T // B, bk // B
    H = N_HEADS

    def kern(q_ref, k_ref, v_ref, o_ref):
        i = pl.program_id(1)
        q = q_ref[...]
        suffix, _ = _sb_tri(B)

        def step(j, carry, masked):
            run, acc = carry
            parts = []
            for s in reversed(range(nsub)):
                jb = j * nsub + s
                off = pl.multiple_of(jb * B, B)
                k = k_ref[pl.ds(off, B), :]
                z = _dot(q, k, NT) * SCALE
                _, sp = _softplus_parts(z)
                lg = -sp
                valid = None
                if masked:
                    r, c = _positions(i, jb, B, B)
                    valid = c < r
                    lg = jnp.where(valid, lg, 0.0)
                hi, lo = _split2(lg)
                cum = _dot(hi, suffix) + _dot(lo, suffix)
                parts.append((off, z, cum, jnp.sum(lg, axis=1, keepdims=True), valid))
            for off, z, cum, rs, valid in parts:
                a = jnp.exp(z + cum + run)
                if masked:
                    a = jnp.where(valid, a, 0.0)
                acc = acc + _dot(a.astype(BF16), v_ref[pl.ds(off, B), :])
                run = run + rs
            return run, acc

        nfull = (i * B) // bk
        carry = step(nfull, (jnp.zeros((B, 1), F32), jnp.zeros((B, HEAD_DIM), F32)), True)
        _, acc = lax.fori_loop(0, nfull, lambda jj, cr: step(nfull - 1 - jj, cr, False), carry)
        o_ref[...] = acc.astype(o_ref.dtype)

    return _call_with_carry(
        kern, name=name, grid=(H, nq), carry=carry,
        in_specs=[pl.BlockSpec((B, HEAD_DIM), lambda h, i: (i, 12 + h)),
                  pl.BlockSpec((T, HEAD_DIM), lambda h, i: (0, 16 + h)),
                  pl.BlockSpec((T, HEAD_DIM), lambda h, i: (0, 20 + h))],
        out_specs=[pl.BlockSpec((B, HEAD_DIM), lambda h, i: (i, h))],
        out_shape=[jax.ShapeDtypeStruct((T, BRANCH_WIDTH), BF16)],
        scratch_shapes=[], args=(u_att, u_att, u_att))


def _sb_bwd(u_att, do, *, name, carry=None):
    T = u_att.shape[0]
    B, bk = _att_blocks(T)
    nq, nsub = T // B, bk // B
    H = N_HEADS

    def kern(q_ref, k_ref, v_ref, do_ref, dq_ref, dk_ref, dv_ref, dk_acc, dv_acc, de_s, sg_s):
        i = pl.program_id(1)

        @pl.when(i == 0)
        def _():
            dk_acc[...] = jnp.zeros_like(dk_acc)
            dv_acc[...] = jnp.zeros_like(dv_acc)

        q = q_ref[...]
        dov = do_ref[...]
        suffix, prefix = _sb_tri(B)

        def sweep1(j, run, masked):
            parts = []
            for s in reversed(range(nsub)):
                jb = j * nsub + s
                off = pl.multiple_of(jb * B, B)
                k = k_ref[pl.ds(off, B), :]
                z = _dot(q, k, NT) * SCALE
                t, sp = _softplus_parts(z)
                lg = -sp
                sg = jnp.exp(z + lg)
                valid = None
                if masked:
                    r, c = _positions(i, jb, B, B)
                    valid = c < r
                    lg = jnp.where(valid, lg, 0.0)
                    sg = jnp.where(valid, sg, 0.0)
                sg_s[jb] = sg.astype(sg_s.dtype)
                hi, lo = _split2(lg)
                cum = _dot(hi, suffix) + _dot(lo, suffix)
                da = _dot(dov, v_ref[pl.ds(off, B), :], NT)
                parts.append((jb, off, z, cum, da, jnp.sum(lg, axis=1, keepdims=True), valid))
            for jb, off, z, cum, da, rs, valid in parts:
                a = jnp.exp(z + cum + run)
                if masked:
                    a = jnp.where(valid, a, 0.0)
                de_s[jb] = a * da
                dv_acc[pl.ds(off, B), :] += _dot(a.astype(BF16), dov, TN)
                run = run + rs
            return run

        nfull = (i * B) // bk
        run = sweep1(nfull, jnp.zeros((B, 1), F32), True)
        lax.fori_loop(0, nfull, lambda jj, cr: sweep1(nfull - 1 - jj, cr, False), run)

        def sweep2(j, carry):
            pre, dq = carry
            parts = []
            for s in range(nsub):
                jb = j * nsub + s
                de = de_s[jb]
                parts.append((jb, de, _dot(de.astype(BF16), prefix), jnp.sum(de, axis=1, keepdims=True)))
            for jb, de, g, rs in parts:
                off = pl.multiple_of(jb * B, B)
                dz = (de - sg_s[jb].astype(F32) * (g + pre)).astype(BF16)
                dq = dq + _dot(dz, k_ref[pl.ds(off, B), :])
                dk_acc[pl.ds(off, B), :] += _dot(dz, q, TN)
                pre = pre + rs
            return pre, dq

        _, dq = lax.fori_loop(0, nfull + 1, sweep2, (jnp.zeros((B, 1), F32), jnp.zeros((B, HEAD_DIM), F32)))
        dq_ref[...] = (dq * SCALE).astype(dq_ref.dtype)

        @pl.when(i == nq - 1)
        def _():
            dk_ref[...] = (dk_acc[...] * SCALE).astype(dk_ref.dtype)
            dv_ref[...] = dv_acc[...].astype(dv_ref.dtype)

    return _call_with_carry(
        kern, name=name, grid=(H, nq), carry=carry,
        in_specs=[pl.BlockSpec((B, HEAD_DIM), lambda h, i: (i, 12 + h)),
                  pl.BlockSpec((T, HEAD_DIM), lambda h, i: (0, 16 + h)),
                  pl.BlockSpec((T, HEAD_DIM), lambda h, i: (0, 20 + h)),
                  pl.BlockSpec((B, HEAD_DIM), lambda h, i: (i, h))],
        out_specs=[pl.BlockSpec((B, HEAD_DIM), lambda h, i: (i, h)),
                   pl.BlockSpec((T, HEAD_DIM), lambda h, i: (0, h)),
                   pl.BlockSpec((T, HEAD_DIM), lambda h, i: (0, h))],
        out_shape=[jax.ShapeDtypeStruct((T, BRANCH_WIDTH), BF16)] * 3,
        scratch_shapes=[pltpu.VMEM((T, HEAD_DIM), F32), pltpu.VMEM((T, HEAD_DIM), F32),
                        pltpu.VMEM((T // B, B, B), F32), pltpu.VMEM((T // B, B, B), BF16)],
        args=(u_att, u_att, u_att, do))


def _rel_onehot(qrow):
    k = lax.broadcasted_iota(jnp.int32, (BAND, REL_PAD), 0)
    rr = lax.broadcasted_iota(jnp.int32, (BAND, REL_PAD), 1)
    idx = jnp.clip(PAD_ROWS + qrow - k, -(CHUNK - 1), REL_CLIP) + (CHUNK - 1)
    return jnp.where(idx == rr, 1.0, 0.0).astype(BF16)


def _band_bias(table, *, name):
    def kern(t_ref, o_ref):
        hi, mid, lo = _split3(t_ref[...])

        def body(qrow, _):
            oh = _rel_onehot(qrow)
            o_ref[qrow] = _dot(hi, oh, NT) + _dot(mid, oh, NT) + _dot(lo, oh, NT)
            return 0

        lax.fori_loop(0, CHUNK, body, 0)

    return pl.pallas_call(
        kern, name=name, out_shape=jax.ShapeDtypeStruct((CHUNK, 8, BAND), F32),
        compiler_params=_cp(),
    )(table)


def _band_bias_bwd(dbias, *, name):
    def kern(d_ref, o_ref):
        def body(qrow, acc):
            oh = _rel_onehot(qrow)
            hi, mid, lo = _split3(d_ref[qrow])
            return acc + _dot(hi, oh) + _dot(mid, oh) + _dot(lo, oh)

        o_ref[...] = lax.fori_loop(0, CHUNK, body, jnp.zeros((8, REL_PAD), F32))

    return pl.pallas_call(
        kern, name=name, out_shape=jax.ShapeDtypeStruct((8, REL_PAD), F32),
        compiler_params=_cp(),
    )(dbias)


def _chunk_rows(T):
    return _pick(T, (512, 256, 128, 64))


def _chunk_scores(q, kw, bias, c_global):
    s = _dot(q, kw, NT) * SCALE + bias
    col = lax.broadcasted_iota(jnp.int32, (CHUNK, BAND), 1)
    valid = (c_global * CHUNK + col) >= PAD_ROWS
    s = jnp.where(valid, s, -jnp.inf)
    m = jnp.max(s, axis=1, keepdims=True)
    e = jnp.exp(s - m)
    return e / jnp.sum(e, axis=1, keepdims=True)


def _chunk_fwd(u_att, bias, *, name, carry=None):
    T = u_att.shape[0]
    R = _chunk_rows(T)
    nr = T // R
    H = N_HEADS

    def kern(q_ref, k_ref, v_ref, b_ref, o_ref, kpad, vpad):
        i = pl.program_id(1)

        @pl.when(i == 0)
        def _():
            kpad[0:PAD_ROWS, :] = jnp.zeros((PAD_ROWS, HEAD_DIM), BF16)
            vpad[0:PAD_ROWS, :] = jnp.zeros((PAD_ROWS, HEAD_DIM), BF16)
            kpad[PAD_ROWS:, :] = k_ref[...]
            vpad[PAD_ROWS:, :] = v_ref[...]

        bias_v = b_ref[...]
        for cc in range(R // CHUNK):
            cg = i * (R // CHUNK) + cc
            off = pl.multiple_of(cg * CHUNK, CHUNK)
            q = q_ref[cc * CHUNK:(cc + 1) * CHUNK, :]
            kw = kpad[pl.ds(off, BAND), :]
            vw = vpad[pl.ds(off, BAND), :]
            p = _chunk_scores(q, kw, bias_v, cg)
            o_ref[cc * CHUNK:(cc + 1) * CHUNK, :] = _dot(p.astype(BF16), vw).astype(o_ref.dtype)

    return _call_with_carry(
        kern, name=name, grid=(H, nr), carry=carry,
        in_specs=[pl.BlockSpec((R, HEAD_DIM), lambda h, i: (i, 24 + h)),
                  pl.BlockSpec((T, HEAD_DIM), lambda h, i: (0, 28 + h)),
                  pl.BlockSpec((T, HEAD_DIM), lambda h, i: (0, 32 + h)),
                  pl.BlockSpec((None, CHUNK, BAND), lambda h, i: (h, 0, 0))],
        out_specs=[pl.BlockSpec((R, HEAD_DIM), lambda h, i: (i, h))],
        out_shape=[jax.ShapeDtypeStruct((T, BRANCH_WIDTH), BF16)],
        scratch_shapes=[pltpu.VMEM((T + PAD_ROWS, HEAD_DIM), BF16), pltpu.VMEM((T + PAD_ROWS, HEAD_DIM), BF16)],
        args=(u_att, u_att, u_att, bias))


def _chunk_bwd(u_att, bias, do, *, name, carry=None):
    T = u_att.shape[0]
    R = _chunk_rows(T)
    nr = T // R
    H = N_HEADS

    def kern(q_ref, k_ref, v_ref, b_ref, do_ref, dq_ref, dk_ref, dv_ref, db_ref, kpad, vpad, dkp, dvp):
        i = pl.program_id(1)

        @pl.when(i == 0)
        def _():
            kpad[0:PAD_ROWS, :] = jnp.zeros((PAD_ROWS, HEAD_DIM), BF16)
            vpad[0:PAD_ROWS, :] = jnp.zeros((PAD_ROWS, HEAD_DIM), BF16)
            kpad[PAD_ROWS:, :] = k_ref[...]
            vpad[PAD_ROWS:, :] = v_ref[...]
            dkp[...] = jnp.zeros_like(dkp)
            dvp[...] = jnp.zeros_like(dvp)
            db_ref[...] = jnp.zeros_like(db_ref)

        bias_v = b_ref[...]
        for cc in range(R // CHUNK):
            cg = i * (R // CHUNK) + cc
            off = pl.multiple_of(cg * CHUNK, CHUNK)
            q = q_ref[cc * CHUNK:(cc + 1) * CHUNK, :]
            dov = do_ref[cc * CHUNK:(cc + 1) * CHUNK, :]
            kw = kpad[pl.ds(off, BAND), :]
            vw = vpad[pl.ds(off, BAND), :]
            p = _chunk_scores(q, kw, bias_v, cg)
            dp = _dot(dov, vw, NT)
            ds = p * (dp - jnp.sum(p * dp, axis=1, keepdims=True))
            dsb = ds.astype(BF16)
            dq_ref[cc * CHUNK:(cc + 1) * CHUNK, :] = (_dot(dsb, kw) * SCALE).astype(dq_ref.dtype)
            dkp[pl.ds(off, BAND), :] += _dot(dsb, q, TN)
            dvp[pl.ds(off, BAND), :] += _dot(p.astype(BF16), dov, TN)
            db_ref[...] += ds

        @pl.when(i == nr - 1)
        def _():
            dk_ref[...] = (dkp[PAD_ROWS:, :] * SCALE).astype(dk_ref.dtype)
            dv_ref[...] = dvp[PAD_ROWS:, :].astype(dv_ref.dtype)

    return _call_with_carry(
        kern, name=name, grid=(H, nr), carry=carry,
        in_specs=[pl.BlockSpec((R, HEAD_DIM), lambda h, i: (i, 24 + h)),
                  pl.BlockSpec((T, HEAD_DIM), lambda h, i: (0, 28 + h)),
                  pl.BlockSpec((T, HEAD_DIM), lambda h, i: (0, 32 + h)),
                  pl.BlockSpec((None, CHUNK, BAND), lambda h, i: (h, 0, 0)),
                  pl.BlockSpec((R, HEAD_DIM), lambda h, i: (i, h))],
        out_specs=[pl.BlockSpec((R, HEAD_DIM), lambda h, i: (i, h)),
                   pl.BlockSpec((T, HEAD_DIM), lambda h, i: (0, h)),
                   pl.BlockSpec((T, HEAD_DIM), lambda h, i: (0, h)),
                   pl.BlockSpec((None, CHUNK, BAND), lambda h, i: (h, 0, 0))],
        out_shape=[jax.ShapeDtypeStruct((T, BRANCH_WIDTH), BF16)] * 3
                  + [jax.ShapeDtypeStruct((H, CHUNK, BAND), F32)],
        scratch_shapes=[pltpu.VMEM((T + PAD_ROWS, HEAD_DIM), BF16), pltpu.VMEM((T + PAD_ROWS, HEAD_DIM), BF16),
                        pltpu.VMEM((T + PAD_ROWS, HEAD_DIM), F32), pltpu.VMEM((T + PAD_ROWS, HEAD_DIM), F32)],
        args=(u_att, u_att, u_att, bias, do))


LRU_ROWS = 256
HALO = 8


def _gelu(y):
    k0 = math.sqrt(2.0 / math.pi)
    t = jnp.tanh(k0 * (y + 0.044715 * y * y * y))
    return 0.5 * y * (1.0 + t), t


def _gelu_grad(y, t):
    k0 = math.sqrt(2.0 / math.pi)
    return 0.5 * (1.0 + t) + 0.5 * y * (1.0 - t * t) * k0 * (1.0 + 3.0 * 0.044715 * y * y)


def _neg_expm1(y):
    poly = -y * (1.0 + y * (1.0 / 2 + y * (1.0 / 6 + y * (1.0 / 24 + y * (1.0 / 120 + y * (1.0 / 720 + y * (1.0 / 5040)))))))
    return jnp.where(y > -0.5, poly, 1.0 - jnp.exp(y))


def _lru_gates(ext, cw_ref, cb_ref, wr_ref, br_ref, wi_ref, bi_ref, lam_ref, rows):
    xc = cb_ref[...] + jnp.zeros((rows, BRANCH_WIDTH), F32)
    for j in range(CONV_WIDTH):
        xc = xc + ext[pl.ds(HALO - (CONV_WIDTH - 1) + j, rows), :] * cw_ref[j:j + 1, :]
    xcb = xc.astype(BF16)
    zr = jnp.concatenate([_dot(xcb[:, n * 128:(n + 1) * 128], wr_ref[n]) for n in range(4)], axis=1) + br_ref[...]
    zi = jnp.concatenate([_dot(xcb[:, n * 128:(n + 1) * 128], wi_ref[n]) for n in range(4)], axis=1) + bi_ref[...]
    r = _sigmoid(zr)
    gi = _sigmoid(zi)
    ls = _log_sigmoid(lam_ref[...])
    la = LRU_C * r * ls
    a = jnp.exp(la)
    mult = jnp.sqrt(_neg_expm1(2.0 * la))
    return xc, xcb, r, gi, ls, a, mult


def _lru_param_specs():
    full2 = lambda s: pl.BlockSpec(s, lambda i: (0, 0))
    full3 = lambda s: pl.BlockSpec(s, lambda i: (0, 0, 0))
    return [full2((8, BRANCH_WIDTH)), full2((1, BRANCH_WIDTH)), full3((4, 128, 128)), full2((1, BRANCH_WIDTH)),
            full3((4, 128, 128)), full2((1, BRANCH_WIDTH)), full2((1, BRANCH_WIDTH))]


def _lru_fwd(u_rec, p, *, name):
    T = u_rec.shape[0]
    R = min(LRU_ROWS, T)
    nb = T // R
    W = BRANCH_WIDTH
    hb = R // HALO

    def kern(rx_ref, halo_ref, ry_ref, cw_ref, cb_ref, wr_ref, br_ref, wi_ref, bi_ref, lam_ref,
             o_ref, h_ref, ext, a_s, b_s, hc):
        i = pl.program_id(0)

        @pl.when(i == 0)
        def _():
            hc[...] = jnp.zeros_like(hc)

        ext[0:HALO, :] = jnp.where(i == 0, 0.0, halo_ref[...])
        ext[HALO:, :] = rx_ref[...]
        xc, _, r, gi, ls, a, mult = _lru_gates(ext, cw_ref, cb_ref, wr_ref, br_ref, wi_ref, bi_ref, lam_ref, R)
        a_s[...] = a
        b_s[...] = mult * (gi * xc)

        def body(t, h):
            h = a_s[pl.ds(t, 1), :] * h + b_s[pl.ds(t, 1), :]
            h_ref[pl.ds(t, 1), :] = h
            return h

        h = lax.fori_loop(0, R, body, hc[0:1, :], unroll=8)
        hc[...] = jnp.broadcast_to(h, hc.shape)
        g, _ = _gelu(ry_ref[...])
        o_ref[...] = (h_ref[...] * g).astype(o_ref.dtype)

    return pl.pallas_call(
        kern, name=name, grid=(nb,),
        in_specs=[pl.BlockSpec((R, W), lambda i: (i, 0)),
                  pl.BlockSpec((HALO, W), lambda i: (jnp.maximum(i * hb - 1, 0), 0)),
                  pl.BlockSpec((R, W), lambda i: (i, 1))] + _lru_param_specs(),
        out_specs=[pl.BlockSpec((R, W), lambda i: (i, 0)), pl.BlockSpec((R, W), lambda i: (i, 0))],
        out_shape=[jax.ShapeDtypeStruct((T, W), BF16), jax.ShapeDtypeStruct((T, W), F32)],
        scratch_shapes=[pltpu.VMEM((R + HALO, W), F32), pltpu.VMEM((R, W), F32), pltpu.VMEM((R, W), F32),
                        pltpu.VMEM((8, W), F32)],
        compiler_params=_cp(("arbitrary",)),
    )(u_rec, u_rec, u_rec, *p)


def _lru_bwd(u_rec, hs, do, p, *, name):
    T = u_rec.shape[0]
    R = min(LRU_ROWS, T)
    nb = T // R
    W = BRANCH_WIDTH
    hb = R // HALO

    def kern(rx_ref, halo_ref, ry_ref, h_ref, hh_ref, do_ref, cw_ref, cb_ref, wr_ref, br_ref, wi_ref, bi_ref, lam_ref,
             drx_ref, dry_ref, dcw_ref, dcb_ref, dwr_ref, dbr_ref, dwi_ref, dbi_ref, dlam_ref,
             ext, hext, a_s, g_s, dext, gc):
        s = pl.program_id(0)
        first_block = s == nb - 1

        @pl.when(s == 0)
        def _():
            gc[...] = jnp.zeros_like(gc)
            dext[R:, :] = jnp.zeros((HALO, W), F32)
            for ref in (dcw_ref, dcb_ref, dwr_ref, dbr_ref, dwi_ref, dbi_ref, dlam_ref):
                ref[...] = jnp.zeros_like(ref)

        ext[0:HALO, :] = jnp.where(first_block, 0.0, halo_ref[...])
        ext[HALO:, :] = rx_ref[...]
        hext[0:HALO, :] = jnp.where(first_block, 0.0, hh_ref[...])
        hext[HALO:, :] = h_ref[...]
        xc, xcb, r, gi, ls, a, mult = _lru_gates(ext, cw_ref, cb_ref, wr_ref, br_ref, wi_ref, bi_ref, lam_ref, R)
        ry = ry_ref[...]
        gel, th = _gelu(ry)
        dov = do_ref[...].astype(F32)
        dry_ref[...] = (dov * h_ref[...] * _gelu_grad(ry, th)).astype(dry_ref.dtype)
        a_s[...] = a
        g_s[...] = dov * gel

        def body(tt, g):
            t = R - 1 - tt
            dh = g_s[pl.ds(t, 1), :] + g
            g_s[pl.ds(t, 1), :] = dh
            return a_s[pl.ds(t, 1), :] * dh

        g = lax.fori_loop(0, R, body, gc[0:1, :], unroll=8)
        gc[...] = jnp.broadcast_to(g, gc.shape)
        dh = g_s[...]
        hprev = hext[pl.ds(HALO - 1, R), :]
        da = dh * hprev
        gx = gi * xc
        dmult = dh * gx
        dgx = dh * mult
        dgi = dgx * xc
        dxc = dgx * gi
        dla = da * a - dmult * (a * a) / mult
        dr = dla * (LRU_C * ls)
        dlam_ref[...] += jnp.sum(dla * (LRU_C * r), axis=0, keepdims=True)
        dzr = dr * r * (1.0 - r)
        dzi = dgi * gi * (1.0 - gi)
        dbr_ref[...] += jnp.sum(dzr, axis=0, keepdims=True)
        dbi_ref[...] += jnp.sum(dzi, axis=0, keepdims=True)
        dzrb = dzr.astype(BF16)
        dzib = dzi.astype(BF16)
        back = []
        for n in range(4):
            sl = slice(n * 128, (n + 1) * 128)
            dwr_ref[n] += _dot(xcb[:, sl], dzrb[:, sl], TN)
            dwi_ref[n] += _dot(xcb[:, sl], dzib[:, sl], TN)
            back.append(_dot(dzrb[:, sl], wr_ref[n], NT) + _dot(dzib[:, sl], wi_ref[n], NT))
        dxc = dxc + jnp.concatenate(back, axis=1)
        dcb_ref[...] += jnp.sum(dxc, axis=0, keepdims=True)
        for j in range(CONV_WIDTH):
            dcw_ref[j:j + 1, :] += jnp.sum(dxc * ext[pl.ds(HALO - (CONV_WIDTH - 1) + j, R), :], axis=0, keepdims=True)
        dext[0:R, :] = dxc
        drx = jnp.zeros((R, W), F32)
        for j in range(CONV_WIDTH):
            drx = drx + dext[pl.ds(CONV_WIDTH - 1 - j, R), :] * cw_ref[j:j + 1, :]
        drx_ref[...] = drx.astype(drx_ref.dtype)
        dext[R:, :] = dxc[0:HALO, :]

        @pl.when(s == nb - 1)
        def _():
            dlam_ref[...] = dlam_ref[...] * _sigmoid(-lam_ref[...])

    rev = lambda c: pl.BlockSpec((R, W), lambda s: (nb - 1 - s, c))
    halo = lambda: pl.BlockSpec((HALO, W), lambda s: (jnp.maximum((nb - 1 - s) * hb - 1, 0), 0))
    v2 = lambda shp: pl.BlockSpec(shp, lambda s: (0, 0))
    v3 = lambda shp: pl.BlockSpec(shp, lambda s: (0, 0, 0))
    return pl.pallas_call(
        kern, name=name, grid=(nb,),
        in_specs=[rev(0), halo(), rev(1), rev(0), halo(), rev(0)] + _lru_param_specs(),
        out_specs=[rev(0), rev(0), v2((8, W)), v2((1, W)), v3((4, 128, 128)), v2((1, W)), v3((4, 128, 128)),
                   v2((1, W)), v2((1, W))],
        out_shape=[jax.ShapeDtypeStruct((T, W), BF16), jax.ShapeDtypeStruct((T, W), BF16),
                   jax.ShapeDtypeStruct((8, W), F32), jax.ShapeDtypeStruct((1, W), F32),
                   jax.ShapeDtypeStruct((4, 128, 128), F32), jax.ShapeDtypeStruct((1, W), F32),
                   jax.ShapeDtypeStruct((4, 128, 128), F32), jax.ShapeDtypeStruct((1, W), F32),
                   jax.ShapeDtypeStruct((1, W), F32)],
        scratch_shapes=[pltpu.VMEM((R + HALO, W), F32), pltpu.VMEM((R + HALO, W), F32), pltpu.VMEM((R, W), F32),
                        pltpu.VMEM((R, W), F32), pltpu.VMEM((R + HALO, W), F32), pltpu.VMEM((8, W), F32)],
        compiler_params=_cp(("arbitrary",)),
    )(u_rec, u_rec, u_rec, hs, hs, do, *p)


def _merge_fwd(o_all, wb, gate, *, name):
    T = o_all.shape[1]
    D = D_MODEL
    bm = _pick(T, (1024, 512, 256, 128))
    bn = 1024
    nj = D // bn

    def kern(o_ref, w_ref, g_ref, m_ref, pb_ref, acc):
        g = pl.program_id(2)
        pbv = _dot(o_ref[...], w_ref[...])
        pb_ref[...] = pbv.astype(pb_ref.dtype)
        term = g_ref[...].astype(F32) * pbv

        @pl.when(g == 0)
        def _():
            acc[...] = term

        @pl.when(g > 0)
        def _():
            acc[...] += term

        @pl.when(g == N_BRANCH - 1)
        def _():
            m_ref[...] = acc[...].astype(m_ref.dtype)

    return pl.pallas_call(
        kern, name=name, grid=(T // bm, nj, N_BRANCH),
        in_specs=[pl.BlockSpec((None, bm, BRANCH_WIDTH), lambda i, j, g: (g, i, 0)),
                  pl.BlockSpec((None, BRANCH_WIDTH, bn), lambda i, j, g: (g, 0, j)),
                  pl.BlockSpec((bm, bn), lambda i, j, g: (i, g * nj + j))],
        out_specs=[pl.BlockSpec((bm, bn), lambda i, j, g: (i, j)),
                   pl.BlockSpec((bm, bn), lambda i, j, g: (i, g * nj + j))],
        out_shape=[jax.ShapeDtypeStruct((T, D), BF16), jax.ShapeDtypeStruct((T, N_BRANCH * D), BF16)],
        scratch_shapes=[pltpu.VMEM((bm, bn), F32)],
        compiler_params=_cp(("parallel", "parallel", "arbitrary")),
    )(o_all, wb, gate)


def _merge_bwd(dm, gate, pb, *, name):
    T = dm.shape[0]
    D = D_MODEL
    bt = _pick(T, (256, 128))

    def kern(dm_ref, g_ref, pb_ref, dpb_ref, dzg_ref, dbg_ref):
        i = pl.program_id(1)
        dmv = dm_ref[...]
        gv = g_ref[...].astype(F32)
        dpb_ref[...] = (dmv * gv).astype(dpb_ref.dtype)
        dzg = dmv * pb_ref[...].astype(F32) * gv * (1.0 - gv)
        dzg_ref[...] = dzg.astype(dzg_ref.dtype)
        part = jnp.sum(dzg, axis=0, keepdims=True)

        @pl.when(i == 0)
        def _():
            dbg_ref[...] = part

        @pl.when(i > 0)
        def _():
            dbg_ref[...] += part

    return pl.pallas_call(
        kern, name=name, grid=(N_BRANCH, T // bt),
        in_specs=[pl.BlockSpec((bt, D), lambda g, i: (i, 0)),
                  pl.BlockSpec((bt, D), lambda g, i: (i, g)),
                  pl.BlockSpec((bt, D), lambda g, i: (i, g))],
        out_specs=[pl.BlockSpec((None, bt, D), lambda g, i: (g, i, 0)),
                   pl.BlockSpec((bt, D), lambda g, i: (i, g)),
                   pl.BlockSpec((1, D), lambda g, i: (0, g))],
        out_shape=[jax.ShapeDtypeStruct((N_BRANCH, T, D), BF16), jax.ShapeDtypeStruct((T, N_BRANCH * D), BF16),
                   jax.ShapeDtypeStruct((1, N_BRANCH * D), F32)],
        compiler_params=_cp(("parallel", "arbitrary")),
    )(dm, gate, pb)


def _col_split(M, N, bm, bn):
    per = N // N_CHIPS // bn
    return (N_CHIPS, M, N // N_CHIPS), (None, bm, bn), lambda i, j: (j // per, i, j % per)


def _pad_lanes(v, n):
    return jnp.pad(v, [(0, 0)] * (v.ndim - 1) + [(0, n - v.shape[-1])])


def _rows8(v):
    return jnp.pad(v, ((0, 8 - v.shape[0]), (0, 0)))


def _device_step(x, tgt, W, hooks=None):
    hooks = hooks or {}

    def carried(fn, key, *args, **kw):
        hook = hooks.get(key)
        outs, extra = fn(*args, name=key[0], carry=hook.spec(W, G) if hook else None, **kw)
        if hook:
            hook.done(extra, W, G)
        return outs

    def mm_carried(key, *args, **kw):
        hook = hooks.get(key)
        if hook is None:
            return _mm(*args, name=key[0], **kw)
        res, extra = _mm(*args, name=key[0], carry=hook.spec(W, G), **kw)
        hook.done(extra, W, G)
        return res

    T = x.shape[0]
    _, bk = _att_blocks(T)
    H = N_HEADS
    G = {}
    saved = []

    xf, xb = _ln_fwd(x, W['ln_in_g'], W['ln_in_b'], name='ln_in_fwd')
    for l in range(DEPTH):
        w_att, w_rec = W['w_att'][l], W['w_rec'][l]
        u_att = mm_carried(('in_proj_att', l), xb, w_att, out_dtypes=(BF16,))
        u_rec = _mm(xb, w_rec, name='in_proj_rec', out_dtypes=(F32,))
        ffl = u_rec[:, 2 * BRANCH_WIDTH:]
        bf = _pad_lanes(W['b_forget'][l].reshape(1, H), LANES)
        Fc = _forget_fwd(ffl, bf, name='forget_fwd')
        Fh = Fc[:, :H].T
        frow = Fh.reshape(H, T // bk, 1, bk)
        o_fox, lse = carried(_fox_fwd, ('fox_fwd', l), u_att, Fc, frow)
        lp = (_rows8(W['conv_w'][l]), W['conv_b'][l].reshape(1, -1), W['w_r'][l].astype(BF16),
              W['b_r'][l].reshape(1, -1), W['w_i'][l].astype(BF16), W['b_i'][l].reshape(1, -1),
              W['lru_lambda'][l].reshape(1, -1))
        o_lru, hs = _lru_fwd(u_rec, lp, name='lru_fwd')
        o_sb, = carried(_sb_fwd, ('sb_fwd', l), u_att)
        table = _rows8(_pad_lanes(W['rel_bias'][l], REL_PAD))
        bias = _band_bias(table, name='band_bias').transpose(1, 0, 2)[:H]
        o_ch, = carried(_chunk_fwd, ('chunk_fwd', l), u_att, bias)
        o_all = jnp.stack([o_fox, o_lru, o_sb, o_ch])
        gate = mm_carried(('gate_proj', l), xb, W['w_gate_cat'][l], out_dtypes=(BF16,),
                          extras=[(W['b_gate'][l].reshape(1, -1), 'n')],
                          epilogue=lambda acc, b: (_sigmoid(acc + b),))
        merged, pb = _merge_fwd(o_all, W['w_branch'][l], gate, name='merge_fwd')
        h1 = _mm(merged, W['w_out'][l], name='out_proj', extras=[(xf, 'mn')],
                 epilogue=lambda acc, xr: (ALPHA * xr + acc,))
        xmf, xmb = _ln_fwd(h1, W['ln1_g'][l], W['ln1_b'][l], name='ln_fwd')
        hid, ra = _mm(xmb, W['w_ff1'], name='ff1', out_dtypes=(BF16, BF16), bn=1024, bk=FF_SHARD,
                      b_view=(D_MODEL, D_FF, (None, None, FF_SHARD, 1024), lambda i, j, k: (j // 2, l, 0, j % 2)),
                      epilogue=lambda acc: (jnp.square(jnp.maximum(acc, 0.0)), jnp.maximum(acc, 0.0)))
        h2 = _mm(hid, W['w_ff2'], name='ff2', extras=[(xmf, 'mn')], bn=1024, bk=FF_SHARD,
                 b_view=(D_FF, D_MODEL, (None, None, FF_SHARD, 1024), lambda i, j, k: (k, l, 0, j)),
                 epilogue=lambda acc, xr: (ALPHA * xr + acc,))
        saved.append(dict(xb=xb, u_att=u_att, u_rec=u_rec, ffl=ffl, bf=bf, fcum=Fc, frow=frow, lse=lse, lp=lp,
                          hs=hs, bias=bias, o_all=o_all, gate=gate, merged=merged, pb=pb, h1=h1, xmb=xmb,
                          hid=hid, ra=ra, h2=h2))
        xf, xb = _ln_fwd(h2, W['ln2_g'][l], W['ln2_b'][l], name='ln_fwd')

    dx, loss_tile = _loss_head(xf, tgt, name='loss_head')
    loss = loss_tile[0, 0]

    for l in reversed(range(DEPTH)):
        S = saved[l]
        dh2, dh2b, G[('ln2_g', l)], G[('ln2_b', l)] = _ln_bwd(S['h2'], dx, W['ln2_g'][l], name='ln_bwd')
        da = _mm(dh2b, W['w_ff2'], tb=True, name='ff2_dx', out_dtypes=(BF16,), extras=[(S['ra'], 'mn')],
                 bn=1024, bk=FF_SHARD,
                 b_view=(D_MODEL, D_FF, (None, None, 1024, FF_SHARD), lambda i, j, k: (j // 2, l, j % 2, 0)),
                 epilogue=lambda acc, rav: (acc * (2.0 * rav.astype(F32)),))
        G[('w_ff2', l)] = _mm(S['hid'], dh2b, ta=True, name='ff2_dw').reshape(N_CHIPS, D_FF // N_CHIPS, D_MODEL)
        G[('w_ff1', l)] = _mm(S['xmb'], da, ta=True, name='ff1_dw', bm=1024, bn=1024,
                              out_map=_col_split(D_MODEL, D_FF, 1024, 1024))
        dxm = _mm(da, W['w_ff1'], tb=True, name='ff1_dx', extras=[(dh2, 'mn')], bn=1024, bk=FF_SHARD,
                  b_view=(D_FF, D_MODEL, (None, None, 1024, FF_SHARD), lambda i, j, k: (k, l, j, 0)),
                  epilogue=lambda acc, d: (ALPHA * d + acc,))
        dh1, dh1b, G[('ln1_g', l)], G[('ln1_b', l)] = _ln_bwd(S['h1'], dxm, W['ln1_g'][l], name='ln_bwd')
        dm = _mm(dh1b, W['w_out'][l], tb=True, name='out_dx')
        G[('w_out', l)] = _mm(S['merged'], dh1b, ta=True, name='out_dw').reshape(
            N_CHIPS, D_MODEL // N_CHIPS, D_MODEL)
        dpb, dzg, G[('b_gate', l)] = _merge_bwd(dm, S['gate'], S['pb'], name='merge_bwd')
        do = [_mm(dpb[g], W['w_branch'][l][g], tb=True, name='branch_dx', out_dtypes=(BF16,)) for g in range(N_BRANCH)]
        G[('w_branch', l)] = jnp.stack(
            [_mm(S['o_all'][g], dpb[g], ta=True, name='branch_dw', bm=BRANCH_WIDTH, bn=BRANCH_WIDTH,
                 out_map=_col_split(BRANCH_WIDTH, D_MODEL, BRANCH_WIDTH, BRANCH_WIDTH))
             for g in range(N_BRANCH)], axis=1)
        G[('w_gate', l)] = _mm(S['xb'], dzg, ta=True, name='gate_dw', bm=1024, bn=1024,
                               out_map=((N_CHIPS, N_BRANCH, D_MODEL // N_CHIPS, D_MODEL),
                                        (2, None, D_MODEL // N_CHIPS, 1024),
                                        lambda i, j: (i, j // 2, 0, j % 2)))
        u_att, u_rec = S['u_att'], S['u_rec']
        delta = _row_dot(do[0], S['o_all'][0], name='row_dot')
        fdq, fdk, fdv, dfk, dfq = carried(_fox_bwd, ('fox_bwd', l), u_att, do[0], S['lse'], delta, S['fcum'],
                                          S['frow'])
        dff, dbf = _forget_bwd(_pad_lanes(dfk.reshape(H, T).T, LANES), _pad_lanes(dfq.reshape(H, T).T, LANES),
                               S['ffl'], S['bf'], name='forget_bwd')
        G[('b_forget', l)] = dbf[0, :H]
        (drx, dry, dcw, dcb, G[('w_r', l)], dbr, G[('w_i', l)], dbi, dlam) = _lru_bwd(
            u_rec, S['hs'], do[1], S['lp'], name='lru_bwd')
        G[('conv_w', l)], G[('conv_b', l)] = dcw[:CONV_WIDTH], dcb[0]
        G[('b_r', l)], G[('b_i', l)], G[('lru_lambda', l)] = dbr[0], dbi[0], dlam[0]
        sdq, sdk, sdv = carried(_sb_bwd, ('sb_bwd', l), u_att, do[2])
        cdq, cdk, cdv, dbias = carried(_chunk_bwd, ('chunk_bwd', l), u_att, S['bias'], do[3])
        dtab = _band_bias_bwd(jnp.pad(dbias, ((0, 8 - H), (0, 0), (0, 0))).transpose(1, 0, 2), name='band_bias_bwd')
        G[('rel_bias', l)] = dtab[:H, :REL_TABLE]
        du_att = jnp.concatenate([fdq, fdk, fdv, sdq, sdk, sdv, cdq, cdk, cdv], axis=1)
        du_rec = jnp.concatenate([drx, dry, dff], axis=1)
        G[('w_att', l)] = _mm(S['xb'], du_att, ta=True, name='in_att_dw')
        G[('w_rec', l)] = _mm(S['xb'], du_rec, ta=True, name='in_rec_dw')
        t1 = mm_carried(('gate_dx', l), dzg, W['w_gate_cat'][l], tb=True, extras=[(dh1, 'mn')],
                        epilogue=lambda acc, d: (ALPHA * d + acc,))
        t2 = mm_carried(('in_att_dx', l), du_att, W['w_att'][l], tb=True, extras=[(t1, 'mn')],
                        epilogue=lambda acc, d: (d + acc,))
        dx = _mm(du_rec, W['w_rec'][l], tb=True, name='in_rec_dx', extras=[(t2, 'mn')],
                 epilogue=lambda acc, d: (d + acc,))

    gx, _, G[('ln_in_g', -1)], G[('ln_in_b', -1)] = _ln_bwd(x, dx, W['ln_in_g'], name='ln_in_bwd')
    return loss, gx, G


_IN_FQKV = (0, 1536)
_IN_FF = (1536, 1540)
_IN_REC = (1540, 2564)
_IN_REST = (2564, D_IN)


def _prep_weights(full, W=None):
    W = {} if W is None else W
    for n, a in full.items():
        if n == 'w_in':
            L = a.shape[0]
            W['w_att'] = jnp.concatenate([a[..., _IN_FQKV[0]:_IN_FQKV[1]], a[..., _IN_REST[0]:_IN_REST[1]]],
                                         -1).astype(BF16)
            W['w_rec'] = jnp.concatenate([a[..., _IN_REC[0]:_IN_REC[1]], a[..., _IN_FF[0]:_IN_FF[1]],
                                          jnp.zeros((L, D_MODEL, N_REC - 1024 - N_HEADS), a.dtype)], -1).astype(BF16)
        elif n == 'w_gate':
            W['w_gate_cat'] = a.transpose(0, 2, 1, 3).reshape(a.shape[0], D_MODEL, N_BRANCH * D_MODEL).astype(BF16)
        elif n == 'b_gate':
            W['b_gate'] = a.reshape(a.shape[0], N_BRANCH * D_MODEL)
        elif n == 'w_ff1' and a.ndim == 3:
            W[n] = a.reshape(a.shape[0], D_MODEL, 4, FF_SHARD).transpose(2, 0, 1, 3).astype(BF16)
        elif n == 'w_ff2' and a.ndim == 3:
            W[n] = a.reshape(a.shape[0], 4, FF_SHARD, D_MODEL).transpose(1, 0, 2, 3).astype(BF16)
        elif n in ('w_branch', 'w_out', 'w_ff1', 'w_ff2'):
            W[n] = a.astype(BF16)
        else:
            W[n] = a
    return W


def _grads_to_reference_layout(G):
    out = {'ln_in_g': G[('ln_in_g', -1)][0], 'ln_in_b': G[('ln_in_b', -1)][0]}
    st = lambda n: jnp.stack([G[(n, l)] for l in range(DEPTH)])
    g_att, g_rec = st('w_att'), st('w_rec')
    out['w_in'] = jnp.concatenate([g_att[..., :1536], g_rec[..., 1024:1024 + N_HEADS], g_rec[..., :1024],
                                   g_att[..., 1536:]], -1)
    out['w_gate'] = st('w_gate').transpose(0, 2, 1, 3, 4).reshape(DEPTH, N_BRANCH, D_MODEL, D_MODEL)
    out['w_branch'] = st('w_branch').transpose(0, 2, 3, 1, 4).reshape(DEPTH, N_BRANCH, BRANCH_WIDTH, D_MODEL)
    out['w_ff1'] = st('w_ff1').transpose(0, 2, 1, 3).reshape(DEPTH, D_MODEL, D_FF)
    out['w_ff2'] = st('w_ff2').reshape(DEPTH, D_FF, D_MODEL)
    out['w_out'] = st('w_out').reshape(DEPTH, D_MODEL, D_MODEL)
    out['b_gate'] = st('b_gate').reshape(DEPTH, N_BRANCH, D_MODEL)
    for n in ('ln1_g', 'ln1_b', 'ln2_g', 'ln2_b'):
        out[n] = st(n)[:, 0]
    for n in ('b_forget', 'conv_w', 'conv_b', 'w_r', 'b_r', 'w_i', 'b_i', 'lru_lambda', 'rel_bias'):
        out[n] = st(n)
    return out


HBM_SPEC = pl.BlockSpec(memory_space=pl.ANY)
N_CHIPS = 4
PACK_COLS = 1024


def _place():
    x, y, c = lax.axis_index("x"), lax.axis_index("y"), lax.axis_index("c")
    chips = [(1 - x, y), (x, 1 - y), (1 - x, 1 - y)]
    return x, y, c, chips


def _remote(src, dst, send_sems, recv_sems, k, to):
    return pltpu.make_async_remote_copy(src_ref=src, dst_ref=dst, send_sem=send_sems.at[k], recv_sem=recv_sems.at[k],
                                        device_id=to, device_id_type=MESH)


class _Exchange:
    def __init__(self, ins, out_shapes, n_sems, start, finish, mid=None):
        self.ins, self.out_shapes, self.n_sems = list(ins), list(out_shapes), n_sems
        self.start, self.mid, self.finish = start, mid, finish


def _gather_spec(params):
    n = len(params)

    def start(ins, outs, ss, rs):
        x, y, c, chips = _place()
        for p in range(n):
            _remote(ins[p], outs[p].at[2 * x + y], ss, rs, 6 * n + p, (x, y, 1 - c)).start()
            for j, (cx, cy) in enumerate(chips):
                _remote(ins[p].at[c], outs[p].at[2 * x + y, c], ss, rs, 6 * p + j, (cx, cy, c)).start()

    def mid(ins, outs, ss, rs):
        x, y, c, chips = _place()
        for p in range(n):
            for j, (cx, cy) in enumerate(chips):
                blk = outs[p].at[2 * cx + cy, c]
                _remote(blk, blk, ss, rs, 6 * p + j, (x, y, c)).wait_recv()
                _remote(blk, blk, ss, rs, 6 * p + 3 + j, (x, y, 1 - c)).start()

    def finish(ins, outs, ss, rs):
        x, y, c, chips = _place()
        me = (x, y, c)
        for p in range(n):
            for j, (cx, cy) in enumerate(chips):
                theirs = outs[p].at[2 * cx + cy, 1 - c]
                _remote(theirs, theirs, ss, rs, 6 * p + 3 + j, me).wait_recv()
        for p in range(n):
            for j, (cx, cy) in enumerate(chips):
                _remote(ins[p].at[c], outs[p].at[2 * x + y, c], ss, rs, 6 * p + j, me).wait_send()
                blk = outs[p].at[2 * cx + cy, c]
                _remote(blk, blk, ss, rs, 6 * p + 3 + j, me).wait_send()
            _remote(ins[p], outs[p].at[2 * x + y], ss, rs, 6 * n + p, me).wait()

    shapes = [jax.ShapeDtypeStruct((N_CHIPS,) + a.shape, a.dtype) for a in params]
    return _Exchange(params, shapes, 7 * n, start, finish, mid)


def _pair_spec(g0, g1):
    n = len(g0)

    def start(ins, outs, ss, rs):
        x, y, c, _ = _place()

        @pl.when(c == 0)
        def _():
            for p in range(n):
                _remote(ins[n + p], outs[p], ss, rs, p, (x, y, 1 - c)).start()

        @pl.when(c == 1)
        def _():
            for p in range(n):
                _remote(ins[p], outs[p], ss, rs, p, (x, y, 1 - c)).start()

    def finish(ins, outs, ss, rs):
        x, y, c, _ = _place()
        for p in range(n):
            _remote(ins[p], outs[p], ss, rs, p, (x, y, 1 - c)).wait()

    return _Exchange(list(g0) + list(g1), [jax.ShapeDtypeStruct(a.shape, a.dtype) for a in g0], n, start, finish)


def _chip_spec(s):
    n = len(s)

    def start(ins, outs, ss, rs):
        x, y, c, chips = _place()
        for p in range(n):
            for j, (cx, cy) in enumerate(chips):
                _remote(ins[p].at[2 * cx + cy], outs[p].at[2 * x + y], ss, rs, 3 * p + j, (cx, cy, c)).start()

    def finish(ins, outs, ss, rs):
        x, y, c, chips = _place()
        for p in range(n):
            for j, (cx, cy) in enumerate(chips):
                slot = outs[p].at[2 * cx + cy]
                _remote(slot, slot, ss, rs, 3 * p + j, (x, y, c)).wait_recv()
        for p in range(n):
            for j, (cx, cy) in enumerate(chips):
                _remote(ins[p].at[2 * cx + cy], outs[p].at[2 * x + y], ss, rs, 3 * p + j, (x, y, c)).wait_send()

    return _Exchange(s, [jax.ShapeDtypeStruct(a.shape, a.dtype) for a in s], 3 * n, start, finish)


def _exchange(ex, *, name):
    ni, no = len(ex.ins), len(ex.out_shapes)

    def body(*refs):
        ins, outs = refs[:ni], refs[ni:ni + no]
        ss, rs = refs[ni + no:]
        ex.start(ins, outs, ss, rs)
        if ex.mid is not None:
            ex.mid(ins, outs, ss, rs)
        ex.finish(ins, outs, ss, rs)

    return list(pl.pallas_call(
        body, name=name, in_specs=[HBM_SPEC] * ni, out_specs=[HBM_SPEC] * no, out_shape=ex.out_shapes,
        scratch_shapes=[pltpu.SemaphoreType.DMA((ex.n_sems,)), pltpu.SemaphoreType.DMA((ex.n_sems,))],
    )(*ex.ins))


def _call_with_carry(kern, *, name, grid, in_specs, out_specs, out_shape, scratch_shapes, args, carry=None,
                     semantics=("parallel", "arbitrary")):
    out_specs, out_shape = list(out_specs), list(out_shape)
    if carry is None:
        res = pl.pallas_call(kern, name=name, grid=grid, in_specs=in_specs, out_specs=out_specs, out_shape=out_shape,
                             scratch_shapes=scratch_shapes, compiler_params=_cp(semantics))(*args)
        return list(res), []
    ni, no, ns = len(in_specs), len(out_specs), len(scratch_shapes)
    ci, co = len(carry.ins), len(carry.out_shapes)

    def wrapped(*refs):
        ins, cins = refs[:ni], refs[ni:ni + ci]
        outs, couts = refs[ni + ci:ni + ci + no], refs[ni + ci + no:ni + ci + no + co]
        scratch = refs[ni + ci + no + co:ni + ci + no + co + ns]
        ss, rs = refs[-2:]
        ids = [pl.program_id(d) for d in range(len(grid))]
        at = lambda pos: functools.reduce(lambda p, q: p & q, [i == v for i, v in zip(ids, pos)])

        @pl.when(at([0] * len(grid)))
        def _():
            carry.start(cins, couts, ss, rs)

        kern(*ins, *outs, *scratch)

        if carry.mid is not None:
            @pl.when(at([grid[0] - 1, grid[1] // 2] + [0] * (len(grid) - 2)))
            def _():
                carry.mid(cins, couts, ss, rs)

        @pl.when(at([g - 1 for g in grid]))
        def _():
            carry.finish(cins, couts, ss, rs)

    res = pl.pallas_call(
        wrapped, name=name, grid=grid, in_specs=list(in_specs) + [HBM_SPEC] * ci,
        out_specs=out_specs + [HBM_SPEC] * co, out_shape=out_shape + carry.out_shapes,
        scratch_shapes=list(scratch_shapes) + [pltpu.SemaphoreType.DMA((carry.n_sems,)),
                                               pltpu.SemaphoreType.DMA((carry.n_sems,))],
        compiler_params=_cp(("arbitrary",) * len(grid)))(*args, *carry.ins)
    return list(res[:no]), list(res[no:])


def _pair_swap(r, *, name):
    n = len(r)

    def body(*refs):
        ins, outs = refs[:n], refs[n:2 * n]
        send_sems, recv_sems = refs[2 * n:]
        x, y, c, _ = _place()
        cps = [_remote(ins[p], outs[p], send_sems, recv_sems, p, (x, y, 1 - c)) for p in range(n)]
        for cp in cps:
            cp.start()
        for cp in cps:
            cp.wait()

    return pl.pallas_call(
        body, name=name, in_specs=[HBM_SPEC] * n, out_specs=[HBM_SPEC] * n,
        out_shape=[jax.ShapeDtypeStruct(a.shape, a.dtype) for a in r],
        scratch_shapes=[pltpu.SemaphoreType.DMA((n,)), pltpu.SemaphoreType.DMA((n,))],
    )(*r)


def _gather8_spec(v):
    R, C = v.shape
    flips = [(bx, by, bc) for bx in (0, 1) for by in (0, 1) for bc in (0, 1)][1:]
    flip = lambda a_, b_: 1 - a_ if b_ else a_

    def start(ins, outs, ss, rs):
        x, y, c, _ = _place()
        mine = outs[0].at[4 * x + 2 * y + c]
        pltpu.make_async_copy(ins[0], mine, ss.at[7]).start()
        for j, (bx, by, bc) in enumerate(flips):
            _remote(ins[0], mine, ss, rs, j, (flip(x, bx), flip(y, by), flip(c, bc))).start()

    def finish(ins, outs, ss, rs):
        x, y, c, _ = _place()
        mine = outs[0].at[4 * x + 2 * y + c]
        for j, (bx, by, bc) in enumerate(flips):
            slot = outs[0].at[4 * flip(x, bx) + 2 * flip(y, by) + flip(c, bc)]
            _remote(slot, slot, ss, rs, j, (x, y, c)).wait_recv()
        for j in range(7):
            _remote(ins[0], mine, ss, rs, j, (x, y, c)).wait_send()
        pltpu.make_async_copy(ins[0], mine, ss.at[7]).wait()

    return _Exchange([v], [jax.ShapeDtypeStruct((8, R, C), v.dtype)], 8, start, finish)


def _row_block(rows, cols, limit=256 * 1024):
    if rows * cols <= limit:
        return rows
    for br in range(limit // cols // 8 * 8, 0, -8):
        if rows % br == 0:
            return br
    return rows


def _sum_slots(buf, *, name):
    n, R, C = buf.shape
    br = _row_block(R, C, limit=64 * 1024)

    def kern(b_ref, o_ref):
        acc = b_ref[0].astype(F32)
        for s in range(1, n):
            acc = acc + b_ref[s].astype(F32)
        o_ref[...] = acc

    return pl.pallas_call(
        kern, name=name, grid=(pl.cdiv(R, br),),
        in_specs=[pl.BlockSpec((n, br, C), lambda i: (0, i, 0))],
        out_specs=pl.BlockSpec((br, C), lambda i: (i, 0)),
        out_shape=jax.ShapeDtypeStruct((R, C), F32),
        compiler_params=_cp(("arbitrary",)),
    )(buf)


def _scalar(s):
    return jnp.reshape(s, (1,)).astype(jnp.int32)


def _sum_pair(g0, g1, other, c, *, name):
    _, R, C = g0.shape
    br = _row_block(R, C)

    def kern(c_ref, g0_ref, g1_ref, o_ref, out_ref):
        own = jnp.where(c_ref[0] == 0, g0_ref[...], g1_ref[...])
        out_ref[...] = (own + o_ref[...]).astype(out_ref.dtype)

    blk = (None, br, C)
    return pl.pallas_call(
        kern, name=name,
        grid_spec=pltpu.PrefetchScalarGridSpec(
            num_scalar_prefetch=1, grid=(N_CHIPS, R // br),
            in_specs=[pl.BlockSpec(blk, lambda k, i, cr: (k, i * (1 - cr[0]), 0)),
                      pl.BlockSpec(blk, lambda k, i, cr: (k, i * cr[0], 0)),
                      pl.BlockSpec(blk, lambda k, i, cr: (k, i, 0))],
            out_specs=pl.BlockSpec(blk, lambda k, i, cr: (k, i, 0))),
        out_shape=jax.ShapeDtypeStruct((N_CHIPS, R, C), BF16),
        compiler_params=_cp(("arbitrary", "arbitrary")),
    )(_scalar(c), g0, g1, other)


def _sum_chips(s, got, k, *, name):
    _, R, C = s.shape
    br = _row_block(R, C)

    def kern(k_ref, s_ref, a_ref, b_ref, c_ref, out_ref):
        out_ref[...] = ((s_ref[...].astype(F32) + a_ref[...].astype(F32)) + b_ref[...].astype(F32)) \
            + c_ref[...].astype(F32)

    blk = (None, br, C)
    peer = lambda d: pl.BlockSpec(blk, lambda i, kr: ((kr[0] + d) % N_CHIPS, i, 0))
    return pl.pallas_call(
        kern, name=name,
        grid_spec=pltpu.PrefetchScalarGridSpec(
            num_scalar_prefetch=1, grid=(R // br,),
            in_specs=[peer(0), peer(1), peer(2), peer(3)],
            out_specs=pl.BlockSpec((br, C), lambda i, kr: (i, 0))),
        out_shape=jax.ShapeDtypeStruct((R, C), F32),
        compiler_params=_cp(("arbitrary",)),
    )(_scalar(k), s, got, got, got)


def _adam_math(w, g, m, v):
    nm = ADAM_B1 * m + (1.0 - ADAM_B1) * g
    nv = ADAM_B2 * v + (1.0 - ADAM_B2) * jnp.square(g)
    m_hat = nm / (1.0 - ADAM_B1 ** ADAM_STEP)
    v_hat = nv / (1.0 - ADAM_B2 ** ADAM_STEP)
    return -ADAM_LR * (m_hat / (jnp.sqrt(v_hat) + ADAM_EPS) + ADAM_WD * w), nm, nv


def _adamw_layers(w, mine, theirs, m, v, c, *, name):
    shape = w.shape
    R, C = mine.shape
    w3, m3, v3 = (a.reshape(DEPTH, R, C) for a in (w, m, v))
    br = _row_block(R, C)

    def kern(c_ref, w_ref, a_ref, b_ref, m_ref, v_ref, g_ref, d_ref, nm_ref, nv_ref):
        g = jnp.where(pl.program_id(0) == c_ref[0], a_ref[...], b_ref[...])
        g_ref[...] = g
        d_ref[...], nm_ref[...], nv_ref[...] = _adam_math(w_ref[...], g, m_ref[...], v_ref[...])

    lay = pl.BlockSpec((None, br, C), lambda l, i, cr: (l, i, 0))
    outs = pl.pallas_call(
        kern, name=name,
        grid_spec=pltpu.PrefetchScalarGridSpec(
            num_scalar_prefetch=1, grid=(DEPTH, R // br),
            in_specs=[lay,
                      pl.BlockSpec((br, C), lambda l, i, cr: (jnp.where(l == cr[0], i, 0), 0)),
                      pl.BlockSpec((br, C), lambda l, i, cr: (jnp.where(l == cr[0], 0, i), 0)),
                      lay, lay],
            out_specs=[lay] * 4),
        out_shape=[jax.ShapeDtypeStruct((DEPTH, R, C), F32)] * 4,
        compiler_params=_cp(("arbitrary", "arbitrary")),
    )(_scalar(c), w3, mine, theirs, m3, v3)
    return [o.reshape(shape) for o in outs]


def _adamw(w, g, m, v, *, name):
    shape = w.shape
    cols = shape[-1]
    w2, g2, m2, v2 = (a.reshape(-1, cols) for a in (w, g, m, v))
    rows = w2.shape[0]
    br = _row_block(rows, cols)

    def kern(w_ref, g_ref, m_ref, v_ref, d_ref, nm_ref, nv_ref):
        d_ref[...], nm_ref[...], nv_ref[...] = _adam_math(w_ref[...], g_ref[...], m_ref[...], v_ref[...])

    spec = pl.BlockSpec((br, cols), lambda i: (i, 0))
    outs = pl.pallas_call(
        kern, name=name, grid=(rows // br,), in_specs=[spec] * 4, out_specs=[spec] * 3,
        out_shape=[jax.ShapeDtypeStruct((rows, cols), F32)] * 3,
        compiler_params=_cp(("arbitrary",)),
    )(w2, g2, m2, v2)
    return [o.reshape(shape) for o in outs]


_NAMES = ['ln_in_g', 'ln_in_b', 'w_in', 'b_forget', 'conv_w', 'conv_b', 'w_r', 'b_r', 'w_i', 'b_i', 'lru_lambda',
          'rel_bias', 'w_branch', 'w_gate', 'b_gate', 'w_out', 'ln1_g', 'ln1_b', 'w_ff1', 'w_ff2', 'ln2_g', 'ln2_b']
_BIG = {'w_in': 2, 'w_branch': 3, 'w_gate': 2, 'w_out': 1, 'w_ff1': 2, 'w_ff2': 1}
_SMALL_SHARDED = {'b_gate': 2, 'conv_w': 2, 'rel_bias': 2}
_SHARDED = {**_BIG, **_SMALL_SHARDED}
_REPLICATED = [n for n in _NAMES if n not in _SHARDED]


def _tiles(a, cols):
    flat = a.reshape(-1)
    per = 8 * cols
    flat = jnp.pad(flat, (0, (-flat.shape[0]) % per))
    return flat.reshape(-1, cols)


def _pack(arrs, cols):
    return jnp.concatenate([_tiles(a, cols) for a in arrs], axis=0)


def _unpack(packed, like, cols):
    out, r0 = [], 0
    for a in like:
        n = math.prod(a.shape)
        rows = -(-n // (8 * cols)) * 8
        out.append(packed[r0:r0 + rows].reshape(-1)[:n].reshape(a.shape))
        r0 += rows
    return out


_EARLY = ['w_branch', 'w_gate', 'w_out', 'w_ff1', 'w_ff2']


def _chip_major_early(G, l):
    return [G[('w_branch', l)].reshape(N_CHIPS, N_BRANCH * BRANCH_WIDTH, BRANCH_WIDTH),
            G[('w_gate', l)].reshape(N_CHIPS, N_BRANCH * (D_MODEL // N_CHIPS), D_MODEL),
            G[('w_out', l)], G[('w_ff1', l)], G[('w_ff2', l)]]


def _chip_major_late(G, l):
    g_att, g_rec = G[('w_att', l)], G[('w_rec', l)]
    w_in = jnp.concatenate([g_att[:, :1536], g_rec[:, 1024:1024 + N_HEADS], g_rec[:, :1024], g_att[:, 1536:]], -1)
    per_chip = lambda g, rows: g.reshape(rows, N_CHIPS, -1).transpose(1, 0, 2)
    bg = per_chip(G[('b_gate', l)], N_BRANCH)
    cw = per_chip(G[('conv_w', l)], CONV_WIDTH)
    rb = per_chip(G[('rel_bias', l)], N_HEADS)
    small = jnp.stack([_pack([bg[j], cw[j], rb[j]], LANES) for j in range(N_CHIPS)])
    return [w_in.reshape(D_MODEL, N_CHIPS, D_IN // N_CHIPS).transpose(1, 0, 2), small]


class _Hook:
    def __init__(self, spec, done):
        self.spec, self.done = spec, done


def _unshard(blocks, axis):
    return jnp.concatenate([blocks[k] for k in range(N_CHIPS)], axis=axis)


def kernel(x, ln_in_g, ln_in_b, w_in, b_forget, conv_w, conv_b, w_r, b_r, w_i, b_i, lru_lambda, rel_bias, w_branch, w_gate, b_gate, w_out, ln1_g, ln1_b, w_ff1, w_ff2, ln2_g, ln2_b, loss_target, m_ln_in_g, m_ln_in_b, m_w_in, m_b_forget, m_conv_w, m_conv_b, m_w_r, m_b_r, m_w_i, m_b_i, m_lru_lambda, m_rel_bias, m_w_branch, m_w_gate, m_b_gate, m_w_out, m_ln1_g, m_ln1_b, m_w_ff1, m_w_ff2, m_ln2_g, m_ln2_b, v_ln_in_g, v_ln_in_b, v_w_in, v_b_forget, v_conv_w, v_conv_b, v_w_r, v_b_r, v_w_i, v_b_i, v_lru_lambda, v_rel_bias, v_w_branch, v_w_gate, v_b_gate, v_w_out, v_ln1_g, v_ln1_b, v_w_ff1, v_w_ff2, v_ln2_g, v_ln2_b):
    w = dict(zip(_NAMES, (ln_in_g, ln_in_b, w_in, b_forget, conv_w, conv_b, w_r, b_r, w_i, b_i, lru_lambda, rel_bias,
                          w_branch, w_gate, b_gate, w_out, ln1_g, ln1_b, w_ff1, w_ff2, ln2_g, ln2_b)))
    m = dict(zip(_NAMES, (m_ln_in_g, m_ln_in_b, m_w_in, m_b_forget, m_conv_w, m_conv_b, m_w_r, m_b_r, m_w_i, m_b_i,
                          m_lru_lambda, m_rel_bias, m_w_branch, m_w_gate, m_b_gate, m_w_out, m_ln1_g, m_ln1_b,
                          m_w_ff1, m_w_ff2, m_ln2_g, m_ln2_b)))
    v = dict(zip(_NAMES, (v_ln_in_g, v_ln_in_b, v_w_in, v_b_forget, v_conv_w, v_conv_b, v_w_r, v_b_r, v_w_i, v_b_i,
                          v_lru_lambda, v_rel_bias, v_w_branch, v_w_gate, v_b_gate, v_w_out, v_ln1_g, v_ln1_b,
                          v_w_ff1, v_w_ff2, v_ln2_g, v_ln2_b)))
    c = lax.axis_index("c")

    k = 2 * lax.axis_index("x") + lax.axis_index("y")
    state = {}

    small_like = [w[n] for n in _SMALL_SHARDED]
    small_pack = jnp.stack([_pack([a[l] for a in small_like], LANES) for l in range(DEPTH)])
    W = _prep_weights({n: w[n] for n in _REPLICATED})
    got_in, got_small = _exchange(_gather_spec([w['w_in'].astype(BF16), small_pack]), name='gather_first')
    small_blocks = [[_unpack(got_small[j, l], [a[l] for a in small_like], LANES) for l in range(DEPTH)]
                    for j in range(N_CHIPS)]
    first = {'w_in': _unshard(got_in, _BIG['w_in'])}
    for i, n in enumerate(_SMALL_SHARDED):
        first[n] = jnp.concatenate([jnp.stack([small_blocks[j][l][i] for l in range(DEPTH)])
                                    for j in range(N_CHIPS)], axis=_SMALL_SHARDED[n])
    _prep_weights(first, W)

    def gather_on(names):
        chip_major = ('w_ff1', 'w_ff2')
        return _Hook(lambda W_, G_: _gather_spec([w[n].astype(BF16) for n in names]),
                     lambda outs, W_, G_: _prep_weights(
                         {n: o if n in chip_major else _unshard(o, _BIG[n]) for n, o in zip(names, outs)}, W_))

    def pair_spec(W_, G_):
        state['early'] = [_chip_major_early(G_, l) for l in range(DEPTH)]
        return _pair_spec(*state['early'])

    def pair_done(outs, W_, G_):
        state['pair_sum'] = [_sum_pair(a0, a1, o, c, name='grad_pair_sum')
                             for a0, a1, o in zip(*state['early'], outs)]

    def late_spec(W_, G_):
        state['late'] = [_chip_major_late(G_, l) for l in range(DEPTH)]
        return _pair_spec(*state['late'])

    def late_done(outs, W_, G_):
        state['late_sum'] = [_sum_pair(a0, a1, o, c, name='grad_pair_sum') for a0, a1, o in zip(*state['late'], outs)]

    rep_main = _REPLICATED[2:]

    def rep_spec(W_, G_):
        dev = [jnp.stack([G_[(n, l)].reshape(w[n].shape[1:]) for l in range(DEPTH)]) for n in rep_main]
        packed = _pack(dev, LANES)
        return _gather8_spec(jnp.pad(packed, ((0, (-packed.shape[0]) % 256), (0, 0))))

    hooks = {('in_proj_att', 0): gather_on(['w_out']),
             ('fox_fwd', 0): gather_on(['w_gate']),
             ('sb_fwd', 0): gather_on(['w_ff1']),
             ('chunk_fwd', 0): gather_on(['w_branch']),
             ('gate_proj', 0): gather_on(['w_ff2']),
             ('chunk_bwd', 0): _Hook(rep_spec, lambda outs, W_, G_: state.update(rep_all=outs[0])),
             ('gate_dx', 0): _Hook(late_spec, late_done),
             ('in_att_dx', 0): _Hook(lambda W_, G_: _chip_spec(state['late_sum']),
                                     lambda outs, W_, G_: state.update(late_chips=outs)),
             ('fox_bwd', 0): _Hook(pair_spec, pair_done),
             ('sb_bwd', 0): _Hook(lambda W_, G_: _chip_spec(state['pair_sum']),
                                  lambda outs, W_, G_: state.update(from_chips=outs))}
    loss, gx, G = _device_step(x[0], loss_target[0], W, hooks)

    late_sum, late_chips = state['late_sum'], state['late_chips']
    pair_sum = [late_sum[0]] + state['pair_sum'] + [late_sum[1]]
    from_chips = [late_chips[0]] + state['from_chips'] + [late_chips[1]]
    mine = [_sum_chips(s, got, k, name='grad_chip_sum') for s, got in zip(pair_sum, from_chips)]
    theirs = _pair_swap(mine, name='grad_pair_swap')

    g_rep = dict(zip(rep_main, _unpack(_sum_slots(state['rep_all'], name='grad_sum8'), [w[n] for n in rep_main], LANES)))
    entry = _exchange(_gather8_spec(_pack([G[('ln_in_g', -1)][0], G[('ln_in_b', -1)][0]], LANES)),
                      name='grad_gather8')[0]
    g_rep.update(zip(_REPLICATED[:2], _unpack(_sum_slots(entry, name='grad_sum8'), [w[n] for n in _REPLICATED[:2]],
                                              LANES)))

    grads, delta, new_m, new_v = {}, {}, {}, {}
    for n, a, b in zip(_BIG, mine, theirs):
        grads[n], delta[n], new_m[n], new_v[n] = _adamw_layers(w[n], a, b, m[n], v[n], c, name='adamw')
    small_layers = [jnp.where(c == l, mine[-1], theirs[-1]) for l in range(DEPTH)]
    small_shards = [_unpack(s, [w[n][0] for n in _SMALL_SHARDED], LANES) for s in small_layers]
    g_shard = {n: jnp.stack([small_shards[l][i] for l in range(DEPTH)]) for i, n in enumerate(_SMALL_SHARDED)}
    small = _REPLICATED + list(_SMALL_SHARDED)
    for n in small:
        grads[n] = g_rep[n] if n in g_rep else g_shard[n]
    packs = [_pack([d[n] for n in small], LANES) for d in (w, grads, m, v)]
    outs = _adamw(*packs, name='adamw_small')
    small_like_all = [w[n] for n in small]
    for d, o in zip((delta, new_m, new_v), outs):
        d.update(zip(small, _unpack(o, small_like_all, LANES)))

    loss = lax.psum(loss, ("x", "y", "c"))
    return (loss, gx[None], *[grads[n] for n in _NAMES], *[delta[n] for n in _NAMES],
            *[new_m[n] for n in _NAMES], *[new_v[n] for n in _NAMES])
```

```python
import functools
import math

import jax
import jax.numpy as jnp
from jax import lax
from jax.experimental import pallas as pl
from jax.experimental.pallas import tpu as pltpu

F32 = jnp.float32
BF16 = jnp.bfloat16

D_MODEL = 2048
DEPTH = 2
CHUNK = 64
HEAD_DIM = 128
N_BRANCH = 4
BRANCH_WIDTH = 512
N_HEADS = 4
CONV_WIDTH = 4
LRU_C = 8.0
LOOKBACK_CHUNKS = 8
BAND = (LOOKBACK_CHUNKS + 1) * CHUNK
PAD_ROWS = LOOKBACK_CHUNKS * CHUNK
REL_CLIP = 256
REL_TABLE = REL_CLIP + CHUNK
REL_PAD = 384
D_FF = 4 * D_MODEL
FF_SHARD = D_FF // 4
D_IN = 5636
ALPHA = (2.0 * DEPTH) ** 0.25
LN_EPS = 1e-5
SCALE = HEAD_DIM ** -0.5

ADAM_LR = 0.001
ADAM_B1 = 0.9
ADAM_B2 = 0.999
ADAM_EPS = 1e-08
ADAM_WD = 0.01
ADAM_STEP = 10

N_ATT = 9 * BRANCH_WIDTH
N_REC = 2 * BRANCH_WIDTH + 128

V7X_VMEM_LIMIT = 56 * 1024 * 1024
LANES = 128
ATT_BLOCK = 256
ATT_KEYS = 1024

NT = (((1,), (1,)), ((), ()))
TN = (((0,), (0,)), ((), ()))
NN = (((1,), (0,)), ((), ()))

MESH = pl.DeviceIdType.MESH


def _cp(sem=None):
    return pltpu.CompilerParams(dimension_semantics=sem, vmem_limit_bytes=V7X_VMEM_LIMIT)


def _dot(a, b, dims=NN):
    return lax.dot_general(a, b, dims, preferred_element_type=F32)


def _pick(n, prefs):
    for p in prefs:
        if n % p == 0:
            return p
    return n


def _split3(x):
    hi = x.astype(BF16)
    r1 = x - hi.astype(F32)
    mid = r1.astype(BF16)
    lo = (r1 - mid.astype(F32)).astype(BF16)
    return hi, mid, lo


def _split2(x):
    hi = x.astype(BF16)
    lo = (x - hi.astype(F32)).astype(BF16)
    return hi, lo


def _sigmoid(z):
    return 1.0 / (1.0 + jnp.exp(-z))


def _log_sigmoid(z):
    return jnp.minimum(z, 0.0) - jnp.log(1.0 + jnp.exp(-jnp.abs(z)))


def _mm(a, b, *, name, ta=False, tb=False, out_dtypes=(F32,), epilogue=None, extras=(),
        bm=None, bn=None, bk=None, out_map=None, b_view=None, carry=None):
    M, K = (a.shape[1], a.shape[0]) if ta else a.shape
    N = b.shape[0] if tb else b.shape[1]
    if b_view is not None:
        K, N = b_view[:2]
    bm = bm or _pick(M, (1024, 512, 256, 128))
    bn = bn or _pick(N, (1024, 1536, 1152, 512, 256, 128))
    bk = bk or _pick(K, (2048, 1536, 1024, 1152, 512, 256, 128))
    nk = K // bk
    a_spec = pl.BlockSpec((bk, bm), lambda i, j, k: (k, i)) if ta else pl.BlockSpec((bm, bk), lambda i, j, k: (i, k))
    b_spec = pl.BlockSpec((bn, bk), lambda i, j, k: (j, k)) if tb else pl.BlockSpec((bk, bn), lambda i, j, k: (k, j))
    if b_view is not None:
        b_spec = pl.BlockSpec(b_view[2], b_view[3])
    ex_specs = [pl.BlockSpec((bm, bn), lambda i, j, k: (i, j)) if kind == 'mn'
                else pl.BlockSpec((1, bn), lambda i, j, k: (0, j)) for _, kind in extras]
    n_ex, n_out = len(extras), len(out_dtypes)
    dims = TN if ta else (NT if tb else NN)

    def kern(*refs):
        a_ref, b_ref = refs[0], refs[1]
        ex_refs = refs[2:2 + n_ex]
        out_refs = refs[2 + n_ex:2 + n_ex + n_out]
        acc_ref = refs[-1]
        k = pl.program_id(2)
        part = _dot(a_ref[...].astype(BF16), b_ref[...].astype(BF16), dims)

        @pl.when(k == 0)
        def _():
            acc_ref[...] = part

        @pl.when(k > 0)
        def _():
            acc_ref[...] += part

        @pl.when(k == nk - 1)
        def _():
            acc = acc_ref[...]
            outs = (acc,) if epilogue is None else epilogue(acc, *[r[...] for r in ex_refs])
            for o_ref, o in zip(out_refs, outs):
                o_ref[...] = o.astype(o_ref.dtype).reshape(o_ref.shape)

    if out_map is None:
        out_specs = [pl.BlockSpec((bm, bn), lambda i, j, k: (i, j)) for _ in out_dtypes]
        out_shape = [jax.ShapeDtypeStruct((M, N), dt) for dt in out_dtypes]
    else:
        shape, block, index = out_map
        out_specs = [pl.BlockSpec(block, lambda i, j, k: index(i, j))]
        out_shape = [jax.ShapeDtypeStruct(shape, out_dtypes[0])]
    res, extra = _call_with_carry(
        kern, name=name, grid=(M // bm, N // bn, nk), carry=carry,
        in_specs=[a_spec, b_spec] + ex_specs, out_specs=out_specs, out_shape=out_shape,
        scratch_shapes=[pltpu.VMEM((bm, bn), F32)], args=(a, b, *[e for e, _ in extras]),
        semantics=("parallel", "parallel", "arbitrary"))
    res = res[0] if n_out == 1 else res
    return res if carry is None else (res, extra)


def _ln_fwd(h, g, b, *, name):
    T, D = h.shape
    bt = _pick(T, (512, 256, 128))

    def kern(h_ref, g_ref, b_ref, y_ref, yb_ref):
        x = h_ref[...]
        mu = jnp.mean(x, axis=-1, keepdims=True)
        xc = x - mu
        var = jnp.mean(xc * xc, axis=-1, keepdims=True)
        y = xc * lax.rsqrt(var + LN_EPS) * g_ref[...] + b_ref[...]
        y_ref[...] = y
        yb_ref[...] = y.astype(BF16)

    row = pl.BlockSpec((bt, D), lambda i: (i, 0))
    vec = pl.BlockSpec((1, D), lambda i: (0, 0))
    return pl.pallas_call(
        kern, name=name, grid=(T // bt,), in_specs=[row, vec, vec], out_specs=[row, row],
        out_shape=[jax.ShapeDtypeStruct((T, D), F32), jax.ShapeDtypeStruct((T, D), BF16)],
        compiler_params=_cp(("arbitrary",)),
    )(h, g.reshape(1, D), b.reshape(1, D))


def _ln_bwd(h, dy, g, *, name):
    T, D = h.shape
    bt = _pick(T, (512, 256, 128))

    def kern(h_ref, dy_ref, g_ref, dh_ref, dhb_ref, dg_ref, db_ref):
        i = pl.program_id(0)
        x = h_ref[...]
        dyv = dy_ref[...]
        mu = jnp.mean(x, axis=-1, keepdims=True)
        xc = x - mu
        var = jnp.mean(xc * xc, axis=-1, keepdims=True)
        rstd = lax.rsqrt(var + LN_EPS)
        xhat = xc * rstd
        dxh = dyv * g_ref[...]
        m1 = jnp.mean(dxh, axis=-1, keepdims=True)
        m2 = jnp.mean(dxh * xhat, axis=-1, keepdims=True)
        dh = rstd * (dxh - m1 - xhat * m2)
        dh_ref[...] = dh
        dhb_ref[...] = dh.astype(BF16)
        pg = jnp.sum(dyv * xhat, axis=0, keepdims=True)
        pb = jnp.sum(dyv, axis=0, keepdims=True)

        @pl.when(i == 0)
        def _():
            dg_ref[...] = pg
            db_ref[...] = pb

        @pl.when(i > 0)
        def _():
            dg_ref[...] += pg
            db_ref[...] += pb

    row = pl.BlockSpec((bt, D), lambda i: (i, 0))
    vec = pl.BlockSpec((1, D), lambda i: (0, 0))
    return pl.pallas_call(
        kern, name=name, grid=(T // bt,), in_specs=[row, row, vec], out_specs=[row, row, vec, vec],
        out_shape=[jax.ShapeDtypeStruct((T, D), F32), jax.ShapeDtypeStruct((T, D), BF16),
                   jax.ShapeDtypeStruct((1, D), F32), jax.ShapeDtypeStruct((1, D), F32)],
        compiler_params=_cp(("arbitrary",)),
    )(h, dy, g.reshape(1, D))


def _loss_head(y, tgt, *, name):
    T, D = y.shape
    bt = _pick(T, (512, 256, 128))

    def kern(y_ref, t_ref, dy_ref, loss_ref):
        i = pl.program_id(0)
        e = y_ref[...] - t_ref[...]
        dy_ref[...] = e * (1.0 / D)
        part = 0.5 * jnp.sum(jnp.sum(e * e, axis=-1, keepdims=True) * (1.0 / D), axis=0, keepdims=True)
        part = jnp.broadcast_to(part, (8, LANES))

        @pl.when(i == 0)
        def _():
            loss_ref[...] = part

        @pl.when(i > 0)
        def _():
            loss_ref[...] += part

    row = pl.BlockSpec((bt, D), lambda i: (i, 0))
    return pl.pallas_call(
        kern, name=name, grid=(T // bt,), in_specs=[row, row],
        out_specs=[row, pl.BlockSpec((8, LANES), lambda i: (0, 0))],
        out_shape=[jax.ShapeDtypeStruct((T, D), F32), jax.ShapeDtypeStruct((8, LANES), F32)],
        compiler_params=_cp(("arbitrary",)),
    )(y, tgt)


def _tri(n, upper):
    r = lax.broadcasted_iota(jnp.int32, (n, n), 0)
    c = lax.broadcasted_iota(jnp.int32, (n, n), 1)
    return jnp.where((c >= r) if upper else (c <= r), 1.0, 0.0).astype(BF16)


def _forget_fwd(ff, bf, *, name):
    T = ff.shape[0]
    bt = 256

    def kern(ff_ref, bf_ref, out_ref, carry):
        i = pl.program_id(0)

        @pl.when(i == 0)
        def _():
            carry[...] = jnp.zeros_like(carry)

        ls = _log_sigmoid(ff_ref[...] + bf_ref[...])
        tri = _tri(bt, upper=False)
        hi, mid, lo = _split3(ls)
        cs = _dot(tri, hi) + _dot(tri, mid) + _dot(tri, lo) + carry[0:1, :]
        out_ref[...] = cs
        carry[...] = jnp.broadcast_to(cs[bt - 1:bt, :], carry.shape)

    return pl.pallas_call(
        kern, name=name, grid=(T // bt,),
        in_specs=[pl.BlockSpec((bt, LANES), lambda i: (i, 0)), pl.BlockSpec((1, LANES), lambda i: (0, 0))],
        out_specs=pl.BlockSpec((bt, LANES), lambda i: (i, 0)),
        out_shape=jax.ShapeDtypeStruct((T, LANES), F32),
        scratch_shapes=[pltpu.VMEM((8, LANES), F32)],
        compiler_params=_cp(("arbitrary",)),
    )(ff, bf)


def _forget_bwd(dFk, dFq, ff, bf, *, name):
    T = ff.shape[0]
    bt = 256
    nb = T // bt

    def kern(dFk_ref, dFq_ref, ff_ref, bf_ref, dff_ref, dbf_ref, carry):
        i = pl.program_id(0)

        @pl.when(i == 0)
        def _():
            carry[...] = jnp.zeros_like(carry)
            dbf_ref[...] = jnp.zeros_like(dbf_ref)

        tri = _tri(bt, upper=True)
        hi, mid, lo = _split3(dFk_ref[...] + dFq_ref[...])
        rs = _dot(tri, hi) + _dot(tri, mid) + _dot(tri, lo) + carry[0:1, :]
        carry[...] = jnp.broadcast_to(rs[0:1, :], carry.shape)
        z = ff_ref[...] + bf_ref[...]
        dff = rs * _sigmoid(-z)
        dff_ref[...] = dff.astype(dff_ref.dtype)
        dbf_ref[...] += jnp.sum(dff, axis=0, keepdims=True)

    rev = pl.BlockSpec((bt, LANES), lambda i: (nb - 1 - i, 0))
    vec = pl.BlockSpec((1, LANES), lambda i: (0, 0))
    return pl.pallas_call(
        kern, name=name, grid=(nb,), in_specs=[rev, rev, rev, vec], out_specs=[rev, vec],
        out_shape=[jax.ShapeDtypeStruct((T, LANES), BF16), jax.ShapeDtypeStruct((1, LANES), F32)],
        scratch_shapes=[pltpu.VMEM((8, LANES), F32)],
        compiler_params=_cp(("arbitrary",)),
    )(dFk, dFq, ff, bf)


def _head_lane(x, h):
    lane = lax.broadcasted_iota(jnp.int32, x.shape, 1)
    return jnp.sum(jnp.where(lane == h, x, 0.0), axis=1, keepdims=True)


def _att_blocks(T):
    return min(ATT_BLOCK, T), min(ATT_KEYS, T)


def _positions(i, j, bq, bk):
    r = i * bq + lax.broadcasted_iota(jnp.int32, (bq, bk), 0)
    c = j * bk + lax.broadcasted_iota(jnp.int32, (bq, bk), 1)
    return r, c


def _fox_fwd(u_att, fcum, frow, *, name, carry=None):
    T = u_att.shape[0]
    bq, bk = _att_blocks(T)
    nq, nk = T // bq, T // bk
    H = N_HEADS

    def kern(q_ref, k_ref, v_ref, fc_ref, fr_ref, o_ref, lse_ref):
        i = pl.program_id(1)
        q = q_ref[...]
        fq = _head_lane(fc_ref[...], pl.program_id(0))

        def step(j, carry, masked):
            m, l, acc = carry
            off = pl.multiple_of(j * bk, bk)
            k = k_ref[pl.ds(off, bk), :]
            v = v_ref[pl.ds(off, bk), :]
            s = _dot(q, k, NT) * SCALE + (fq - fr_ref[j])
            if masked:
                r, c = _positions(i, j, bq, bk)
                s = jnp.where(c <= r, s, -jnp.inf)
            m_new = jnp.maximum(m, jnp.max(s, axis=1, keepdims=True))
            a = jnp.exp(m - m_new)
            p = jnp.exp(s - m_new)
            l = a * l + jnp.sum(p, axis=1, keepdims=True)
            acc = a * acc + _dot(p.astype(BF16), v)
            return m_new, l, acc

        init = (jnp.full((bq, 1), -1e30, F32), jnp.zeros((bq, 1), F32), jnp.zeros((bq, HEAD_DIM), F32))
        nfull = (i * bq) // bk
        carry = lax.fori_loop(0, nfull, lambda j, cr: step(j, cr, False), init)
        m, l, acc = step(nfull, carry, True)
        o_ref[...] = (acc / l).astype(o_ref.dtype)
        lse_ref[...] = m + jnp.log(l)

    return _call_with_carry(
        kern, name=name, grid=(H, nq), carry=carry,
        in_specs=[pl.BlockSpec((bq, HEAD_DIM), lambda h, i: (i, h)),
                  pl.BlockSpec((T, HEAD_DIM), lambda h, i: (0, 4 + h)),
                  pl.BlockSpec((T, HEAD_DIM), lambda h, i: (0, 8 + h)),
                  pl.BlockSpec((bq, LANES), lambda h, i: (i, 0)),
                  pl.BlockSpec((None, nk, 1, bk), lambda h, i: (h, 0, 0, 0))],
        out_specs=[pl.BlockSpec((bq, HEAD_DIM), lambda h, i: (i, h)),
                   pl.BlockSpec((None, bq, 1), lambda h, i: (h, i, 0))],
        out_shape=[jax.ShapeDtypeStruct((T, BRANCH_WIDTH), BF16), jax.ShapeDtypeStruct((H, T, 1), F32)],
        scratch_shapes=[], args=(u_att, u_att, u_att, fcum, frow))


def _row_dot(a, b, *, name):
    T = a.shape[0]
    bt = _pick(T, (512, 256, 128))

    def kern(a_ref, b_ref, o_ref):
        p = a_ref[...].astype(F32) * b_ref[...].astype(F32)
        for h in range(N_HEADS):
            o_ref[h] = jnp.sum(p[:, h * HEAD_DIM:(h + 1) * HEAD_DIM], axis=1, keepdims=True)

    row = pl.BlockSpec((bt, BRANCH_WIDTH), lambda i: (i, 0))
    return pl.pallas_call(
        kern, name=name, grid=(T // bt,), in_specs=[row, row],
        out_specs=pl.BlockSpec((N_HEADS, bt, 1), lambda i: (0, i, 0)),
        out_shape=jax.ShapeDtypeStruct((N_HEADS, T, 1), F32),
        compiler_params=_cp(("arbitrary",)),
    )(a, b)


def _fox_bwd(u_att, do, lse, delta, fcum, frow, *, name, carry=None):
    T = u_att.shape[0]
    bq, bk = _att_blocks(T)
    nq, nk = T // bq, T // bk
    H = N_HEADS

    def kern(q_ref, k_ref, v_ref, do_ref, lse_ref, dl_ref, fc_ref, fr_ref,
             dq_ref, dk_ref, dv_ref, df_ref, dfq_ref, dk_acc, dv_acc, df_acc):
        i = pl.program_id(1)

        @pl.when(i == 0)
        def _():
            dk_acc[...] = jnp.zeros_like(dk_acc)
            dv_acc[...] = jnp.zeros_like(dv_acc)
            df_acc[...] = jnp.zeros_like(df_acc)

        q = q_ref[...]
        dov = do_ref[...]
        fq = _head_lane(fc_ref[...], pl.program_id(0))
        lsev = lse_ref[...]
        dlt = dl_ref[...]

        def step(j, carry, masked):
            dq, dfq = carry
            off = pl.multiple_of(j * bk, bk)
            k = k_ref[pl.ds(off, bk), :]
            v = v_ref[pl.ds(off, bk), :]
            s = _dot(q, k, NT) * SCALE + (fq - fr_ref[j])
            p = jnp.exp(s - lsev)
            if masked:
                r, c = _positions(i, j, bq, bk)
                p = jnp.where(c <= r, p, 0.0)
            dp = _dot(dov, v, NT)
            ds = p * (dp - dlt)
            dsb = ds.astype(BF16)
            dq = dq + _dot(dsb, k)
            dk_acc[pl.ds(off, bk), :] += _dot(dsb, q, TN)
            dv_acc[pl.ds(off, bk), :] += _dot(p.astype(BF16), dov, TN)
            df_acc[j] += -jnp.sum(ds, axis=0, keepdims=True)
            return dq, dfq + jnp.sum(ds, axis=1, keepdims=True)

        nfull = (i * bq) // bk
        carry = lax.fori_loop(0, nfull, lambda j, cr: step(j, cr, False),
                              (jnp.zeros((bq, HEAD_DIM), F32), jnp.zeros((bq, 1), F32)))
        dq, dfq = step(nfull, carry, True)
        dq_ref[...] = (dq * SCALE).astype(dq_ref.dtype)
        dfq_ref[...] = dfq

        @pl.when(i == nq - 1)
        def _():
            dk_ref[...] = (dk_acc[...] * SCALE).astype(dk_ref.dtype)
            dv_ref[...] = dv_acc[...].astype(dv_ref.dtype)
            df_ref[...] = df_acc[...]

    col = lambda: pl.BlockSpec((None, bq, 1), lambda h, i: (h, i, 0))
    return _call_with_carry(
        kern, name=name, grid=(H, nq), carry=carry,
        in_specs=[pl.BlockSpec((bq, HEAD_DIM), lambda h, i: (i, h)),
                  pl.BlockSpec((T, HEAD_DIM), lambda h, i: (0, 4 + h)),
                  pl.BlockSpec((T, HEAD_DIM), lambda h, i: (0, 8 + h)),
                  pl.BlockSpec((bq, HEAD_DIM), lambda h, i: (i, h)),
                  col(), col(), pl.BlockSpec((bq, LANES), lambda h, i: (i, 0)),
                  pl.BlockSpec((None, nk, 1, bk), lambda h, i: (h, 0, 0, 0))],
        out_specs=[pl.BlockSpec((bq, HEAD_DIM), lambda h, i: (i, h)),
                   pl.BlockSpec((T, HEAD_DIM), lambda h, i: (0, h)),
                   pl.BlockSpec((T, HEAD_DIM), lambda h, i: (0, h)),
                   pl.BlockSpec((None, nk, 1, bk), lambda h, i: (h, 0, 0, 0)),
                   pl.BlockSpec((None, bq, 1), lambda h, i: (h, i, 0))],
        out_shape=[jax.ShapeDtypeStruct((T, BRANCH_WIDTH), BF16)] * 3
                  + [jax.ShapeDtypeStruct((H, nk, 1, bk), F32), jax.ShapeDtypeStruct((H, T, 1), F32)],
        scratch_shapes=[pltpu.VMEM((T, HEAD_DIM), F32), pltpu.VMEM((T, HEAD_DIM), F32),
                        pltpu.VMEM((nk, 1, bk), F32)],
        args=(u_att, u_att, u_att, do, lse, delta, fcum, frow))


def _softplus_parts(z):
    t = jnp.exp(-jnp.abs(z))
    sp = jnp.maximum(z, 0.0) + jnp.log(1.0 + t)
    return t, sp


def _sb_tri(B):
    r = lax.broadcasted_iota(jnp.int32, (B, B), 0)
    c = lax.broadcasted_iota(jnp.int32, (B, B), 1)
    suffix = jnp.where(r >= c, 1.0, 0.0).astype(BF16)
    prefix = jnp.where(r <= c, 1.0, 0.0).astype(BF16)
    return suffix, prefix


def _sb_fwd(u_att, *, name, carry=None):
    T = u_att.shape[0]
    B, bk = _att_blocks(T)
    nq, nsub = T // B, bk // B
    H = N_HEADS

    def kern(q_ref, k_ref, v_ref, o_ref):
        i = pl.program_id(1)
        q = q_ref[...]
        suffix, _ = _sb_tri(B)

        def step(j, carry, masked):
            run, acc = carry
            parts = []
            for s in reversed(range(nsub)):
                jb = j * nsub + s
                off = pl.multiple_of(jb * B, B)
                k = k_ref[pl.ds(off, B), :]
                z = _dot(q, k, NT) * SCALE
                _, sp = _softplus_parts(z)
                lg = -sp
                valid = None
                if masked:
                    r, c = _positions(i, jb, B, B)
                    valid = c < r
                    lg = jnp.where(valid, lg, 0.0)
                hi, lo = _split2(lg)
                cum = _dot(hi, suffix) + _dot(lo, suffix)
                parts.append((off, z, cum, jnp.sum(lg, axis=1, keepdims=True), valid))
            for off, z, cum, rs, valid in parts:
                a = jnp.exp(z + cum + run)
                if masked:
                    a = jnp.where(valid, a, 0.0)
                acc = acc + _dot(a.astype(BF16), v_ref[pl.ds(off, B), :])
                run = run + rs
            return run, acc

        nfull = (i * B) // bk
        carry = step(nfull, (jnp.zeros((B, 1), F32), jnp.zeros((B, HEAD_DIM), F32)), True)
        _, acc = lax.fori_loop(0, nfull, lambda jj, cr: step(nfull - 1 - jj, cr, False), carry)
        o_ref[...] = acc.astype(o_ref.dtype)

    return _call_with_carry(
        kern, name=name, grid=(H, nq), carry=carry,
        in_specs=[pl.BlockSpec((B, HEAD_DIM), lambda h, i: (i, 12 + h)),
                  pl.BlockSpec((T, HEAD_DIM), lambda h, i: (0, 16 + h)),
                  pl.BlockSpec((T, HEAD_DIM), lambda h, i: (0, 20 + h))],
        out_specs=[pl.BlockSpec((B, HEAD_DIM), lambda h, i: (i, h))],
        out_shape=[jax.ShapeDtypeStruct((T, BRANCH_WIDTH), BF16)],
        scratch_shapes=[], args=(u_att, u_att, u_att))


def _sb_bwd(u_att, do, *, name, carry=None):
    T = u_att.shape[0]
    B, bk = _att_blocks(T)
    nq, nsub = T // B, bk // B
    H = N_HEADS

    def kern(q_ref, k_ref, v_ref, do_ref, dq_ref, dk_ref, dv_ref, dk_acc, dv_acc, de_s, sg_s):
        i = pl.program_id(1)

        @pl.when(i == 0)
        def _():
            dk_acc[...] = jnp.zeros_like(dk_acc)
            dv_acc[...] = jnp.zeros_like(dv_acc)

        q = q_ref[...]
        dov = do_ref[...]
        suffix, prefix = _sb_tri(B)

        def sweep1(j, run, masked):
            parts = []
            for s in reversed(range(nsub)):
                jb = j * nsub + s
                off = pl.multiple_of(jb * B, B)
                k = k_ref[pl.ds(off, B), :]
                z = _dot(q, k, NT) * SCALE
                t, sp = _softplus_parts(z)
                lg = -sp
                sg = jnp.exp(z + lg)
                valid = None
                if masked:
                    r, c = _positions(i, jb, B, B)
                    valid = c < r
                    lg = jnp.where(valid, lg, 0.0)
                    sg = jnp.where(valid, sg, 0.0)
                sg_s[jb] = sg.astype(sg_s.dtype)
                hi, lo = _split2(lg)
                cum = _dot(hi, suffix) + _dot(lo, suffix)
                da = _dot(dov, v_ref[pl.ds(off, B), :], NT)
                parts.append((jb, off, z, cum, da, jnp.sum(lg, axis=1, keepdims=True), valid))
            for jb, off, z, cum, da, rs, valid in parts:
                a = jnp.exp(z + cum + run)
                if masked:
                    a = jnp.where(valid, a, 0.0)
                de_s[jb] = a * da
                dv_acc[pl.ds(off, B), :] += _dot(a.astype(BF16), dov, TN)
                run = run + rs
            return run

        nfull = (i * B) // bk
        run = sweep1(nfull, jnp.zeros((B, 1), F32), True)
        lax.fori_loop(0, nfull, lambda jj, cr: sweep1(nfull - 1 - jj, cr, False), run)

        def sweep2(j, carry):
            pre, dq = carry
            parts = []
            for s in range(nsub):
                jb = j * nsub + s
                de = de_s[jb]
                hi, lo = _split2(de)
                parts.append((jb, de, _dot(hi, prefix) + _dot(lo, prefix), jnp.sum(de, axis=1, keepdims=True)))
            for jb, de, g, rs in parts:
                off = pl.multiple_of(jb * B, B)
                dz = (de - sg_s[jb].astype(F32) * (g + pre)).astype(BF16)
                dq = dq + _dot(dz, k_ref[pl.ds(off, B), :])
                dk_acc[pl.ds(off, B), :] += _dot(dz, q, TN)
                pre = pre + rs
            return pre, dq

        _, dq = lax.fori_loop(0, nfull + 1, sweep2, (jnp.zeros((B, 1), F32), jnp.zeros((B, HEAD_DIM), F32)))
        dq_ref[...] = (dq * SCALE).astype(dq_ref.dtype)

        @pl.when(i == nq - 1)
        def _():
            dk_ref[...] = (dk_acc[...] * SCALE).astype(dk_ref.dtype)
            dv_ref[...] = dv_acc[...].astype(dv_ref.dtype)

    return _call_with_carry(
        kern, name=name, grid=(H, nq), carry=carry,
        in_specs=[pl.BlockSpec((B, HEAD_DIM), lambda h, i: (i, 12 + h)),
                  pl.BlockSpec((T, HEAD_DIM), lambda h, i: (0, 16 + h)),
                  pl.BlockSpec((T, HEAD_DIM), lambda h, i: (0, 20 + h)),
                  pl.BlockSpec((B, HEAD_DIM), lambda h, i: (i, h))],
        out_specs=[pl.BlockSpec((B, HEAD_DIM), lambda h, i: (i, h)),
                   pl.BlockSpec((T, HEAD_DIM), lambda h, i: (0, h)),
                   pl.BlockSpec((T, HEAD_DIM), lambda h, i: (0, h))],
        out_shape=[jax.ShapeDtypeStruct((T, BRANCH_WIDTH), BF16)] * 3,
        scratch_shapes=[pltpu.VMEM((T, HEAD_DIM), F32), pltpu.VMEM((T, HEAD_DIM), F32),
                        pltpu.VMEM((T // B, B, B), F32), pltpu.VMEM((T // B, B, B), BF16)],
        args=(u_att, u_att, u_att, do))


def _rel_onehot(qrow):
    k = lax.broadcasted_iota(jnp.int32, (BAND, REL_PAD), 0)
    rr = lax.broadcasted_iota(jnp.int32, (BAND, REL_PAD), 1)
    idx = jnp.clip(PAD_ROWS + qrow - k, -(CHUNK - 1), REL_CLIP) + (CHUNK - 1)
    return jnp.where(idx == rr, 1.0, 0.0).astype(BF16)


def _band_bias(table, *, name):
    def kern(t_ref, o_ref):
        hi, mid, lo = _split3(t_ref[...])

        def body(qrow, _):
            oh = _rel_onehot(qrow)
            o_ref[qrow] = _dot(hi, oh, NT) + _dot(mid, oh, NT) + _dot(lo, oh, NT)
            return 0

        lax.fori_loop(0, CHUNK, body, 0)

    return pl.pallas_call(
        kern, name=name, out_shape=jax.ShapeDtypeStruct((CHUNK, 8, BAND), F32),
        compiler_params=_cp(),
    )(table)


def _band_bias_bwd(dbias, *, name):
    def kern(d_ref, o_ref):
        def body(qrow, acc):
            oh = _rel_onehot(qrow)
            hi, mid, lo = _split3(d_ref[qrow])
            return acc + _dot(hi, oh) + _dot(mid, oh) + _dot(lo, oh)

        o_ref[...] = lax.fori_loop(0, CHUNK, body, jnp.zeros((8, REL_PAD), F32))

    return pl.pallas_call(
        kern, name=name, out_shape=jax.ShapeDtypeStruct((8, REL_PAD), F32),
        compiler_params=_cp(),
    )(dbias)


def _chunk_rows(T):
    return _pick(T, (512, 256, 128, 64))


def _chunk_scores(q, kw, bias, c_global):
    s = _dot(q, kw, NT) * SCALE + bias
    col = lax.broadcasted_iota(jnp.int32, (CHUNK, BAND), 1)
    valid = (c_global * CHUNK + col) >= PAD_ROWS
    s = jnp.where(valid, s, -jnp.inf)
    m = jnp.max(s, axis=1, keepdims=True)
    e = jnp.exp(s - m)
    return e / jnp.sum(e, axis=1, keepdims=True)


def _chunk_fwd(u_att, bias, *, name, carry=None):
    T = u_att.shape[0]
    R = _chunk_rows(T)
    nr = T // R
    H = N_HEADS

    def kern(q_ref, k_ref, v_ref, b_ref, o_ref, kpad, vpad):
        i = pl.program_id(1)

        @pl.when(i == 0)
        def _():
            kpad[0:PAD_ROWS, :] = jnp.zeros((PAD_ROWS, HEAD_DIM), BF16)
            vpad[0:PAD_ROWS, :] = jnp.zeros((PAD_ROWS, HEAD_DIM), BF16)
            kpad[PAD_ROWS:, :] = k_ref[...]
            vpad[PAD_ROWS:, :] = v_ref[...]

        bias_v = b_ref[...]
        for cc in range(R // CHUNK):
            cg = i * (R // CHUNK) + cc
            off = pl.multiple_of(cg * CHUNK, CHUNK)
            q = q_ref[cc * CHUNK:(cc + 1) * CHUNK, :]
            kw = kpad[pl.ds(off, BAND), :]
            vw = vpad[pl.ds(off, BAND), :]
            p = _chunk_scores(q, kw, bias_v, cg)
            o_ref[cc * CHUNK:(cc + 1) * CHUNK, :] = _dot(p.astype(BF16), vw).astype(o_ref.dtype)

    return _call_with_carry(
        kern, name=name, grid=(H, nr), carry=carry,
        in_specs=[pl.BlockSpec((R, HEAD_DIM), lambda h, i: (i, 24 + h)),
                  pl.BlockSpec((T, HEAD_DIM), lambda h, i: (0, 28 + h)),
                  pl.BlockSpec((T, HEAD_DIM), lambda h, i: (0, 32 + h)),
                  pl.BlockSpec((None, CHUNK, BAND), lambda h, i: (h, 0, 0))],
        out_specs=[pl.BlockSpec((R, HEAD_DIM), lambda h, i: (i, h))],
        out_shape=[jax.ShapeDtypeStruct((T, BRANCH_WIDTH), BF16)],
        scratch_shapes=[pltpu.VMEM((T + PAD_ROWS, HEAD_DIM), BF16), pltpu.VMEM((T + PAD_ROWS, HEAD_DIM), BF16)],
        args=(u_att, u_att, u_att, bias))


def _chunk_bwd(u_att, bias, do, *, name, carry=None):
    T = u_att.shape[0]
    R = _chunk_rows(T)
    nr = T // R
    H = N_HEADS

    def kern(q_ref, k_ref, v_ref, b_ref, do_ref, dq_ref, dk_ref, dv_ref, db_ref, kpad, vpad, dkp, dvp):
        i = pl.program_id(1)

        @pl.when(i == 0)
        def _():
            kpad[0:PAD_ROWS, :] = jnp.zeros((PAD_ROWS, HEAD_DIM), BF16)
            vpad[0:PAD_ROWS, :] = jnp.zeros((PAD_ROWS, HEAD_DIM), BF16)
            kpad[PAD_ROWS:, :] = k_ref[...]
            vpad[PAD_ROWS:, :] = v_ref[...]
            dkp[...] = jnp.zeros_like(dkp)
            dvp[...] = jnp.zeros_like(dvp)
            db_ref[...] = jnp.zeros_like(db_ref)

        bias_v = b_ref[...]
        for cc in range(R // CHUNK):
            cg = i * (R // CHUNK) + cc
            off = pl.multiple_of(cg * CHUNK, CHUNK)
            q = q_ref[cc * CHUNK:(cc + 1) * CHUNK, :]
            dov = do_ref[cc * CHUNK:(cc + 1) * CHUNK, :]
            kw = kpad[pl.ds(off, BAND), :]
            vw = vpad[pl.ds(off, BAND), :]
            p = _chunk_scores(q, kw, bias_v, cg)
            dp = _dot(dov, vw, NT)
            ds = p * (dp - jnp.sum(p * dp, axis=1, keepdims=True))
            dsb = ds.astype(BF16)
            dq_ref[cc * CHUNK:(cc + 1) * CHUNK, :] = (_dot(dsb, kw) * SCALE).astype(dq_ref.dtype)
            dkp[pl.ds(off, BAND), :] += _dot(dsb, q, TN)
            dvp[pl.ds(off, BAND), :] += _dot(p.astype(BF16), dov, TN)
            db_ref[...] += ds

        @pl.when(i == nr - 1)
        def _():
            dk_ref[...] = (dkp[PAD_ROWS:, :] * SCALE).astype(dk_ref.dtype)
            dv_ref[...] = dvp[PAD_ROWS:, :].astype(dv_ref.dtype)

    return _call_with_carry(
        kern, name=name, grid=(H, nr), carry=carry,
        in_specs=[pl.BlockSpec((R, HEAD_DIM), lambda h, i: (i, 24 + h)),
                  pl.BlockSpec((T, HEAD_DIM), lambda h, i: (0, 28 + h)),
                  pl.BlockSpec((T, HEAD_DIM), lambda h, i: (0, 32 + h)),
                  pl.BlockSpec((None, CHUNK, BAND), lambda h, i: (h, 0, 0)),
                  pl.BlockSpec((R, HEAD_DIM), lambda h, i: (i, h))],
        out_specs=[pl.BlockSpec((R, HEAD_DIM), lambda h, i: (i, h)),
                   pl.BlockSpec((T, HEAD_DIM), lambda h, i: (0, h)),
                   pl.BlockSpec((T, HEAD_DIM), lambda h, i: (0, h)),
                   pl.BlockSpec((None, CHUNK, BAND), lambda h, i: (h, 0, 0))],
        out_shape=[jax.ShapeDtypeStruct((T, BRANCH_WIDTH), BF16)] * 3
                  + [jax.ShapeDtypeStruct((H, CHUNK, BAND), F32)],
        scratch_shapes=[pltpu.VMEM((T + PAD_ROWS, HEAD_DIM), BF16), pltpu.VMEM((T + PAD_ROWS, HEAD_DIM), BF16),
                        pltpu.VMEM((T + PAD_ROWS, HEAD_DIM), F32), pltpu.VMEM((T + PAD_ROWS, HEAD_DIM), F32)],
        args=(u_att, u_att, u_att, bias, do))


LRU_ROWS = 512
HALO = 8


def _gelu(y):
    k0 = math.sqrt(2.0 / math.pi)
    t = jnp.tanh(k0 * (y + 0.044715 * y * y * y))
    return 0.5 * y * (1.0 + t), t


def _gelu_grad(y, t):
    k0 = math.sqrt(2.0 / math.pi)
    return 0.5 * (1.0 + t) + 0.5 * y * (1.0 - t * t) * k0 * (1.0 + 3.0 * 0.044715 * y * y)


def _neg_expm1(y):
    poly = -y * (1.0 + y * (1.0 / 2 + y * (1.0 / 6 + y * (1.0 / 24 + y * (1.0 / 120 + y * (1.0 / 720 + y * (1.0 / 5040)))))))
    return jnp.where(y > -0.5, poly, 1.0 - jnp.exp(y))


def _lru_gates(ext, cw_ref, cb_ref, wr_ref, br_ref, wi_ref, bi_ref, lam_ref, rows):
    xc = cb_ref[...] + jnp.zeros((rows, BRANCH_WIDTH), F32)
    for j in range(CONV_WIDTH):
        xc = xc + ext[pl.ds(HALO - (CONV_WIDTH - 1) + j, rows), :] * cw_ref[j:j + 1, :]
    xcb = xc.astype(BF16)
    zr = jnp.concatenate([_dot(xcb[:, n * 128:(n + 1) * 128], wr_ref[n]) for n in range(4)], axis=1) + br_ref[...]
    zi = jnp.concatenate([_dot(xcb[:, n * 128:(n + 1) * 128], wi_ref[n]) for n in range(4)], axis=1) + bi_ref[...]
    r = _sigmoid(zr)
    gi = _sigmoid(zi)
    ls = _log_sigmoid(lam_ref[...])
    la = LRU_C * r * ls
    a = jnp.exp(la)
    mult = jnp.sqrt(_neg_expm1(2.0 * la))
    return xc, xcb, r, gi, ls, a, mult


def _lru_param_specs():
    full2 = lambda s: pl.BlockSpec(s, lambda i: (0, 0))
    full3 = lambda s: pl.BlockSpec(s, lambda i: (0, 0, 0))
    return [full2((8, BRANCH_WIDTH)), full2((1, BRANCH_WIDTH)), full3((4, 128, 128)), full2((1, BRANCH_WIDTH)),
            full3((4, 128, 128)), full2((1, BRANCH_WIDTH)), full2((1, BRANCH_WIDTH))]


def _lru_fwd(u_rec, p, *, name):
    T = u_rec.shape[0]
    R = min(LRU_ROWS, T)
    nb = T // R
    W = BRANCH_WIDTH
    hb = R // HALO

    def kern(rx_ref, halo_ref, ry_ref, cw_ref, cb_ref, wr_ref, br_ref, wi_ref, bi_ref, lam_ref,
             o_ref, h_ref, ext, a_s, b_s, hc):
        i = pl.program_id(0)

        @pl.when(i == 0)
        def _():
            hc[...] = jnp.zeros_like(hc)

        ext[0:HALO, :] = jnp.where(i == 0, 0.0, halo_ref[...])
        ext[HALO:, :] = rx_ref[...]
        xc, _, r, gi, ls, a, mult = _lru_gates(ext, cw_ref, cb_ref, wr_ref, br_ref, wi_ref, bi_ref, lam_ref, R)
        a_s[...] = a
        b_s[...] = mult * (gi * xc)

        def body(t, h):
            h = a_s[pl.ds(t, 1), :] * h + b_s[pl.ds(t, 1), :]
            h_ref[pl.ds(t, 1), :] = h
            return h

        h = lax.fori_loop(0, R, body, hc[0:1, :], unroll=8)
        hc[...] = jnp.broadcast_to(h, hc.shape)
        g, _ = _gelu(ry_ref[...])
        o_ref[...] = (h_ref[...] * g).astype(o_ref.dtype)

    return pl.pallas_call(
        kern, name=name, grid=(nb,),
        in_specs=[pl.BlockSpec((R, W), lambda i: (i, 0)),
                  pl.BlockSpec((HALO, W), lambda i: (jnp.maximum(i * hb - 1, 0), 0)),
                  pl.BlockSpec((R, W), lambda i: (i, 1))] + _lru_param_specs(),
        out_specs=[pl.BlockSpec((R, W), lambda i: (i, 0)), pl.BlockSpec((R, W), lambda i: (i, 0))],
        out_shape=[jax.ShapeDtypeStruct((T, W), BF16), jax.ShapeDtypeStruct((T, W), F32)],
        scratch_shapes=[pltpu.VMEM((R + HALO, W), F32), pltpu.VMEM((R, W), F32), pltpu.VMEM((R, W), F32),
                        pltpu.VMEM((8, W), F32)],
        compiler_params=_cp(("arbitrary",)),
    )(u_rec, u_rec, u_rec, *p)


def _lru_bwd(u_rec, hs, do, p, *, name):
    T = u_rec.shape[0]
    R = min(LRU_ROWS, T)
    nb = T // R
    W = BRANCH_WIDTH
    hb = R // HALO

    def kern(rx_ref, halo_ref, ry_ref, h_ref, hh_ref, do_ref, cw_ref, cb_ref, wr_ref, br_ref, wi_ref, bi_ref, lam_ref,
             drx_ref, dry_ref, dcw_ref, dcb_ref, dwr_ref, dbr_ref, dwi_ref, dbi_ref, dlam_ref,
             ext, hext, a_s, g_s, dext, gc):
        s = pl.program_id(0)
        first_block = s == nb - 1

        @pl.when(s == 0)
        def _():
            gc[...] = jnp.zeros_like(gc)
            dext[R:, :] = jnp.zeros((HALO, W), F32)
            for ref in (dcw_ref, dcb_ref, dwr_ref, dbr_ref, dwi_ref, dbi_ref, dlam_ref):
                ref[...] = jnp.zeros_like(ref)

        ext[0:HALO, :] = jnp.where(first_block, 0.0, halo_ref[...])
        ext[HALO:, :] = rx_ref[...]
        hext[0:HALO, :] = jnp.where(first_block, 0.0, hh_ref[...])
        hext[HALO:, :] = h_ref[...]
        xc, xcb, r, gi, ls, a, mult = _lru_gates(ext, cw_ref, cb_ref, wr_ref, br_ref, wi_ref, bi_ref, lam_ref, R)
        ry = ry_ref[...]
        gel, th = _gelu(ry)
        dov = do_ref[...].astype(F32)
        dry_ref[...] = (dov * h_ref[...] * _gelu_grad(ry, th)).astype(dry_ref.dtype)
        a_s[...] = a
        g_s[...] = dov * gel

        def body(tt, g):
            t = R - 1 - tt
            dh = g_s[pl.ds(t, 1), :] + g
            g_s[pl.ds(t, 1), :] = dh
            return a_s[pl.ds(t, 1), :] * dh

        g = lax.fori_loop(0, R, body, gc[0:1, :], unroll=8)
        gc[...] = jnp.broadcast_to(g, gc.shape)
        dh = g_s[...]
        hprev = hext[pl.ds(HALO - 1, R), :]
        da = dh * hprev
        gx = gi * xc
        dmult = dh * gx
        dgx = dh * mult
        dgi = dgx * xc
        dxc = dgx * gi
        dla = da * a - dmult * (a * a) / mult
        dr = dla * (LRU_C * ls)
        dlam_ref[...] += jnp.sum(dla * (LRU_C * r), axis=0, keepdims=True)
        dzr = dr * r * (1.0 - r)
        dzi = dgi * gi * (1.0 - gi)
        dbr_ref[...] += jnp.sum(dzr, axis=0, keepdims=True)
        dbi_ref[...] += jnp.sum(dzi, axis=0, keepdims=True)
        dzrb = dzr.astype(BF16)
        dzib = dzi.astype(BF16)
        back = []
        for n in range(4):
            sl = slice(n * 128, (n + 1) * 128)
            dwr_ref[n] += _dot(xcb[:, sl], dzrb[:, sl], TN)
            dwi_ref[n] += _dot(xcb[:, sl], dzib[:, sl], TN)
            back.append(_dot(dzrb[:, sl], wr_ref[n], NT) + _dot(dzib[:, sl], wi_ref[n], NT))
        dxc = dxc + jnp.concatenate(back, axis=1)
        dcb_ref[...] += jnp.sum(dxc, axis=0, keepdims=True)
        for j in range(CONV_WIDTH):
            dcw_ref[j:j + 1, :] += jnp.sum(dxc * ext[pl.ds(HALO - (CONV_WIDTH - 1) + j, R), :], axis=0, keepdims=True)
        dext[0:R, :] = dxc
        drx = jnp.zeros((R, W), F32)
        for j in range(CONV_WIDTH):
            drx = drx + dext[pl.ds(CONV_WIDTH - 1 - j, R), :] * cw_ref[j:j + 1, :]
        drx_ref[...] = drx.astype(drx_ref.dtype)
        dext[R:, :] = dxc[0:HALO, :]

        @pl.when(s == nb - 1)
        def _():
            dlam_ref[...] = dlam_ref[...] * _sigmoid(-lam_ref[...])

    rev = lambda c: pl.BlockSpec((R, W), lambda s: (nb - 1 - s, c))
    halo = lambda: pl.BlockSpec((HALO, W), lambda s: (jnp.maximum((nb - 1 - s) * hb - 1, 0), 0))
    v2 = lambda shp: pl.BlockSpec(shp, lambda s: (0, 0))
    v3 = lambda shp: pl.BlockSpec(shp, lambda s: (0, 0, 0))
    return pl.pallas_call(
        kern, name=name, grid=(nb,),
        in_specs=[rev(0), halo(), rev(1), rev(0), halo(), rev(0)] + _lru_param_specs(),
        out_specs=[rev(0), rev(0), v2((8, W)), v2((1, W)), v3((4, 128, 128)), v2((1, W)), v3((4, 128, 128)),
                   v2((1, W)), v2((1, W))],
        out_shape=[jax.ShapeDtypeStruct((T, W), BF16), jax.ShapeDtypeStruct((T, W), BF16),
                   jax.ShapeDtypeStruct((8, W), F32), jax.ShapeDtypeStruct((1, W), F32),
                   jax.ShapeDtypeStruct((4, 128, 128), F32), jax.ShapeDtypeStruct((1, W), F32),
                   jax.ShapeDtypeStruct((4, 128, 128), F32), jax.ShapeDtypeStruct((1, W), F32),
                   jax.ShapeDtypeStruct((1, W), F32)],
        scratch_shapes=[pltpu.VMEM((R + HALO, W), F32), pltpu.VMEM((R + HALO, W), F32), pltpu.VMEM((R, W), F32),
                        pltpu.VMEM((R, W), F32), pltpu.VMEM((R + HALO, W), F32), pltpu.VMEM((8, W), F32)],
        compiler_params=_cp(("arbitrary",)),
    )(u_rec, u_rec, u_rec, hs, hs, do, *p)


def _merge_fwd(o_all, wb, gate, *, name):
    T = o_all.shape[1]
    D = D_MODEL
    bm = _pick(T, (1024, 512, 256, 128))
    bn = 1024
    nj = D // bn

    def kern(o_ref, w_ref, g_ref, m_ref, pb_ref, acc):
        g = pl.program_id(2)
        pbv = _dot(o_ref[...], w_ref[...])
        pb_ref[...] = pbv.astype(pb_ref.dtype)
        term = g_ref[...].astype(F32) * pbv

        @pl.when(g == 0)
        def _():
            acc[...] = term

        @pl.when(g > 0)
        def _():
            acc[...] += term

        @pl.when(g == N_BRANCH - 1)
        def _():
            m_ref[...] = acc[...].astype(m_ref.dtype)

    return pl.pallas_call(
        kern, name=name, grid=(T // bm, nj, N_BRANCH),
        in_specs=[pl.BlockSpec((None, bm, BRANCH_WIDTH), lambda i, j, g: (g, i, 0)),
                  pl.BlockSpec((None, BRANCH_WIDTH, bn), lambda i, j, g: (g, 0, j)),
                  pl.BlockSpec((bm, bn), lambda i, j, g: (i, g * nj + j))],
        out_specs=[pl.BlockSpec((bm, bn), lambda i, j, g: (i, j)),
                   pl.BlockSpec((bm, bn), lambda i, j, g: (i, g * nj + j))],
        out_shape=[jax.ShapeDtypeStruct((T, D), BF16), jax.ShapeDtypeStruct((T, N_BRANCH * D), BF16)],
        scratch_shapes=[pltpu.VMEM((bm, bn), F32)],
        compiler_params=_cp(("parallel", "parallel", "arbitrary")),
    )(o_all, wb, gate)


def _merge_bwd(dm, gate, pb, *, name):
    T = dm.shape[0]
    D = D_MODEL
    bt = _pick(T, (256, 128))

    def kern(dm_ref, g_ref, pb_ref, dpb_ref, dzg_ref, dbg_ref):
        i = pl.program_id(1)
        dmv = dm_ref[...]
        gv = g_ref[...].astype(F32)
        dpb_ref[...] = (dmv * gv).astype(dpb_ref.dtype)
        dzg = dmv * pb_ref[...].astype(F32) * gv * (1.0 - gv)
        dzg_ref[...] = dzg.astype(dzg_ref.dtype)
        part = jnp.sum(dzg, axis=0, keepdims=True)

        @pl.when(i == 0)
        def _():
            dbg_ref[...] = part

        @pl.when(i > 0)
        def _():
            dbg_ref[...] += part

    return pl.pallas_call(
        kern, name=name, grid=(N_BRANCH, T // bt),
        in_specs=[pl.BlockSpec((bt, D), lambda g, i: (i, 0)),
                  pl.BlockSpec((bt, D), lambda g, i: (i, g)),
                  pl.BlockSpec((bt, D), lambda g, i: (i, g))],
        out_specs=[pl.BlockSpec((None, bt, D), lambda g, i: (g, i, 0)),
                   pl.BlockSpec((bt, D), lambda g, i: (i, g)),
                   pl.BlockSpec((1, D), lambda g, i: (0, g))],
        out_shape=[jax.ShapeDtypeStruct((N_BRANCH, T, D), BF16), jax.ShapeDtypeStruct((T, N_BRANCH * D), BF16),
                   jax.ShapeDtypeStruct((1, N_BRANCH * D), F32)],
        compiler_params=_cp(("parallel", "arbitrary")),
    )(dm, gate, pb)


def _col_split(M, N, bm, bn):
    per = N // N_CHIPS // bn
    return (N_CHIPS, M, N // N_CHIPS), (None, bm, bn), lambda i, j: (j // per, i, j % per)


def _pad_lanes(v, n):
    return jnp.pad(v, [(0, 0)] * (v.ndim - 1) + [(0, n - v.shape[-1])])


def _rows8(v):
    return jnp.pad(v, ((0, 8 - v.shape[0]), (0, 0)))


def _device_step(x, tgt, W, hooks=None):
    hooks = hooks or {}

    def carried(fn, key, *args, **kw):
        hook = hooks.get(key)
        outs, extra = fn(*args, name=key[0], carry=hook.spec(W, G) if hook else None, **kw)
        if hook:
            hook.done(extra, W, G)
        return outs

    def mm_carried(key, *args, **kw):
        hook = hooks.get(key)
        if hook is None:
            return _mm(*args, name=key[0], **kw)
        res, extra = _mm(*args, name=key[0], carry=hook.spec(W, G), **kw)
        hook.done(extra, W, G)
        return res

    T = x.shape[0]
    _, bk = _att_blocks(T)
    H = N_HEADS
    G = {}
    saved = []

    xf, xb = _ln_fwd(x, W['ln_in_g'], W['ln_in_b'], name='ln_in_fwd')
    for l in range(DEPTH):
        w_att, w_rec = W['w_att'][l], W['w_rec'][l]
        u_att = mm_carried(('in_proj_att', l), xb, w_att, out_dtypes=(BF16,))
        u_rec = _mm(xb, w_rec, name='in_proj_rec', out_dtypes=(F32,))
        ffl = u_rec[:, 2 * BRANCH_WIDTH:]
        bf = _pad_lanes(W['b_forget'][l].reshape(1, H), LANES)
        Fc = _forget_fwd(ffl, bf, name='forget_fwd')
        Fh = Fc[:, :H].T
        frow = Fh.reshape(H, T // bk, 1, bk)
        o_fox, lse = carried(_fox_fwd, ('fox_fwd', l), u_att, Fc, frow)
        lp = (_rows8(W['conv_w'][l]), W['conv_b'][l].reshape(1, -1), W['w_r'][l].astype(BF16),
              W['b_r'][l].reshape(1, -1), W['w_i'][l].astype(BF16), W['b_i'][l].reshape(1, -1),
              W['lru_lambda'][l].reshape(1, -1))
        o_lru, hs = _lru_fwd(u_rec, lp, name='lru_fwd')
        o_sb, = carried(_sb_fwd, ('sb_fwd', l), u_att)
        table = _rows8(_pad_lanes(W['rel_bias'][l], REL_PAD))
        bias = _band_bias(table, name='band_bias').transpose(1, 0, 2)[:H]
        o_ch, = carried(_chunk_fwd, ('chunk_fwd', l), u_att, bias)
        o_all = jnp.stack([o_fox, o_lru, o_sb, o_ch])
        gate = mm_carried(('gate_proj', l), xb, W['w_gate_cat'][l], out_dtypes=(BF16,),
                          extras=[(W['b_gate'][l].reshape(1, -1), 'n')],
                          epilogue=lambda acc, b: (_sigmoid(acc + b),))
        merged, pb = _merge_fwd(o_all, W['w_branch'][l], gate, name='merge_fwd')
        h1 = _mm(merged, W['w_out'][l], name='out_proj', extras=[(xf, 'mn')],
                 epilogue=lambda acc, xr: (ALPHA * xr + acc,))
        xmf, xmb = _ln_fwd(h1, W['ln1_g'][l], W['ln1_b'][l], name='ln_fwd')
        hid, ra = _mm(xmb, W['w_ff1'], name='ff1', out_dtypes=(BF16, BF16), bn=1024, bk=FF_SHARD,
                      b_view=(D_MODEL, D_FF, (None, None, FF_SHARD, 1024), lambda i, j, k: (j // 2, l, 0, j % 2)),
                      epilogue=lambda acc: (jnp.square(jnp.maximum(acc, 0.0)), jnp.maximum(acc, 0.0)))
        h2 = _mm(hid, W['w_ff2'], name='ff2', extras=[(xmf, 'mn')], bn=1024, bk=FF_SHARD,
                 b_view=(D_FF, D_MODEL, (None, None, FF_SHARD, 1024), lambda i, j, k: (k, l, 0, j)),
                 epilogue=lambda acc, xr: (ALPHA * xr + acc,))
        saved.append(dict(xb=xb, u_att=u_att, u_rec=u_rec, ffl=ffl, bf=bf, fcum=Fc, frow=frow, lse=lse, lp=lp,
                          hs=hs, bias=bias, o_all=o_all, gate=gate, merged=merged, pb=pb, h1=h1, xmb=xmb,
                          hid=hid, ra=ra, h2=h2))
        xf, xb = _ln_fwd(h2, W['ln2_g'][l], W['ln2_b'][l], name='ln_fwd')

    dx, loss_tile = _loss_head(xf, tgt, name='loss_head')
    loss = loss_tile[0, 0]

    for l in reversed(range(DEPTH)):
        S = saved[l]
        dh2, dh2b, G[('ln2_g', l)], G[('ln2_b', l)] = _ln_bwd(S['h2'], dx, W['ln2_g'][l], name='ln_bwd')
        da = _mm(dh2b, W['w_ff2'], tb=True, name='ff2_dx', out_dtypes=(BF16,), extras=[(S['ra'], 'mn')],
                 bn=1024, bk=FF_SHARD,
                 b_view=(D_MODEL, D_FF, (None, None, 1024, FF_SHARD), lambda i, j, k: (j // 2, l, j % 2, 0)),
                 epilogue=lambda acc, rav: (acc * (2.0 * rav.astype(F32)),))
        G[('w_ff2', l)] = _mm(S['hid'], dh2b, ta=True, name='ff2_dw').reshape(N_CHIPS, D_FF // N_CHIPS, D_MODEL)
        G[('w_ff1', l)] = _mm(S['xmb'], da, ta=True, name='ff1_dw', bm=1024, bn=1024,
                              out_map=_col_split(D_MODEL, D_FF, 1024, 1024))
        dxm = _mm(da, W['w_ff1'], tb=True, name='ff1_dx', extras=[(dh2, 'mn')], bn=1024, bk=FF_SHARD,
                  b_view=(D_FF, D_MODEL, (None, None, 1024, FF_SHARD), lambda i, j, k: (k, l, j, 0)),
                  epilogue=lambda acc, d: (ALPHA * d + acc,))
        dh1, dh1b, G[('ln1_g', l)], G[('ln1_b', l)] = _ln_bwd(S['h1'], dxm, W['ln1_g'][l], name='ln_bwd')
        dm = _mm(dh1b, W['w_out'][l], tb=True, name='out_dx')
        G[('w_out', l)] = _mm(S['merged'], dh1b, ta=True, name='out_dw').reshape(
            N_CHIPS, D_MODEL // N_CHIPS, D_MODEL)
        dpb, dzg, G[('b_gate', l)] = _merge_bwd(dm, S['gate'], S['pb'], name='merge_bwd')
        do = [_mm(dpb[g], W['w_branch'][l][g], tb=True, name='branch_dx', out_dtypes=(BF16,)) for g in range(N_BRANCH)]
        G[('w_branch', l)] = jnp.stack(
            [_mm(S['o_all'][g], dpb[g], ta=True, name='branch_dw', bm=BRANCH_WIDTH, bn=BRANCH_WIDTH,
                 out_map=_col_split(BRANCH_WIDTH, D_MODEL, BRANCH_WIDTH, BRANCH_WIDTH))
             for g in range(N_BRANCH)], axis=1)
        G[('w_gate', l)] = _mm(S['xb'], dzg, ta=True, name='gate_dw', bm=1024, bn=1024,
                               out_map=((N_CHIPS, N_BRANCH, D_MODEL // N_CHIPS, D_MODEL),
                                        (2, None, D_MODEL // N_CHIPS, 1024),
                                        lambda i, j: (i, j // 2, 0, j % 2)))
        u_att, u_rec = S['u_att'], S['u_rec']
        delta = _row_dot(do[0], S['o_all'][0], name='row_dot')
        fdq, fdk, fdv, dfk, dfq = carried(_fox_bwd, ('fox_bwd', l), u_att, do[0], S['lse'], delta, S['fcum'],
                                          S['frow'])
        dff, dbf = _forget_bwd(_pad_lanes(dfk.reshape(H, T).T, LANES), _pad_lanes(dfq.reshape(H, T).T, LANES),
                               S['ffl'], S['bf'], name='forget_bwd')
        G[('b_forget', l)] = dbf[0, :H]
        (drx, dry, dcw, dcb, G[('w_r', l)], dbr, G[('w_i', l)], dbi, dlam) = _lru_bwd(
            u_rec, S['hs'], do[1], S['lp'], name='lru_bwd')
        G[('conv_w', l)], G[('conv_b', l)] = dcw[:CONV_WIDTH], dcb[0]
        G[('b_r', l)], G[('b_i', l)], G[('lru_lambda', l)] = dbr[0], dbi[0], dlam[0]
        sdq, sdk, sdv = carried(_sb_bwd, ('sb_bwd', l), u_att, do[2])
        cdq, cdk, cdv, dbias = carried(_chunk_bwd, ('chunk_bwd', l), u_att, S['bias'], do[3])
        dtab = _band_bias_bwd(jnp.pad(dbias, ((0, 8 - H), (0, 0), (0, 0))).transpose(1, 0, 2), name='band_bias_bwd')
        G[('rel_bias', l)] = dtab[:H, :REL_TABLE]
        du_att = jnp.concatenate([fdq, fdk, fdv, sdq, sdk, sdv, cdq, cdk, cdv], axis=1)
        du_rec = jnp.concatenate([drx, dry, dff], axis=1)
        G[('w_att', l)] = _mm(S['xb'], du_att, ta=True, name='in_att_dw')
        G[('w_rec', l)] = _mm(S['xb'], du_rec, ta=True, name='in_rec_dw')
        t1 = mm_carried(('gate_dx', l), dzg, W['w_gate_cat'][l], tb=True, extras=[(dh1, 'mn')],
                        epilogue=lambda acc, d: (ALPHA * d + acc,))
        t2 = mm_carried(('in_att_dx', l), du_att, W['w_att'][l], tb=True, extras=[(t1, 'mn')],
                        epilogue=lambda acc, d: (d + acc,))
        dx = _mm(du_rec, W['w_rec'][l], tb=True, name='in_rec_dx', extras=[(t2, 'mn')],
                 epilogue=lambda acc, d: (d + acc,))

    gx, _, G[('ln_in_g', -1)], G[('ln_in_b', -1)] = _ln_bwd(x, dx, W['ln_in_g'], name='ln_in_bwd')
    return loss, gx, G


_IN_FQKV = (0, 1536)
_IN_FF = (1536, 1540)
_IN_REC = (1540, 2564)
_IN_REST = (2564, D_IN)


def _prep_weights(full, W=None):
    W = {} if W is None else W
    for n, a in full.items():
        if n == 'w_in':
            L = a.shape[0]
            W['w_att'] = jnp.concatenate([a[..., _IN_FQKV[0]:_IN_FQKV[1]], a[..., _IN_REST[0]:_IN_REST[1]]],
                                         -1).astype(BF16)
            W['w_rec'] = jnp.concatenate([a[..., _IN_REC[0]:_IN_REC[1]], a[..., _IN_FF[0]:_IN_FF[1]],
                                          jnp.zeros((L, D_MODEL, N_REC - 1024 - N_HEADS), a.dtype)], -1).astype(BF16)
        elif n == 'w_gate':
            W['w_gate_cat'] = a.transpose(0, 2, 1, 3).reshape(a.shape[0], D_MODEL, N_BRANCH * D_MODEL).astype(BF16)
        elif n == 'b_gate':
            W['b_gate'] = a.reshape(a.shape[0], N_BRANCH * D_MODEL)
        elif n == 'w_ff1' and a.ndim == 3:
            W[n] = a.reshape(a.shape[0], D_MODEL, 4, FF_SHARD).transpose(2, 0, 1, 3).astype(BF16)
        elif n == 'w_ff2' and a.ndim == 3:
            W[n] = a.reshape(a.shape[0], 4, FF_SHARD, D_MODEL).transpose(1, 0, 2, 3).astype(BF16)
        elif n in ('w_branch', 'w_out', 'w_ff1', 'w_ff2'):
            W[n] = a.astype(BF16)
        else:
            W[n] = a
    return W


def _grads_to_reference_layout(G):
    out = {'ln_in_g': G[('ln_in_g', -1)][0], 'ln_in_b': G[('ln_in_b', -1)][0]}
    st = lambda n: jnp.stack([G[(n, l)] for l in range(DEPTH)])
    g_att, g_rec = st('w_att'), st('w_rec')
    out['w_in'] = jnp.concatenate([g_att[..., :1536], g_rec[..., 1024:1024 + N_HEADS], g_rec[..., :1024],
                                   g_att[..., 1536:]], -1)
    out['w_gate'] = st('w_gate').transpose(0, 2, 1, 3, 4).reshape(DEPTH, N_BRANCH, D_MODEL, D_MODEL)
    out['w_branch'] = st('w_branch').transpose(0, 2, 3, 1, 4).reshape(DEPTH, N_BRANCH, BRANCH_WIDTH, D_MODEL)
    out['w_ff1'] = st('w_ff1').transpose(0, 2, 1, 3).reshape(DEPTH, D_MODEL, D_FF)
    out['w_ff2'] = st('w_ff2').reshape(DEPTH, D_FF, D_MODEL)
    out['w_out'] = st('w_out').reshape(DEPTH, D_MODEL, D_MODEL)
    out['b_gate'] = st('b_gate').reshape(DEPTH, N_BRANCH, D_MODEL)
    for n in ('ln1_g', 'ln1_b', 'ln2_g', 'ln2_b'):
        out[n] = st(n)[:, 0]
    for n in ('b_forget', 'conv_w', 'conv_b', 'w_r', 'b_r', 'w_i', 'b_i', 'lru_lambda', 'rel_bias'):
        out[n] = st(n)
    return out


HBM_SPEC = pl.BlockSpec(memory_space=pl.ANY)
N_CHIPS = 4
PACK_COLS = 1024


def _place():
    x, y, c = lax.axis_index("x"), lax.axis_index("y"), lax.axis_index("c")
    chips = [(1 - x, y), (x, 1 - y), (1 - x, 1 - y)]
    return x, y, c, chips


def _remote(src, dst, send_sems, recv_sems, k, to):
    return pltpu.make_async_remote_copy(src_ref=src, dst_ref=dst, send_sem=send_sems.at[k], recv_sem=recv_sems.at[k],
                                        device_id=to, device_id_type=MESH)


class _Exchange:
    def __init__(self, ins, out_shapes, n_sems, start, finish, mid=None):
        self.ins, self.out_shapes, self.n_sems = list(ins), list(out_shapes), n_sems
        self.start, self.mid, self.finish = start, mid, finish


def _gather_spec(params):
    n = len(params)

    def start(ins, outs, ss, rs):
        x, y, c, chips = _place()
        for p in range(n):
            _remote(ins[p], outs[p].at[2 * x + y], ss, rs, 6 * n + p, (x, y, 1 - c)).start()
            for j, (cx, cy) in enumerate(chips):
                _remote(ins[p].at[c], outs[p].at[2 * x + y, c], ss, rs, 6 * p + j, (cx, cy, c)).start()

    def mid(ins, outs, ss, rs):
        x, y, c, chips = _place()
        for p in range(n):
            for j, (cx, cy) in enumerate(chips):
                blk = outs[p].at[2 * cx + cy, c]
                _remote(blk, blk, ss, rs, 6 * p + j, (x, y, c)).wait_recv()
                _remote(blk, blk, ss, rs, 6 * p + 3 + j, (x, y, 1 - c)).start()

    def finish(ins, outs, ss, rs):
        x, y, c, chips = _place()
        me = (x, y, c)
        for p in range(n):
            for j, (cx, cy) in enumerate(chips):
                theirs = outs[p].at[2 * cx + cy, 1 - c]
                _remote(theirs, theirs, ss, rs, 6 * p + 3 + j, me).wait_recv()
        for p in range(n):
            for j, (cx, cy) in enumerate(chips):
                _remote(ins[p].at[c], outs[p].at[2 * x + y, c], ss, rs, 6 * p + j, me).wait_send()
                blk = outs[p].at[2 * cx + cy, c]
                _remote(blk, blk, ss, rs, 6 * p + 3 + j, me).wait_send()
            _remote(ins[p], outs[p].at[2 * x + y], ss, rs, 6 * n + p, me).wait()

    shapes = [jax.ShapeDtypeStruct((N_CHIPS,) + a.shape, a.dtype) for a in params]
    return _Exchange(params, shapes, 7 * n, start, finish, mid)


def _pair_spec(g0, g1):
    n = len(g0)

    def start(ins, outs, ss, rs):
        x, y, c, _ = _place()

        @pl.when(c == 0)
        def _():
            for p in range(n):
                _remote(ins[n + p], outs[p], ss, rs, p, (x, y, 1 - c)).start()

        @pl.when(c == 1)
        def _():
            for p in range(n):
                _remote(ins[p], outs[p], ss, rs, p, (x, y, 1 - c)).start()

    def finish(ins, outs, ss, rs):
        x, y, c, _ = _place()
        for p in range(n):
            _remote(ins[p], outs[p], ss, rs, p, (x, y, 1 - c)).wait()

    return _Exchange(list(g0) + list(g1), [jax.ShapeDtypeStruct(a.shape, a.dtype) for a in g0], n, start, finish)


def _chip_spec(s):
    n = len(s)

    def start(ins, outs, ss, rs):
        x, y, c, chips = _place()
        for p in range(n):
            for j, (cx, cy) in enumerate(chips):
                _remote(ins[p].at[2 * cx + cy], outs[p].at[2 * x + y], ss, rs, 3 * p + j, (cx, cy, c)).start()

    def finish(ins, outs, ss, rs):
        x, y, c, chips = _place()
        for p in range(n):
            for j, (cx, cy) in enumerate(chips):
                slot = outs[p].at[2 * cx + cy]
                _remote(slot, slot, ss, rs, 3 * p + j, (x, y, c)).wait_recv()
        for p in range(n):
            for j, (cx, cy) in enumerate(chips):
                _remote(ins[p].at[2 * cx + cy], outs[p].at[2 * x + y], ss, rs, 3 * p + j, (x, y, c)).wait_send()

    return _Exchange(s, [jax.ShapeDtypeStruct(a.shape, a.dtype) for a in s], 3 * n, start, finish)


def _exchange(ex, *, name):
    ni, no = len(ex.ins), len(ex.out_shapes)

    def body(*refs):
        ins, outs = refs[:ni], refs[ni:ni + no]
        ss, rs = refs[ni + no:]
        ex.start(ins, outs, ss, rs)
        if ex.mid is not None:
            ex.mid(ins, outs, ss, rs)
        ex.finish(ins, outs, ss, rs)

    return list(pl.pallas_call(
        body, name=name, in_specs=[HBM_SPEC] * ni, out_specs=[HBM_SPEC] * no, out_shape=ex.out_shapes,
        scratch_shapes=[pltpu.SemaphoreType.DMA((ex.n_sems,)), pltpu.SemaphoreType.DMA((ex.n_sems,))],
    )(*ex.ins))


def _call_with_carry(kern, *, name, grid, in_specs, out_specs, out_shape, scratch_shapes, args, carry=None,
                     semantics=("parallel", "arbitrary")):
    out_specs, out_shape = list(out_specs), list(out_shape)
    if carry is None:
        res = pl.pallas_call(kern, name=name, grid=grid, in_specs=in_specs, out_specs=out_specs, out_shape=out_shape,
                             scratch_shapes=scratch_shapes, compiler_params=_cp(semantics))(*args)
        return list(res), []
    ni, no, ns = len(in_specs), len(out_specs), len(scratch_shapes)
    ci, co = len(carry.ins), len(carry.out_shapes)

    def wrapped(*refs):
        ins, cins = refs[:ni], refs[ni:ni + ci]
        outs, couts = refs[ni + ci:ni + ci + no], refs[ni + ci + no:ni + ci + no + co]
        scratch = refs[ni + ci + no + co:ni + ci + no + co + ns]
        ss, rs = refs[-2:]
        ids = [pl.program_id(d) for d in range(len(grid))]
        at = lambda pos: functools.reduce(lambda p, q: p & q, [i == v for i, v in zip(ids, pos)])

        @pl.when(at([0] * len(grid)))
        def _():
            carry.start(cins, couts, ss, rs)

        kern(*ins, *outs, *scratch)

        if carry.mid is not None:
            @pl.when(at([grid[0] - 1, grid[1] // 2] + [0] * (len(grid) - 2)))
            def _():
                carry.mid(cins, couts, ss, rs)

        @pl.when(at([g - 1 for g in grid]))
        def _():
            carry.finish(cins, couts, ss, rs)

    res = pl.pallas_call(
        wrapped, name=name, grid=grid, in_specs=list(in_specs) + [HBM_SPEC] * ci,
        out_specs=out_specs + [HBM_SPEC] * co, out_shape=out_shape + carry.out_shapes,
        scratch_shapes=list(scratch_shapes) + [pltpu.SemaphoreType.DMA((carry.n_sems,)),
                                               pltpu.SemaphoreType.DMA((carry.n_sems,))],
        compiler_params=_cp(("arbitrary",) * len(grid)))(*args, *carry.ins)
    return list(res[:no]), list(res[no:])


def _pair_swap(r, *, name):
    n = len(r)

    def body(*refs):
        ins, outs = refs[:n], refs[n:2 * n]
        send_sems, recv_sems = refs[2 * n:]
        x, y, c, _ = _place()
        cps = [_remote(ins[p], outs[p], send_sems, recv_sems, p, (x, y, 1 - c)) for p in range(n)]
        for cp in cps:
            cp.start()
        for cp in cps:
            cp.wait()

    return pl.pallas_call(
        body, name=name, in_specs=[HBM_SPEC] * n, out_specs=[HBM_SPEC] * n,
        out_shape=[jax.ShapeDtypeStruct(a.shape, a.dtype) for a in r],
        scratch_shapes=[pltpu.SemaphoreType.DMA((n,)), pltpu.SemaphoreType.DMA((n,))],
    )(*r)


def _gather8_spec(v):
    R, C = v.shape
    flips = [(bx, by, bc) for bx in (0, 1) for by in (0, 1) for bc in (0, 1)][1:]
    flip = lambda a_, b_: 1 - a_ if b_ else a_

    def start(ins, outs, ss, rs):
        x, y, c, _ = _place()
        mine = outs[0].at[4 * x + 2 * y + c]
        pltpu.make_async_copy(ins[0], mine, ss.at[7]).start()
        for j, (bx, by, bc) in enumerate(flips):
            _remote(ins[0], mine, ss, rs, j, (flip(x, bx), flip(y, by), flip(c, bc))).start()

    def finish(ins, outs, ss, rs):
        x, y, c, _ = _place()
        mine = outs[0].at[4 * x + 2 * y + c]
        for j, (bx, by, bc) in enumerate(flips):
            slot = outs[0].at[4 * flip(x, bx) + 2 * flip(y, by) + flip(c, bc)]
            _remote(slot, slot, ss, rs, j, (x, y, c)).wait_recv()
        for j in range(7):
            _remote(ins[0], mine, ss, rs, j, (x, y, c)).wait_send()
        pltpu.make_async_copy(ins[0], mine, ss.at[7]).wait()

    return _Exchange([v], [jax.ShapeDtypeStruct((8, R, C), v.dtype)], 8, start, finish)


def _row_block(rows, cols, limit=256 * 1024):
    if rows * cols <= limit:
        return rows
    for br in range(limit // cols // 8 * 8, 0, -8):
        if rows % br == 0:
            return br
    return rows


def _sum_slots(buf, *, name):
    n, R, C = buf.shape
    br = _row_block(R, C, limit=64 * 1024)

    def kern(b_ref, o_ref):
        acc = b_ref[0].astype(F32)
        for s in range(1, n):
            acc = acc + b_ref[s].astype(F32)
        o_ref[...] = acc

    return pl.pallas_call(
        kern, name=name, grid=(pl.cdiv(R, br),),
        in_specs=[pl.BlockSpec((n, br, C), lambda i: (0, i, 0))],
        out_specs=pl.BlockSpec((br, C), lambda i: (i, 0)),
        out_shape=jax.ShapeDtypeStruct((R, C), F32),
        compiler_params=_cp(("arbitrary",)),
    )(buf)


def _scalar(s):
    return jnp.reshape(s, (1,)).astype(jnp.int32)


def _sum_pair(g0, g1, other, c, *, name):
    _, R, C = g0.shape
    br = _row_block(R, C)

    def kern(c_ref, g0_ref, g1_ref, o_ref, out_ref):
        own = jnp.where(c_ref[0] == 0, g0_ref[...], g1_ref[...])
        out_ref[...] = (own + o_ref[...]).astype(out_ref.dtype)

    blk = (None, br, C)
    return pl.pallas_call(
        kern, name=name,
        grid_spec=pltpu.PrefetchScalarGridSpec(
            num_scalar_prefetch=1, grid=(N_CHIPS, R // br),
            in_specs=[pl.BlockSpec(blk, lambda k, i, cr: (k, i * (1 - cr[0]), 0)),
                      pl.BlockSpec(blk, lambda k, i, cr: (k, i * cr[0], 0)),
                      pl.BlockSpec(blk, lambda k, i, cr: (k, i, 0))],
            out_specs=pl.BlockSpec(blk, lambda k, i, cr: (k, i, 0))),
        out_shape=jax.ShapeDtypeStruct((N_CHIPS, R, C), BF16),
        compiler_params=_cp(("arbitrary", "arbitrary")),
    )(_scalar(c), g0, g1, other)


def _sum_chips(s, got, k, *, name):
    _, R, C = s.shape
    br = _row_block(R, C)

    def kern(k_ref, s_ref, a_ref, b_ref, c_ref, out_ref):
        out_ref[...] = ((s_ref[...].astype(F32) + a_ref[...].astype(F32)) + b_ref[...].astype(F32)) \
            + c_ref[...].astype(F32)

    blk = (None, br, C)
    peer = lambda d: pl.BlockSpec(blk, lambda i, kr: ((kr[0] + d) % N_CHIPS, i, 0))
    return pl.pallas_call(
        kern, name=name,
        grid_spec=pltpu.PrefetchScalarGridSpec(
            num_scalar_prefetch=1, grid=(R // br,),
            in_specs=[peer(0), peer(1), peer(2), peer(3)],
            out_specs=pl.BlockSpec((br, C), lambda i, kr: (i, 0))),
        out_shape=jax.ShapeDtypeStruct((R, C), F32),
        compiler_params=_cp(("arbitrary",)),
    )(_scalar(k), s, got, got, got)


def _adam_math(w, g, m, v):
    nm = ADAM_B1 * m + (1.0 - ADAM_B1) * g
    nv = ADAM_B2 * v + (1.0 - ADAM_B2) * jnp.square(g)
    m_hat = nm / (1.0 - ADAM_B1 ** ADAM_STEP)
    v_hat = nv / (1.0 - ADAM_B2 ** ADAM_STEP)
    return -ADAM_LR * (m_hat / (jnp.sqrt(v_hat) + ADAM_EPS) + ADAM_WD * w), nm, nv


def _adamw_layers(w, mine, theirs, m, v, c, *, name):
    shape = w.shape
    R, C = mine.shape
    w3, m3, v3 = (a.reshape(DEPTH, R, C) for a in (w, m, v))
    br = _row_block(R, C)

    def kern(c_ref, w_ref, a_ref, b_ref, m_ref, v_ref, g_ref, d_ref, nm_ref, nv_ref):
        g = jnp.where(pl.program_id(0) == c_ref[0], a_ref[...], b_ref[...])
        g_ref[...] = g
        d_ref[...], nm_ref[...], nv_ref[...] = _adam_math(w_ref[...], g, m_ref[...], v_ref[...])

    lay = pl.BlockSpec((None, br, C), lambda l, i, cr: (l, i, 0))
    outs = pl.pallas_call(
        kern, name=name,
        grid_spec=pltpu.PrefetchScalarGridSpec(
            num_scalar_prefetch=1, grid=(DEPTH, R // br),
            in_specs=[lay,
                      pl.BlockSpec((br, C), lambda l, i, cr: (jnp.where(l == cr[0], i, 0), 0)),
                      pl.BlockSpec((br, C), lambda l, i, cr: (jnp.where(l == cr[0], 0, i), 0)),
                      lay, lay],
            out_specs=[lay] * 4),
        out_shape=[jax.ShapeDtypeStruct((DEPTH, R, C), F32)] * 4,
        compiler_params=_cp(("arbitrary", "arbitrary")),
    )(_scalar(c), w3, mine, theirs, m3, v3)
    return [o.reshape(shape) for o in outs]


def _adamw(w, g, m, v, *, name):
    shape = w.shape
    cols = shape[-1]
    w2, g2, m2, v2 = (a.reshape(-1, cols) for a in (w, g, m, v))
    rows = w2.shape[0]
    br = _row_block(rows, cols)

    def kern(w_ref, g_ref, m_ref, v_ref, d_ref, nm_ref, nv_ref):
        gv = g_ref[...]
        nm = ADAM_B1 * m_ref[...] + (1.0 - ADAM_B1) * gv
        nv = ADAM_B2 * v_ref[...] + (1.0 - ADAM_B2) * jnp.square(gv)
        m_hat = nm / (1.0 - ADAM_B1 ** ADAM_STEP)
        v_hat = nv / (1.0 - ADAM_B2 ** ADAM_STEP)
        d_ref[...] = -ADAM_LR * (m_hat / (jnp.sqrt(v_hat) + ADAM_EPS) + ADAM_WD * w_ref[...])
        nm_ref[...] = nm
        nv_ref[...] = nv

    spec = pl.BlockSpec((br, cols), lambda i: (i, 0))
    outs = pl.pallas_call(
        kern, name=name, grid=(rows // br,), in_specs=[spec] * 4, out_specs=[spec] * 3,
        out_shape=[jax.ShapeDtypeStruct((rows, cols), F32)] * 3,
        compiler_params=_cp(("arbitrary",)),
    )(w2, g2, m2, v2)
    return [o.reshape(shape) for o in outs]


_NAMES = ['ln_in_g', 'ln_in_b', 'w_in', 'b_forget', 'conv_w', 'conv_b', 'w_r', 'b_r', 'w_i', 'b_i', 'lru_lambda',
          'rel_bias', 'w_branch', 'w_gate', 'b_gate', 'w_out', 'ln1_g', 'ln1_b', 'w_ff1', 'w_ff2', 'ln2_g', 'ln2_b']
_BIG = {'w_in': 2, 'w_branch': 3, 'w_gate': 2, 'w_out': 1, 'w_ff1': 2, 'w_ff2': 1}
_SMALL_SHARDED = {'b_gate': 2, 'conv_w': 2, 'rel_bias': 2}
_SHARDED = {**_BIG, **_SMALL_SHARDED}
_REPLICATED = [n for n in _NAMES if n not in _SHARDED]
_TILE = 8 * LANES


def _tiles(a, cols):
    flat = a.reshape(-1)
    per = 8 * cols
    flat = jnp.pad(flat, (0, (-flat.shape[0]) % per))
    return flat.reshape(-1, cols)


def _pack(arrs, cols):
    return jnp.concatenate([_tiles(a, cols) for a in arrs], axis=0)


def _unpack(packed, like, cols):
    out, r0 = [], 0
    for a in like:
        n = math.prod(a.shape)
        rows = -(-n // (8 * cols)) * 8
        out.append(packed[r0:r0 + rows].reshape(-1)[:n].reshape(a.shape))
        r0 += rows
    return out


_EARLY = ['w_branch', 'w_gate', 'w_out', 'w_ff1', 'w_ff2']


def _chip_major_early(G, l):
    return [G[('w_branch', l)].reshape(N_CHIPS, N_BRANCH * BRANCH_WIDTH, BRANCH_WIDTH),
            G[('w_gate', l)].reshape(N_CHIPS, N_BRANCH * (D_MODEL // N_CHIPS), D_MODEL),
            G[('w_out', l)], G[('w_ff1', l)], G[('w_ff2', l)]]


def _chip_major_late(G, l):
    g_att, g_rec = G[('w_att', l)], G[('w_rec', l)]
    w_in = jnp.concatenate([g_att[:, :1536], g_rec[:, 1024:1024 + N_HEADS], g_rec[:, :1024], g_att[:, 1536:]], -1)
    per_chip = lambda g, rows: g.reshape(rows, N_CHIPS, -1).transpose(1, 0, 2)
    bg = per_chip(G[('b_gate', l)], N_BRANCH)
    cw = per_chip(G[('conv_w', l)], CONV_WIDTH)
    rb = per_chip(G[('rel_bias', l)], N_HEADS)
    small = jnp.stack([_pack([bg[j], cw[j], rb[j]], LANES) for j in range(N_CHIPS)])
    return [w_in.reshape(D_MODEL, N_CHIPS, D_IN // N_CHIPS).transpose(1, 0, 2), small]


class _Hook:
    def __init__(self, spec, done):
        self.spec, self.done = spec, done


def _unshard(blocks, axis):
    return jnp.concatenate([blocks[k] for k in range(N_CHIPS)], axis=axis)


def kernel(x, ln_in_g, ln_in_b, w_in, b_forget, conv_w, conv_b, w_r, b_r, w_i, b_i, lru_lambda, rel_bias, w_branch, w_gate, b_gate, w_out, ln1_g, ln1_b, w_ff1, w_ff2, ln2_g, ln2_b, loss_target, m_ln_in_g, m_ln_in_b, m_w_in, m_b_forget, m_conv_w, m_conv_b, m_w_r, m_b_r, m_w_i, m_b_i, m_lru_lambda, m_rel_bias, m_w_branch, m_w_gate, m_b_gate, m_w_out, m_ln1_g, m_ln1_b, m_w_ff1, m_w_ff2, m_ln2_g, m_ln2_b, v_ln_in_g, v_ln_in_b, v_w_in, v_b_forget, v_conv_w, v_conv_b, v_w_r, v_b_r, v_w_i, v_b_i, v_lru_lambda, v_rel_bias, v_w_branch, v_w_gate, v_b_gate, v_w_out, v_ln1_g, v_ln1_b, v_w_ff1, v_w_ff2, v_ln2_g, v_ln2_b):
    w = dict(zip(_NAMES, (ln_in_g, ln_in_b, w_in, b_forget, conv_w, conv_b, w_r, b_r, w_i, b_i, lru_lambda, rel_bias,
                          w_branch, w_gate, b_gate, w_out, ln1_g, ln1_b, w_ff1, w_ff2, ln2_g, ln2_b)))
    m = dict(zip(_NAMES, (m_ln_in_g, m_ln_in_b, m_w_in, m_b_forget, m_conv_w, m_conv_b, m_w_r, m_b_r, m_w_i, m_b_i,
                          m_lru_lambda, m_rel_bias, m_w_branch, m_w_gate, m_b_gate, m_w_out, m_ln1_g, m_ln1_b,
                          m_w_ff1, m_w_ff2, m_ln2_g, m_ln2_b)))
    v = dict(zip(_NAMES, (v_ln_in_g, v_ln_in_b, v_w_in, v_b_forget, v_conv_w, v_conv_b, v_w_r, v_b_r, v_w_i, v_b_i,
                          v_lru_lambda, v_rel_bias, v_w_branch, v_w_gate, v_b_gate, v_w_out, v_ln1_g, v_ln1_b,
                          v_w_ff1, v_w_ff2, v_ln2_g, v_ln2_b)))
    c = lax.axis_index("c")

    k = 2 * lax.axis_index("x") + lax.axis_index("y")
    state = {}

    small_like = [w[n] for n in _SMALL_SHARDED]
    small_pack = jnp.stack([_pack([a[l] for a in small_like], LANES) for l in range(DEPTH)])
    W = _prep_weights({n: w[n] for n in _REPLICATED})
    got_in, got_small = _exchange(_gather_spec([w['w_in'].astype(BF16), small_pack]), name='gather_first')
    small_blocks = [[_unpack(got_small[j, l], [a[l] for a in small_like], LANES) for l in range(DEPTH)]
                    for j in range(N_CHIPS)]
    first = {'w_in': _unshard(got_in, _BIG['w_in'])}
    for i, n in enumerate(_SMALL_SHARDED):
        first[n] = jnp.concatenate([jnp.stack([small_blocks[j][l][i] for l in range(DEPTH)])
                                    for j in range(N_CHIPS)], axis=_SMALL_SHARDED[n])
    _prep_weights(first, W)

    def gather_on(names):
        chip_major = ('w_ff1', 'w_ff2')
        return _Hook(lambda W_, G_: _gather_spec([w[n].astype(BF16) for n in names]),
                     lambda outs, W_, G_: _prep_weights(
                         {n: o if n in chip_major else _unshard(o, _BIG[n]) for n, o in zip(names, outs)}, W_))

    def pair_spec(W_, G_):
        state['early'] = [_chip_major_early(G_, l) for l in range(DEPTH)]
        return _pair_spec(*state['early'])

    def pair_done(outs, W_, G_):
        state['pair_sum'] = [_sum_pair(a0, a1, o, c, name='grad_pair_sum')
                             for a0, a1, o in zip(*state['early'], outs)]

    def late_spec(W_, G_):
        state['late'] = [_chip_major_late(G_, l) for l in range(DEPTH)]
        return _pair_spec(*state['late'])

    def late_done(outs, W_, G_):
        state['late_sum'] = [_sum_pair(a0, a1, o, c, name='grad_pair_sum') for a0, a1, o in zip(*state['late'], outs)]

    rep_main = _REPLICATED[2:]

    def rep_spec(W_, G_):
        dev = [jnp.stack([G_[(n, l)].reshape(w[n].shape[1:]) for l in range(DEPTH)]) for n in rep_main]
        packed = _pack(dev, LANES)
        return _gather8_spec(jnp.pad(packed, ((0, (-packed.shape[0]) % 256), (0, 0))))

    hooks = {('in_proj_att', 0): gather_on(['w_out']),
             ('fox_fwd', 0): gather_on(['w_gate']),
             ('sb_fwd', 0): gather_on(['w_ff1']),
             ('chunk_fwd', 0): gather_on(['w_branch']),
             ('gate_proj', 0): gather_on(['w_ff2']),
             ('chunk_bwd', 0): _Hook(rep_spec, lambda outs, W_, G_: state.update(rep_all=outs[0])),
             ('gate_dx', 0): _Hook(late_spec, late_done),
             ('in_att_dx', 0): _Hook(lambda W_, G_: _chip_spec(state['late_sum']),
                                     lambda outs, W_, G_: state.update(late_chips=outs)),
             ('fox_bwd', 0): _Hook(pair_spec, pair_done),
             ('sb_bwd', 0): _Hook(lambda W_, G_: _chip_spec(state['pair_sum']),
                                  lambda outs, W_, G_: state.update(from_chips=outs))}
    loss, gx, G = _device_step(x[0], loss_target[0], W, hooks)

    late_sum, late_chips = state['late_sum'], state['late_chips']
    pair_sum = [late_sum[0]] + state['pair_sum'] + [late_sum[1]]
    from_chips = [late_chips[0]] + state['from_chips'] + [late_chips[1]]
    mine = [_sum_chips(s, got, k, name='grad_chip_sum') for s, got in zip(pair_sum, from_chips)]
    theirs = _pair_swap(mine, name='grad_pair_swap')

    g_rep = dict(zip(rep_main, _unpack(_sum_slots(state['rep_all'], name='grad_sum8'), [w[n] for n in rep_main], LANES)))
    entry = _exchange(_gather8_spec(_pack([G[('ln_in_g', -1)][0], G[('ln_in_b', -1)][0]], LANES)),
                      name='grad_gather8')[0]
    g_rep.update(zip(_REPLICATED[:2], _unpack(_sum_slots(entry, name='grad_sum8'), [w[n] for n in _REPLICATED[:2]],
                                              LANES)))

    grads, delta, new_m, new_v = {}, {}, {}, {}
    for n, a, b in zip(_BIG, mine, theirs):
        grads[n], delta[n], new_m[n], new_v[n] = _adamw_layers(w[n], a, b, m[n], v[n], c, name='adamw')
    small_layers = [jnp.where(c == l, mine[-1], theirs[-1]) for l in range(DEPTH)]
    small_shards = [_unpack(s, [w[n][0] for n in _SMALL_SHARDED], LANES) for s in small_layers]
    g_shard = {n: jnp.stack([small_shards[l][i] for l in range(DEPTH)]) for i, n in enumerate(_SMALL_SHARDED)}
    small = _REPLICATED + list(_SMALL_SHARDED)
    for n in small:
        grads[n] = g_rep[n] if n in g_rep else g_shard[n]
    packs = [_pack([d[n] for n in small], LANES) for d in (w, grads, m, v)]
    outs = _adamw(*packs, name='adamw_small')
    small_like_all = [w[n] for n in small]
    for d, o in zip((delta, new_m, new_v), outs):
        d.update(zip(small, _unpack(o, small_like_all, LANES)))

    loss = lax.psum(loss, ("x", "y", "c"))
    return (loss, gx[None], *[grads[n] for n in _NAMES], *[delta[n] for n in _NAMES],
            *[new_m[n] for n in _NAMES], *[new_v[n] for n in _NAMES])
```
